```python
import math
import jax, jax.numpy as jnp
from jax import lax
import numpy as np

D_MODEL = 2048
BATCH = 8
SEQ = 2048
DEPTH = 2

EPS = 1e-6
NEG_INF = -1e30

SSM_WIDTH = D_MODEL // 2
SSM_GROUP = 16
SSM_GROUPS = SSM_WIDTH // SSM_GROUP
SSM_STATE = 64
DT_MIN = 1e-3
DT_MAX = 1e-1

SG_WIDTH = D_MODEL // 2
SG_HEADS = 8
SG_HEAD_DIM = SG_WIDTH // SG_HEADS
SG_CHUNK = 128

HEAD_DIM = 64
ATT_HEADS = D_MODEL // 128
ATT_KV_HEADS = ATT_HEADS // 8
GQA_GROUP = ATT_HEADS // ATT_KV_HEADS
ATT_WIDTH = ATT_HEADS * HEAD_DIM
KV_WIDTH = ATT_KV_HEADS * HEAD_DIM
WINDOW = 128
ATT_BLOCK = 128
ROT_DIM = HEAD_DIM // 4
ROPE_THETA = 500000.0

N_BRANCH = 3
IN_SIZES = (SSM_WIDTH, SSM_WIDTH, SG_WIDTH, SG_WIDTH, SG_WIDTH,
            ATT_WIDTH, KV_WIDTH, KV_WIDTH, ATT_WIDTH, N_BRANCH * D_MODEL)
D_IN = sum(IN_SIZES)
IN_OFFSETS = tuple(sum(IN_SIZES[:i + 1]) for i in range(len(IN_SIZES) - 1))

kernel_name = "hybrid_s5_gmlp_swa_gated_block"


def rmsnorm(x, w):
    xf = x.astype(jnp.float32)
    y = xf * lax.rsqrt(jnp.mean(xf * xf, axis=-1, keepdims=True) + EPS)
    return (y * w.astype(jnp.float32)).astype(x.dtype)


def layernorm(x, w, b):
    xf = x.astype(jnp.float32)
    mu = jnp.mean(xf, axis=-1, keepdims=True)
    var = jnp.mean(jnp.square(xf - mu), axis=-1, keepdims=True)
    y = (xf - mu) * lax.rsqrt(var + EPS)
    return (y * w.astype(jnp.float32) + b.astype(jnp.float32)).astype(x.dtype)


def partial_rope(t, pos):
    tf = t.astype(jnp.float32)
    half = ROT_DIM // 2
    inv_freq = ROPE_THETA ** (-jnp.arange(0, ROT_DIM, 2, dtype=jnp.float32) / ROT_DIM)
    ang = pos.astype(jnp.float32)[:, None] * inv_freq[None, :]
    cos = jnp.cos(ang)[None, :, None, :]
    sin = jnp.sin(ang)[None, :, None, :]
    t1 = tf[..., :half]
    t2 = tf[..., half:ROT_DIM]
    rot = jnp.concatenate([t1 * cos - t2 * sin, t2 * cos + t1 * sin, tf[..., ROT_DIM:]], axis=-1)
    return rot.astype(t.dtype)


def s5_mixer(u, a_re, a_im, log_dt, b_re, b_im, c_re, c_im, d, glu_w, glu_b):
    bsz, L, _ = u.shape
    uf = u.astype(jnp.float32).reshape(bsz, L, SSM_GROUPS, SSM_GROUP)
    lam = lax.complex(a_re.astype(jnp.float32), a_im.astype(jnp.float32))
    dt = jnp.exp(log_dt.astype(jnp.float32))[:, None]
    lam_bar = jnp.exp(lam * dt)
    b = lax.complex(b_re.astype(jnp.float32), b_im.astype(jnp.float32))
    b_bar = ((lam_bar - 1.0) / lam)[..., None] * b
    bu = jnp.einsum('blgc,gpc->blgp', uf.astype(jnp.complex64), b_bar)
    a = jnp.broadcast_to(lam_bar, bu.shape)

    def combine(e1, e2):
        a1, b1 = e1
        a2, b2 = e2
        return a1 * a2, a2 * b1 + b2

    _, states = lax.associative_scan(combine, (a, bu), axis=1)
    c = lax.complex(c_re.astype(jnp.float32), c_im.astype(jnp.float32))
    y = jnp.real(jnp.einsum('blgp,gcp->blgc', states, c))
    y = y + d.astype(jnp.float32).reshape(SSM_GROUPS, SSM_GROUP) * uf
    y = jax.nn.gelu(y.reshape(bsz, L, SSM_WIDTH)).astype(u.dtype)
    return y * jax.nn.sigmoid(y @ glu_w + glu_b)


def spatial_gating(u, v, ln_w, ln_b, w_s, b_s):
    bsz, L, _ = u.shape
    n = L // SG_CHUNK
    v = layernorm(v, ln_w, ln_b)
    vc = v.reshape(bsz, n, SG_CHUNK, SG_HEADS, SG_HEAD_DIM)
    causal = jnp.tril(jnp.ones((SG_CHUNK, SG_CHUNK), dtype=bool))
    w = jnp.where(causal[None], w_s, jnp.zeros_like(w_s))
    mixed = jnp.einsum('hts,bnshc->bnthc', w, vc) + b_s.T[:, :, None]
    return u * mixed.reshape(bsz, L, SG_WIDTH)


def sliding_window_attention(q, k, v, sinks):
    bsz, L, _ = q.shape
    n = L // ATT_BLOCK
    pos = jnp.arange(L)
    q = partial_rope(q.reshape(bsz, L, ATT_HEADS, HEAD_DIM), pos)
    k = partial_rope(k.reshape(bsz, L, ATT_KV_HEADS, HEAD_DIM), pos)
    v = v.reshape(bsz, L, ATT_KV_HEADS, HEAD_DIM)

    def banded(t):
        tp = jnp.pad(t, ((0, 0), (ATT_BLOCK, 0), (0, 0), (0, 0)))
        prev = tp[:, :L].reshape(bsz, n, ATT_BLOCK, ATT_KV_HEADS, HEAD_DIM)
        cur = t.reshape(bsz, n, ATT_BLOCK, ATT_KV_HEADS, HEAD_DIM)
        return jnp.concatenate([prev, cur], axis=2)

    kb = banded(k)
    vb = banded(v)
    qb = q.reshape(bsz, n, ATT_BLOCK, ATT_KV_HEADS, GQA_GROUP, HEAD_DIM)
    s = jnp.einsum('bnqkgd,bnskd->bnkgqs', qb, kb).astype(jnp.float32) * (HEAD_DIM ** -0.5)
    blk = jnp.arange(n)[:, None, None]
    qpos = blk * ATT_BLOCK + jnp.arange(ATT_BLOCK)[None, :, None]
    kpos = (blk - 1) * ATT_BLOCK + jnp.arange(2 * ATT_BLOCK)[None, None, :]
    diff = qpos - kpos
    allowed = (diff >= 0) & (diff < WINDOW) & (kpos >= 0)
    s = jnp.where(allowed[None, :, None, None], s, NEG_INF)
    sink = sinks.astype(jnp.float32).reshape(ATT_KV_HEADS, GQA_GROUP)[None, None, :, :, None, None]
    sink = jnp.broadcast_to(sink, s.shape[:-1] + (1,))
    p = jax.nn.softmax(jnp.concatenate([s, sink], axis=-1), axis=-1)[..., :-1]
    o = jnp.einsum('bnkgqs,bnskd->bnqkgd', p.astype(vb.dtype), vb)
    return o.reshape(bsz, L, ATT_WIDTH)


def _fwd_setup_inputs(seed: int = 0) -> dict:
    key = jax.random.key(seed)
    ks = jax.random.split(key, 24)
    f32 = jnp.float32
    nrm = lambda k, shape, scale: jax.random.normal(k, shape, f32) * scale
    G, P, C = SSM_GROUPS, SSM_STATE, SSM_GROUP
    x = jax.random.normal(ks[0], (BATCH, SEQ, D_MODEL), f32)
    norm_w = 1.0 + nrm(ks[1], (DEPTH, D_MODEL), 0.02)
    w_in = nrm(ks[2], (DEPTH, D_MODEL, D_IN), D_MODEL ** -0.5)
    ssm_a_re = -0.5 + nrm(ks[3], (DEPTH, G, P), 0.01)
    ssm_a_im = math.pi * jnp.arange(P, dtype=f32)[None, None, :] + nrm(ks[4], (DEPTH, G, P), 0.01)
    ssm_log_dt = jax.random.uniform(ks[5], (DEPTH, G), f32, math.log(DT_MIN), math.log(DT_MAX))
    ssm_b_re = nrm(ks[6], (DEPTH, G, P, C), C ** -0.5)
    ssm_b_im = nrm(ks[7], (DEPTH, G, P, C), C ** -0.5)
    ssm_c_re = nrm(ks[8], (DEPTH, G, C, P), P ** -0.5)
    ssm_c_im = nrm(ks[9], (DEPTH, G, C, P), P ** -0.5)
    ssm_d = nrm(ks[10], (DEPTH, SSM_WIDTH), 1.0)
    ssm_glu_w = nrm(ks[11], (DEPTH, SSM_WIDTH, SSM_WIDTH), SSM_WIDTH ** -0.5)
    ssm_glu_b = nrm(ks[12], (DEPTH, SSM_WIDTH), 0.01)
    sg_ln_w = 1.0 + nrm(ks[13], (DEPTH, SG_WIDTH), 0.02)
    sg_ln_b = nrm(ks[14], (DEPTH, SG_WIDTH), 0.02)
    sg_w = nrm(ks[15], (DEPTH, SG_HEADS, SG_CHUNK, SG_CHUNK), SG_CHUNK ** -0.5)
    sg_b = 1.0 + nrm(ks[16], (DEPTH, SG_HEADS, SG_CHUNK), 0.02)
    attn_sinks = nrm(ks[17], (DEPTH, ATT_HEADS), 1.0)
    w_branch_a = nrm(ks[18], (DEPTH, SSM_WIDTH, D_MODEL), SSM_WIDTH ** -0.5)
    w_branch_b = nrm(ks[19], (DEPTH, SG_WIDTH, D_MODEL), SG_WIDTH ** -0.5)
    w_branch_c = nrm(ks[20], (DEPTH, ATT_WIDTH, D_MODEL), ATT_WIDTH ** -0.5)
    w_out = nrm(ks[21], (DEPTH, D_MODEL, D_MODEL), D_MODEL ** -0.5)
    final_norm_w = 1.0 + nrm(ks[22], (D_MODEL,), 0.02)
    return {"x": x, "norm_w": norm_w, "w_in": w_in,
            "ssm_a_re": ssm_a_re, "ssm_a_im": ssm_a_im, "ssm_log_dt": ssm_log_dt,
            "ssm_b_re": ssm_b_re, "ssm_b_im": ssm_b_im, "ssm_c_re": ssm_c_re, "ssm_c_im": ssm_c_im,
            "ssm_d": ssm_d, "ssm_glu_w": ssm_glu_w, "ssm_glu_b": ssm_glu_b,
            "sg_ln_w": sg_ln_w, "sg_ln_b": sg_ln_b, "sg_w": sg_w, "sg_b": sg_b,
            "attn_sinks": attn_sinks,
            "w_branch_a": w_branch_a, "w_branch_b": w_branch_b, "w_branch_c": w_branch_c,
            "w_out": w_out, "final_norm_w": final_norm_w}


def _fwd_reference(x, norm_w, w_in, ssm_a_re, ssm_a_im, ssm_log_dt, ssm_b_re, ssm_b_im,
              ssm_c_re, ssm_c_im, ssm_d, ssm_glu_w, ssm_glu_b, sg_ln_w, sg_ln_b, sg_w, sg_b,
              attn_sinks, w_branch_a, w_branch_b, w_branch_c, w_out, final_norm_w):
    bsz, L, _ = x.shape
    for l in range(DEPTH):
        h = rmsnorm(x, norm_w[l])
        proj = h @ w_in[l]
        u_a, z_a, u_b, v_b, z_b, q, k, v, z_c, gates = jnp.split(proj, IN_OFFSETS, axis=-1)
        y_a = s5_mixer(u_a, ssm_a_re[l], ssm_a_im[l], ssm_log_dt[l], ssm_b_re[l], ssm_b_im[l],
                       ssm_c_re[l], ssm_c_im[l], ssm_d[l], ssm_glu_w[l], ssm_glu_b[l]) * jax.nn.silu(z_a)
        y_b = spatial_gating(jax.nn.gelu(u_b), jax.nn.gelu(v_b), sg_ln_w[l], sg_ln_b[l],
                             sg_w[l], sg_b[l]) * jax.nn.silu(z_b)
        y_c = sliding_window_attention(q, k, v, attn_sinks[l]) * jax.nn.silu(z_c)
        g = jax.nn.sigmoid(gates.reshape(bsz, L, N_BRANCH, D_MODEL))
        merged = (g[:, :, 0] * (y_a @ w_branch_a[l])
                  + g[:, :, 1] * (y_b @ w_branch_b[l])
                  + g[:, :, 2] * (y_c @ w_branch_c[l]))
        x = x + merged @ w_out[l]
    return rmsnorm(x, final_norm_w)


import jax as _jax
import jax.numpy as _jnp

TWIN_FORMAT = 'train_step'
FWD_PARAMS = ['x', 'norm_w', 'w_in', 'ssm_a_re', 'ssm_a_im', 'ssm_log_dt', 'ssm_b_re', 'ssm_b_im', 'ssm_c_re', 'ssm_c_im', 'ssm_d', 'ssm_glu_w', 'ssm_glu_b', 'sg_ln_w', 'sg_ln_b', 'sg_w', 'sg_b', 'attn_sinks', 'w_branch_a', 'w_branch_b', 'w_branch_c', 'w_out', 'final_norm_w']
TWIN_WEIGHTS = ['norm_w', 'w_in', 'ssm_a_re', 'ssm_a_im', 'ssm_log_dt', 'ssm_b_re', 'ssm_b_im', 'ssm_c_re', 'ssm_c_im', 'ssm_d', 'ssm_glu_w', 'ssm_glu_b', 'sg_ln_w', 'sg_ln_b', 'sg_w', 'sg_b', 'attn_sinks', 'w_branch_a', 'w_branch_b', 'w_branch_c', 'w_out', 'final_norm_w']
TWIN_DIFF_INPUT = 'x'
TWIN_INPUTS = ['x', 'norm_w', 'w_in', 'ssm_a_re', 'ssm_a_im', 'ssm_log_dt', 'ssm_b_re', 'ssm_b_im', 'ssm_c_re', 'ssm_c_im', 'ssm_d', 'ssm_glu_w', 'ssm_glu_b', 'sg_ln_w', 'sg_ln_b', 'sg_w', 'sg_b', 'attn_sinks', 'w_branch_a', 'w_branch_b', 'w_branch_c', 'w_out', 'final_norm_w', 'loss_target', 'm_norm_w', 'm_w_in', 'm_ssm_a_re', 'm_ssm_a_im', 'm_ssm_log_dt', 'm_ssm_b_re', 'm_ssm_b_im', 'm_ssm_c_re', 'm_ssm_c_im', 'm_ssm_d', 'm_ssm_glu_w', 'm_ssm_glu_b', 'm_sg_ln_w', 'm_sg_ln_b', 'm_sg_w', 'm_sg_b', 'm_attn_sinks', 'm_w_branch_a', 'm_w_branch_b', 'm_w_branch_c', 'm_w_out', 'm_final_norm_w', 'v_norm_w', 'v_w_in', 'v_ssm_a_re', 'v_ssm_a_im', 'v_ssm_log_dt', 'v_ssm_b_re', 'v_ssm_b_im', 'v_ssm_c_re', 'v_ssm_c_im', 'v_ssm_d', 'v_ssm_glu_w', 'v_ssm_glu_b', 'v_sg_ln_w', 'v_sg_ln_b', 'v_sg_w', 'v_sg_b', 'v_attn_sinks', 'v_w_branch_a', 'v_w_branch_b', 'v_w_branch_c', 'v_w_out', 'v_final_norm_w']
TWIN_OUTPUTS = ['loss', 'grad_x', 'grad_norm_w', 'grad_w_in', 'grad_ssm_a_re', 'grad_ssm_a_im', 'grad_ssm_log_dt', 'grad_ssm_b_re', 'grad_ssm_b_im', 'grad_ssm_c_re', 'grad_ssm_c_im', 'grad_ssm_d', 'grad_ssm_glu_w', 'grad_ssm_glu_b', 'grad_sg_ln_w', 'grad_sg_ln_b', 'grad_sg_w', 'grad_sg_b', 'grad_attn_sinks', 'grad_w_branch_a', 'grad_w_branch_b', 'grad_w_branch_c', 'grad_w_out', 'grad_final_norm_w', 'delta_norm_w', 'delta_w_in', 'delta_ssm_a_re', 'delta_ssm_a_im', 'delta_ssm_log_dt', 'delta_ssm_b_re', 'delta_ssm_b_im', 'delta_ssm_c_re', 'delta_ssm_c_im', 'delta_ssm_d', 'delta_ssm_glu_w', 'delta_ssm_glu_b', 'delta_sg_ln_w', 'delta_sg_ln_b', 'delta_sg_w', 'delta_sg_b', 'delta_attn_sinks', 'delta_w_branch_a', 'delta_w_branch_b', 'delta_w_branch_c', 'delta_w_out', 'delta_final_norm_w', 'new_m_norm_w', 'new_m_w_in', 'new_m_ssm_a_re', 'new_m_ssm_a_im', 'new_m_ssm_log_dt', 'new_m_ssm_b_re', 'new_m_ssm_b_im', 'new_m_ssm_c_re', 'new_m_ssm_c_im', 'new_m_ssm_d', 'new_m_ssm_glu_w', 'new_m_ssm_glu_b', 'new_m_sg_ln_w', 'new_m_sg_ln_b', 'new_m_sg_w', 'new_m_sg_b', 'new_m_attn_sinks', 'new_m_w_branch_a', 'new_m_w_branch_b', 'new_m_w_branch_c', 'new_m_w_out', 'new_m_final_norm_w', 'new_v_norm_w', 'new_v_w_in', 'new_v_ssm_a_re', 'new_v_ssm_a_im', 'new_v_ssm_log_dt', 'new_v_ssm_b_re', 'new_v_ssm_b_im', 'new_v_ssm_c_re', 'new_v_ssm_c_im', 'new_v_ssm_d', 'new_v_ssm_glu_w', 'new_v_ssm_glu_b', 'new_v_sg_ln_w', 'new_v_sg_ln_b', 'new_v_sg_w', 'new_v_sg_b', 'new_v_attn_sinks', 'new_v_w_branch_a', 'new_v_w_branch_b', 'new_v_w_branch_c', 'new_v_w_out', 'new_v_final_norm_w']
TWIN_LEAF_KINDS = {'loss': 'loss', 'grad_x': 'grad_x', 'grad_norm_w': 'grad_w', 'grad_w_in': 'grad_w', 'grad_ssm_a_re': 'grad_w', 'grad_ssm_a_im': 'grad_w', 'grad_ssm_log_dt': 'grad_w', 'grad_ssm_b_re': 'grad_w', 'grad_ssm_b_im': 'grad_w', 'grad_ssm_c_re': 'grad_w', 'grad_ssm_c_im': 'grad_w', 'grad_ssm_d': 'grad_w', 'grad_ssm_glu_w': 'grad_w', 'grad_ssm_glu_b': 'grad_w', 'grad_sg_ln_w': 'grad_w', 'grad_sg_ln_b': 'grad_w', 'grad_sg_w': 'grad_w', 'grad_sg_b': 'grad_w', 'grad_attn_sinks': 'grad_w', 'grad_w_branch_a': 'grad_w', 'grad_w_branch_b': 'grad_w', 'grad_w_branch_c': 'grad_w', 'grad_w_out': 'grad_w', 'grad_final_norm_w': 'grad_w', 'delta_norm_w': 'delta_w', 'delta_w_in': 'delta_w', 'delta_ssm_a_re': 'delta_w', 'delta_ssm_a_im': 'delta_w', 'delta_ssm_log_dt': 'delta_w', 'delta_ssm_b_re': 'delta_w', 'delta_ssm_b_im': 'delta_w', 'delta_ssm_c_re': 'delta_w', 'delta_ssm_c_im': 'delta_w', 'delta_ssm_d': 'delta_w', 'delta_ssm_glu_w': 'delta_w', 'delta_ssm_glu_b': 'delta_w', 'delta_sg_ln_w': 'delta_w', 'delta_sg_ln_b': 'delta_w', 'delta_sg_w': 'delta_w', 'delta_sg_b': 'delta_w', 'delta_attn_sinks': 'delta_w', 'delta_w_branch_a': 'delta_w', 'delta_w_branch_b': 'delta_w', 'delta_w_branch_c': 'delta_w', 'delta_w_out': 'delta_w', 'delta_final_norm_w': 'delta_w', 'new_m_norm_w': 'new_m', 'new_m_w_in': 'new_m', 'new_m_ssm_a_re': 'new_m', 'new_m_ssm_a_im': 'new_m', 'new_m_ssm_log_dt': 'new_m', 'new_m_ssm_b_re': 'new_m', 'new_m_ssm_b_im': 'new_m', 'new_m_ssm_c_re': 'new_m', 'new_m_ssm_c_im': 'new_m', 'new_m_ssm_d': 'new_m', 'new_m_ssm_glu_w': 'new_m', 'new_m_ssm_glu_b': 'new_m', 'new_m_sg_ln_w': 'new_m', 'new_m_sg_ln_b': 'new_m', 'new_m_sg_w': 'new_m', 'new_m_sg_b': 'new_m', 'new_m_attn_sinks': 'new_m', 'new_m_w_branch_a': 'new_m', 'new_m_w_branch_b': 'new_m', 'new_m_w_branch_c': 'new_m', 'new_m_w_out': 'new_m', 'new_m_final_norm_w': 'new_m', 'new_v_norm_w': 'new_v', 'new_v_w_in': 'new_v', 'new_v_ssm_a_re': 'new_v', 'new_v_ssm_a_im': 'new_v', 'new_v_ssm_log_dt': 'new_v', 'new_v_ssm_b_re': 'new_v', 'new_v_ssm_b_im': 'new_v', 'new_v_ssm_c_re': 'new_v', 'new_v_ssm_c_im': 'new_v', 'new_v_ssm_d': 'new_v', 'new_v_ssm_glu_w': 'new_v', 'new_v_ssm_glu_b': 'new_v', 'new_v_sg_ln_w': 'new_v', 'new_v_sg_ln_b': 'new_v', 'new_v_sg_w': 'new_v', 'new_v_sg_b': 'new_v', 'new_v_attn_sinks': 'new_v', 'new_v_w_branch_a': 'new_v', 'new_v_w_branch_b': 'new_v', 'new_v_w_branch_c': 'new_v', 'new_v_w_out': 'new_v', 'new_v_final_norm_w': 'new_v'}


def _forward(args):
    return _fwd_reference(*[args[k] for k in FWD_PARAMS])


def _output_shape():
    out = _jax.eval_shape(lambda: _forward(_fwd_setup_inputs(0)))
    return out.shape, out.dtype

N_MICROBATCH = 1
ADAM_LR = 0.001
ADAM_B1 = 0.9
ADAM_B2 = 0.999
ADAM_EPS = 1e-08
ADAM_WD = 0.01
ADAM_STEP = 10
PER_EXAMPLE_BATCH_AXIS = {'x': 0, 'loss_target': 0}
SHARED_INPUTS = []
_WEIGHT_DTYPES = {'norm_w': _jnp.float32, 'w_in': _jnp.float32, 'ssm_a_re': _jnp.float32, 'ssm_a_im': _jnp.float32, 'ssm_log_dt': _jnp.float32, 'ssm_b_re': _jnp.float32, 'ssm_b_im': _jnp.float32, 'ssm_c_re': _jnp.float32, 'ssm_c_im': _jnp.float32, 'ssm_d': _jnp.float32, 'ssm_glu_w': _jnp.float32, 'ssm_glu_b': _jnp.float32, 'sg_ln_w': _jnp.float32, 'sg_ln_b': _jnp.float32, 'sg_w': _jnp.float32, 'sg_b': _jnp.float32, 'attn_sinks': _jnp.float32, 'w_branch_a': _jnp.float32, 'w_branch_b': _jnp.float32, 'w_branch_c': _jnp.float32, 'w_out': _jnp.float32, 'final_norm_w': _jnp.float32}
MOMENT_SCALE = {'norm_w': 2.888048e-02, 'w_in': 1.130601e-02, 'ssm_a_re': 9.747158e-04, 'ssm_a_im': 9.358262e-04, 'ssm_log_dt': 7.841482e-01, 'ssm_b_re': 4.767766e-04, 'ssm_b_im': 4.770842e-04, 'ssm_c_re': 9.507635e-04, 'ssm_c_im': 9.438794e-04, 'ssm_d': 1.044115e-02, 'ssm_glu_w': 2.903141e-03, 'ssm_glu_b': 4.190917e-03, 'sg_ln_w': 1.279126e-02, 'sg_ln_b': 1.262823e-02, 'sg_w': 1.284245e-02, 'sg_b': 1.853532e-02, 'attn_sinks': 5.134384e-03, 'w_branch_a': 6.919063e-03, 'w_branch_b': 1.566312e-02, 'w_branch_c': 4.018165e-03, 'w_out': 1.760810e-02, 'final_norm_w': 7.996441e+00}


def _to_microbatches(a, axis):
    t = _jnp.moveaxis(a, axis, 0)
    t = t.reshape((N_MICROBATCH, t.shape[0] // N_MICROBATCH) + t.shape[1:])
    return _jnp.moveaxis(t, 1, axis + 1)


def setup_inputs(seed: int = 0) -> dict:
    inp = _fwd_setup_inputs(seed)
    key = _jax.random.fold_in(_jax.random.key(seed), 7919)
    shape, _ = _output_shape()
    out = dict(inp)
    out["loss_target"] = _jax.random.normal(_jax.random.fold_in(key, 0), shape, _jnp.float32)
    for i, name in enumerate(TWIN_WEIGHTS):
        w = inp[name].astype(_jnp.float32)
        if MOMENT_SCALE is None:
            s = _jnp.sqrt(_jnp.mean(_jnp.square(w)) + 1e-30)
        else:
            s = MOMENT_SCALE[name]
        km, kv = _jax.random.split(_jax.random.fold_in(key, i + 1))
        out[name] = w
        out["m_" + name] = s * _jax.random.normal(km, w.shape, _jnp.float32)
        out["v_" + name] = (s * s) * _jax.random.uniform(kv, w.shape, _jnp.float32, 0.5, 1.5)
    if N_MICROBATCH > 1:
        for name, axis in PER_EXAMPLE_BATCH_AXIS.items():
            out[name] = _to_microbatches(out[name], axis)
    return {'x': out['x'], 'norm_w': out['norm_w'], 'w_in': out['w_in'], 'ssm_a_re': out['ssm_a_re'], 'ssm_a_im': out['ssm_a_im'], 'ssm_log_dt': out['ssm_log_dt'], 'ssm_b_re': out['ssm_b_re'], 'ssm_b_im': out['ssm_b_im'], 'ssm_c_re': out['ssm_c_re'], 'ssm_c_im': out['ssm_c_im'], 'ssm_d': out['ssm_d'], 'ssm_glu_w': out['ssm_glu_w'], 'ssm_glu_b': out['ssm_glu_b'], 'sg_ln_w': out['sg_ln_w'], 'sg_ln_b': out['sg_ln_b'], 'sg_w': out['sg_w'], 'sg_b': out['sg_b'], 'attn_sinks': out['attn_sinks'], 'w_branch_a': out['w_branch_a'], 'w_branch_b': out['w_branch_b'], 'w_branch_c': out['w_branch_c'], 'w_out': out['w_out'], 'final_norm_w': out['final_norm_w'], 'loss_target': out['loss_target'], 'm_norm_w': out['m_norm_w'], 'm_w_in': out['m_w_in'], 'm_ssm_a_re': out['m_ssm_a_re'], 'm_ssm_a_im': out['m_ssm_a_im'], 'm_ssm_log_dt': out['m_ssm_log_dt'], 'm_ssm_b_re': out['m_ssm_b_re'], 'm_ssm_b_im': out['m_ssm_b_im'], 'm_ssm_c_re': out['m_ssm_c_re'], 'm_ssm_c_im': out['m_ssm_c_im'], 'm_ssm_d': out['m_ssm_d'], 'm_ssm_glu_w': out['m_ssm_glu_w'], 'm_ssm_glu_b': out['m_ssm_glu_b'], 'm_sg_ln_w': out['m_sg_ln_w'], 'm_sg_ln_b': out['m_sg_ln_b'], 'm_sg_w': out['m_sg_w'], 'm_sg_b': out['m_sg_b'], 'm_attn_sinks': out['m_attn_sinks'], 'm_w_branch_a': out['m_w_branch_a'], 'm_w_branch_b': out['m_w_branch_b'], 'm_w_branch_c': out['m_w_branch_c'], 'm_w_out': out['m_w_out'], 'm_final_norm_w': out['m_final_norm_w'], 'v_norm_w': out['v_norm_w'], 'v_w_in': out['v_w_in'], 'v_ssm_a_re': out['v_ssm_a_re'], 'v_ssm_a_im': out['v_ssm_a_im'], 'v_ssm_log_dt': out['v_ssm_log_dt'], 'v_ssm_b_re': out['v_ssm_b_re'], 'v_ssm_b_im': out['v_ssm_b_im'], 'v_ssm_c_re': out['v_ssm_c_re'], 'v_ssm_c_im': out['v_ssm_c_im'], 'v_ssm_d': out['v_ssm_d'], 'v_ssm_glu_w': out['v_ssm_glu_w'], 'v_ssm_glu_b': out['v_ssm_glu_b'], 'v_sg_ln_w': out['v_sg_ln_w'], 'v_sg_ln_b': out['v_sg_ln_b'], 'v_sg_w': out['v_sg_w'], 'v_sg_b': out['v_sg_b'], 'v_attn_sinks': out['v_attn_sinks'], 'v_w_branch_a': out['v_w_branch_a'], 'v_w_branch_b': out['v_w_branch_b'], 'v_w_branch_c': out['v_w_branch_c'], 'v_w_out': out['v_w_out'], 'v_final_norm_w': out['v_final_norm_w']}


def _loss(weights, diff, rest, loss_target):
    with _jax.named_scope("forward"):
        args = {**rest, TWIN_DIFF_INPUT: diff, **{k: w.astype(_WEIGHT_DTYPES[k]) for k, w in weights.items()}}
        y = _forward(args)
    with _jax.named_scope("loss_head"):
        err = _jnp.square(y.astype(_jnp.float32) - loss_target)
        return 0.5 * _jnp.sum(_jnp.mean(err, axis=-1)) if err.ndim else 0.5 * err


def _adamw(w, g, m, v):
    m = ADAM_B1 * m + (1.0 - ADAM_B1) * g
    v = ADAM_B2 * v + (1.0 - ADAM_B2) * _jnp.square(g)
    m_hat = m / (1.0 - ADAM_B1 ** ADAM_STEP)
    v_hat = v / (1.0 - ADAM_B2 ** ADAM_STEP)
    delta = -ADAM_LR * (m_hat / (_jnp.sqrt(v_hat) + ADAM_EPS) + ADAM_WD * w)
    return delta, m, v


def reference(x, norm_w, w_in, ssm_a_re, ssm_a_im, ssm_log_dt, ssm_b_re, ssm_b_im, ssm_c_re, ssm_c_im, ssm_d, ssm_glu_w, ssm_glu_b, sg_ln_w, sg_ln_b, sg_w, sg_b, attn_sinks, w_branch_a, w_branch_b, w_branch_c, w_out, final_norm_w, loss_target, m_norm_w, m_w_in, m_ssm_a_re, m_ssm_a_im, m_ssm_log_dt, m_ssm_b_re, m_ssm_b_im, m_ssm_c_re, m_ssm_c_im, m_ssm_d, m_ssm_glu_w, m_ssm_glu_b, m_sg_ln_w, m_sg_ln_b, m_sg_w, m_sg_b, m_attn_sinks, m_w_branch_a, m_w_branch_b, m_w_branch_c, m_w_out, m_final_norm_w, v_norm_w, v_w_in, v_ssm_a_re, v_ssm_a_im, v_ssm_log_dt, v_ssm_b_re, v_ssm_b_im, v_ssm_c_re, v_ssm_c_im, v_ssm_d, v_ssm_glu_w, v_ssm_glu_b, v_sg_ln_w, v_sg_ln_b, v_sg_w, v_sg_b, v_attn_sinks, v_w_branch_a, v_w_branch_b, v_w_branch_c, v_w_out, v_final_norm_w):
    given = dict(x=x, norm_w=norm_w, w_in=w_in, ssm_a_re=ssm_a_re, ssm_a_im=ssm_a_im, ssm_log_dt=ssm_log_dt, ssm_b_re=ssm_b_re, ssm_b_im=ssm_b_im, ssm_c_re=ssm_c_re, ssm_c_im=ssm_c_im, ssm_d=ssm_d, ssm_glu_w=ssm_glu_w, ssm_glu_b=ssm_glu_b, sg_ln_w=sg_ln_w, sg_ln_b=sg_ln_b, sg_w=sg_w, sg_b=sg_b, attn_sinks=attn_sinks, w_branch_a=w_branch_a, w_branch_b=w_branch_b, w_branch_c=w_branch_c, w_out=w_out, final_norm_w=final_norm_w, loss_target=loss_target, m_norm_w=m_norm_w, m_w_in=m_w_in, m_ssm_a_re=m_ssm_a_re, m_ssm_a_im=m_ssm_a_im, m_ssm_log_dt=m_ssm_log_dt, m_ssm_b_re=m_ssm_b_re, m_ssm_b_im=m_ssm_b_im, m_ssm_c_re=m_ssm_c_re, m_ssm_c_im=m_ssm_c_im, m_ssm_d=m_ssm_d, m_ssm_glu_w=m_ssm_glu_w, m_ssm_glu_b=m_ssm_glu_b, m_sg_ln_w=m_sg_ln_w, m_sg_ln_b=m_sg_ln_b, m_sg_w=m_sg_w, m_sg_b=m_sg_b, m_attn_sinks=m_attn_sinks, m_w_branch_a=m_w_branch_a, m_w_branch_b=m_w_branch_b, m_w_branch_c=m_w_branch_c, m_w_out=m_w_out, m_final_norm_w=m_final_norm_w, v_norm_w=v_norm_w, v_w_in=v_w_in, v_ssm_a_re=v_ssm_a_re, v_ssm_a_im=v_ssm_a_im, v_ssm_log_dt=v_ssm_log_dt, v_ssm_b_re=v_ssm_b_re, v_ssm_b_im=v_ssm_b_im, v_ssm_c_re=v_ssm_c_re, v_ssm_c_im=v_ssm_c_im, v_ssm_d=v_ssm_d, v_ssm_glu_w=v_ssm_glu_w, v_ssm_glu_b=v_ssm_glu_b, v_sg_ln_w=v_sg_ln_w, v_sg_ln_b=v_sg_ln_b, v_sg_w=v_sg_w, v_sg_b=v_sg_b, v_attn_sinks=v_attn_sinks, v_w_branch_a=v_w_branch_a, v_w_branch_b=v_w_branch_b, v_w_branch_c=v_w_branch_c, v_w_out=v_w_out, v_final_norm_w=v_final_norm_w)
    weights = {n: given[n] for n in TWIN_WEIGHTS}
    shared = {n: given[n] for n in SHARED_INPUTS}
    per_example = {n: given[n] for n in ['x']}
    grad_fn = _jax.value_and_grad(_loss, argnums=(0, 1))

    def one_microbatch(ex, loss_target):
        ex = dict(ex)
        diff = ex.pop(TWIN_DIFF_INPUT)
        return grad_fn(weights, diff, {**shared, **ex}, loss_target)

    if N_MICROBATCH == 1:
        loss, (grad_w, grad_x) = one_microbatch(per_example, given["loss_target"])
    else:
        def body(carry, xs):
            loss_sum, grad_sum = carry
            l_k, (gw_k, gx_k) = one_microbatch(xs[0], xs[1])
            with _jax.named_scope("update"):
                return (loss_sum + l_k, _jax.tree.map(_jnp.add, grad_sum, gw_k)), gx_k

        init = (_jnp.zeros((), _jnp.float32), _jax.tree.map(_jnp.zeros_like, weights))
        (loss, grad_w), grad_x = _jax.lax.scan(body, init, (per_example, given["loss_target"]))
    with _jax.named_scope("update"):
        delta_w, new_m, new_v = {}, {}, {}
        for n in TWIN_WEIGHTS:
            delta_w[n], new_m[n], new_v[n] = _adamw(weights[n], grad_w[n], given["m_" + n], given["v_" + n])
    return (loss, grad_x, *[grad_w[n] for n in TWIN_WEIGHTS], *[delta_w[n] for n in TWIN_WEIGHTS],
            *[new_m[n] for n in TWIN_WEIGHTS], *[new_v[n] for n in TWIN_WEIGHTS])
```

```python
import functools
import math

import numpy as np
import jax
import jax.numpy as jnp
from jax import lax
from jax.experimental import pallas as pl
from jax.experimental.pallas import tpu as pltpu

F32 = jnp.float32
MXU = jnp.bfloat16
HIGHEST = lax.Precision.HIGHEST

D_MODEL = 2048
DEPTH = 2
EPS = 1e-6
NEG_INF = -1e30
SSM_WIDTH = 1024
SSM_GROUP = 16
SSM_GROUPS = 64
SSM_STATE = 64
SSM_CH = SSM_GROUPS * SSM_STATE
SLAB = 128
SLAB_CH = (SLAB // SSM_GROUP) * SSM_STATE
N_SLAB = SSM_WIDTH // SLAB
SCAN_SEG = 8
SG_HEADS = 8
SG_CHUNK = 128
HEAD_DIM = 64
ATT_HEADS = 16
ATT_KV_HEADS = 2
GQA_GROUP = 8
ATT_BLOCK = 128
WINDOW = 128
ROT_DIM = 16
ROPE_THETA = 500000.0
N_MAIN = 6400
N_ZC = 1024
N_GATES = 6144
D_IN = N_MAIN + N_ZC + N_GATES

ADAM_LR = 0.001
ADAM_B1 = 0.9
ADAM_B2 = 0.999
ADAM_EPS = 1e-08
ADAM_WD = 0.01
ADAM_STEP = 10

_DIMS = {"nn": (((1,), (0,)), ((), ())), "nt": (((1,), (1,)), ((), ())), "tn": (((0,), (0,)), ((), ()))}
_MB = 1024 * 1024


def _cp(sem, vmem_mb=48):
    return pltpu.CompilerParams(dimension_semantics=sem, vmem_limit_bytes=vmem_mb * _MB)


def _dot(a, b, mode):
    return lax.dot_general(a.astype(MXU), b.astype(MXU), _DIMS[mode], preferred_element_type=F32)


@jax.custom_vjp
def _mm_nn(a, b):
    return _dot(a, b, "nn")


def _mm_nn_fwd(a, b):
    return _dot(a, b, "nn"), (a, b)


def _mm_nn_bwd(res, g):
    a, b = res
    return _dot(g, b, "nt"), _dot(a, g, "tn")


_mm_nn.defvjp(_mm_nn_fwd, _mm_nn_bwd)


@jax.custom_vjp
def _mm_nt(a, bt):
    return _dot(a, bt, "nt")


def _mm_nt_fwd(a, bt):
    return _dot(a, bt, "nt"), (a, bt)


def _mm_nt_bwd(res, g):
    a, bt = res
    return _dot(g, bt, "nn"), _dot(g, a, "tn")


_mm_nt.defvjp(_mm_nt_fwd, _mm_nt_bwd)


def _rmsnorm(x, w):
    return x * lax.rsqrt(jnp.mean(x * x, axis=-1, keepdims=True) + EPS) * w


def _layernorm(x, w, b):
    mu = jnp.mean(x, axis=-1, keepdims=True)
    var = jnp.mean(jnp.square(x - mu), axis=-1, keepdims=True)
    return (x - mu) * lax.rsqrt(var + EPS) * w + b


def _silu(x):
    return x * jax.nn.sigmoid(x)


def _matmul(a, b, mode, *, name, shape, tm, tn, tk, out_dtype=F32, add=None,
            a_off=(0, 0), b_off=(0, 0), out=None, out_off=(0, 0)):
    m, n, k = shape
    tm, tn, tk = min(tm, m), min(tn, n), min(tk, k)
    assert m % tm == 0 and n % tn == 0 and k % tk == 0, (name, shape, tm, tn, tk)
    nk = k // tk
    has_add, has_out = add is not None, out is not None

    def body(*refs):
        a_ref, b_ref = refs[0], refs[1]
        pos = 2
        add_ref = None
        if has_add:
            add_ref = refs[pos]
            pos += 1
        if has_out:
            pos += 1
        o_ref = refs[pos]
        p = _dot(a_ref[...], b_ref[...], mode)
        if nk == 1:
            if has_add:
                p = p + add_ref[...].astype(F32)
            o_ref[...] = p.astype(out_dtype)
            return
        acc_ref = refs[pos + 1]
        kk = pl.program_id(2)

        @pl.when(kk == 0)
        def _():
            acc_ref[...] = p

        @pl.when(kk > 0)
        def _():
            acc_ref[...] += p

        @pl.when(kk == nk - 1)
        def _():
            r = acc_ref[...]
            if has_add:
                r = r + add_ref[...].astype(F32)
            o_ref[...] = r.astype(out_dtype)

    a0, a1 = a_off
    b0, b1 = b_off
    o0, o1 = out_off
    if mode == "tn":
        a_spec = pl.BlockSpec((tk, tm), lambda i, j, kk: (kk + a0, i + a1))
    else:
        a_spec = pl.BlockSpec((tm, tk), lambda i, j, kk: (i + a0, kk + a1))
    if mode == "nt":
        b_spec = pl.BlockSpec((tn, tk), lambda i, j, kk: (j + b0, kk + b1))
    else:
        b_spec = pl.BlockSpec((tk, tn), lambda i, j, kk: (kk + b0, j + b1))
    in_specs = [a_spec, b_spec]
    args = [a, b]
    if has_add:
        in_specs.append(pl.BlockSpec((tm, tn), lambda i, j, kk: (i, j)))
        args.append(add)
    aliases = {}
    if has_out:
        aliases = {len(args): 0}
        in_specs.append(pl.BlockSpec(memory_space=pl.ANY))
        args.append(out)
        out_shape = jax.ShapeDtypeStruct(out.shape, out.dtype)
        assert out.dtype == out_dtype
    else:
        out_shape = jax.ShapeDtypeStruct((m, n), out_dtype)
    return pl.pallas_call(
        body, name=name, grid=(m // tm, n // tn, nk),
        in_specs=in_specs,
        out_specs=pl.BlockSpec((tm, tn), lambda i, j, kk: (i + o0, j + o1)),
        out_shape=out_shape,
        scratch_shapes=[pltpu.VMEM((tm, tn), F32)] if nk > 1 else [],
        input_output_aliases=aliases,
        compiler_params=_cp(("parallel", "parallel", "arbitrary")),
    )(*args)


def _rms_fwd(x, w, *, name, tm=256):
    L, d = x.shape
    tm = min(tm, L)

    def body(x_ref, w_ref, h_ref):
        h_ref[...] = _rmsnorm(x_ref[...], w_ref[...]).astype(MXU)

    return pl.pallas_call(
        body, name=name, grid=(L // tm,),
        in_specs=[pl.BlockSpec((tm, d), lambda i: (i, 0)), pl.BlockSpec((1, d), lambda i: (0, 0))],
        out_specs=pl.BlockSpec((tm, d), lambda i: (i, 0)),
        out_shape=jax.ShapeDtypeStruct((L, d), MXU),
        compiler_params=_cp(("parallel",)),
    )(x, w.reshape(1, d))


def _rms_bwd(x, w, dh, dxn, *, name, tm=256):
    L, d = x.shape
    tm = min(tm, L)

    def body(x_ref, w_ref, dh_ref, dxn_ref, dx_ref, dw_ref):
        _, vjp = jax.vjp(_rmsnorm, x_ref[...], w_ref[...])
        dx, dw = vjp(dh_ref[...])
        dx_ref[...] = dx + dxn_ref[...]

        @pl.when(pl.program_id(0) == 0)
        def _():
            dw_ref[...] = jnp.zeros_like(dw_ref)

        dw_ref[...] += dw

    row = pl.BlockSpec((tm, d), lambda i: (i, 0))
    vec = pl.BlockSpec((1, d), lambda i: (0, 0))
    dx, dw = pl.pallas_call(
        body, name=name, grid=(L // tm,),
        in_specs=[row, vec, row, row], out_specs=[row, vec],
        out_shape=[jax.ShapeDtypeStruct((L, d), F32), jax.ShapeDtypeStruct((1, d), F32)],
        compiler_params=_cp(("arbitrary",)),
    )(x, w.reshape(1, d), dh, dxn)
    return dx, dw.reshape(d)


def _final_loss(x, w, tgt, *, name, tm=256):
    L, d = x.shape
    tm = min(tm, L)

    def loss_fn(xv, wv, tv):
        err = jnp.square(_rmsnorm(xv, wv) - tv)
        return 0.5 * jnp.sum(jnp.mean(err, axis=-1, keepdims=True), axis=0, keepdims=True)

    def body(x_ref, w_ref, t_ref, loss_ref, dx_ref, dw_ref):
        tv = t_ref[...]
        val, vjp = jax.vjp(lambda xv, wv: loss_fn(xv, wv, tv), x_ref[...], w_ref[...])
        dx, dw = vjp(jnp.ones((1, 1), F32))
        dx_ref[...] = dx

        @pl.when(pl.program_id(0) == 0)
        def _():
            dw_ref[...] = jnp.zeros_like(dw_ref)
            loss_ref[...] = jnp.zeros_like(loss_ref)

        dw_ref[...] += dw
        loss_ref[...] += jnp.broadcast_to(val, loss_ref.shape)

    row = pl.BlockSpec((tm, d), lambda i: (i, 0))
    vec = pl.BlockSpec((1, d), lambda i: (0, 0))
    loss, dx, dw = pl.pallas_call(
        body, name=name, grid=(L // tm,),
        in_specs=[row, vec, row],
        out_specs=[pl.BlockSpec((8, 128), lambda i: (0, 0)), row, vec],
        out_shape=[jax.ShapeDtypeStruct((8, 128), F32), jax.ShapeDtypeStruct((L, d), F32),
                   jax.ShapeDtypeStruct((1, d), F32)],
        compiler_params=_cp(("arbitrary",)),
    )(x, w.reshape(1, d), tgt)
    return loss[0, 0], dx, dw.reshape(d)


PARAM_ROWS = 512


def _s5_param_fn(are, aim, ldt, bre, bim, row0):
    n = are.shape[0]
    grp = (row0 + lax.broadcasted_iota(jnp.int32, (n, SSM_GROUPS), 0)) // SSM_STATE
    col = lax.broadcasted_iota(jnp.int32, (n, SSM_GROUPS), 1)
    sel = (grp == col).astype(F32)
    dt = jnp.sum(sel * jnp.exp(ldt), axis=-1, keepdims=True)
    mag = jnp.exp(are * dt)
    ang = aim * dt
    lbr = mag * jnp.cos(ang)
    lbi = mag * jnp.sin(ang)
    den = are * are + aim * aim
    nr = lbr - 1.0
    kr = (nr * are + lbi * aim) / den
    ki = (lbi * are - nr * aim) / den
    return lbr, lbi, kr * bre - ki * bim, kr * bim + ki * bre


def _s5_param_specs():
    col = pl.BlockSpec((PARAM_ROWS, 1), lambda i: (i, 0))
    mat = pl.BlockSpec((PARAM_ROWS, SSM_GROUP), lambda i: (i, 0))
    vec = pl.BlockSpec((1, SSM_GROUPS), lambda i: (0, 0))
    return col, mat, vec


def _s5_params_fwd(are, aim, ldt, bre, bim, *, name):
    n = are.shape[0]
    col, mat, vec = _s5_param_specs()

    def body(are_ref, aim_ref, ldt_ref, bre_ref, bim_ref, lbr_ref, lbi_ref, bbr_ref, bbi_ref):
        row0 = pl.program_id(0) * PARAM_ROWS
        lbr, lbi, bbr, bbi = _s5_param_fn(are_ref[...], aim_ref[...], ldt_ref[...], bre_ref[...], bim_ref[...], row0)
        lbr_ref[...] = lbr
        lbi_ref[...] = lbi
        bbr_ref[...] = bbr
        bbi_ref[...] = bbi

    cshape = jax.ShapeDtypeStruct((n, 1), F32)
    mshape = jax.ShapeDtypeStruct((n, SSM_GROUP), F32)
    return pl.pallas_call(body, name=name, grid=(n // PARAM_ROWS,),
                          in_specs=[col, col, vec, mat, mat], out_specs=[col, col, mat, mat],
                          out_shape=[cshape, cshape, mshape, mshape],
                          compiler_params=_cp(("parallel",)))(are, aim, ldt, bre, bim)


def _s5_params_bwd(are, aim, ldt, bre, bim, dlbr, dlbi, dbbr, dbbi, *, name):
    n = are.shape[0]
    col, mat, vec = _s5_param_specs()

    def body(are_ref, aim_ref, ldt_ref, bre_ref, bim_ref, g0, g1, g2, g3, o0, o1, o2, o3, o4):
        row0 = pl.program_id(0) * PARAM_ROWS
        _, vjp = jax.vjp(lambda a, b, c, d, e: _s5_param_fn(a, b, c, d, e, row0),
                         are_ref[...], aim_ref[...], ldt_ref[...], bre_ref[...], bim_ref[...])
        dare, daim, dldt, dbre, dbim = vjp((g0[...], g1[...], g2[...], g3[...]))
        o0[...] = dare
        o1[...] = daim
        o3[...] = dbre
        o4[...] = dbim

        @pl.when(pl.program_id(0) == 0)
        def _():
            o2[...] = jnp.zeros_like(o2)

        o2[...] += dldt

    cshape = jax.ShapeDtypeStruct((n, 1), F32)
    mshape = jax.ShapeDtypeStruct((n, SSM_GROUP), F32)
    return pl.pallas_call(body, name=name, grid=(n // PARAM_ROWS,),
                          in_specs=[col, col, vec, mat, mat, col, col, mat, mat],
                          out_specs=[col, col, vec, mat, mat],
                          out_shape=[cshape, cshape, jax.ShapeDtypeStruct((1, SSM_GROUPS), F32), mshape, mshape],
                          compiler_params=_cp(("arbitrary",)))(are, aim, ldt, bre, bim, dlbr, dlbi, dbbr, dbbi)


SLAB_NC = SLAB_CH // 128


def _s5_specs(L):
    slab = pl.BlockSpec((L, SLAB), lambda s: (0, s))
    wspec = pl.BlockSpec((SLAB_NC, 128, SLAB), lambda s: (s, 0, 0))
    lspec = pl.BlockSpec((SLAB_NC, 1, 128), lambda s: (s, 0, 0))
    sspec = pl.BlockSpec((SLAB_NC, L, 128), lambda s: (s, 0, 0))
    dspec = pl.BlockSpec((1, SLAB), lambda s: (0, s))
    return slab, wspec, lspec, sspec, dspec


def _scan_inplace(sr_ref, si_ref, lr, li, pr_ref, pi_ref, *, reverse):
    NC, L, W = sr_ref.shape
    T = L // SCAN_SEG
    lr8 = [jnp.broadcast_to(lr[k], (SCAN_SEG, W)) for k in range(NC)]
    li8 = [jnp.broadcast_to(li[k], (SCAN_SEG, W)) for k in range(NC)]

    def step(j, carry):
        row = (T - 1 - j) if reverse else j
        idx = pl.ds(row, SCAN_SEG, stride=T)
        out = []
        for k in range(NC):
            sr, si, qr, qi = carry[k]
            nsr = lr8[k] * sr - li8[k] * si + sr_ref[k, idx, :]
            nsi = lr8[k] * si + li8[k] * sr + si_ref[k, idx, :]
            sr_ref[k, idx, :] = nsr
            si_ref[k, idx, :] = nsi
            pr_ref[k, pl.ds(row, 1), :] = qr
            pi_ref[k, pl.ds(row, 1), :] = qi
            out.append((nsr, nsi, qr * lr[k] - qi * li[k], qr * li[k] + qi * lr[k]))
        return tuple(out)

    zero = jnp.zeros((SCAN_SEG, W), F32)
    ends = lax.fori_loop(0, T, step, tuple((zero, zero, lr[k], li[k]) for k in range(NC)))
    last = 0 if reverse else T - 1
    order = range(SCAN_SEG - 1, -1, -1) if reverse else range(SCAN_SEG)
    for k in range(NC):
        er, ei = ends[k][0], ends[k][1]
        ltr = pr_ref[k, pl.ds(last, 1), :]
        lti = pi_ref[k, pl.ds(last, 1), :]
        cr = jnp.zeros((1, W), F32)
        ci = jnp.zeros((1, W), F32)
        for n, seg in enumerate(order):
            if n > 0:
                rows = pl.ds(seg * T, T)
                pr = pr_ref[k]
                pi = pi_ref[k]
                sr_ref[k, rows, :] += pr * cr - pi * ci
                si_ref[k, rows, :] += pr * ci + pi * cr
            if n < SCAN_SEG - 1:
                ncr = er[seg:seg + 1, :] + ltr * cr - lti * ci
                nci = ei[seg:seg + 1, :] + ltr * ci + lti * cr
                cr, ci = ncr, nci


def _s5_fwd(main, btr, bti, cbr, cbi, lbr, lbi, dvec, *, name):
    L = main.shape[0]
    T = L // SCAN_SEG

    def body(u_ref, btr_ref, bti_ref, cbr_ref, cbi_ref, lr_ref, li_ref, d_ref, ys_ref, sr_ref, si_ref, pr_ref, pi_ref):
        u = u_ref[...]
        for k in range(SLAB_NC):
            sr_ref[k] = _dot(u, btr_ref[k], "nt")
            si_ref[k] = _dot(u, bti_ref[k], "nt")
        _scan_inplace(sr_ref, si_ref, lr_ref[...], li_ref[...], pr_ref, pi_ref, reverse=False)
        ys = d_ref[...] * u
        for k in range(SLAB_NC):
            ys = ys + _dot(sr_ref[k], cbr_ref[k], "nn") - _dot(si_ref[k], cbi_ref[k], "nn")
        ys_ref[...] = ys

    slab, wspec, lspec, sspec, dspec = _s5_specs(L)
    sshape = jax.ShapeDtypeStruct((N_SLAB * SLAB_NC, L, 128), F32)
    return pl.pallas_call(
        body, name=name, grid=(N_SLAB,),
        in_specs=[slab, wspec, wspec, wspec, wspec, lspec, lspec, dspec],
        out_specs=[slab, sspec, sspec],
        out_shape=[jax.ShapeDtypeStruct((L, SSM_WIDTH), F32), sshape, sshape],
        scratch_shapes=[pltpu.VMEM((SLAB_NC, T, 128), F32), pltpu.VMEM((SLAB_NC, T, 128), F32)],
        compiler_params=_cp(("parallel",), 56),
    )(main, btr, bti, cbr, cbi, lbr, lbi, dvec)


def _s5_bwd(dys, main, sr, si, btr, bti, cbr, cbi, lbr, lbi, dvec, *, name):
    L = main.shape[0]
    T = L // SCAN_SEG

    def body(dys_ref, u_ref, sr_ref, si_ref, btr_ref, bti_ref, cbr_ref, cbi_ref, lr_ref, li_ref, d_ref,
             du_ref, dbtr_ref, dbti_ref, dcbr_ref, dcbi_ref, dlr_ref, dli_ref, dd_ref,
             ar_ref, ai_ref, pr_ref, pi_ref):
        dys = dys_ref[...]
        u = u_ref[...]
        for k in range(SLAB_NC):
            ar_ref[k] = _dot(dys, cbr_ref[k], "nt")
            ai_ref[k] = -_dot(dys, cbi_ref[k], "nt")
        _scan_inplace(ar_ref, ai_ref, lr_ref[...], -li_ref[...], pr_ref, pi_ref, reverse=True)
        first = lax.broadcasted_iota(jnp.int32, (L, 1), 0) == 0
        du = d_ref[...] * dys
        for k in range(SLAB_NC):
            a_re = ar_ref[k]
            a_im = ai_ref[k]
            du = du + _dot(a_re, btr_ref[k], "nn") + _dot(a_im, bti_ref[k], "nn")
            dbtr_ref[k] = _dot(a_re, u, "tn")
            dbti_ref[k] = _dot(a_im, u, "tn")
            s_re = sr_ref[k]
            s_im = si_ref[k]
            dcbr_ref[k] = _dot(s_re, dys, "tn")
            dcbi_ref[k] = -_dot(s_im, dys, "tn")
            p_re = jnp.where(first, 0.0, pltpu.roll(s_re, 1, 0))
            p_im = jnp.where(first, 0.0, pltpu.roll(s_im, 1, 0))
            dlr_ref[k] = jnp.sum(p_re * a_re + p_im * a_im, axis=0, keepdims=True)
            dli_ref[k] = jnp.sum(p_re * a_im - p_im * a_re, axis=0, keepdims=True)
        du_ref[...] = du
        dd_ref[...] = jnp.sum(dys * u, axis=0, keepdims=True)

    slab, wspec, lspec, sspec, dspec = _s5_specs(L)
    wshape = jax.ShapeDtypeStruct((N_SLAB * SLAB_NC, 128, SLAB), F32)
    lshape = jax.ShapeDtypeStruct((N_SLAB * SLAB_NC, 1, 128), F32)
    return pl.pallas_call(
        body, name=name, grid=(N_SLAB,),
        in_specs=[slab, slab, sspec, sspec, wspec, wspec, wspec, wspec, lspec, lspec, dspec],
        out_specs=[slab, wspec, wspec, wspec, wspec, lspec, lspec, dspec],
        out_shape=[jax.ShapeDtypeStruct((L, SSM_WIDTH), F32), wshape, wshape, wshape, wshape, lshape, lshape,
                   jax.ShapeDtypeStruct((1, SSM_WIDTH), F32)],
        scratch_shapes=[pltpu.VMEM((SLAB_NC, L, 128), F32), pltpu.VMEM((SLAB_NC, L, 128), F32),
                        pltpu.VMEM((SLAB_NC, T, 128), F32), pltpu.VMEM((SLAB_NC, T, 128), F32)],
        compiler_params=_cp(("parallel",), 56),
    )(dys, main, sr, si, btr, bti, cbr, cbi, lbr, lbi, dvec)


_SLAB_MASK = (np.arange(SLAB_CH)[:, None] // SSM_STATE == np.arange(SLAB)[None, :] // SSM_GROUP)


def _expand_bd(x):
    t = jnp.tile(x.reshape(N_SLAB, SLAB_CH, SSM_GROUP), (1, 1, SLAB // SSM_GROUP))
    return jnp.where(_SLAB_MASK[None], t, 0.0).astype(MXU).reshape(N_SLAB * SLAB_NC, 128, SLAB)


def _contract_bd(dx):
    t = jnp.where(_SLAB_MASK[None], dx.reshape(N_SLAB, SLAB_CH, SLAB), 0.0)
    return jnp.sum(t.reshape(N_SLAB, SLAB_CH, SLAB // SSM_GROUP, SSM_GROUP), axis=2).reshape(SSM_CH, SSM_GROUP)


def _glu_ew(ys, zlin, za):
    a1 = jax.nn.gelu(ys)
    return a1 * jax.nn.sigmoid(zlin) * _silu(za)


def _glu_fwd(ys, main, gw, gb, *, name, tm=256):
    L = ys.shape[0]
    tm = min(tm, L)
    W = SSM_WIDTH

    def body(ys_ref, za_ref, gw_ref, gb_ref, ya_ref):
        ys = ys_ref[...]
        a1 = jax.nn.gelu(ys)
        zlin = _dot(a1, gw_ref[...], "nn") + gb_ref[...]
        ya_ref[...] = _glu_ew(ys, zlin, za_ref[...]).astype(MXU)

    return pl.pallas_call(
        body, name=name, grid=(L // tm,),
        in_specs=[pl.BlockSpec((tm, W), lambda i: (i, 0)), pl.BlockSpec((tm, W), lambda i: (i, 1)),
                  pl.BlockSpec((W, W), lambda i: (0, 0)), pl.BlockSpec((1, W), lambda i: (0, 0))],
        out_specs=pl.BlockSpec((tm, W), lambda i: (i, 0)),
        out_shape=jax.ShapeDtypeStruct((L, W), MXU),
        compiler_params=_cp(("parallel",)),
    )(ys, main, gw, gb.reshape(1, W))


def _glu_bwd(dya, ys, main, gw, gb, *, name, tm=256):
    L = ys.shape[0]
    tm = min(tm, L)
    W = SSM_WIDTH

    def body(dya_ref, ys_ref, za_ref, gw_ref, gb_ref, dys_ref, dza_ref, a1_ref, dzl_ref, db_ref):
        ys = ys_ref[...]
        a1, gelu_vjp = jax.vjp(jax.nn.gelu, ys)
        zlin = _dot(a1, gw_ref[...], "nn") + gb_ref[...]
        _, vjp = jax.vjp(lambda a, z, za: a * jax.nn.sigmoid(z) * _silu(za), a1, zlin, za_ref[...])
        da1, dzlin, dza = vjp(dya_ref[...].astype(F32))
        da1 = da1 + _dot(dzlin, gw_ref[...], "nt")
        dys_ref[...] = gelu_vjp(da1)[0]
        dza_ref[...] = dza
        a1_ref[...] = a1.astype(MXU)
        dzl_ref[...] = dzlin.astype(MXU)

        @pl.when(pl.program_id(0) == 0)
        def _():
            db_ref[...] = jnp.zeros_like(db_ref)

        db_ref[...] += jnp.sum(dzlin, axis=0, keepdims=True)

    row = pl.BlockSpec((tm, W), lambda i: (i, 0))
    vec = pl.BlockSpec((1, W), lambda i: (0, 0))
    return pl.pallas_call(
        body, name=name, grid=(L // tm,),
        in_specs=[row, row, pl.BlockSpec((tm, W), lambda i: (i, 1)), pl.BlockSpec((W, W), lambda i: (0, 0)), vec],
        out_specs=[row, row, row, row, vec],
        out_shape=[jax.ShapeDtypeStruct((L, W), F32), jax.ShapeDtypeStruct((L, W), F32),
                   jax.ShapeDtypeStruct((L, W), MXU), jax.ShapeDtypeStruct((L, W), MXU),
                   jax.ShapeDtypeStruct((1, W), F32)],
        compiler_params=_cp(("arbitrary",)),
    )(dya, ys, main, gw, gb.reshape(1, W))


def _sg_fn(ub, vb, zb, lnw, lnb, ws, bs):
    u = jax.nn.gelu(ub)
    v = _layernorm(jax.nn.gelu(vb), lnw, lnb)
    r = lax.broadcasted_iota(jnp.int32, (SG_CHUNK, SG_CHUNK), 0)
    c = lax.broadcasted_iota(jnp.int32, (SG_CHUNK, SG_CHUNK), 1)
    tri = r >= c
    outs = []
    for h in range(SG_HEADS):
        wh = jnp.where(tri, ws[h], 0.0)
        outs.append(_mm_nn(wh, v[:, h * 128:(h + 1) * 128]) + bs[h])
    mixed = jnp.concatenate(outs, axis=1)
    return u * mixed * _silu(zb)


def _sg_specs(L):
    W = SSM_WIDTH
    blk = lambda c: pl.BlockSpec((SG_CHUNK, W), lambda i, c=c: (i, c))
    vec = pl.BlockSpec((1, W), lambda i: (0, 0))
    wspec = pl.BlockSpec((SG_HEADS, SG_CHUNK, SG_CHUNK), lambda i: (0, 0, 0))
    bspec = pl.BlockSpec((SG_HEADS, SG_CHUNK, 1), lambda i: (0, 0, 0))
    return blk, vec, wspec, bspec


def _sg_fwd(main, lnw, lnb, sgw, sgb, *, name):
    L = main.shape[0]
    W = SSM_WIDTH
    blk, vec, wspec, bspec = _sg_specs(L)

    def body(ub_ref, vb_ref, zb_ref, lnw_ref, lnb_ref, w_ref, b_ref, yb_ref):
        ws = [w_ref[h] for h in range(SG_HEADS)]
        bs = [b_ref[h] for h in range(SG_HEADS)]
        yb_ref[...] = _sg_fn(ub_ref[...], vb_ref[...], zb_ref[...], lnw_ref[...], lnb_ref[...], ws, bs).astype(MXU)

    return pl.pallas_call(
        body, name=name, grid=(L // SG_CHUNK,),
        in_specs=[blk(2), blk(3), blk(4), vec, vec, wspec, bspec],
        out_specs=pl.BlockSpec((SG_CHUNK, W), lambda i: (i, 0)),
        out_shape=jax.ShapeDtypeStruct((L, W), MXU),
        compiler_params=_cp(("parallel",)),
    )(main, main, main, lnw.reshape(1, W), lnb.reshape(1, W), sgw, sgb.reshape(SG_HEADS, SG_CHUNK, 1))


def _sg_bwd(dyb, main, lnw, lnb, sgw, sgb, *, name):
    L = main.shape[0]
    W = SSM_WIDTH
    blk, vec, wspec, bspec = _sg_specs(L)

    def body(dyb_ref, ub_ref, vb_ref, zb_ref, lnw_ref, lnb_ref, w_ref, b_ref,
             dub_ref, dvb_ref, dzb_ref, dlnw_ref, dlnb_ref, dw_ref, db_ref):
        ws = [w_ref[h] for h in range(SG_HEADS)]
        bs = [b_ref[h] for h in range(SG_HEADS)]
        _, vjp = jax.vjp(_sg_fn, ub_ref[...], vb_ref[...], zb_ref[...], lnw_ref[...], lnb_ref[...], ws, bs)
        dub, dvb, dzb, dlnw, dlnb, dws, dbs = vjp(dyb_ref[...])

        @pl.when(pl.program_id(0) == 0)
        def _():
            dlnw_ref[...] = jnp.zeros_like(dlnw_ref)
            dlnb_ref[...] = jnp.zeros_like(dlnb_ref)
            dw_ref[...] = jnp.zeros_like(dw_ref)
            db_ref[...] = jnp.zeros_like(db_ref)

        dub_ref[...] = dub
        dvb_ref[...] = dvb
        dzb_ref[...] = dzb
        dlnw_ref[...] += dlnw
        dlnb_ref[...] += dlnb
        for h in range(SG_HEADS):
            dw_ref[h] += dws[h]
            db_ref[h] += dbs[h]

    row = pl.BlockSpec((SG_CHUNK, W), lambda i: (i, 0))
    out = jax.ShapeDtypeStruct((L, W), F32)
    return pl.pallas_call(
        body, name=name, grid=(L // SG_CHUNK,),
        in_specs=[row, blk(2), blk(3), blk(4), vec, vec, wspec, bspec],
        out_specs=[row, row, row, vec, vec, wspec, bspec],
        out_shape=[out, out, out, jax.ShapeDtypeStruct((1, W), F32), jax.ShapeDtypeStruct((1, W), F32),
                   jax.ShapeDtypeStruct((SG_HEADS, SG_CHUNK, SG_CHUNK), F32),
                   jax.ShapeDtypeStruct((SG_HEADS, SG_CHUNK, 1), F32)],
        compiler_params=_cp(("arbitrary",)),
    )(dyb, main, main, main, lnw.reshape(1, W), lnb.reshape(1, W), sgw, sgb.reshape(SG_HEADS, SG_CHUNK, 1))


def _rope_tables(L):
    half = ROT_DIM // 2
    inv_freq = ROPE_THETA ** (-jnp.arange(0, ROT_DIM, 2, dtype=F32) / ROT_DIM)
    ang = jnp.arange(L, dtype=F32)[:, None] * inv_freq[None, :]
    cos = jnp.cos(ang)
    sin = jnp.sin(ang)
    ones = jnp.ones((L, HEAD_DIM - ROT_DIM), F32)
    cosf = jnp.concatenate([cos, cos, ones], axis=1)
    sinf = jnp.concatenate([sin, sin, 0.0 * ones], axis=1)
    rot = np.zeros((HEAD_DIM, HEAD_DIM), np.float32)
    for d in range(half):
        rot[d + half, d] = -1.0
        rot[d, d + half] = 1.0
    return cosf, sinf, jnp.asarray(rot)


def _rope(t, cosf, sinf, rot):
    shp = t.shape
    t2 = t.reshape(-1, HEAD_DIM)
    sw = lax.dot_general(t2, rot, _DIMS["nn"], precision=HIGHEST, preferred_element_type=F32).reshape(shp)
    return t * cosf + sw * sinf


def _attn_block_fn(q, kw, vw, sinks, cq, sq, ck, sk, rot, q0, k0):
    nk = kw.shape[1]
    qr = _rope(q, cq, sq, rot)
    kr = _rope(kw, ck, sk, rot)
    qpos = q0 + lax.broadcasted_iota(jnp.int32, (1, ATT_BLOCK, nk), 1)
    kpos = k0 + lax.broadcasted_iota(jnp.int32, (1, ATT_BLOCK, nk), 2)
    diff = qpos - kpos
    allowed = (diff >= 0) & (diff < WINDOW)
    outs = []
    for kh in range(ATT_KV_HEADS):
        qh = qr[kh * GQA_GROUP:(kh + 1) * GQA_GROUP].reshape(GQA_GROUP * ATT_BLOCK, HEAD_DIM)
        s = _mm_nt(qh, kr[kh]).reshape(GQA_GROUP, ATT_BLOCK, nk) * (HEAD_DIM ** -0.5)
        s = jnp.where(allowed, s, NEG_INF)
        sink = sinks[kh * GQA_GROUP:(kh + 1) * GQA_GROUP]
        m = lax.stop_gradient(jnp.maximum(jnp.max(s, axis=-1, keepdims=True), sink))
        e = jnp.exp(s - m)
        p = e / (jnp.sum(e, axis=-1, keepdims=True) + jnp.exp(sink - m))
        o = _mm_nn(p.reshape(GQA_GROUP * ATT_BLOCK, nk), vw[kh])
        outs.append(o.reshape(GQA_GROUP, ATT_BLOCK, HEAD_DIM))
    return jnp.concatenate(outs, axis=0)


def _attn_common(L):
    nwin = min(2 * ATT_BLOCK, L)
    qspec = pl.BlockSpec((ATT_HEADS, ATT_BLOCK, HEAD_DIM), lambda n: (0, n, 0))
    kvspec = pl.BlockSpec((ATT_KV_HEADS, L, HEAD_DIM), lambda n: (0, 0, 0))
    sspec = pl.BlockSpec((ATT_HEADS, 1, 1), lambda n: (0, 0, 0))
    tq = pl.BlockSpec((ATT_BLOCK, HEAD_DIM), lambda n: (n, 0))
    tk = pl.BlockSpec((L, HEAD_DIM), lambda n: (0, 0))
    rspec = pl.BlockSpec((HEAD_DIM, HEAD_DIM), lambda n: (0, 0))
    return nwin, qspec, kvspec, sspec, tq, tk, rspec


def _attn_fwd(qh, kh, vh, sinks, cosf, sinf, rot, *, name):
    L = qh.shape[1]
    nwin, qspec, kvspec, sspec, tq, tk, rspec = _attn_common(L)

    def body(q_ref, k_ref, v_ref, s_ref, cq_ref, sq_ref, ck_ref, sk_ref, r_ref, o_ref):
        n = pl.program_id(0)
        k0 = pl.multiple_of(jnp.maximum(n - 1, 0) * ATT_BLOCK, ATT_BLOCK)
        win = pl.ds(k0, nwin)
        o_ref[...] = _attn_block_fn(q_ref[...], k_ref[:, win, :], v_ref[:, win, :], s_ref[...],
                                    cq_ref[...], sq_ref[...], ck_ref[win, :], sk_ref[win, :], r_ref[...],
                                    n * ATT_BLOCK, k0)

    return pl.pallas_call(
        body, name=name, grid=(L // ATT_BLOCK,),
        in_specs=[qspec, kvspec, kvspec, sspec, tq, tq, tk, tk, rspec],
        out_specs=qspec,
        out_shape=jax.ShapeDtypeStruct((ATT_HEADS, L, HEAD_DIM), F32),
        compiler_params=_cp(("parallel",)),
    )(qh, kh, vh, sinks.reshape(ATT_HEADS, 1, 1), cosf, sinf, cosf, sinf, rot)


def _attn_bwd(do, qh, kh, vh, sinks, cosf, sinf, rot, *, name):
    L = qh.shape[1]
    nwin, qspec, kvspec, sspec, tq, tk, rspec = _attn_common(L)

    def body(do_ref, q_ref, k_ref, v_ref, s_ref, cq_ref, sq_ref, ck_ref, sk_ref, r_ref,
             dq_ref, dk_ref, dv_ref, ds_ref):
        n = pl.program_id(0)
        k0 = pl.multiple_of(jnp.maximum(n - 1, 0) * ATT_BLOCK, ATT_BLOCK)
        win = pl.ds(k0, nwin)
        cq, sq, ck, sk, rt = cq_ref[...], sq_ref[...], ck_ref[win, :], sk_ref[win, :], r_ref[...]
        q0 = n * ATT_BLOCK
        _, vjp = jax.vjp(lambda q, kw, vw, s: _attn_block_fn(q, kw, vw, s, cq, sq, ck, sk, rt, q0, k0),
                         q_ref[...], k_ref[:, win, :], v_ref[:, win, :], s_ref[...])
        dq, dkw, dvw, ds = vjp(do_ref[...])

        @pl.when(n == 0)
        def _():
            dk_ref[...] = jnp.zeros_like(dk_ref)
            dv_ref[...] = jnp.zeros_like(dv_ref)
            ds_ref[...] = jnp.zeros_like(ds_ref)

        dq_ref[...] = dq
        dk_ref[:, win, :] += dkw
        dv_ref[:, win, :] += dvw
        ds_ref[...] += ds

    return pl.pallas_call(
        body, name=name, grid=(L // ATT_BLOCK,),
        in_specs=[qspec, qspec, kvspec, kvspec, sspec, tq, tq, tk, tk, rspec],
        out_specs=[qspec, kvspec, kvspec, sspec],
        out_shape=[jax.ShapeDtypeStruct((ATT_HEADS, L, HEAD_DIM), F32),
                   jax.ShapeDtypeStruct((ATT_KV_HEADS, L, HEAD_DIM), F32),
                   jax.ShapeDtypeStruct((ATT_KV_HEADS, L, HEAD_DIM), F32),
                   jax.ShapeDtypeStruct((ATT_HEADS, 1, 1), F32)],
        compiler_params=_cp(("arbitrary",)),
    )(do, qh, kh, vh, sinks.reshape(ATT_HEADS, 1, 1), cosf, sinf, cosf, sinf, rot)


def _to_heads(t, nh):
    L = t.shape[0]
    return t.reshape(L, nh, HEAD_DIM).transpose(1, 0, 2)


def _from_heads(t):
    nh, L, _ = t.shape
    return t.transpose(1, 0, 2).reshape(L, nh * HEAD_DIM)


def _branch_fwd(ya, yb, o2d, zc, gates, wa, wb, wc, *, name, tm=256):
    L = ya.shape[0]
    tm = min(tm, L)
    W, D = SSM_WIDTH, D_MODEL

    def body(ya_ref, yb_ref, o_ref, zc_ref, g0_ref, g1_ref, g2_ref, wa_ref, wb_ref, wc_ref,
             mg_ref, ta_ref, tb_ref, tc_ref, yc_ref):
        yc = (o_ref[...] * _silu(zc_ref[...])).astype(MXU)
        ta = _dot(ya_ref[...], wa_ref[...], "nt")
        tb = _dot(yb_ref[...], wb_ref[...], "nt")
        tc = _dot(yc, wc_ref[...], "nt")
        ta_ref[...] = ta
        tb_ref[...] = tb
        tc_ref[...] = tc
        yc_ref[...] = yc
        mg_ref[...] = (jax.nn.sigmoid(g0_ref[...]) * ta + jax.nn.sigmoid(g1_ref[...]) * tb
                       + jax.nn.sigmoid(g2_ref[...]) * tc).astype(MXU)

    row = pl.BlockSpec((tm, W), lambda i: (i, 0))
    wide = pl.BlockSpec((tm, D), lambda i: (i, 0))
    gate = lambda c: pl.BlockSpec((tm, D), lambda i, c=c: (i, c))
    wspec = pl.BlockSpec((D, W), lambda i: (0, 0))
    return pl.pallas_call(
        body, name=name, grid=(L // tm,),
        in_specs=[row, row, row, row, gate(0), gate(1), gate(2), wspec, wspec, wspec],
        out_specs=[wide, wide, wide, wide, row],
        out_shape=[jax.ShapeDtypeStruct((L, D), MXU), jax.ShapeDtypeStruct((L, D), F32),
                   jax.ShapeDtypeStruct((L, D), F32), jax.ShapeDtypeStruct((L, D), F32),
                   jax.ShapeDtypeStruct((L, W), MXU)],
        compiler_params=_cp(("parallel",), 56),
    )(ya, yb, o2d, zc, gates, gates, gates, wa, wb, wc)


def _branch_bwd(dmg, ta, tb, tc, gates, *, name, tm=256):
    L = dmg.shape[0]
    tm = min(tm, L)
    D = D_MODEL

    def body(dm_ref, ta_ref, tb_ref, tc_ref, g0_ref, g1_ref, g2_ref, da_ref, db_ref, dc_ref, dg_ref):
        dm = dm_ref[...]
        for i, (t_ref, g_ref, d_ref) in enumerate(((ta_ref, g0_ref, da_ref), (tb_ref, g1_ref, db_ref),
                                                   (tc_ref, g2_ref, dc_ref))):
            sg = jax.nn.sigmoid(g_ref[...])
            d_ref[...] = (sg * dm).astype(MXU)
            dg_ref[:, i * D:(i + 1) * D] = dm * t_ref[...] * sg * (1.0 - sg)

    wide = pl.BlockSpec((tm, D), lambda i: (i, 0))
    gate = lambda c: pl.BlockSpec((tm, D), lambda i, c=c: (i, c))
    bf = jax.ShapeDtypeStruct((L, D), MXU)
    return pl.pallas_call(
        body, name=name, grid=(L // tm,),
        in_specs=[wide, wide, wide, wide, gate(0), gate(1), gate(2)],
        out_specs=[wide, wide, wide, pl.BlockSpec((tm, 3 * D), lambda i: (i, 0))],
        out_shape=[bf, bf, bf, jax.ShapeDtypeStruct((L, 3 * D), F32)],
        compiler_params=_cp(("parallel",), 56),
    )(dmg, ta, tb, tc, gates, gates, gates)


def _gate_c_bwd(dyc, o2d, zc, *, name, tm=256):
    L, W = dyc.shape
    tm = min(tm, L)

    def body(dy_ref, o_ref, z_ref, do_ref, dz_ref):
        _, vjp = jax.vjp(lambda o, z: o * _silu(z), o_ref[...], z_ref[...])
        do, dz = vjp(dy_ref[...])
        do_ref[...] = do
        dz_ref[...] = dz

    row = pl.BlockSpec((tm, W), lambda i: (i, 0))
    out = jax.ShapeDtypeStruct((L, W), F32)
    return pl.pallas_call(body, name=name, grid=(L // tm,), in_specs=[row, row, row], out_specs=[row, row],
                          out_shape=[out, out], compiler_params=_cp(("parallel",)))(dyc, o2d, zc)


def _adamw(w, g, m, v, *, name):
    shape = w.shape
    cols = shape[-1]
    w2, g2, m2, v2 = (t.reshape(-1, cols) for t in (w, g, m, v))
    rows = w2.shape[0]
    tr = rows
    while tr % 16 == 0 and tr * cols * 4 > 2 * _MB:
        tr //= 2

    def body(w_ref, g_ref, m_ref, v_ref, d_ref, nm_ref, nv_ref):
        gv = g_ref[...]
        nm = ADAM_B1 * m_ref[...] + (1.0 - ADAM_B1) * gv
        nv = ADAM_B2 * v_ref[...] + (1.0 - ADAM_B2) * jnp.square(gv)
        m_hat = nm / (1.0 - ADAM_B1 ** ADAM_STEP)
        v_hat = nv / (1.0 - ADAM_B2 ** ADAM_STEP)
        d_ref[...] = -ADAM_LR * (m_hat / (jnp.sqrt(v_hat) + ADAM_EPS) + ADAM_WD * w_ref[...])
        nm_ref[...] = nm
        nv_ref[...] = nv

    spec = pl.BlockSpec((tr, cols), lambda i: (i, 0))
    out = jax.ShapeDtypeStruct((rows, cols), F32)
    d, nm, nv = pl.pallas_call(body, name=name, grid=(rows // tr,), in_specs=[spec] * 4, out_specs=[spec] * 3,
                               out_shape=[out, out, out], compiler_params=_cp(("parallel",)))(w2, g2, m2, v2)
    return d.reshape(shape), nm.reshape(shape), nv.reshape(shape)


def _prep_layer(p, l):
    are = p["ssm_a_re"][l].reshape(SSM_CH, 1)
    aim = p["ssm_a_im"][l].reshape(SSM_CH, 1)
    ldt = p["ssm_log_dt"][l].reshape(1, SSM_GROUPS)
    bre = p["ssm_b_re"][l].reshape(SSM_CH, SSM_GROUP)
    bim = p["ssm_b_im"][l].reshape(SSM_CH, SSM_GROUP)
    lbr, lbi, bbr, bbi = _s5_params_fwd(are, aim, ldt, bre, bim, name=f"s5_params_fwd_{l}")
    cre = p["ssm_c_re"][l].transpose(0, 2, 1).reshape(SSM_CH, SSM_GROUP)
    cim = p["ssm_c_im"][l].transpose(0, 2, 1).reshape(SSM_CH, SSM_GROUP)
    return dict(raw=(are, aim, ldt, bre, bim),
                lbr=lbr.reshape(N_SLAB * SLAB_NC, 1, 128), lbi=lbi.reshape(N_SLAB * SLAB_NC, 1, 128),
                btr=_expand_bd(bbr), bti=_expand_bd(bbi), cbr=_expand_bd(cre), cbi=_expand_bd(cim),
                dvec=p["ssm_d"][l].reshape(1, SSM_WIDTH))


def _layer_fwd(x, p, big, l, tabs):
    L = x.shape[0]
    cosf, sinf, rot = tabs
    sp = _prep_layer(p, l)
    h = _rms_fwd(x, p["norm_w"][l], name=f"rms_fwd_{l}")
    mm = functools.partial(_matmul, h, big["winT"], "nt", tm=L, tn=256, tk=D_MODEL)
    main = mm(name=f"proj_main_{l}", shape=(L, N_MAIN, D_MODEL))
    zc = mm(name=f"proj_zc_{l}", shape=(L, N_ZC, D_MODEL), b_off=(N_MAIN // 256, 0))
    gates = mm(name=f"proj_gates_{l}", shape=(L, N_GATES, D_MODEL), b_off=((N_MAIN + N_ZC) // 256, 0))
    ys, sr, si = _s5_fwd(main, sp["btr"], sp["bti"], sp["cbr"], sp["cbi"], sp["lbr"], sp["lbi"], sp["dvec"],
                         name=f"s5_fwd_{l}")
    ya = _glu_fwd(ys, main, big["glu_w"], p["ssm_glu_b"][l], name=f"glu_fwd_{l}")
    yb = _sg_fwd(main, p["sg_ln_w"][l], p["sg_ln_b"][l], p["sg_w"][l], p["sg_b"][l], name=f"sg_fwd_{l}")
    qh = _to_heads(main[:, 5120:6144], ATT_HEADS)
    kh = _to_heads(main[:, 6144:6272], ATT_KV_HEADS)
    vh = _to_heads(main[:, 6272:6400], ATT_KV_HEADS)
    oh = _attn_fwd(qh, kh, vh, p["attn_sinks"][l], cosf, sinf, rot, name=f"attn_fwd_{l}")
    o2d = _from_heads(oh)
    mg, ta, tb, tc, yc = _branch_fwd(ya, yb, o2d, zc, gates, big["wbaT"], big["wbbT"], big["wbcT"],
                                     name=f"branch_fwd_{l}")
    xn = _matmul(mg, big["w_out"], "nn", name=f"out_fwd_{l}", shape=(L, D_MODEL, D_MODEL), tm=512, tn=512,
                 tk=D_MODEL, add=x)
    saved = dict(x=x, h=h, main=main, zc=zc, gates=gates, ys=ys, sr=sr, si=si, ya=ya, yb=yb, yc=yc, o2d=o2d,
                 qh=qh, kh=kh, vh=vh, mg=mg, ta=ta, tb=tb, tc=tc, sp=sp)
    return xn, saved


def _layer_bwd(dxn, s, p, big, l, tabs):
    L = dxn.shape[0]
    D, W = D_MODEL, SSM_WIDTH
    cosf, sinf, rot = tabs
    sp = s["sp"]
    g = {}
    dmg = _matmul(dxn, big["w_out"], "nt", name=f"out_bwd_dm_{l}", shape=(L, D, D), tm=512, tn=512, tk=D)
    g["w_out"] = _matmul(s["mg"], dxn, "tn", name=f"out_bwd_dw_{l}", shape=(D, D, L), tm=512, tn=512, tk=L)
    dta, dtb, dtc, dgates = _branch_bwd(dmg, s["ta"], s["tb"], s["tc"], s["gates"], name=f"branch_bwd_{l}")
    dys_ = {}
    for nm, dt, y, wt in (("a", dta, s["ya"], big["wbaT"]), ("b", dtb, s["yb"], big["wbbT"]),
                          ("c", dtc, s["yc"], big["wbcT"])):
        dys_[nm] = _matmul(dt, wt, "nn", name=f"branch_bwd_dy{nm}_{l}", shape=(L, W, D), tm=512, tn=512, tk=D)
        g["wb" + nm + "T"] = _matmul(dt, y, "tn", name=f"branch_bwd_dw{nm}_{l}", shape=(D, W, L),
                                     tm=512, tn=512, tk=L)
    do2d, dzc = _gate_c_bwd(dys_["c"], s["o2d"], s["zc"], name=f"gate_c_bwd_{l}")
    dqh, dkh, dvh, dsinks = _attn_bwd(_to_heads(do2d, ATT_HEADS), s["qh"], s["kh"], s["vh"], p["attn_sinks"][l],
                                      cosf, sinf, rot, name=f"attn_bwd_{l}")
    g["attn_sinks"] = dsinks.reshape(ATT_HEADS)
    dub, dvb, dzb, dlnw, dlnb, dsgw, dsgb = _sg_bwd(dys_["b"], s["main"], p["sg_ln_w"][l], p["sg_ln_b"][l],
                                                    p["sg_w"][l], p["sg_b"][l], name=f"sg_bwd_{l}")
    g["sg_ln_w"], g["sg_ln_b"] = dlnw.reshape(W), dlnb.reshape(W)
    g["sg_w"], g["sg_b"] = dsgw, dsgb.reshape(SG_HEADS, SG_CHUNK)
    dys, dza, a1, dzl, dgb = _glu_bwd(dys_["a"], s["ys"], s["main"], big["glu_w"], p["ssm_glu_b"][l],
                                      name=f"glu_bwd_{l}")
    g["ssm_glu_b"] = dgb.reshape(W)
    g["glu_w"] = _matmul(a1, dzl, "tn", name=f"glu_bwd_dw_{l}", shape=(W, W, L), tm=512, tn=512, tk=L)
    dua, dbtr, dbti, dcbr, dcbi, dlr, dli, dd = _s5_bwd(dys, s["main"], s["sr"], s["si"], sp["btr"], sp["bti"],
                                                        sp["cbr"], sp["cbi"], sp["lbr"], sp["lbi"], sp["dvec"],
                                                        name=f"s5_bwd_{l}")
    g["ssm_d"] = dd.reshape(W)
    to_c = lambda t: _contract_bd(t).reshape(SSM_GROUPS, SSM_STATE, SSM_GROUP).transpose(0, 2, 1)
    g["ssm_c_re"], g["ssm_c_im"] = to_c(dcbr), to_c(dcbi)
    dare, daim, dldt, dbre, dbim = _s5_params_bwd(*sp["raw"], dlr.reshape(SSM_CH, 1), dli.reshape(SSM_CH, 1),
                                                  _contract_bd(dbtr), _contract_bd(dbti),
                                                  name=f"s5_params_bwd_{l}")
    g["ssm_a_re"] = dare.reshape(SSM_GROUPS, SSM_STATE)
    g["ssm_a_im"] = daim.reshape(SSM_GROUPS, SSM_STATE)
    g["ssm_log_dt"] = dldt.reshape(SSM_GROUPS)
    g["ssm_b_re"] = dbre.reshape(SSM_GROUPS, SSM_STATE, SSM_GROUP)
    g["ssm_b_im"] = dbim.reshape(SSM_GROUPS, SSM_STATE, SSM_GROUP)
    dmain = jnp.concatenate([dua, dza, dub, dvb, dzb, _from_heads(dqh), _from_heads(dkh), _from_heads(dvh)], axis=1)
    mmh = functools.partial(_matmul, mode="nn", tm=512, tn=512, tk=256)
    dh = mmh(dmain, big["winT"], name=f"proj_bwd_dh0_{l}", shape=(L, D, N_MAIN))
    dh = mmh(dzc, big["winT"], name=f"proj_bwd_dh1_{l}", shape=(L, D, N_ZC), b_off=(N_MAIN // 256, 0), add=dh)
    dh = mmh(dgates, big["winT"], name=f"proj_bwd_dh2_{l}", shape=(L, D, N_GATES),
             b_off=((N_MAIN + N_ZC) // 256, 0), add=dh)
    mmw = functools.partial(_matmul, b=s["h"], mode="tn", tm=256, tn=D, tk=L)
    dwin = lax.empty((D_IN, D), F32)
    dwin = mmw(dmain, name=f"proj_bwd_dwm_{l}", shape=(N_MAIN, D, L), out=dwin)
    dwin = mmw(dzc, name=f"proj_bwd_dwz_{l}", shape=(N_ZC, D, L), out=dwin, out_off=(N_MAIN // 256, 0))
    dwin = mmw(dgates, name=f"proj_bwd_dwg_{l}", shape=(N_GATES, D, L), out=dwin,
               out_off=((N_MAIN + N_ZC) // 256, 0))
    g["winT"] = dwin
    dx, dnw = _rms_bwd(s["x"], p["norm_w"][l], dh, dxn, name=f"rms_bwd_{l}")
    g["norm_w"] = dnw
    return dx, g


def _local_step(x, tgt, p, bigs):
    L = x.shape[0]
    tabs = _rope_tables(L)
    saved = []
    for l in range(DEPTH):
        x, s = _layer_fwd(x, p, bigs[l], l, tabs)
        saved.append(s)
    loss, dx, dfw = _final_loss(x, p["final_norm_w"], tgt, name="final_loss")
    grads = [None] * DEPTH
    for l in reversed(range(DEPTH)):
        dx, grads[l] = _layer_bwd(dx, saved[l], p, bigs[l], l, tabs)
    return loss, dx, grads, dfw


MESH = pl.DeviceIdType.MESH
_ANY = pl.BlockSpec(memory_space=pl.ANY)
ROW_ALIGN = 16


def _coords():
    return lax.axis_index("x"), lax.axis_index("y"), lax.axis_index("c")


def _gather8(arrs, *, split, name):
    n = len(arrs)
    rows = [a.shape[0] // 2 if split else a.shape[0] for a in arrs]
    for r in rows:
        assert r % ROW_ALIGN == 0

    def body(*refs):
        ins, outs = refs[:n], refs[n:2 * n]
        send, recv, lsem = refs[2 * n:]
        x, y, c = _coords()
        me, sibling = (x, y, c), (x, y, 1 - c)
        chips = [(1 - x, y), (x, 1 - y), (1 - x, 1 - y)]

        def blk(a, px, py, pc):
            return outs[a].at[pl.ds(pl.multiple_of((4 * px + 2 * py + pc) * rows[a], ROW_ALIGN), rows[a]), :]

        def own(a):
            if split:
                return ins[a].at[pl.ds(pl.multiple_of(c * rows[a], ROW_ALIGN), rows[a]), :]
            return ins[a]

        def copy(a, k, block, to, src=None):
            return pltpu.make_async_remote_copy(
                src_ref=blk(a, *block) if src is None else src, dst_ref=blk(a, *block),
                send_sem=send.at[a, k], recv_sem=recv.at[a, k], device_id=to, device_id_type=MESH)

        mine, first, passed = [], [], []
        for a in range(n):
            mine.append(pltpu.make_async_copy(own(a), blk(a, *me), lsem.at[a]))
            mine[a].start()
            f = [copy(a, 0, me, sibling, src=own(a))]
            f += [copy(a, 1 + j, me, (*chip, c), src=own(a)) for j, chip in enumerate(chips)]
            for cp in f:
                cp.start()
            first.append(f)
        for a in range(n):
            ps = [copy(a, 4 + j, (*chip, c), sibling) for j, chip in enumerate(chips)]
            for j, chip in enumerate(chips):
                copy(a, 1 + j, (*chip, c), me).wait_recv()
                ps[j].start()
            passed.append(ps)
        for a in range(n):
            copy(a, 0, sibling, me).wait_recv()
            for j, chip in enumerate(chips):
                copy(a, 4 + j, (*chip, 1 - c), me).wait_recv()
            for cp in first[a] + passed[a]:
                cp.wait_send()
            mine[a].wait()

    return pl.pallas_call(
        body, name=name,
        in_specs=[_ANY] * n, out_specs=[_ANY] * n,
        out_shape=[jax.ShapeDtypeStruct((8 * r,) + a.shape[1:], a.dtype) for r, a in zip(rows, arrs)],
        scratch_shapes=[pltpu.SemaphoreType.DMA((n, 7)), pltpu.SemaphoreType.DMA((n, 7)), pltpu.SemaphoreType.DMA((n,))],
    )(*arrs)


def _sibling_swap(arrs, *, pick_other, name):
    n = len(arrs)

    def body(*refs):
        ins, outs = refs[:n], refs[n:2 * n]
        send, recv = refs[2 * n:]
        x, y, c = _coords()
        sel = (1 - c) if pick_other else c
        cps = [pltpu.make_async_remote_copy(src_ref=ins[a].at[:, sel], dst_ref=outs[a], send_sem=send.at[a],
                                            recv_sem=recv.at[a], device_id=(x, y, 1 - c), device_id_type=MESH)
               for a in range(n)]
        for cp in cps:
            cp.start()
        for cp in cps:
            cp.wait_recv()
        for cp in cps:
            cp.wait_send()

    return pl.pallas_call(
        body, name=name, in_specs=[_ANY] * n, out_specs=[_ANY] * n,
        out_shape=[jax.ShapeDtypeStruct((a.shape[0],) + a.shape[2:], a.dtype) for a in arrs],
        scratch_shapes=[pltpu.SemaphoreType.DMA((n,)), pltpu.SemaphoreType.DMA((n,))],
    )(*arrs)


def _col_tile(lead, rows, cols, itemsize=4, cap=4 * _MB):
    tc = cols
    while tc % 256 == 0 and lead * rows * tc * itemsize > cap:
        tc //= 2
    return tc


def _pair_sum(mine, theirs, *, name):
    _, _, rows, cols = mine.shape
    tc = _col_tile(1, rows, cols)
    c = lax.axis_index("c")

    def body(c_ref, a_ref, b_ref, o_ref):
        o_ref[...] = (a_ref[...] + b_ref[...]).astype(MXU)

    return pl.pallas_call(
        body, name=name,
        grid_spec=pltpu.PrefetchScalarGridSpec(
            num_scalar_prefetch=1, grid=(4, cols // tc),
            in_specs=[pl.BlockSpec((None, None, rows, tc), lambda j, i, cr: (j, cr[0], 0, i)),
                      pl.BlockSpec((None, rows, tc), lambda j, i, cr: (j, 0, i))],
            out_specs=pl.BlockSpec((None, rows, tc), lambda j, i, cr: (j, 0, i))),
        out_shape=jax.ShapeDtypeStruct((4, rows, cols), MXU),
        compiler_params=_cp(("parallel", "parallel")),
    )(c.reshape(1).astype(jnp.int32), mine, theirs)


def _chip_scatter(parts, *, name):
    n = len(parts)

    def body(*refs):
        ins, outs = refs[:n], refs[n:2 * n]
        send, recv, lsem = refs[2 * n:]
        x, y, c = _coords()
        jme = 2 * x + y
        chips = [(1 - x, y), (x, 1 - y), (1 - x, 1 - y)]
        local, cps = [], []
        for a in range(n):
            local.append(pltpu.make_async_copy(ins[a].at[jme], outs[a].at[jme], lsem.at[a]))
            local[a].start()
            for k, (px, py) in enumerate(chips):
                cp = pltpu.make_async_remote_copy(src_ref=ins[a].at[2 * px + py], dst_ref=outs[a].at[jme],
                                                  send_sem=send.at[a, k], recv_sem=recv.at[a, k],
                                                  device_id=(px, py, c), device_id_type=MESH)
                cp.start()
                cps.append(cp)
        for cp in cps:
            cp.wait_recv()
        for cp in cps:
            cp.wait_send()
        for a in range(n):
            local[a].wait()

    return pl.pallas_call(
        body, name=name, in_specs=[_ANY] * n, out_specs=[_ANY] * n,
        out_shape=[jax.ShapeDtypeStruct(p.shape, p.dtype) for p in parts],
        scratch_shapes=[pltpu.SemaphoreType.DMA((n, 3)), pltpu.SemaphoreType.DMA((n, 3)), pltpu.SemaphoreType.DMA((n,))],
    )(*parts)


def _sum_slots(t, *, name):
    S, rows, cols = t.shape
    tc = _col_tile(S, rows, cols)

    def body(t_ref, o_ref):
        acc = t_ref[0].astype(F32)
        for s in range(1, S):
            acc = acc + t_ref[s].astype(F32)
        o_ref[...] = acc

    return pl.pallas_call(
        body, name=name, grid=(cols // tc,),
        in_specs=[pl.BlockSpec((S, rows, tc), lambda i: (0, 0, i))],
        out_specs=pl.BlockSpec((rows, tc), lambda i: (0, i)),
        out_shape=jax.ShapeDtypeStruct((rows, cols), F32),
        compiler_params=_cp(("parallel",)),
    )(t)


def _halves_join(halves, *, name):
    n = len(halves)

    def body(*refs):
        ins, outs = refs[:n], refs[n:2 * n]
        send, recv, lsem = refs[2 * n:]
        x, y, c = _coords()
        cps, loc = [], []
        for a in range(n):
            loc.append(pltpu.make_async_copy(ins[a], outs[a].at[c], lsem.at[a]))
            loc[a].start()
            cps.append(pltpu.make_async_remote_copy(src_ref=ins[a], dst_ref=outs[a].at[c], send_sem=send.at[a],
                                                    recv_sem=recv.at[a], device_id=(x, y, 1 - c), device_id_type=MESH))
            cps[a].start()
        for cp in cps:
            cp.wait_recv()
        for cp in cps:
            cp.wait_send()
        for l in loc:
            l.wait()

    return pl.pallas_call(
        body, name=name, in_specs=[_ANY] * n, out_specs=[_ANY] * n,
        out_shape=[jax.ShapeDtypeStruct((2,) + h.shape, h.dtype) for h in halves],
        scratch_shapes=[pltpu.SemaphoreType.DMA((n,)), pltpu.SemaphoreType.DMA((n,)), pltpu.SemaphoreType.DMA((n,))],
    )(*halves)


def _reduce_scatter(grads, *, tag):
    views = [g.reshape(4, 2, g.shape[0] // 8, g.shape[1]) for g in grads]
    theirs = _sibling_swap(views, pick_other=True, name=f"rs_swap_{tag}")
    parts = [_pair_sum(v, t, name=f"rs_pair_{tag}_{i}") for i, (v, t) in enumerate(zip(views, theirs))]
    got = _chip_scatter(parts, name=f"rs_scatter_{tag}")
    halves = [_sum_slots(t, name=f"rs_sum_{tag}_{i}") for i, t in enumerate(got)]
    joined = _halves_join(halves, name=f"rs_join_{tag}")
    return [j.reshape(2 * j.shape[1], j.shape[2]) for j in joined]


_SMALL = ("norm_w", "ssm_a_re", "ssm_a_im", "ssm_log_dt", "ssm_b_re", "ssm_b_im", "ssm_c_re", "ssm_c_im", "ssm_d",
          "ssm_glu_b", "sg_ln_w", "sg_ln_b", "sg_w", "sg_b", "attn_sinks", "final_norm_w")
_BIG = ("w_in", "ssm_glu_w", "w_branch_a", "w_branch_b", "w_branch_c", "w_out")
_WEIGHTS = ("norm_w", "w_in", "ssm_a_re", "ssm_a_im", "ssm_log_dt", "ssm_b_re", "ssm_b_im", "ssm_c_re", "ssm_c_im",
            "ssm_d", "ssm_glu_w", "ssm_glu_b", "sg_ln_w", "sg_ln_b", "sg_w", "sg_b", "attn_sinks", "w_branch_a",
            "w_branch_b", "w_branch_c", "w_out", "final_norm_w")
_PACK_COLS = 1024
_PACK_ALIGN = 16 * _PACK_COLS


def _pack(ts):
    flat = jnp.concatenate([t.reshape(-1) for t in ts])
    pad = (-flat.shape[0]) % _PACK_ALIGN
    return jnp.pad(flat, (0, pad)).reshape(-1, _PACK_COLS)


def _unpack(buf, like):
    flat = buf.reshape(-1)
    out, pos = [], 0
    for t in like:
        out.append(flat[pos:pos + t.size].reshape(t.shape))
        pos += t.size
    return out


def kernel(x, norm_w, w_in, ssm_a_re, ssm_a_im, ssm_log_dt, ssm_b_re, ssm_b_im, ssm_c_re, ssm_c_im, ssm_d, ssm_glu_w, ssm_glu_b, sg_ln_w, sg_ln_b, sg_w, sg_b, attn_sinks, w_branch_a, w_branch_b, w_branch_c, w_out, final_norm_w, loss_target, m_norm_w, m_w_in, m_ssm_a_re, m_ssm_a_im, m_ssm_log_dt, m_ssm_b_re, m_ssm_b_im, m_ssm_c_re, m_ssm_c_im, m_ssm_d, m_ssm_glu_w, m_ssm_glu_b, m_sg_ln_w, m_sg_ln_b, m_sg_w, m_sg_b, m_attn_sinks, m_w_branch_a, m_w_branch_b, m_w_branch_c, m_w_out, m_final_norm_w, v_norm_w, v_w_in, v_ssm_a_re, v_ssm_a_im, v_ssm_log_dt, v_ssm_b_re, v_ssm_b_im, v_ssm_c_re, v_ssm_c_im, v_ssm_d, v_ssm_glu_w, v_ssm_glu_b, v_sg_ln_w, v_sg_ln_b, v_sg_w, v_sg_b, v_attn_sinks, v_w_branch_a, v_w_branch_b, v_w_branch_c, v_w_out, v_final_norm_w):
    w = dict(norm_w=norm_w, w_in=w_in, ssm_a_re=ssm_a_re, ssm_a_im=ssm_a_im, ssm_log_dt=ssm_log_dt, ssm_b_re=ssm_b_re,
             ssm_b_im=ssm_b_im, ssm_c_re=ssm_c_re, ssm_c_im=ssm_c_im, ssm_d=ssm_d, ssm_glu_w=ssm_glu_w,
             ssm_glu_b=ssm_glu_b, sg_ln_w=sg_ln_w, sg_ln_b=sg_ln_b, sg_w=sg_w, sg_b=sg_b, attn_sinks=attn_sinks,
             w_branch_a=w_branch_a, w_branch_b=w_branch_b, w_branch_c=w_branch_c, w_out=w_out,
             final_norm_w=final_norm_w)
    m = dict(norm_w=m_norm_w, w_in=m_w_in, ssm_a_re=m_ssm_a_re, ssm_a_im=m_ssm_a_im, ssm_log_dt=m_ssm_log_dt,
             ssm_b_re=m_ssm_b_re, ssm_b_im=m_ssm_b_im, ssm_c_re=m_ssm_c_re, ssm_c_im=m_ssm_c_im, ssm_d=m_ssm_d,
             ssm_glu_w=m_ssm_glu_w, ssm_glu_b=m_ssm_glu_b, sg_ln_w=m_sg_ln_w, sg_ln_b=m_sg_ln_b, sg_w=m_sg_w,
             sg_b=m_sg_b, attn_sinks=m_attn_sinks, w_branch_a=m_w_branch_a, w_branch_b=m_w_branch_b,
             w_branch_c=m_w_branch_c, w_out=m_w_out, final_norm_w=m_final_norm_w)
    v = dict(norm_w=v_norm_w, w_in=v_w_in, ssm_a_re=v_ssm_a_re, ssm_a_im=v_ssm_a_im, ssm_log_dt=v_ssm_log_dt,
             ssm_b_re=v_ssm_b_re, ssm_b_im=v_ssm_b_im, ssm_c_re=v_ssm_c_re, ssm_c_im=v_ssm_c_im, ssm_d=v_ssm_d,
             ssm_glu_w=v_ssm_glu_w, ssm_glu_b=v_ssm_glu_b, sg_ln_w=v_sg_ln_w, sg_ln_b=v_sg_ln_b, sg_w=v_sg_w,
             sg_b=v_sg_b, attn_sinks=v_attn_sinks, w_branch_a=v_w_branch_a, w_branch_b=v_w_branch_b,
             w_branch_c=v_w_branch_c, w_out=v_w_out, final_norm_w=v_final_norm_w)

    bigs = []
    for l in range(DEPTH):
        shards = [w["w_in"][l].T.astype(MXU), w["ssm_glu_w"][l].astype(MXU), w["w_branch_a"][l].T.astype(MXU),
                  w["w_branch_b"][l].T.astype(MXU), w["w_branch_c"][l].T.astype(MXU), w["w_out"][l].astype(MXU)]
        full = _gather8(shards, split=True, name=f"gather_weights_{l}")
        bigs.append(dict(zip(("winT", "glu_w", "wbaT", "wbbT", "wbcT", "w_out"), full)))

    p = {k: w[k] for k in _SMALL}
    loss, dx, grads, dfw = _local_step(x[0], loss_target[0], p, bigs)
    loss = lax.psum(loss, ("x", "y", "c"))

    gbig = {k: [] for k in _BIG}
    for l in range(DEPTH):
        g = grads[l]
        red = _reduce_scatter([g["winT"], g["glu_w"], g["wbaT"], g["wbbT"], g["wbcT"], g["w_out"]], tag=str(l))
        gbig["w_in"].append(red[0].T)
        gbig["ssm_glu_w"].append(red[1])
        gbig["w_branch_a"].append(red[2].T)
        gbig["w_branch_b"].append(red[3].T)
        gbig["w_branch_c"].append(red[4].T)
        gbig["w_out"].append(red[5])
    gfull = {k: jnp.stack(vs) for k, vs in gbig.items()}

    small_like = [w[k] for k in _SMALL]
    gs = [jnp.stack([grads[l][k] for l in range(DEPTH)]) if k != "final_norm_w" else dfw for k in _SMALL]
    packed = _pack(gs)
    allg = _gather8([packed], split=False, name="gather_small_grads")[0]
    gsum = _sum_slots(allg.reshape(8, packed.shape[0], _PACK_COLS), name="sum_small_grads")
    for k, t in zip(_SMALL, _unpack(gsum, small_like)):
        gfull[k] = t

    delta, new_m, new_v = {}, {}, {}
    for k in _BIG:
        delta[k], new_m[k], new_v[k] = _adamw(w[k], gfull[k], m[k], v[k], name=f"adamw_{k}")
    ds, ms, vs = _adamw(_pack(small_like), gsum, _pack([m[k] for k in _SMALL]), _pack([v[k] for k in _SMALL]),
                        name="adamw_small")
    for k, a, b, c in zip(_SMALL, _unpack(ds, small_like), _unpack(ms, small_like), _unpack(vs, small_like)):
        delta[k], new_m[k], new_v[k] = a, b, c

    return (loss, dx[None], *[gfull[k] for k in _WEIGHTS], *[delta[k] for k in _WEIGHTS],
            *[new_m[k] for k in _WEIGHTS], *[new_v[k] for k in _WEIGHTS])
```

```python
import functools
import math

import numpy as np
import jax
import jax.numpy as jnp
from jax import lax
from jax.experimental import pallas as pl
from jax.experimental.pallas import tpu as pltpu

F32 = jnp.float32
MXU = jnp.bfloat16
HIGHEST = lax.Precision.HIGHEST

D_MODEL = 2048
DEPTH = 2
EPS = 1e-6
NEG_INF = -1e30
SSM_WIDTH = 1024
SSM_GROUP = 16
SSM_GROUPS = 64
SSM_STATE = 64
SSM_CH = SSM_GROUPS * SSM_STATE
SLAB = 128
SLAB_CH = (SLAB // SSM_GROUP) * SSM_STATE
N_SLAB = SSM_WIDTH // SLAB
SCAN_SEG = 8
SG_HEADS = 8
SG_CHUNK = 128
HEAD_DIM = 64
ATT_HEADS = 16
ATT_KV_HEADS = 2
GQA_GROUP = 8
ATT_BLOCK = 128
WINDOW = 128
ROT_DIM = 16
ROPE_THETA = 500000.0
N_MAIN = 6400
N_ZC = 1024
N_GATES = 6144
D_IN = N_MAIN + N_ZC + N_GATES

ADAM_LR = 0.001
ADAM_B1 = 0.9
ADAM_B2 = 0.999
ADAM_EPS = 1e-08
ADAM_WD = 0.01
ADAM_STEP = 10

_DIMS = {"nn": (((1,), (0,)), ((), ())), "nt": (((1,), (1,)), ((), ())), "tn": (((0,), (0,)), ((), ()))}
_MB = 1024 * 1024


def _cp(sem, vmem_mb=48):
    return pltpu.CompilerParams(dimension_semantics=sem, vmem_limit_bytes=vmem_mb * _MB)


def _dot(a, b, mode):
    return lax.dot_general(a.astype(MXU), b.astype(MXU), _DIMS[mode], preferred_element_type=F32)


@jax.custom_vjp
def _mm_nn(a, b):
    return _dot(a, b, "nn")


def _mm_nn_fwd(a, b):
    return _dot(a, b, "nn"), (a, b)


def _mm_nn_bwd(res, g):
    a, b = res
    return _dot(g, b, "nt"), _dot(a, g, "tn")


_mm_nn.defvjp(_mm_nn_fwd, _mm_nn_bwd)


@jax.custom_vjp
def _mm_nt(a, bt):
    return _dot(a, bt, "nt")


def _mm_nt_fwd(a, bt):
    return _dot(a, bt, "nt"), (a, bt)


def _mm_nt_bwd(res, g):
    a, bt = res
    return _dot(g, bt, "nn"), _dot(g, a, "tn")


_mm_nt.defvjp(_mm_nt_fwd, _mm_nt_bwd)


def _rmsnorm(x, w):
    return x * lax.rsqrt(jnp.mean(x * x, axis=-1, keepdims=True) + EPS) * w


def _layernorm(x, w, b):
    mu = jnp.mean(x, axis=-1, keepdims=True)
    var = jnp.mean(jnp.square(x - mu), axis=-1, keepdims=True)
    return (x - mu) * lax.rsqrt(var + EPS) * w + b


def _silu(x):
    return x * jax.nn.sigmoid(x)


def _matmul(a, b, mode, *, name, shape, tm, tn, tk, out_dtype=F32, add=None,
            a_off=(0, 0), b_off=(0, 0), out=None, out_off=(0, 0)):
    m, n, k = shape
    tm, tn, tk = min(tm, m), min(tn, n), min(tk, k)
    assert m % tm == 0 and n % tn == 0 and k % tk == 0, (name, shape, tm, tn, tk)
    nk = k // tk
    has_add, has_out = add is not None, out is not None

    def body(*refs):
        a_ref, b_ref = refs[0], refs[1]
        pos = 2
        add_ref = None
        if has_add:
            add_ref = refs[pos]
            pos += 1
        if has_out:
            pos += 1
        o_ref = refs[pos]
        p = _dot(a_ref[...], b_ref[...], mode)
        if nk == 1:
            if has_add:
                p = p + add_ref[...].astype(F32)
            o_ref[...] = p.astype(out_dtype)
            return
        acc_ref = refs[pos + 1]
        kk = pl.program_id(2)

        @pl.when(kk == 0)
        def _():
            acc_ref[...] = p

        @pl.when(kk > 0)
        def _():
            acc_ref[...] += p

        @pl.when(kk == nk - 1)
        def _():
            r = acc_ref[...]
            if has_add:
                r = r + add_ref[...].astype(F32)
            o_ref[...] = r.astype(out_dtype)

    a0, a1 = a_off
    b0, b1 = b_off
    o0, o1 = out_off
    if mode == "tn":
        a_spec = pl.BlockSpec((tk, tm), lambda i, j, kk: (kk + a0, i + a1))
    else:
        a_spec = pl.BlockSpec((tm, tk), lambda i, j, kk: (i + a0, kk + a1))
    if mode == "nt":
        b_spec = pl.BlockSpec((tn, tk), lambda i, j, kk: (j + b0, kk + b1))
    else:
        b_spec = pl.BlockSpec((tk, tn), lambda i, j, kk: (kk + b0, j + b1))
    in_specs = [a_spec, b_spec]
    args = [a, b]
    if has_add:
        in_specs.append(pl.BlockSpec((tm, tn), lambda i, j, kk: (i, j)))
        args.append(add)
    aliases = {}
    if has_out:
        aliases = {len(args): 0}
        in_specs.append(pl.BlockSpec(memory_space=pl.ANY))
        args.append(out)
        out_shape = jax.ShapeDtypeStruct(out.shape, out.dtype)
        assert out.dtype == out_dtype
    else:
        out_shape = jax.ShapeDtypeStruct((m, n), out_dtype)
    return pl.pallas_call(
        body, name=name, grid=(m // tm, n // tn, nk),
        in_specs=in_specs,
        out_specs=pl.BlockSpec((tm, tn), lambda i, j, kk: (i + o0, j + o1)),
        out_shape=out_shape,
        scratch_shapes=[pltpu.VMEM((tm, tn), F32)] if nk > 1 else [],
        input_output_aliases=aliases,
        compiler_params=_cp(("parallel", "parallel", "arbitrary")),
    )(*args)


def _rms_fwd(x, w, *, name, tm=256):
    L, d = x.shape
    tm = min(tm, L)

    def body(x_ref, w_ref, h_ref):
        h_ref[...] = _rmsnorm(x_ref[...], w_ref[...]).astype(MXU)

    return pl.pallas_call(
        body, name=name, grid=(L // tm,),
        in_specs=[pl.BlockSpec((tm, d), lambda i: (i, 0)), pl.BlockSpec((1, d), lambda i: (0, 0))],
        out_specs=pl.BlockSpec((tm, d), lambda i: (i, 0)),
        out_shape=jax.ShapeDtypeStruct((L, d), MXU),
        compiler_params=_cp(("parallel",)),
    )(x, w.reshape(1, d))


def _rms_bwd(x, w, dh, dxn, *, name, tm=256):
    L, d = x.shape
    tm = min(tm, L)

    def body(x_ref, w_ref, dh_ref, dxn_ref, dx_ref, dw_ref):
        _, vjp = jax.vjp(_rmsnorm, x_ref[...], w_ref[...])
        dx, dw = vjp(dh_ref[...])
        dx_ref[...] = dx + dxn_ref[...]

        @pl.when(pl.program_id(0) == 0)
        def _():
            dw_ref[...] = jnp.zeros_like(dw_ref)

        dw_ref[...] += dw

    row = pl.BlockSpec((tm, d), lambda i: (i, 0))
    vec = pl.BlockSpec((1, d), lambda i: (0, 0))
    dx, dw = pl.pallas_call(
        body, name=name, grid=(L // tm,),
        in_specs=[row, vec, row, row], out_specs=[row, vec],
        out_shape=[jax.ShapeDtypeStruct((L, d), F32), jax.ShapeDtypeStruct((1, d), F32)],
        compiler_params=_cp(("arbitrary",)),
    )(x, w.reshape(1, d), dh, dxn)
    return dx, dw.reshape(d)


def _final_loss(x, w, tgt, *, name, tm=256):
    L, d = x.shape
    tm = min(tm, L)

    def loss_fn(xv, wv, tv):
        err = jnp.square(_rmsnorm(xv, wv) - tv)
        return 0.5 * jnp.sum(jnp.mean(err, axis=-1, keepdims=True), axis=0, keepdims=True)

    def body(x_ref, w_ref, t_ref, loss_ref, dx_ref, dw_ref):
        tv = t_ref[...]
        val, vjp = jax.vjp(lambda xv, wv: loss_fn(xv, wv, tv), x_ref[...], w_ref[...])
        dx, dw = vjp(jnp.ones((1, 1), F32))
        dx_ref[...] = dx

        @pl.when(pl.program_id(0) == 0)
        def _():
            dw_ref[...] = jnp.zeros_like(dw_ref)
            loss_ref[...] = jnp.zeros_like(loss_ref)

        dw_ref[...] += dw
        loss_ref[...] += jnp.broadcast_to(val, loss_ref.shape)

    row = pl.BlockSpec((tm, d), lambda i: (i, 0))
    vec = pl.BlockSpec((1, d), lambda i: (0, 0))
    loss, dx, dw = pl.pallas_call(
        body, name=name, grid=(L // tm,),
        in_specs=[row, vec, row],
        out_specs=[pl.BlockSpec((8, 128), lambda i: (0, 0)), row, vec],
        out_shape=[jax.ShapeDtypeStruct((8, 128), F32), jax.ShapeDtypeStruct((L, d), F32),
                   jax.ShapeDtypeStruct((1, d), F32)],
        compiler_params=_cp(("arbitrary",)),
    )(x, w.reshape(1, d), tgt)
    return loss[0, 0], dx, dw.reshape(d)


PARAM_ROWS = 512


def _s5_param_fn(are, aim, ldt, bre, bim, row0):
    n = are.shape[0]
    grp = (row0 + lax.broadcasted_iota(jnp.int32, (n, SSM_GROUPS), 0)) // SSM_STATE
    col = lax.broadcasted_iota(jnp.int32, (n, SSM_GROUPS), 1)
    sel = (grp == col).astype(F32)
    dt = jnp.sum(sel * jnp.exp(ldt), axis=-1, keepdims=True)
    mag = jnp.exp(are * dt)
    ang = aim * dt
    lbr = mag * jnp.cos(ang)
    lbi = mag * jnp.sin(ang)
    den = are * are + aim * aim
    nr = lbr - 1.0
    kr = (nr * are + lbi * aim) / den
    ki = (lbi * are - nr * aim) / den
    return lbr, lbi, kr * bre - ki * bim, kr * bim + ki * bre


def _s5_param_specs():
    col = pl.BlockSpec((PARAM_ROWS, 1), lambda i: (i, 0))
    mat = pl.BlockSpec((PARAM_ROWS, SSM_GROUP), lambda i: (i, 0))
    vec = pl.BlockSpec((1, SSM_GROUPS), lambda i: (0, 0))
    return col, mat, vec


def _s5_params_fwd(are, aim, ldt, bre, bim, *, name):
    n = are.shape[0]
    col, mat, vec = _s5_param_specs()

    def body(are_ref, aim_ref, ldt_ref, bre_ref, bim_ref, lbr_ref, lbi_ref, bbr_ref, bbi_ref):
        row0 = pl.program_id(0) * PARAM_ROWS
        lbr, lbi, bbr, bbi = _s5_param_fn(are_ref[...], aim_ref[...], ldt_ref[...], bre_ref[...], bim_ref[...], row0)
        lbr_ref[...] = lbr
        lbi_ref[...] = lbi
        bbr_ref[...] = bbr
        bbi_ref[...] = bbi

    cshape = jax.ShapeDtypeStruct((n, 1), F32)
    mshape = jax.ShapeDtypeStruct((n, SSM_GROUP), F32)
    return pl.pallas_call(body, name=name, grid=(n // PARAM_ROWS,),
                          in_specs=[col, col, vec, mat, mat], out_specs=[col, col, mat, mat],
                          out_shape=[cshape, cshape, mshape, mshape],
                          compiler_params=_cp(("parallel",)))(are, aim, ldt, bre, bim)


def _s5_params_bwd(are, aim, ldt, bre, bim, dlbr, dlbi, dbbr, dbbi, *, name):
    n = are.shape[0]
    col, mat, vec = _s5_param_specs()

    def body(are_ref, aim_ref, ldt_ref, bre_ref, bim_ref, g0, g1, g2, g3, o0, o1, o2, o3, o4):
        row0 = pl.program_id(0) * PARAM_ROWS
        _, vjp = jax.vjp(lambda a, b, c, d, e: _s5_param_fn(a, b, c, d, e, row0),
                         are_ref[...], aim_ref[...], ldt_ref[...], bre_ref[...], bim_ref[...])
        dare, daim, dldt, dbre, dbim = vjp((g0[...], g1[...], g2[...], g3[...]))
        o0[...] = dare
        o1[...] = daim
        o3[...] = dbre
        o4[...] = dbim

        @pl.when(pl.program_id(0) == 0)
        def _():
            o2[...] = jnp.zeros_like(o2)

        o2[...] += dldt

    cshape = jax.ShapeDtypeStruct((n, 1), F32)
    mshape = jax.ShapeDtypeStruct((n, SSM_GROUP), F32)
    return pl.pallas_call(body, name=name, grid=(n // PARAM_ROWS,),
                          in_specs=[col, col, vec, mat, mat, col, col, mat, mat],
                          out_specs=[col, col, vec, mat, mat],
                          out_shape=[cshape, cshape, jax.ShapeDtypeStruct((1, SSM_GROUPS), F32), mshape, mshape],
                          compiler_params=_cp(("arbitrary",)))(are, aim, ldt, bre, bim, dlbr, dlbi, dbbr, dbbi)


SLAB_NC = SLAB_CH // 128


def _s5_specs(L):
    slab = pl.BlockSpec((L, SLAB), lambda s: (0, s))
    wspec = pl.BlockSpec((SLAB_NC, 128, SLAB), lambda s: (s, 0, 0))
    lspec = pl.BlockSpec((SLAB_NC, 1, 128), lambda s: (s, 0, 0))
    sspec = pl.BlockSpec((SLAB_NC, L, 128), lambda s: (s, 0, 0))
    dspec = pl.BlockSpec((1, SLAB), lambda s: (0, s))
    return slab, wspec, lspec, sspec, dspec


def _scan_inplace(sr_ref, si_ref, lr, li, pr_ref, pi_ref, *, reverse):
    NC, L, W = sr_ref.shape
    T = L // SCAN_SEG
    lr8 = [jnp.broadcast_to(lr[k], (SCAN_SEG, W)) for k in range(NC)]
    li8 = [jnp.broadcast_to(li[k], (SCAN_SEG, W)) for k in range(NC)]

    def step(j, carry):
        row = (T - 1 - j) if reverse else j
        idx = pl.ds(row, SCAN_SEG, stride=T)
        out = []
        for k in range(NC):
            sr, si, qr, qi = carry[k]
            nsr = lr8[k] * sr - li8[k] * si + sr_ref[k, idx, :]
            nsi = lr8[k] * si + li8[k] * sr + si_ref[k, idx, :]
            sr_ref[k, idx, :] = nsr
            si_ref[k, idx, :] = nsi
            pr_ref[k, pl.ds(row, 1), :] = qr
            pi_ref[k, pl.ds(row, 1), :] = qi
            out.append((nsr, nsi, qr * lr[k] - qi * li[k], qr * li[k] + qi * lr[k]))
        return tuple(out)

    zero = jnp.zeros((SCAN_SEG, W), F32)
    ends = lax.fori_loop(0, T, step, tuple((zero, zero, lr[k], li[k]) for k in range(NC)))
    last = 0 if reverse else T - 1
    order = range(SCAN_SEG - 1, -1, -1) if reverse else range(SCAN_SEG)
    for k in range(NC):
        er, ei = ends[k][0], ends[k][1]
        ltr = pr_ref[k, pl.ds(last, 1), :]
        lti = pi_ref[k, pl.ds(last, 1), :]
        cr = jnp.zeros((1, W), F32)
        ci = jnp.zeros((1, W), F32)
        for n, seg in enumerate(order):
            if n > 0:
                rows = pl.ds(seg * T, T)
                pr = pr_ref[k]
                pi = pi_ref[k]
                sr_ref[k, rows, :] += pr * cr - pi * ci
                si_ref[k, rows, :] += pr * ci + pi * cr
            if n < SCAN_SEG - 1:
                ncr = er[seg:seg + 1, :] + ltr * cr - lti * ci
                nci = ei[seg:seg + 1, :] + ltr * ci + lti * cr
                cr, ci = ncr, nci


def _s5_fwd(main, btr, bti, cbr, cbi, lbr, lbi, dvec, *, name):
    L = main.shape[0]
    T = L // SCAN_SEG

    def body(u_ref, btr_ref, bti_ref, cbr_ref, cbi_ref, lr_ref, li_ref, d_ref, ys_ref, sr_ref, si_ref, pr_ref, pi_ref):
        u = u_ref[...]
        for k in range(SLAB_NC):
            sr_ref[k] = _dot(u, btr_ref[k], "nt")
            si_ref[k] = _dot(u, bti_ref[k], "nt")
        _scan_inplace(sr_ref, si_ref, lr_ref[...], li_ref[...], pr_ref, pi_ref, reverse=False)
        ys = d_ref[...] * u
        for k in range(SLAB_NC):
            ys = ys + _dot(sr_ref[k], cbr_ref[k], "nn") - _dot(si_ref[k], cbi_ref[k], "nn")
        ys_ref[...] = ys

    slab, wspec, lspec, sspec, dspec = _s5_specs(L)
    sshape = jax.ShapeDtypeStruct((N_SLAB * SLAB_NC, L, 128), F32)
    return pl.pallas_call(
        body, name=name, grid=(N_SLAB,),
        in_specs=[slab, wspec, wspec, wspec, wspec, lspec, lspec, dspec],
        out_specs=[slab, sspec, sspec],
        out_shape=[jax.ShapeDtypeStruct((L, SSM_WIDTH), F32), sshape, sshape],
        scratch_shapes=[pltpu.VMEM((SLAB_NC, T, 128), F32), pltpu.VMEM((SLAB_NC, T, 128), F32)],
        compiler_params=_cp(("parallel",), 56),
    )(main, btr, bti, cbr, cbi, lbr, lbi, dvec)


def _s5_bwd(dys, main, sr, si, btr, bti, cbr, cbi, lbr, lbi, dvec, *, name):
    L = main.shape[0]
    T = L // SCAN_SEG

    def body(dys_ref, u_ref, sr_ref, si_ref, btr_ref, bti_ref, cbr_ref, cbi_ref, lr_ref, li_ref, d_ref,
             du_ref, dbtr_ref, dbti_ref, dcbr_ref, dcbi_ref, dlr_ref, dli_ref, dd_ref,
             ar_ref, ai_ref, pr_ref, pi_ref):
        dys = dys_ref[...]
        u = u_ref[...]
        for k in range(SLAB_NC):
            ar_ref[k] = _dot(dys, cbr_ref[k], "nt")
            ai_ref[k] = -_dot(dys, cbi_ref[k], "nt")
        _scan_inplace(ar_ref, ai_ref, lr_ref[...], -li_ref[...], pr_ref, pi_ref, reverse=True)
        first = lax.broadcasted_iota(jnp.int32, (L, 1), 0) == 0
        du = d_ref[...] * dys
        for k in range(SLAB_NC):
            a_re = ar_ref[k]
            a_im = ai_ref[k]
            du = du + _dot(a_re, btr_ref[k], "nn") + _dot(a_im, bti_ref[k], "nn")
            dbtr_ref[k] = _dot(a_re, u, "tn")
            dbti_ref[k] = _dot(a_im, u, "tn")
            s_re = sr_ref[k]
            s_im = si_ref[k]
            dcbr_ref[k] = _dot(s_re, dys, "tn")
            dcbi_ref[k] = -_dot(s_im, dys, "tn")
            p_re = jnp.where(first, 0.0, pltpu.roll(s_re, 1, 0))
            p_im = jnp.where(first, 0.0, pltpu.roll(s_im, 1, 0))
            dlr_ref[k] = jnp.sum(p_re * a_re + p_im * a_im, axis=0, keepdims=True)
            dli_ref[k] = jnp.sum(p_re * a_im - p_im * a_re, axis=0, keepdims=True)
        du_ref[...] = du
        dd_ref[...] = jnp.sum(dys * u, axis=0, keepdims=True)

    slab, wspec, lspec, sspec, dspec = _s5_specs(L)
    wshape = jax.ShapeDtypeStruct((N_SLAB * SLAB_NC, 128, SLAB), F32)
    lshape = jax.ShapeDtypeStruct((N_SLAB * SLAB_NC, 1, 128), F32)
    return pl.pallas_call(
        body, name=name, grid=(N_SLAB,),
        in_specs=[slab, slab, sspec, sspec, wspec, wspec, wspec, wspec, lspec, lspec, dspec],
        out_specs=[slab, wspec, wspec, wspec, wspec, lspec, lspec, dspec],
        out_shape=[jax.ShapeDtypeStruct((L, SSM_WIDTH), F32), wshape, wshape, wshape, wshape, lshape, lshape,
                   jax.ShapeDtypeStruct((1, SSM_WIDTH), F32)],
        scratch_shapes=[pltpu.VMEM((SLAB_NC, L, 128), F32), pltpu.VMEM((SLAB_NC, L, 128), F32),
                        pltpu.VMEM((SLAB_NC, T, 128), F32), pltpu.VMEM((SLAB_NC, T, 128), F32)],
        compiler_params=_cp(("parallel",), 56),
    )(dys, main, sr, si, btr, bti, cbr, cbi, lbr, lbi, dvec)


_SLAB_MASK = (np.arange(SLAB_CH)[:, None] // SSM_STATE == np.arange(SLAB)[None, :] // SSM_GROUP)


def _expand_bd(x):
    t = jnp.tile(x.reshape(N_SLAB, SLAB_CH, SSM_GROUP), (1, 1, SLAB // SSM_GROUP))
    return jnp.where(_SLAB_MASK[None], t, 0.0).astype(MXU).reshape(N_SLAB * SLAB_NC, 128, SLAB)


def _contract_bd(dx):
    t = jnp.where(_SLAB_MASK[None], dx.reshape(N_SLAB, SLAB_CH, SLAB), 0.0)
    return jnp.sum(t.reshape(N_SLAB, SLAB_CH, SLAB // SSM_GROUP, SSM_GROUP), axis=2).reshape(SSM_CH, SSM_GROUP)


def _glu_ew(ys, zlin, za):
    a1 = jax.nn.gelu(ys)
    return a1 * jax.nn.sigmoid(zlin) * _silu(za)


def _glu_fwd(ys, main, gw, gb, *, name, tm=256):
    L = ys.shape[0]
    tm = min(tm, L)
    W = SSM_WIDTH

    def body(ys_ref, za_ref, gw_ref, gb_ref, ya_ref):
        ys = ys_ref[...]
        a1 = jax.nn.gelu(ys)
        zlin = _dot(a1, gw_ref[...], "nn") + gb_ref[...]
        ya_ref[...] = _glu_ew(ys, zlin, za_ref[...]).astype(MXU)

    return pl.pallas_call(
        body, name=name, grid=(L // tm,),
        in_specs=[pl.BlockSpec((tm, W), lambda i: (i, 0)), pl.BlockSpec((tm, W), lambda i: (i, 1)),
                  pl.BlockSpec((W, W), lambda i: (0, 0)), pl.BlockSpec((1, W), lambda i: (0, 0))],
        out_specs=pl.BlockSpec((tm, W), lambda i: (i, 0)),
        out_shape=jax.ShapeDtypeStruct((L, W), MXU),
        compiler_params=_cp(("parallel",)),
    )(ys, main, gw, gb.reshape(1, W))


def _glu_bwd(dya, ys, main, gw, gb, *, name, tm=256):
    L = ys.shape[0]
    tm = min(tm, L)
    W = SSM_WIDTH

    def body(dya_ref, ys_ref, za_ref, gw_ref, gb_ref, dys_ref, dza_ref, a1_ref, dzl_ref, db_ref):
        ys = ys_ref[...]
        a1, gelu_vjp = jax.vjp(jax.nn.gelu, ys)
        zlin = _dot(a1, gw_ref[...], "nn") + gb_ref[...]
        _, vjp = jax.vjp(lambda a, z, za: a * jax.nn.sigmoid(z) * _silu(za), a1, zlin, za_ref[...])
        da1, dzlin, dza = vjp(dya_ref[...].astype(F32))
        da1 = da1 + _dot(dzlin, gw_ref[...], "nt")
        dys_ref[...] = gelu_vjp(da1)[0]
        dza_ref[...] = dza
        a1_ref[...] = a1.astype(MXU)
        dzl_ref[...] = dzlin.astype(MXU)

        @pl.when(pl.program_id(0) == 0)
        def _():
            db_ref[...] = jnp.zeros_like(db_ref)

        db_ref[...] += jnp.sum(dzlin, axis=0, keepdims=True)

    row = pl.BlockSpec((tm, W), lambda i: (i, 0))
    vec = pl.BlockSpec((1, W), lambda i: (0, 0))
    return pl.pallas_call(
        body, name=name, grid=(L // tm,),
        in_specs=[row, row, pl.BlockSpec((tm, W), lambda i: (i, 1)), pl.BlockSpec((W, W), lambda i: (0, 0)), vec],
        out_specs=[row, row, row, row, vec],
        out_shape=[jax.ShapeDtypeStruct((L, W), F32), jax.ShapeDtypeStruct((L, W), F32),
                   jax.ShapeDtypeStruct((L, W), MXU), jax.ShapeDtypeStruct((L, W), MXU),
                   jax.ShapeDtypeStruct((1, W), F32)],
        compiler_params=_cp(("arbitrary",)),
    )(dya, ys, main, gw, gb.reshape(1, W))


def _sg_fn(ub, vb, zb, lnw, lnb, ws, bs):
    u = jax.nn.gelu(ub)
    v = _layernorm(jax.nn.gelu(vb), lnw, lnb)
    r = lax.broadcasted_iota(jnp.int32, (SG_CHUNK, SG_CHUNK), 0)
    c = lax.broadcasted_iota(jnp.int32, (SG_CHUNK, SG_CHUNK), 1)
    tri = r >= c
    outs = []
    for h in range(SG_HEADS):
        wh = jnp.where(tri, ws[h], 0.0)
        outs.append(_mm_nn(wh, v[:, h * 128:(h + 1) * 128]) + bs[h])
    mixed = jnp.concatenate(outs, axis=1)
    return u * mixed * _silu(zb)


def _sg_specs(L):
    W = SSM_WIDTH
    blk = lambda c: pl.BlockSpec((SG_CHUNK, W), lambda i, c=c: (i, c))
    vec = pl.BlockSpec((1, W), lambda i: (0, 0))
    wspec = pl.BlockSpec((SG_HEADS, SG_CHUNK, SG_CHUNK), lambda i: (0, 0, 0))
    bspec = pl.BlockSpec((SG_HEADS, SG_CHUNK, 1), lambda i: (0, 0, 0))
    return blk, vec, wspec, bspec


def _sg_fwd(main, lnw, lnb, sgw, sgb, *, name):
    L = main.shape[0]
    W = SSM_WIDTH
    blk, vec, wspec, bspec = _sg_specs(L)

    def body(ub_ref, vb_ref, zb_ref, lnw_ref, lnb_ref, w_ref, b_ref, yb_ref):
        ws = [w_ref[h] for h in range(SG_HEADS)]
        bs = [b_ref[h] for h in range(SG_HEADS)]
        yb_ref[...] = _sg_fn(ub_ref[...], vb_ref[...], zb_ref[...], lnw_ref[...], lnb_ref[...], ws, bs).astype(MXU)

    return pl.pallas_call(
        body, name=name, grid=(L // SG_CHUNK,),
        in_specs=[blk(2), blk(3), blk(4), vec, vec, wspec, bspec],
        out_specs=pl.BlockSpec((SG_CHUNK, W), lambda i: (i, 0)),
        out_shape=jax.ShapeDtypeStruct((L, W), MXU),
        compiler_params=_cp(("parallel",)),
    )(main, main, main, lnw.reshape(1, W), lnb.reshape(1, W), sgw, sgb.reshape(SG_HEADS, SG_CHUNK, 1))


def _sg_bwd(dyb, main, lnw, lnb, sgw, sgb, *, name):
    L = main.shape[0]
    W = SSM_WIDTH
    blk, vec, wspec, bspec = _sg_specs(L)

    def body(dyb_ref, ub_ref, vb_ref, zb_ref, lnw_ref, lnb_ref, w_ref, b_ref,
             dub_ref, dvb_ref, dzb_ref, dlnw_ref, dlnb_ref, dw_ref, db_ref):
        ws = [w_ref[h] for h in range(SG_HEADS)]
        bs = [b_ref[h] for h in range(SG_HEADS)]
        _, vjp = jax.vjp(_sg_fn, ub_ref[...], vb_ref[...], zb_ref[...], lnw_ref[...], lnb_ref[...], ws, bs)
        dub, dvb, dzb, dlnw, dlnb, dws, dbs = vjp(dyb_ref[...])

        @pl.when(pl.program_id(0) == 0)
        def _():
            dlnw_ref[...] = jnp.zeros_like(dlnw_ref)
            dlnb_ref[...] = jnp.zeros_like(dlnb_ref)
            dw_ref[...] = jnp.zeros_like(dw_ref)
            db_ref[...] = jnp.zeros_like(db_ref)

        dub_ref[...] = dub
        dvb_ref[...] = dvb
        dzb_ref[...] = dzb
        dlnw_ref[...] += dlnw
        dlnb_ref[...] += dlnb
        for h in range(SG_HEADS):
            dw_ref[h] += dws[h]
            db_ref[h] += dbs[h]

    row = pl.BlockSpec((SG_CHUNK, W), lambda i: (i, 0))
    out = jax.ShapeDtypeStruct((L, W), F32)
    return pl.pallas_call(
        body, name=name, grid=(L // SG_CHUNK,),
        in_specs=[row, blk(2), blk(3), blk(4), vec, vec, wspec, bspec],
        out_specs=[row, row, row, vec, vec, wspec, bspec],
        out_shape=[out, out, out, jax.ShapeDtypeStruct((1, W), F32), jax.ShapeDtypeStruct((1, W), F32),
                   jax.ShapeDtypeStruct((SG_HEADS, SG_CHUNK, SG_CHUNK), F32),
                   jax.ShapeDtypeStruct((SG_HEADS, SG_CHUNK, 1), F32)],
        compiler_params=_cp(("arbitrary",)),
    )(dyb, main, main, main, lnw.reshape(1, W), lnb.reshape(1, W), sgw, sgb.reshape(SG_HEADS, SG_CHUNK, 1))


def _rope_tables(L):
    half = ROT_DIM // 2
    inv_freq = ROPE_THETA ** (-jnp.arange(0, ROT_DIM, 2, dtype=F32) / ROT_DIM)
    ang = jnp.arange(L, dtype=F32)[:, None] * inv_freq[None, :]
    cos = jnp.cos(ang)
    sin = jnp.sin(ang)
    ones = jnp.ones((L, HEAD_DIM - ROT_DIM), F32)
    cosf = jnp.concatenate([cos, cos, ones], axis=1)
    sinf = jnp.concatenate([sin, sin, 0.0 * ones], axis=1)
    rot = np.zeros((HEAD_DIM, HEAD_DIM), np.float32)
    for d in range(half):
        rot[d + half, d] = -1.0
        rot[d, d + half] = 1.0
    return cosf, sinf, jnp.asarray(rot)


def _rope(t, cosf, sinf, rot):
    shp = t.shape
    t2 = t.reshape(-1, HEAD_DIM)
    sw = lax.dot_general(t2, rot, _DIMS["nn"], precision=HIGHEST, preferred_element_type=F32).reshape(shp)
    return t * cosf + sw * sinf


def _attn_block_fn(q, kw, vw, sinks, cq, sq, ck, sk, rot, q0, k0):
    nk = kw.shape[1]
    qr = _rope(q, cq, sq, rot)
    kr = _rope(kw, ck, sk, rot)
    qpos = q0 + lax.broadcasted_iota(jnp.int32, (1, ATT_BLOCK, nk), 1)
    kpos = k0 + lax.broadcasted_iota(jnp.int32, (1, ATT_BLOCK, nk), 2)
    diff = qpos - kpos
    allowed = (diff >= 0) & (diff < WINDOW)
    outs = []
    for kh in range(ATT_KV_HEADS):
        qh = qr[kh * GQA_GROUP:(kh + 1) * GQA_GROUP].reshape(GQA_GROUP * ATT_BLOCK, HEAD_DIM)
        s = _mm_nt(qh, kr[kh]).reshape(GQA_GROUP, ATT_BLOCK, nk) * (HEAD_DIM ** -0.5)
        s = jnp.where(allowed, s, NEG_INF)
        sink = sinks[kh * GQA_GROUP:(kh + 1) * GQA_GROUP]
        m = lax.stop_gradient(jnp.maximum(jnp.max(s, axis=-1, keepdims=True), sink))
        e = jnp.exp(s - m)
        p = e / (jnp.sum(e, axis=-1, keepdims=True) + jnp.exp(sink - m))
        o = _mm_nn(p.reshape(GQA_GROUP * ATT_BLOCK, nk), vw[kh])
        outs.append(o.reshape(GQA_GROUP, ATT_BLOCK, HEAD_DIM))
    return jnp.concatenate(outs, axis=0)


def _attn_common(L):
    nwin = min(2 * ATT_BLOCK, L)
    qspec = pl.BlockSpec((ATT_HEADS, ATT_BLOCK, HEAD_DIM), lambda n: (0, n, 0))
    kvspec = pl.BlockSpec((ATT_KV_HEADS, L, HEAD_DIM), lambda n: (0, 0, 0))
    sspec = pl.BlockSpec((ATT_HEADS, 1, 1), lambda n: (0, 0, 0))
    tq = pl.BlockSpec((ATT_BLOCK, HEAD_DIM), lambda n: (n, 0))
    tk = pl.BlockSpec((L, HEAD_DIM), lambda n: (0, 0))
    rspec = pl.BlockSpec((HEAD_DIM, HEAD_DIM), lambda n: (0, 0))
    return nwin, qspec, kvspec, sspec, tq, tk, rspec


def _attn_fwd(qh, kh, vh, sinks, cosf, sinf, rot, *, name):
    L = qh.shape[1]
    nwin, qspec, kvspec, sspec, tq, tk, rspec = _attn_common(L)

    def body(q_ref, k_ref, v_ref, s_ref, cq_ref, sq_ref, ck_ref, sk_ref, r_ref, o_ref):
        n = pl.program_id(0)
        k0 = pl.multiple_of(jnp.maximum(n - 1, 0) * ATT_BLOCK, ATT_BLOCK)
        win = pl.ds(k0, nwin)
        o_ref[...] = _attn_block_fn(q_ref[...], k_ref[:, win, :], v_ref[:, win, :], s_ref[...],
                                    cq_ref[...], sq_ref[...], ck_ref[win, :], sk_ref[win, :], r_ref[...],
                                    n * ATT_BLOCK, k0)

    return pl.pallas_call(
        body, name=name, grid=(L // ATT_BLOCK,),
        in_specs=[qspec, kvspec, kvspec, sspec, tq, tq, tk, tk, rspec],
        out_specs=qspec,
        out_shape=jax.ShapeDtypeStruct((ATT_HEADS, L, HEAD_DIM), F32),
        compiler_params=_cp(("parallel",)),
    )(qh, kh, vh, sinks.reshape(ATT_HEADS, 1, 1), cosf, sinf, cosf, sinf, rot)


def _attn_bwd(do, qh, kh, vh, sinks, cosf, sinf, rot, *, name):
    L = qh.shape[1]
    nwin, qspec, kvspec, sspec, tq, tk, rspec = _attn_common(L)

    def body(do_ref, q_ref, k_ref, v_ref, s_ref, cq_ref, sq_ref, ck_ref, sk_ref, r_ref,
             dq_ref, dk_ref, dv_ref, ds_ref):
        n = pl.program_id(0)
        k0 = pl.multiple_of(jnp.maximum(n - 1, 0) * ATT_BLOCK, ATT_BLOCK)
        win = pl.ds(k0, nwin)
        cq, sq, ck, sk, rt = cq_ref[...], sq_ref[...], ck_ref[win, :], sk_ref[win, :], r_ref[...]
        q0 = n * ATT_BLOCK
        _, vjp = jax.vjp(lambda q, kw, vw, s: _attn_block_fn(q, kw, vw, s, cq, sq, ck, sk, rt, q0, k0),
                         q_ref[...], k_ref[:, win, :], v_ref[:, win, :], s_ref[...])
        dq, dkw, dvw, ds = vjp(do_ref[...])

        @pl.when(n == 0)
        def _():
            dk_ref[...] = jnp.zeros_like(dk_ref)
            dv_ref[...] = jnp.zeros_like(dv_ref)
            ds_ref[...] = jnp.zeros_like(ds_ref)

        dq_ref[...] = dq
        dk_ref[:, win, :] += dkw
        dv_ref[:, win, :] += dvw
        ds_ref[...] += ds

    return pl.pallas_call(
        body, name=name, grid=(L // ATT_BLOCK,),
        in_specs=[qspec, qspec, kvspec, kvspec, sspec, tq, tq, tk, tk, rspec],
        out_specs=[qspec, kvspec, kvspec, sspec],
        out_shape=[jax.ShapeDtypeStruct((ATT_HEADS, L, HEAD_DIM), F32),
                   jax.ShapeDtypeStruct((ATT_KV_HEADS, L, HEAD_DIM), F32),
                   jax.ShapeDtypeStruct((ATT_KV_HEADS, L, HEAD_DIM), F32),
                   jax.ShapeDtypeStruct((ATT_HEADS, 1, 1), F32)],
        compiler_params=_cp(("arbitrary",)),
    )(do, qh, kh, vh, sinks.reshape(ATT_HEADS, 1, 1), cosf, sinf, cosf, sinf, rot)


def _to_heads(t, nh):
    L = t.shape[0]
    return t.reshape(L, nh, HEAD_DIM).transpose(1, 0, 2)


def _from_heads(t):
    nh, L, _ = t.shape
    return t.transpose(1, 0, 2).reshape(L, nh * HEAD_DIM)


def _branch_fwd(ya, yb, o2d, zc, gates, wa, wb, wc, *, name, tm=256):
    L = ya.shape[0]
    tm = min(tm, L)
    W, D = SSM_WIDTH, D_MODEL

    def body(ya_ref, yb_ref, o_ref, zc_ref, g0_ref, g1_ref, g2_ref, wa_ref, wb_ref, wc_ref,
             mg_ref, ta_ref, tb_ref, tc_ref, yc_ref):
        yc = (o_ref[...] * _silu(zc_ref[...])).astype(MXU)
        ta = _dot(ya_ref[...], wa_ref[...], "nt")
        tb = _dot(yb_ref[...], wb_ref[...], "nt")
        tc = _dot(yc, wc_ref[...], "nt")
        ta_ref[...] = ta
        tb_ref[...] = tb
        tc_ref[...] = tc
        yc_ref[...] = yc
        mg_ref[...] = (jax.nn.sigmoid(g0_ref[...]) * ta + jax.nn.sigmoid(g1_ref[...]) * tb
                       + jax.nn.sigmoid(g2_ref[...]) * tc).astype(MXU)

    row = pl.BlockSpec((tm, W), lambda i: (i, 0))
    wide = pl.BlockSpec((tm, D), lambda i: (i, 0))
    gate = lambda c: pl.BlockSpec((tm, D), lambda i, c=c: (i, c))
    wspec = pl.BlockSpec((D, W), lambda i: (0, 0))
    return pl.pallas_call(
        body, name=name, grid=(L // tm,),
        in_specs=[row, row, row, row, gate(0), gate(1), gate(2), wspec, wspec, wspec],
        out_specs=[wide, wide, wide, wide, row],
        out_shape=[jax.ShapeDtypeStruct((L, D), MXU), jax.ShapeDtypeStruct((L, D), F32),
                   jax.ShapeDtypeStruct((L, D), F32), jax.ShapeDtypeStruct((L, D), F32),
                   jax.ShapeDtypeStruct((L, W), MXU)],
        compiler_params=_cp(("parallel",), 56),
    )(ya, yb, o2d, zc, gates, gates, gates, wa, wb, wc)


def _branch_bwd(dmg, ta, tb, tc, gates, *, name, tm=256):
    L = dmg.shape[0]
    tm = min(tm, L)
    D = D_MODEL

    def body(dm_ref, ta_ref, tb_ref, tc_ref, g0_ref, g1_ref, g2_ref, da_ref, db_ref, dc_ref, dg_ref):
        dm = dm_ref[...]
        for i, (t_ref, g_ref, d_ref) in enumerate(((ta_ref, g0_ref, da_ref), (tb_ref, g1_ref, db_ref),
                                                   (tc_ref, g2_ref, dc_ref))):
            sg = jax.nn.sigmoid(g_ref[...])
            d_ref[...] = (sg * dm).astype(MXU)
            dg_ref[:, i * D:(i + 1) * D] = (dm * t_ref[...] * sg * (1.0 - sg)).astype(MXU)

    wide = pl.BlockSpec((tm, D), lambda i: (i, 0))
    gate = lambda c: pl.BlockSpec((tm, D), lambda i, c=c: (i, c))
    bf = jax.ShapeDtypeStruct((L, D), MXU)
    return pl.pallas_call(
        body, name=name, grid=(L // tm,),
        in_specs=[wide, wide, wide, wide, gate(0), gate(1), gate(2)],
        out_specs=[wide, wide, wide, pl.BlockSpec((tm, 3 * D), lambda i: (i, 0))],
        out_shape=[bf, bf, bf, jax.ShapeDtypeStruct((L, 3 * D), MXU)],
        compiler_params=_cp(("parallel",), 56),
    )(dmg, ta, tb, tc, gates, gates, gates)


def _gate_c_bwd(dyc, o2d, zc, *, name, tm=256):
    L, W = dyc.shape
    tm = min(tm, L)

    def body(dy_ref, o_ref, z_ref, do_ref, dz_ref):
        _, vjp = jax.vjp(lambda o, z: o * _silu(z), o_ref[...], z_ref[...])
        do, dz = vjp(dy_ref[...])
        do_ref[...] = do
        dz_ref[...] = dz.astype(MXU)

    row = pl.BlockSpec((tm, W), lambda i: (i, 0))
    return pl.pallas_call(body, name=name, grid=(L // tm,), in_specs=[row, row, row], out_specs=[row, row],
                          out_shape=[jax.ShapeDtypeStruct((L, W), F32), jax.ShapeDtypeStruct((L, W), MXU)],
                          compiler_params=_cp(("parallel",)))(dyc, o2d, zc)


def _adamw(w, g, m, v, *, name):
    shape = w.shape
    cols = shape[-1]
    w2, g2, m2, v2 = (t.reshape(-1, cols) for t in (w, g, m, v))
    rows = w2.shape[0]
    tc = 1024 if cols % 1024 == 0 else cols
    lane_cols = -(-tc // 128) * 128
    tr = rows
    while tr % 16 == 0 and tr * lane_cols * 4 > 2 * _MB:
        tr //= 2

    def body(w_ref, g_ref, m_ref, v_ref, d_ref, nm_ref, nv_ref):
        gv = g_ref[...]
        nm = ADAM_B1 * m_ref[...] + (1.0 - ADAM_B1) * gv
        nv = ADAM_B2 * v_ref[...] + (1.0 - ADAM_B2) * jnp.square(gv)
        m_hat = nm / (1.0 - ADAM_B1 ** ADAM_STEP)
        v_hat = nv / (1.0 - ADAM_B2 ** ADAM_STEP)
        d_ref[...] = -ADAM_LR * (m_hat / (jnp.sqrt(v_hat) + ADAM_EPS) + ADAM_WD * w_ref[...])
        nm_ref[...] = nm
        nv_ref[...] = nv

    spec = pl.BlockSpec((tr, tc), lambda i, j: (i, j))
    out = jax.ShapeDtypeStruct((rows, cols), F32)
    d, nm, nv = pl.pallas_call(body, name=name, grid=(rows // tr, cols // tc), in_specs=[spec] * 4,
                               out_specs=[spec] * 3, out_shape=[out, out, out],
                               compiler_params=_cp(("parallel", "parallel")))(w2, g2, m2, v2)
    return d.reshape(shape), nm.reshape(shape), nv.reshape(shape)


def _prep_layer(p, l):
    are = p["ssm_a_re"][l].reshape(SSM_CH, 1)
    aim = p["ssm_a_im"][l].reshape(SSM_CH, 1)
    ldt = p["ssm_log_dt"][l].reshape(1, SSM_GROUPS)
    bre = p["ssm_b_re"][l].reshape(SSM_CH, SSM_GROUP)
    bim = p["ssm_b_im"][l].reshape(SSM_CH, SSM_GROUP)
    lbr, lbi, bbr, bbi = _s5_params_fwd(are, aim, ldt, bre, bim, name=f"s5_params_fwd_{l}")
    cre = p["ssm_c_re"][l].transpose(0, 2, 1).reshape(SSM_CH, SSM_GROUP)
    cim = p["ssm_c_im"][l].transpose(0, 2, 1).reshape(SSM_CH, SSM_GROUP)
    return dict(raw=(are, aim, ldt, bre, bim),
                lbr=lbr.reshape(N_SLAB * SLAB_NC, 1, 128), lbi=lbi.reshape(N_SLAB * SLAB_NC, 1, 128),
                btr=_expand_bd(bbr), bti=_expand_bd(bbi), cbr=_expand_bd(cre), cbi=_expand_bd(cim),
                dvec=p["ssm_d"][l].reshape(1, SSM_WIDTH))


def _layer_fwd(x, p, big, l, tabs):
    L = x.shape[0]
    cosf, sinf, rot = tabs
    sp = _prep_layer(p, l)
    h = _rms_fwd(x, p["norm_w"][l], name=f"rms_fwd_{l}")
    mm = functools.partial(_matmul, h, big["winT"], "nt", tm=L, tn=256, tk=D_MODEL)
    main = mm(name=f"proj_main_{l}", shape=(L, N_MAIN, D_MODEL))
    zc = mm(name=f"proj_zc_{l}", shape=(L, N_ZC, D_MODEL), b_off=(N_MAIN // 256, 0))
    gates = mm(name=f"proj_gates_{l}", shape=(L, N_GATES, D_MODEL), b_off=((N_MAIN + N_ZC) // 256, 0))
    ys, sr, si = _s5_fwd(main, sp["btr"], sp["bti"], sp["cbr"], sp["cbi"], sp["lbr"], sp["lbi"], sp["dvec"],
                         name=f"s5_fwd_{l}")
    ya = _glu_fwd(ys, main, big["glu_w"], p["ssm_glu_b"][l], name=f"glu_fwd_{l}")
    yb = _sg_fwd(main, p["sg_ln_w"][l], p["sg_ln_b"][l], p["sg_w"][l], p["sg_b"][l], name=f"sg_fwd_{l}")
    qh = _to_heads(main[:, 5120:6144], ATT_HEADS)
    kh = _to_heads(main[:, 6144:6272], ATT_KV_HEADS)
    vh = _to_heads(main[:, 6272:6400], ATT_KV_HEADS)
    oh = _attn_fwd(qh, kh, vh, p["attn_sinks"][l], cosf, sinf, rot, name=f"attn_fwd_{l}")
    o2d = _from_heads(oh)
    mg, ta, tb, tc, yc = _branch_fwd(ya, yb, o2d, zc, gates, big["wbaT"], big["wbbT"], big["wbcT"],
                                     name=f"branch_fwd_{l}")
    xn = _matmul(mg, big["w_out"], "nn", name=f"out_fwd_{l}", shape=(L, D_MODEL, D_MODEL), tm=512, tn=512,
                 tk=D_MODEL, add=x)
    saved = dict(x=x, h=h, main=main, zc=zc, gates=gates, ys=ys, sr=sr, si=si, ya=ya, yb=yb, yc=yc, o2d=o2d,
                 qh=qh, kh=kh, vh=vh, mg=mg, ta=ta, tb=tb, tc=tc, sp=sp)
    return xn, saved


def _layer_bwd(dxn, s, p, big, l, tabs):
    L = dxn.shape[0]
    D, W = D_MODEL, SSM_WIDTH
    cosf, sinf, rot = tabs
    sp = s["sp"]
    g = {}
    dmg = _matmul(dxn, big["w_out"], "nt", name=f"out_bwd_dm_{l}", shape=(L, D, D), tm=512, tn=512, tk=D)
    g["w_out"] = _matmul(s["mg"], dxn, "tn", name=f"out_bwd_dw_{l}", shape=(D, D, L), tm=512, tn=512, tk=L)
    dta, dtb, dtc, dgates = _branch_bwd(dmg, s["ta"], s["tb"], s["tc"], s["gates"], name=f"branch_bwd_{l}")
    dys_ = {}
    for nm, dt, y, wt in (("a", dta, s["ya"], big["wbaT"]), ("b", dtb, s["yb"], big["wbbT"]),
                          ("c", dtc, s["yc"], big["wbcT"])):
        dys_[nm] = _matmul(dt, wt, "nn", name=f"branch_bwd_dy{nm}_{l}", shape=(L, W, D), tm=512, tn=512, tk=D)
        g["wb" + nm + "T"] = _matmul(dt, y, "tn", name=f"branch_bwd_dw{nm}_{l}", shape=(D, W, L),
                                     tm=512, tn=512, tk=L)
    do2d, dzc = _gate_c_bwd(dys_["c"], s["o2d"], s["zc"], name=f"gate_c_bwd_{l}")
    dqh, dkh, dvh, dsinks = _attn_bwd(_to_heads(do2d, ATT_HEADS), s["qh"], s["kh"], s["vh"], p["attn_sinks"][l],
                                      cosf, sinf, rot, name=f"attn_bwd_{l}")
    g["attn_sinks"] = dsinks.reshape(ATT_HEADS)
    dub, dvb, dzb, dlnw, dlnb, dsgw, dsgb = _sg_bwd(dys_["b"], s["main"], p["sg_ln_w"][l], p["sg_ln_b"][l],
                                                    p["sg_w"][l], p["sg_b"][l], name=f"sg_bwd_{l}")
    g["sg_ln_w"], g["sg_ln_b"] = dlnw.reshape(W), dlnb.reshape(W)
    g["sg_w"], g["sg_b"] = dsgw, dsgb.reshape(SG_HEADS, SG_CHUNK)
    dys, dza, a1, dzl, dgb = _glu_bwd(dys_["a"], s["ys"], s["main"], big["glu_w"], p["ssm_glu_b"][l],
                                      name=f"glu_bwd_{l}")
    g["ssm_glu_b"] = dgb.reshape(W)
    g["glu_w"] = _matmul(a1, dzl, "tn", name=f"glu_bwd_dw_{l}", shape=(W, W, L), tm=512, tn=512, tk=L)
    dua, dbtr, dbti, dcbr, dcbi, dlr, dli, dd = _s5_bwd(dys, s["main"], s["sr"], s["si"], sp["btr"], sp["bti"],
                                                        sp["cbr"], sp["cbi"], sp["lbr"], sp["lbi"], sp["dvec"],
                                                        name=f"s5_bwd_{l}")
    g["ssm_d"] = dd.reshape(W)
    to_c = lambda t: _contract_bd(t).reshape(SSM_GROUPS, SSM_STATE, SSM_GROUP).transpose(0, 2, 1)
    g["ssm_c_re"], g["ssm_c_im"] = to_c(dcbr), to_c(dcbi)
    dare, daim, dldt, dbre, dbim = _s5_params_bwd(*sp["raw"], dlr.reshape(SSM_CH, 1), dli.reshape(SSM_CH, 1),
                                                  _contract_bd(dbtr), _contract_bd(dbti),
                                                  name=f"s5_params_bwd_{l}")
    g["ssm_a_re"] = dare.reshape(SSM_GROUPS, SSM_STATE)
    g["ssm_a_im"] = daim.reshape(SSM_GROUPS, SSM_STATE)
    g["ssm_log_dt"] = dldt.reshape(SSM_GROUPS)
    g["ssm_b_re"] = dbre.reshape(SSM_GROUPS, SSM_STATE, SSM_GROUP)
    g["ssm_b_im"] = dbim.reshape(SSM_GROUPS, SSM_STATE, SSM_GROUP)
    dproj = jnp.concatenate([t.astype(MXU) for t in (dua, dza, dub, dvb, dzb, _from_heads(dqh), _from_heads(dkh),
                                                     _from_heads(dvh), dzc, dgates)], axis=1)
    dh = _matmul(dproj, big["winT"], "nn", name=f"proj_bwd_dh_{l}", shape=(L, D, D_IN), tm=512, tn=512, tk=D_IN // 2)
    g["winT"] = _matmul(dproj, s["h"], "tn", name=f"proj_bwd_dw_{l}", shape=(D_IN, D, L), tm=256, tn=D, tk=L)
    dx, dnw = _rms_bwd(s["x"], p["norm_w"][l], dh, dxn, name=f"rms_bwd_{l}")
    g["norm_w"] = dnw
    return dx, g


def _local_step(x, tgt, p, bigs):
    L = x.shape[0]
    tabs = _rope_tables(L)
    saved = []
    for l in range(DEPTH):
        x, s = _layer_fwd(x, p, bigs[l], l, tabs)
        saved.append(s)
    loss, dx, dfw = _final_loss(x, p["final_norm_w"], tgt, name="final_loss")
    grads = [None] * DEPTH
    for l in reversed(range(DEPTH)):
        dx, grads[l] = _layer_bwd(dx, saved[l], p, bigs[l], l, tabs)
    return loss, dx, grads, dfw


MESH = pl.DeviceIdType.MESH
_ANY = pl.BlockSpec(memory_space=pl.ANY)
ROW_ALIGN = 16


def _coords():
    return lax.axis_index("x"), lax.axis_index("y"), lax.axis_index("c")


def _gather8(arrs, *, split, name):
    n = len(arrs)
    rows = [a.shape[0] // 2 if split else a.shape[0] for a in arrs]
    for r in rows:
        assert r % ROW_ALIGN == 0

    def body(*refs):
        ins, outs = refs[:n], refs[n:2 * n]
        send, recv, lsem = refs[2 * n:]
        x, y, c = _coords()
        me, sibling = (x, y, c), (x, y, 1 - c)
        chips = [(1 - x, y), (x, 1 - y), (1 - x, 1 - y)]

        def blk(a, px, py, pc):
            return outs[a].at[pl.ds(pl.multiple_of((4 * px + 2 * py + pc) * rows[a], ROW_ALIGN), rows[a]), :]

        def own(a):
            if split:
                return ins[a].at[pl.ds(pl.multiple_of(c * rows[a], ROW_ALIGN), rows[a]), :]
            return ins[a]

        def copy(a, k, block, to, src=None):
            return pltpu.make_async_remote_copy(
                src_ref=blk(a, *block) if src is None else src, dst_ref=blk(a, *block),
                send_sem=send.at[a, k], recv_sem=recv.at[a, k], device_id=to, device_id_type=MESH)

        mine, first, passed = [], [], []
        for a in range(n):
            mine.append(pltpu.make_async_copy(own(a), blk(a, *me), lsem.at[a]))
            mine[a].start()
            f = [copy(a, 0, me, sibling, src=own(a))]
            f += [copy(a, 1 + j, me, (*chip, c), src=own(a)) for j, chip in enumerate(chips)]
            for cp in f:
                cp.start()
            first.append(f)
        for a in range(n):
            ps = [copy(a, 4 + j, (*chip, c), sibling) for j, chip in enumerate(chips)]
            for j, chip in enumerate(chips):
                copy(a, 1 + j, (*chip, c), me).wait_recv()
                ps[j].start()
            passed.append(ps)
        for a in range(n):
            copy(a, 0, sibling, me).wait_recv()
            for j, chip in enumerate(chips):
                copy(a, 4 + j, (*chip, 1 - c), me).wait_recv()
            for cp in first[a] + passed[a]:
                cp.wait_send()
            mine[a].wait()

    return pl.pallas_call(
        body, name=name,
        in_specs=[_ANY] * n, out_specs=[_ANY] * n,
        out_shape=[jax.ShapeDtypeStruct((8 * r,) + a.shape[1:], a.dtype) for r, a in zip(rows, arrs)],
        scratch_shapes=[pltpu.SemaphoreType.DMA((n, 7)), pltpu.SemaphoreType.DMA((n, 7)), pltpu.SemaphoreType.DMA((n,))],
    )(*arrs)


def _sibling_swap(arrs, *, pick_other, name):
    n = len(arrs)

    def body(*refs):
        ins, outs = refs[:n], refs[n:2 * n]
        send, recv = refs[2 * n:]
        x, y, c = _coords()
        sel = (1 - c) if pick_other else c
        cps = [pltpu.make_async_remote_copy(src_ref=ins[a].at[:, sel], dst_ref=outs[a], send_sem=send.at[a],
                                            recv_sem=recv.at[a], device_id=(x, y, 1 - c), device_id_type=MESH)
               for a in range(n)]
        for cp in cps:
            cp.start()
        for cp in cps:
            cp.wait_recv()
        for cp in cps:
            cp.wait_send()

    return pl.pallas_call(
        body, name=name, in_specs=[_ANY] * n, out_specs=[_ANY] * n,
        out_shape=[jax.ShapeDtypeStruct((a.shape[0],) + a.shape[2:], a.dtype) for a in arrs],
        scratch_shapes=[pltpu.SemaphoreType.DMA((n,)), pltpu.SemaphoreType.DMA((n,))],
    )(*arrs)


def _col_tile(lead, rows, cols, itemsize=4, cap=4 * _MB):
    tc = cols
    while tc % 256 == 0 and lead * rows * tc * itemsize > cap:
        tc //= 2
    return tc


def _pair_sum(mine, theirs, *, name):
    _, _, rows, cols = mine.shape
    tc = _col_tile(1, rows, cols)
    c = lax.axis_index("c")

    def body(c_ref, a_ref, b_ref, o_ref):
        o_ref[...] = (a_ref[...] + b_ref[...]).astype(MXU)

    return pl.pallas_call(
        body, name=name,
        grid_spec=pltpu.PrefetchScalarGridSpec(
            num_scalar_prefetch=1, grid=(4, cols // tc),
            in_specs=[pl.BlockSpec((None, None, rows, tc), lambda j, i, cr: (j, cr[0], 0, i)),
                      pl.BlockSpec((None, rows, tc), lambda j, i, cr: (j, 0, i))],
            out_specs=pl.BlockSpec((None, rows, tc), lambda j, i, cr: (j, 0, i))),
        out_shape=jax.ShapeDtypeStruct((4, rows, cols), MXU),
        compiler_params=_cp(("parallel", "parallel")),
    )(c.reshape(1).astype(jnp.int32), mine, theirs)


def _chip_scatter(parts, *, name):
    n = len(parts)

    def body(*refs):
        ins, outs = refs[:n], refs[n:2 * n]
        send, recv, lsem = refs[2 * n:]
        x, y, c = _coords()
        jme = 2 * x + y
        chips = [(1 - x, y), (x, 1 - y), (1 - x, 1 - y)]
        local, cps = [], []
        for a in range(n):
            local.append(pltpu.make_async_copy(ins[a].at[jme], outs[a].at[jme], lsem.at[a]))
            local[a].start()
            for k, (px, py) in enumerate(chips):
                cp = pltpu.make_async_remote_copy(src_ref=ins[a].at[2 * px + py], dst_ref=outs[a].at[jme],
                                                  send_sem=send.at[a, k], recv_sem=recv.at[a, k],
                                                  device_id=(px, py, c), device_id_type=MESH)
                cp.start()
                cps.append(cp)
        for cp in cps:
            cp.wait_recv()
        for cp in cps:
            cp.wait_send()
        for a in range(n):
            local[a].wait()

    return pl.pallas_call(
        body, name=name, in_specs=[_ANY] * n, out_specs=[_ANY] * n,
        out_shape=[jax.ShapeDtypeStruct(p.shape, p.dtype) for p in parts],
        scratch_shapes=[pltpu.SemaphoreType.DMA((n, 3)), pltpu.SemaphoreType.DMA((n, 3)), pltpu.SemaphoreType.DMA((n,))],
    )(*parts)


def _sum_slots(t, *, name, into_half=False):
    S, rows, cols = t.shape
    tc = _col_tile(S, rows, cols)

    def body(*refs):
        t_ref, o_ref = refs[-2], refs[-1]
        acc = t_ref[0].astype(F32)
        for s in range(1, S):
            acc = acc + t_ref[s].astype(F32)
        o_ref[...] = acc

    if not into_half:
        return pl.pallas_call(
            body, name=name, grid=(cols // tc,),
            in_specs=[pl.BlockSpec((S, rows, tc), lambda i: (0, 0, i))],
            out_specs=pl.BlockSpec((rows, tc), lambda i: (0, i)),
            out_shape=jax.ShapeDtypeStruct((rows, cols), F32),
            compiler_params=_cp(("parallel",)),
        )(t)
    c = lax.axis_index("c")
    return pl.pallas_call(
        body, name=name,
        grid_spec=pltpu.PrefetchScalarGridSpec(
            num_scalar_prefetch=1, grid=(cols // tc,),
            in_specs=[pl.BlockSpec((S, rows, tc), lambda i, cr: (0, 0, i))],
            out_specs=pl.BlockSpec((None, rows, tc), lambda i, cr: (cr[0], 0, i))),
        out_shape=jax.ShapeDtypeStruct((2, rows, cols), F32),
        compiler_params=_cp(("parallel",)),
    )(c.reshape(1).astype(jnp.int32), t)


def _halves_join(bufs, *, name):
    n = len(bufs)

    def body(*refs):
        outs = refs[n:2 * n]
        send, recv = refs[2 * n:]
        x, y, c = _coords()
        cps = [pltpu.make_async_remote_copy(src_ref=outs[a].at[c], dst_ref=outs[a].at[c], send_sem=send.at[a],
                                            recv_sem=recv.at[a], device_id=(x, y, 1 - c), device_id_type=MESH)
               for a in range(n)]
        for cp in cps:
            cp.start()
        for cp in cps:
            cp.wait_recv()
        for cp in cps:
            cp.wait_send()

    return pl.pallas_call(
        body, name=name, in_specs=[_ANY] * n, out_specs=[_ANY] * n,
        out_shape=[jax.ShapeDtypeStruct(b.shape, b.dtype) for b in bufs],
        input_output_aliases={a: a for a in range(n)},
        scratch_shapes=[pltpu.SemaphoreType.DMA((n,)), pltpu.SemaphoreType.DMA((n,))],
    )(*bufs)


def _reduce_scatter(grads, *, tag):
    views = [g.reshape(4, 2, g.shape[0] // 8, g.shape[1]) for g in grads]
    theirs = _sibling_swap(views, pick_other=True, name=f"rs_swap_{tag}")
    parts = [_pair_sum(v, t, name=f"rs_pair_{tag}_{i}") for i, (v, t) in enumerate(zip(views, theirs))]
    got = _chip_scatter(parts, name=f"rs_scatter_{tag}")
    halves = [_sum_slots(t, name=f"rs_sum_{tag}_{i}", into_half=True) for i, t in enumerate(got)]
    joined = _halves_join(halves, name=f"rs_join_{tag}")
    return [j.reshape(2 * j.shape[1], j.shape[2]) for j in joined]


_SMALL = ("norm_w", "ssm_a_re", "ssm_a_im", "ssm_log_dt", "ssm_b_re", "ssm_b_im", "ssm_c_re", "ssm_c_im", "ssm_d",
          "ssm_glu_b", "sg_ln_w", "sg_ln_b", "sg_w", "sg_b", "attn_sinks", "final_norm_w")
_BIG = ("w_in", "ssm_glu_w", "w_branch_a", "w_branch_b", "w_branch_c", "w_out")
_WEIGHTS = ("norm_w", "w_in", "ssm_a_re", "ssm_a_im", "ssm_log_dt", "ssm_b_re", "ssm_b_im", "ssm_c_re", "ssm_c_im",
            "ssm_d", "ssm_glu_w", "ssm_glu_b", "sg_ln_w", "sg_ln_b", "sg_w", "sg_b", "attn_sinks", "w_branch_a",
            "w_branch_b", "w_branch_c", "w_out", "final_norm_w")
_PACK_COLS = 1024
_PACK_ALIGN = 16 * _PACK_COLS


def _pack(ts):
    flat = jnp.concatenate([t.reshape(-1) for t in ts])
    pad = (-flat.shape[0]) % _PACK_ALIGN
    return jnp.pad(flat, (0, pad)).reshape(-1, _PACK_COLS)


def _unpack(buf, like):
    flat = buf.reshape(-1)
    out, pos = [], 0
    for t in like:
        out.append(flat[pos:pos + t.size].reshape(t.shape))
        pos += t.size
    return out


def kernel(x, norm_w, w_in, ssm_a_re, ssm_a_im, ssm_log_dt, ssm_b_re, ssm_b_im, ssm_c_re, ssm_c_im, ssm_d, ssm_glu_w, ssm_glu_b, sg_ln_w, sg_ln_b, sg_w, sg_b, attn_sinks, w_branch_a, w_branch_b, w_branch_c, w_out, final_norm_w, loss_target, m_norm_w, m_w_in, m_ssm_a_re, m_ssm_a_im, m_ssm_log_dt, m_ssm_b_re, m_ssm_b_im, m_ssm_c_re, m_ssm_c_im, m_ssm_d, m_ssm_glu_w, m_ssm_glu_b, m_sg_ln_w, m_sg_ln_b, m_sg_w, m_sg_b, m_attn_sinks, m_w_branch_a, m_w_branch_b, m_w_branch_c, m_w_out, m_final_norm_w, v_norm_w, v_w_in, v_ssm_a_re, v_ssm_a_im, v_ssm_log_dt, v_ssm_b_re, v_ssm_b_im, v_ssm_c_re, v_ssm_c_im, v_ssm_d, v_ssm_glu_w, v_ssm_glu_b, v_sg_ln_w, v_sg_ln_b, v_sg_w, v_sg_b, v_attn_sinks, v_w_branch_a, v_w_branch_b, v_w_branch_c, v_w_out, v_final_norm_w):
    w = dict(norm_w=norm_w, w_in=w_in, ssm_a_re=ssm_a_re, ssm_a_im=ssm_a_im, ssm_log_dt=ssm_log_dt, ssm_b_re=ssm_b_re,
             ssm_b_im=ssm_b_im, ssm_c_re=ssm_c_re, ssm_c_im=ssm_c_im, ssm_d=ssm_d, ssm_glu_w=ssm_glu_w,
             ssm_glu_b=ssm_glu_b, sg_ln_w=sg_ln_w, sg_ln_b=sg_ln_b, sg_w=sg_w, sg_b=sg_b, attn_sinks=attn_sinks,
             w_branch_a=w_branch_a, w_branch_b=w_branch_b, w_branch_c=w_branch_c, w_out=w_out,
             final_norm_w=final_norm_w)
    m = dict(norm_w=m_norm_w, w_in=m_w_in, ssm_a_re=m_ssm_a_re, ssm_a_im=m_ssm_a_im, ssm_log_dt=m_ssm_log_dt,
             ssm_b_re=m_ssm_b_re, ssm_b_im=m_ssm_b_im, ssm_c_re=m_ssm_c_re, ssm_c_im=m_ssm_c_im, ssm_d=m_ssm_d,
             ssm_glu_w=m_ssm_glu_w, ssm_glu_b=m_ssm_glu_b, sg_ln_w=m_sg_ln_w, sg_ln_b=m_sg_ln_b, sg_w=m_sg_w,
             sg_b=m_sg_b, attn_sinks=m_attn_sinks, w_branch_a=m_w_branch_a, w_branch_b=m_w_branch_b,
             w_branch_c=m_w_branch_c, w_out=m_w_out, final_norm_w=m_final_norm_w)
    v = dict(norm_w=v_norm_w, w_in=v_w_in, ssm_a_re=v_ssm_a_re, ssm_a_im=v_ssm_a_im, ssm_log_dt=v_ssm_log_dt,
             ssm_b_re=v_ssm_b_re, ssm_b_im=v_ssm_b_im, ssm_c_re=v_ssm_c_re, ssm_c_im=v_ssm_c_im, ssm_d=v_ssm_d,
             ssm_glu_w=v_ssm_glu_w, ssm_glu_b=v_ssm_glu_b, sg_ln_w=v_sg_ln_w, sg_ln_b=v_sg_ln_b, sg_w=v_sg_w,
             sg_b=v_sg_b, attn_sinks=v_attn_sinks, w_branch_a=v_w_branch_a, w_branch_b=v_w_branch_b,
             w_branch_c=v_w_branch_c, w_out=v_w_out, final_norm_w=v_final_norm_w)

    bigs = []
    for l in range(DEPTH):
        shards = [w["w_in"][l].T.astype(MXU), w["ssm_glu_w"][l].astype(MXU), w["w_branch_a"][l].T.astype(MXU),
                  w["w_branch_b"][l].T.astype(MXU), w["w_branch_c"][l].T.astype(MXU), w["w_out"][l].astype(MXU)]
        full = _gather8(shards, split=True, name=f"gather_weights_{l}")
        bigs.append(dict(zip(("winT", "glu_w", "wbaT", "wbbT", "wbcT", "w_out"), full)))

    p = {k: w[k] for k in _SMALL}
    loss, dx, grads, dfw = _local_step(x[0], loss_target[0], p, bigs)
    loss = lax.psum(loss, ("x", "y", "c"))

    gbig = {k: [] for k in _BIG}
    for l in range(DEPTH):
        g = grads[l]
        red = _reduce_scatter([g["winT"], g["glu_w"], g["wbaT"], g["wbbT"], g["wbcT"], g["w_out"]], tag=str(l))
        gbig["w_in"].append(red[0])
        gbig["ssm_glu_w"].append(red[1])
        gbig["w_branch_a"].append(red[2].T)
        gbig["w_branch_b"].append(red[3].T)
        gbig["w_branch_c"].append(red[4].T)
        gbig["w_out"].append(red[5])
    gfull = {k: jnp.stack(vs) for k, vs in gbig.items()}

    small_like = [w[k] for k in _SMALL]
    gs = [jnp.stack([grads[l][k] for l in range(DEPTH)]) if k != "final_norm_w" else dfw for k in _SMALL]
    packed = _pack(gs)
    allg = _gather8([packed], split=False, name="gather_small_grads")[0]
    gsum = _sum_slots(allg.reshape(8, packed.shape[0], _PACK_COLS), name="sum_small_grads")
    for k, t in zip(_SMALL, _unpack(gsum, small_like)):
        gfull[k] = t

    delta, new_m, new_v = {}, {}, {}
    for k in _BIG:
        if k == "w_in":
            tr = lambda t: t.transpose(0, 2, 1)
            res = _adamw(tr(w[k]), gfull[k], tr(m[k]), tr(v[k]), name=f"adamw_{k}")
            delta[k], new_m[k], new_v[k] = (tr(t) for t in res)
            gfull[k] = tr(gfull[k])
        else:
            delta[k], new_m[k], new_v[k] = _adamw(w[k], gfull[k], m[k], v[k], name=f"adamw_{k}")
    for k in _SMALL:
        delta[k], new_m[k], new_v[k] = _adamw(w[k], gfull[k], m[k], v[k], name=f"adamw_{k}")

    return (loss, dx[None], *[gfull[k] for k in _WEIGHTS], *[delta[k] for k in _WEIGHTS],
            *[new_m[k] for k in _WEIGHTS], *[new_v[k] for k in _WEIGHTS])
```

```python
import functools
import math

import numpy as np
import jax
import jax.numpy as jnp
from jax import lax
from jax.experimental import pallas as pl
from jax.experimental.pallas import tpu as pltpu

F32 = jnp.float32
MXU = jnp.bfloat16
HIGHEST = lax.Precision.HIGHEST

D_MODEL = 2048
DEPTH = 2
EPS = 1e-6
NEG_INF = -1e30
SSM_WIDTH = 1024
SSM_GROUP = 16
SSM_GROUPS = 64
SSM_STATE = 64
SSM_CH = SSM_GROUPS * SSM_STATE
SLAB = 128
SLAB_CH = (SLAB // SSM_GROUP) * SSM_STATE
N_SLAB = SSM_WIDTH // SLAB
SCAN_SEG = 8
SG_HEADS = 8
SG_CHUNK = 128
HEAD_DIM = 64
ATT_HEADS = 16
ATT_KV_HEADS = 2
GQA_GROUP = 8
ATT_BLOCK = 128
WINDOW = 128
ROT_DIM = 16
ROPE_THETA = 500000.0
N_MAIN = 6400
N_ZC = 1024
N_GATES = 6144
D_IN = N_MAIN + N_ZC + N_GATES

ADAM_LR = 0.001
ADAM_B1 = 0.9
ADAM_B2 = 0.999
ADAM_EPS = 1e-08
ADAM_WD = 0.01
ADAM_STEP = 10

_DIMS = {"nn": (((1,), (0,)), ((), ())), "nt": (((1,), (1,)), ((), ())), "tn": (((0,), (0,)), ((), ()))}
_MB = 1024 * 1024


def _cp(sem, vmem_mb=48):
    return pltpu.CompilerParams(dimension_semantics=sem, vmem_limit_bytes=vmem_mb * _MB)


def _dot(a, b, mode):
    return lax.dot_general(a.astype(MXU), b.astype(MXU), _DIMS[mode], preferred_element_type=F32)


@jax.custom_vjp
def _mm_nn(a, b):
    return _dot(a, b, "nn")


def _mm_nn_fwd(a, b):
    return _dot(a, b, "nn"), (a, b)


def _mm_nn_bwd(res, g):
    a, b = res
    return _dot(g, b, "nt"), _dot(a, g, "tn")


_mm_nn.defvjp(_mm_nn_fwd, _mm_nn_bwd)


@jax.custom_vjp
def _mm_nt(a, bt):
    return _dot(a, bt, "nt")


def _mm_nt_fwd(a, bt):
    return _dot(a, bt, "nt"), (a, bt)


def _mm_nt_bwd(res, g):
    a, bt = res
    return _dot(g, bt, "nn"), _dot(g, a, "tn")


_mm_nt.defvjp(_mm_nt_fwd, _mm_nt_bwd)


def _rmsnorm(x, w):
    return x * lax.rsqrt(jnp.mean(x * x, axis=-1, keepdims=True) + EPS) * w


def _layernorm(x, w, b):
    mu = jnp.mean(x, axis=-1, keepdims=True)
    var = jnp.mean(jnp.square(x - mu), axis=-1, keepdims=True)
    return (x - mu) * lax.rsqrt(var + EPS) * w + b


def _silu(x):
    return x * jax.nn.sigmoid(x)


def _matmul(a, b, mode, *, name, shape, tm, tn, tk, out_dtype=F32, add=None,
            a_off=(0, 0), b_off=(0, 0), out=None, out_off=(0, 0)):
    m, n, k = shape
    tm, tn, tk = min(tm, m), min(tn, n), min(tk, k)
    assert m % tm == 0 and n % tn == 0 and k % tk == 0, (name, shape, tm, tn, tk)
    nk = k // tk
    has_add, has_out = add is not None, out is not None

    def body(*refs):
        a_ref, b_ref = refs[0], refs[1]
        pos = 2
        add_ref = None
        if has_add:
            add_ref = refs[pos]
            pos += 1
        if has_out:
            pos += 1
        o_ref = refs[pos]
        p = _dot(a_ref[...], b_ref[...], mode)
        if nk == 1:
            if has_add:
                p = p + add_ref[...].astype(F32)
            o_ref[...] = p.astype(out_dtype)
            return
        acc_ref = refs[pos + 1]
        kk = pl.program_id(2)

        @pl.when(kk == 0)
        def _():
            acc_ref[...] = p

        @pl.when(kk > 0)
        def _():
            acc_ref[...] += p

        @pl.when(kk == nk - 1)
        def _():
            r = acc_ref[...]
            if has_add:
                r = r + add_ref[...].astype(F32)
            o_ref[...] = r.astype(out_dtype)

    a0, a1 = a_off
    b0, b1 = b_off
    o0, o1 = out_off
    if mode == "tn":
        a_spec = pl.BlockSpec((tk, tm), lambda i, j, kk: (kk + a0, i + a1))
    else:
        a_spec = pl.BlockSpec((tm, tk), lambda i, j, kk: (i + a0, kk + a1))
    if mode == "nt":
        b_spec = pl.BlockSpec((tn, tk), lambda i, j, kk: (j + b0, kk + b1))
    else:
        b_spec = pl.BlockSpec((tk, tn), lambda i, j, kk: (kk + b0, j + b1))
    in_specs = [a_spec, b_spec]
    args = [a, b]
    if has_add:
        in_specs.append(pl.BlockSpec((tm, tn), lambda i, j, kk: (i, j)))
        args.append(add)
    aliases = {}
    if has_out:
        aliases = {len(args): 0}
        in_specs.append(pl.BlockSpec(memory_space=pl.ANY))
        args.append(out)
        out_shape = jax.ShapeDtypeStruct(out.shape, out.dtype)
        assert out.dtype == out_dtype
    else:
        out_shape = jax.ShapeDtypeStruct((m, n), out_dtype)
    return pl.pallas_call(
        body, name=name, grid=(m // tm, n // tn, nk),
        in_specs=in_specs,
        out_specs=pl.BlockSpec((tm, tn), lambda i, j, kk: (i + o0, j + o1)),
        out_shape=out_shape,
        scratch_shapes=[pltpu.VMEM((tm, tn), F32)] if nk > 1 else [],
        input_output_aliases=aliases,
        compiler_params=_cp(("parallel", "parallel", "arbitrary")),
    )(*args)


def _rms_fwd(x, w, *, name, tm=256):
    L, d = x.shape
    tm = min(tm, L)

    def body(x_ref, w_ref, h_ref):
        h_ref[...] = _rmsnorm(x_ref[...], w_ref[...]).astype(MXU)

    return pl.pallas_call(
        body, name=name, grid=(L // tm,),
        in_specs=[pl.BlockSpec((tm, d), lambda i: (i, 0)), pl.BlockSpec((1, d), lambda i: (0, 0))],
        out_specs=pl.BlockSpec((tm, d), lambda i: (i, 0)),
        out_shape=jax.ShapeDtypeStruct((L, d), MXU),
        compiler_params=_cp(("parallel",)),
    )(x, w.reshape(1, d))


def _rms_bwd(x, w, dh, dxn, *, name, tm=256):
    L, d = x.shape
    tm = min(tm, L)

    def body(x_ref, w_ref, dh_ref, dxn_ref, dx_ref, dw_ref):
        _, vjp = jax.vjp(_rmsnorm, x_ref[...], w_ref[...])
        dx, dw = vjp(dh_ref[...])
        dx_ref[...] = dx + dxn_ref[...]

        @pl.when(pl.program_id(0) == 0)
        def _():
            dw_ref[...] = jnp.zeros_like(dw_ref)

        dw_ref[...] += dw

    row = pl.BlockSpec((tm, d), lambda i: (i, 0))
    vec = pl.BlockSpec((1, d), lambda i: (0, 0))
    dx, dw = pl.pallas_call(
        body, name=name, grid=(L // tm,),
        in_specs=[row, vec, row, row], out_specs=[row, vec],
        out_shape=[jax.ShapeDtypeStruct((L, d), F32), jax.ShapeDtypeStruct((1, d), F32)],
        compiler_params=_cp(("arbitrary",)),
    )(x, w.reshape(1, d), dh, dxn)
    return dx, dw.reshape(d)


def _final_loss(x, w, tgt, *, name, tm=256):
    L, d = x.shape
    tm = min(tm, L)

    def loss_fn(xv, wv, tv):
        err = jnp.square(_rmsnorm(xv, wv) - tv)
        return 0.5 * jnp.sum(jnp.mean(err, axis=-1, keepdims=True), axis=0, keepdims=True)

    def body(x_ref, w_ref, t_ref, loss_ref, dx_ref, dw_ref):
        tv = t_ref[...]
        val, vjp = jax.vjp(lambda xv, wv: loss_fn(xv, wv, tv), x_ref[...], w_ref[...])
        dx, dw = vjp(jnp.ones((1, 1), F32))
        dx_ref[...] = dx

        @pl.when(pl.program_id(0) == 0)
        def _():
            dw_ref[...] = jnp.zeros_like(dw_ref)
            loss_ref[...] = jnp.zeros_like(loss_ref)

        dw_ref[...] += dw
        loss_ref[...] += jnp.broadcast_to(val, loss_ref.shape)

    row = pl.BlockSpec((tm, d), lambda i: (i, 0))
    vec = pl.BlockSpec((1, d), lambda i: (0, 0))
    loss, dx, dw = pl.pallas_call(
        body, name=name, grid=(L // tm,),
        in_specs=[row, vec, row],
        out_specs=[pl.BlockSpec((8, 128), lambda i: (0, 0)), row, vec],
        out_shape=[jax.ShapeDtypeStruct((8, 128), F32), jax.ShapeDtypeStruct((L, d), F32),
                   jax.ShapeDtypeStruct((1, d), F32)],
        compiler_params=_cp(("arbitrary",)),
    )(x, w.reshape(1, d), tgt)
    return loss[0, 0], dx, dw.reshape(d)


PARAM_ROWS = 512


def _s5_param_fn(are, aim, ldt, bre, bim, row0):
    n = are.shape[0]
    grp = (row0 + lax.broadcasted_iota(jnp.int32, (n, SSM_GROUPS), 0)) // SSM_STATE
    col = lax.broadcasted_iota(jnp.int32, (n, SSM_GROUPS), 1)
    sel = (grp == col).astype(F32)
    dt = jnp.sum(sel * jnp.exp(ldt), axis=-1, keepdims=True)
    mag = jnp.exp(are * dt)
    ang = aim * dt
    lbr = mag * jnp.cos(ang)
    lbi = mag * jnp.sin(ang)
    den = are * are + aim * aim
    nr = lbr - 1.0
    kr = (nr * are + lbi * aim) / den
    ki = (lbi * are - nr * aim) / den
    return lbr, lbi, kr * bre - ki * bim, kr * bim + ki * bre


def _s5_param_specs():
    col = pl.BlockSpec((PARAM_ROWS, 1), lambda i: (i, 0))
    mat = pl.BlockSpec((PARAM_ROWS, SSM_GROUP), lambda i: (i, 0))
    vec = pl.BlockSpec((1, SSM_GROUPS), lambda i: (0, 0))
    return col, mat, vec


def _s5_params_fwd(are, aim, ldt, bre, bim, *, name):
    n = are.shape[0]
    col, mat, vec = _s5_param_specs()

    def body(are_ref, aim_ref, ldt_ref, bre_ref, bim_ref, lbr_ref, lbi_ref, bbr_ref, bbi_ref):
        row0 = pl.program_id(0) * PARAM_ROWS
        lbr, lbi, bbr, bbi = _s5_param_fn(are_ref[...], aim_ref[...], ldt_ref[...], bre_ref[...], bim_ref[...], row0)
        lbr_ref[...] = lbr
        lbi_ref[...] = lbi
        bbr_ref[...] = bbr
        bbi_ref[...] = bbi

    cshape = jax.ShapeDtypeStruct((n, 1), F32)
    mshape = jax.ShapeDtypeStruct((n, SSM_GROUP), F32)
    return pl.pallas_call(body, name=name, grid=(n // PARAM_ROWS,),
                          in_specs=[col, col, vec, mat, mat], out_specs=[col, col, mat, mat],
                          out_shape=[cshape, cshape, mshape, mshape],
                          compiler_params=_cp(("parallel",)))(are, aim, ldt, bre, bim)


def _s5_params_bwd(are, aim, ldt, bre, bim, dlbr, dlbi, dbbr, dbbi, *, name):
    n = are.shape[0]
    col, mat, vec = _s5_param_specs()

    def body(are_ref, aim_ref, ldt_ref, bre_ref, bim_ref, g0, g1, g2, g3, o0, o1, o2, o3, o4):
        row0 = pl.program_id(0) * PARAM_ROWS
        _, vjp = jax.vjp(lambda a, b, c, d, e: _s5_param_fn(a, b, c, d, e, row0),
                         are_ref[...], aim_ref[...], ldt_ref[...], bre_ref[...], bim_ref[...])
        dare, daim, dldt, dbre, dbim = vjp((g0[...], g1[...], g2[...], g3[...]))
        o0[...] = dare
        o1[...] = daim
        o3[...] = dbre
        o4[...] = dbim

        @pl.when(pl.program_id(0) == 0)
        def _():
            o2[...] = jnp.zeros_like(o2)

        o2[...] += dldt

    cshape = jax.ShapeDtypeStruct((n, 1), F32)
    mshape = jax.ShapeDtypeStruct((n, SSM_GROUP), F32)
    return pl.pallas_call(body, name=name, grid=(n // PARAM_ROWS,),
                          in_specs=[col, col, vec, mat, mat, col, col, mat, mat],
                          out_specs=[col, col, vec, mat, mat],
                          out_shape=[cshape, cshape, jax.ShapeDtypeStruct((1, SSM_GROUPS), F32), mshape, mshape],
                          compiler_params=_cp(("arbitrary",)))(are, aim, ldt, bre, bim, dlbr, dlbi, dbbr, dbbi)


SLAB_NC = SLAB_CH // 128


def _s5_specs(L):
    slab = pl.BlockSpec((L, SLAB), lambda s: (0, s))
    wspec = pl.BlockSpec((SLAB_NC, 128, SLAB), lambda s: (s, 0, 0))
    lspec = pl.BlockSpec((SLAB_NC, 1, 128), lambda s: (s, 0, 0))
    sspec = pl.BlockSpec((SLAB_NC, L, 128), lambda s: (s, 0, 0))
    dspec = pl.BlockSpec((1, SLAB), lambda s: (0, s))
    return slab, wspec, lspec, sspec, dspec


def _scan_inplace(sr_ref, si_ref, lr, li, pr_ref, pi_ref, *, reverse):
    NC, L, W = sr_ref.shape
    T = L // SCAN_SEG
    lr8 = [jnp.broadcast_to(lr[k], (SCAN_SEG, W)) for k in range(NC)]
    li8 = [jnp.broadcast_to(li[k], (SCAN_SEG, W)) for k in range(NC)]

    def step(j, carry):
        row = (T - 1 - j) if reverse else j
        idx = pl.ds(row, SCAN_SEG, stride=T)
        out = []
        for k in range(NC):
            sr, si, qr, qi = carry[k]
            nsr = lr8[k] * sr - li8[k] * si + sr_ref[k, idx, :]
            nsi = lr8[k] * si + li8[k] * sr + si_ref[k, idx, :]
            sr_ref[k, idx, :] = nsr
            si_ref[k, idx, :] = nsi
            pr_ref[k, pl.ds(row, 1), :] = qr
            pi_ref[k, pl.ds(row, 1), :] = qi
            out.append((nsr, nsi, qr * lr[k] - qi * li[k], qr * li[k] + qi * lr[k]))
        return tuple(out)

    zero = jnp.zeros((SCAN_SEG, W), F32)
    ends = lax.fori_loop(0, T, step, tuple((zero, zero, lr[k], li[k]) for k in range(NC)))
    last = 0 if reverse else T - 1
    order = range(SCAN_SEG - 1, -1, -1) if reverse else range(SCAN_SEG)
    for k in range(NC):
        er, ei = ends[k][0], ends[k][1]
        ltr = pr_ref[k, pl.ds(last, 1), :]
        lti = pi_ref[k, pl.ds(last, 1), :]
        cr = jnp.zeros((1, W), F32)
        ci = jnp.zeros((1, W), F32)
        for n, seg in enumerate(order):
            if n > 0:
                rows = pl.ds(seg * T, T)
                pr = pr_ref[k]
                pi = pi_ref[k]
                sr_ref[k, rows, :] += pr * cr - pi * ci
                si_ref[k, rows, :] += pr * ci + pi * cr
            if n < SCAN_SEG - 1:
                ncr = er[seg:seg + 1, :] + ltr * cr - lti * ci
                nci = ei[seg:seg + 1, :] + ltr * ci + lti * cr
                cr, ci = ncr, nci


def _s5_fwd(main, btr, bti, cbr, cbi, lbr, lbi, dvec, *, name):
    L = main.shape[0]
    T = L // SCAN_SEG

    def body(u_ref, btr_ref, bti_ref, cbr_ref, cbi_ref, lr_ref, li_ref, d_ref, ys_ref, sr_ref, si_ref, pr_ref, pi_ref):
        u = u_ref[...]
        for k in range(SLAB_NC):
            sr_ref[k] = _dot(u, btr_ref[k], "nt")
            si_ref[k] = _dot(u, bti_ref[k], "nt")
        _scan_inplace(sr_ref, si_ref, lr_ref[...], li_ref[...], pr_ref, pi_ref, reverse=False)
        ys = d_ref[...] * u
        for k in range(SLAB_NC):
            ys = ys + _dot(sr_ref[k], cbr_ref[k], "nn") - _dot(si_ref[k], cbi_ref[k], "nn")
        ys_ref[...] = ys

    slab, wspec, lspec, sspec, dspec = _s5_specs(L)
    sshape = jax.ShapeDtypeStruct((N_SLAB * SLAB_NC, L, 128), F32)
    return pl.pallas_call(
        body, name=name, grid=(N_SLAB,),
        in_specs=[slab, wspec, wspec, wspec, wspec, lspec, lspec, dspec],
        out_specs=[slab, sspec, sspec],
        out_shape=[jax.ShapeDtypeStruct((L, SSM_WIDTH), F32), sshape, sshape],
        scratch_shapes=[pltpu.VMEM((SLAB_NC, T, 128), F32), pltpu.VMEM((SLAB_NC, T, 128), F32)],
        compiler_params=_cp(("parallel",), 56),
    )(main, btr, bti, cbr, cbi, lbr, lbi, dvec)


def _s5_bwd(dys, main, sr, si, btr, bti, cbr, cbi, lbr, lbi, dvec, *, name):
    L = main.shape[0]
    T = L // SCAN_SEG

    def body(dys_ref, u_ref, sr_ref, si_ref, btr_ref, bti_ref, cbr_ref, cbi_ref, lr_ref, li_ref, d_ref,
             du_ref, dbtr_ref, dbti_ref, dcbr_ref, dcbi_ref, dlr_ref, dli_ref, dd_ref,
             ar_ref, ai_ref, pr_ref, pi_ref):
        dys = dys_ref[...]
        u = u_ref[...]
        for k in range(SLAB_NC):
            ar_ref[k] = _dot(dys, cbr_ref[k], "nt")
            ai_ref[k] = -_dot(dys, cbi_ref[k], "nt")
        _scan_inplace(ar_ref, ai_ref, lr_ref[...], -li_ref[...], pr_ref, pi_ref, reverse=True)
        first = lax.broadcasted_iota(jnp.int32, (L, 1), 0) == 0
        du = d_ref[...] * dys
        for k in range(SLAB_NC):
            a_re = ar_ref[k]
            a_im = ai_ref[k]
            du = du + _dot(a_re, btr_ref[k], "nn") + _dot(a_im, bti_ref[k], "nn")
            dbtr_ref[k] = _dot(a_re, u, "tn")
            dbti_ref[k] = _dot(a_im, u, "tn")
            s_re = sr_ref[k]
            s_im = si_ref[k]
            dcbr_ref[k] = _dot(s_re, dys, "tn")
            dcbi_ref[k] = -_dot(s_im, dys, "tn")
            p_re = jnp.where(first, 0.0, pltpu.roll(s_re, 1, 0))
            p_im = jnp.where(first, 0.0, pltpu.roll(s_im, 1, 0))
            dlr_ref[k] = jnp.sum(p_re * a_re + p_im * a_im, axis=0, keepdims=True)
            dli_ref[k] = jnp.sum(p_re * a_im - p_im * a_re, axis=0, keepdims=True)
        du_ref[...] = du
        dd_ref[...] = jnp.sum(dys * u, axis=0, keepdims=True)

    slab, wspec, lspec, sspec, dspec = _s5_specs(L)
    wshape = jax.ShapeDtypeStruct((N_SLAB * SLAB_NC, 128, SLAB), F32)
    lshape = jax.ShapeDtypeStruct((N_SLAB * SLAB_NC, 1, 128), F32)
    return pl.pallas_call(
        body, name=name, grid=(N_SLAB,),
        in_specs=[slab, slab, sspec, sspec, wspec, wspec, wspec, wspec, lspec, lspec, dspec],
        out_specs=[slab, wspec, wspec, wspec, wspec, lspec, lspec, dspec],
        out_shape=[jax.ShapeDtypeStruct((L, SSM_WIDTH), F32), wshape, wshape, wshape, wshape, lshape, lshape,
                   jax.ShapeDtypeStruct((1, SSM_WIDTH), F32)],
        scratch_shapes=[pltpu.VMEM((SLAB_NC, L, 128), F32), pltpu.VMEM((SLAB_NC, L, 128), F32),
                        pltpu.VMEM((SLAB_NC, T, 128), F32), pltpu.VMEM((SLAB_NC, T, 128), F32)],
        compiler_params=_cp(("parallel",), 56),
    )(dys, main, sr, si, btr, bti, cbr, cbi, lbr, lbi, dvec)


_SLAB_MASK = (np.arange(SLAB_CH)[:, None] // SSM_STATE == np.arange(SLAB)[None, :] // SSM_GROUP)


def _expand_bd(x):
    t = jnp.tile(x.reshape(N_SLAB, SLAB_CH, SSM_GROUP), (1, 1, SLAB // SSM_GROUP))
    return jnp.where(_SLAB_MASK[None], t, 0.0).astype(MXU).reshape(N_SLAB * SLAB_NC, 128, SLAB)


def _contract_bd(dx):
    t = jnp.where(_SLAB_MASK[None], dx.reshape(N_SLAB, SLAB_CH, SLAB), 0.0)
    return jnp.sum(t.reshape(N_SLAB, SLAB_CH, SLAB // SSM_GROUP, SSM_GROUP), axis=2).reshape(SSM_CH, SSM_GROUP)


def _glu_ew(ys, zlin, za):
    a1 = jax.nn.gelu(ys)
    return a1 * jax.nn.sigmoid(zlin) * _silu(za)


def _glu_fwd(ys, main, gw, gb, *, name, tm=256):
    L = ys.shape[0]
    tm = min(tm, L)
    W = SSM_WIDTH

    def body(ys_ref, za_ref, gw_ref, gb_ref, ya_ref):
        ys = ys_ref[...]
        a1 = jax.nn.gelu(ys)
        zlin = _dot(a1, gw_ref[...], "nn") + gb_ref[...]
        ya_ref[...] = _glu_ew(ys, zlin, za_ref[...]).astype(MXU)

    return pl.pallas_call(
        body, name=name, grid=(L // tm,),
        in_specs=[pl.BlockSpec((tm, W), lambda i: (i, 0)), pl.BlockSpec((tm, W), lambda i: (i, 1)),
                  pl.BlockSpec((W, W), lambda i: (0, 0)), pl.BlockSpec((1, W), lambda i: (0, 0))],
        out_specs=pl.BlockSpec((tm, W), lambda i: (i, 0)),
        out_shape=jax.ShapeDtypeStruct((L, W), MXU),
        compiler_params=_cp(("parallel",)),
    )(ys, main, gw, gb.reshape(1, W))


def _glu_bwd(dya, ys, main, gw, gb, *, name, tm=256):
    L = ys.shape[0]
    tm = min(tm, L)
    W = SSM_WIDTH

    def body(dya_ref, ys_ref, za_ref, gw_ref, gb_ref, dys_ref, dza_ref, a1_ref, dzl_ref, db_ref):
        ys = ys_ref[...]
        a1, gelu_vjp = jax.vjp(jax.nn.gelu, ys)
        zlin = _dot(a1, gw_ref[...], "nn") + gb_ref[...]
        _, vjp = jax.vjp(lambda a, z, za: a * jax.nn.sigmoid(z) * _silu(za), a1, zlin, za_ref[...])
        da1, dzlin, dza = vjp(dya_ref[...].astype(F32))
        da1 = da1 + _dot(dzlin, gw_ref[...], "nt")
        dys_ref[...] = gelu_vjp(da1)[0]
        dza_ref[...] = dza
        a1_ref[...] = a1.astype(MXU)
        dzl_ref[...] = dzlin.astype(MXU)

        @pl.when(pl.program_id(0) == 0)
        def _():
            db_ref[...] = jnp.zeros_like(db_ref)

        db_ref[...] += jnp.sum(dzlin, axis=0, keepdims=True)

    row = pl.BlockSpec((tm, W), lambda i: (i, 0))
    vec = pl.BlockSpec((1, W), lambda i: (0, 0))
    return pl.pallas_call(
        body, name=name, grid=(L // tm,),
        in_specs=[row, row, pl.BlockSpec((tm, W), lambda i: (i, 1)), pl.BlockSpec((W, W), lambda i: (0, 0)), vec],
        out_specs=[row, row, row, row, vec],
        out_shape=[jax.ShapeDtypeStruct((L, W), F32), jax.ShapeDtypeStruct((L, W), F32),
                   jax.ShapeDtypeStruct((L, W), MXU), jax.ShapeDtypeStruct((L, W), MXU),
                   jax.ShapeDtypeStruct((1, W), F32)],
        compiler_params=_cp(("arbitrary",)),
    )(dya, ys, main, gw, gb.reshape(1, W))


def _sg_fn(ub, vb, zb, lnw, lnb, ws, bs):
    u = jax.nn.gelu(ub)
    v = _layernorm(jax.nn.gelu(vb), lnw, lnb)
    r = lax.broadcasted_iota(jnp.int32, (SG_CHUNK, SG_CHUNK), 0)
    c = lax.broadcasted_iota(jnp.int32, (SG_CHUNK, SG_CHUNK), 1)
    tri = r >= c
    outs = []
    for h in range(SG_HEADS):
        wh = jnp.where(tri, ws[h], 0.0)
        outs.append(_mm_nn(wh, v[:, h * 128:(h + 1) * 128]) + bs[h])
    mixed = jnp.concatenate(outs, axis=1)
    return u * mixed * _silu(zb)


def _sg_specs(L):
    W = SSM_WIDTH
    blk = lambda c: pl.BlockSpec((SG_CHUNK, W), lambda i, c=c: (i, c))
    vec = pl.BlockSpec((1, W), lambda i: (0, 0))
    wspec = pl.BlockSpec((SG_HEADS, SG_CHUNK, SG_CHUNK), lambda i: (0, 0, 0))
    bspec = pl.BlockSpec((SG_HEADS, SG_CHUNK, 1), lambda i: (0, 0, 0))
    return blk, vec, wspec, bspec


def _sg_fwd(main, lnw, lnb, sgw, sgb, *, name):
    L = main.shape[0]
    W = SSM_WIDTH
    blk, vec, wspec, bspec = _sg_specs(L)

    def body(ub_ref, vb_ref, zb_ref, lnw_ref, lnb_ref, w_ref, b_ref, yb_ref):
        ws = [w_ref[h] for h in range(SG_HEADS)]
        bs = [b_ref[h] for h in range(SG_HEADS)]
        yb_ref[...] = _sg_fn(ub_ref[...], vb_ref[...], zb_ref[...], lnw_ref[...], lnb_ref[...], ws, bs).astype(MXU)

    return pl.pallas_call(
        body, name=name, grid=(L // SG_CHUNK,),
        in_specs=[blk(2), blk(3), blk(4), vec, vec, wspec, bspec],
        out_specs=pl.BlockSpec((SG_CHUNK, W), lambda i: (i, 0)),
        out_shape=jax.ShapeDtypeStruct((L, W), MXU),
        compiler_params=_cp(("parallel",)),
    )(main, main, main, lnw.reshape(1, W), lnb.reshape(1, W), sgw, sgb.reshape(SG_HEADS, SG_CHUNK, 1))


def _sg_bwd(dyb, main, lnw, lnb, sgw, sgb, *, name):
    L = main.shape[0]
    W = SSM_WIDTH
    blk, vec, wspec, bspec = _sg_specs(L)

    def body(dyb_ref, ub_ref, vb_ref, zb_ref, lnw_ref, lnb_ref, w_ref, b_ref,
             dub_ref, dvb_ref, dzb_ref, dlnw_ref, dlnb_ref, dw_ref, db_ref):
        ws = [w_ref[h] for h in range(SG_HEADS)]
        bs = [b_ref[h] for h in range(SG_HEADS)]
        _, vjp = jax.vjp(_sg_fn, ub_ref[...], vb_ref[...], zb_ref[...], lnw_ref[...], lnb_ref[...], ws, bs)
        dub, dvb, dzb, dlnw, dlnb, dws, dbs = vjp(dyb_ref[...])

        @pl.when(pl.program_id(0) == 0)
        def _():
            dlnw_ref[...] = jnp.zeros_like(dlnw_ref)
            dlnb_ref[...] = jnp.zeros_like(dlnb_ref)
            dw_ref[...] = jnp.zeros_like(dw_ref)
            db_ref[...] = jnp.zeros_like(db_ref)

        dub_ref[...] = dub
        dvb_ref[...] = dvb
        dzb_ref[...] = dzb
        dlnw_ref[...] += dlnw
        dlnb_ref[...] += dlnb
        for h in range(SG_HEADS):
            dw_ref[h] += dws[h]
            db_ref[h] += dbs[h]

    row = pl.BlockSpec((SG_CHUNK, W), lambda i: (i, 0))
    out = jax.ShapeDtypeStruct((L, W), F32)
    return pl.pallas_call(
        body, name=name, grid=(L // SG_CHUNK,),
        in_specs=[row, blk(2), blk(3), blk(4), vec, vec, wspec, bspec],
        out_specs=[row, row, row, vec, vec, wspec, bspec],
        out_shape=[out, out, out, jax.ShapeDtypeStruct((1, W), F32), jax.ShapeDtypeStruct((1, W), F32),
                   jax.ShapeDtypeStruct((SG_HEADS, SG_CHUNK, SG_CHUNK), F32),
                   jax.ShapeDtypeStruct((SG_HEADS, SG_CHUNK, 1), F32)],
        compiler_params=_cp(("arbitrary",)),
    )(dyb, main, main, main, lnw.reshape(1, W), lnb.reshape(1, W), sgw, sgb.reshape(SG_HEADS, SG_CHUNK, 1))


def _rope_tables(L):
    half = ROT_DIM // 2
    inv_freq = ROPE_THETA ** (-jnp.arange(0, ROT_DIM, 2, dtype=F32) / ROT_DIM)
    ang = jnp.arange(L, dtype=F32)[:, None] * inv_freq[None, :]
    cos = jnp.cos(ang)
    sin = jnp.sin(ang)
    ones = jnp.ones((L, HEAD_DIM - ROT_DIM), F32)
    cosf = jnp.concatenate([cos, cos, ones], axis=1)
    sinf = jnp.concatenate([sin, sin, 0.0 * ones], axis=1)
    rot = np.zeros((HEAD_DIM, HEAD_DIM), np.float32)
    for d in range(half):
        rot[d + half, d] = -1.0
        rot[d, d + half] = 1.0
    return cosf, sinf, jnp.asarray(rot)


def _rope(t, cosf, sinf, rot):
    shp = t.shape
    t2 = t.reshape(-1, HEAD_DIM)
    sw = lax.dot_general(t2, rot, _DIMS["nn"], precision=HIGHEST, preferred_element_type=F32).reshape(shp)
    return t * cosf + sw * sinf


def _attn_block_fn(q, kw, vw, sinks, cq, sq, ck, sk, rot, q0, k0):
    nk = kw.shape[1]
    qr = _rope(q, cq, sq, rot)
    kr = _rope(kw, ck, sk, rot)
    qpos = q0 + lax.broadcasted_iota(jnp.int32, (1, ATT_BLOCK, nk), 1)
    kpos = k0 + lax.broadcasted_iota(jnp.int32, (1, ATT_BLOCK, nk), 2)
    diff = qpos - kpos
    allowed = (diff >= 0) & (diff < WINDOW)
    outs = []
    for kh in range(ATT_KV_HEADS):
        qh = qr[kh * GQA_GROUP:(kh + 1) * GQA_GROUP].reshape(GQA_GROUP * ATT_BLOCK, HEAD_DIM)
        s = _mm_nt(qh, kr[kh]).reshape(GQA_GROUP, ATT_BLOCK, nk) * (HEAD_DIM ** -0.5)
        s = jnp.where(allowed, s, NEG_INF)
        sink = sinks[kh * GQA_GROUP:(kh + 1) * GQA_GROUP]
        m = lax.stop_gradient(jnp.maximum(jnp.max(s, axis=-1, keepdims=True), sink))
        e = jnp.exp(s - m)
        p = e / (jnp.sum(e, axis=-1, keepdims=True) + jnp.exp(sink - m))
        o = _mm_nn(p.reshape(GQA_GROUP * ATT_BLOCK, nk), vw[kh])
        outs.append(o.reshape(GQA_GROUP, ATT_BLOCK, HEAD_DIM))
    return jnp.concatenate(outs, axis=0)


def _attn_common(L):
    nwin = min(2 * ATT_BLOCK, L)
    qspec = pl.BlockSpec((ATT_HEADS, ATT_BLOCK, HEAD_DIM), lambda n: (0, n, 0))
    kvspec = pl.BlockSpec((ATT_KV_HEADS, L, HEAD_DIM), lambda n: (0, 0, 0))
    sspec = pl.BlockSpec((ATT_HEADS, 1, 1), lambda n: (0, 0, 0))
    tq = pl.BlockSpec((ATT_BLOCK, HEAD_DIM), lambda n: (n, 0))
    tk = pl.BlockSpec((L, HEAD_DIM), lambda n: (0, 0))
    rspec = pl.BlockSpec((HEAD_DIM, HEAD_DIM), lambda n: (0, 0))
    return nwin, qspec, kvspec, sspec, tq, tk, rspec


def _attn_fwd(qh, kh, vh, sinks, cosf, sinf, rot, *, name):
    L = qh.shape[1]
    nwin, qspec, kvspec, sspec, tq, tk, rspec = _attn_common(L)

    def body(q_ref, k_ref, v_ref, s_ref, cq_ref, sq_ref, ck_ref, sk_ref, r_ref, o_ref):
        n = pl.program_id(0)
        k0 = pl.multiple_of(jnp.maximum(n - 1, 0) * ATT_BLOCK, ATT_BLOCK)
        win = pl.ds(k0, nwin)
        o_ref[...] = _attn_block_fn(q_ref[...], k_ref[:, win, :], v_ref[:, win, :], s_ref[...],
                                    cq_ref[...], sq_ref[...], ck_ref[win, :], sk_ref[win, :], r_ref[...],
                                    n * ATT_BLOCK, k0)

    return pl.pallas_call(
        body, name=name, grid=(L // ATT_BLOCK,),
        in_specs=[qspec, kvspec, kvspec, sspec, tq, tq, tk, tk, rspec],
        out_specs=qspec,
        out_shape=jax.ShapeDtypeStruct((ATT_HEADS, L, HEAD_DIM), F32),
        compiler_params=_cp(("parallel",)),
    )(qh, kh, vh, sinks.reshape(ATT_HEADS, 1, 1), cosf, sinf, cosf, sinf, rot)


def _attn_bwd(do, qh, kh, vh, sinks, cosf, sinf, rot, *, name):
    L = qh.shape[1]
    nwin, qspec, kvspec, sspec, tq, tk, rspec = _attn_common(L)

    def body(do_ref, q_ref, k_ref, v_ref, s_ref, cq_ref, sq_ref, ck_ref, sk_ref, r_ref,
             dq_ref, dk_ref, dv_ref, ds_ref):
        n = pl.program_id(0)
        k0 = pl.multiple_of(jnp.maximum(n - 1, 0) * ATT_BLOCK, ATT_BLOCK)
        win = pl.ds(k0, nwin)
        cq, sq, ck, sk, rt = cq_ref[...], sq_ref[...], ck_ref[win, :], sk_ref[win, :], r_ref[...]
        q0 = n * ATT_BLOCK
        _, vjp = jax.vjp(lambda q, kw, vw, s: _attn_block_fn(q, kw, vw, s, cq, sq, ck, sk, rt, q0, k0),
                         q_ref[...], k_ref[:, win, :], v_ref[:, win, :], s_ref[...])
        dq, dkw, dvw, ds = vjp(do_ref[...])

        @pl.when(n == 0)
        def _():
            dk_ref[...] = jnp.zeros_like(dk_ref)
            dv_ref[...] = jnp.zeros_like(dv_ref)
            ds_ref[...] = jnp.zeros_like(ds_ref)

        dq_ref[...] = dq
        dk_ref[:, win, :] += dkw
        dv_ref[:, win, :] += dvw
        ds_ref[...] += ds

    return pl.pallas_call(
        body, name=name, grid=(L // ATT_BLOCK,),
        in_specs=[qspec, qspec, kvspec, kvspec, sspec, tq, tq, tk, tk, rspec],
        out_specs=[qspec, kvspec, kvspec, sspec],
        out_shape=[jax.ShapeDtypeStruct((ATT_HEADS, L, HEAD_DIM), F32),
                   jax.ShapeDtypeStruct((ATT_KV_HEADS, L, HEAD_DIM), F32),
                   jax.ShapeDtypeStruct((ATT_KV_HEADS, L, HEAD_DIM), F32),
                   jax.ShapeDtypeStruct((ATT_HEADS, 1, 1), F32)],
        compiler_params=_cp(("arbitrary",)),
    )(do, qh, kh, vh, sinks.reshape(ATT_HEADS, 1, 1), cosf, sinf, cosf, sinf, rot)


def _to_heads(t, nh):
    L = t.shape[0]
    return t.reshape(L, nh, HEAD_DIM).transpose(1, 0, 2)


def _from_heads(t):
    nh, L, _ = t.shape
    return t.transpose(1, 0, 2).reshape(L, nh * HEAD_DIM)


def _branch_fwd(ya, yb, o2d, zc, gates, wa, wb, wc, *, name, tm=256):
    L = ya.shape[0]
    tm = min(tm, L)
    W, D = SSM_WIDTH, D_MODEL

    def body(ya_ref, yb_ref, o_ref, zc_ref, g0_ref, g1_ref, g2_ref, wa_ref, wb_ref, wc_ref,
             mg_ref, ta_ref, tb_ref, tc_ref, yc_ref):
        yc = (o_ref[...] * _silu(zc_ref[...])).astype(MXU)
        ta = _dot(ya_ref[...], wa_ref[...], "nt")
        tb = _dot(yb_ref[...], wb_ref[...], "nt")
        tc = _dot(yc, wc_ref[...], "nt")
        ta_ref[...] = ta
        tb_ref[...] = tb
        tc_ref[...] = tc
        yc_ref[...] = yc
        mg_ref[...] = (jax.nn.sigmoid(g0_ref[...]) * ta + jax.nn.sigmoid(g1_ref[...]) * tb
                       + jax.nn.sigmoid(g2_ref[...]) * tc).astype(MXU)

    row = pl.BlockSpec((tm, W), lambda i: (i, 0))
    wide = pl.BlockSpec((tm, D), lambda i: (i, 0))
    gate = lambda c: pl.BlockSpec((tm, D), lambda i, c=c: (i, c))
    wspec = pl.BlockSpec((D, W), lambda i: (0, 0))
    return pl.pallas_call(
        body, name=name, grid=(L // tm,),
        in_specs=[row, row, row, row, gate(0), gate(1), gate(2), wspec, wspec, wspec],
        out_specs=[wide, wide, wide, wide, row],
        out_shape=[jax.ShapeDtypeStruct((L, D), MXU), jax.ShapeDtypeStruct((L, D), F32),
                   jax.ShapeDtypeStruct((L, D), F32), jax.ShapeDtypeStruct((L, D), F32),
                   jax.ShapeDtypeStruct((L, W), MXU)],
        compiler_params=_cp(("parallel",), 56),
    )(ya, yb, o2d, zc, gates, gates, gates, wa, wb, wc)


def _branch_bwd(dmg, ta, tb, tc, gates, *, name, tm=256):
    L = dmg.shape[0]
    tm = min(tm, L)
    D = D_MODEL

    def body(dm_ref, ta_ref, tb_ref, tc_ref, g0_ref, g1_ref, g2_ref, da_ref, db_ref, dc_ref, dg_ref):
        dm = dm_ref[...]
        for i, (t_ref, g_ref, d_ref) in enumerate(((ta_ref, g0_ref, da_ref), (tb_ref, g1_ref, db_ref),
                                                   (tc_ref, g2_ref, dc_ref))):
            sg = jax.nn.sigmoid(g_ref[...])
            d_ref[...] = (sg * dm).astype(MXU)
            dg_ref[:, i * D:(i + 1) * D] = (dm * t_ref[...] * sg * (1.0 - sg)).astype(MXU)

    wide = pl.BlockSpec((tm, D), lambda i: (i, 0))
    gate = lambda c: pl.BlockSpec((tm, D), lambda i, c=c: (i, c))
    bf = jax.ShapeDtypeStruct((L, D), MXU)
    return pl.pallas_call(
        body, name=name, grid=(L // tm,),
        in_specs=[wide, wide, wide, wide, gate(0), gate(1), gate(2)],
        out_specs=[wide, wide, wide, pl.BlockSpec((tm, 3 * D), lambda i: (i, 0))],
        out_shape=[bf, bf, bf, jax.ShapeDtypeStruct((L, 3 * D), MXU)],
        compiler_params=_cp(("parallel",), 56),
    )(dmg, ta, tb, tc, gates, gates, gates)


def _gate_c_bwd(dyc, o2d, zc, *, name, tm=256):
    L, W = dyc.shape
    tm = min(tm, L)

    def body(dy_ref, o_ref, z_ref, do_ref, dz_ref):
        _, vjp = jax.vjp(lambda o, z: o * _silu(z), o_ref[...], z_ref[...])
        do, dz = vjp(dy_ref[...])
        do_ref[...] = do
        dz_ref[...] = dz.astype(MXU)

    row = pl.BlockSpec((tm, W), lambda i: (i, 0))
    return pl.pallas_call(body, name=name, grid=(L // tm,), in_specs=[row, row, row], out_specs=[row, row],
                          out_shape=[jax.ShapeDtypeStruct((L, W), F32), jax.ShapeDtypeStruct((L, W), MXU)],
                          compiler_params=_cp(("parallel",)))(dyc, o2d, zc)


def _adamw(w, g, m, v, *, name):
    shape = w.shape
    cols = shape[-1]
    w2, g2, m2, v2 = (t.reshape(-1, cols) for t in (w, g, m, v))
    rows = w2.shape[0]
    tc = 1024 if cols % 1024 == 0 else cols
    lane_cols = -(-tc // 128) * 128
    tr = rows
    while tr % 16 == 0 and tr * lane_cols * 4 > 2 * _MB:
        tr //= 2

    def body(w_ref, g_ref, m_ref, v_ref, d_ref, nm_ref, nv_ref):
        gv = g_ref[...]
        nm = ADAM_B1 * m_ref[...] + (1.0 - ADAM_B1) * gv
        nv = ADAM_B2 * v_ref[...] + (1.0 - ADAM_B2) * jnp.square(gv)
        m_hat = nm / (1.0 - ADAM_B1 ** ADAM_STEP)
        v_hat = nv / (1.0 - ADAM_B2 ** ADAM_STEP)
        d_ref[...] = -ADAM_LR * (m_hat / (jnp.sqrt(v_hat) + ADAM_EPS) + ADAM_WD * w_ref[...])
        nm_ref[...] = nm
        nv_ref[...] = nv

    spec = pl.BlockSpec((tr, tc), lambda i, j: (i, j))
    out = jax.ShapeDtypeStruct((rows, cols), F32)
    d, nm, nv = pl.pallas_call(body, name=name, grid=(rows // tr, cols // tc), in_specs=[spec] * 4,
                               out_specs=[spec] * 3, out_shape=[out, out, out],
                               compiler_params=_cp(("parallel", "parallel")))(w2, g2, m2, v2)
    return d.reshape(shape), nm.reshape(shape), nv.reshape(shape)


def _prep_layer(p, l):
    are = p["ssm_a_re"][l].reshape(SSM_CH, 1)
    aim = p["ssm_a_im"][l].reshape(SSM_CH, 1)
    ldt = p["ssm_log_dt"][l].reshape(1, SSM_GROUPS)
    bre = p["ssm_b_re"][l].reshape(SSM_CH, SSM_GROUP)
    bim = p["ssm_b_im"][l].reshape(SSM_CH, SSM_GROUP)
    lbr, lbi, bbr, bbi = _s5_params_fwd(are, aim, ldt, bre, bim, name=f"s5_params_fwd_{l}")
    cre = p["ssm_c_re"][l].transpose(0, 2, 1).reshape(SSM_CH, SSM_GROUP)
    cim = p["ssm_c_im"][l].transpose(0, 2, 1).reshape(SSM_CH, SSM_GROUP)
    return dict(raw=(are, aim, ldt, bre, bim),
                lbr=lbr.reshape(N_SLAB * SLAB_NC, 1, 128), lbi=lbi.reshape(N_SLAB * SLAB_NC, 1, 128),
                btr=_expand_bd(bbr), bti=_expand_bd(bbi), cbr=_expand_bd(cre), cbi=_expand_bd(cim),
                dvec=p["ssm_d"][l].reshape(1, SSM_WIDTH))


def _layer_fwd(x, p, big, l, tabs):
    L = x.shape[0]
    cosf, sinf, rot = tabs
    sp = _prep_layer(p, l)
    h = _rms_fwd(x, p["norm_w"][l], name=f"rms_fwd_{l}")
    mm = functools.partial(_matmul, h, big["winT"], "nt", tm=L, tn=256, tk=D_MODEL)
    main = mm(name=f"proj_main_{l}", shape=(L, N_MAIN, D_MODEL))
    zc = mm(name=f"proj_zc_{l}", shape=(L, N_ZC, D_MODEL), b_off=(N_MAIN // 256, 0))
    gates = mm(name=f"proj_gates_{l}", shape=(L, N_GATES, D_MODEL), b_off=((N_MAIN + N_ZC) // 256, 0))
    ys, sr, si = _s5_fwd(main, sp["btr"], sp["bti"], sp["cbr"], sp["cbi"], sp["lbr"], sp["lbi"], sp["dvec"],
                         name=f"s5_fwd_{l}")
    ya = _glu_fwd(ys, main, big["glu_w"], p["ssm_glu_b"][l], name=f"glu_fwd_{l}")
    yb = _sg_fwd(main, p["sg_ln_w"][l], p["sg_ln_b"][l], p["sg_w"][l], p["sg_b"][l], name=f"sg_fwd_{l}")
    qh = _to_heads(main[:, 5120:6144], ATT_HEADS)
    kh = _to_heads(main[:, 6144:6272], ATT_KV_HEADS)
    vh = _to_heads(main[:, 6272:6400], ATT_KV_HEADS)
    oh = _attn_fwd(qh, kh, vh, p["attn_sinks"][l], cosf, sinf, rot, name=f"attn_fwd_{l}")
    o2d = _from_heads(oh)
    mg, ta, tb, tc, yc = _branch_fwd(ya, yb, o2d, zc, gates, big["wbaT"], big["wbbT"], big["wbcT"],
                                     name=f"branch_fwd_{l}")
    xn = _matmul(mg, big["w_out"], "nn", name=f"out_fwd_{l}", shape=(L, D_MODEL, D_MODEL), tm=512, tn=512,
                 tk=D_MODEL, add=x)
    saved = dict(x=x, h=h, main=main, zc=zc, gates=gates, ys=ys, sr=sr, si=si, ya=ya, yb=yb, yc=yc, o2d=o2d,
                 qh=qh, kh=kh, vh=vh, mg=mg, ta=ta, tb=tb, tc=tc, sp=sp)
    return xn, saved


def _layer_bwd(dxn, s, p, big, l, tabs):
    L = dxn.shape[0]
    D, W = D_MODEL, SSM_WIDTH
    cosf, sinf, rot = tabs
    sp = s["sp"]
    g = {}
    dmg = _matmul(dxn, big["w_out"], "nt", name=f"out_bwd_dm_{l}", shape=(L, D, D), tm=512, tn=512, tk=D)
    g["w_out"] = _matmul(s["mg"], dxn, "tn", name=f"out_bwd_dw_{l}", shape=(D, D, L), tm=512, tn=512, tk=L)
    dta, dtb, dtc, dgates = _branch_bwd(dmg, s["ta"], s["tb"], s["tc"], s["gates"], name=f"branch_bwd_{l}")
    dys_ = {}
    for nm, dt, y, wt in (("a", dta, s["ya"], big["wbaT"]), ("b", dtb, s["yb"], big["wbbT"]),
                          ("c", dtc, s["yc"], big["wbcT"])):
        dys_[nm] = _matmul(dt, wt, "nn", name=f"branch_bwd_dy{nm}_{l}", shape=(L, W, D), tm=512, tn=512, tk=D)
        g["wb" + nm + "T"] = _matmul(dt, y, "tn", name=f"branch_bwd_dw{nm}_{l}", shape=(D, W, L),
                                     tm=512, tn=512, tk=L)
    do2d, dzc = _gate_c_bwd(dys_["c"], s["o2d"], s["zc"], name=f"gate_c_bwd_{l}")
    dqh, dkh, dvh, dsinks = _attn_bwd(_to_heads(do2d, ATT_HEADS), s["qh"], s["kh"], s["vh"], p["attn_sinks"][l],
                                      cosf, sinf, rot, name=f"attn_bwd_{l}")
    g["attn_sinks"] = dsinks.reshape(ATT_HEADS)
    dub, dvb, dzb, dlnw, dlnb, dsgw, dsgb = _sg_bwd(dys_["b"], s["main"], p["sg_ln_w"][l], p["sg_ln_b"][l],
                                                    p["sg_w"][l], p["sg_b"][l], name=f"sg_bwd_{l}")
    g["sg_ln_w"], g["sg_ln_b"] = dlnw.reshape(W), dlnb.reshape(W)
    g["sg_w"], g["sg_b"] = dsgw, dsgb.reshape(SG_HEADS, SG_CHUNK)
    dys, dza, a1, dzl, dgb = _glu_bwd(dys_["a"], s["ys"], s["main"], big["glu_w"], p["ssm_glu_b"][l],
                                      name=f"glu_bwd_{l}")
    g["ssm_glu_b"] = dgb.reshape(W)
    g["glu_w"] = _matmul(a1, dzl, "tn", name=f"glu_bwd_dw_{l}", shape=(W, W, L), tm=512, tn=512, tk=L)
    dua, dbtr, dbti, dcbr, dcbi, dlr, dli, dd = _s5_bwd(dys, s["main"], s["sr"], s["si"], sp["btr"], sp["bti"],
                                                        sp["cbr"], sp["cbi"], sp["lbr"], sp["lbi"], sp["dvec"],
                                                        name=f"s5_bwd_{l}")
    g["ssm_d"] = dd.reshape(W)
    to_c = lambda t: _contract_bd(t).reshape(SSM_GROUPS, SSM_STATE, SSM_GROUP).transpose(0, 2, 1)
    g["ssm_c_re"], g["ssm_c_im"] = to_c(dcbr), to_c(dcbi)
    dare, daim, dldt, dbre, dbim = _s5_params_bwd(*sp["raw"], dlr.reshape(SSM_CH, 1), dli.reshape(SSM_CH, 1),
                                                  _contract_bd(dbtr), _contract_bd(dbti),
                                                  name=f"s5_params_bwd_{l}")
    g["ssm_a_re"] = dare.reshape(SSM_GROUPS, SSM_STATE)
    g["ssm_a_im"] = daim.reshape(SSM_GROUPS, SSM_STATE)
    g["ssm_log_dt"] = dldt.reshape(SSM_GROUPS)
    g["ssm_b_re"] = dbre.reshape(SSM_GROUPS, SSM_STATE, SSM_GROUP)
    g["ssm_b_im"] = dbim.reshape(SSM_GROUPS, SSM_STATE, SSM_GROUP)
    dproj = jnp.concatenate([t.astype(MXU) for t in (dua, dza, dub, dvb, dzb, _from_heads(dqh), _from_heads(dkh),
                                                     _from_heads(dvh), dzc, dgates)], axis=1)
    dh = _matmul(dproj, big["winT"], "nn", name=f"proj_bwd_dh_{l}", shape=(L, D, D_IN), tm=512, tn=512, tk=D_IN // 2)
    g["winT"] = _matmul(dproj, s["h"], "tn", name=f"proj_bwd_dw_{l}", shape=(D_IN, D, L), tm=256, tn=D, tk=L)
    dx, dnw = _rms_bwd(s["x"], p["norm_w"][l], dh, dxn, name=f"rms_bwd_{l}")
    g["norm_w"] = dnw
    return dx, g


MESH = pl.DeviceIdType.MESH
_ANY = pl.BlockSpec(memory_space=pl.ANY)
ROW_ALIGN = 16


def _coords():
    return lax.axis_index("x"), lax.axis_index("y"), lax.axis_index("c")


def _gather8(arrs, *, split, name):
    n = len(arrs)
    rows = [a.shape[0] // 2 if split else a.shape[0] for a in arrs]
    for r in rows:
        assert r % ROW_ALIGN == 0

    def body(*refs):
        ins, outs = refs[:n], refs[n:2 * n]
        send, recv, lsem = refs[2 * n:]
        x, y, c = _coords()
        me, sibling = (x, y, c), (x, y, 1 - c)
        chips = [(1 - x, y), (x, 1 - y), (1 - x, 1 - y)]

        def blk(a, px, py, pc):
            return outs[a].at[pl.ds(pl.multiple_of((4 * px + 2 * py + pc) * rows[a], ROW_ALIGN), rows[a]), :]

        def own(a):
            if split:
                return ins[a].at[pl.ds(pl.multiple_of(c * rows[a], ROW_ALIGN), rows[a]), :]
            return ins[a]

        def copy(a, k, block, to, src=None):
            return pltpu.make_async_remote_copy(
                src_ref=blk(a, *block) if src is None else src, dst_ref=blk(a, *block),
                send_sem=send.at[a, k], recv_sem=recv.at[a, k], device_id=to, device_id_type=MESH)

        mine, first, passed = [], [], []
        for a in range(n):
            mine.append(pltpu.make_async_copy(own(a), blk(a, *me), lsem.at[a]))
            mine[a].start()
            f = [copy(a, 0, me, sibling, src=own(a))]
            f += [copy(a, 1 + j, me, (*chip, c), src=own(a)) for j, chip in enumerate(chips)]
            for cp in f:
                cp.start()
            first.append(f)
        for a in range(n):
            ps = [copy(a, 4 + j, (*chip, c), sibling) for j, chip in enumerate(chips)]
            for j, chip in enumerate(chips):
                copy(a, 1 + j, (*chip, c), me).wait_recv()
                ps[j].start()
            passed.append(ps)
        for a in range(n):
            copy(a, 0, sibling, me).wait_recv()
            for j, chip in enumerate(chips):
                copy(a, 4 + j, (*chip, 1 - c), me).wait_recv()
            for cp in first[a] + passed[a]:
                cp.wait_send()
            mine[a].wait()

    return pl.pallas_call(
        body, name=name,
        in_specs=[_ANY] * n, out_specs=[_ANY] * n,
        out_shape=[jax.ShapeDtypeStruct((8 * r,) + a.shape[1:], a.dtype) for r, a in zip(rows, arrs)],
        scratch_shapes=[pltpu.SemaphoreType.DMA((n, 7)), pltpu.SemaphoreType.DMA((n, 7)), pltpu.SemaphoreType.DMA((n,))],
    )(*arrs)


def _sibling_swap(arrs, *, pick_other, name):
    n = len(arrs)

    def body(*refs):
        ins, outs = refs[:n], refs[n:2 * n]
        send, recv = refs[2 * n:]
        x, y, c = _coords()
        sel = (1 - c) if pick_other else c
        cps = [pltpu.make_async_remote_copy(src_ref=ins[a].at[:, sel], dst_ref=outs[a], send_sem=send.at[a],
                                            recv_sem=recv.at[a], device_id=(x, y, 1 - c), device_id_type=MESH)
               for a in range(n)]
        for cp in cps:
            cp.start()
        for cp in cps:
            cp.wait_recv()
        for cp in cps:
            cp.wait_send()

    return pl.pallas_call(
        body, name=name, in_specs=[_ANY] * n, out_specs=[_ANY] * n,
        out_shape=[jax.ShapeDtypeStruct((a.shape[0],) + a.shape[2:], a.dtype) for a in arrs],
        scratch_shapes=[pltpu.SemaphoreType.DMA((n,)), pltpu.SemaphoreType.DMA((n,))],
    )(*arrs)


def _col_tile(lead, rows, cols, itemsize=4, cap=4 * _MB):
    tc = cols
    while tc % 256 == 0 and lead * rows * tc * itemsize > cap:
        tc //= 2
    return tc


def _pair_sum(mine, theirs, *, name):
    _, _, rows, cols = mine.shape
    tc = _col_tile(1, rows, cols)
    c = lax.axis_index("c")

    def body(c_ref, a_ref, b_ref, o_ref):
        o_ref[...] = (a_ref[...] + b_ref[...]).astype(MXU)

    return pl.pallas_call(
        body, name=name,
        grid_spec=pltpu.PrefetchScalarGridSpec(
            num_scalar_prefetch=1, grid=(4, cols // tc),
            in_specs=[pl.BlockSpec((None, None, rows, tc), lambda j, i, cr: (j, cr[0], 0, i)),
                      pl.BlockSpec((None, rows, tc), lambda j, i, cr: (j, 0, i))],
            out_specs=pl.BlockSpec((None, rows, tc), lambda j, i, cr: (j, 0, i))),
        out_shape=jax.ShapeDtypeStruct((4, rows, cols), MXU),
        compiler_params=_cp(("parallel", "parallel")),
    )(c.reshape(1).astype(jnp.int32), mine, theirs)


_HBM = pl.BlockSpec(memory_space=pltpu.HBM)
_SEM = pl.BlockSpec(memory_space=pltpu.SEMAPHORE)
_EFFECT = pltpu.SideEffectType.DATAFLOW_SIDE_EFFECTING
N_PEER_CHIPS = 3


def _peer_chips(x, y):
    return [(1 - x, y), (x, 1 - y), (1 - x, 1 - y)]


def _split_start(srcs, lands, src_slot, dst_slot, *, name):
    n = len(srcs)
    ns = n * N_PEER_CHIPS

    def body(*refs):
        src_refs, land_refs = refs[:n], refs[n:2 * n]
        send, recv, token = refs[2 * n:2 * n + ns], refs[2 * n + ns:2 * n + 2 * ns], refs[-1]
        x, y, c = _coords()
        for a in range(n):
            for k, (px, py) in enumerate(_peer_chips(x, y)):
                pltpu.make_async_remote_copy(
                    src_ref=src_refs[a].at[src_slot(x, y, c, px, py)], dst_ref=land_refs[a].at[dst_slot(x, y, c)],
                    send_sem=send[a * N_PEER_CHIPS + k], recv_sem=recv[a * N_PEER_CHIPS + k],
                    device_id=(px, py, c), device_id_type=MESH).start()
        token[...] = jnp.zeros_like(token)

    bufs = list(srcs) + list(lands)
    res = pl.pallas_call(
        body, name=name,
        out_shape=(*[pltpu.SemaphoreType.DMA(())] * (2 * ns), *[pltpu.HBM(b.shape, b.dtype) for b in bufs],
                   jax.ShapeDtypeStruct((8, 128), F32)),
        in_specs=[_HBM] * (2 * n),
        out_specs=(*[_SEM] * (2 * ns), *[_HBM] * (2 * n), pl.BlockSpec(memory_space=pltpu.VMEM)),
        input_output_aliases={i: 2 * ns + i for i in range(2 * n)},
        compiler_params=pltpu.CompilerParams(has_side_effects=_EFFECT),
    )(*[pltpu.with_memory_space_constraint(b, pltpu.HBM) for b in bufs])
    sems = list(res[:2 * ns])
    return sems, list(res[2 * ns:2 * ns + n]), list(res[2 * ns + n:2 * ns + 2 * n]), res[-1]


def _split_wait(sems, srcs, lands, after, *, name):
    n = len(srcs)
    ns = n * N_PEER_CHIPS

    def body(*refs):
        src_refs, land_refs = refs[:n], refs[n:2 * n]
        send, recv = refs[2 * n:2 * n + ns], refs[2 * n + ns:2 * n + 2 * ns]
        x, y, c = _coords()
        for a in range(n):
            for k in range(N_PEER_CHIPS):
                cp = pltpu.make_async_remote_copy(
                    src_ref=src_refs[a].at[0], dst_ref=land_refs[a].at[0], send_sem=send[a * N_PEER_CHIPS + k],
                    recv_sem=recv[a * N_PEER_CHIPS + k], device_id=(x, y, 1 - c), device_id_type=MESH)
                cp.wait_send()
                cp.wait_recv()

    bufs = list(srcs) + list(lands)
    res = pl.pallas_call(
        body, name=name,
        out_shape=tuple(pltpu.HBM(b.shape, b.dtype) for b in bufs),
        in_specs=[_HBM] * (2 * n) + [_SEM] * (2 * ns) + [_ANY],
        out_specs=tuple([_HBM] * (2 * n)),
        input_output_aliases={i: i for i in range(2 * n)},
        compiler_params=pltpu.CompilerParams(has_side_effects=_EFFECT),
    )(*bufs, *sems, after)
    return list(res[:n]), list(res[n:])


def _fill_own(shard2, *, name):
    _, rows, cols = shard2.shape
    tc = _col_tile(1, rows, cols, itemsize=shard2.dtype.itemsize)
    j = 2 * lax.axis_index("x") + lax.axis_index("y")

    def body(j_ref, s_ref, o_ref):
        o_ref[...] = s_ref[...]

    return pl.pallas_call(
        body, name=name,
        grid_spec=pltpu.PrefetchScalarGridSpec(
            num_scalar_prefetch=1, grid=(2, cols // tc),
            in_specs=[pl.BlockSpec((None, rows, tc), lambda h, i, jr: (h, 0, i))],
            out_specs=pl.BlockSpec((None, rows, tc), lambda h, i, jr: (2 * jr[0] + h, 0, i))),
        out_shape=jax.ShapeDtypeStruct((8, rows, cols), shard2.dtype),
        compiler_params=_cp(("parallel", "parallel")),
    )(j.reshape(1).astype(jnp.int32), shard2)


def _pass_to_sibling(lands, *, name):
    n = len(lands)

    def body(*refs):
        outs = refs[n:2 * n]
        send, recv = refs[2 * n:]
        x, y, c = _coords()
        cps = []
        for a in range(n):
            for k, (px, py) in enumerate(_peer_chips(x, y)):
                slot = 4 * px + 2 * py + c
                cps.append(pltpu.make_async_remote_copy(
                    src_ref=outs[a].at[slot], dst_ref=outs[a].at[slot], send_sem=send.at[a, k], recv_sem=recv.at[a, k],
                    device_id=(x, y, 1 - c), device_id_type=MESH))
        for cp in cps:
            cp.start()
        for cp in cps:
            cp.wait_recv()
        for cp in cps:
            cp.wait_send()

    return pl.pallas_call(
        body, name=name, in_specs=[_ANY] * n, out_specs=[_ANY] * n,
        out_shape=[jax.ShapeDtypeStruct(b.shape, b.dtype) for b in lands],
        input_output_aliases={a: a for a in range(n)},
        scratch_shapes=[pltpu.SemaphoreType.DMA((n, N_PEER_CHIPS)), pltpu.SemaphoreType.DMA((n, N_PEER_CHIPS))],
    )(*lands)


def _sum_parts(parts, got, *, name):
    _, rows, cols = parts.shape
    tc = _col_tile(4, rows, cols, itemsize=parts.dtype.itemsize)
    x, y, c = _coords()
    idx = jnp.stack([2 * x + y, 2 * (1 - x) + y, 2 * x + (1 - y), 2 * (1 - x) + (1 - y), c]).astype(jnp.int32)

    def body(i_ref, p_ref, g0_ref, g1_ref, g2_ref, o_ref):
        o_ref[...] = ((p_ref[...].astype(F32) + g0_ref[...].astype(F32)) + g1_ref[...].astype(F32)) + g2_ref[...].astype(F32)

    slot = lambda s: pl.BlockSpec((None, rows, tc), lambda i, ir, s=s: (ir[s], 0, i))
    return pl.pallas_call(
        body, name=name,
        grid_spec=pltpu.PrefetchScalarGridSpec(
            num_scalar_prefetch=1, grid=(cols // tc,),
            in_specs=[slot(0), slot(1), slot(2), slot(3)],
            out_specs=pl.BlockSpec((None, rows, tc), lambda i, ir: (ir[4], 0, i))),
        out_shape=jax.ShapeDtypeStruct((2, rows, cols), F32),
        compiler_params=_cp(("parallel",)),
    )(idx, parts, got, got, got)


def _sum_slots(t, *, name):
    S, rows, cols = t.shape
    tc = _col_tile(S, rows, cols)

    def body(t_ref, o_ref):
        acc = t_ref[0].astype(F32)
        for s in range(1, S):
            acc = acc + t_ref[s].astype(F32)
        o_ref[...] = acc

    return pl.pallas_call(
        body, name=name, grid=(cols // tc,),
        in_specs=[pl.BlockSpec((S, rows, tc), lambda i: (0, 0, i))],
        out_specs=pl.BlockSpec((rows, tc), lambda i: (0, i)),
        out_shape=jax.ShapeDtypeStruct((rows, cols), F32),
        compiler_params=_cp(("parallel",)),
    )(t)


def _halves_join(bufs, *, name):
    n = len(bufs)

    def body(*refs):
        outs = refs[n:2 * n]
        send, recv = refs[2 * n:]
        x, y, c = _coords()
        cps = [pltpu.make_async_remote_copy(src_ref=outs[a].at[c], dst_ref=outs[a].at[c], send_sem=send.at[a],
                                            recv_sem=recv.at[a], device_id=(x, y, 1 - c), device_id_type=MESH)
               for a in range(n)]
        for cp in cps:
            cp.start()
        for cp in cps:
            cp.wait_recv()
        for cp in cps:
            cp.wait_send()

    return pl.pallas_call(
        body, name=name, in_specs=[_ANY] * n, out_specs=[_ANY] * n,
        out_shape=[jax.ShapeDtypeStruct(b.shape, b.dtype) for b in bufs],
        input_output_aliases={a: a for a in range(n)},
        scratch_shapes=[pltpu.SemaphoreType.DMA((n,)), pltpu.SemaphoreType.DMA((n,))],
    )(*bufs)


def _reduce_scatter_begin(grads, *, tag):
    views = [g.reshape(4, 2, g.shape[0] // 8, g.shape[1]) for g in grads]
    theirs = _sibling_swap(views, pick_other=True, name=f"rs_swap_{tag}")
    parts = [_pair_sum(v, t, name=f"rs_pair_{tag}_{i}") for i, (v, t) in enumerate(zip(views, theirs))]
    got = [lax.empty(p.shape, p.dtype) for p in parts]
    sems, parts, got, token = _split_start(
        parts, got, lambda x, y, c, px, py: 2 * px + py, lambda x, y, c: 2 * x + y, name=f"rs_start_{tag}")
    return (sems, parts, got), token


def _reduce_scatter_end(state, after, *, tag):
    sems, parts, got = state
    parts, got = _split_wait(sems, parts, got, after, name=f"rs_wait_{tag}")
    halves = [_sum_parts(p, t, name=f"rs_sum_{tag}_{i}") for i, (p, t) in enumerate(zip(parts, got))]
    joined = _halves_join(halves, name=f"rs_join_{tag}")
    return [j.reshape(2 * j.shape[1], j.shape[2]) for j in joined]


_SMALL = ("norm_w", "ssm_a_re", "ssm_a_im", "ssm_log_dt", "ssm_b_re", "ssm_b_im", "ssm_c_re", "ssm_c_im", "ssm_d",
          "ssm_glu_b", "sg_ln_w", "sg_ln_b", "sg_w", "sg_b", "attn_sinks", "final_norm_w")
_BIG = ("w_in", "ssm_glu_w", "w_branch_a", "w_branch_b", "w_branch_c", "w_out")
_WEIGHTS = ("norm_w", "w_in", "ssm_a_re", "ssm_a_im", "ssm_log_dt", "ssm_b_re", "ssm_b_im", "ssm_c_re", "ssm_c_im",
            "ssm_d", "ssm_glu_w", "ssm_glu_b", "sg_ln_w", "sg_ln_b", "sg_w", "sg_b", "attn_sinks", "w_branch_a",
            "w_branch_b", "w_branch_c", "w_out", "final_norm_w")
_PACK_COLS = 1024
_PACK_ALIGN = 16 * _PACK_COLS


def _pack(ts):
    flat = jnp.concatenate([t.reshape(-1) for t in ts])
    pad = (-flat.shape[0]) % _PACK_ALIGN
    return jnp.pad(flat, (0, pad)).reshape(-1, _PACK_COLS)


def _unpack(buf, like):
    flat = buf.reshape(-1)
    out, pos = [], 0
    for t in like:
        out.append(flat[pos:pos + t.size].reshape(t.shape))
        pos += t.size
    return out


def kernel(x, norm_w, w_in, ssm_a_re, ssm_a_im, ssm_log_dt, ssm_b_re, ssm_b_im, ssm_c_re, ssm_c_im, ssm_d, ssm_glu_w, ssm_glu_b, sg_ln_w, sg_ln_b, sg_w, sg_b, attn_sinks, w_branch_a, w_branch_b, w_branch_c, w_out, final_norm_w, loss_target, m_norm_w, m_w_in, m_ssm_a_re, m_ssm_a_im, m_ssm_log_dt, m_ssm_b_re, m_ssm_b_im, m_ssm_c_re, m_ssm_c_im, m_ssm_d, m_ssm_glu_w, m_ssm_glu_b, m_sg_ln_w, m_sg_ln_b, m_sg_w, m_sg_b, m_attn_sinks, m_w_branch_a, m_w_branch_b, m_w_branch_c, m_w_out, m_final_norm_w, v_norm_w, v_w_in, v_ssm_a_re, v_ssm_a_im, v_ssm_log_dt, v_ssm_b_re, v_ssm_b_im, v_ssm_c_re, v_ssm_c_im, v_ssm_d, v_ssm_glu_w, v_ssm_glu_b, v_sg_ln_w, v_sg_ln_b, v_sg_w, v_sg_b, v_attn_sinks, v_w_branch_a, v_w_branch_b, v_w_branch_c, v_w_out, v_final_norm_w):
    w = dict(norm_w=norm_w, w_in=w_in, ssm_a_re=ssm_a_re, ssm_a_im=ssm_a_im, ssm_log_dt=ssm_log_dt, ssm_b_re=ssm_b_re,
             ssm_b_im=ssm_b_im, ssm_c_re=ssm_c_re, ssm_c_im=ssm_c_im, ssm_d=ssm_d, ssm_glu_w=ssm_glu_w,
             ssm_glu_b=ssm_glu_b, sg_ln_w=sg_ln_w, sg_ln_b=sg_ln_b, sg_w=sg_w, sg_b=sg_b, attn_sinks=attn_sinks,
             w_branch_a=w_branch_a, w_branch_b=w_branch_b, w_branch_c=w_branch_c, w_out=w_out,
             final_norm_w=final_norm_w)
    m = dict(norm_w=m_norm_w, w_in=m_w_in, ssm_a_re=m_ssm_a_re, ssm_a_im=m_ssm_a_im, ssm_log_dt=m_ssm_log_dt,
             ssm_b_re=m_ssm_b_re, ssm_b_im=m_ssm_b_im, ssm_c_re=m_ssm_c_re, ssm_c_im=m_ssm_c_im, ssm_d=m_ssm_d,
             ssm_glu_w=m_ssm_glu_w, ssm_glu_b=m_ssm_glu_b, sg_ln_w=m_sg_ln_w, sg_ln_b=m_sg_ln_b, sg_w=m_sg_w,
             sg_b=m_sg_b, attn_sinks=m_attn_sinks, w_branch_a=m_w_branch_a, w_branch_b=m_w_branch_b,
             w_branch_c=m_w_branch_c, w_out=m_w_out, final_norm_w=m_final_norm_w)
    v = dict(norm_w=v_norm_w, w_in=v_w_in, ssm_a_re=v_ssm_a_re, ssm_a_im=v_ssm_a_im, ssm_log_dt=v_ssm_log_dt,
             ssm_b_re=v_ssm_b_re, ssm_b_im=v_ssm_b_im, ssm_c_re=v_ssm_c_re, ssm_c_im=v_ssm_c_im, ssm_d=v_ssm_d,
             ssm_glu_w=v_ssm_glu_w, ssm_glu_b=v_ssm_glu_b, sg_ln_w=v_sg_ln_w, sg_ln_b=v_sg_ln_b, sg_w=v_sg_w,
             sg_b=v_sg_b, attn_sinks=v_attn_sinks, w_branch_a=v_w_branch_a, w_branch_b=v_w_branch_b,
             w_branch_c=v_w_branch_c, w_out=v_w_out, final_norm_w=v_final_norm_w)

    big_names = ("winT", "glu_w", "wbaT", "wbbT", "wbcT", "w_out")
    L = x.shape[1]
    tabs = _rope_tables(L)
    p = {k: w[k] for k in _SMALL}

    shards = [[w["w_in"][l].T.astype(MXU), w["ssm_glu_w"][l].astype(MXU), w["w_branch_a"][l].T.astype(MXU),
               w["w_branch_b"][l].T.astype(MXU), w["w_branch_c"][l].T.astype(MXU), w["w_out"][l].astype(MXU)]
              for l in range(DEPTH)]
    full0 = _gather8(shards[0], split=True, name="gather_weights_0")
    full0, halves1 = lax.optimization_barrier((full0, [s.reshape(2, s.shape[0] // 2, s.shape[1]) for s in shards[1]]))
    lands1 = [_fill_own(s, name=f"gather_fill_1_{i}") for i, s in enumerate(halves1)]
    sems1, halves1, lands1, token = _split_start(
        halves1, lands1, lambda x_, y_, c_, px, py: c_, lambda x_, y_, c_: 4 * x_ + 2 * y_ + c_, name="gather_start_1")
    bigs = [dict(zip(big_names, full0)), None]

    p_fwd = dict(p, norm_w=p["norm_w"] + token[0, 0])
    saved = [None] * DEPTH
    x1, saved[0] = _layer_fwd(x[0], p_fwd, bigs[0], 0, tabs)
    _, lands1 = _split_wait(sems1, halves1, lands1, x1, name="gather_wait_1")
    lands1 = _pass_to_sibling(lands1, name="gather_pass_1")
    bigs[1] = dict(zip(big_names, [t.reshape(8 * t.shape[1], t.shape[2]) for t in lands1]))
    x2, saved[1] = _layer_fwd(x1, p, bigs[1], 1, tabs)
    loss, dx, dfw = _final_loss(x2, p["final_norm_w"], loss_target[0], name="final_loss")
    loss = lax.psum(loss, ("x", "y", "c"))

    grads = [None] * DEPTH
    big_of = lambda g: [g[k] for k in big_names]
    dx, grads[1] = _layer_bwd(dx, saved[1], p, bigs[1], 1, tabs)
    rs1, token = _reduce_scatter_begin(big_of(grads[1]), tag="1")
    dx, grads[0] = _layer_bwd(dx + token[0, 0], saved[0], p, bigs[0], 0, tabs)
    red = [None, _reduce_scatter_end(rs1, dx, tag="1")]
    rs0, token = _reduce_scatter_begin(big_of(grads[0]), tag="0")

    small_like = [w[k] for k in _SMALL]
    gs = [jnp.stack([grads[l][k] for l in range(DEPTH)]) if k != "final_norm_w" else dfw for k in _SMALL]
    packed = _pack(gs) + token[0, 0]
    allg = _gather8([packed], split=False, name="gather_small_grads")[0]
    gsum = _sum_slots(allg.reshape(8, packed.shape[0], _PACK_COLS), name="sum_small_grads")
    red[0] = _reduce_scatter_end(rs0, gsum, tag="0")

    gbig = {k: [] for k in _BIG}
    for l in range(DEPTH):
        gbig["w_in"].append(red[l][0])
        gbig["ssm_glu_w"].append(red[l][1])
        gbig["w_branch_a"].append(red[l][2].T)
        gbig["w_branch_b"].append(red[l][3].T)
        gbig["w_branch_c"].append(red[l][4].T)
        gbig["w_out"].append(red[l][5])
    gfull = {k: jnp.stack(vs) for k, vs in gbig.items()}
    for k, t in zip(_SMALL, _unpack(gsum, small_like)):
        gfull[k] = t

    delta, new_m, new_v = {}, {}, {}
    for k in _BIG:
        if k == "w_in":
            tr = lambda t: t.transpose(0, 2, 1)
            res = _adamw(tr(w[k]), gfull[k], tr(m[k]), tr(v[k]), name=f"adamw_{k}")
            delta[k], new_m[k], new_v[k] = (tr(t) for t in res)
            gfull[k] = tr(gfull[k])
        else:
            delta[k], new_m[k], new_v[k] = _adamw(w[k], gfull[k], m[k], v[k], name=f"adamw_{k}")
    for k in _SMALL:
        delta[k], new_m[k], new_v[k] = _adamw(w[k], gfull[k], m[k], v[k], name=f"adamw_{k}")

    return (loss, dx[None], *[gfull[k] for k in _WEIGHTS], *[delta[k] for k in _WEIGHTS],
            *[new_m[k] for k in _WEIGHTS], *[new_v[k] for k in _WEIGHTS])
```

```python
import functools
import math

import numpy as np
import jax
import jax.numpy as jnp
from jax import lax
from jax.experimental import pallas as pl
from jax.experimental.pallas import tpu as pltpu

F32 = jnp.float32
MXU = jnp.bfloat16
HIGHEST = lax.Precision.HIGHEST

D_MODEL = 2048
DEPTH = 2
EPS = 1e-6
NEG_INF = -1e30
SSM_WIDTH = 1024
SSM_GROUP = 16
SSM_GROUPS = 64
SSM_STATE = 64
SSM_CH = SSM_GROUPS * SSM_STATE
SLAB = 128
SLAB_CH = (SLAB // SSM_GROUP) * SSM_STATE
N_SLAB = SSM_WIDTH // SLAB
SCAN_SEG = 8
SCAN_STEPS = 4
SG_HEADS = 8
SG_CHUNK = 128
HEAD_DIM = 64
ATT_HEADS = 16
ATT_KV_HEADS = 2
GQA_GROUP = 8
ATT_BLOCK = 128
WINDOW = 128
ROT_DIM = 16
ROPE_THETA = 500000.0
N_MAIN = 6400
N_ZC = 1024
N_GATES = 6144
D_IN = N_MAIN + N_ZC + N_GATES

ADAM_LR = 0.001
ADAM_B1 = 0.9
ADAM_B2 = 0.999
ADAM_EPS = 1e-08
ADAM_WD = 0.01
ADAM_STEP = 10

_DIMS = {"nn": (((1,), (0,)), ((), ())), "nt": (((1,), (1,)), ((), ())), "tn": (((0,), (0,)), ((), ()))}
_MB = 1024 * 1024


def _cp(sem, vmem_mb=48):
    return pltpu.CompilerParams(dimension_semantics=sem, vmem_limit_bytes=vmem_mb * _MB)


def _dot(a, b, mode):
    return lax.dot_general(a.astype(MXU), b.astype(MXU), _DIMS[mode], preferred_element_type=F32)


@jax.custom_vjp
def _mm_nn(a, b):
    return _dot(a, b, "nn")


def _mm_nn_fwd(a, b):
    return _dot(a, b, "nn"), (a, b)


def _mm_nn_bwd(res, g):
    a, b = res
    return _dot(g, b, "nt"), _dot(a, g, "tn")


_mm_nn.defvjp(_mm_nn_fwd, _mm_nn_bwd)


@jax.custom_vjp
def _mm_nt(a, bt):
    return _dot(a, bt, "nt")


def _mm_nt_fwd(a, bt):
    return _dot(a, bt, "nt"), (a, bt)


def _mm_nt_bwd(res, g):
    a, bt = res
    return _dot(g, bt, "nn"), _dot(g, a, "tn")


_mm_nt.defvjp(_mm_nt_fwd, _mm_nt_bwd)


def _rmsnorm(x, w):
    return x * lax.rsqrt(jnp.mean(x * x, axis=-1, keepdims=True) + EPS) * w


def _layernorm(x, w, b):
    mu = jnp.mean(x, axis=-1, keepdims=True)
    var = jnp.mean(jnp.square(x - mu), axis=-1, keepdims=True)
    return (x - mu) * lax.rsqrt(var + EPS) * w + b


def _silu(x):
    return x * jax.nn.sigmoid(x)


def _matmul(a, b, mode, *, name, shape, tm, tn, tk, out_dtype=F32, add=None, a_off=(0, 0), b_off=(0, 0), after=None):
    m, n, k = shape
    tm, tn, tk = min(tm, m), min(tn, n), min(tk, k)
    assert m % tm == 0 and n % tn == 0 and k % tk == 0, (name, shape, tm, tn, tk)
    nk = k // tk
    has_add, has_after = add is not None, after is not None

    def body(*refs):
        a_ref, b_ref = refs[0], refs[1]
        pos = 2
        add_ref = None
        if has_add:
            add_ref = refs[pos]
            pos += 1
        if has_after:
            pos += 1
        o_ref = refs[pos]
        p = _dot(a_ref[...], b_ref[...], mode)
        if nk == 1:
            if has_add:
                p = p + add_ref[...].astype(F32)
            o_ref[...] = p.astype(out_dtype)
            return
        acc_ref = refs[pos + 1]
        kk = pl.program_id(2)

        @pl.when(kk == 0)
        def _():
            acc_ref[...] = p

        @pl.when(kk > 0)
        def _():
            acc_ref[...] += p

        @pl.when(kk == nk - 1)
        def _():
            r = acc_ref[...]
            if has_add:
                r = r + add_ref[...].astype(F32)
            o_ref[...] = r.astype(out_dtype)

    a0, a1 = a_off
    b0, b1 = b_off
    if mode == "tn":
        a_spec = pl.BlockSpec((tk, tm), lambda i, j, kk: (kk + a0, i + a1))
    else:
        a_spec = pl.BlockSpec((tm, tk), lambda i, j, kk: (i + a0, kk + a1))
    if mode == "nt":
        b_spec = pl.BlockSpec((tn, tk), lambda i, j, kk: (j + b0, kk + b1))
    else:
        b_spec = pl.BlockSpec((tk, tn), lambda i, j, kk: (kk + b0, j + b1))
    in_specs = [a_spec, b_spec]
    args = [a, b]
    if has_add:
        in_specs.append(pl.BlockSpec((tm, tn), lambda i, j, kk: (i, j)))
        args.append(add)
    if has_after:
        in_specs.append(pl.BlockSpec(memory_space=pl.ANY))
        args.append(after)
    return pl.pallas_call(
        body, name=name, grid=(m // tm, n // tn, nk),
        in_specs=in_specs,
        out_specs=pl.BlockSpec((tm, tn), lambda i, j, kk: (i, j)),
        out_shape=jax.ShapeDtypeStruct((m, n), out_dtype),
        scratch_shapes=[pltpu.VMEM((tm, tn), F32)] if nk > 1 else [],
        compiler_params=_cp(("parallel", "parallel", "arbitrary")),
    )(*args)


def _rms_fwd(x, w, *, name, tm=256):
    L, d = x.shape
    tm = min(tm, L)

    def body(x_ref, w_ref, h_ref):
        h_ref[...] = _rmsnorm(x_ref[...], w_ref[...]).astype(MXU)

    return pl.pallas_call(
        body, name=name, grid=(L // tm,),
        in_specs=[pl.BlockSpec((tm, d), lambda i: (i, 0)), pl.BlockSpec((1, d), lambda i: (0, 0))],
        out_specs=pl.BlockSpec((tm, d), lambda i: (i, 0)),
        out_shape=jax.ShapeDtypeStruct((L, d), MXU),
        compiler_params=_cp(("parallel",)),
    )(x, w.reshape(1, d))


def _rms_bwd(x, w, dh, dxn, *, name, tm=256):
    L, d = x.shape
    tm = min(tm, L)

    def body(x_ref, w_ref, dh_ref, dxn_ref, dx_ref, dw_ref):
        _, vjp = jax.vjp(_rmsnorm, x_ref[...], w_ref[...])
        dx, dw = vjp(dh_ref[...])
        dx_ref[...] = dx + dxn_ref[...]

        @pl.when(pl.program_id(0) == 0)
        def _():
            dw_ref[...] = jnp.zeros_like(dw_ref)

        dw_ref[...] += dw

    row = pl.BlockSpec((tm, d), lambda i: (i, 0))
    vec = pl.BlockSpec((1, d), lambda i: (0, 0))
    dx, dw = pl.pallas_call(
        body, name=name, grid=(L // tm,),
        in_specs=[row, vec, row, row], out_specs=[row, vec],
        out_shape=[jax.ShapeDtypeStruct((L, d), F32), jax.ShapeDtypeStruct((1, d), F32)],
        compiler_params=_cp(("arbitrary",)),
    )(x, w.reshape(1, d), dh, dxn)
    return dx, dw.reshape(d)


def _final_loss(x, w, tgt, *, name, tm=256):
    L, d = x.shape
    tm = min(tm, L)

    def loss_fn(xv, wv, tv):
        err = jnp.square(_rmsnorm(xv, wv) - tv)
        return 0.5 * jnp.sum(jnp.mean(err, axis=-1, keepdims=True), axis=0, keepdims=True)

    def body(x_ref, w_ref, t_ref, loss_ref, dx_ref, dw_ref):
        tv = t_ref[...]
        val, vjp = jax.vjp(lambda xv, wv: loss_fn(xv, wv, tv), x_ref[...], w_ref[...])
        dx, dw = vjp(jnp.ones((1, 1), F32))
        dx_ref[...] = dx

        @pl.when(pl.program_id(0) == 0)
        def _():
            dw_ref[...] = jnp.zeros_like(dw_ref)
            loss_ref[...] = jnp.zeros_like(loss_ref)

        dw_ref[...] += dw
        loss_ref[...] += jnp.broadcast_to(val, loss_ref.shape)

    row = pl.BlockSpec((tm, d), lambda i: (i, 0))
    vec = pl.BlockSpec((1, d), lambda i: (0, 0))
    loss, dx, dw = pl.pallas_call(
        body, name=name, grid=(L // tm,),
        in_specs=[row, vec, row],
        out_specs=[pl.BlockSpec((8, 128), lambda i: (0, 0)), row, vec],
        out_shape=[jax.ShapeDtypeStruct((8, 128), F32), jax.ShapeDtypeStruct((L, d), F32),
                   jax.ShapeDtypeStruct((1, d), F32)],
        compiler_params=_cp(("arbitrary",)),
    )(x, w.reshape(1, d), tgt)
    return loss[0, 0], dx, dw.reshape(d)


PARAM_ROWS = 512


def _s5_param_fn(are, aim, ldt, bre, bim, row0):
    n = are.shape[0]
    grp = (row0 + lax.broadcasted_iota(jnp.int32, (n, SSM_GROUPS), 0)) // SSM_STATE
    col = lax.broadcasted_iota(jnp.int32, (n, SSM_GROUPS), 1)
    sel = (grp == col).astype(F32)
    dt = jnp.sum(sel * jnp.exp(ldt), axis=-1, keepdims=True)
    mag = jnp.exp(are * dt)
    ang = aim * dt
    lbr = mag * jnp.cos(ang)
    lbi = mag * jnp.sin(ang)
    den = are * are + aim * aim
    nr = lbr - 1.0
    kr = (nr * are + lbi * aim) / den
    ki = (lbi * are - nr * aim) / den
    return lbr, lbi, kr * bre - ki * bim, kr * bim + ki * bre


def _s5_param_specs():
    col = pl.BlockSpec((PARAM_ROWS, 1), lambda i: (i, 0))
    mat = pl.BlockSpec((PARAM_ROWS, SSM_GROUP), lambda i: (i, 0))
    vec = pl.BlockSpec((1, SSM_GROUPS), lambda i: (0, 0))
    return col, mat, vec


def _s5_params_fwd(are, aim, ldt, bre, bim, *, name):
    n = are.shape[0]
    col, mat, vec = _s5_param_specs()

    def body(are_ref, aim_ref, ldt_ref, bre_ref, bim_ref, lbr_ref, lbi_ref, bbr_ref, bbi_ref):
        row0 = pl.program_id(0) * PARAM_ROWS
        lbr, lbi, bbr, bbi = _s5_param_fn(are_ref[...], aim_ref[...], ldt_ref[...], bre_ref[...], bim_ref[...], row0)
        lbr_ref[...] = lbr
        lbi_ref[...] = lbi
        bbr_ref[...] = bbr
        bbi_ref[...] = bbi

    cshape = jax.ShapeDtypeStruct((n, 1), F32)
    mshape = jax.ShapeDtypeStruct((n, SSM_GROUP), F32)
    return pl.pallas_call(body, name=name, grid=(n // PARAM_ROWS,),
                          in_specs=[col, col, vec, mat, mat], out_specs=[col, col, mat, mat],
                          out_shape=[cshape, cshape, mshape, mshape],
                          compiler_params=_cp(("parallel",)))(are, aim, ldt, bre, bim)


def _s5_params_bwd(are, aim, ldt, bre, bim, dlbr, dlbi, dbbr, dbbi, *, name):
    n = are.shape[0]
    col, mat, vec = _s5_param_specs()

    def body(are_ref, aim_ref, ldt_ref, bre_ref, bim_ref, g0, g1, g2, g3, o0, o1, o2, o3, o4):
        row0 = pl.program_id(0) * PARAM_ROWS
        _, vjp = jax.vjp(lambda a, b, c, d, e: _s5_param_fn(a, b, c, d, e, row0),
                         are_ref[...], aim_ref[...], ldt_ref[...], bre_ref[...], bim_ref[...])
        dare, daim, dldt, dbre, dbim = vjp((g0[...], g1[...], g2[...], g3[...]))
        o0[...] = dare
        o1[...] = daim
        o3[...] = dbre
        o4[...] = dbim

        @pl.when(pl.program_id(0) == 0)
        def _():
            o2[...] = jnp.zeros_like(o2)

        o2[...] += dldt

    cshape = jax.ShapeDtypeStruct((n, 1), F32)
    mshape = jax.ShapeDtypeStruct((n, SSM_GROUP), F32)
    return pl.pallas_call(body, name=name, grid=(n // PARAM_ROWS,),
                          in_specs=[col, col, vec, mat, mat, col, col, mat, mat],
                          out_specs=[col, col, vec, mat, mat],
                          out_shape=[cshape, cshape, jax.ShapeDtypeStruct((1, SSM_GROUPS), F32), mshape, mshape],
                          compiler_params=_cp(("arbitrary",)))(are, aim, ldt, bre, bim, dlbr, dlbi, dbbr, dbbi)


SLAB_NC = SLAB_CH // 128


def _s5_specs(L):
    slab = pl.BlockSpec((L, SLAB), lambda s: (0, s))
    wspec = pl.BlockSpec((SLAB_NC, 128, SLAB), lambda s: (s, 0, 0))
    lspec = pl.BlockSpec((SLAB_NC, 1, 128), lambda s: (s, 0, 0))
    sspec = pl.BlockSpec((SLAB_NC, L, 128), lambda s: (s, 0, 0))
    dspec = pl.BlockSpec((1, SLAB), lambda s: (0, s))
    return slab, wspec, lspec, sspec, dspec


def _scan_inplace(sr_ref, si_ref, lr, li, pr_ref, pi_ref, *, reverse):
    NC, L, W = sr_ref.shape
    S = SCAN_SEG
    T = L // S
    lr8 = [jnp.broadcast_to(lr[k], (S, W)) for k in range(NC)]
    li8 = [jnp.broadcast_to(li[k], (S, W)) for k in range(NC)]

    def tiles(first, count):
        return pl.ds(first * S, count * S)

    for k in range(NC):
        pr_ref[k, tiles(T - 1 if reverse else 0, 1), :] = lr8[k]
        pi_ref[k, tiles(T - 1 if reverse else 0, 1), :] = li8[k]
        n = 1
        while n < T:
            have = tiles(T - n, n) if reverse else tiles(0, n)
            new = tiles(T - 2 * n, n) if reverse else tiles(n, n)
            top = tiles(T - n, 1) if reverse else tiles(n - 1, 1)
            ar, ai = pr_ref[k, top, :][None], pi_ref[k, top, :][None]
            hr, hi = pr_ref[k, have, :].reshape(n, S, W), pi_ref[k, have, :].reshape(n, S, W)
            pr_ref[k, new, :] = (hr * ar - hi * ai).reshape(n * S, W)
            pi_ref[k, new, :] = (hr * ai + hi * ar).reshape(n * S, W)
            n *= 2

    def step(i, carry):
        for u in range(SCAN_STEPS):
            jj = i * SCAN_STEPS + u
            rows = pl.ds(pl.multiple_of(((T - 1 - jj) if reverse else jj) * S, S), S)
            out = []
            for k in range(NC):
                sr, si = carry[k]
                nsr = lr8[k] * sr - li8[k] * si + sr_ref[k, rows, :]
                nsi = lr8[k] * si + li8[k] * sr + si_ref[k, rows, :]
                sr_ref[k, rows, :] = nsr
                si_ref[k, rows, :] = nsi
                out.append((nsr, nsi))
            carry = tuple(out)
        return carry

    zero = jnp.zeros((S, W), F32)
    ends = lax.fori_loop(0, T // SCAN_STEPS, step, tuple((zero, zero) for k in range(NC)))
    sub = lax.broadcasted_iota(jnp.int32, (S, W), 0)
    order = range(S - 1, -1, -1) if reverse else range(S)
    for k in range(NC):
        er, ei = ends[k]
        full = tiles(0 if reverse else T - 1, 1)
        ltr = pr_ref[k, full, :][0:1]
        lti = pi_ref[k, full, :][0:1]
        cr = jnp.zeros((1, W), F32)
        ci = jnp.zeros((1, W), F32)
        ctr = jnp.zeros((S, W), F32)
        cti = jnp.zeros((S, W), F32)
        for seg in order:
            ctr = jnp.where(sub == seg, cr, ctr)
            cti = jnp.where(sub == seg, ci, cti)
            cr, ci = (er[seg:seg + 1, :] + ltr * cr - lti * ci, ei[seg:seg + 1, :] + ltr * ci + lti * cr)
        pr = pr_ref[k].reshape(T, S, W)
        pi = pi_ref[k].reshape(T, S, W)
        sr_ref[k] += (pr * ctr[None] - pi * cti[None]).reshape(L, W)
        si_ref[k] += (pr * cti[None] + pi * ctr[None]).reshape(L, W)


def _time_interleave(a):
    L, W = a.shape
    return a.reshape(SCAN_SEG, L // SCAN_SEG, W).transpose(1, 0, 2).reshape(L, W)


def _time_deinterleave(a):
    L, W = a.shape
    return a.reshape(L // SCAN_SEG, SCAN_SEG, W).transpose(1, 0, 2).reshape(L, W)


def _s5_fwd(u, btr, bti, cbr, cbi, lbr, lbi, dvec, *, name):
    L = u.shape[0]

    def body(u_ref, btr_ref, bti_ref, cbr_ref, cbi_ref, lr_ref, li_ref, d_ref, ys_ref, sr_ref, si_ref, pr_ref, pi_ref):
        u = u_ref[...]
        for k in range(SLAB_NC):
            sr_ref[k] = _dot(u, btr_ref[k], "nt")
            si_ref[k] = _dot(u, bti_ref[k], "nt")
        _scan_inplace(sr_ref, si_ref, lr_ref[...], li_ref[...], pr_ref, pi_ref, reverse=False)
        ys = d_ref[...] * u
        for k in range(SLAB_NC):
            ys = ys + _dot(sr_ref[k], cbr_ref[k], "nn") - _dot(si_ref[k], cbi_ref[k], "nn")
        ys_ref[...] = ys

    slab, wspec, lspec, sspec, dspec = _s5_specs(L)
    sshape = jax.ShapeDtypeStruct((N_SLAB * SLAB_NC, L, 128), F32)
    return pl.pallas_call(
        body, name=name, grid=(N_SLAB,),
        in_specs=[slab, wspec, wspec, wspec, wspec, lspec, lspec, dspec],
        out_specs=[slab, sspec, sspec],
        out_shape=[jax.ShapeDtypeStruct((L, SSM_WIDTH), F32), sshape, sshape],
        scratch_shapes=[pltpu.VMEM((SLAB_NC, L, 128), F32), pltpu.VMEM((SLAB_NC, L, 128), F32)],
        compiler_params=_cp(("parallel",), 56),
    )(u, btr, bti, cbr, cbi, lbr, lbi, dvec)


def _s5_bwd(dys, u, sr, si, btr, bti, cbr, cbi, lbr, lbi, dvec, *, name):
    L = u.shape[0]
    S = SCAN_SEG

    def body(dys_ref, u_ref, sr_ref, si_ref, btr_ref, bti_ref, cbr_ref, cbi_ref, lr_ref, li_ref, d_ref,
             du_ref, dbtr_ref, dbti_ref, dcbr_ref, dcbi_ref, dlr_ref, dli_ref, dd_ref,
             ar_ref, ai_ref, pr_ref, pi_ref):
        dys = dys_ref[...]
        u = u_ref[...]
        for k in range(SLAB_NC):
            ar_ref[k] = _dot(dys, cbr_ref[k], "nt")
            ai_ref[k] = -_dot(dys, cbi_ref[k], "nt")
        _scan_inplace(ar_ref, ai_ref, lr_ref[...], -li_ref[...], pr_ref, pi_ref, reverse=True)
        head = lax.broadcasted_iota(jnp.int32, (L, 1), 0) < S
        sub0 = lax.broadcasted_iota(jnp.int32, (S, 1), 0) == 0

        def prev_state(s):
            up = pltpu.roll(s, S, 0)
            return jnp.where(head, 0.0, up), jnp.where(sub0, 0.0, pltpu.roll(up[0:S], 1, 0))

        du = d_ref[...] * dys
        for k in range(SLAB_NC):
            a_re = ar_ref[k]
            a_im = ai_ref[k]
            du = du + _dot(a_re, btr_ref[k], "nn") + _dot(a_im, bti_ref[k], "nn")
            dbtr_ref[k] = _dot(a_re, u, "tn")
            dbti_ref[k] = _dot(a_im, u, "tn")
            s_re = sr_ref[k]
            s_im = si_ref[k]
            dcbr_ref[k] = _dot(s_re, dys, "tn")
            dcbi_ref[k] = -_dot(s_im, dys, "tn")
            p_re, q_re = prev_state(s_re)
            p_im, q_im = prev_state(s_im)
            b_re, b_im = a_re[0:S], a_im[0:S]
            dlr_ref[k] = (jnp.sum(p_re * a_re + p_im * a_im, axis=0, keepdims=True)
                          + jnp.sum(q_re * b_re + q_im * b_im, axis=0, keepdims=True))
            dli_ref[k] = (jnp.sum(p_re * a_im - p_im * a_re, axis=0, keepdims=True)
                          + jnp.sum(q_re * b_im - q_im * b_re, axis=0, keepdims=True))
        du_ref[...] = du
        dd_ref[...] = jnp.sum(dys * u, axis=0, keepdims=True)

    slab, wspec, lspec, sspec, dspec = _s5_specs(L)
    wshape = jax.ShapeDtypeStruct((N_SLAB * SLAB_NC, 128, SLAB), F32)
    lshape = jax.ShapeDtypeStruct((N_SLAB * SLAB_NC, 1, 128), F32)
    return pl.pallas_call(
        body, name=name, grid=(N_SLAB,),
        in_specs=[slab, slab, sspec, sspec, wspec, wspec, wspec, wspec, lspec, lspec, dspec],
        out_specs=[slab, wspec, wspec, wspec, wspec, lspec, lspec, dspec],
        out_shape=[jax.ShapeDtypeStruct((L, SSM_WIDTH), F32), wshape, wshape, wshape, wshape, lshape, lshape,
                   jax.ShapeDtypeStruct((1, SSM_WIDTH), F32)],
        scratch_shapes=[pltpu.VMEM((SLAB_NC, L, 128), F32)] * 4,
        compiler_params=_cp(("parallel",), 56),
    )(dys, u, sr, si, btr, bti, cbr, cbi, lbr, lbi, dvec)


_SLAB_MASK = (np.arange(SLAB_CH)[:, None] // SSM_STATE == np.arange(SLAB)[None, :] // SSM_GROUP)


def _expand_bd(x):
    t = jnp.tile(x.reshape(N_SLAB, SLAB_CH, SSM_GROUP), (1, 1, SLAB // SSM_GROUP))
    return jnp.where(_SLAB_MASK[None], t, 0.0).astype(MXU).reshape(N_SLAB * SLAB_NC, 128, SLAB)


def _contract_bd(dx):
    t = jnp.where(_SLAB_MASK[None], dx.reshape(N_SLAB, SLAB_CH, SLAB), 0.0)
    return jnp.sum(t.reshape(N_SLAB, SLAB_CH, SLAB // SSM_GROUP, SSM_GROUP), axis=2).reshape(SSM_CH, SSM_GROUP)


def _glu_ew(ys, zlin, za):
    a1 = jax.nn.gelu(ys)
    return a1 * jax.nn.sigmoid(zlin) * _silu(za)


def _glu_fwd(ys, main, gw, gb, *, name, tm=256):
    L = ys.shape[0]
    tm = min(tm, L)
    W = SSM_WIDTH

    def body(ys_ref, za_ref, gw_ref, gb_ref, ya_ref):
        ys = ys_ref[...]
        a1 = jax.nn.gelu(ys)
        zlin = _dot(a1, gw_ref[...], "nn") + gb_ref[...]
        ya_ref[...] = _glu_ew(ys, zlin, za_ref[...]).astype(MXU)

    return pl.pallas_call(
        body, name=name, grid=(L // tm,),
        in_specs=[pl.BlockSpec((tm, W), lambda i: (i, 0)), pl.BlockSpec((tm, W), lambda i: (i, 1)),
                  pl.BlockSpec((W, W), lambda i: (0, 0)), pl.BlockSpec((1, W), lambda i: (0, 0))],
        out_specs=pl.BlockSpec((tm, W), lambda i: (i, 0)),
        out_shape=jax.ShapeDtypeStruct((L, W), MXU),
        compiler_params=_cp(("parallel",)),
    )(ys, main, gw, gb.reshape(1, W))


def _glu_bwd(dya, ys, main, gw, gb, *, name, tm=256):
    L = ys.shape[0]
    tm = min(tm, L)
    W = SSM_WIDTH

    def body(dya_ref, ys_ref, za_ref, gw_ref, gb_ref, dys_ref, dza_ref, a1_ref, dzl_ref, db_ref):
        ys = ys_ref[...]
        a1, gelu_vjp = jax.vjp(jax.nn.gelu, ys)
        zlin = _dot(a1, gw_ref[...], "nn") + gb_ref[...]
        _, vjp = jax.vjp(lambda a, z, za: a * jax.nn.sigmoid(z) * _silu(za), a1, zlin, za_ref[...])
        da1, dzlin, dza = vjp(dya_ref[...].astype(F32))
        da1 = da1 + _dot(dzlin, gw_ref[...], "nt")
        dys_ref[...] = gelu_vjp(da1)[0]
        dza_ref[...] = dza
        a1_ref[...] = a1.astype(MXU)
        dzl_ref[...] = dzlin.astype(MXU)

        @pl.when(pl.program_id(0) == 0)
        def _():
            db_ref[...] = jnp.zeros_like(db_ref)

        db_ref[...] += jnp.sum(dzlin, axis=0, keepdims=True)

    row = pl.BlockSpec((tm, W), lambda i: (i, 0))
    vec = pl.BlockSpec((1, W), lambda i: (0, 0))
    return pl.pallas_call(
        body, name=name, grid=(L // tm,),
        in_specs=[row, row, pl.BlockSpec((tm, W), lambda i: (i, 1)), pl.BlockSpec((W, W), lambda i: (0, 0)), vec],
        out_specs=[row, row, row, row, vec],
        out_shape=[jax.ShapeDtypeStruct((L, W), F32), jax.ShapeDtypeStruct((L, W), F32),
                   jax.ShapeDtypeStruct((L, W), MXU), jax.ShapeDtypeStruct((L, W), MXU),
                   jax.ShapeDtypeStruct((1, W), F32)],
        compiler_params=_cp(("arbitrary",)),
    )(dya, ys, main, gw, gb.reshape(1, W))


def _sg_fn(ub, vb, zb, lnw, lnb, ws, bs):
    u = jax.nn.gelu(ub)
    v = _layernorm(jax.nn.gelu(vb), lnw, lnb)
    r = lax.broadcasted_iota(jnp.int32, (SG_CHUNK, SG_CHUNK), 0)
    c = lax.broadcasted_iota(jnp.int32, (SG_CHUNK, SG_CHUNK), 1)
    tri = r >= c
    outs = []
    for h in range(SG_HEADS):
        wh = jnp.where(tri, ws[h], 0.0)
        outs.append(_mm_nn(wh, v[:, h * 128:(h + 1) * 128]) + bs[h])
    mixed = jnp.concatenate(outs, axis=1)
    return u * mixed * _silu(zb)


def _sg_specs(L):
    W = SSM_WIDTH
    blk = lambda c: pl.BlockSpec((SG_CHUNK, W), lambda i, c=c: (i, c))
    vec = pl.BlockSpec((1, W), lambda i: (0, 0))
    wspec = pl.BlockSpec((SG_HEADS, SG_CHUNK, SG_CHUNK), lambda i: (0, 0, 0))
    bspec = pl.BlockSpec((SG_HEADS, SG_CHUNK, 1), lambda i: (0, 0, 0))
    return blk, vec, wspec, bspec


def _sg_fwd(main, lnw, lnb, sgw, sgb, *, name):
    L = main.shape[0]
    W = SSM_WIDTH
    blk, vec, wspec, bspec = _sg_specs(L)

    def body(ub_ref, vb_ref, zb_ref, lnw_ref, lnb_ref, w_ref, b_ref, yb_ref):
        ws = [w_ref[h] for h in range(SG_HEADS)]
        bs = [b_ref[h] for h in range(SG_HEADS)]
        yb_ref[...] = _sg_fn(ub_ref[...], vb_ref[...], zb_ref[...], lnw_ref[...], lnb_ref[...], ws, bs).astype(MXU)

    return pl.pallas_call(
        body, name=name, grid=(L // SG_CHUNK,),
        in_specs=[blk(2), blk(3), blk(4), vec, vec, wspec, bspec],
        out_specs=pl.BlockSpec((SG_CHUNK, W), lambda i: (i, 0)),
        out_shape=jax.ShapeDtypeStruct((L, W), MXU),
        compiler_params=_cp(("parallel",)),
    )(main, main, main, lnw.reshape(1, W), lnb.reshape(1, W), sgw, sgb.reshape(SG_HEADS, SG_CHUNK, 1))


def _sg_bwd(dyb, main, lnw, lnb, sgw, sgb, *, name):
    L = main.shape[0]
    W = SSM_WIDTH
    blk, vec, wspec, bspec = _sg_specs(L)

    def body(dyb_ref, ub_ref, vb_ref, zb_ref, lnw_ref, lnb_ref, w_ref, b_ref,
             dub_ref, dvb_ref, dzb_ref, dlnw_ref, dlnb_ref, dw_ref, db_ref):
        ws = [w_ref[h] for h in range(SG_HEADS)]
        bs = [b_ref[h] for h in range(SG_HEADS)]
        _, vjp = jax.vjp(_sg_fn, ub_ref[...], vb_ref[...], zb_ref[...], lnw_ref[...], lnb_ref[...], ws, bs)
        dub, dvb, dzb, dlnw, dlnb, dws, dbs = vjp(dyb_ref[...])

        @pl.when(pl.program_id(0) == 0)
        def _():
            dlnw_ref[...] = jnp.zeros_like(dlnw_ref)
            dlnb_ref[...] = jnp.zeros_like(dlnb_ref)
            dw_ref[...] = jnp.zeros_like(dw_ref)
            db_ref[...] = jnp.zeros_like(db_ref)

        dub_ref[...] = dub
        dvb_ref[...] = dvb
        dzb_ref[...] = dzb
        dlnw_ref[...] += dlnw
        dlnb_ref[...] += dlnb
        for h in range(SG_HEADS):
            dw_ref[h] += dws[h]
            db_ref[h] += dbs[h]

    row = pl.BlockSpec((SG_CHUNK, W), lambda i: (i, 0))
    out = jax.ShapeDtypeStruct((L, W), F32)
    return pl.pallas_call(
        body, name=name, grid=(L // SG_CHUNK,),
        in_specs=[row, blk(2), blk(3), blk(4), vec, vec, wspec, bspec],
        out_specs=[row, row, row, vec, vec, wspec, bspec],
        out_shape=[out, out, out, jax.ShapeDtypeStruct((1, W), F32), jax.ShapeDtypeStruct((1, W), F32),
                   jax.ShapeDtypeStruct((SG_HEADS, SG_CHUNK, SG_CHUNK), F32),
                   jax.ShapeDtypeStruct((SG_HEADS, SG_CHUNK, 1), F32)],
        compiler_params=_cp(("arbitrary",)),
    )(dyb, main, main, main, lnw.reshape(1, W), lnb.reshape(1, W), sgw, sgb.reshape(SG_HEADS, SG_CHUNK, 1))


def _rope_tables(L):
    half = ROT_DIM // 2
    inv_freq = ROPE_THETA ** (-jnp.arange(0, ROT_DIM, 2, dtype=F32) / ROT_DIM)
    ang = jnp.arange(L, dtype=F32)[:, None] * inv_freq[None, :]
    cos = jnp.cos(ang)
    sin = jnp.sin(ang)
    ones = jnp.ones((L, HEAD_DIM - ROT_DIM), F32)
    cosf = jnp.concatenate([cos, cos, ones], axis=1)
    sinf = jnp.concatenate([sin, sin, 0.0 * ones], axis=1)
    rot = np.zeros((HEAD_DIM, HEAD_DIM), np.float32)
    for d in range(half):
        rot[d + half, d] = -1.0
        rot[d, d + half] = 1.0
    return cosf, sinf, jnp.asarray(rot)


def _rope(t, cosf, sinf, rot):
    shp = t.shape
    t2 = t.reshape(-1, HEAD_DIM)
    sw = lax.dot_general(t2, rot, _DIMS["nn"], precision=HIGHEST, preferred_element_type=F32).reshape(shp)
    return t * cosf + sw * sinf


def _attn_block_fn(q, kw, vw, sinks, cq, sq, ck, sk, rot, q0, k0):
    nk = kw.shape[1]
    qr = _rope(q, cq, sq, rot)
    kr = _rope(kw, ck, sk, rot)
    qpos = q0 + lax.broadcasted_iota(jnp.int32, (1, ATT_BLOCK, nk), 1)
    kpos = k0 + lax.broadcasted_iota(jnp.int32, (1, ATT_BLOCK, nk), 2)
    diff = qpos - kpos
    allowed = (diff >= 0) & (diff < WINDOW)
    outs = []
    for kh in range(ATT_KV_HEADS):
        qh = qr[kh * GQA_GROUP:(kh + 1) * GQA_GROUP].reshape(GQA_GROUP * ATT_BLOCK, HEAD_DIM)
        s = _mm_nt(qh, kr[kh]).reshape(GQA_GROUP, ATT_BLOCK, nk) * (HEAD_DIM ** -0.5)
        s = jnp.where(allowed, s, NEG_INF)
        sink = sinks[kh * GQA_GROUP:(kh + 1) * GQA_GROUP]
        m = lax.stop_gradient(jnp.maximum(jnp.max(s, axis=-1, keepdims=True), sink))
        e = jnp.exp(s - m)
        p = e / (jnp.sum(e, axis=-1, keepdims=True) + jnp.exp(sink - m))
        o = _mm_nn(p.reshape(GQA_GROUP * ATT_BLOCK, nk), vw[kh])
        outs.append(o.reshape(GQA_GROUP, ATT_BLOCK, HEAD_DIM))
    return jnp.concatenate(outs, axis=0)


def _attn_common(L):
    nwin = min(2 * ATT_BLOCK, L)
    qspec = pl.BlockSpec((ATT_HEADS, ATT_BLOCK, HEAD_DIM), lambda n: (0, n, 0))
    kvspec = pl.BlockSpec((ATT_KV_HEADS, L, HEAD_DIM), lambda n: (0, 0, 0))
    sspec = pl.BlockSpec((ATT_HEADS, 1, 1), lambda n: (0, 0, 0))
    tq = pl.BlockSpec((ATT_BLOCK, HEAD_DIM), lambda n: (n, 0))
    tk = pl.BlockSpec((L, HEAD_DIM), lambda n: (0, 0))
    rspec = pl.BlockSpec((HEAD_DIM, HEAD_DIM), lambda n: (0, 0))
    return nwin, qspec, kvspec, sspec, tq, tk, rspec


def _attn_fwd(qh, kh, vh, sinks, cosf, sinf, rot, *, name):
    L = qh.shape[1]
    nwin, qspec, kvspec, sspec, tq, tk, rspec = _attn_common(L)

    def body(q_ref, k_ref, v_ref, s_ref, cq_ref, sq_ref, ck_ref, sk_ref, r_ref, o_ref):
        n = pl.program_id(0)
        k0 = pl.multiple_of(jnp.maximum(n - 1, 0) * ATT_BLOCK, ATT_BLOCK)
        win = pl.ds(k0, nwin)
        o_ref[...] = _attn_block_fn(q_ref[...], k_ref[:, win, :], v_ref[:, win, :], s_ref[...],
                                    cq_ref[...], sq_ref[...], ck_ref[win, :], sk_ref[win, :], r_ref[...],
                                    n * ATT_BLOCK, k0)

    return pl.pallas_call(
        body, name=name, grid=(L // ATT_BLOCK,),
        in_specs=[qspec, kvspec, kvspec, sspec, tq, tq, tk, tk, rspec],
        out_specs=qspec,
        out_shape=jax.ShapeDtypeStruct((ATT_HEADS, L, HEAD_DIM), F32),
        compiler_params=_cp(("parallel",)),
    )(qh, kh, vh, sinks.reshape(ATT_HEADS, 1, 1), cosf, sinf, cosf, sinf, rot)


def _attn_bwd(do, qh, kh, vh, sinks, cosf, sinf, rot, *, name):
    L = qh.shape[1]
    nwin, qspec, kvspec, sspec, tq, tk, rspec = _attn_common(L)

    def body(do_ref, q_ref, k_ref, v_ref, s_ref, cq_ref, sq_ref, ck_ref, sk_ref, r_ref,
             dq_ref, dk_ref, dv_ref, ds_ref):
        n = pl.program_id(0)
        k0 = pl.multiple_of(jnp.maximum(n - 1, 0) * ATT_BLOCK, ATT_BLOCK)
        win = pl.ds(k0, nwin)
        cq, sq, ck, sk, rt = cq_ref[...], sq_ref[...], ck_ref[win, :], sk_ref[win, :], r_ref[...]
        q0 = n * ATT_BLOCK
        _, vjp = jax.vjp(lambda q, kw, vw, s: _attn_block_fn(q, kw, vw, s, cq, sq, ck, sk, rt, q0, k0),
                         q_ref[...], k_ref[:, win, :], v_ref[:, win, :], s_ref[...])
        dq, dkw, dvw, ds = vjp(do_ref[...])

        @pl.when(n == 0)
        def _():
            dk_ref[...] = jnp.zeros_like(dk_ref)
            dv_ref[...] = jnp.zeros_like(dv_ref)
            ds_ref[...] = jnp.zeros_like(ds_ref)

        dq_ref[...] = dq
        dk_ref[:, win, :] += dkw
        dv_ref[:, win, :] += dvw
        ds_ref[...] += ds

    return pl.pallas_call(
        body, name=name, grid=(L // ATT_BLOCK,),
        in_specs=[qspec, qspec, kvspec, kvspec, sspec, tq, tq, tk, tk, rspec],
        out_specs=[qspec, kvspec, kvspec, sspec],
        out_shape=[jax.ShapeDtypeStruct((ATT_HEADS, L, HEAD_DIM), F32),
                   jax.ShapeDtypeStruct((ATT_KV_HEADS, L, HEAD_DIM), F32),
                   jax.ShapeDtypeStruct((ATT_KV_HEADS, L, HEAD_DIM), F32),
                   jax.ShapeDtypeStruct((ATT_HEADS, 1, 1), F32)],
        compiler_params=_cp(("arbitrary",)),
    )(do, qh, kh, vh, sinks.reshape(ATT_HEADS, 1, 1), cosf, sinf, cosf, sinf, rot)


def _to_heads(t, nh):
    L = t.shape[0]
    return t.reshape(L, nh, HEAD_DIM).transpose(1, 0, 2)


def _from_heads(t):
    nh, L, _ = t.shape
    return t.transpose(1, 0, 2).reshape(L, nh * HEAD_DIM)


def _branch_fwd(ya, yb, o2d, zc, gates, wa, wb, wc, *, name, tm=256):
    L = ya.shape[0]
    tm = min(tm, L)
    W, D = SSM_WIDTH, D_MODEL

    def body(ya_ref, yb_ref, o_ref, zc_ref, g0_ref, g1_ref, g2_ref, wa_ref, wb_ref, wc_ref,
             mg_ref, ta_ref, tb_ref, tc_ref, yc_ref):
        yc = (o_ref[...] * _silu(zc_ref[...])).astype(MXU)
        ta = _dot(ya_ref[...], wa_ref[...], "nt")
        tb = _dot(yb_ref[...], wb_ref[...], "nt")
        tc = _dot(yc, wc_ref[...], "nt")
        ta_ref[...] = ta
        tb_ref[...] = tb
        tc_ref[...] = tc
        yc_ref[...] = yc
        mg_ref[...] = (jax.nn.sigmoid(g0_ref[...]) * ta + jax.nn.sigmoid(g1_ref[...]) * tb
                       + jax.nn.sigmoid(g2_ref[...]) * tc).astype(MXU)

    row = pl.BlockSpec((tm, W), lambda i: (i, 0))
    wide = pl.BlockSpec((tm, D), lambda i: (i, 0))
    gate = lambda c: pl.BlockSpec((tm, D), lambda i, c=c: (i, c))
    wspec = pl.BlockSpec((D, W), lambda i: (0, 0))
    return pl.pallas_call(
        body, name=name, grid=(L // tm,),
        in_specs=[row, row, row, row, gate(0), gate(1), gate(2), wspec, wspec, wspec],
        out_specs=[wide, wide, wide, wide, row],
        out_shape=[jax.ShapeDtypeStruct((L, D), MXU), jax.ShapeDtypeStruct((L, D), F32),
                   jax.ShapeDtypeStruct((L, D), F32), jax.ShapeDtypeStruct((L, D), F32),
                   jax.ShapeDtypeStruct((L, W), MXU)],
        compiler_params=_cp(("parallel",), 56),
    )(ya, yb, o2d, zc, gates, gates, gates, wa, wb, wc)


def _branch_bwd(dmg, ta, tb, tc, gates, *, name, tm=256):
    L = dmg.shape[0]
    tm = min(tm, L)
    D = D_MODEL

    def body(dm_ref, ta_ref, tb_ref, tc_ref, g0_ref, g1_ref, g2_ref, da_ref, db_ref, dc_ref, dg_ref):
        dm = dm_ref[...]
        for i, (t_ref, g_ref, d_ref) in enumerate(((ta_ref, g0_ref, da_ref), (tb_ref, g1_ref, db_ref),
                                                   (tc_ref, g2_ref, dc_ref))):
            sg = jax.nn.sigmoid(g_ref[...])
            d_ref[...] = (sg * dm).astype(MXU)
            dg_ref[:, i * D:(i + 1) * D] = (dm * t_ref[...] * sg * (1.0 - sg)).astype(MXU)

    wide = pl.BlockSpec((tm, D), lambda i: (i, 0))
    gate = lambda c: pl.BlockSpec((tm, D), lambda i, c=c: (i, c))
    bf = jax.ShapeDtypeStruct((L, D), MXU)
    return pl.pallas_call(
        body, name=name, grid=(L // tm,),
        in_specs=[wide, wide, wide, wide, gate(0), gate(1), gate(2)],
        out_specs=[wide, wide, wide, pl.BlockSpec((tm, 3 * D), lambda i: (i, 0))],
        out_shape=[bf, bf, bf, jax.ShapeDtypeStruct((L, 3 * D), MXU)],
        compiler_params=_cp(("parallel",), 56),
    )(dmg, ta, tb, tc, gates, gates, gates)


def _gate_c_bwd(dyc, o2d, zc, *, name, tm=256):
    L, W = dyc.shape
    tm = min(tm, L)

    def body(dy_ref, o_ref, z_ref, do_ref, dz_ref):
        _, vjp = jax.vjp(lambda o, z: o * _silu(z), o_ref[...], z_ref[...])
        do, dz = vjp(dy_ref[...])
        do_ref[...] = do
        dz_ref[...] = dz.astype(MXU)

    row = pl.BlockSpec((tm, W), lambda i: (i, 0))
    return pl.pallas_call(body, name=name, grid=(L // tm,), in_specs=[row, row, row], out_specs=[row, row],
                          out_shape=[jax.ShapeDtypeStruct((L, W), F32), jax.ShapeDtypeStruct((L, W), MXU)],
                          compiler_params=_cp(("parallel",)))(dyc, o2d, zc)


def _adamw(w, g, m, v, *, name):
    shape = w.shape
    cols = shape[-1]
    w2, g2, m2, v2 = (t.reshape(-1, cols) for t in (w, g, m, v))
    rows = w2.shape[0]
    tc = 1024 if cols % 1024 == 0 else cols
    lane_cols = -(-tc // 128) * 128
    tr = rows
    while tr % 16 == 0 and tr * lane_cols * 4 > 2 * _MB:
        tr //= 2

    def body(w_ref, g_ref, m_ref, v_ref, d_ref, nm_ref, nv_ref):
        gv = g_ref[...]
        nm = ADAM_B1 * m_ref[...] + (1.0 - ADAM_B1) * gv
        nv = ADAM_B2 * v_ref[...] + (1.0 - ADAM_B2) * jnp.square(gv)
        m_hat = nm / (1.0 - ADAM_B1 ** ADAM_STEP)
        v_hat = nv / (1.0 - ADAM_B2 ** ADAM_STEP)
        d_ref[...] = -ADAM_LR * (m_hat / (jnp.sqrt(v_hat) + ADAM_EPS) + ADAM_WD * w_ref[...])
        nm_ref[...] = nm
        nv_ref[...] = nv

    spec = pl.BlockSpec((tr, tc), lambda i, j: (i, j))
    out = jax.ShapeDtypeStruct((rows, cols), F32)
    d, nm, nv = pl.pallas_call(body, name=name, grid=(rows // tr, cols // tc), in_specs=[spec] * 4,
                               out_specs=[spec] * 3, out_shape=[out, out, out],
                               compiler_params=_cp(("parallel", "parallel")))(w2, g2, m2, v2)
    return d.reshape(shape), nm.reshape(shape), nv.reshape(shape)


def _prep_layer(p, l):
    are = p["ssm_a_re"][l].reshape(SSM_CH, 1)
    aim = p["ssm_a_im"][l].reshape(SSM_CH, 1)
    ldt = p["ssm_log_dt"][l].reshape(1, SSM_GROUPS)
    bre = p["ssm_b_re"][l].reshape(SSM_CH, SSM_GROUP)
    bim = p["ssm_b_im"][l].reshape(SSM_CH, SSM_GROUP)
    lbr, lbi, bbr, bbi = _s5_params_fwd(are, aim, ldt, bre, bim, name=f"s5_params_fwd_{l}")
    cre = p["ssm_c_re"][l].transpose(0, 2, 1).reshape(SSM_CH, SSM_GROUP)
    cim = p["ssm_c_im"][l].transpose(0, 2, 1).reshape(SSM_CH, SSM_GROUP)
    return dict(raw=(are, aim, ldt, bre, bim),
                lbr=lbr.reshape(N_SLAB * SLAB_NC, 1, 128), lbi=lbi.reshape(N_SLAB * SLAB_NC, 1, 128),
                btr=_expand_bd(bbr), bti=_expand_bd(bbi), cbr=_expand_bd(cre), cbi=_expand_bd(cim),
                dvec=p["ssm_d"][l].reshape(1, SSM_WIDTH))


def _layer_fwd(x, p, big, l, tabs):
    L = x.shape[0]
    cosf, sinf, rot = tabs
    sp = _prep_layer(p, l)
    h = _rms_fwd(x, p["norm_w"][l], name=f"rms_fwd_{l}")
    mm = functools.partial(_matmul, h, big["winT"], "nt", tm=L, tn=256, tk=D_MODEL)
    main = mm(name=f"proj_main_{l}", shape=(L, N_MAIN, D_MODEL))
    zc = mm(name=f"proj_zc_{l}", shape=(L, N_ZC, D_MODEL), b_off=(N_MAIN // 256, 0))
    gates = mm(name=f"proj_gates_{l}", shape=(L, N_GATES, D_MODEL), b_off=((N_MAIN + N_ZC) // 256, 0))
    ua = _time_interleave(main[:, :SSM_WIDTH])
    ys, sr, si = _s5_fwd(ua, sp["btr"], sp["bti"], sp["cbr"], sp["cbi"], sp["lbr"], sp["lbi"], sp["dvec"],
                         name=f"s5_fwd_{l}")
    ys = _time_deinterleave(ys)
    ya = _glu_fwd(ys, main, big["glu_w"], p["ssm_glu_b"][l], name=f"glu_fwd_{l}")
    yb = _sg_fwd(main, p["sg_ln_w"][l], p["sg_ln_b"][l], p["sg_w"][l], p["sg_b"][l], name=f"sg_fwd_{l}")
    qh = _to_heads(main[:, 5120:6144], ATT_HEADS)
    kh = _to_heads(main[:, 6144:6272], ATT_KV_HEADS)
    vh = _to_heads(main[:, 6272:6400], ATT_KV_HEADS)
    oh = _attn_fwd(qh, kh, vh, p["attn_sinks"][l], cosf, sinf, rot, name=f"attn_fwd_{l}")
    o2d = _from_heads(oh)
    mg, ta, tb, tc, yc = _branch_fwd(ya, yb, o2d, zc, gates, big["wbaT"], big["wbbT"], big["wbcT"],
                                     name=f"branch_fwd_{l}")
    xn = _matmul(mg, big["w_out"], "nn", name=f"out_fwd_{l}", shape=(L, D_MODEL, D_MODEL), tm=512, tn=512,
                 tk=D_MODEL, add=x)
    saved = dict(x=x, h=h, main=main, zc=zc, gates=gates, ua=ua, ys=ys, sr=sr, si=si, ya=ya, yb=yb, yc=yc, o2d=o2d,
                 qh=qh, kh=kh, vh=vh, mg=mg, ta=ta, tb=tb, tc=tc, sp=sp)
    return xn, saved


def _layer_bwd(dxn, s, p, big, l, tabs):
    L = dxn.shape[0]
    D, W = D_MODEL, SSM_WIDTH
    cosf, sinf, rot = tabs
    sp = s["sp"]
    g = {}
    dmg = _matmul(dxn, big["w_out"], "nt", name=f"out_bwd_dm_{l}", shape=(L, D, D), tm=512, tn=512, tk=D)
    g["w_out"] = _matmul(s["mg"], dxn, "tn", name=f"out_bwd_dw_{l}", shape=(D, D, L), tm=512, tn=512, tk=L,
                         out_dtype=MXU)
    dta, dtb, dtc, dgates = _branch_bwd(dmg, s["ta"], s["tb"], s["tc"], s["gates"], name=f"branch_bwd_{l}")
    dys_ = {}
    for nm, dt, y, wt in (("a", dta, s["ya"], big["wbaT"]), ("b", dtb, s["yb"], big["wbbT"]),
                          ("c", dtc, s["yc"], big["wbcT"])):
        dys_[nm] = _matmul(dt, wt, "nn", name=f"branch_bwd_dy{nm}_{l}", shape=(L, W, D), tm=512, tn=512, tk=D)
        g["wb" + nm + "T"] = _matmul(dt, y, "tn", name=f"branch_bwd_dw{nm}_{l}", shape=(D, W, L),
                                     tm=512, tn=512, tk=L, out_dtype=MXU)
    do2d, dzc = _gate_c_bwd(dys_["c"], s["o2d"], s["zc"], name=f"gate_c_bwd_{l}")
    dqh, dkh, dvh, dsinks = _attn_bwd(_to_heads(do2d, ATT_HEADS), s["qh"], s["kh"], s["vh"], p["attn_sinks"][l],
                                      cosf, sinf, rot, name=f"attn_bwd_{l}")
    g["attn_sinks"] = dsinks.reshape(ATT_HEADS)
    dub, dvb, dzb, dlnw, dlnb, dsgw, dsgb = _sg_bwd(dys_["b"], s["main"], p["sg_ln_w"][l], p["sg_ln_b"][l],
                                                    p["sg_w"][l], p["sg_b"][l], name=f"sg_bwd_{l}")
    g["sg_ln_w"], g["sg_ln_b"] = dlnw.reshape(W), dlnb.reshape(W)
    g["sg_w"], g["sg_b"] = dsgw, dsgb.reshape(SG_HEADS, SG_CHUNK)
    dys, dza, a1, dzl, dgb = _glu_bwd(dys_["a"], s["ys"], s["main"], big["glu_w"], p["ssm_glu_b"][l],
                                      name=f"glu_bwd_{l}")
    g["ssm_glu_b"] = dgb.reshape(W)
    g["glu_w"] = _matmul(a1, dzl, "tn", name=f"glu_bwd_dw_{l}", shape=(W, W, L), tm=512, tn=512, tk=L, out_dtype=MXU)
    dua, dbtr, dbti, dcbr, dcbi, dlr, dli, dd = _s5_bwd(_time_interleave(dys), s["ua"], s["sr"], s["si"], sp["btr"],
                                                        sp["bti"], sp["cbr"], sp["cbi"], sp["lbr"], sp["lbi"],
                                                        sp["dvec"], name=f"s5_bwd_{l}")
    dua = _time_deinterleave(dua)
    g["ssm_d"] = dd.reshape(W)
    to_c = lambda t: _contract_bd(t).reshape(SSM_GROUPS, SSM_STATE, SSM_GROUP).transpose(0, 2, 1)
    g["ssm_c_re"], g["ssm_c_im"] = to_c(dcbr), to_c(dcbi)
    dare, daim, dldt, dbre, dbim = _s5_params_bwd(*sp["raw"], dlr.reshape(SSM_CH, 1), dli.reshape(SSM_CH, 1),
                                                  _contract_bd(dbtr), _contract_bd(dbti),
                                                  name=f"s5_params_bwd_{l}")
    g["ssm_a_re"] = dare.reshape(SSM_GROUPS, SSM_STATE)
    g["ssm_a_im"] = daim.reshape(SSM_GROUPS, SSM_STATE)
    g["ssm_log_dt"] = dldt.reshape(SSM_GROUPS)
    g["ssm_b_re"] = dbre.reshape(SSM_GROUPS, SSM_STATE, SSM_GROUP)
    g["ssm_b_im"] = dbim.reshape(SSM_GROUPS, SSM_STATE, SSM_GROUP)
    dproj = jnp.concatenate([t.astype(MXU) for t in (dua, dza, dub, dvb, dzb, _from_heads(dqh), _from_heads(dkh),
                                                     _from_heads(dvh), dzc, dgates)], axis=1)
    g["winT"] = _matmul(dproj, s["h"], "tn", name=f"proj_bwd_dw_{l}", shape=(D_IN, D, L), tm=256, tn=D, tk=L,
                        out_dtype=MXU)
    return dproj, g


def _layer_bwd_input(dproj, dxn, s, p, big, l, after):
    L = dxn.shape[0]
    dh = _matmul(dproj, big["winT"], "nn", name=f"proj_bwd_dh_{l}", shape=(L, D_MODEL, D_IN), tm=512, tn=512,
                 tk=D_IN // 2, after=after)
    dx, dnw = _rms_bwd(s["x"], p["norm_w"][l], dh, dxn, name=f"rms_bwd_{l}")
    return dx, dnw


MESH = pl.DeviceIdType.MESH
_ANY = pl.BlockSpec(memory_space=pl.ANY)
ROW_ALIGN = 16


def _coords():
    return lax.axis_index("x"), lax.axis_index("y"), lax.axis_index("c")


def _gather8(arrs, *, split, name):
    n = len(arrs)
    rows = [a.shape[0] // 2 if split else a.shape[0] for a in arrs]
    for r in rows:
        assert r % ROW_ALIGN == 0

    def body(*refs):
        ins, outs = refs[:n], refs[n:2 * n]
        send, recv, lsem = refs[2 * n:]
        x, y, c = _coords()
        me, sibling = (x, y, c), (x, y, 1 - c)
        chips = [(1 - x, y), (x, 1 - y), (1 - x, 1 - y)]

        def blk(a, px, py, pc):
            return outs[a].at[pl.ds(pl.multiple_of((4 * px + 2 * py + pc) * rows[a], ROW_ALIGN), rows[a]), :]

        def own(a):
            if split:
                return ins[a].at[pl.ds(pl.multiple_of(c * rows[a], ROW_ALIGN), rows[a]), :]
            return ins[a]

        def copy(a, k, block, to, src=None):
            return pltpu.make_async_remote_copy(
                src_ref=blk(a, *block) if src is None else src, dst_ref=blk(a, *block),
                send_sem=send.at[a, k], recv_sem=recv.at[a, k], device_id=to, device_id_type=MESH)

        mine, first, passed = [], [], []
        for a in range(n):
            mine.append(pltpu.make_async_copy(own(a), blk(a, *me), lsem.at[a]))
            mine[a].start()
            f = [copy(a, 0, me, sibling, src=own(a))]
            f += [copy(a, 1 + j, me, (*chip, c), src=own(a)) for j, chip in enumerate(chips)]
            for cp in f:
                cp.start()
            first.append(f)
        for a in range(n):
            ps = [copy(a, 4 + j, (*chip, c), sibling) for j, chip in enumerate(chips)]
            for j, chip in enumerate(chips):
                copy(a, 1 + j, (*chip, c), me).wait_recv()
                ps[j].start()
            passed.append(ps)
        for a in range(n):
            copy(a, 0, sibling, me).wait_recv()
            for j, chip in enumerate(chips):
                copy(a, 4 + j, (*chip, 1 - c), me).wait_recv()
            for cp in first[a] + passed[a]:
                cp.wait_send()
            mine[a].wait()

    return pl.pallas_call(
        body, name=name,
        in_specs=[_ANY] * n, out_specs=[_ANY] * n,
        out_shape=[jax.ShapeDtypeStruct((8 * r,) + a.shape[1:], a.dtype) for r, a in zip(rows, arrs)],
        scratch_shapes=[pltpu.SemaphoreType.DMA((n, 7)), pltpu.SemaphoreType.DMA((n, 7)), pltpu.SemaphoreType.DMA((n,))],
    )(*arrs)


def _sibling_swap(arrs, *, pick_other, name):
    n = len(arrs)

    def body(*refs):
        ins, outs = refs[:n], refs[n:2 * n]
        send, recv = refs[2 * n:]
        x, y, c = _coords()
        sel = (1 - c) if pick_other else c
        cps = [pltpu.make_async_remote_copy(src_ref=ins[a].at[:, sel], dst_ref=outs[a], send_sem=send.at[a],
                                            recv_sem=recv.at[a], device_id=(x, y, 1 - c), device_id_type=MESH)
               for a in range(n)]
        for cp in cps:
            cp.start()
        for cp in cps:
            cp.wait_recv()
        for cp in cps:
            cp.wait_send()

    return pl.pallas_call(
        body, name=name, in_specs=[_ANY] * n, out_specs=[_ANY] * n,
        out_shape=[jax.ShapeDtypeStruct((a.shape[0],) + a.shape[2:], a.dtype) for a in arrs],
        scratch_shapes=[pltpu.SemaphoreType.DMA((n,)), pltpu.SemaphoreType.DMA((n,))],
    )(*arrs)


def _col_tile(lead, rows, cols, itemsize=4, cap=4 * _MB):
    tc = cols
    while tc % 256 == 0 and lead * rows * tc * itemsize > cap:
        tc //= 2
    return tc


def _pair_sum(mine, theirs, *, name):
    _, _, rows, cols = mine.shape
    tc = _col_tile(1, rows, cols)
    c = lax.axis_index("c")

    def body(c_ref, a_ref, b_ref, o_ref):
        o_ref[...] = (a_ref[...].astype(F32) + b_ref[...].astype(F32)).astype(MXU)

    return pl.pallas_call(
        body, name=name,
        grid_spec=pltpu.PrefetchScalarGridSpec(
            num_scalar_prefetch=1, grid=(4, cols // tc),
            in_specs=[pl.BlockSpec((None, None, rows, tc), lambda j, i, cr: (j, cr[0], 0, i)),
                      pl.BlockSpec((None, rows, tc), lambda j, i, cr: (j, 0, i))],
            out_specs=pl.BlockSpec((None, rows, tc), lambda j, i, cr: (j, 0, i))),
        out_shape=jax.ShapeDtypeStruct((4, rows, cols), MXU),
        compiler_params=_cp(("parallel", "parallel")),
    )(c.reshape(1).astype(jnp.int32), mine, theirs)


_HBM = pl.BlockSpec(memory_space=pltpu.HBM)
_SEM = pl.BlockSpec(memory_space=pltpu.SEMAPHORE)
_EFFECT = pltpu.SideEffectType.DATAFLOW_SIDE_EFFECTING
N_PEER_CHIPS = 3


def _peer_chips(x, y):
    return [(1 - x, y), (x, 1 - y), (1 - x, 1 - y)]


def _split_start(srcs, lands, src_slot, dst_slot, *, name):
    n = len(srcs)
    ns = n * N_PEER_CHIPS

    def body(*refs):
        src_refs, land_refs = refs[:n], refs[n:2 * n]
        send, recv, token = refs[2 * n:2 * n + ns], refs[2 * n + ns:2 * n + 2 * ns], refs[-1]
        x, y, c = _coords()
        for a in range(n):
            for k, (px, py) in enumerate(_peer_chips(x, y)):
                pltpu.make_async_remote_copy(
                    src_ref=src_refs[a].at[src_slot(x, y, c, px, py)], dst_ref=land_refs[a].at[dst_slot(x, y, c)],
                    send_sem=send[a * N_PEER_CHIPS + k], recv_sem=recv[a * N_PEER_CHIPS + k],
                    device_id=(px, py, c), device_id_type=MESH).start()
        token[...] = jnp.zeros_like(token)

    bufs = list(srcs) + list(lands)
    res = pl.pallas_call(
        body, name=name,
        out_shape=(*[pltpu.SemaphoreType.DMA(())] * (2 * ns), *[pltpu.HBM(b.shape, b.dtype) for b in bufs],
                   jax.ShapeDtypeStruct((8, 128), F32)),
        in_specs=[_HBM] * (2 * n),
        out_specs=(*[_SEM] * (2 * ns), *[_HBM] * (2 * n), pl.BlockSpec(memory_space=pltpu.VMEM)),
        input_output_aliases={i: 2 * ns + i for i in range(2 * n)},
        compiler_params=pltpu.CompilerParams(has_side_effects=_EFFECT),
    )(*[pltpu.with_memory_space_constraint(b, pltpu.HBM) for b in bufs])
    sems = list(res[:2 * ns])
    return sems, list(res[2 * ns:2 * ns + n]), list(res[2 * ns + n:2 * ns + 2 * n]), res[-1]


def _split_wait(sems, srcs, lands, after, *, name):
    n = len(srcs)
    ns = n * N_PEER_CHIPS

    def body(*refs):
        src_refs, land_refs = refs[:n], refs[n:2 * n]
        send, recv = refs[2 * n:2 * n + ns], refs[2 * n + ns:2 * n + 2 * ns]
        x, y, c = _coords()
        for a in range(n):
            for k in range(N_PEER_CHIPS):
                cp = pltpu.make_async_remote_copy(
                    src_ref=src_refs[a].at[0], dst_ref=land_refs[a].at[0], send_sem=send[a * N_PEER_CHIPS + k],
                    recv_sem=recv[a * N_PEER_CHIPS + k], device_id=(x, y, 1 - c), device_id_type=MESH)
                cp.wait_send()
                cp.wait_recv()

    bufs = list(srcs) + list(lands)
    res = pl.pallas_call(
        body, name=name,
        out_shape=tuple(pltpu.HBM(b.shape, b.dtype) for b in bufs),
        in_specs=[_HBM] * (2 * n) + [_SEM] * (2 * ns) + [_ANY],
        out_specs=tuple([_HBM] * (2 * n)),
        input_output_aliases={i: i for i in range(2 * n)},
        compiler_params=pltpu.CompilerParams(has_side_effects=_EFFECT),
    )(*bufs, *sems, after)
    return list(res[:n]), list(res[n:])


def _fill_own(shard2, *, name):
    _, rows, cols = shard2.shape
    tc = _col_tile(1, rows, cols, itemsize=shard2.dtype.itemsize)
    j = 2 * lax.axis_index("x") + lax.axis_index("y")

    def body(j_ref, s_ref, o_ref):
        o_ref[...] = s_ref[...]

    return pl.pallas_call(
        body, name=name,
        grid_spec=pltpu.PrefetchScalarGridSpec(
            num_scalar_prefetch=1, grid=(2, cols // tc),
            in_specs=[pl.BlockSpec((None, rows, tc), lambda h, i, jr: (h, 0, i))],
            out_specs=pl.BlockSpec((None, rows, tc), lambda h, i, jr: (2 * jr[0] + h, 0, i))),
        out_shape=jax.ShapeDtypeStruct((8, rows, cols), shard2.dtype),
        compiler_params=_cp(("parallel", "parallel")),
    )(j.reshape(1).astype(jnp.int32), shard2)


def _pass_to_sibling(lands, *, name):
    n = len(lands)

    def body(*refs):
        outs = refs[n:2 * n]
        send, recv = refs[2 * n:]
        x, y, c = _coords()
        cps = []
        for a in range(n):
            for k, (px, py) in enumerate(_peer_chips(x, y)):
                slot = 4 * px + 2 * py + c
                cps.append(pltpu.make_async_remote_copy(
                    src_ref=outs[a].at[slot], dst_ref=outs[a].at[slot], send_sem=send.at[a, k], recv_sem=recv.at[a, k],
                    device_id=(x, y, 1 - c), device_id_type=MESH))
        for cp in cps:
            cp.start()
        for cp in cps:
            cp.wait_recv()
        for cp in cps:
            cp.wait_send()

    return pl.pallas_call(
        body, name=name, in_specs=[_ANY] * n, out_specs=[_ANY] * n,
        out_shape=[jax.ShapeDtypeStruct(b.shape, b.dtype) for b in lands],
        input_output_aliases={a: a for a in range(n)},
        scratch_shapes=[pltpu.SemaphoreType.DMA((n, N_PEER_CHIPS)), pltpu.SemaphoreType.DMA((n, N_PEER_CHIPS))],
    )(*lands)


def _sum_parts(parts, got, *, name):
    _, rows, cols = parts.shape
    tc = _col_tile(4, rows, cols, itemsize=parts.dtype.itemsize)
    x, y, c = _coords()
    idx = jnp.stack([2 * x + y, 2 * (1 - x) + y, 2 * x + (1 - y), 2 * (1 - x) + (1 - y), c]).astype(jnp.int32)

    def body(i_ref, p_ref, g0_ref, g1_ref, g2_ref, o_ref):
        o_ref[...] = ((p_ref[...].astype(F32) + g0_ref[...].astype(F32)) + g1_ref[...].astype(F32)) + g2_ref[...].astype(F32)

    slot = lambda s: pl.BlockSpec((None, rows, tc), lambda i, ir, s=s: (ir[s], 0, i))
    return pl.pallas_call(
        body, name=name,
        grid_spec=pltpu.PrefetchScalarGridSpec(
            num_scalar_prefetch=1, grid=(cols // tc,),
            in_specs=[slot(0), slot(1), slot(2), slot(3)],
            out_specs=pl.BlockSpec((None, rows, tc), lambda i, ir: (ir[4], 0, i))),
        out_shape=jax.ShapeDtypeStruct((2, rows, cols), F32),
        compiler_params=_cp(("parallel",)),
    )(idx, parts, got, got, got)


def _sum_slots(t, *, name):
    S, rows, cols = t.shape
    tc = _col_tile(S, rows, cols)

    def body(t_ref, o_ref):
        acc = t_ref[0].astype(F32)
        for s in range(1, S):
            acc = acc + t_ref[s].astype(F32)
        o_ref[...] = acc

    return pl.pallas_call(
        body, name=name, grid=(cols // tc,),
        in_specs=[pl.BlockSpec((S, rows, tc), lambda i: (0, 0, i))],
        out_specs=pl.BlockSpec((rows, tc), lambda i: (0, i)),
        out_shape=jax.ShapeDtypeStruct((rows, cols), F32),
        compiler_params=_cp(("parallel",)),
    )(t)


def _halves_join(bufs, *, name):
    n = len(bufs)

    def body(*refs):
        outs = refs[n:2 * n]
        send, recv = refs[2 * n:]
        x, y, c = _coords()
        cps = [pltpu.make_async_remote_copy(src_ref=outs[a].at[c], dst_ref=outs[a].at[c], send_sem=send.at[a],
                                            recv_sem=recv.at[a], device_id=(x, y, 1 - c), device_id_type=MESH)
               for a in range(n)]
        for cp in cps:
            cp.start()
        for cp in cps:
            cp.wait_recv()
        for cp in cps:
            cp.wait_send()

    return pl.pallas_call(
        body, name=name, in_specs=[_ANY] * n, out_specs=[_ANY] * n,
        out_shape=[jax.ShapeDtypeStruct(b.shape, b.dtype) for b in bufs],
        input_output_aliases={a: a for a in range(n)},
        scratch_shapes=[pltpu.SemaphoreType.DMA((n,)), pltpu.SemaphoreType.DMA((n,))],
    )(*bufs)


def _reduce_scatter_begin(grads, *, tag):
    views = [g.reshape(4, 2, g.shape[0] // 8, g.shape[1]) for g in grads]
    theirs = _sibling_swap(views, pick_other=True, name=f"rs_swap_{tag}")
    parts = [_pair_sum(v, t, name=f"rs_pair_{tag}_{i}") for i, (v, t) in enumerate(zip(views, theirs))]
    got = [lax.empty(p.shape, p.dtype) for p in parts]
    sems, parts, got, token = _split_start(
        parts, got, lambda x, y, c, px, py: 2 * px + py, lambda x, y, c: 2 * x + y, name=f"rs_start_{tag}")
    return (sems, parts, got), token


def _reduce_scatter_end(state, after, *, tag):
    sems, parts, got = state
    parts, got = _split_wait(sems, parts, got, after, name=f"rs_wait_{tag}")
    halves = [_sum_parts(p, t, name=f"rs_sum_{tag}_{i}") for i, (p, t) in enumerate(zip(parts, got))]
    joined = _halves_join(halves, name=f"rs_join_{tag}")
    return [j.reshape(2 * j.shape[1], j.shape[2]) for j in joined]


_SMALL = ("norm_w", "ssm_a_re", "ssm_a_im", "ssm_log_dt", "ssm_b_re", "ssm_b_im", "ssm_c_re", "ssm_c_im", "ssm_d",
          "ssm_glu_b", "sg_ln_w", "sg_ln_b", "sg_w", "sg_b", "attn_sinks", "final_norm_w")
_BIG = ("w_in", "ssm_glu_w", "w_branch_a", "w_branch_b", "w_branch_c", "w_out")
_WEIGHTS = ("norm_w", "w_in", "ssm_a_re", "ssm_a_im", "ssm_log_dt", "ssm_b_re", "ssm_b_im", "ssm_c_re", "ssm_c_im",
            "ssm_d", "ssm_glu_w", "ssm_glu_b", "sg_ln_w", "sg_ln_b", "sg_w", "sg_b", "attn_sinks", "w_branch_a",
            "w_branch_b", "w_branch_c", "w_out", "final_norm_w")
_PACK_COLS = 1024
_PACK_ALIGN = 16 * _PACK_COLS


def _pack(ts):
    flat = jnp.concatenate([t.reshape(-1) for t in ts])
    pad = (-flat.shape[0]) % _PACK_ALIGN
    return jnp.pad(flat, (0, pad)).reshape(-1, _PACK_COLS)


def _unpack(buf, like):
    flat = buf.reshape(-1)
    out, pos = [], 0
    for t in like:
        out.append(flat[pos:pos + t.size].reshape(t.shape))
        pos += t.size
    return out


def kernel(x, norm_w, w_in, ssm_a_re, ssm_a_im, ssm_log_dt, ssm_b_re, ssm_b_im, ssm_c_re, ssm_c_im, ssm_d, ssm_glu_w, ssm_glu_b, sg_ln_w, sg_ln_b, sg_w, sg_b, attn_sinks, w_branch_a, w_branch_b, w_branch_c, w_out, final_norm_w, loss_target, m_norm_w, m_w_in, m_ssm_a_re, m_ssm_a_im, m_ssm_log_dt, m_ssm_b_re, m_ssm_b_im, m_ssm_c_re, m_ssm_c_im, m_ssm_d, m_ssm_glu_w, m_ssm_glu_b, m_sg_ln_w, m_sg_ln_b, m_sg_w, m_sg_b, m_attn_sinks, m_w_branch_a, m_w_branch_b, m_w_branch_c, m_w_out, m_final_norm_w, v_norm_w, v_w_in, v_ssm_a_re, v_ssm_a_im, v_ssm_log_dt, v_ssm_b_re, v_ssm_b_im, v_ssm_c_re, v_ssm_c_im, v_ssm_d, v_ssm_glu_w, v_ssm_glu_b, v_sg_ln_w, v_sg_ln_b, v_sg_w, v_sg_b, v_attn_sinks, v_w_branch_a, v_w_branch_b, v_w_branch_c, v_w_out, v_final_norm_w):
    w = dict(norm_w=norm_w, w_in=w_in, ssm_a_re=ssm_a_re, ssm_a_im=ssm_a_im, ssm_log_dt=ssm_log_dt, ssm_b_re=ssm_b_re,
             ssm_b_im=ssm_b_im, ssm_c_re=ssm_c_re, ssm_c_im=ssm_c_im, ssm_d=ssm_d, ssm_glu_w=ssm_glu_w,
             ssm_glu_b=ssm_glu_b, sg_ln_w=sg_ln_w, sg_ln_b=sg_ln_b, sg_w=sg_w, sg_b=sg_b, attn_sinks=attn_sinks,
             w_branch_a=w_branch_a, w_branch_b=w_branch_b, w_branch_c=w_branch_c, w_out=w_out,
             final_norm_w=final_norm_w)
    m = dict(norm_w=m_norm_w, w_in=m_w_in, ssm_a_re=m_ssm_a_re, ssm_a_im=m_ssm_a_im, ssm_log_dt=m_ssm_log_dt,
             ssm_b_re=m_ssm_b_re, ssm_b_im=m_ssm_b_im, ssm_c_re=m_ssm_c_re, ssm_c_im=m_ssm_c_im, ssm_d=m_ssm_d,
             ssm_glu_w=m_ssm_glu_w, ssm_glu_b=m_ssm_glu_b, sg_ln_w=m_sg_ln_w, sg_ln_b=m_sg_ln_b, sg_w=m_sg_w,
             sg_b=m_sg_b, attn_sinks=m_attn_sinks, w_branch_a=m_w_branch_a, w_branch_b=m_w_branch_b,
             w_branch_c=m_w_branch_c, w_out=m_w_out, final_norm_w=m_final_norm_w)
    v = dict(norm_w=v_norm_w, w_in=v_w_in, ssm_a_re=v_ssm_a_re, ssm_a_im=v_ssm_a_im, ssm_log_dt=v_ssm_log_dt,
             ssm_b_re=v_ssm_b_re, ssm_b_im=v_ssm_b_im, ssm_c_re=v_ssm_c_re, ssm_c_im=v_ssm_c_im, ssm_d=v_ssm_d,
             ssm_glu_w=v_ssm_glu_w, ssm_glu_b=v_ssm_glu_b, sg_ln_w=v_sg_ln_w, sg_ln_b=v_sg_ln_b, sg_w=v_sg_w,
             sg_b=v_sg_b, attn_sinks=v_attn_sinks, w_branch_a=v_w_branch_a, w_branch_b=v_w_branch_b,
             w_branch_c=v_w_branch_c, w_out=v_w_out, final_norm_w=v_final_norm_w)

    big_names = ("winT", "glu_w", "wbaT", "wbbT", "wbcT", "w_out")
    L = x.shape[1]
    tabs = _rope_tables(L)
    p = {k: w[k] for k in _SMALL}

    shards = [[w["w_in"][l].T.astype(MXU), w["ssm_glu_w"][l].astype(MXU), w["w_branch_a"][l].T.astype(MXU),
               w["w_branch_b"][l].T.astype(MXU), w["w_branch_c"][l].T.astype(MXU), w["w_out"][l].astype(MXU)]
              for l in range(DEPTH)]
    full0 = _gather8(shards[0], split=True, name="gather_weights_0")
    full0, halves1 = lax.optimization_barrier((full0, [s.reshape(2, s.shape[0] // 2, s.shape[1]) for s in shards[1]]))
    lands1 = [_fill_own(s, name=f"gather_fill_1_{i}") for i, s in enumerate(halves1)]
    sems1, halves1, lands1, token = _split_start(
        halves1, lands1, lambda x_, y_, c_, px, py: c_, lambda x_, y_, c_: 4 * x_ + 2 * y_ + c_, name="gather_start_1")
    bigs = [dict(zip(big_names, full0)), None]

    p_fwd = dict(p, norm_w=p["norm_w"] + token[0, 0])
    saved = [None] * DEPTH
    x1, saved[0] = _layer_fwd(x[0], p_fwd, bigs[0], 0, tabs)
    _, lands1 = _split_wait(sems1, halves1, lands1, x1, name="gather_wait_1")
    lands1 = _pass_to_sibling(lands1, name="gather_pass_1")
    bigs[1] = dict(zip(big_names, [t.reshape(8 * t.shape[1], t.shape[2]) for t in lands1]))
    x2, saved[1] = _layer_fwd(x1, p, bigs[1], 1, tabs)
    loss, dx, dfw = _final_loss(x2, p["final_norm_w"], loss_target[0], name="final_loss")
    loss = lax.psum(loss, ("x", "y", "c"))

    grads = [None] * DEPTH
    big_of = lambda g: [g[k] for k in big_names]
    dproj, grads[1] = _layer_bwd(dx, saved[1], p, bigs[1], 1, tabs)
    rs1, token = _reduce_scatter_begin(big_of(grads[1]), tag="1")
    dx, grads[1]["norm_w"] = _layer_bwd_input(dproj, dx, saved[1], p, bigs[1], 1, token)
    dproj, grads[0] = _layer_bwd(dx, saved[0], p, bigs[0], 0, tabs)
    red = [None, _reduce_scatter_end(rs1, dproj, tag="1")]
    rs0, token = _reduce_scatter_begin(big_of(grads[0]), tag="0")
    dx, grads[0]["norm_w"] = _layer_bwd_input(dproj, dx, saved[0], p, bigs[0], 0, token)

    small_like = [w[k] for k in _SMALL]
    gs = [jnp.stack([grads[l][k] for l in range(DEPTH)]) if k != "final_norm_w" else dfw for k in _SMALL]
    packed = _pack(gs)
    allg = _gather8([packed], split=False, name="gather_small_grads")[0]
    gsum = _sum_slots(allg.reshape(8, packed.shape[0], _PACK_COLS), name="sum_small_grads")
    red[0] = _reduce_scatter_end(rs0, gsum, tag="0")

    gbig = {k: [] for k in _BIG}
    for l in range(DEPTH):
        gbig["w_in"].append(red[l][0])
        gbig["ssm_glu_w"].append(red[l][1])
        gbig["w_branch_a"].append(red[l][2].T)
        gbig["w_branch_b"].append(red[l][3].T)
        gbig["w_branch_c"].append(red[l][4].T)
        gbig["w_out"].append(red[l][5])
    gfull = {k: jnp.stack(vs) for k, vs in gbig.items()}
    for k, t in zip(_SMALL, _unpack(gsum, small_like)):
        gfull[k] = t

    delta, new_m, new_v = {}, {}, {}
    for k in _BIG:
        if k == "w_in":
            tr = lambda t: t.transpose(0, 2, 1)
            res = _adamw(tr(w[k]), gfull[k], tr(m[k]), tr(v[k]), name=f"adamw_{k}")
            delta[k], new_m[k], new_v[k] = (tr(t) for t in res)
            gfull[k] = tr(gfull[k])
        else:
            delta[k], new_m[k], new_v[k] = _adamw(w[k], gfull[k], m[k], v[k], name=f"adamw_{k}")
    for k in _SMALL:
        delta[k], new_m[k], new_v[k] = _adamw(w[k], gfull[k], m[k], v[k], name=f"adamw_{k}")

    return (loss, dx[None], *[gfull[k] for k in _WEIGHTS], *[delta[k] for k in _WEIGHTS],
            *[new_m[k] for k in _WEIGHTS], *[new_v[k] for k in _WEIGHTS])
```

```python
import functools
import math

import numpy as np
import jax
import jax.numpy as jnp
from jax import lax
from jax.experimental import pallas as pl
from jax.experimental.pallas import tpu as pltpu

F32 = jnp.float32
MXU = jnp.bfloat16
HIGHEST = lax.Precision.HIGHEST

D_MODEL = 2048
DEPTH = 2
EPS = 1e-6
NEG_INF = -1e30
SSM_WIDTH = 1024
SSM_GROUP = 16
SSM_GROUPS = 64
SSM_STATE = 64
SSM_CH = SSM_GROUPS * SSM_STATE
SLAB = 128
SLAB_CH = (SLAB // SSM_GROUP) * SSM_STATE
N_SLAB = SSM_WIDTH // SLAB
SCAN_SEG = 8
SCAN_STEPS = 4
SG_HEADS = 8
SG_CHUNK = 128
HEAD_DIM = 64
ATT_HEADS = 16
ATT_KV_HEADS = 2
GQA_GROUP = 8
ATT_BLOCK = 128
WINDOW = 128
ROT_DIM = 16
ROPE_THETA = 500000.0
N_MAIN = 6400
N_ZC = 1024
N_GATES = 6144
D_IN = N_MAIN + N_ZC + N_GATES

ADAM_LR = 0.001
ADAM_B1 = 0.9
ADAM_B2 = 0.999
ADAM_EPS = 1e-08
ADAM_WD = 0.01
ADAM_STEP = 10

_DIMS = {"nn": (((1,), (0,)), ((), ())), "nt": (((1,), (1,)), ((), ())), "tn": (((0,), (0,)), ((), ()))}
_MB = 1024 * 1024


def _cp(sem, vmem_mb=48):
    return pltpu.CompilerParams(dimension_semantics=sem, vmem_limit_bytes=vmem_mb * _MB)


def _dot(a, b, mode):
    return lax.dot_general(a.astype(MXU), b.astype(MXU), _DIMS[mode], preferred_element_type=F32)


@jax.custom_vjp
def _mm_nn(a, b):
    return _dot(a, b, "nn")


def _mm_nn_fwd(a, b):
    return _dot(a, b, "nn"), (a, b)


def _mm_nn_bwd(res, g):
    a, b = res
    return _dot(g, b, "nt"), _dot(a, g, "tn")


_mm_nn.defvjp(_mm_nn_fwd, _mm_nn_bwd)


@jax.custom_vjp
def _mm_nt(a, bt):
    return _dot(a, bt, "nt")


def _mm_nt_fwd(a, bt):
    return _dot(a, bt, "nt"), (a, bt)


def _mm_nt_bwd(res, g):
    a, bt = res
    return _dot(g, bt, "nn"), _dot(g, a, "tn")


_mm_nt.defvjp(_mm_nt_fwd, _mm_nt_bwd)


def _rmsnorm(x, w):
    return x * lax.rsqrt(jnp.mean(x * x, axis=-1, keepdims=True) + EPS) * w


def _layernorm(x, w, b):
    mu = jnp.mean(x, axis=-1, keepdims=True)
    var = jnp.mean(jnp.square(x - mu), axis=-1, keepdims=True)
    return (x - mu) * lax.rsqrt(var + EPS) * w + b


def _silu(x):
    return x * jax.nn.sigmoid(x)


def _matmul(a, b, mode, *, name, shape, tm, tn, tk, out_dtype=F32, add=None, a_off=(0, 0), b_off=(0, 0), after=None):
    m, n, k = shape
    tm, tn, tk = min(tm, m), min(tn, n), min(tk, k)
    assert m % tm == 0 and n % tn == 0 and k % tk == 0, (name, shape, tm, tn, tk)
    nk = k // tk
    has_add, has_after = add is not None, after is not None

    def body(*refs):
        a_ref, b_ref = refs[0], refs[1]
        pos = 2
        add_ref = None
        if has_add:
            add_ref = refs[pos]
            pos += 1
        if has_after:
            pos += 1
        o_ref = refs[pos]
        p = _dot(a_ref[...], b_ref[...], mode)
        if nk == 1:
            if has_add:
                p = p + add_ref[...].astype(F32)
            o_ref[...] = p.astype(out_dtype)
            return
        acc_ref = refs[pos + 1]
        kk = pl.program_id(2)

        @pl.when(kk == 0)
        def _():
            acc_ref[...] = p

        @pl.when(kk > 0)
        def _():
            acc_ref[...] += p

        @pl.when(kk == nk - 1)
        def _():
            r = acc_ref[...]
            if has_add:
                r = r + add_ref[...].astype(F32)
            o_ref[...] = r.astype(out_dtype)

    a0, a1 = a_off
    b0, b1 = b_off
    if mode == "tn":
        a_spec = pl.BlockSpec((tk, tm), lambda i, j, kk: (kk + a0, i + a1))
    else:
        a_spec = pl.BlockSpec((tm, tk), lambda i, j, kk: (i + a0, kk + a1))
    if mode == "nt":
        b_spec = pl.BlockSpec((tn, tk), lambda i, j, kk: (j + b0, kk + b1))
    else:
        b_spec = pl.BlockSpec((tk, tn), lambda i, j, kk: (kk + b0, j + b1))
    in_specs = [a_spec, b_spec]
    args = [a, b]
    if has_add:
        in_specs.append(pl.BlockSpec((tm, tn), lambda i, j, kk: (i, j)))
        args.append(add)
    if has_after:
        in_specs.append(pl.BlockSpec(memory_space=pl.ANY))
        args.append(after)
    return pl.pallas_call(
        body, name=name, grid=(m // tm, n // tn, nk),
        in_specs=in_specs,
        out_specs=pl.BlockSpec((tm, tn), lambda i, j, kk: (i, j)),
        out_shape=jax.ShapeDtypeStruct((m, n), out_dtype),
        scratch_shapes=[pltpu.VMEM((tm, tn), F32)] if nk > 1 else [],
        compiler_params=_cp(("parallel", "parallel", "arbitrary")),
    )(*args)


def _rms_fwd(x, w, *, name, tm=256):
    L, d = x.shape
    tm = min(tm, L)

    def body(x_ref, w_ref, h_ref):
        h_ref[...] = _rmsnorm(x_ref[...], w_ref[...]).astype(MXU)

    return pl.pallas_call(
        body, name=name, grid=(L // tm,),
        in_specs=[pl.BlockSpec((tm, d), lambda i: (i, 0)), pl.BlockSpec((1, d), lambda i: (0, 0))],
        out_specs=pl.BlockSpec((tm, d), lambda i: (i, 0)),
        out_shape=jax.ShapeDtypeStruct((L, d), MXU),
        compiler_params=_cp(("parallel",)),
    )(x, w.reshape(1, d))


def _rms_bwd(x, w, dh, dxn, *, name, tm=256):
    L, d = x.shape
    tm = min(tm, L)

    def body(x_ref, w_ref, dh_ref, dxn_ref, dx_ref, dw_ref):
        _, vjp = jax.vjp(_rmsnorm, x_ref[...], w_ref[...])
        dx, dw = vjp(dh_ref[...])
        dx_ref[...] = dx + dxn_ref[...]

        @pl.when(pl.program_id(0) == 0)
        def _():
            dw_ref[...] = jnp.zeros_like(dw_ref)

        dw_ref[...] += dw

    row = pl.BlockSpec((tm, d), lambda i: (i, 0))
    vec = pl.BlockSpec((1, d), lambda i: (0, 0))
    dx, dw = pl.pallas_call(
        body, name=name, grid=(L // tm,),
        in_specs=[row, vec, row, row], out_specs=[row, vec],
        out_shape=[jax.ShapeDtypeStruct((L, d), F32), jax.ShapeDtypeStruct((1, d), F32)],
        compiler_params=_cp(("arbitrary",)),
    )(x, w.reshape(1, d), dh, dxn)
    return dx, dw.reshape(d)


def _final_loss(x, w, tgt, *, name, tm=256):
    L, d = x.shape
    tm = min(tm, L)

    def loss_fn(xv, wv, tv):
        err = jnp.square(_rmsnorm(xv, wv) - tv)
        return 0.5 * jnp.sum(jnp.mean(err, axis=-1, keepdims=True), axis=0, keepdims=True)

    def body(x_ref, w_ref, t_ref, loss_ref, dx_ref, dw_ref):
        tv = t_ref[...]
        val, vjp = jax.vjp(lambda xv, wv: loss_fn(xv, wv, tv), x_ref[...], w_ref[...])
        dx, dw = vjp(jnp.ones((1, 1), F32))
        dx_ref[...] = dx

        @pl.when(pl.program_id(0) == 0)
        def _():
            dw_ref[...] = jnp.zeros_like(dw_ref)
            loss_ref[...] = jnp.zeros_like(loss_ref)

        dw_ref[...] += dw
        loss_ref[...] += jnp.broadcast_to(val, loss_ref.shape)

    row = pl.BlockSpec((tm, d), lambda i: (i, 0))
    vec = pl.BlockSpec((1, d), lambda i: (0, 0))
    loss, dx, dw = pl.pallas_call(
        body, name=name, grid=(L // tm,),
        in_specs=[row, vec, row],
        out_specs=[pl.BlockSpec((8, 128), lambda i: (0, 0)), row, vec],
        out_shape=[jax.ShapeDtypeStruct((8, 128), F32), jax.ShapeDtypeStruct((L, d), F32),
                   jax.ShapeDtypeStruct((1, d), F32)],
        compiler_params=_cp(("arbitrary",)),
    )(x, w.reshape(1, d), tgt)
    return loss[0, 0], dx, dw.reshape(d)


PARAM_ROWS = 512


def _s5_param_fn(are, aim, ldt, bre, bim, row0):
    n = are.shape[0]
    grp = (row0 + lax.broadcasted_iota(jnp.int32, (n, SSM_GROUPS), 0)) // SSM_STATE
    col = lax.broadcasted_iota(jnp.int32, (n, SSM_GROUPS), 1)
    sel = (grp == col).astype(F32)
    dt = jnp.sum(sel * jnp.exp(ldt), axis=-1, keepdims=True)
    mag = jnp.exp(are * dt)
    ang = aim * dt
    lbr = mag * jnp.cos(ang)
    lbi = mag * jnp.sin(ang)
    den = are * are + aim * aim
    nr = lbr - 1.0
    kr = (nr * are + lbi * aim) / den
    ki = (lbi * are - nr * aim) / den
    return lbr, lbi, kr * bre - ki * bim, kr * bim + ki * bre


def _s5_param_specs():
    col = pl.BlockSpec((PARAM_ROWS, 1), lambda i: (i, 0))
    mat = pl.BlockSpec((PARAM_ROWS, SSM_GROUP), lambda i: (i, 0))
    vec = pl.BlockSpec((1, SSM_GROUPS), lambda i: (0, 0))
    return col, mat, vec


def _s5_params_fwd(are, aim, ldt, bre, bim, *, name):
    n = are.shape[0]
    col, mat, vec = _s5_param_specs()

    def body(are_ref, aim_ref, ldt_ref, bre_ref, bim_ref, lbr_ref, lbi_ref, bbr_ref, bbi_ref):
        row0 = pl.program_id(0) * PARAM_ROWS
        lbr, lbi, bbr, bbi = _s5_param_fn(are_ref[...], aim_ref[...], ldt_ref[...], bre_ref[...], bim_ref[...], row0)
        lbr_ref[...] = lbr
        lbi_ref[...] = lbi
        bbr_ref[...] = bbr
        bbi_ref[...] = bbi

    cshape = jax.ShapeDtypeStruct((n, 1), F32)
    mshape = jax.ShapeDtypeStruct((n, SSM_GROUP), F32)
    return pl.pallas_call(body, name=name, grid=(n // PARAM_ROWS,),
                          in_specs=[col, col, vec, mat, mat], out_specs=[col, col, mat, mat],
                          out_shape=[cshape, cshape, mshape, mshape],
                          compiler_params=_cp(("parallel",)))(are, aim, ldt, bre, bim)


def _s5_params_bwd(are, aim, ldt, bre, bim, dlbr, dlbi, dbbr, dbbi, *, name):
    n = are.shape[0]
    col, mat, vec = _s5_param_specs()

    def body(are_ref, aim_ref, ldt_ref, bre_ref, bim_ref, g0, g1, g2, g3, o0, o1, o2, o3, o4):
        row0 = pl.program_id(0) * PARAM_ROWS
        _, vjp = jax.vjp(lambda a, b, c, d, e: _s5_param_fn(a, b, c, d, e, row0),
                         are_ref[...], aim_ref[...], ldt_ref[...], bre_ref[...], bim_ref[...])
        dare, daim, dldt, dbre, dbim = vjp((g0[...], g1[...], g2[...], g3[...]))
        o0[...] = dare
        o1[...] = daim
        o3[...] = dbre
        o4[...] = dbim

        @pl.when(pl.program_id(0) == 0)
        def _():
            o2[...] = jnp.zeros_like(o2)

        o2[...] += dldt

    cshape = jax.ShapeDtypeStruct((n, 1), F32)
    mshape = jax.ShapeDtypeStruct((n, SSM_GROUP), F32)
    return pl.pallas_call(body, name=name, grid=(n // PARAM_ROWS,),
                          in_specs=[col, col, vec, mat, mat, col, col, mat, mat],
                          out_specs=[col, col, vec, mat, mat],
                          out_shape=[cshape, cshape, jax.ShapeDtypeStruct((1, SSM_GROUPS), F32), mshape, mshape],
                          compiler_params=_cp(("arbitrary",)))(are, aim, ldt, bre, bim, dlbr, dlbi, dbbr, dbbi)


SLAB_NC = SLAB_CH // 128


def _s5_specs(L):
    slab = pl.BlockSpec((L, SLAB), lambda s: (0, s))
    wspec = pl.BlockSpec((SLAB_NC, 128, SLAB), lambda s: (s, 0, 0))
    lspec = pl.BlockSpec((SLAB_NC, 1, 128), lambda s: (s, 0, 0))
    sspec = pl.BlockSpec((SLAB_NC, L, 128), lambda s: (s, 0, 0))
    dspec = pl.BlockSpec((1, SLAB), lambda s: (0, s))
    return slab, wspec, lspec, sspec, dspec


def _scan_inplace(sr_ref, si_ref, lr, li, pr_ref, pi_ref, *, reverse):
    NC, L, W = sr_ref.shape
    S = SCAN_SEG
    T = L // S
    lr8 = [jnp.broadcast_to(lr[k], (S, W)) for k in range(NC)]
    li8 = [jnp.broadcast_to(li[k], (S, W)) for k in range(NC)]

    def tiles(first, count):
        return pl.ds(first * S, count * S)

    for k in range(NC):
        pr_ref[k, tiles(T - 1 if reverse else 0, 1), :] = lr8[k]
        pi_ref[k, tiles(T - 1 if reverse else 0, 1), :] = li8[k]
        n = 1
        while n < T:
            have = tiles(T - n, n) if reverse else tiles(0, n)
            new = tiles(T - 2 * n, n) if reverse else tiles(n, n)
            top = tiles(T - n, 1) if reverse else tiles(n - 1, 1)
            ar, ai = pr_ref[k, top, :][None], pi_ref[k, top, :][None]
            hr, hi = pr_ref[k, have, :].reshape(n, S, W), pi_ref[k, have, :].reshape(n, S, W)
            pr_ref[k, new, :] = (hr * ar - hi * ai).reshape(n * S, W)
            pi_ref[k, new, :] = (hr * ai + hi * ar).reshape(n * S, W)
            n *= 2

    def step(i, carry):
        for u in range(SCAN_STEPS):
            jj = i * SCAN_STEPS + u
            rows = pl.ds(pl.multiple_of(((T - 1 - jj) if reverse else jj) * S, S), S)
            out = []
            for k in range(NC):
                sr, si = carry[k]
                nsr = lr8[k] * sr - li8[k] * si + sr_ref[k, rows, :]
                nsi = lr8[k] * si + li8[k] * sr + si_ref[k, rows, :]
                sr_ref[k, rows, :] = nsr
                si_ref[k, rows, :] = nsi
                out.append((nsr, nsi))
            carry = tuple(out)
        return carry

    zero = jnp.zeros((S, W), F32)
    ends = lax.fori_loop(0, T // SCAN_STEPS, step, tuple((zero, zero) for k in range(NC)))
    sub = lax.broadcasted_iota(jnp.int32, (S, W), 0)
    order = range(S - 1, -1, -1) if reverse else range(S)
    for k in range(NC):
        er, ei = ends[k]
        full = tiles(0 if reverse else T - 1, 1)
        ltr = pr_ref[k, full, :][0:1]
        lti = pi_ref[k, full, :][0:1]
        cr = jnp.zeros((1, W), F32)
        ci = jnp.zeros((1, W), F32)
        ctr = jnp.zeros((S, W), F32)
        cti = jnp.zeros((S, W), F32)
        for seg in order:
            ctr = jnp.where(sub == seg, cr, ctr)
            cti = jnp.where(sub == seg, ci, cti)
            cr, ci = (er[seg:seg + 1, :] + ltr * cr - lti * ci, ei[seg:seg + 1, :] + ltr * ci + lti * cr)
        pr = pr_ref[k].reshape(T, S, W)
        pi = pi_ref[k].reshape(T, S, W)
        sr_ref[k] += (pr * ctr[None] - pi * cti[None]).reshape(L, W)
        si_ref[k] += (pr * cti[None] + pi * ctr[None]).reshape(L, W)


def _time_interleave(a):
    L, W = a.shape
    return a.reshape(SCAN_SEG, L // SCAN_SEG, W).transpose(1, 0, 2).reshape(L, W)


def _time_deinterleave(a):
    L, W = a.shape
    return a.reshape(L // SCAN_SEG, SCAN_SEG, W).transpose(1, 0, 2).reshape(L, W)


def _s5_fwd(u, btr, bti, cbr, cbi, lbr, lbi, dvec, *, name):
    L = u.shape[0]

    def body(u_ref, btr_ref, bti_ref, cbr_ref, cbi_ref, lr_ref, li_ref, d_ref, ys_ref, sr_ref, si_ref, pr_ref, pi_ref):
        u = u_ref[...]
        for k in range(SLAB_NC):
            sr_ref[k] = _dot(u, btr_ref[k], "nt")
            si_ref[k] = _dot(u, bti_ref[k], "nt")
        _scan_inplace(sr_ref, si_ref, lr_ref[...], li_ref[...], pr_ref, pi_ref, reverse=False)
        ys = d_ref[...] * u
        for k in range(SLAB_NC):
            ys = ys + _dot(sr_ref[k], cbr_ref[k], "nn") - _dot(si_ref[k], cbi_ref[k], "nn")
        ys_ref[...] = ys

    slab, wspec, lspec, sspec, dspec = _s5_specs(L)
    sshape = jax.ShapeDtypeStruct((N_SLAB * SLAB_NC, L, 128), F32)
    return pl.pallas_call(
        body, name=name, grid=(N_SLAB,),
        in_specs=[slab, wspec, wspec, wspec, wspec, lspec, lspec, dspec],
        out_specs=[slab, sspec, sspec],
        out_shape=[jax.ShapeDtypeStruct((L, SSM_WIDTH), F32), sshape, sshape],
        scratch_shapes=[pltpu.VMEM((SLAB_NC, L, 128), F32), pltpu.VMEM((SLAB_NC, L, 128), F32)],
        compiler_params=_cp(("parallel",), 56),
    )(u, btr, bti, cbr, cbi, lbr, lbi, dvec)


def _s5_bwd(dys, u, sr, si, btr, bti, cbr, cbi, lbr, lbi, dvec, *, name):
    L = u.shape[0]
    S = SCAN_SEG

    def body(dys_ref, u_ref, sr_ref, si_ref, btr_ref, bti_ref, cbr_ref, cbi_ref, lr_ref, li_ref, d_ref,
             du_ref, dbtr_ref, dbti_ref, dcbr_ref, dcbi_ref, dlr_ref, dli_ref, dd_ref,
             ar_ref, ai_ref, pr_ref, pi_ref):
        dys = dys_ref[...]
        u = u_ref[...]
        for k in range(SLAB_NC):
            ar_ref[k] = _dot(dys, cbr_ref[k], "nt")
            ai_ref[k] = -_dot(dys, cbi_ref[k], "nt")
        _scan_inplace(ar_ref, ai_ref, lr_ref[...], -li_ref[...], pr_ref, pi_ref, reverse=True)
        head = lax.broadcasted_iota(jnp.int32, (L, 1), 0) < S
        sub0 = lax.broadcasted_iota(jnp.int32, (S, 1), 0) == 0

        def prev_state(s):
            up = pltpu.roll(s, S, 0)
            return jnp.where(head, 0.0, up), jnp.where(sub0, 0.0, pltpu.roll(up[0:S], 1, 0))

        du = d_ref[...] * dys
        for k in range(SLAB_NC):
            a_re = ar_ref[k]
            a_im = ai_ref[k]
            du = du + _dot(a_re, btr_ref[k], "nn") + _dot(a_im, bti_ref[k], "nn")
            dbtr_ref[k] = _dot(a_re, u, "tn")
            dbti_ref[k] = _dot(a_im, u, "tn")
            s_re = sr_ref[k]
            s_im = si_ref[k]
            dcbr_ref[k] = _dot(s_re, dys, "tn")
            dcbi_ref[k] = -_dot(s_im, dys, "tn")
            p_re, q_re = prev_state(s_re)
            p_im, q_im = prev_state(s_im)
            b_re, b_im = a_re[0:S], a_im[0:S]
            dlr_ref[k] = (jnp.sum(p_re * a_re + p_im * a_im, axis=0, keepdims=True)
                          + jnp.sum(q_re * b_re + q_im * b_im, axis=0, keepdims=True))
            dli_ref[k] = (jnp.sum(p_re * a_im - p_im * a_re, axis=0, keepdims=True)
                          + jnp.sum(q_re * b_im - q_im * b_re, axis=0, keepdims=True))
        du_ref[...] = du
        dd_ref[...] = jnp.sum(dys * u, axis=0, keepdims=True)

    slab, wspec, lspec, sspec, dspec = _s5_specs(L)
    wshape = jax.ShapeDtypeStruct((N_SLAB * SLAB_NC, 128, SLAB), F32)
    lshape = jax.ShapeDtypeStruct((N_SLAB * SLAB_NC, 1, 128), F32)
    return pl.pallas_call(
        body, name=name, grid=(N_SLAB,),
        in_specs=[slab, slab, sspec, sspec, wspec, wspec, wspec, wspec, lspec, lspec, dspec],
        out_specs=[slab, wspec, wspec, wspec, wspec, lspec, lspec, dspec],
        out_shape=[jax.ShapeDtypeStruct((L, SSM_WIDTH), F32), wshape, wshape, wshape, wshape, lshape, lshape,
                   jax.ShapeDtypeStruct((1, SSM_WIDTH), F32)],
        scratch_shapes=[pltpu.VMEM((SLAB_NC, L, 128), F32)] * 4,
        compiler_params=_cp(("parallel",), 56),
    )(dys, u, sr, si, btr, bti, cbr, cbi, lbr, lbi, dvec)


_SLAB_MASK = (np.arange(SLAB_CH)[:, None] // SSM_STATE == np.arange(SLAB)[None, :] // SSM_GROUP)


def _expand_bd(x):
    t = jnp.tile(x.reshape(N_SLAB, SLAB_CH, SSM_GROUP), (1, 1, SLAB // SSM_GROUP))
    return jnp.where(_SLAB_MASK[None], t, 0.0).astype(MXU).reshape(N_SLAB * SLAB_NC, 128, SLAB)


def _contract_bd(dx):
    t = jnp.where(_SLAB_MASK[None], dx.reshape(N_SLAB, SLAB_CH, SLAB), 0.0)
    return jnp.sum(t.reshape(N_SLAB, SLAB_CH, SLAB // SSM_GROUP, SSM_GROUP), axis=2).reshape(SSM_CH, SSM_GROUP)


def _glu_ew(ys, zlin, za):
    a1 = jax.nn.gelu(ys)
    return a1 * jax.nn.sigmoid(zlin) * _silu(za)


def _glu_fwd(ys, main, gw, gb, *, name, tm=256):
    L = ys.shape[0]
    tm = min(tm, L)
    W = SSM_WIDTH

    def body(ys_ref, za_ref, gw_ref, gb_ref, ya_ref):
        ys = ys_ref[...]
        a1 = jax.nn.gelu(ys)
        zlin = _dot(a1, gw_ref[...], "nn") + gb_ref[...]
        ya_ref[...] = _glu_ew(ys, zlin, za_ref[...]).astype(MXU)

    return pl.pallas_call(
        body, name=name, grid=(L // tm,),
        in_specs=[pl.BlockSpec((tm, W), lambda i: (i, 0)), pl.BlockSpec((tm, W), lambda i: (i, 1)),
                  pl.BlockSpec((W, W), lambda i: (0, 0)), pl.BlockSpec((1, W), lambda i: (0, 0))],
        out_specs=pl.BlockSpec((tm, W), lambda i: (i, 0)),
        out_shape=jax.ShapeDtypeStruct((L, W), MXU),
        compiler_params=_cp(("parallel",)),
    )(ys, main, gw, gb.reshape(1, W))


def _glu_bwd(dya, ys, main, gw, gb, *, name, tm=256):
    L = ys.shape[0]
    tm = min(tm, L)
    W = SSM_WIDTH

    def body(dya_ref, ys_ref, za_ref, gw_ref, gb_ref, dys_ref, dza_ref, a1_ref, dzl_ref, db_ref):
        ys = ys_ref[...]
        a1, gelu_vjp = jax.vjp(jax.nn.gelu, ys)
        zlin = _dot(a1, gw_ref[...], "nn") + gb_ref[...]
        _, vjp = jax.vjp(lambda a, z, za: a * jax.nn.sigmoid(z) * _silu(za), a1, zlin, za_ref[...])
        da1, dzlin, dza = vjp(dya_ref[...].astype(F32))
        da1 = da1 + _dot(dzlin, gw_ref[...], "nt")
        dys_ref[...] = gelu_vjp(da1)[0]
        dza_ref[...] = dza
        a1_ref[...] = a1.astype(MXU)
        dzl_ref[...] = dzlin.astype(MXU)

        @pl.when(pl.program_id(0) == 0)
        def _():
            db_ref[...] = jnp.zeros_like(db_ref)

        db_ref[...] += jnp.sum(dzlin, axis=0, keepdims=True)

    row = pl.BlockSpec((tm, W), lambda i: (i, 0))
    vec = pl.BlockSpec((1, W), lambda i: (0, 0))
    return pl.pallas_call(
        body, name=name, grid=(L // tm,),
        in_specs=[row, row, pl.BlockSpec((tm, W), lambda i: (i, 1)), pl.BlockSpec((W, W), lambda i: (0, 0)), vec],
        out_specs=[row, row, row, row, vec],
        out_shape=[jax.ShapeDtypeStruct((L, W), F32), jax.ShapeDtypeStruct((L, W), F32),
                   jax.ShapeDtypeStruct((L, W), MXU), jax.ShapeDtypeStruct((L, W), MXU),
                   jax.ShapeDtypeStruct((1, W), F32)],
        compiler_params=_cp(("arbitrary",)),
    )(dya, ys, main, gw, gb.reshape(1, W))


def _sg_fn(ub, vb, zb, lnw, lnb, ws, bs):
    u = jax.nn.gelu(ub)
    v = _layernorm(jax.nn.gelu(vb), lnw, lnb)
    r = lax.broadcasted_iota(jnp.int32, (SG_CHUNK, SG_CHUNK), 0)
    c = lax.broadcasted_iota(jnp.int32, (SG_CHUNK, SG_CHUNK), 1)
    tri = r >= c
    outs = []
    for h in range(SG_HEADS):
        wh = jnp.where(tri, ws[h], 0.0)
        outs.append(_mm_nn(wh, v[:, h * 128:(h + 1) * 128]) + bs[h])
    mixed = jnp.concatenate(outs, axis=1)
    return u * mixed * _silu(zb)


def _sg_specs(L):
    W = SSM_WIDTH
    blk = lambda c: pl.BlockSpec((SG_CHUNK, W), lambda i, c=c: (i, c))
    vec = pl.BlockSpec((1, W), lambda i: (0, 0))
    wspec = pl.BlockSpec((SG_HEADS, SG_CHUNK, SG_CHUNK), lambda i: (0, 0, 0))
    bspec = pl.BlockSpec((SG_HEADS, SG_CHUNK, 1), lambda i: (0, 0, 0))
    return blk, vec, wspec, bspec


def _sg_fwd(main, lnw, lnb, sgw, sgb, *, name):
    L = main.shape[0]
    W = SSM_WIDTH
    blk, vec, wspec, bspec = _sg_specs(L)

    def body(ub_ref, vb_ref, zb_ref, lnw_ref, lnb_ref, w_ref, b_ref, yb_ref):
        ws = [w_ref[h] for h in range(SG_HEADS)]
        bs = [b_ref[h] for h in range(SG_HEADS)]
        yb_ref[...] = _sg_fn(ub_ref[...], vb_ref[...], zb_ref[...], lnw_ref[...], lnb_ref[...], ws, bs).astype(MXU)

    return pl.pallas_call(
        body, name=name, grid=(L // SG_CHUNK,),
        in_specs=[blk(2), blk(3), blk(4), vec, vec, wspec, bspec],
        out_specs=pl.BlockSpec((SG_CHUNK, W), lambda i: (i, 0)),
        out_shape=jax.ShapeDtypeStruct((L, W), MXU),
        compiler_params=_cp(("parallel",)),
    )(main, main, main, lnw.reshape(1, W), lnb.reshape(1, W), sgw, sgb.reshape(SG_HEADS, SG_CHUNK, 1))


def _sg_bwd(dyb, main, lnw, lnb, sgw, sgb, *, name):
    L = main.shape[0]
    W = SSM_WIDTH
    blk, vec, wspec, bspec = _sg_specs(L)

    def body(dyb_ref, ub_ref, vb_ref, zb_ref, lnw_ref, lnb_ref, w_ref, b_ref,
             dub_ref, dvb_ref, dzb_ref, dlnw_ref, dlnb_ref, dw_ref, db_ref):
        ws = [w_ref[h] for h in range(SG_HEADS)]
        bs = [b_ref[h] for h in range(SG_HEADS)]
        _, vjp = jax.vjp(_sg_fn, ub_ref[...], vb_ref[...], zb_ref[...], lnw_ref[...], lnb_ref[...], ws, bs)
        dub, dvb, dzb, dlnw, dlnb, dws, dbs = vjp(dyb_ref[...])

        @pl.when(pl.program_id(0) == 0)
        def _():
            dlnw_ref[...] = jnp.zeros_like(dlnw_ref)
            dlnb_ref[...] = jnp.zeros_like(dlnb_ref)
            dw_ref[...] = jnp.zeros_like(dw_ref)
            db_ref[...] = jnp.zeros_like(db_ref)

        dub_ref[...] = dub
        dvb_ref[...] = dvb
        dzb_ref[...] = dzb
        dlnw_ref[...] += dlnw
        dlnb_ref[...] += dlnb
        for h in range(SG_HEADS):
            dw_ref[h] += dws[h]
            db_ref[h] += dbs[h]

    row = pl.BlockSpec((SG_CHUNK, W), lambda i: (i, 0))
    out = jax.ShapeDtypeStruct((L, W), F32)
    return pl.pallas_call(
        body, name=name, grid=(L // SG_CHUNK,),
        in_specs=[row, blk(2), blk(3), blk(4), vec, vec, wspec, bspec],
        out_specs=[row, row, row, vec, vec, wspec, bspec],
        out_shape=[out, out, out, jax.ShapeDtypeStruct((1, W), F32), jax.ShapeDtypeStruct((1, W), F32),
                   jax.ShapeDtypeStruct((SG_HEADS, SG_CHUNK, SG_CHUNK), F32),
                   jax.ShapeDtypeStruct((SG_HEADS, SG_CHUNK, 1), F32)],
        compiler_params=_cp(("arbitrary",)),
    )(dyb, main, main, main, lnw.reshape(1, W), lnb.reshape(1, W), sgw, sgb.reshape(SG_HEADS, SG_CHUNK, 1))


def _rope_tables(L):
    half = ROT_DIM // 2
    inv_freq = ROPE_THETA ** (-jnp.arange(0, ROT_DIM, 2, dtype=F32) / ROT_DIM)
    ang = jnp.arange(L, dtype=F32)[:, None] * inv_freq[None, :]
    cos = jnp.cos(ang)
    sin = jnp.sin(ang)
    ones = jnp.ones((L, HEAD_DIM - ROT_DIM), F32)
    cosf = jnp.concatenate([cos, cos, ones], axis=1)
    sinf = jnp.concatenate([sin, sin, 0.0 * ones], axis=1)
    rot = np.zeros((HEAD_DIM, HEAD_DIM), np.float32)
    for d in range(half):
        rot[d + half, d] = -1.0
        rot[d, d + half] = 1.0
    return cosf, sinf, jnp.asarray(rot)


def _rope(t, cosf, sinf, rot):
    shp = t.shape
    t2 = t.reshape(-1, HEAD_DIM)
    sw = lax.dot_general(t2, rot, _DIMS["nn"], precision=HIGHEST, preferred_element_type=F32).reshape(shp)
    return t * cosf + sw * sinf


def _attn_block_fn(q, kw, vw, sinks, cq, sq, ck, sk, rot, q0, k0):
    nk = kw.shape[1]
    qr = _rope(q, cq, sq, rot)
    kr = _rope(kw, ck, sk, rot)
    qpos = q0 + lax.broadcasted_iota(jnp.int32, (1, ATT_BLOCK, nk), 1)
    kpos = k0 + lax.broadcasted_iota(jnp.int32, (1, ATT_BLOCK, nk), 2)
    diff = qpos - kpos
    allowed = (diff >= 0) & (diff < WINDOW)
    outs = []
    for kh in range(ATT_KV_HEADS):
        qh = qr[kh * GQA_GROUP:(kh + 1) * GQA_GROUP].reshape(GQA_GROUP * ATT_BLOCK, HEAD_DIM)
        s = _mm_nt(qh, kr[kh]).reshape(GQA_GROUP, ATT_BLOCK, nk) * (HEAD_DIM ** -0.5)
        s = jnp.where(allowed, s, NEG_INF)
        sink = sinks[kh * GQA_GROUP:(kh + 1) * GQA_GROUP]
        m = lax.stop_gradient(jnp.maximum(jnp.max(s, axis=-1, keepdims=True), sink))
        e = jnp.exp(s - m)
        p = e / (jnp.sum(e, axis=-1, keepdims=True) + jnp.exp(sink - m))
        o = _mm_nn(p.reshape(GQA_GROUP * ATT_BLOCK, nk), vw[kh])
        outs.append(o.reshape(GQA_GROUP, ATT_BLOCK, HEAD_DIM))
    return jnp.concatenate(outs, axis=0)


def _attn_common(L):
    nwin = min(2 * ATT_BLOCK, L)
    qspec = pl.BlockSpec((ATT_HEADS, ATT_BLOCK, HEAD_DIM), lambda n: (0, n, 0))
    kvspec = pl.BlockSpec((ATT_KV_HEADS, L, HEAD_DIM), lambda n: (0, 0, 0))
    sspec = pl.BlockSpec((ATT_HEADS, 1, 1), lambda n: (0, 0, 0))
    tq = pl.BlockSpec((ATT_BLOCK, HEAD_DIM), lambda n: (n, 0))
    tk = pl.BlockSpec((L, HEAD_DIM), lambda n: (0, 0))
    rspec = pl.BlockSpec((HEAD_DIM, HEAD_DIM), lambda n: (0, 0))
    return nwin, qspec, kvspec, sspec, tq, tk, rspec


def _attn_fwd(qh, kh, vh, sinks, cosf, sinf, rot, *, name):
    L = qh.shape[1]
    nwin, qspec, kvspec, sspec, tq, tk, rspec = _attn_common(L)

    def body(q_ref, k_ref, v_ref, s_ref, cq_ref, sq_ref, ck_ref, sk_ref, r_ref, o_ref):
        n = pl.program_id(0)
        k0 = pl.multiple_of(jnp.maximum(n - 1, 0) * ATT_BLOCK, ATT_BLOCK)
        win = pl.ds(k0, nwin)
        o_ref[...] = _attn_block_fn(q_ref[...], k_ref[:, win, :], v_ref[:, win, :], s_ref[...],
                                    cq_ref[...], sq_ref[...], ck_ref[win, :], sk_ref[win, :], r_ref[...],
                                    n * ATT_BLOCK, k0)

    return pl.pallas_call(
        body, name=name, grid=(L // ATT_BLOCK,),
        in_specs=[qspec, kvspec, kvspec, sspec, tq, tq, tk, tk, rspec],
        out_specs=qspec,
        out_shape=jax.ShapeDtypeStruct((ATT_HEADS, L, HEAD_DIM), F32),
        compiler_params=_cp(("parallel",)),
    )(qh, kh, vh, sinks.reshape(ATT_HEADS, 1, 1), cosf, sinf, cosf, sinf, rot)


def _attn_bwd(do, qh, kh, vh, sinks, cosf, sinf, rot, *, name):
    L = qh.shape[1]
    nwin, qspec, kvspec, sspec, tq, tk, rspec = _attn_common(L)

    def body(do_ref, q_ref, k_ref, v_ref, s_ref, cq_ref, sq_ref, ck_ref, sk_ref, r_ref,
             dq_ref, dk_ref, dv_ref, ds_ref):
        n = pl.program_id(0)
        k0 = pl.multiple_of(jnp.maximum(n - 1, 0) * ATT_BLOCK, ATT_BLOCK)
        win = pl.ds(k0, nwin)
        cq, sq, ck, sk, rt = cq_ref[...], sq_ref[...], ck_ref[win, :], sk_ref[win, :], r_ref[...]
        q0 = n * ATT_BLOCK
        _, vjp = jax.vjp(lambda q, kw, vw, s: _attn_block_fn(q, kw, vw, s, cq, sq, ck, sk, rt, q0, k0),
                         q_ref[...], k_ref[:, win, :], v_ref[:, win, :], s_ref[...])
        dq, dkw, dvw, ds = vjp(do_ref[...])

        @pl.when(n == 0)
        def _():
            dk_ref[...] = jnp.zeros_like(dk_ref)
            dv_ref[...] = jnp.zeros_like(dv_ref)
            ds_ref[...] = jnp.zeros_like(ds_ref)

        dq_ref[...] = dq
        dk_ref[:, win, :] += dkw
        dv_ref[:, win, :] += dvw
        ds_ref[...] += ds

    return pl.pallas_call(
        body, name=name, grid=(L // ATT_BLOCK,),
        in_specs=[qspec, qspec, kvspec, kvspec, sspec, tq, tq, tk, tk, rspec],
        out_specs=[qspec, kvspec, kvspec, sspec],
        out_shape=[jax.ShapeDtypeStruct((ATT_HEADS, L, HEAD_DIM), F32),
                   jax.ShapeDtypeStruct((ATT_KV_HEADS, L, HEAD_DIM), F32),
                   jax.ShapeDtypeStruct((ATT_KV_HEADS, L, HEAD_DIM), F32),
                   jax.ShapeDtypeStruct((ATT_HEADS, 1, 1), F32)],
        compiler_params=_cp(("arbitrary",)),
    )(do, qh, kh, vh, sinks.reshape(ATT_HEADS, 1, 1), cosf, sinf, cosf, sinf, rot)


def _to_heads(t, nh):
    L = t.shape[0]
    return t.reshape(L, nh, HEAD_DIM).transpose(1, 0, 2)


def _from_heads(t):
    nh, L, _ = t.shape
    return t.transpose(1, 0, 2).reshape(L, nh * HEAD_DIM)


def _branch_fwd(ya, yb, o2d, zc, gates, wa, wb, wc, *, name, tm=256):
    L = ya.shape[0]
    tm = min(tm, L)
    W, D = SSM_WIDTH, D_MODEL

    def body(ya_ref, yb_ref, o_ref, zc_ref, g0_ref, g1_ref, g2_ref, wa_ref, wb_ref, wc_ref,
             mg_ref, ta_ref, tb_ref, tc_ref, yc_ref):
        yc = (o_ref[...] * _silu(zc_ref[...])).astype(MXU)
        ta = _dot(ya_ref[...], wa_ref[...], "nt")
        tb = _dot(yb_ref[...], wb_ref[...], "nt")
        tc = _dot(yc, wc_ref[...], "nt")
        ta_ref[...] = ta
        tb_ref[...] = tb
        tc_ref[...] = tc
        yc_ref[...] = yc
        mg_ref[...] = (jax.nn.sigmoid(g0_ref[...]) * ta + jax.nn.sigmoid(g1_ref[...]) * tb
                       + jax.nn.sigmoid(g2_ref[...]) * tc).astype(MXU)

    row = pl.BlockSpec((tm, W), lambda i: (i, 0))
    wide = pl.BlockSpec((tm, D), lambda i: (i, 0))
    gate = lambda c: pl.BlockSpec((tm, D), lambda i, c=c: (i, c))
    wspec = pl.BlockSpec((D, W), lambda i: (0, 0))
    return pl.pallas_call(
        body, name=name, grid=(L // tm,),
        in_specs=[row, row, row, row, gate(0), gate(1), gate(2), wspec, wspec, wspec],
        out_specs=[wide, wide, wide, wide, row],
        out_shape=[jax.ShapeDtypeStruct((L, D), MXU), jax.ShapeDtypeStruct((L, D), F32),
                   jax.ShapeDtypeStruct((L, D), F32), jax.ShapeDtypeStruct((L, D), F32),
                   jax.ShapeDtypeStruct((L, W), MXU)],
        compiler_params=_cp(("parallel",), 56),
    )(ya, yb, o2d, zc, gates, gates, gates, wa, wb, wc)


def _branch_bwd(dmg, ta, tb, tc, gates, *, name, tm=256):
    L = dmg.shape[0]
    tm = min(tm, L)
    D = D_MODEL

    def body(dm_ref, ta_ref, tb_ref, tc_ref, g0_ref, g1_ref, g2_ref, da_ref, db_ref, dc_ref, dg_ref):
        dm = dm_ref[...]
        for i, (t_ref, g_ref, d_ref) in enumerate(((ta_ref, g0_ref, da_ref), (tb_ref, g1_ref, db_ref),
                                                   (tc_ref, g2_ref, dc_ref))):
            sg = jax.nn.sigmoid(g_ref[...])
            d_ref[...] = (sg * dm).astype(MXU)
            dg_ref[:, i * D:(i + 1) * D] = (dm * t_ref[...] * sg * (1.0 - sg)).astype(MXU)

    wide = pl.BlockSpec((tm, D), lambda i: (i, 0))
    gate = lambda c: pl.BlockSpec((tm, D), lambda i, c=c: (i, c))
    bf = jax.ShapeDtypeStruct((L, D), MXU)
    return pl.pallas_call(
        body, name=name, grid=(L // tm,),
        in_specs=[wide, wide, wide, wide, gate(0), gate(1), gate(2)],
        out_specs=[wide, wide, wide, pl.BlockSpec((tm, 3 * D), lambda i: (i, 0))],
        out_shape=[bf, bf, bf, jax.ShapeDtypeStruct((L, 3 * D), MXU)],
        compiler_params=_cp(("parallel",), 56),
    )(dmg, ta, tb, tc, gates, gates, gates)


def _gate_c_bwd(dyc, o2d, zc, *, name, tm=256):
    L, W = dyc.shape
    tm = min(tm, L)

    def body(dy_ref, o_ref, z_ref, do_ref, dz_ref):
        _, vjp = jax.vjp(lambda o, z: o * _silu(z), o_ref[...], z_ref[...])
        do, dz = vjp(dy_ref[...])
        do_ref[...] = do
        dz_ref[...] = dz.astype(MXU)

    row = pl.BlockSpec((tm, W), lambda i: (i, 0))
    return pl.pallas_call(body, name=name, grid=(L // tm,), in_specs=[row, row, row], out_specs=[row, row],
                          out_shape=[jax.ShapeDtypeStruct((L, W), F32), jax.ShapeDtypeStruct((L, W), MXU)],
                          compiler_params=_cp(("parallel",)))(dyc, o2d, zc)


def _adamw(w, g, m, v, *, name):
    shape = w.shape
    cols = shape[-1]
    w2, g2, m2, v2 = (t.reshape(-1, cols) for t in (w, g, m, v))
    rows = w2.shape[0]
    tc = 1024 if cols % 1024 == 0 else cols
    lane_cols = -(-tc // 128) * 128
    tr = rows
    while tr % 16 == 0 and tr * lane_cols * 4 > 2 * _MB:
        tr //= 2

    def body(w_ref, g_ref, m_ref, v_ref, d_ref, nm_ref, nv_ref):
        gv = g_ref[...]
        nm = ADAM_B1 * m_ref[...] + (1.0 - ADAM_B1) * gv
        nv = ADAM_B2 * v_ref[...] + (1.0 - ADAM_B2) * jnp.square(gv)
        m_hat = nm / (1.0 - ADAM_B1 ** ADAM_STEP)
        v_hat = nv / (1.0 - ADAM_B2 ** ADAM_STEP)
        d_ref[...] = -ADAM_LR * (m_hat / (jnp.sqrt(v_hat) + ADAM_EPS) + ADAM_WD * w_ref[...])
        nm_ref[...] = nm
        nv_ref[...] = nv

    spec = pl.BlockSpec((tr, tc), lambda i, j: (i, j))
    out = jax.ShapeDtypeStruct((rows, cols), F32)
    d, nm, nv = pl.pallas_call(body, name=name, grid=(rows // tr, cols // tc), in_specs=[spec] * 4,
                               out_specs=[spec] * 3, out_shape=[out, out, out],
                               compiler_params=_cp(("parallel", "parallel")))(w2, g2, m2, v2)
    return d.reshape(shape), nm.reshape(shape), nv.reshape(shape)


def _adamw_layer(w, g, m, v, l, prev, *, name):
    _, rows, cols = w.shape
    tc = 1024 if cols % 1024 == 0 else cols
    tr = rows
    while tr % 16 == 0 and tr * tc * 4 > 2 * _MB:
        tr //= 2

    def body(w_ref, g_ref, m_ref, v_ref, *rest):
        go_ref, d_ref, nm_ref, nv_ref = rest[-4:]
        gv = g_ref[...]
        nm = ADAM_B1 * m_ref[...] + (1.0 - ADAM_B1) * gv
        nv = ADAM_B2 * v_ref[...] + (1.0 - ADAM_B2) * jnp.square(gv)
        m_hat = nm / (1.0 - ADAM_B1 ** ADAM_STEP)
        v_hat = nv / (1.0 - ADAM_B2 ** ADAM_STEP)
        d_ref[...] = -ADAM_LR * (m_hat / (jnp.sqrt(v_hat) + ADAM_EPS) + ADAM_WD * w_ref[...])
        nm_ref[...] = nm
        nv_ref[...] = nv
        go_ref[...] = gv

    lspec = pl.BlockSpec((None, tr, tc), lambda i, j: (l, i, j))
    gspec = pl.BlockSpec((tr, tc), lambda i, j: (i, j))
    out = jax.ShapeDtypeStruct(w.shape, F32)
    extra = [] if prev is None else list(prev)
    return pl.pallas_call(
        body, name=name, grid=(rows // tr, cols // tc),
        in_specs=[lspec, gspec, lspec, lspec] + [_ANY] * len(extra),
        out_specs=[lspec] * 4, out_shape=[out] * 4,
        input_output_aliases={4 + i: i for i in range(len(extra))},
        compiler_params=_cp(("parallel", "parallel")),
    )(w, g, m, v, *extra)


def _prep_layer(p, l):
    are = p["ssm_a_re"][l].reshape(SSM_CH, 1)
    aim = p["ssm_a_im"][l].reshape(SSM_CH, 1)
    ldt = p["ssm_log_dt"][l].reshape(1, SSM_GROUPS)
    bre = p["ssm_b_re"][l].reshape(SSM_CH, SSM_GROUP)
    bim = p["ssm_b_im"][l].reshape(SSM_CH, SSM_GROUP)
    lbr, lbi, bbr, bbi = _s5_params_fwd(are, aim, ldt, bre, bim, name=f"s5_params_fwd_{l}")
    cre = p["ssm_c_re"][l].transpose(0, 2, 1).reshape(SSM_CH, SSM_GROUP)
    cim = p["ssm_c_im"][l].transpose(0, 2, 1).reshape(SSM_CH, SSM_GROUP)
    return dict(raw=(are, aim, ldt, bre, bim),
                lbr=lbr.reshape(N_SLAB * SLAB_NC, 1, 128), lbi=lbi.reshape(N_SLAB * SLAB_NC, 1, 128),
                btr=_expand_bd(bbr), bti=_expand_bd(bbi), cbr=_expand_bd(cre), cbi=_expand_bd(cim),
                dvec=p["ssm_d"][l].reshape(1, SSM_WIDTH))


def _layer_fwd(x, h, p, sp, winT, rest_of, l, tabs, proj_after=None):
    L = x.shape[0]
    cosf, sinf, rot = tabs
    mm = functools.partial(_matmul, h, winT, "nt", tm=L, tn=256, tk=D_MODEL)
    main = mm(name=f"proj_main_{l}", shape=(L, N_MAIN, D_MODEL), after=proj_after)
    zc = mm(name=f"proj_zc_{l}", shape=(L, N_ZC, D_MODEL), b_off=(N_MAIN // 256, 0))
    gates = mm(name=f"proj_gates_{l}", shape=(L, N_GATES, D_MODEL), b_off=((N_MAIN + N_ZC) // 256, 0))
    big, token = rest_of(gates)
    ua = _time_interleave(main[:, :SSM_WIDTH])
    if token is not None:
        ua = ua + token[0, 0]
    ys, sr, si = _s5_fwd(ua, sp["btr"], sp["bti"], sp["cbr"], sp["cbi"], sp["lbr"], sp["lbi"], sp["dvec"],
                         name=f"s5_fwd_{l}")
    ys = _time_deinterleave(ys)
    ya = _glu_fwd(ys, main, big["glu_w"], p["ssm_glu_b"][l], name=f"glu_fwd_{l}")
    yb = _sg_fwd(main, p["sg_ln_w"][l], p["sg_ln_b"][l], p["sg_w"][l], p["sg_b"][l], name=f"sg_fwd_{l}")
    qh = _to_heads(main[:, 5120:6144], ATT_HEADS)
    kh = _to_heads(main[:, 6144:6272], ATT_KV_HEADS)
    vh = _to_heads(main[:, 6272:6400], ATT_KV_HEADS)
    oh = _attn_fwd(qh, kh, vh, p["attn_sinks"][l], cosf, sinf, rot, name=f"attn_fwd_{l}")
    o2d = _from_heads(oh)
    mg, ta, tb, tc, yc = _branch_fwd(ya, yb, o2d, zc, gates, big["wbaT"], big["wbbT"], big["wbcT"],
                                     name=f"branch_fwd_{l}")
    xn = _matmul(mg, big["w_out"], "nn", name=f"out_fwd_{l}", shape=(L, D_MODEL, D_MODEL), tm=512, tn=512,
                 tk=D_MODEL, add=x)
    saved = dict(x=x, h=h, main=main, zc=zc, gates=gates, ua=ua, ys=ys, sr=sr, si=si, ya=ya, yb=yb, yc=yc, o2d=o2d,
                 qh=qh, kh=kh, vh=vh, mg=mg, ta=ta, tb=tb, tc=tc, sp=sp)
    return xn, saved, big


def _layer_bwd(dxn, s, p, big, l, tabs):
    L = dxn.shape[0]
    D, W = D_MODEL, SSM_WIDTH
    cosf, sinf, rot = tabs
    sp = s["sp"]
    g = {}
    dmg = _matmul(dxn, big["w_out"], "nt", name=f"out_bwd_dm_{l}", shape=(L, D, D), tm=512, tn=512, tk=D)
    g["w_out"] = _matmul(s["mg"], dxn, "tn", name=f"out_bwd_dw_{l}", shape=(D, D, L), tm=512, tn=512, tk=L,
                         out_dtype=MXU)
    dta, dtb, dtc, dgates = _branch_bwd(dmg, s["ta"], s["tb"], s["tc"], s["gates"], name=f"branch_bwd_{l}")
    dys_ = {}
    for nm, dt, y, wt in (("a", dta, s["ya"], big["wbaT"]), ("b", dtb, s["yb"], big["wbbT"]),
                          ("c", dtc, s["yc"], big["wbcT"])):
        dys_[nm] = _matmul(dt, wt, "nn", name=f"branch_bwd_dy{nm}_{l}", shape=(L, W, D), tm=512, tn=512, tk=D)
        g["wb" + nm + "T"] = _matmul(dt, y, "tn", name=f"branch_bwd_dw{nm}_{l}", shape=(D, W, L),
                                     tm=512, tn=512, tk=L, out_dtype=MXU)
    do2d, dzc = _gate_c_bwd(dys_["c"], s["o2d"], s["zc"], name=f"gate_c_bwd_{l}")
    dqh, dkh, dvh, dsinks = _attn_bwd(_to_heads(do2d, ATT_HEADS), s["qh"], s["kh"], s["vh"], p["attn_sinks"][l],
                                      cosf, sinf, rot, name=f"attn_bwd_{l}")
    g["attn_sinks"] = dsinks.reshape(ATT_HEADS)
    dub, dvb, dzb, dlnw, dlnb, dsgw, dsgb = _sg_bwd(dys_["b"], s["main"], p["sg_ln_w"][l], p["sg_ln_b"][l],
                                                    p["sg_w"][l], p["sg_b"][l], name=f"sg_bwd_{l}")
    g["sg_ln_w"], g["sg_ln_b"] = dlnw.reshape(W), dlnb.reshape(W)
    g["sg_w"], g["sg_b"] = dsgw, dsgb.reshape(SG_HEADS, SG_CHUNK)
    dys, dza, a1, dzl, dgb = _glu_bwd(dys_["a"], s["ys"], s["main"], big["glu_w"], p["ssm_glu_b"][l],
                                      name=f"glu_bwd_{l}")
    g["ssm_glu_b"] = dgb.reshape(W)
    g["glu_w"] = _matmul(a1, dzl, "tn", name=f"glu_bwd_dw_{l}", shape=(W, W, L), tm=512, tn=512, tk=L, out_dtype=MXU)
    dua, dbtr, dbti, dcbr, dcbi, dlr, dli, dd = _s5_bwd(_time_interleave(dys), s["ua"], s["sr"], s["si"], sp["btr"],
                                                        sp["bti"], sp["cbr"], sp["cbi"], sp["lbr"], sp["lbi"],
                                                        sp["dvec"], name=f"s5_bwd_{l}")
    dua = _time_deinterleave(dua)
    g["ssm_d"] = dd.reshape(W)
    to_c = lambda t: _contract_bd(t).reshape(SSM_GROUPS, SSM_STATE, SSM_GROUP).transpose(0, 2, 1)
    g["ssm_c_re"], g["ssm_c_im"] = to_c(dcbr), to_c(dcbi)
    dare, daim, dldt, dbre, dbim = _s5_params_bwd(*sp["raw"], dlr.reshape(SSM_CH, 1), dli.reshape(SSM_CH, 1),
                                                  _contract_bd(dbtr), _contract_bd(dbti),
                                                  name=f"s5_params_bwd_{l}")
    g["ssm_a_re"] = dare.reshape(SSM_GROUPS, SSM_STATE)
    g["ssm_a_im"] = daim.reshape(SSM_GROUPS, SSM_STATE)
    g["ssm_log_dt"] = dldt.reshape(SSM_GROUPS)
    g["ssm_b_re"] = dbre.reshape(SSM_GROUPS, SSM_STATE, SSM_GROUP)
    g["ssm_b_im"] = dbim.reshape(SSM_GROUPS, SSM_STATE, SSM_GROUP)
    dproj = jnp.concatenate([t.astype(MXU) for t in (dua, dza, dub, dvb, dzb, _from_heads(dqh), _from_heads(dkh),
                                                     _from_heads(dvh), dzc, dgates)], axis=1)
    g["winT"] = _matmul(dproj, s["h"], "tn", name=f"proj_bwd_dw_{l}", shape=(D_IN, D, L), tm=256, tn=D, tk=L,
                        out_dtype=MXU)
    return dproj, g


def _layer_bwd_input(dproj, dxn, s, p, big, l, after):
    L = dxn.shape[0]
    dh = _matmul(dproj, big["winT"], "nn", name=f"proj_bwd_dh_{l}", shape=(L, D_MODEL, D_IN), tm=512, tn=512,
                 tk=D_IN // 2, after=after)
    dx, dnw = _rms_bwd(s["x"], p["norm_w"][l], dh, dxn, name=f"rms_bwd_{l}")
    return dx, dnw


MESH = pl.DeviceIdType.MESH
_ANY = pl.BlockSpec(memory_space=pl.ANY)
ROW_ALIGN = 16


def _coords():
    return lax.axis_index("x"), lax.axis_index("y"), lax.axis_index("c")


def _gather8(arrs, *, name):
    n = len(arrs)
    rows = [a.shape[0] for a in arrs]
    for r in rows:
        assert r % ROW_ALIGN == 0

    def body(*refs):
        ins, outs = refs[:n], refs[n:2 * n]
        send, recv, lsem = refs[2 * n:]
        x, y, c = _coords()
        me, sibling = (x, y, c), (x, y, 1 - c)
        chips = [(1 - x, y), (x, 1 - y), (1 - x, 1 - y)]

        def blk(a, px, py, pc):
            return outs[a].at[pl.ds(pl.multiple_of((4 * px + 2 * py + pc) * rows[a], ROW_ALIGN), rows[a]), :]

        def own(a):
            return ins[a]

        def copy(a, k, block, to, src=None):
            return pltpu.make_async_remote_copy(
                src_ref=blk(a, *block) if src is None else src, dst_ref=blk(a, *block),
                send_sem=send.at[a, k], recv_sem=recv.at[a, k], device_id=to, device_id_type=MESH)

        mine, first, passed = [], [], []
        for a in range(n):
            mine.append(pltpu.make_async_copy(own(a), blk(a, *me), lsem.at[a]))
            mine[a].start()
            f = [copy(a, 0, me, sibling, src=own(a))]
            f += [copy(a, 1 + j, me, (*chip, c), src=own(a)) for j, chip in enumerate(chips)]
            for cp in f:
                cp.start()
            first.append(f)
        for a in range(n):
            ps = [copy(a, 4 + j, (*chip, c), sibling) for j, chip in enumerate(chips)]
            for j, chip in enumerate(chips):
                copy(a, 1 + j, (*chip, c), me).wait_recv()
                ps[j].start()
            passed.append(ps)
        for a in range(n):
            copy(a, 0, sibling, me).wait_recv()
            for j, chip in enumerate(chips):
                copy(a, 4 + j, (*chip, 1 - c), me).wait_recv()
            for cp in first[a] + passed[a]:
                cp.wait_send()
            mine[a].wait()

    return pl.pallas_call(
        body, name=name,
        in_specs=[_ANY] * n, out_specs=[_ANY] * n,
        out_shape=[jax.ShapeDtypeStruct((8 * r,) + a.shape[1:], a.dtype) for r, a in zip(rows, arrs)],
        scratch_shapes=[pltpu.SemaphoreType.DMA((n, 7)), pltpu.SemaphoreType.DMA((n, 7)), pltpu.SemaphoreType.DMA((n,))],
    )(*arrs)


def _sibling_swap(arrs, *, pick_other, name):
    n = len(arrs)

    def body(*refs):
        ins, outs = refs[:n], refs[n:2 * n]
        send, recv = refs[2 * n:]
        x, y, c = _coords()
        sel = (1 - c) if pick_other else c
        cps = [pltpu.make_async_remote_copy(src_ref=ins[a].at[:, sel], dst_ref=outs[a], send_sem=send.at[a],
                                            recv_sem=recv.at[a], device_id=(x, y, 1 - c), device_id_type=MESH)
               for a in range(n)]
        for cp in cps:
            cp.start()
        for cp in cps:
            cp.wait_recv()
        for cp in cps:
            cp.wait_send()

    return pl.pallas_call(
        body, name=name, in_specs=[_ANY] * n, out_specs=[_ANY] * n,
        out_shape=[jax.ShapeDtypeStruct((a.shape[0],) + a.shape[2:], a.dtype) for a in arrs],
        scratch_shapes=[pltpu.SemaphoreType.DMA((n,)), pltpu.SemaphoreType.DMA((n,))],
    )(*arrs)


def _col_tile(lead, rows, cols, itemsize=4, cap=4 * _MB):
    tc = cols
    while tc % 256 == 0 and lead * rows * tc * itemsize > cap:
        tc //= 2
    return tc


def _pair_sum(mine, theirs, *, name):
    _, _, rows, cols = mine.shape
    tc = _col_tile(1, rows, cols)
    c = lax.axis_index("c")

    def body(c_ref, a_ref, b_ref, o_ref):
        o_ref[...] = (a_ref[...].astype(F32) + b_ref[...].astype(F32)).astype(MXU)

    return pl.pallas_call(
        body, name=name,
        grid_spec=pltpu.PrefetchScalarGridSpec(
            num_scalar_prefetch=1, grid=(4, cols // tc),
            in_specs=[pl.BlockSpec((None, None, rows, tc), lambda j, i, cr: (j, cr[0], 0, i)),
                      pl.BlockSpec((None, rows, tc), lambda j, i, cr: (j, 0, i))],
            out_specs=pl.BlockSpec((None, rows, tc), lambda j, i, cr: (j, 0, i))),
        out_shape=jax.ShapeDtypeStruct((4, rows, cols), MXU),
        compiler_params=_cp(("parallel", "parallel")),
    )(c.reshape(1).astype(jnp.int32), mine, theirs)


_HBM = pl.BlockSpec(memory_space=pltpu.HBM)
_SEM = pl.BlockSpec(memory_space=pltpu.SEMAPHORE)
_EFFECT = pltpu.SideEffectType.DATAFLOW_SIDE_EFFECTING
N_PEER_CHIPS = 3


def _peer_chips(x, y):
    return [(1 - x, y), (x, 1 - y), (1 - x, 1 - y)]


def _split_start(srcs, lands, src_slot, dst_slot, *, name):
    n = len(srcs)
    ns = n * N_PEER_CHIPS

    def body(*refs):
        src_refs, land_refs = refs[:n], refs[n:2 * n]
        send, recv, token = refs[2 * n:2 * n + ns], refs[2 * n + ns:2 * n + 2 * ns], refs[-1]
        x, y, c = _coords()
        for a in range(n):
            for k, (px, py) in enumerate(_peer_chips(x, y)):
                pltpu.make_async_remote_copy(
                    src_ref=src_refs[a].at[src_slot(x, y, c, px, py)], dst_ref=land_refs[a].at[dst_slot(x, y, c)],
                    send_sem=send[a * N_PEER_CHIPS + k], recv_sem=recv[a * N_PEER_CHIPS + k],
                    device_id=(px, py, c), device_id_type=MESH).start()
        token[...] = jnp.zeros_like(token)

    bufs = list(srcs) + list(lands)
    res = pl.pallas_call(
        body, name=name,
        out_shape=(*[pltpu.SemaphoreType.DMA(())] * (2 * ns), *[pltpu.HBM(b.shape, b.dtype) for b in bufs],
                   jax.ShapeDtypeStruct((8, 128), F32)),
        in_specs=[_HBM] * (2 * n),
        out_specs=(*[_SEM] * (2 * ns), *[_HBM] * (2 * n), pl.BlockSpec(memory_space=pltpu.VMEM)),
        input_output_aliases={i: 2 * ns + i for i in range(2 * n)},
        compiler_params=pltpu.CompilerParams(has_side_effects=_EFFECT),
    )(*[pltpu.with_memory_space_constraint(b, pltpu.HBM) for b in bufs])
    sems = list(res[:2 * ns])
    return sems, list(res[2 * ns:2 * ns + n]), list(res[2 * ns + n:2 * ns + 2 * n]), res[-1]


def _split_wait(sems, srcs, lands, after, *, name):
    n = len(srcs)
    ns = n * N_PEER_CHIPS

    def body(*refs):
        src_refs, land_refs = refs[:n], refs[n:2 * n]
        send, recv = refs[2 * n:2 * n + ns], refs[2 * n + ns:2 * n + 2 * ns]
        x, y, c = _coords()
        for a in range(n):
            for k in range(N_PEER_CHIPS):
                cp = pltpu.make_async_remote_copy(
                    src_ref=src_refs[a].at[0], dst_ref=land_refs[a].at[0], send_sem=send[a * N_PEER_CHIPS + k],
                    recv_sem=recv[a * N_PEER_CHIPS + k], device_id=(x, y, 1 - c), device_id_type=MESH)
                cp.wait_send()
                cp.wait_recv()

    bufs = list(srcs) + list(lands)
    res = pl.pallas_call(
        body, name=name,
        out_shape=tuple(pltpu.HBM(b.shape, b.dtype) for b in bufs),
        in_specs=[_HBM] * (2 * n) + [_SEM] * (2 * ns) + [_ANY] * len(after),
        out_specs=tuple([_HBM] * (2 * n)),
        input_output_aliases={i: i for i in range(2 * n)},
        compiler_params=pltpu.CompilerParams(has_side_effects=_EFFECT),
    )(*bufs, *sems, *after)
    return list(res[:n]), list(res[n:])


def _fill_own(shard2, *, name):
    _, rows, cols = shard2.shape
    tc = _col_tile(1, rows, cols, itemsize=shard2.dtype.itemsize)
    j = 2 * lax.axis_index("x") + lax.axis_index("y")

    def body(j_ref, s_ref, o_ref):
        o_ref[...] = s_ref[...]

    return pl.pallas_call(
        body, name=name,
        grid_spec=pltpu.PrefetchScalarGridSpec(
            num_scalar_prefetch=1, grid=(2, cols // tc),
            in_specs=[pl.BlockSpec((None, rows, tc), lambda h, i, jr: (h, 0, i))],
            out_specs=pl.BlockSpec((None, rows, tc), lambda h, i, jr: (2 * jr[0] + h, 0, i))),
        out_shape=jax.ShapeDtypeStruct((8, rows, cols), shard2.dtype),
        compiler_params=_cp(("parallel", "parallel")),
    )(j.reshape(1).astype(jnp.int32), shard2)


def _pass_to_sibling(lands, *, name):
    n = len(lands)

    def body(*refs):
        outs = refs[n:2 * n]
        send, recv = refs[2 * n:]
        x, y, c = _coords()
        cps = []
        for a in range(n):
            for k, (px, py) in enumerate(_peer_chips(x, y)):
                slot = 4 * px + 2 * py + c
                cps.append(pltpu.make_async_remote_copy(
                    src_ref=outs[a].at[slot], dst_ref=outs[a].at[slot], send_sem=send.at[a, k], recv_sem=recv.at[a, k],
                    device_id=(x, y, 1 - c), device_id_type=MESH))
        for cp in cps:
            cp.start()
        for cp in cps:
            cp.wait_recv()
        for cp in cps:
            cp.wait_send()

    return pl.pallas_call(
        body, name=name, in_specs=[_ANY] * n, out_specs=[_ANY] * n,
        out_shape=[jax.ShapeDtypeStruct(b.shape, b.dtype) for b in lands],
        input_output_aliases={a: a for a in range(n)},
        scratch_shapes=[pltpu.SemaphoreType.DMA((n, N_PEER_CHIPS)), pltpu.SemaphoreType.DMA((n, N_PEER_CHIPS))],
    )(*lands)


def _sum_parts(parts, got, *, name):
    _, rows, cols = parts.shape
    tc = _col_tile(4, rows, cols, itemsize=parts.dtype.itemsize)
    x, y, c = _coords()
    idx = jnp.stack([2 * x + y, 2 * (1 - x) + y, 2 * x + (1 - y), 2 * (1 - x) + (1 - y), c]).astype(jnp.int32)

    def body(i_ref, p_ref, g0_ref, g1_ref, g2_ref, o_ref):
        o_ref[...] = ((p_ref[...].astype(F32) + g0_ref[...].astype(F32)) + g1_ref[...].astype(F32)) + g2_ref[...].astype(F32)

    slot = lambda s: pl.BlockSpec((None, rows, tc), lambda i, ir, s=s: (ir[s], 0, i))
    return pl.pallas_call(
        body, name=name,
        grid_spec=pltpu.PrefetchScalarGridSpec(
            num_scalar_prefetch=1, grid=(cols // tc,),
            in_specs=[slot(0), slot(1), slot(2), slot(3)],
            out_specs=pl.BlockSpec((None, rows, tc), lambda i, ir: (ir[4], 0, i))),
        out_shape=jax.ShapeDtypeStruct((2, rows, cols), F32),
        compiler_params=_cp(("parallel",)),
    )(idx, parts, got, got, got)


def _sum_slots(t, *, name):
    S, rows, cols = t.shape
    tc = _col_tile(S, rows, cols)

    def body(t_ref, o_ref):
        acc = t_ref[0].astype(F32)
        for s in range(1, S):
            acc = acc + t_ref[s].astype(F32)
        o_ref[...] = acc

    return pl.pallas_call(
        body, name=name, grid=(cols // tc,),
        in_specs=[pl.BlockSpec((S, rows, tc), lambda i: (0, 0, i))],
        out_specs=pl.BlockSpec((rows, tc), lambda i: (0, i)),
        out_shape=jax.ShapeDtypeStruct((rows, cols), F32),
        compiler_params=_cp(("parallel",)),
    )(t)


def _halves_join(bufs, *, name):
    n = len(bufs)

    def body(*refs):
        outs = refs[n:2 * n]
        send, recv = refs[2 * n:]
        x, y, c = _coords()
        cps = [pltpu.make_async_remote_copy(src_ref=outs[a].at[c], dst_ref=outs[a].at[c], send_sem=send.at[a],
                                            recv_sem=recv.at[a], device_id=(x, y, 1 - c), device_id_type=MESH)
               for a in range(n)]
        for cp in cps:
            cp.start()
        for cp in cps:
            cp.wait_recv()
        for cp in cps:
            cp.wait_send()

    return pl.pallas_call(
        body, name=name, in_specs=[_ANY] * n, out_specs=[_ANY] * n,
        out_shape=[jax.ShapeDtypeStruct(b.shape, b.dtype) for b in bufs],
        input_output_aliases={a: a for a in range(n)},
        scratch_shapes=[pltpu.SemaphoreType.DMA((n,)), pltpu.SemaphoreType.DMA((n,))],
    )(*bufs)


def _reduce_scatter_begin(grads, *, tag):
    views = [g.reshape(4, 2, g.shape[0] // 8, g.shape[1]) for g in grads]
    theirs = _sibling_swap(views, pick_other=True, name=f"rs_swap_{tag}")
    parts = [_pair_sum(v, t, name=f"rs_pair_{tag}_{i}") for i, (v, t) in enumerate(zip(views, theirs))]
    got = [lax.empty(p.shape, p.dtype) for p in parts]
    sems, parts, got, token = _split_start(
        parts, got, lambda x, y, c, px, py: 2 * px + py, lambda x, y, c: 2 * x + y, name=f"rs_start_{tag}")
    return (sems, parts, got), token


def _reduce_scatter_end(state, after, *, tag):
    sems, parts, got = state
    parts, got = _split_wait(sems, parts, got, after, name=f"rs_wait_{tag}")
    halves = [_sum_parts(p, t, name=f"rs_sum_{tag}_{i}") for i, (p, t) in enumerate(zip(parts, got))]
    joined = _halves_join(halves, name=f"rs_join_{tag}")
    return [j.reshape(2 * j.shape[1], j.shape[2]) for j in joined]


_SMALL = ("norm_w", "ssm_a_re", "ssm_a_im", "ssm_log_dt", "ssm_b_re", "ssm_b_im", "ssm_c_re", "ssm_c_im", "ssm_d",
          "ssm_glu_b", "sg_ln_w", "sg_ln_b", "sg_w", "sg_b", "attn_sinks", "final_norm_w")
_BIG = ("w_in", "ssm_glu_w", "w_branch_a", "w_branch_b", "w_branch_c", "w_out")
_WEIGHTS = ("norm_w", "w_in", "ssm_a_re", "ssm_a_im", "ssm_log_dt", "ssm_b_re", "ssm_b_im", "ssm_c_re", "ssm_c_im",
            "ssm_d", "ssm_glu_w", "ssm_glu_b", "sg_ln_w", "sg_ln_b", "sg_w", "sg_b", "attn_sinks", "w_branch_a",
            "w_branch_b", "w_branch_c", "w_out", "final_norm_w")
_PACK_COLS = 1024
_PACK_ALIGN = 16 * _PACK_COLS


def _pack(ts):
    flat = jnp.concatenate([t.reshape(-1) for t in ts])
    pad = (-flat.shape[0]) % _PACK_ALIGN
    return jnp.pad(flat, (0, pad)).reshape(-1, _PACK_COLS)


def _unpack(buf, like):
    flat = buf.reshape(-1)
    out, pos = [], 0
    for t in like:
        out.append(flat[pos:pos + t.size].reshape(t.shape))
        pos += t.size
    return out


def kernel(x, norm_w, w_in, ssm_a_re, ssm_a_im, ssm_log_dt, ssm_b_re, ssm_b_im, ssm_c_re, ssm_c_im, ssm_d, ssm_glu_w, ssm_glu_b, sg_ln_w, sg_ln_b, sg_w, sg_b, attn_sinks, w_branch_a, w_branch_b, w_branch_c, w_out, final_norm_w, loss_target, m_norm_w, m_w_in, m_ssm_a_re, m_ssm_a_im, m_ssm_log_dt, m_ssm_b_re, m_ssm_b_im, m_ssm_c_re, m_ssm_c_im, m_ssm_d, m_ssm_glu_w, m_ssm_glu_b, m_sg_ln_w, m_sg_ln_b, m_sg_w, m_sg_b, m_attn_sinks, m_w_branch_a, m_w_branch_b, m_w_branch_c, m_w_out, m_final_norm_w, v_norm_w, v_w_in, v_ssm_a_re, v_ssm_a_im, v_ssm_log_dt, v_ssm_b_re, v_ssm_b_im, v_ssm_c_re, v_ssm_c_im, v_ssm_d, v_ssm_glu_w, v_ssm_glu_b, v_sg_ln_w, v_sg_ln_b, v_sg_w, v_sg_b, v_attn_sinks, v_w_branch_a, v_w_branch_b, v_w_branch_c, v_w_out, v_final_norm_w):
    w = dict(norm_w=norm_w, w_in=w_in, ssm_a_re=ssm_a_re, ssm_a_im=ssm_a_im, ssm_log_dt=ssm_log_dt, ssm_b_re=ssm_b_re,
             ssm_b_im=ssm_b_im, ssm_c_re=ssm_c_re, ssm_c_im=ssm_c_im, ssm_d=ssm_d, ssm_glu_w=ssm_glu_w,
             ssm_glu_b=ssm_glu_b, sg_ln_w=sg_ln_w, sg_ln_b=sg_ln_b, sg_w=sg_w, sg_b=sg_b, attn_sinks=attn_sinks,
             w_branch_a=w_branch_a, w_branch_b=w_branch_b, w_branch_c=w_branch_c, w_out=w_out,
             final_norm_w=final_norm_w)
    m = dict(norm_w=m_norm_w, w_in=m_w_in, ssm_a_re=m_ssm_a_re, ssm_a_im=m_ssm_a_im, ssm_log_dt=m_ssm_log_dt,
             ssm_b_re=m_ssm_b_re, ssm_b_im=m_ssm_b_im, ssm_c_re=m_ssm_c_re, ssm_c_im=m_ssm_c_im, ssm_d=m_ssm_d,
             ssm_glu_w=m_ssm_glu_w, ssm_glu_b=m_ssm_glu_b, sg_ln_w=m_sg_ln_w, sg_ln_b=m_sg_ln_b, sg_w=m_sg_w,
             sg_b=m_sg_b, attn_sinks=m_attn_sinks, w_branch_a=m_w_branch_a, w_branch_b=m_w_branch_b,
             w_branch_c=m_w_branch_c, w_out=m_w_out, final_norm_w=m_final_norm_w)
    v = dict(norm_w=v_norm_w, w_in=v_w_in, ssm_a_re=v_ssm_a_re, ssm_a_im=v_ssm_a_im, ssm_log_dt=v_ssm_log_dt,
             ssm_b_re=v_ssm_b_re, ssm_b_im=v_ssm_b_im, ssm_c_re=v_ssm_c_re, ssm_c_im=v_ssm_c_im, ssm_d=v_ssm_d,
             ssm_glu_w=v_ssm_glu_w, ssm_glu_b=v_ssm_glu_b, sg_ln_w=v_sg_ln_w, sg_ln_b=v_sg_ln_b, sg_w=v_sg_w,
             sg_b=v_sg_b, attn_sinks=v_attn_sinks, w_branch_a=v_w_branch_a, w_branch_b=v_w_branch_b,
             w_branch_c=v_w_branch_c, w_out=v_w_out, final_norm_w=v_final_norm_w)

    big_names = ("winT", "glu_w", "wbaT", "wbbT", "wbcT", "w_out")
    L = x.shape[1]
    tabs = _rope_tables(L)
    p = {k: w[k] for k in _SMALL}

    halves = [[t.reshape(2, t.shape[0] // 2, t.shape[1]) for t in
               (w["w_in"][l].T.astype(MXU), w["ssm_glu_w"][l].astype(MXU), w["w_branch_a"][l].T.astype(MXU),
                w["w_branch_b"][l].T.astype(MXU), w["w_branch_c"][l].T.astype(MXU), w["w_out"][l].astype(MXU))]
              for l in range(DEPTH)]
    my_half = lambda x_, y_, c_, px, py: c_
    my_block = lambda x_, y_, c_: 4 * x_ + 2 * y_ + c_
    rows_of = lambda lands: [t.reshape(8 * t.shape[1], t.shape[2]) for t in lands]
    lands = [[_fill_own(s, name=f"gather_fill_{l}_{i}") for i, s in enumerate(halves[l])] for l in range(DEPTH)]
    sp = [_prep_layer(p, l) for l in range(DEPTH)]
    saved = [None] * DEPTH

    sems_a, src_a, land_a, _ = _split_start(halves[0][:1], lands[0][:1], my_half, my_block, name="gather_start_0a")
    h0 = _rms_fwd(x[0], p["norm_w"][0], name="rms_fwd_0")
    _, land_a = _split_wait(sems_a, src_a, land_a, [h0, sp[0]["btr"], sp[1]["btr"]] + lands[1], name="gather_wait_0a")
    land_a = _pass_to_sibling(land_a, name="gather_pass_0a")
    land_a, src_b = lax.optimization_barrier((land_a, halves[0][1:]))
    sems_b, src_b, land_b, token_b = _split_start(src_b, lands[0][1:], my_half, my_block, name="gather_start_0b")
    split1 = {}

    def rest0(t):
        _, got = _split_wait(sems_b, src_b, land_b, [t], name="gather_wait_0b")
        got = _pass_to_sibling(got, name="gather_pass_0b")
        got, src1 = lax.optimization_barrier((got, halves[1]))
        split1["sems"], split1["src"], split1["land"], token1 = _split_start(src1, lands[1], my_half, my_block,
                                                                             name="gather_start_1")
        return dict(zip(big_names, rows_of(land_a + got))), token1

    x1, saved[0], big0 = _layer_fwd(x[0], h0, p, sp[0], rows_of(land_a)[0], rest0, 0, tabs, proj_after=token_b)
    _, lands1 = _split_wait(split1["sems"], split1["src"], split1["land"], [x1], name="gather_wait_1")
    big1 = dict(zip(big_names, rows_of(_pass_to_sibling(lands1, name="gather_pass_1"))))
    bigs = [big0, big1]
    h1 = _rms_fwd(x1, p["norm_w"][1], name="rms_fwd_1")
    x2, saved[1], _ = _layer_fwd(x1, h1, p, sp[1], big1["winT"], lambda t: (big1, None), 1, tabs)
    loss, dx, dfw = _final_loss(x2, p["final_norm_w"], loss_target[0], name="final_loss")
    loss = lax.psum(loss, ("x", "y", "c"))

    grads = [None] * DEPTH
    big_of = lambda g: [g[k] for k in big_names]
    dproj, grads[1] = _layer_bwd(dx, saved[1], p, bigs[1], 1, tabs)
    rs1, token = _reduce_scatter_begin(big_of(grads[1]), tag="1")
    dx, grads[1]["norm_w"] = _layer_bwd_input(dproj, dx, saved[1], p, bigs[1], 1, token)
    dproj, grads[0] = _layer_bwd(dx, saved[0], p, bigs[0], 0, tabs)
    red1 = _reduce_scatter_end(rs1, [dproj], tag="1")
    rs0, token = _reduce_scatter_begin(big_of(grads[0]), tag="0")
    dx, grads[0]["norm_w"] = _layer_bwd_input(dproj, dx, saved[0], p, bigs[0], 0, token)

    tr = lambda t: t.transpose(0, 2, 1)
    view = {k: (tr if k == "w_in" else (lambda t: t)) for k in _BIG}
    shard_grads = lambda red: dict(zip(_BIG, (red[0], red[1], red[2].T, red[3].T, red[4].T, red[5])))
    outs = {k: None for k in _BIG}

    def adamw_big(l, red):
        for k, g in shard_grads(red).items():
            outs[k] = _adamw_layer(view[k](w[k]), g, view[k](m[k]), view[k](v[k]), l, outs[k], name=f"adamw_{k}_{l}")

    adamw_big(1, red1)

    small_like = [w[k] for k in _SMALL]
    gs = [jnp.stack([grads[l][k] for l in range(DEPTH)]) if k != "final_norm_w" else dfw for k in _SMALL]
    packed = _pack(gs)
    allg = _gather8([packed], name="gather_small_grads")[0]
    gsum = _sum_slots(allg.reshape(8, packed.shape[0], _PACK_COLS), name="sum_small_grads")
    adamw_big(0, _reduce_scatter_end(rs0, [gsum] + [outs[k][0] for k in _BIG], tag="0"))

    gfull, delta, new_m, new_v = {}, {}, {}, {}
    for k in _BIG:
        gfull[k], delta[k], new_m[k], new_v[k] = (view[k](t) for t in outs[k])
    for k, t in zip(_SMALL, _unpack(gsum, small_like)):
        gfull[k] = t
        delta[k], new_m[k], new_v[k] = _adamw(w[k], t, m[k], v[k], name=f"adamw_{k}")

    return (loss, dx[None], *[gfull[k] for k in _WEIGHTS], *[delta[k] for k in _WEIGHTS],
            *[new_m[k] for k in _WEIGHTS], *[new_v[k] for k in _WEIGHTS])
```

```python
import functools
import math

import numpy as np
import jax
import jax.numpy as jnp
from jax import lax
from jax.experimental import pallas as pl
from jax.experimental.pallas import tpu as pltpu

F32 = jnp.float32
MXU = jnp.bfloat16
HIGHEST = lax.Precision.HIGHEST

D_MODEL = 2048
DEPTH = 2
EPS = 1e-6
NEG_INF = -1e30
SSM_WIDTH = 1024
SSM_GROUP = 16
SSM_GROUPS = 64
SSM_STATE = 64
SSM_CH = SSM_GROUPS * SSM_STATE
SLAB = 128
SLAB_CH = (SLAB // SSM_GROUP) * SSM_STATE
N_SLAB = SSM_WIDTH // SLAB
SCAN_SEG = 8
SCAN_STEPS = 4
SG_HEADS = 8
SG_CHUNK = 128
HEAD_DIM = 64
ATT_HEADS = 16
ATT_KV_HEADS = 2
GQA_GROUP = 8
ATT_BLOCK = 128
WINDOW = 128
ROT_DIM = 16
ROPE_THETA = 500000.0
N_MAIN = 6400
N_ZC = 1024
N_GATES = 6144
D_IN = N_MAIN + N_ZC + N_GATES

ADAM_LR = 0.001
ADAM_B1 = 0.9
ADAM_B2 = 0.999
ADAM_EPS = 1e-08
ADAM_WD = 0.01
ADAM_STEP = 10

_DIMS = {"nn": (((1,), (0,)), ((), ())), "nt": (((1,), (1,)), ((), ())), "tn": (((0,), (0,)), ((), ()))}
_MB = 1024 * 1024


def _cp(sem, vmem_mb=48):
    return pltpu.CompilerParams(dimension_semantics=sem, vmem_limit_bytes=vmem_mb * _MB)


def _dot(a, b, mode):
    return lax.dot_general(a.astype(MXU), b.astype(MXU), _DIMS[mode], preferred_element_type=F32)


@jax.custom_vjp
def _mm_nn(a, b):
    return _dot(a, b, "nn")


def _mm_nn_fwd(a, b):
    return _dot(a, b, "nn"), (a, b)


def _mm_nn_bwd(res, g):
    a, b = res
    return _dot(g, b, "nt"), _dot(a, g, "tn")


_mm_nn.defvjp(_mm_nn_fwd, _mm_nn_bwd)


@jax.custom_vjp
def _mm_nt(a, bt):
    return _dot(a, bt, "nt")


def _mm_nt_fwd(a, bt):
    return _dot(a, bt, "nt"), (a, bt)


def _mm_nt_bwd(res, g):
    a, bt = res
    return _dot(g, bt, "nn"), _dot(g, a, "tn")


_mm_nt.defvjp(_mm_nt_fwd, _mm_nt_bwd)


def _rmsnorm(x, w):
    return x * lax.rsqrt(jnp.mean(x * x, axis=-1, keepdims=True) + EPS) * w


def _layernorm(x, w, b):
    mu = jnp.mean(x, axis=-1, keepdims=True)
    var = jnp.mean(jnp.square(x - mu), axis=-1, keepdims=True)
    return (x - mu) * lax.rsqrt(var + EPS) * w + b


def _silu(x):
    return x * jax.nn.sigmoid(x)


def _matmul(a, b, mode, *, name, shape, tm, tn, tk, out_dtype=F32, add=None, a_off=(0, 0), b_off=(0, 0), after=None):
    m, n, k = shape
    tm, tn, tk = min(tm, m), min(tn, n), min(tk, k)
    assert m % tm == 0 and n % tn == 0 and k % tk == 0, (name, shape, tm, tn, tk)
    nk = k // tk
    has_add, has_after = add is not None, after is not None

    def body(*refs):
        a_ref, b_ref = refs[0], refs[1]
        pos = 2
        add_ref = None
        if has_add:
            add_ref = refs[pos]
            pos += 1
        if has_after:
            pos += 1
        o_ref = refs[pos]
        p = _dot(a_ref[...], b_ref[...], mode)
        if nk == 1:
            if has_add:
                p = p + add_ref[...].astype(F32)
            o_ref[...] = p.astype(out_dtype)
            return
        acc_ref = refs[pos + 1]
        kk = pl.program_id(2)

        @pl.when(kk == 0)
        def _():
            acc_ref[...] = p

        @pl.when(kk > 0)
        def _():
            acc_ref[...] += p

        @pl.when(kk == nk - 1)
        def _():
            r = acc_ref[...]
            if has_add:
                r = r + add_ref[...].astype(F32)
            o_ref[...] = r.astype(out_dtype)

    a0, a1 = a_off
    b0, b1 = b_off
    if mode == "tn":
        a_spec = pl.BlockSpec((tk, tm), lambda i, j, kk: (kk + a0, i + a1))
    else:
        a_spec = pl.BlockSpec((tm, tk), lambda i, j, kk: (i + a0, kk + a1))
    if mode == "nt":
        b_spec = pl.BlockSpec((tn, tk), lambda i, j, kk: (j + b0, kk + b1))
    else:
        b_spec = pl.BlockSpec((tk, tn), lambda i, j, kk: (kk + b0, j + b1))
    in_specs = [a_spec, b_spec]
    args = [a, b]
    if has_add:
        in_specs.append(pl.BlockSpec((tm, tn), lambda i, j, kk: (i, j)))
        args.append(add)
    if has_after:
        in_specs.append(pl.BlockSpec(memory_space=pl.ANY))
        args.append(after)
    return pl.pallas_call(
        body, name=name, grid=(m // tm, n // tn, nk),
        in_specs=in_specs,
        out_specs=pl.BlockSpec((tm, tn), lambda i, j, kk: (i, j)),
        out_shape=jax.ShapeDtypeStruct((m, n), out_dtype),
        scratch_shapes=[pltpu.VMEM((tm, tn), F32)] if nk > 1 else [],
        compiler_params=_cp(("parallel", "parallel", "arbitrary")),
    )(*args)


def _rms_fwd(x, w, *, name, tm=256):
    L, d = x.shape
    tm = min(tm, L)

    def body(x_ref, w_ref, h_ref):
        h_ref[...] = _rmsnorm(x_ref[...], w_ref[...]).astype(MXU)

    return pl.pallas_call(
        body, name=name, grid=(L // tm,),
        in_specs=[pl.BlockSpec((tm, d), lambda i: (i, 0)), pl.BlockSpec((1, d), lambda i: (0, 0))],
        out_specs=pl.BlockSpec((tm, d), lambda i: (i, 0)),
        out_shape=jax.ShapeDtypeStruct((L, d), MXU),
        compiler_params=_cp(("parallel",)),
    )(x, w.reshape(1, d))


def _rms_bwd(x, w, dh, dxn, *, name, tm=256):
    L, d = x.shape
    tm = min(tm, L)

    def body(x_ref, w_ref, dh_ref, dxn_ref, dx_ref, dw_ref):
        _, vjp = jax.vjp(_rmsnorm, x_ref[...], w_ref[...])
        dx, dw = vjp(dh_ref[...])
        dx_ref[...] = dx + dxn_ref[...]

        @pl.when(pl.program_id(0) == 0)
        def _():
            dw_ref[...] = jnp.zeros_like(dw_ref)

        dw_ref[...] += dw

    row = pl.BlockSpec((tm, d), lambda i: (i, 0))
    vec = pl.BlockSpec((1, d), lambda i: (0, 0))
    dx, dw = pl.pallas_call(
        body, name=name, grid=(L // tm,),
        in_specs=[row, vec, row, row], out_specs=[row, vec],
        out_shape=[jax.ShapeDtypeStruct((L, d), F32), jax.ShapeDtypeStruct((1, d), F32)],
        compiler_params=_cp(("arbitrary",)),
    )(x, w.reshape(1, d), dh, dxn)
    return dx, dw.reshape(d)


def _final_loss(x, w, tgt, *, name, tm=256):
    L, d = x.shape
    tm = min(tm, L)

    def loss_fn(xv, wv, tv):
        err = jnp.square(_rmsnorm(xv, wv) - tv)
        return 0.5 * jnp.sum(jnp.mean(err, axis=-1, keepdims=True), axis=0, keepdims=True)

    def body(x_ref, w_ref, t_ref, loss_ref, dx_ref, dw_ref):
        tv = t_ref[...]
        val, vjp = jax.vjp(lambda xv, wv: loss_fn(xv, wv, tv), x_ref[...], w_ref[...])
        dx, dw = vjp(jnp.ones((1, 1), F32))
        dx_ref[...] = dx

        @pl.when(pl.program_id(0) == 0)
        def _():
            dw_ref[...] = jnp.zeros_like(dw_ref)
            loss_ref[...] = jnp.zeros_like(loss_ref)

        dw_ref[...] += dw
        loss_ref[...] += jnp.broadcast_to(val, loss_ref.shape)

    row = pl.BlockSpec((tm, d), lambda i: (i, 0))
    vec = pl.BlockSpec((1, d), lambda i: (0, 0))
    loss, dx, dw = pl.pallas_call(
        body, name=name, grid=(L // tm,),
        in_specs=[row, vec, row],
        out_specs=[pl.BlockSpec((8, 128), lambda i: (0, 0)), row, vec],
        out_shape=[jax.ShapeDtypeStruct((8, 128), F32), jax.ShapeDtypeStruct((L, d), F32),
                   jax.ShapeDtypeStruct((1, d), F32)],
        compiler_params=_cp(("arbitrary",)),
    )(x, w.reshape(1, d), tgt)
    return loss[0, 0], dx, dw.reshape(d)


PARAM_ROWS = 512


def _s5_param_fn(are, aim, ldt, bre, bim, row0):
    n = are.shape[0]
    grp = (row0 + lax.broadcasted_iota(jnp.int32, (n, SSM_GROUPS), 0)) // SSM_STATE
    col = lax.broadcasted_iota(jnp.int32, (n, SSM_GROUPS), 1)
    sel = (grp == col).astype(F32)
    dt = jnp.sum(sel * jnp.exp(ldt), axis=-1, keepdims=True)
    mag = jnp.exp(are * dt)
    ang = aim * dt
    lbr = mag * jnp.cos(ang)
    lbi = mag * jnp.sin(ang)
    den = are * are + aim * aim
    nr = lbr - 1.0
    kr = (nr * are + lbi * aim) / den
    ki = (lbi * are - nr * aim) / den
    return lbr, lbi, kr * bre - ki * bim, kr * bim + ki * bre


def _s5_param_specs():
    col = pl.BlockSpec((PARAM_ROWS, 1), lambda i: (i, 0))
    mat = pl.BlockSpec((PARAM_ROWS, SSM_GROUP), lambda i: (i, 0))
    vec = pl.BlockSpec((1, SSM_GROUPS), lambda i: (0, 0))
    return col, mat, vec


def _s5_params_fwd(are, aim, ldt, bre, bim, *, name):
    n = are.shape[0]
    col, mat, vec = _s5_param_specs()

    def body(are_ref, aim_ref, ldt_ref, bre_ref, bim_ref, lbr_ref, lbi_ref, bbr_ref, bbi_ref):
        row0 = pl.program_id(0) * PARAM_ROWS
        lbr, lbi, bbr, bbi = _s5_param_fn(are_ref[...], aim_ref[...], ldt_ref[...], bre_ref[...], bim_ref[...], row0)
        lbr_ref[...] = lbr
        lbi_ref[...] = lbi
        bbr_ref[...] = bbr
        bbi_ref[...] = bbi

    cshape = jax.ShapeDtypeStruct((n, 1), F32)
    mshape = jax.ShapeDtypeStruct((n, SSM_GROUP), F32)
    return pl.pallas_call(body, name=name, grid=(n // PARAM_ROWS,),
                          in_specs=[col, col, vec, mat, mat], out_specs=[col, col, mat, mat],
                          out_shape=[cshape, cshape, mshape, mshape],
                          compiler_params=_cp(("parallel",)))(are, aim, ldt, bre, bim)


def _s5_params_bwd(are, aim, ldt, bre, bim, dlbr, dlbi, dbbr, dbbi, *, name):
    n = are.shape[0]
    col, mat, vec = _s5_param_specs()

    def body(are_ref, aim_ref, ldt_ref, bre_ref, bim_ref, g0, g1, g2, g3, o0, o1, o2, o3, o4):
        row0 = pl.program_id(0) * PARAM_ROWS
        _, vjp = jax.vjp(lambda a, b, c, d, e: _s5_param_fn(a, b, c, d, e, row0),
                         are_ref[...], aim_ref[...], ldt_ref[...], bre_ref[...], bim_ref[...])
        dare, daim, dldt, dbre, dbim = vjp((g0[...], g1[...], g2[...], g3[...]))
        o0[...] = dare
        o1[...] = daim
        o3[...] = dbre
        o4[...] = dbim

        @pl.when(pl.program_id(0) == 0)
        def _():
            o2[...] = jnp.zeros_like(o2)

        o2[...] += dldt

    cshape = jax.ShapeDtypeStruct((n, 1), F32)
    mshape = jax.ShapeDtypeStruct((n, SSM_GROUP), F32)
    return pl.pallas_call(body, name=name, grid=(n // PARAM_ROWS,),
                          in_specs=[col, col, vec, mat, mat, col, col, mat, mat],
                          out_specs=[col, col, vec, mat, mat],
                          out_shape=[cshape, cshape, jax.ShapeDtypeStruct((1, SSM_GROUPS), F32), mshape, mshape],
                          compiler_params=_cp(("arbitrary",)))(are, aim, ldt, bre, bim, dlbr, dlbi, dbbr, dbbi)


SLAB_NC = SLAB_CH // 128


def _s5_specs(L):
    slab = pl.BlockSpec((L, SLAB), lambda s: (0, s))
    wspec = pl.BlockSpec((SLAB_NC, 128, SLAB), lambda s: (s, 0, 0))
    lspec = pl.BlockSpec((SLAB_NC, 1, 128), lambda s: (s, 0, 0))
    sspec = pl.BlockSpec((SLAB_NC, L, 128), lambda s: (s, 0, 0))
    dspec = pl.BlockSpec((1, SLAB), lambda s: (0, s))
    return slab, wspec, lspec, sspec, dspec


def _scan_inplace(sr_ref, si_ref, lr, li, pr_ref, pi_ref, *, reverse):
    NC, L, W = sr_ref.shape
    S = SCAN_SEG
    T = L // S
    lr8 = [jnp.broadcast_to(lr[k], (S, W)) for k in range(NC)]
    li8 = [jnp.broadcast_to(li[k], (S, W)) for k in range(NC)]

    def tiles(first, count):
        return pl.ds(first * S, count * S)

    for k in range(NC):
        pr_ref[k, tiles(T - 1 if reverse else 0, 1), :] = lr8[k]
        pi_ref[k, tiles(T - 1 if reverse else 0, 1), :] = li8[k]
        n = 1
        while n < T:
            have = tiles(T - n, n) if reverse else tiles(0, n)
            new = tiles(T - 2 * n, n) if reverse else tiles(n, n)
            top = tiles(T - n, 1) if reverse else tiles(n - 1, 1)
            ar, ai = pr_ref[k, top, :][None], pi_ref[k, top, :][None]
            hr, hi = pr_ref[k, have, :].reshape(n, S, W), pi_ref[k, have, :].reshape(n, S, W)
            pr_ref[k, new, :] = (hr * ar - hi * ai).reshape(n * S, W)
            pi_ref[k, new, :] = (hr * ai + hi * ar).reshape(n * S, W)
            n *= 2

    def step(i, carry):
        for u in range(SCAN_STEPS):
            jj = i * SCAN_STEPS + u
            rows = pl.ds(pl.multiple_of(((T - 1 - jj) if reverse else jj) * S, S), S)
            out = []
            for k in range(NC):
                sr, si = carry[k]
                nsr = lr8[k] * sr - li8[k] * si + sr_ref[k, rows, :]
                nsi = lr8[k] * si + li8[k] * sr + si_ref[k, rows, :]
                sr_ref[k, rows, :] = nsr
                si_ref[k, rows, :] = nsi
                out.append((nsr, nsi))
            carry = tuple(out)
        return carry

    zero = jnp.zeros((S, W), F32)
    ends = lax.fori_loop(0, T // SCAN_STEPS, step, tuple((zero, zero) for k in range(NC)))
    sub = lax.broadcasted_iota(jnp.int32, (S, W), 0)
    order = range(S - 1, -1, -1) if reverse else range(S)
    for k in range(NC):
        er, ei = ends[k]
        full = tiles(0 if reverse else T - 1, 1)
        ltr = pr_ref[k, full, :][0:1]
        lti = pi_ref[k, full, :][0:1]
        cr = jnp.zeros((1, W), F32)
        ci = jnp.zeros((1, W), F32)
        ctr = jnp.zeros((S, W), F32)
        cti = jnp.zeros((S, W), F32)
        for seg in order:
            ctr = jnp.where(sub == seg, cr, ctr)
            cti = jnp.where(sub == seg, ci, cti)
            cr, ci = (er[seg:seg + 1, :] + ltr * cr - lti * ci, ei[seg:seg + 1, :] + ltr * ci + lti * cr)
        pr = pr_ref[k].reshape(T, S, W)
        pi = pi_ref[k].reshape(T, S, W)
        sr_ref[k] += (pr * ctr[None] - pi * cti[None]).reshape(L, W)
        si_ref[k] += (pr * cti[None] + pi * ctr[None]).reshape(L, W)


def _time_interleave(a):
    L, W = a.shape
    return a.reshape(SCAN_SEG, L // SCAN_SEG, W).transpose(1, 0, 2).reshape(L, W)


def _time_deinterleave(a):
    L, W = a.shape
    return a.reshape(L // SCAN_SEG, SCAN_SEG, W).transpose(1, 0, 2).reshape(L, W)


def _s5_fwd(u, btr, bti, cbr, cbi, lbr, lbi, dvec, *, name):
    L = u.shape[0]

    def body(u_ref, btr_ref, bti_ref, cbr_ref, cbi_ref, lr_ref, li_ref, d_ref, ys_ref, sr_ref, si_ref, pr_ref, pi_ref):
        u = u_ref[...]
        for k in range(SLAB_NC):
            sr_ref[k] = _dot(u, btr_ref[k], "nt")
            si_ref[k] = _dot(u, bti_ref[k], "nt")
        _scan_inplace(sr_ref, si_ref, lr_ref[...], li_ref[...], pr_ref, pi_ref, reverse=False)
        ys = d_ref[...] * u
        for k in range(SLAB_NC):
            ys = ys + _dot(sr_ref[k], cbr_ref[k], "nn") - _dot(si_ref[k], cbi_ref[k], "nn")
        ys_ref[...] = ys

    slab, wspec, lspec, sspec, dspec = _s5_specs(L)
    sshape = jax.ShapeDtypeStruct((N_SLAB * SLAB_NC, L, 128), F32)
    return pl.pallas_call(
        body, name=name, grid=(N_SLAB,),
        in_specs=[slab, wspec, wspec, wspec, wspec, lspec, lspec, dspec],
        out_specs=[slab, sspec, sspec],
        out_shape=[jax.ShapeDtypeStruct((L, SSM_WIDTH), F32), sshape, sshape],
        scratch_shapes=[pltpu.VMEM((SLAB_NC, L, 128), F32), pltpu.VMEM((SLAB_NC, L, 128), F32)],
        compiler_params=_cp(("parallel",), 56),
    )(u, btr, bti, cbr, cbi, lbr, lbi, dvec)


def _s5_bwd(dys, u, sr, si, btr, bti, cbr, cbi, lbr, lbi, dvec, *, name):
    L = u.shape[0]
    S = SCAN_SEG

    def body(dys_ref, u_ref, sr_ref, si_ref, btr_ref, bti_ref, cbr_ref, cbi_ref, lr_ref, li_ref, d_ref,
             du_ref, dbtr_ref, dbti_ref, dcbr_ref, dcbi_ref, dlr_ref, dli_ref, dd_ref,
             ar_ref, ai_ref, pr_ref, pi_ref):
        dys = dys_ref[...]
        u = u_ref[...]
        for k in range(SLAB_NC):
            ar_ref[k] = _dot(dys, cbr_ref[k], "nt")
            ai_ref[k] = -_dot(dys, cbi_ref[k], "nt")
        _scan_inplace(ar_ref, ai_ref, lr_ref[...], -li_ref[...], pr_ref, pi_ref, reverse=True)
        head = lax.broadcasted_iota(jnp.int32, (L, 1), 0) < S
        sub0 = lax.broadcasted_iota(jnp.int32, (S, 1), 0) == 0

        def prev_state(s):
            up = pltpu.roll(s, S, 0)
            return jnp.where(head, 0.0, up), jnp.where(sub0, 0.0, pltpu.roll(up[0:S], 1, 0))

        du = d_ref[...] * dys
        for k in range(SLAB_NC):
            a_re = ar_ref[k]
            a_im = ai_ref[k]
            du = du + _dot(a_re, btr_ref[k], "nn") + _dot(a_im, bti_ref[k], "nn")
            dbtr_ref[k] = _dot(a_re, u, "tn")
            dbti_ref[k] = _dot(a_im, u, "tn")
            s_re = sr_ref[k]
            s_im = si_ref[k]
            dcbr_ref[k] = _dot(s_re, dys, "tn")
            dcbi_ref[k] = -_dot(s_im, dys, "tn")
            p_re, q_re = prev_state(s_re)
            p_im, q_im = prev_state(s_im)
            b_re, b_im = a_re[0:S], a_im[0:S]
            dlr_ref[k] = (jnp.sum(p_re * a_re + p_im * a_im, axis=0, keepdims=True)
                          + jnp.sum(q_re * b_re + q_im * b_im, axis=0, keepdims=True))
            dli_ref[k] = (jnp.sum(p_re * a_im - p_im * a_re, axis=0, keepdims=True)
                          + jnp.sum(q_re * b_im - q_im * b_re, axis=0, keepdims=True))
        du_ref[...] = du
        dd_ref[...] = jnp.sum(dys * u, axis=0, keepdims=True)

    slab, wspec, lspec, sspec, dspec = _s5_specs(L)
    wshape = jax.ShapeDtypeStruct((N_SLAB * SLAB_NC, 128, SLAB), F32)
    lshape = jax.ShapeDtypeStruct((N_SLAB * SLAB_NC, 1, 128), F32)
    return pl.pallas_call(
        body, name=name, grid=(N_SLAB,),
        in_specs=[slab, slab, sspec, sspec, wspec, wspec, wspec, wspec, lspec, lspec, dspec],
        out_specs=[slab, wspec, wspec, wspec, wspec, lspec, lspec, dspec],
        out_shape=[jax.ShapeDtypeStruct((L, SSM_WIDTH), F32), wshape, wshape, wshape, wshape, lshape, lshape,
                   jax.ShapeDtypeStruct((1, SSM_WIDTH), F32)],
        scratch_shapes=[pltpu.VMEM((SLAB_NC, L, 128), F32)] * 4,
        compiler_params=_cp(("parallel",), 56),
    )(dys, u, sr, si, btr, bti, cbr, cbi, lbr, lbi, dvec)


_SLAB_MASK = (np.arange(SLAB_CH)[:, None] // SSM_STATE == np.arange(SLAB)[None, :] // SSM_GROUP)


def _expand_bd(x):
    t = jnp.tile(x.reshape(N_SLAB, SLAB_CH, SSM_GROUP), (1, 1, SLAB // SSM_GROUP))
    return jnp.where(_SLAB_MASK[None], t, 0.0).astype(MXU).reshape(N_SLAB * SLAB_NC, 128, SLAB)


def _contract_bd(dx):
    t = jnp.where(_SLAB_MASK[None], dx.reshape(N_SLAB, SLAB_CH, SLAB), 0.0)
    return jnp.sum(t.reshape(N_SLAB, SLAB_CH, SLAB // SSM_GROUP, SSM_GROUP), axis=2).reshape(SSM_CH, SSM_GROUP)


def _glu_ew(ys, zlin, za):
    a1 = jax.nn.gelu(ys)
    return a1 * jax.nn.sigmoid(zlin) * _silu(za)


def _glu_fwd(ys, main, gw, gb, *, name, tm=256):
    L = ys.shape[0]
    tm = min(tm, L)
    W = SSM_WIDTH

    def body(ys_ref, za_ref, gw_ref, gb_ref, ya_ref):
        ys = ys_ref[...]
        a1 = jax.nn.gelu(ys)
        zlin = _dot(a1, gw_ref[...], "nn") + gb_ref[...]
        ya_ref[...] = _glu_ew(ys, zlin, za_ref[...]).astype(MXU)

    return pl.pallas_call(
        body, name=name, grid=(L // tm,),
        in_specs=[pl.BlockSpec((tm, W), lambda i: (i, 0)), pl.BlockSpec((tm, W), lambda i: (i, 1)),
                  pl.BlockSpec((W, W), lambda i: (0, 0)), pl.BlockSpec((1, W), lambda i: (0, 0))],
        out_specs=pl.BlockSpec((tm, W), lambda i: (i, 0)),
        out_shape=jax.ShapeDtypeStruct((L, W), MXU),
        compiler_params=_cp(("parallel",)),
    )(ys, main, gw, gb.reshape(1, W))


def _glu_bwd(dya, ys, main, gw, gb, *, name, tm=256):
    L = ys.shape[0]
    tm = min(tm, L)
    W = SSM_WIDTH

    def body(dya_ref, ys_ref, za_ref, gw_ref, gb_ref, dys_ref, dza_ref, a1_ref, dzl_ref, db_ref):
        ys = ys_ref[...]
        a1, gelu_vjp = jax.vjp(jax.nn.gelu, ys)
        zlin = _dot(a1, gw_ref[...], "nn") + gb_ref[...]
        _, vjp = jax.vjp(lambda a, z, za: a * jax.nn.sigmoid(z) * _silu(za), a1, zlin, za_ref[...])
        da1, dzlin, dza = vjp(dya_ref[...].astype(F32))
        da1 = da1 + _dot(dzlin, gw_ref[...], "nt")
        dys_ref[...] = gelu_vjp(da1)[0]
        dza_ref[...] = dza
        a1_ref[...] = a1.astype(MXU)
        dzl_ref[...] = dzlin.astype(MXU)

        @pl.when(pl.program_id(0) == 0)
        def _():
            db_ref[...] = jnp.zeros_like(db_ref)

        db_ref[...] += jnp.sum(dzlin, axis=0, keepdims=True)

    row = pl.BlockSpec((tm, W), lambda i: (i, 0))
    vec = pl.BlockSpec((1, W), lambda i: (0, 0))
    return pl.pallas_call(
        body, name=name, grid=(L // tm,),
        in_specs=[row, row, pl.BlockSpec((tm, W), lambda i: (i, 1)), pl.BlockSpec((W, W), lambda i: (0, 0)), vec],
        out_specs=[row, row, row, row, vec],
        out_shape=[jax.ShapeDtypeStruct((L, W), F32), jax.ShapeDtypeStruct((L, W), F32),
                   jax.ShapeDtypeStruct((L, W), MXU), jax.ShapeDtypeStruct((L, W), MXU),
                   jax.ShapeDtypeStruct((1, W), F32)],
        compiler_params=_cp(("arbitrary",)),
    )(dya, ys, main, gw, gb.reshape(1, W))


def _sg_fn(ub, vb, zb, lnw, lnb, ws, bs):
    u = jax.nn.gelu(ub)
    v = _layernorm(jax.nn.gelu(vb), lnw, lnb)
    r = lax.broadcasted_iota(jnp.int32, (SG_CHUNK, SG_CHUNK), 0)
    c = lax.broadcasted_iota(jnp.int32, (SG_CHUNK, SG_CHUNK), 1)
    tri = r >= c
    outs = []
    for h in range(SG_HEADS):
        wh = jnp.where(tri, ws[h], 0.0)
        outs.append(_mm_nn(wh, v[:, h * 128:(h + 1) * 128]) + bs[h])
    mixed = jnp.concatenate(outs, axis=1)
    return u * mixed * _silu(zb)


def _sg_specs(L):
    W = SSM_WIDTH
    blk = lambda c: pl.BlockSpec((SG_CHUNK, W), lambda i, c=c: (i, c))
    vec = pl.BlockSpec((1, W), lambda i: (0, 0))
    wspec = pl.BlockSpec((SG_HEADS, SG_CHUNK, SG_CHUNK), lambda i: (0, 0, 0))
    bspec = pl.BlockSpec((SG_HEADS, SG_CHUNK, 1), lambda i: (0, 0, 0))
    return blk, vec, wspec, bspec


def _sg_fwd(main, lnw, lnb, sgw, sgb, *, name):
    L = main.shape[0]
    W = SSM_WIDTH
    blk, vec, wspec, bspec = _sg_specs(L)

    def body(ub_ref, vb_ref, zb_ref, lnw_ref, lnb_ref, w_ref, b_ref, yb_ref):
        ws = [w_ref[h] for h in range(SG_HEADS)]
        bs = [b_ref[h] for h in range(SG_HEADS)]
        yb_ref[...] = _sg_fn(ub_ref[...], vb_ref[...], zb_ref[...], lnw_ref[...], lnb_ref[...], ws, bs).astype(MXU)

    return pl.pallas_call(
        body, name=name, grid=(L // SG_CHUNK,),
        in_specs=[blk(2), blk(3), blk(4), vec, vec, wspec, bspec],
        out_specs=pl.BlockSpec((SG_CHUNK, W), lambda i: (i, 0)),
        out_shape=jax.ShapeDtypeStruct((L, W), MXU),
        compiler_params=_cp(("parallel",)),
    )(main, main, main, lnw.reshape(1, W), lnb.reshape(1, W), sgw, sgb.reshape(SG_HEADS, SG_CHUNK, 1))


def _sg_bwd(dyb, main, lnw, lnb, sgw, sgb, *, name):
    L = main.shape[0]
    W = SSM_WIDTH
    blk, vec, wspec, bspec = _sg_specs(L)

    def body(dyb_ref, ub_ref, vb_ref, zb_ref, lnw_ref, lnb_ref, w_ref, b_ref,
             dub_ref, dvb_ref, dzb_ref, dlnw_ref, dlnb_ref, dw_ref, db_ref):
        ws = [w_ref[h] for h in range(SG_HEADS)]
        bs = [b_ref[h] for h in range(SG_HEADS)]
        _, vjp = jax.vjp(_sg_fn, ub_ref[...], vb_ref[...], zb_ref[...], lnw_ref[...], lnb_ref[...], ws, bs)
        dub, dvb, dzb, dlnw, dlnb, dws, dbs = vjp(dyb_ref[...])

        @pl.when(pl.program_id(0) == 0)
        def _():
            dlnw_ref[...] = jnp.zeros_like(dlnw_ref)
            dlnb_ref[...] = jnp.zeros_like(dlnb_ref)
            dw_ref[...] = jnp.zeros_like(dw_ref)
            db_ref[...] = jnp.zeros_like(db_ref)

        dub_ref[...] = dub
        dvb_ref[...] = dvb
        dzb_ref[...] = dzb
        dlnw_ref[...] += dlnw
        dlnb_ref[...] += dlnb
        for h in range(SG_HEADS):
            dw_ref[h] += dws[h]
            db_ref[h] += dbs[h]

    row = pl.BlockSpec((SG_CHUNK, W), lambda i: (i, 0))
    out = jax.ShapeDtypeStruct((L, W), F32)
    return pl.pallas_call(
        body, name=name, grid=(L // SG_CHUNK,),
        in_specs=[row, blk(2), blk(3), blk(4), vec, vec, wspec, bspec],
        out_specs=[row, row, row, vec, vec, wspec, bspec],
        out_shape=[out, out, out, jax.ShapeDtypeStruct((1, W), F32), jax.ShapeDtypeStruct((1, W), F32),
                   jax.ShapeDtypeStruct((SG_HEADS, SG_CHUNK, SG_CHUNK), F32),
                   jax.ShapeDtypeStruct((SG_HEADS, SG_CHUNK, 1), F32)],
        compiler_params=_cp(("arbitrary",)),
    )(dyb, main, main, main, lnw.reshape(1, W), lnb.reshape(1, W), sgw, sgb.reshape(SG_HEADS, SG_CHUNK, 1))


def _rope_tables(L):
    half = ROT_DIM // 2
    inv_freq = ROPE_THETA ** (-jnp.arange(0, ROT_DIM, 2, dtype=F32) / ROT_DIM)
    ang = jnp.arange(L, dtype=F32)[:, None] * inv_freq[None, :]
    cos = jnp.cos(ang)
    sin = jnp.sin(ang)
    ones = jnp.ones((L, HEAD_DIM - ROT_DIM), F32)
    cosf = jnp.concatenate([cos, cos, ones], axis=1)
    sinf = jnp.concatenate([sin, sin, 0.0 * ones], axis=1)
    rot = np.zeros((HEAD_DIM, HEAD_DIM), np.float32)
    for d in range(half):
        rot[d + half, d] = -1.0
        rot[d, d + half] = 1.0
    return cosf, sinf, jnp.asarray(rot)


def _rope(t, cosf, sinf, rot):
    shp = t.shape
    t2 = t.reshape(-1, HEAD_DIM)
    sw = lax.dot_general(t2, rot, _DIMS["nn"], precision=lax.Precision.HIGH, preferred_element_type=F32).reshape(shp)
    return t * cosf + sw * sinf


def _softmax_sink_parts(s, sink):
    m = jnp.maximum(jnp.max(s, axis=-1, keepdims=True), sink)
    e = jnp.exp(s - m)
    es = jnp.exp(sink - m)
    r = 1.0 / (jnp.sum(e, axis=-1, keepdims=True) + es)
    return e * r, es * r


@jax.custom_vjp
def _softmax_sink(s, sink):
    return _softmax_sink_parts(s, sink)[0]


def _softmax_sink_fwd(s, sink):
    p, p_sink = _softmax_sink_parts(s, sink)
    return p, (p, p_sink)


def _softmax_sink_bwd(res, dp):
    p, p_sink = res
    t = jnp.sum(p * dp, axis=-1, keepdims=True)
    return p * (dp - t), -jnp.sum(p_sink * t, axis=1, keepdims=True)


_softmax_sink.defvjp(_softmax_sink_fwd, _softmax_sink_bwd)


def _attn_block_fn(q, kw, vw, sinks, cq, sq, ck, sk, rot, q0, k0):
    nk = kw.shape[1]
    qr = _rope(q, cq, sq, rot)
    kr = _rope(kw, ck, sk, rot)
    qpos = q0 + lax.broadcasted_iota(jnp.int32, (1, ATT_BLOCK, nk), 1)
    kpos = k0 + lax.broadcasted_iota(jnp.int32, (1, ATT_BLOCK, nk), 2)
    diff = qpos - kpos
    allowed = (diff >= 0) & (diff < WINDOW)
    outs = []
    for kh in range(ATT_KV_HEADS):
        qh = qr[kh * GQA_GROUP:(kh + 1) * GQA_GROUP].reshape(GQA_GROUP * ATT_BLOCK, HEAD_DIM)
        s = _mm_nt(qh, kr[kh]).reshape(GQA_GROUP, ATT_BLOCK, nk) * (HEAD_DIM ** -0.5)
        s = jnp.where(allowed, s, NEG_INF)
        p = _softmax_sink(s, sinks[kh * GQA_GROUP:(kh + 1) * GQA_GROUP])
        o = _mm_nn(p.reshape(GQA_GROUP * ATT_BLOCK, nk), vw[kh])
        outs.append(o.reshape(GQA_GROUP, ATT_BLOCK, HEAD_DIM))
    return jnp.concatenate(outs, axis=0)


def _attn_common(L):
    nwin = min(2 * ATT_BLOCK, L)
    qspec = pl.BlockSpec((ATT_HEADS, ATT_BLOCK, HEAD_DIM), lambda n: (0, n, 0))
    kvspec = pl.BlockSpec((ATT_KV_HEADS, L, HEAD_DIM), lambda n: (0, 0, 0))
    sspec = pl.BlockSpec((ATT_HEADS, 1, 1), lambda n: (0, 0, 0))
    tq = pl.BlockSpec((ATT_BLOCK, HEAD_DIM), lambda n: (n, 0))
    tk = pl.BlockSpec((L, HEAD_DIM), lambda n: (0, 0))
    rspec = pl.BlockSpec((HEAD_DIM, HEAD_DIM), lambda n: (0, 0))
    return nwin, qspec, kvspec, sspec, tq, tk, rspec


def _attn_fwd(qh, kh, vh, sinks, cosf, sinf, rot, *, name):
    L = qh.shape[1]
    nwin, qspec, kvspec, sspec, tq, tk, rspec = _attn_common(L)

    def body(q_ref, k_ref, v_ref, s_ref, cq_ref, sq_ref, ck_ref, sk_ref, r_ref, o_ref):
        n = pl.program_id(0)
        k0 = pl.multiple_of(jnp.maximum(n - 1, 0) * ATT_BLOCK, ATT_BLOCK)
        win = pl.ds(k0, nwin)
        o_ref[...] = _attn_block_fn(q_ref[...], k_ref[:, win, :], v_ref[:, win, :], s_ref[...],
                                    cq_ref[...], sq_ref[...], ck_ref[win, :], sk_ref[win, :], r_ref[...],
                                    n * ATT_BLOCK, k0)

    return pl.pallas_call(
        body, name=name, grid=(L // ATT_BLOCK,),
        in_specs=[qspec, kvspec, kvspec, sspec, tq, tq, tk, tk, rspec],
        out_specs=qspec,
        out_shape=jax.ShapeDtypeStruct((ATT_HEADS, L, HEAD_DIM), F32),
        compiler_params=_cp(("parallel",)),
    )(qh, kh, vh, sinks.reshape(ATT_HEADS, 1, 1), cosf, sinf, cosf, sinf, rot)


def _attn_bwd(do, qh, kh, vh, sinks, cosf, sinf, rot, *, name):
    L = qh.shape[1]
    nwin, qspec, kvspec, sspec, tq, tk, rspec = _attn_common(L)

    def body(do_ref, q_ref, k_ref, v_ref, s_ref, cq_ref, sq_ref, ck_ref, sk_ref, r_ref,
             dq_ref, dk_ref, dv_ref, ds_ref):
        n = pl.program_id(0)
        k0 = pl.multiple_of(jnp.maximum(n - 1, 0) * ATT_BLOCK, ATT_BLOCK)
        win = pl.ds(k0, nwin)
        cq, sq, ck, sk, rt = cq_ref[...], sq_ref[...], ck_ref[win, :], sk_ref[win, :], r_ref[...]
        q0 = n * ATT_BLOCK
        _, vjp = jax.vjp(lambda q, kw, vw, s: _attn_block_fn(q, kw, vw, s, cq, sq, ck, sk, rt, q0, k0),
                         q_ref[...], k_ref[:, win, :], v_ref[:, win, :], s_ref[...])
        dq, dkw, dvw, ds = vjp(do_ref[...])

        @pl.when(n == 0)
        def _():
            dk_ref[...] = jnp.zeros_like(dk_ref)
            dv_ref[...] = jnp.zeros_like(dv_ref)
            ds_ref[...] = jnp.zeros_like(ds_ref)

        dq_ref[...] = dq
        dk_ref[:, win, :] += dkw
        dv_ref[:, win, :] += dvw
        ds_ref[...] += ds

    return pl.pallas_call(
        body, name=name, grid=(L // ATT_BLOCK,),
        in_specs=[qspec, qspec, kvspec, kvspec, sspec, tq, tq, tk, tk, rspec],
        out_specs=[qspec, kvspec, kvspec, sspec],
        out_shape=[jax.ShapeDtypeStruct((ATT_HEADS, L, HEAD_DIM), F32),
                   jax.ShapeDtypeStruct((ATT_KV_HEADS, L, HEAD_DIM), F32),
                   jax.ShapeDtypeStruct((ATT_KV_HEADS, L, HEAD_DIM), F32),
                   jax.ShapeDtypeStruct((ATT_HEADS, 1, 1), F32)],
        compiler_params=_cp(("arbitrary",)),
    )(do, qh, kh, vh, sinks.reshape(ATT_HEADS, 1, 1), cosf, sinf, cosf, sinf, rot)


def _to_heads(t, nh):
    L = t.shape[0]
    return t.reshape(L, nh, HEAD_DIM).transpose(1, 0, 2)


def _from_heads(t):
    nh, L, _ = t.shape
    return t.transpose(1, 0, 2).reshape(L, nh * HEAD_DIM)


def _branch_fwd(ya, yb, o2d, zc, gates, wa, wb, wc, *, name, tm=256):
    L = ya.shape[0]
    tm = min(tm, L)
    W, D = SSM_WIDTH, D_MODEL

    def body(ya_ref, yb_ref, o_ref, zc_ref, g0_ref, g1_ref, g2_ref, wa_ref, wb_ref, wc_ref,
             mg_ref, ta_ref, tb_ref, tc_ref, yc_ref):
        yc = (o_ref[...] * _silu(zc_ref[...])).astype(MXU)
        ta = _dot(ya_ref[...], wa_ref[...], "nt")
        tb = _dot(yb_ref[...], wb_ref[...], "nt")
        tc = _dot(yc, wc_ref[...], "nt")
        ta_ref[...] = ta
        tb_ref[...] = tb
        tc_ref[...] = tc
        yc_ref[...] = yc
        mg_ref[...] = (jax.nn.sigmoid(g0_ref[...]) * ta + jax.nn.sigmoid(g1_ref[...]) * tb
                       + jax.nn.sigmoid(g2_ref[...]) * tc).astype(MXU)

    row = pl.BlockSpec((tm, W), lambda i: (i, 0))
    wide = pl.BlockSpec((tm, D), lambda i: (i, 0))
    gate = lambda c: pl.BlockSpec((tm, D), lambda i, c=c: (i, c))
    wspec = pl.BlockSpec((D, W), lambda i: (0, 0))
    return pl.pallas_call(
        body, name=name, grid=(L // tm,),
        in_specs=[row, row, row, row, gate(0), gate(1), gate(2), wspec, wspec, wspec],
        out_specs=[wide, wide, wide, wide, row],
        out_shape=[jax.ShapeDtypeStruct((L, D), MXU), jax.ShapeDtypeStruct((L, D), F32),
                   jax.ShapeDtypeStruct((L, D), F32), jax.ShapeDtypeStruct((L, D), F32),
                   jax.ShapeDtypeStruct((L, W), MXU)],
        compiler_params=_cp(("parallel",), 56),
    )(ya, yb, o2d, zc, gates, gates, gates, wa, wb, wc)


def _branch_bwd(dmg, ta, tb, tc, gates, *, name, tm=256):
    L = dmg.shape[0]
    tm = min(tm, L)
    D = D_MODEL

    def body(dm_ref, ta_ref, tb_ref, tc_ref, g0_ref, g1_ref, g2_ref, da_ref, db_ref, dc_ref, dg_ref):
        dm = dm_ref[...]
        for i, (t_ref, g_ref, d_ref) in enumerate(((ta_ref, g0_ref, da_ref), (tb_ref, g1_ref, db_ref),
                                                   (tc_ref, g2_ref, dc_ref))):
            sg = jax.nn.sigmoid(g_ref[...])
            d_ref[...] = (sg * dm).astype(MXU)
            dg_ref[:, i * D:(i + 1) * D] = (dm * t_ref[...] * sg * (1.0 - sg)).astype(MXU)

    wide = pl.BlockSpec((tm, D), lambda i: (i, 0))
    gate = lambda c: pl.BlockSpec((tm, D), lambda i, c=c: (i, c))
    bf = jax.ShapeDtypeStruct((L, D), MXU)
    return pl.pallas_call(
        body, name=name, grid=(L // tm,),
        in_specs=[wide, wide, wide, wide, gate(0), gate(1), gate(2)],
        out_specs=[wide, wide, wide, pl.BlockSpec((tm, 3 * D), lambda i: (i, 0))],
        out_shape=[bf, bf, bf, jax.ShapeDtypeStruct((L, 3 * D), MXU)],
        compiler_params=_cp(("parallel",), 56),
    )(dmg, ta, tb, tc, gates, gates, gates)


def _gate_c_bwd(dyc, o2d, zc, *, name, tm=256):
    L, W = dyc.shape
    tm = min(tm, L)

    def body(dy_ref, o_ref, z_ref, do_ref, dz_ref):
        _, vjp = jax.vjp(lambda o, z: o * _silu(z), o_ref[...], z_ref[...])
        do, dz = vjp(dy_ref[...])
        do_ref[...] = do
        dz_ref[...] = dz.astype(MXU)

    row = pl.BlockSpec((tm, W), lambda i: (i, 0))
    return pl.pallas_call(body, name=name, grid=(L // tm,), in_specs=[row, row, row], out_specs=[row, row],
                          out_shape=[jax.ShapeDtypeStruct((L, W), F32), jax.ShapeDtypeStruct((L, W), MXU)],
                          compiler_params=_cp(("parallel",)))(dyc, o2d, zc)


def _adamw(w, g, m, v, *, name):
    shape = w.shape
    cols = shape[-1]
    w2, g2, m2, v2 = (t.reshape(-1, cols) for t in (w, g, m, v))
    rows = w2.shape[0]
    tc = 1024 if cols % 1024 == 0 else cols
    lane_cols = -(-tc // 128) * 128
    tr = rows
    while tr % 16 == 0 and tr * lane_cols * 4 > 2 * _MB:
        tr //= 2

    def body(w_ref, g_ref, m_ref, v_ref, d_ref, nm_ref, nv_ref):
        gv = g_ref[...]
        nm = ADAM_B1 * m_ref[...] + (1.0 - ADAM_B1) * gv
        nv = ADAM_B2 * v_ref[...] + (1.0 - ADAM_B2) * jnp.square(gv)
        m_hat = nm / (1.0 - ADAM_B1 ** ADAM_STEP)
        v_hat = nv / (1.0 - ADAM_B2 ** ADAM_STEP)
        d_ref[...] = -ADAM_LR * (m_hat / (jnp.sqrt(v_hat) + ADAM_EPS) + ADAM_WD * w_ref[...])
        nm_ref[...] = nm
        nv_ref[...] = nv

    spec = pl.BlockSpec((tr, tc), lambda i, j: (i, j))
    out = jax.ShapeDtypeStruct((rows, cols), F32)
    d, nm, nv = pl.pallas_call(body, name=name, grid=(rows // tr, cols // tc), in_specs=[spec] * 4,
                               out_specs=[spec] * 3, out_shape=[out, out, out],
                               compiler_params=_cp(("parallel", "parallel")))(w2, g2, m2, v2)
    return d.reshape(shape), nm.reshape(shape), nv.reshape(shape)


def _adamw_layer(w, g, m, v, l, prev, *, name):
    _, rows, cols = w.shape
    tc = 1024 if cols % 1024 == 0 else cols
    tr = rows
    while tr % 16 == 0 and tr * tc * 4 > 2 * _MB:
        tr //= 2

    def body(w_ref, g_ref, m_ref, v_ref, *rest):
        go_ref, d_ref, nm_ref, nv_ref = rest[-4:]
        gv = g_ref[...]
        nm = ADAM_B1 * m_ref[...] + (1.0 - ADAM_B1) * gv
        nv = ADAM_B2 * v_ref[...] + (1.0 - ADAM_B2) * jnp.square(gv)
        m_hat = nm / (1.0 - ADAM_B1 ** ADAM_STEP)
        v_hat = nv / (1.0 - ADAM_B2 ** ADAM_STEP)
        d_ref[...] = -ADAM_LR * (m_hat / (jnp.sqrt(v_hat) + ADAM_EPS) + ADAM_WD * w_ref[...])
        nm_ref[...] = nm
        nv_ref[...] = nv
        go_ref[...] = gv

    lspec = pl.BlockSpec((None, tr, tc), lambda i, j: (l, i, j))
    gspec = pl.BlockSpec((tr, tc), lambda i, j: (i, j))
    out = jax.ShapeDtypeStruct(w.shape, F32)
    extra = [] if prev is None else list(prev)
    return pl.pallas_call(
        body, name=name, grid=(rows // tr, cols // tc),
        in_specs=[lspec, gspec, lspec, lspec] + [_ANY] * len(extra),
        out_specs=[lspec] * 4, out_shape=[out] * 4,
        input_output_aliases={4 + i: i for i in range(len(extra))},
        compiler_params=_cp(("parallel", "parallel")),
    )(w, g, m, v, *extra)


def _prep_layer(p, l):
    are = p["ssm_a_re"][l].reshape(SSM_CH, 1)
    aim = p["ssm_a_im"][l].reshape(SSM_CH, 1)
    ldt = p["ssm_log_dt"][l].reshape(1, SSM_GROUPS)
    bre = p["ssm_b_re"][l].reshape(SSM_CH, SSM_GROUP)
    bim = p["ssm_b_im"][l].reshape(SSM_CH, SSM_GROUP)
    lbr, lbi, bbr, bbi = _s5_params_fwd(are, aim, ldt, bre, bim, name=f"s5_params_fwd_{l}")
    cre = p["ssm_c_re"][l].transpose(0, 2, 1).reshape(SSM_CH, SSM_GROUP)
    cim = p["ssm_c_im"][l].transpose(0, 2, 1).reshape(SSM_CH, SSM_GROUP)
    return dict(raw=(are, aim, ldt, bre, bim),
                lbr=lbr.reshape(N_SLAB * SLAB_NC, 1, 128), lbi=lbi.reshape(N_SLAB * SLAB_NC, 1, 128),
                btr=_expand_bd(bbr), bti=_expand_bd(bbi), cbr=_expand_bd(cre), cbi=_expand_bd(cim),
                dvec=p["ssm_d"][l].reshape(1, SSM_WIDTH))


def _layer_fwd(x, h, p, sp, winT, rest_of, l, tabs, proj_after=None):
    L = x.shape[0]
    cosf, sinf, rot = tabs
    mm = functools.partial(_matmul, h, winT, "nt", tm=L, tn=256, tk=D_MODEL)
    main = mm(name=f"proj_main_{l}", shape=(L, N_MAIN, D_MODEL), after=proj_after)
    zc = mm(name=f"proj_zc_{l}", shape=(L, N_ZC, D_MODEL), b_off=(N_MAIN // 256, 0))
    gates = mm(name=f"proj_gates_{l}", shape=(L, N_GATES, D_MODEL), b_off=((N_MAIN + N_ZC) // 256, 0))
    big, token = rest_of(gates)
    ua = _time_interleave(main[:, :SSM_WIDTH])
    if token is not None:
        ua = ua + token[0, 0]
    ys, sr, si = _s5_fwd(ua, sp["btr"], sp["bti"], sp["cbr"], sp["cbi"], sp["lbr"], sp["lbi"], sp["dvec"],
                         name=f"s5_fwd_{l}")
    ys = _time_deinterleave(ys)
    ya = _glu_fwd(ys, main, big["glu_w"], p["ssm_glu_b"][l], name=f"glu_fwd_{l}")
    yb = _sg_fwd(main, p["sg_ln_w"][l], p["sg_ln_b"][l], p["sg_w"][l], p["sg_b"][l], name=f"sg_fwd_{l}")
    qh = _to_heads(main[:, 5120:6144], ATT_HEADS)
    kh = _to_heads(main[:, 6144:6272], ATT_KV_HEADS)
    vh = _to_heads(main[:, 6272:6400], ATT_KV_HEADS)
    oh = _attn_fwd(qh, kh, vh, p["attn_sinks"][l], cosf, sinf, rot, name=f"attn_fwd_{l}")
    o2d = _from_heads(oh)
    mg, ta, tb, tc, yc = _branch_fwd(ya, yb, o2d, zc, gates, big["wbaT"], big["wbbT"], big["wbcT"],
                                     name=f"branch_fwd_{l}")
    xn = _matmul(mg, big["w_out"], "nn", name=f"out_fwd_{l}", shape=(L, D_MODEL, D_MODEL), tm=512, tn=512,
                 tk=D_MODEL, add=x)
    saved = dict(x=x, h=h, main=main, zc=zc, gates=gates, ua=ua, ys=ys, sr=sr, si=si, ya=ya, yb=yb, yc=yc, o2d=o2d,
                 qh=qh, kh=kh, vh=vh, mg=mg, ta=ta, tb=tb, tc=tc, sp=sp)
    return xn, saved, big


def _layer_bwd(dxn, s, p, big, l, tabs, early):
    L = dxn.shape[0]
    D, W = D_MODEL, SSM_WIDTH
    cosf, sinf, rot = tabs
    sp = s["sp"]
    g = {}
    dmg = _matmul(dxn, big["w_out"], "nt", name=f"out_bwd_dm_{l}", shape=(L, D, D), tm=512, tn=512, tk=D)
    g["w_out"] = _matmul(s["mg"], dxn, "tn", name=f"out_bwd_dw_{l}", shape=(D, D, L), tm=512, tn=512, tk=L,
                         out_dtype=MXU)
    dta, dtb, dtc, dgates = _branch_bwd(dmg, s["ta"], s["tb"], s["tc"], s["gates"], name=f"branch_bwd_{l}")
    dys_ = {}
    for nm, dt, y, wt in (("a", dta, s["ya"], big["wbaT"]), ("b", dtb, s["yb"], big["wbbT"]),
                          ("c", dtc, s["yc"], big["wbcT"])):
        dys_[nm] = _matmul(dt, wt, "nn", name=f"branch_bwd_dy{nm}_{l}", shape=(L, W, D), tm=512, tn=512, tk=D)
        g["wb" + nm + "T"] = _matmul(dt, y, "tn", name=f"branch_bwd_dw{nm}_{l}", shape=(D, W, L),
                                     tm=512, tn=512, tk=L, out_dtype=MXU)
    do2d, dzc = _gate_c_bwd(dys_["c"], s["o2d"], s["zc"], name=f"gate_c_bwd_{l}")
    dqh, dkh, dvh, dsinks = _attn_bwd(_to_heads(do2d, ATT_HEADS), s["qh"], s["kh"], s["vh"], p["attn_sinks"][l],
                                      cosf, sinf, rot, name=f"attn_bwd_{l}")
    g["attn_sinks"] = dsinks.reshape(ATT_HEADS)
    dub, dvb, dzb, dlnw, dlnb, dsgw, dsgb = _sg_bwd(dys_["b"], s["main"], p["sg_ln_w"][l], p["sg_ln_b"][l],
                                                    p["sg_w"][l], p["sg_b"][l], name=f"sg_bwd_{l}")
    g["sg_ln_w"], g["sg_ln_b"] = dlnw.reshape(W), dlnb.reshape(W)
    g["sg_w"], g["sg_b"] = dsgw, dsgb.reshape(SG_HEADS, SG_CHUNK)
    dys, dza, a1, dzl, dgb = _glu_bwd(dys_["a"], s["ys"], s["main"], big["glu_w"], p["ssm_glu_b"][l],
                                      name=f"glu_bwd_{l}")
    g["ssm_glu_b"] = dgb.reshape(W)
    g["glu_w"] = _matmul(a1, dzl, "tn", name=f"glu_bwd_dw_{l}", shape=(W, W, L), tm=512, tn=512, tk=L, out_dtype=MXU)
    token = early(g)
    dys = _time_interleave(dys)
    if token is not None:
        dys = dys + token[0, 0]
    dua, dbtr, dbti, dcbr, dcbi, dlr, dli, dd = _s5_bwd(dys, s["ua"], s["sr"], s["si"], sp["btr"],
                                                        sp["bti"], sp["cbr"], sp["cbi"], sp["lbr"], sp["lbi"],
                                                        sp["dvec"], name=f"s5_bwd_{l}")
    dua = _time_deinterleave(dua)
    g["ssm_d"] = dd.reshape(W)
    to_c = lambda t: _contract_bd(t).reshape(SSM_GROUPS, SSM_STATE, SSM_GROUP).transpose(0, 2, 1)
    g["ssm_c_re"], g["ssm_c_im"] = to_c(dcbr), to_c(dcbi)
    dare, daim, dldt, dbre, dbim = _s5_params_bwd(*sp["raw"], dlr.reshape(SSM_CH, 1), dli.reshape(SSM_CH, 1),
                                                  _contract_bd(dbtr), _contract_bd(dbti),
                                                  name=f"s5_params_bwd_{l}")
    g["ssm_a_re"] = dare.reshape(SSM_GROUPS, SSM_STATE)
    g["ssm_a_im"] = daim.reshape(SSM_GROUPS, SSM_STATE)
    g["ssm_log_dt"] = dldt.reshape(SSM_GROUPS)
    g["ssm_b_re"] = dbre.reshape(SSM_GROUPS, SSM_STATE, SSM_GROUP)
    g["ssm_b_im"] = dbim.reshape(SSM_GROUPS, SSM_STATE, SSM_GROUP)
    dproj = jnp.concatenate([t.astype(MXU) for t in (dua, dza, dub, dvb, dzb, _from_heads(dqh), _from_heads(dkh),
                                                     _from_heads(dvh), dzc, dgates)], axis=1)
    g["winT"] = _matmul(dproj, s["h"], "tn", name=f"proj_bwd_dw_{l}", shape=(D_IN, D, L), tm=256, tn=D, tk=L,
                        out_dtype=MXU)
    return dproj, g


def _layer_bwd_input(dproj, dxn, s, p, big, l, after):
    L = dxn.shape[0]
    dh = _matmul(dproj, big["winT"], "nn", name=f"proj_bwd_dh_{l}", shape=(L, D_MODEL, D_IN), tm=512, tn=512,
                 tk=D_IN // 2, after=after)
    dx, dnw = _rms_bwd(s["x"], p["norm_w"][l], dh, dxn, name=f"rms_bwd_{l}")
    return dx, dnw


MESH = pl.DeviceIdType.MESH
_ANY = pl.BlockSpec(memory_space=pl.ANY)
ROW_ALIGN = 16


def _coords():
    return lax.axis_index("x"), lax.axis_index("y"), lax.axis_index("c")


def _gather8(arrs, *, name):
    n = len(arrs)
    rows = [a.shape[0] for a in arrs]
    for r in rows:
        assert r % ROW_ALIGN == 0

    def body(*refs):
        ins, outs = refs[:n], refs[n:2 * n]
        send, recv, lsem = refs[2 * n:]
        x, y, c = _coords()
        me, sibling = (x, y, c), (x, y, 1 - c)
        chips = [(1 - x, y), (x, 1 - y), (1 - x, 1 - y)]

        def blk(a, px, py, pc):
            return outs[a].at[pl.ds(pl.multiple_of((4 * px + 2 * py + pc) * rows[a], ROW_ALIGN), rows[a]), :]

        def own(a):
            return ins[a]

        def copy(a, k, block, to, src=None):
            return pltpu.make_async_remote_copy(
                src_ref=blk(a, *block) if src is None else src, dst_ref=blk(a, *block),
                send_sem=send.at[a, k], recv_sem=recv.at[a, k], device_id=to, device_id_type=MESH)

        mine, first, passed = [], [], []
        for a in range(n):
            mine.append(pltpu.make_async_copy(own(a), blk(a, *me), lsem.at[a]))
            mine[a].start()
            f = [copy(a, 0, me, sibling, src=own(a))]
            f += [copy(a, 1 + j, me, (*chip, c), src=own(a)) for j, chip in enumerate(chips)]
            for cp in f:
                cp.start()
            first.append(f)
        for a in range(n):
            ps = [copy(a, 4 + j, (*chip, c), sibling) for j, chip in enumerate(chips)]
            for j, chip in enumerate(chips):
                copy(a, 1 + j, (*chip, c), me).wait_recv()
                ps[j].start()
            passed.append(ps)
        for a in range(n):
            copy(a, 0, sibling, me).wait_recv()
            for j, chip in enumerate(chips):
                copy(a, 4 + j, (*chip, 1 - c), me).wait_recv()
            for cp in first[a] + passed[a]:
                cp.wait_send()
            mine[a].wait()

    return pl.pallas_call(
        body, name=name,
        in_specs=[_ANY] * n, out_specs=[_ANY] * n,
        out_shape=[jax.ShapeDtypeStruct((8 * r,) + a.shape[1:], a.dtype) for r, a in zip(rows, arrs)],
        scratch_shapes=[pltpu.SemaphoreType.DMA((n, 7)), pltpu.SemaphoreType.DMA((n, 7)), pltpu.SemaphoreType.DMA((n,))],
    )(*arrs)


def _sibling_swap(arrs, *, pick_other, name):
    n = len(arrs)

    def body(*refs):
        ins, outs = refs[:n], refs[n:2 * n]
        send, recv = refs[2 * n:]
        x, y, c = _coords()
        sel = (1 - c) if pick_other else c
        cps = [pltpu.make_async_remote_copy(src_ref=ins[a].at[:, sel], dst_ref=outs[a], send_sem=send.at[a],
                                            recv_sem=recv.at[a], device_id=(x, y, 1 - c), device_id_type=MESH)
               for a in range(n)]
        for cp in cps:
            cp.start()
        for cp in cps:
            cp.wait_recv()
        for cp in cps:
            cp.wait_send()

    return pl.pallas_call(
        body, name=name, in_specs=[_ANY] * n, out_specs=[_ANY] * n,
        out_shape=[jax.ShapeDtypeStruct((a.shape[0],) + a.shape[2:], a.dtype) for a in arrs],
        scratch_shapes=[pltpu.SemaphoreType.DMA((n,)), pltpu.SemaphoreType.DMA((n,))],
    )(*arrs)


def _col_tile(lead, rows, cols, itemsize=4, cap=4 * _MB):
    tc = cols
    while tc % 256 == 0 and lead * rows * tc * itemsize > cap:
        tc //= 2
    return tc


def _pair_sum(mine, theirs, *, name):
    _, _, rows, cols = mine.shape
    tc = _col_tile(1, rows, cols)
    c = lax.axis_index("c")

    def body(c_ref, a_ref, b_ref, o_ref):
        o_ref[...] = (a_ref[...].astype(F32) + b_ref[...].astype(F32)).astype(MXU)

    return pl.pallas_call(
        body, name=name,
        grid_spec=pltpu.PrefetchScalarGridSpec(
            num_scalar_prefetch=1, grid=(4, cols // tc),
            in_specs=[pl.BlockSpec((None, None, rows, tc), lambda j, i, cr: (j, cr[0], 0, i)),
                      pl.BlockSpec((None, rows, tc), lambda j, i, cr: (j, 0, i))],
            out_specs=pl.BlockSpec((None, rows, tc), lambda j, i, cr: (j, 0, i))),
        out_shape=jax.ShapeDtypeStruct((4, rows, cols), MXU),
        compiler_params=_cp(("parallel", "parallel")),
    )(c.reshape(1).astype(jnp.int32), mine, theirs)


_HBM = pl.BlockSpec(memory_space=pltpu.HBM)
_SEM = pl.BlockSpec(memory_space=pltpu.SEMAPHORE)
_EFFECT = pltpu.SideEffectType.DATAFLOW_SIDE_EFFECTING
N_PEER_CHIPS = 3


def _peer_chips(x, y):
    return [(1 - x, y), (x, 1 - y), (1 - x, 1 - y)]


def _split_start(srcs, lands, src_slot, dst_slot, *, name):
    n = len(srcs)
    ns = n * N_PEER_CHIPS

    def body(*refs):
        src_refs, land_refs = refs[:n], refs[n:2 * n]
        send, recv, token = refs[2 * n:2 * n + ns], refs[2 * n + ns:2 * n + 2 * ns], refs[-1]
        x, y, c = _coords()
        for a in range(n):
            for k, (px, py) in enumerate(_peer_chips(x, y)):
                pltpu.make_async_remote_copy(
                    src_ref=src_refs[a].at[src_slot(x, y, c, px, py)], dst_ref=land_refs[a].at[dst_slot(x, y, c)],
                    send_sem=send[a * N_PEER_CHIPS + k], recv_sem=recv[a * N_PEER_CHIPS + k],
                    device_id=(px, py, c), device_id_type=MESH).start()
        token[...] = jnp.zeros_like(token)

    bufs = list(srcs) + list(lands)
    res = pl.pallas_call(
        body, name=name,
        out_shape=(*[pltpu.SemaphoreType.DMA(())] * (2 * ns), *[pltpu.HBM(b.shape, b.dtype) for b in bufs],
                   jax.ShapeDtypeStruct((8, 128), F32)),
        in_specs=[_HBM] * (2 * n),
        out_specs=(*[_SEM] * (2 * ns), *[_HBM] * (2 * n), pl.BlockSpec(memory_space=pltpu.VMEM)),
        input_output_aliases={i: 2 * ns + i for i in range(2 * n)},
        compiler_params=pltpu.CompilerParams(has_side_effects=_EFFECT),
    )(*[pltpu.with_memory_space_constraint(b, pltpu.HBM) for b in bufs])
    sems = list(res[:2 * ns])
    return sems, list(res[2 * ns:2 * ns + n]), list(res[2 * ns + n:2 * ns + 2 * n]), res[-1]


def _split_wait(sems, srcs, lands, after, *, name):
    n = len(srcs)
    ns = n * N_PEER_CHIPS

    def body(*refs):
        src_refs, land_refs = refs[:n], refs[n:2 * n]
        send, recv = refs[2 * n:2 * n + ns], refs[2 * n + ns:2 * n + 2 * ns]
        x, y, c = _coords()
        for a in range(n):
            for k in range(N_PEER_CHIPS):
                cp = pltpu.make_async_remote_copy(
                    src_ref=src_refs[a].at[0], dst_ref=land_refs[a].at[0], send_sem=send[a * N_PEER_CHIPS + k],
                    recv_sem=recv[a * N_PEER_CHIPS + k], device_id=(x, y, 1 - c), device_id_type=MESH)
                cp.wait_send()
                cp.wait_recv()

    bufs = list(srcs) + list(lands)
    res = pl.pallas_call(
        body, name=name,
        out_shape=tuple(pltpu.HBM(b.shape, b.dtype) for b in bufs),
        in_specs=[_HBM] * (2 * n) + [_SEM] * (2 * ns) + [_ANY] * len(after),
        out_specs=tuple([_HBM] * (2 * n)),
        input_output_aliases={i: i for i in range(2 * n)},
        compiler_params=pltpu.CompilerParams(has_side_effects=_EFFECT),
    )(*bufs, *sems, *after)
    return list(res[:n]), list(res[n:])


def _fill_own(shard2, *, name):
    _, rows, cols = shard2.shape
    tc = _col_tile(1, rows, cols, itemsize=shard2.dtype.itemsize)
    j = 2 * lax.axis_index("x") + lax.axis_index("y")

    def body(j_ref, s_ref, o_ref):
        o_ref[...] = s_ref[...]

    return pl.pallas_call(
        body, name=name,
        grid_spec=pltpu.PrefetchScalarGridSpec(
            num_scalar_prefetch=1, grid=(2, cols // tc),
            in_specs=[pl.BlockSpec((None, rows, tc), lambda h, i, jr: (h, 0, i))],
            out_specs=pl.BlockSpec((None, rows, tc), lambda h, i, jr: (2 * jr[0] + h, 0, i))),
        out_shape=jax.ShapeDtypeStruct((8, rows, cols), shard2.dtype),
        compiler_params=_cp(("parallel", "parallel")),
    )(j.reshape(1).astype(jnp.int32), shard2)


def _pass_to_sibling(lands, *, name):
    n = len(lands)

    def body(*refs):
        outs = refs[n:2 * n]
        send, recv = refs[2 * n:]
        x, y, c = _coords()
        cps = []
        for a in range(n):
            for k, (px, py) in enumerate(_peer_chips(x, y)):
                slot = 4 * px + 2 * py + c
                cps.append(pltpu.make_async_remote_copy(
                    src_ref=outs[a].at[slot], dst_ref=outs[a].at[slot], send_sem=send.at[a, k], recv_sem=recv.at[a, k],
                    device_id=(x, y, 1 - c), device_id_type=MESH))
        for cp in cps:
            cp.start()
        for cp in cps:
            cp.wait_recv()
        for cp in cps:
            cp.wait_send()

    return pl.pallas_call(
        body, name=name, in_specs=[_ANY] * n, out_specs=[_ANY] * n,
        out_shape=[jax.ShapeDtypeStruct(b.shape, b.dtype) for b in lands],
        input_output_aliases={a: a for a in range(n)},
        scratch_shapes=[pltpu.SemaphoreType.DMA((n, N_PEER_CHIPS)), pltpu.SemaphoreType.DMA((n, N_PEER_CHIPS))],
    )(*lands)


def _sum_parts(parts, got, *, name):
    _, rows, cols = parts.shape
    tc = _col_tile(4, rows, cols, itemsize=parts.dtype.itemsize)
    x, y, c = _coords()
    idx = jnp.stack([2 * x + y, 2 * (1 - x) + y, 2 * x + (1 - y), 2 * (1 - x) + (1 - y), c]).astype(jnp.int32)

    def body(i_ref, p_ref, g0_ref, g1_ref, g2_ref, o_ref):
        o_ref[...] = ((p_ref[...].astype(F32) + g0_ref[...].astype(F32)) + g1_ref[...].astype(F32)) + g2_ref[...].astype(F32)

    slot = lambda s: pl.BlockSpec((None, rows, tc), lambda i, ir, s=s: (ir[s], 0, i))
    return pl.pallas_call(
        body, name=name,
        grid_spec=pltpu.PrefetchScalarGridSpec(
            num_scalar_prefetch=1, grid=(cols // tc,),
            in_specs=[slot(0), slot(1), slot(2), slot(3)],
            out_specs=pl.BlockSpec((None, rows, tc), lambda i, ir: (ir[4], 0, i))),
        out_shape=jax.ShapeDtypeStruct((2, rows, cols), F32),
        compiler_params=_cp(("parallel",)),
    )(idx, parts, got, got, got)


def _sum_slots(t, *, name):
    S, rows, cols = t.shape
    tc = _col_tile(S, rows, cols)

    def body(t_ref, o_ref):
        acc = t_ref[0].astype(F32)
        for s in range(1, S):
            acc = acc + t_ref[s].astype(F32)
        o_ref[...] = acc

    return pl.pallas_call(
        body, name=name, grid=(cols // tc,),
        in_specs=[pl.BlockSpec((S, rows, tc), lambda i: (0, 0, i))],
        out_specs=pl.BlockSpec((rows, tc), lambda i: (0, i)),
        out_shape=jax.ShapeDtypeStruct((rows, cols), F32),
        compiler_params=_cp(("parallel",)),
    )(t)


def _halves_join(bufs, *, name):
    n = len(bufs)

    def body(*refs):
        outs = refs[n:2 * n]
        send, recv = refs[2 * n:]
        x, y, c = _coords()
        cps = [pltpu.make_async_remote_copy(src_ref=outs[a].at[c], dst_ref=outs[a].at[c], send_sem=send.at[a],
                                            recv_sem=recv.at[a], device_id=(x, y, 1 - c), device_id_type=MESH)
               for a in range(n)]
        for cp in cps:
            cp.start()
        for cp in cps:
            cp.wait_recv()
        for cp in cps:
            cp.wait_send()

    return pl.pallas_call(
        body, name=name, in_specs=[_ANY] * n, out_specs=[_ANY] * n,
        out_shape=[jax.ShapeDtypeStruct(b.shape, b.dtype) for b in bufs],
        input_output_aliases={a: a for a in range(n)},
        scratch_shapes=[pltpu.SemaphoreType.DMA((n,)), pltpu.SemaphoreType.DMA((n,))],
    )(*bufs)


def _reduce_scatter_begin(grads, *, tag):
    views = [g.reshape(4, 2, g.shape[0] // 8, g.shape[1]) for g in grads]
    theirs = _sibling_swap(views, pick_other=True, name=f"rs_swap_{tag}")
    parts = [_pair_sum(v, t, name=f"rs_pair_{tag}_{i}") for i, (v, t) in enumerate(zip(views, theirs))]
    got = [lax.empty(p.shape, p.dtype) for p in parts]
    sems, parts, got, token = _split_start(
        parts, got, lambda x, y, c, px, py: 2 * px + py, lambda x, y, c: 2 * x + y, name=f"rs_start_{tag}")
    return (sems, parts, got), token


def _reduce_scatter_end(state, after, *, tag):
    sems, parts, got = state
    parts, got = _split_wait(sems, parts, got, after, name=f"rs_wait_{tag}")
    halves = [_sum_parts(p, t, name=f"rs_sum_{tag}_{i}") for i, (p, t) in enumerate(zip(parts, got))]
    joined = _halves_join(halves, name=f"rs_join_{tag}")
    return [j.reshape(2 * j.shape[1], j.shape[2]) for j in joined]


_SMALL = ("norm_w", "ssm_a_re", "ssm_a_im", "ssm_log_dt", "ssm_b_re", "ssm_b_im", "ssm_c_re", "ssm_c_im", "ssm_d",
          "ssm_glu_b", "sg_ln_w", "sg_ln_b", "sg_w", "sg_b", "attn_sinks", "final_norm_w")
_BIG = ("w_in", "ssm_glu_w", "w_branch_a", "w_branch_b", "w_branch_c", "w_out")
_WEIGHTS = ("norm_w", "w_in", "ssm_a_re", "ssm_a_im", "ssm_log_dt", "ssm_b_re", "ssm_b_im", "ssm_c_re", "ssm_c_im",
            "ssm_d", "ssm_glu_w", "ssm_glu_b", "sg_ln_w", "sg_ln_b", "sg_w", "sg_b", "attn_sinks", "w_branch_a",
            "w_branch_b", "w_branch_c", "w_out", "final_norm_w")
_PACK_COLS = 1024
_PACK_ALIGN = 8 * ROW_ALIGN * _PACK_COLS


def _slice_exchange(buf, *, name):
    def body(in_ref, out_ref, send, recv, lsem):
        x, y, c = _coords()
        me = 4 * x + 2 * y + c
        own = pltpu.make_async_copy(in_ref.at[me], out_ref.at[me], lsem)
        own.start()
        cps = []
        for k in range(1, 8):
            px, py, pc = x ^ (k >> 2), y ^ ((k >> 1) & 1), c ^ (k & 1)
            cps.append(pltpu.make_async_remote_copy(
                src_ref=in_ref.at[4 * px + 2 * py + pc], dst_ref=out_ref.at[me], send_sem=send.at[k - 1],
                recv_sem=recv.at[k - 1], device_id=(px, py, pc), device_id_type=MESH))
        for cp in cps:
            cp.start()
        for cp in cps:
            cp.wait_recv()
        for cp in cps:
            cp.wait_send()
        own.wait()

    return pl.pallas_call(
        body, name=name, in_specs=[_ANY], out_specs=_ANY,
        out_shape=jax.ShapeDtypeStruct(buf.shape, buf.dtype),
        scratch_shapes=[pltpu.SemaphoreType.DMA((7,)), pltpu.SemaphoreType.DMA((7,)), pltpu.SemaphoreType.DMA],
    )(buf)


def _allreduce_small(packed):
    rows, cols = packed.shape
    got = _slice_exchange(packed.reshape(8, rows // 8, cols), name="small_grads_exchange")
    mine = _sum_slots(got, name="small_grads_sum")
    return _gather8([mine], name="small_grads_gather")[0]


def _pack(ts):
    flat = jnp.concatenate([t.reshape(-1) for t in ts])
    pad = (-flat.shape[0]) % _PACK_ALIGN
    return jnp.pad(flat, (0, pad)).reshape(-1, _PACK_COLS)


def _unpack(buf, like):
    flat = buf.reshape(-1)
    out, pos = [], 0
    for t in like:
        out.append(flat[pos:pos + t.size].reshape(t.shape))
        pos += t.size
    return out


def kernel(x, norm_w, w_in, ssm_a_re, ssm_a_im, ssm_log_dt, ssm_b_re, ssm_b_im, ssm_c_re, ssm_c_im, ssm_d, ssm_glu_w, ssm_glu_b, sg_ln_w, sg_ln_b, sg_w, sg_b, attn_sinks, w_branch_a, w_branch_b, w_branch_c, w_out, final_norm_w, loss_target, m_norm_w, m_w_in, m_ssm_a_re, m_ssm_a_im, m_ssm_log_dt, m_ssm_b_re, m_ssm_b_im, m_ssm_c_re, m_ssm_c_im, m_ssm_d, m_ssm_glu_w, m_ssm_glu_b, m_sg_ln_w, m_sg_ln_b, m_sg_w, m_sg_b, m_attn_sinks, m_w_branch_a, m_w_branch_b, m_w_branch_c, m_w_out, m_final_norm_w, v_norm_w, v_w_in, v_ssm_a_re, v_ssm_a_im, v_ssm_log_dt, v_ssm_b_re, v_ssm_b_im, v_ssm_c_re, v_ssm_c_im, v_ssm_d, v_ssm_glu_w, v_ssm_glu_b, v_sg_ln_w, v_sg_ln_b, v_sg_w, v_sg_b, v_attn_sinks, v_w_branch_a, v_w_branch_b, v_w_branch_c, v_w_out, v_final_norm_w):
    w = dict(norm_w=norm_w, w_in=w_in, ssm_a_re=ssm_a_re, ssm_a_im=ssm_a_im, ssm_log_dt=ssm_log_dt, ssm_b_re=ssm_b_re,
             ssm_b_im=ssm_b_im, ssm_c_re=ssm_c_re, ssm_c_im=ssm_c_im, ssm_d=ssm_d, ssm_glu_w=ssm_glu_w,
             ssm_glu_b=ssm_glu_b, sg_ln_w=sg_ln_w, sg_ln_b=sg_ln_b, sg_w=sg_w, sg_b=sg_b, attn_sinks=attn_sinks,
             w_branch_a=w_branch_a, w_branch_b=w_branch_b, w_branch_c=w_branch_c, w_out=w_out,
             final_norm_w=final_norm_w)
    m = dict(norm_w=m_norm_w, w_in=m_w_in, ssm_a_re=m_ssm_a_re, ssm_a_im=m_ssm_a_im, ssm_log_dt=m_ssm_log_dt,
             ssm_b_re=m_ssm_b_re, ssm_b_im=m_ssm_b_im, ssm_c_re=m_ssm_c_re, ssm_c_im=m_ssm_c_im, ssm_d=m_ssm_d,
             ssm_glu_w=m_ssm_glu_w, ssm_glu_b=m_ssm_glu_b, sg_ln_w=m_sg_ln_w, sg_ln_b=m_sg_ln_b, sg_w=m_sg_w,
             sg_b=m_sg_b, attn_sinks=m_attn_sinks, w_branch_a=m_w_branch_a, w_branch_b=m_w_branch_b,
             w_branch_c=m_w_branch_c, w_out=m_w_out, final_norm_w=m_final_norm_w)
    v = dict(norm_w=v_norm_w, w_in=v_w_in, ssm_a_re=v_ssm_a_re, ssm_a_im=v_ssm_a_im, ssm_log_dt=v_ssm_log_dt,
             ssm_b_re=v_ssm_b_re, ssm_b_im=v_ssm_b_im, ssm_c_re=v_ssm_c_re, ssm_c_im=v_ssm_c_im, ssm_d=v_ssm_d,
             ssm_glu_w=v_ssm_glu_w, ssm_glu_b=v_ssm_glu_b, sg_ln_w=v_sg_ln_w, sg_ln_b=v_sg_ln_b, sg_w=v_sg_w,
             sg_b=v_sg_b, attn_sinks=v_attn_sinks, w_branch_a=v_w_branch_a, w_branch_b=v_w_branch_b,
             w_branch_c=v_w_branch_c, w_out=v_w_out, final_norm_w=v_final_norm_w)

    big_names = ("winT", "glu_w", "wbaT", "wbbT", "wbcT", "w_out")
    L = x.shape[1]
    tabs = _rope_tables(L)
    p = {k: w[k] for k in _SMALL}

    halves = [[t.reshape(2, t.shape[0] // 2, t.shape[1]) for t in
               (w["w_in"][l].T.astype(MXU), w["ssm_glu_w"][l].astype(MXU), w["w_branch_a"][l].T.astype(MXU),
                w["w_branch_b"][l].T.astype(MXU), w["w_branch_c"][l].T.astype(MXU), w["w_out"][l].astype(MXU))]
              for l in range(DEPTH)]
    my_half = lambda x_, y_, c_, px, py: c_
    my_block = lambda x_, y_, c_: 4 * x_ + 2 * y_ + c_
    rows_of = lambda lands: [t.reshape(8 * t.shape[1], t.shape[2]) for t in lands]
    lands = [[_fill_own(s, name=f"gather_fill_{l}_{i}") for i, s in enumerate(halves[l])] for l in range(DEPTH)]
    sp = [_prep_layer(p, l) for l in range(DEPTH)]
    saved = [None] * DEPTH

    sems_a, src_a, land_a, _ = _split_start(halves[0][:1], lands[0][:1], my_half, my_block, name="gather_start_0a")
    h0 = _rms_fwd(x[0], p["norm_w"][0], name="rms_fwd_0")
    _, land_a = _split_wait(sems_a, src_a, land_a, [h0, sp[0]["btr"], sp[1]["btr"]] + lands[1], name="gather_wait_0a")
    land_a = _pass_to_sibling(land_a, name="gather_pass_0a")
    land_a, src_b = lax.optimization_barrier((land_a, halves[0][1:]))
    sems_b, src_b, land_b, token_b = _split_start(src_b, lands[0][1:], my_half, my_block, name="gather_start_0b")
    split1 = {}

    def rest0(t):
        _, got = _split_wait(sems_b, src_b, land_b, [t], name="gather_wait_0b")
        got = _pass_to_sibling(got, name="gather_pass_0b")
        got, src1 = lax.optimization_barrier((got, halves[1]))
        split1["sems"], split1["src"], split1["land"], token1 = _split_start(src1, lands[1], my_half, my_block,
                                                                             name="gather_start_1")
        return dict(zip(big_names, rows_of(land_a + got))), token1

    x1, saved[0], big0 = _layer_fwd(x[0], h0, p, sp[0], rows_of(land_a)[0], rest0, 0, tabs, proj_after=token_b)
    _, lands1 = _split_wait(split1["sems"], split1["src"], split1["land"], [x1], name="gather_wait_1")
    big1 = dict(zip(big_names, rows_of(_pass_to_sibling(lands1, name="gather_pass_1"))))
    bigs = [big0, big1]
    h1 = _rms_fwd(x1, p["norm_w"][1], name="rms_fwd_1")
    x2, saved[1], _ = _layer_fwd(x1, h1, p, sp[1], big1["winT"], lambda t: (big1, None), 1, tabs)
    loss, dx, dfw = _final_loss(x2, p["final_norm_w"], loss_target[0], name="final_loss")
    loss = lax.psum(loss, ("x", "y", "c"))

    grads = [None] * DEPTH
    rs = {}

    def early(l):
        def begin(g):
            rs[f"{l}a"], token_a = _reduce_scatter_begin([g[k] for k in big_names[1:]], tag=f"{l}a")
            return token_a
        return begin

    def late(l, dproj, dx):
        rs[f"{l}b"], token_b = _reduce_scatter_begin([grads[l]["winT"]], tag=f"{l}b")
        return _layer_bwd_input(dproj, dx, saved[l], p, bigs[l], l, token_b)

    def reduced(l, after):
        return _reduce_scatter_end(rs[f"{l}b"], after, tag=f"{l}b") + _reduce_scatter_end(rs[f"{l}a"], after, tag=f"{l}a")

    dproj, grads[1] = _layer_bwd(dx, saved[1], p, bigs[1], 1, tabs, early(1))
    dx, grads[1]["norm_w"] = late(1, dproj, dx)
    dproj, grads[0] = _layer_bwd(dx, saved[0], p, bigs[0], 0, tabs, early(0))
    red1 = reduced(1, [dproj])
    dx, grads[0]["norm_w"] = late(0, dproj, dx)

    tr = lambda t: t.transpose(0, 2, 1)
    view = {k: (tr if k == "w_in" else (lambda t: t)) for k in _BIG}
    shard_grads = lambda red: dict(zip(_BIG, (red[0], red[1], red[2].T, red[3].T, red[4].T, red[5])))
    outs = {k: None for k in _BIG}

    def adamw_big(l, red):
        for k, g in shard_grads(red).items():
            outs[k] = _adamw_layer(view[k](w[k]), g, view[k](m[k]), view[k](v[k]), l, outs[k], name=f"adamw_{k}_{l}")

    adamw_big(1, red1)

    small_like = [w[k] for k in _SMALL]
    gs = [jnp.stack([grads[l][k] for l in range(DEPTH)]) if k != "final_norm_w" else dfw for k in _SMALL]
    gsum = _allreduce_small(_pack(gs))
    adamw_big(0, reduced(0, [gsum] + [outs[k][0] for k in _BIG]))

    gfull, delta, new_m, new_v = {}, {}, {}, {}
    for k in _BIG:
        gfull[k], delta[k], new_m[k], new_v[k] = (view[k](t) for t in outs[k])
    for k, t in zip(_SMALL, _unpack(gsum, small_like)):
        gfull[k] = t
        delta[k], new_m[k], new_v[k] = _adamw(w[k], t, m[k], v[k], name=f"adamw_{k}")

    return (loss, dx[None], *[gfull[k] for k in _WEIGHTS], *[delta[k] for k in _WEIGHTS],
            *[new_m[k] for k in _WEIGHTS], *[new_v[k] for k in _WEIGHTS])
```

```python
import functools
import math

import numpy as np
import jax
import jax.numpy as jnp
from jax import lax
from jax.experimental import pallas as pl
from jax.experimental.pallas import tpu as pltpu

F32 = jnp.float32
MXU = jnp.bfloat16
HIGHEST = lax.Precision.HIGHEST

D_MODEL = 2048
DEPTH = 2
EPS = 1e-6
NEG_INF = -1e30
SSM_WIDTH = 1024
SSM_GROUP = 16
SSM_GROUPS = 64
SSM_STATE = 64
SSM_CH = SSM_GROUPS * SSM_STATE
SLAB = 128
SLAB_CH = (SLAB // SSM_GROUP) * SSM_STATE
N_SLAB = SSM_WIDTH // SLAB
SCAN_SEG = 8
SCAN_STEPS = 4
SG_HEADS = 8
SG_CHUNK = 128
HEAD_DIM = 64
ATT_HEADS = 16
ATT_KV_HEADS = 2
GQA_GROUP = 8
ATT_BLOCK = 128
WINDOW = 128
ROT_DIM = 16
ROPE_THETA = 500000.0
N_MAIN = 6400
N_ZC = 1024
N_GATES = 6144
D_IN = N_MAIN + N_ZC + N_GATES

ADAM_LR = 0.001
ADAM_B1 = 0.9
ADAM_B2 = 0.999
ADAM_EPS = 1e-08
ADAM_WD = 0.01
ADAM_STEP = 10

_DIMS = {"nn": (((1,), (0,)), ((), ())), "nt": (((1,), (1,)), ((), ())), "tn": (((0,), (0,)), ((), ()))}
_MB = 1024 * 1024


def _cp(sem, vmem_mb=48):
    return pltpu.CompilerParams(dimension_semantics=sem, vmem_limit_bytes=vmem_mb * _MB)


def _dot(a, b, mode):
    return lax.dot_general(a.astype(MXU), b.astype(MXU), _DIMS[mode], preferred_element_type=F32)


@jax.custom_vjp
def _mm_nn(a, b):
    return _dot(a, b, "nn")


def _mm_nn_fwd(a, b):
    return _dot(a, b, "nn"), (a, b)


def _mm_nn_bwd(res, g):
    a, b = res
    return _dot(g, b, "nt"), _dot(a, g, "tn")


_mm_nn.defvjp(_mm_nn_fwd, _mm_nn_bwd)


@jax.custom_vjp
def _mm_nt(a, bt):
    return _dot(a, bt, "nt")


def _mm_nt_fwd(a, bt):
    return _dot(a, bt, "nt"), (a, bt)


def _mm_nt_bwd(res, g):
    a, bt = res
    return _dot(g, bt, "nn"), _dot(g, a, "tn")


_mm_nt.defvjp(_mm_nt_fwd, _mm_nt_bwd)


def _rmsnorm(x, w):
    return x * lax.rsqrt(jnp.mean(x * x, axis=-1, keepdims=True) + EPS) * w


def _layernorm(x, w, b):
    mu = jnp.mean(x, axis=-1, keepdims=True)
    var = jnp.mean(jnp.square(x - mu), axis=-1, keepdims=True)
    return (x - mu) * lax.rsqrt(var + EPS) * w + b


def _silu(x):
    return x * jax.nn.sigmoid(x)


def _matmul(a, b, mode, *, name, shape, tm, tn, tk, out_dtype=F32, add=None, a_off=(0, 0), b_off=(0, 0), after=None):
    m, n, k = shape
    tm, tn, tk = min(tm, m), min(tn, n), min(tk, k)
    assert m % tm == 0 and n % tn == 0 and k % tk == 0, (name, shape, tm, tn, tk)
    nk = k // tk
    has_add, has_after = add is not None, after is not None

    def body(*refs):
        a_ref, b_ref = refs[0], refs[1]
        pos = 2
        add_ref = None
        if has_add:
            add_ref = refs[pos]
            pos += 1
        if has_after:
            pos += 1
        o_ref = refs[pos]
        p = _dot(a_ref[...], b_ref[...], mode)
        if nk == 1:
            if has_add:
                p = p + add_ref[...].astype(F32)
            o_ref[...] = p.astype(out_dtype)
            return
        acc_ref = refs[pos + 1]
        kk = pl.program_id(2)

        @pl.when(kk == 0)
        def _():
            acc_ref[...] = p

        @pl.when(kk > 0)
        def _():
            acc_ref[...] += p

        @pl.when(kk == nk - 1)
        def _():
            r = acc_ref[...]
            if has_add:
                r = r + add_ref[...].astype(F32)
            o_ref[...] = r.astype(out_dtype)

    a0, a1 = a_off
    b0, b1 = b_off
    if mode == "tn":
        a_spec = pl.BlockSpec((tk, tm), lambda i, j, kk: (kk + a0, i + a1))
    else:
        a_spec = pl.BlockSpec((tm, tk), lambda i, j, kk: (i + a0, kk + a1))
    if mode == "nt":
        b_spec = pl.BlockSpec((tn, tk), lambda i, j, kk: (j + b0, kk + b1))
    else:
        b_spec = pl.BlockSpec((tk, tn), lambda i, j, kk: (kk + b0, j + b1))
    in_specs = [a_spec, b_spec]
    args = [a, b]
    if has_add:
        in_specs.append(pl.BlockSpec((tm, tn), lambda i, j, kk: (i, j)))
        args.append(add)
    if has_after:
        in_specs.append(pl.BlockSpec(memory_space=pl.ANY))
        args.append(after)
    return pl.pallas_call(
        body, name=name, grid=(m // tm, n // tn, nk),
        in_specs=in_specs,
        out_specs=pl.BlockSpec((tm, tn), lambda i, j, kk: (i, j)),
        out_shape=jax.ShapeDtypeStruct((m, n), out_dtype),
        scratch_shapes=[pltpu.VMEM((tm, tn), F32)] if nk > 1 else [],
        compiler_params=_cp(("parallel", "parallel", "arbitrary")),
    )(*args)


def _rms_fwd(x, w, *, name, tm=256):
    L, d = x.shape
    tm = min(tm, L)

    def body(x_ref, w_ref, h_ref):
        h_ref[...] = _rmsnorm(x_ref[...], w_ref[...]).astype(MXU)

    return pl.pallas_call(
        body, name=name, grid=(L // tm,),
        in_specs=[pl.BlockSpec((tm, d), lambda i: (i, 0)), pl.BlockSpec((1, d), lambda i: (0, 0))],
        out_specs=pl.BlockSpec((tm, d), lambda i: (i, 0)),
        out_shape=jax.ShapeDtypeStruct((L, d), MXU),
        compiler_params=_cp(("parallel",)),
    )(x, w.reshape(1, d))


def _rms_bwd(x, w, dh, dxn, *, name, tm=256):
    L, d = x.shape
    tm = min(tm, L)

    def body(x_ref, w_ref, dh_ref, dxn_ref, dx_ref, dw_ref):
        _, vjp = jax.vjp(_rmsnorm, x_ref[...], w_ref[...])
        dx, dw = vjp(dh_ref[...])
        dx_ref[...] = dx + dxn_ref[...]

        @pl.when(pl.program_id(0) == 0)
        def _():
            dw_ref[...] = jnp.zeros_like(dw_ref)

        dw_ref[...] += dw

    row = pl.BlockSpec((tm, d), lambda i: (i, 0))
    vec = pl.BlockSpec((1, d), lambda i: (0, 0))
    dx, dw = pl.pallas_call(
        body, name=name, grid=(L // tm,),
        in_specs=[row, vec, row, row], out_specs=[row, vec],
        out_shape=[jax.ShapeDtypeStruct((L, d), F32), jax.ShapeDtypeStruct((1, d), F32)],
        compiler_params=_cp(("arbitrary",)),
    )(x, w.reshape(1, d), dh, dxn)
    return dx, dw.reshape(d)


def _final_loss(x, w, tgt, *, name, tm=256):
    L, d = x.shape
    tm = min(tm, L)

    def loss_fn(xv, wv, tv):
        err = jnp.square(_rmsnorm(xv, wv) - tv)
        return 0.5 * jnp.sum(jnp.mean(err, axis=-1, keepdims=True), axis=0, keepdims=True)

    def body(x_ref, w_ref, t_ref, loss_ref, dx_ref, dw_ref):
        tv = t_ref[...]
        val, vjp = jax.vjp(lambda xv, wv: loss_fn(xv, wv, tv), x_ref[...], w_ref[...])
        dx, dw = vjp(jnp.ones((1, 1), F32))
        dx_ref[...] = dx

        @pl.when(pl.program_id(0) == 0)
        def _():
            dw_ref[...] = jnp.zeros_like(dw_ref)
            loss_ref[...] = jnp.zeros_like(loss_ref)

        dw_ref[...] += dw
        loss_ref[...] += jnp.broadcast_to(val, loss_ref.shape)

    row = pl.BlockSpec((tm, d), lambda i: (i, 0))
    vec = pl.BlockSpec((1, d), lambda i: (0, 0))
    loss, dx, dw = pl.pallas_call(
        body, name=name, grid=(L // tm,),
        in_specs=[row, vec, row],
        out_specs=[pl.BlockSpec((8, 128), lambda i: (0, 0)), row, vec],
        out_shape=[jax.ShapeDtypeStruct((8, 128), F32), jax.ShapeDtypeStruct((L, d), F32),
                   jax.ShapeDtypeStruct((1, d), F32)],
        compiler_params=_cp(("arbitrary",)),
    )(x, w.reshape(1, d), tgt)
    return loss[0, 0], dx, dw.reshape(d)


PARAM_ROWS = 512


def _s5_param_fn(are, aim, ldt, bre, bim, row0):
    n = are.shape[0]
    grp = (row0 + lax.broadcasted_iota(jnp.int32, (n, SSM_GROUPS), 0)) // SSM_STATE
    col = lax.broadcasted_iota(jnp.int32, (n, SSM_GROUPS), 1)
    sel = (grp == col).astype(F32)
    dt = jnp.sum(sel * jnp.exp(ldt), axis=-1, keepdims=True)
    mag = jnp.exp(are * dt)
    ang = aim * dt
    lbr = mag * jnp.cos(ang)
    lbi = mag * jnp.sin(ang)
    den = are * are + aim * aim
    nr = lbr - 1.0
    kr = (nr * are + lbi * aim) / den
    ki = (lbi * are - nr * aim) / den
    return lbr, lbi, kr * bre - ki * bim, kr * bim + ki * bre


def _s5_param_specs():
    col = pl.BlockSpec((PARAM_ROWS, 1), lambda i: (i, 0))
    mat = pl.BlockSpec((PARAM_ROWS, SSM_GROUP), lambda i: (i, 0))
    vec = pl.BlockSpec((1, SSM_GROUPS), lambda i: (0, 0))
    return col, mat, vec


def _s5_params_fwd(are, aim, ldt, bre, bim, *, name):
    n = are.shape[0]
    col, mat, vec = _s5_param_specs()

    def body(are_ref, aim_ref, ldt_ref, bre_ref, bim_ref, lbr_ref, lbi_ref, bbr_ref, bbi_ref):
        row0 = pl.program_id(0) * PARAM_ROWS
        lbr, lbi, bbr, bbi = _s5_param_fn(are_ref[...], aim_ref[...], ldt_ref[...], bre_ref[...], bim_ref[...], row0)
        lbr_ref[...] = lbr
        lbi_ref[...] = lbi
        bbr_ref[...] = bbr
        bbi_ref[...] = bbi

    cshape = jax.ShapeDtypeStruct((n, 1), F32)
    mshape = jax.ShapeDtypeStruct((n, SSM_GROUP), F32)
    return pl.pallas_call(body, name=name, grid=(n // PARAM_ROWS,),
                          in_specs=[col, col, vec, mat, mat], out_specs=[col, col, mat, mat],
                          out_shape=[cshape, cshape, mshape, mshape],
                          compiler_params=_cp(("parallel",)))(are, aim, ldt, bre, bim)


def _s5_params_bwd(are, aim, ldt, bre, bim, dlbr, dlbi, dbbr, dbbi, *, name):
    n = are.shape[0]
    col, mat, vec = _s5_param_specs()

    def body(are_ref, aim_ref, ldt_ref, bre_ref, bim_ref, g0, g1, g2, g3, o0, o1, o2, o3, o4):
        row0 = pl.program_id(0) * PARAM_ROWS
        _, vjp = jax.vjp(lambda a, b, c, d, e: _s5_param_fn(a, b, c, d, e, row0),
                         are_ref[...], aim_ref[...], ldt_ref[...], bre_ref[...], bim_ref[...])
        dare, daim, dldt, dbre, dbim = vjp((g0[...], g1[...], g2[...], g3[...]))
        o0[...] = dare
        o1[...] = daim
        o3[...] = dbre
        o4[...] = dbim

        @pl.when(pl.program_id(0) == 0)
        def _():
            o2[...] = jnp.zeros_like(o2)

        o2[...] += dldt

    cshape = jax.ShapeDtypeStruct((n, 1), F32)
    mshape = jax.ShapeDtypeStruct((n, SSM_GROUP), F32)
    return pl.pallas_call(body, name=name, grid=(n // PARAM_ROWS,),
                          in_specs=[col, col, vec, mat, mat, col, col, mat, mat],
                          out_specs=[col, col, vec, mat, mat],
                          out_shape=[cshape, cshape, jax.ShapeDtypeStruct((1, SSM_GROUPS), F32), mshape, mshape],
                          compiler_params=_cp(("arbitrary",)))(are, aim, ldt, bre, bim, dlbr, dlbi, dbbr, dbbi)


SLAB_NC = SLAB_CH // 128


def _s5_specs(L):
    slab = pl.BlockSpec((L, SLAB), lambda s: (0, s))
    wspec = pl.BlockSpec((SLAB_NC, 128, SLAB), lambda s: (s, 0, 0))
    lspec = pl.BlockSpec((SLAB_NC, 1, 128), lambda s: (s, 0, 0))
    sspec = pl.BlockSpec((SLAB_NC, L, 128), lambda s: (s, 0, 0))
    dspec = pl.BlockSpec((1, SLAB), lambda s: (0, s))
    return slab, wspec, lspec, sspec, dspec


def _scan_inplace(sr_ref, si_ref, lr, li, pr_ref, pi_ref, *, reverse):
    NC, L, W = sr_ref.shape
    S = SCAN_SEG
    T = L // S
    lr8 = [jnp.broadcast_to(lr[k], (S, W)) for k in range(NC)]
    li8 = [jnp.broadcast_to(li[k], (S, W)) for k in range(NC)]

    def tiles(first, count):
        return pl.ds(first * S, count * S)

    for k in range(NC):
        pr_ref[k, tiles(T - 1 if reverse else 0, 1), :] = lr8[k]
        pi_ref[k, tiles(T - 1 if reverse else 0, 1), :] = li8[k]
        n = 1
        while n < T:
            have = tiles(T - n, n) if reverse else tiles(0, n)
            new = tiles(T - 2 * n, n) if reverse else tiles(n, n)
            top = tiles(T - n, 1) if reverse else tiles(n - 1, 1)
            ar, ai = pr_ref[k, top, :][None], pi_ref[k, top, :][None]
            hr, hi = pr_ref[k, have, :].reshape(n, S, W), pi_ref[k, have, :].reshape(n, S, W)
            pr_ref[k, new, :] = (hr * ar - hi * ai).reshape(n * S, W)
            pi_ref[k, new, :] = (hr * ai + hi * ar).reshape(n * S, W)
            n *= 2

    def step(i, carry):
        for u in range(SCAN_STEPS):
            jj = i * SCAN_STEPS + u
            rows = pl.ds(pl.multiple_of(((T - 1 - jj) if reverse else jj) * S, S), S)
            out = []
            for k in range(NC):
                sr, si = carry[k]
                nsr = lr8[k] * sr - li8[k] * si + sr_ref[k, rows, :]
                nsi = lr8[k] * si + li8[k] * sr + si_ref[k, rows, :]
                sr_ref[k, rows, :] = nsr
                si_ref[k, rows, :] = nsi
                out.append((nsr, nsi))
            carry = tuple(out)
        return carry

    zero = jnp.zeros((S, W), F32)
    ends = lax.fori_loop(0, T // SCAN_STEPS, step, tuple((zero, zero) for k in range(NC)))
    sub = lax.broadcasted_iota(jnp.int32, (S, W), 0)
    order = range(S - 1, -1, -1) if reverse else range(S)
    for k in range(NC):
        er, ei = ends[k]
        full = tiles(0 if reverse else T - 1, 1)
        ltr = pr_ref[k, full, :][0:1]
        lti = pi_ref[k, full, :][0:1]
        cr = jnp.zeros((1, W), F32)
        ci = jnp.zeros((1, W), F32)
        ctr = jnp.zeros((S, W), F32)
        cti = jnp.zeros((S, W), F32)
        for seg in order:
            ctr = jnp.where(sub == seg, cr, ctr)
            cti = jnp.where(sub == seg, ci, cti)
            cr, ci = (er[seg:seg + 1, :] + ltr * cr - lti * ci, ei[seg:seg + 1, :] + ltr * ci + lti * cr)
        pr = pr_ref[k].reshape(T, S, W)
        pi = pi_ref[k].reshape(T, S, W)
        sr_ref[k] += (pr * ctr[None] - pi * cti[None]).reshape(L, W)
        si_ref[k] += (pr * cti[None] + pi * ctr[None]).reshape(L, W)


def _time_interleave(a):
    L, W = a.shape
    return a.reshape(SCAN_SEG, L // SCAN_SEG, W).transpose(1, 0, 2).reshape(L, W)


def _time_deinterleave(a):
    L, W = a.shape
    return a.reshape(L // SCAN_SEG, SCAN_SEG, W).transpose(1, 0, 2).reshape(L, W)


def _s5_fwd(u, btr, bti, cbr, cbi, lbr, lbi, dvec, *, name, after=None):
    L = u.shape[0]
    extra = [] if after is None else [after]

    def body(u_ref, btr_ref, bti_ref, cbr_ref, cbi_ref, lr_ref, li_ref, d_ref, *rest):
        ys_ref, sr_ref, si_ref, pr_ref, pi_ref = rest[-5:]
        u = u_ref[...]
        for k in range(SLAB_NC):
            sr_ref[k] = _dot(u, btr_ref[k], "nt")
            si_ref[k] = _dot(u, bti_ref[k], "nt")
        _scan_inplace(sr_ref, si_ref, lr_ref[...], li_ref[...], pr_ref, pi_ref, reverse=False)
        ys = d_ref[...] * u
        for k in range(SLAB_NC):
            ys = ys + _dot(sr_ref[k], cbr_ref[k], "nn") - _dot(si_ref[k], cbi_ref[k], "nn")
        ys_ref[...] = ys

    slab, wspec, lspec, sspec, dspec = _s5_specs(L)
    sshape = jax.ShapeDtypeStruct((N_SLAB * SLAB_NC, L, 128), F32)
    return pl.pallas_call(
        body, name=name, grid=(N_SLAB,),
        in_specs=[slab, wspec, wspec, wspec, wspec, lspec, lspec, dspec] + [pl.BlockSpec(memory_space=pl.ANY)] * len(extra),
        out_specs=[slab, sspec, sspec],
        out_shape=[jax.ShapeDtypeStruct((L, SSM_WIDTH), F32), sshape, sshape],
        scratch_shapes=[pltpu.VMEM((SLAB_NC, L, 128), F32), pltpu.VMEM((SLAB_NC, L, 128), F32)],
        compiler_params=_cp(("parallel",), 56),
    )(u, btr, bti, cbr, cbi, lbr, lbi, dvec, *extra)


def _s5_bwd(dys, u, sr, si, btr, bti, cbr, cbi, lbr, lbi, dvec, *, name, after=None):
    L = u.shape[0]
    S = SCAN_SEG
    extra = [] if after is None else [after]

    def body(dys_ref, u_ref, sr_ref, si_ref, btr_ref, bti_ref, cbr_ref, cbi_ref, lr_ref, li_ref, d_ref, *rest):
        (du_ref, dbtr_ref, dbti_ref, dcbr_ref, dcbi_ref, dlr_ref, dli_ref, dd_ref,
         ar_ref, ai_ref, pr_ref, pi_ref) = rest[-12:]
        dys = dys_ref[...]
        u = u_ref[...]
        for k in range(SLAB_NC):
            ar_ref[k] = _dot(dys, cbr_ref[k], "nt")
            ai_ref[k] = -_dot(dys, cbi_ref[k], "nt")
        _scan_inplace(ar_ref, ai_ref, lr_ref[...], -li_ref[...], pr_ref, pi_ref, reverse=True)
        head = lax.broadcasted_iota(jnp.int32, (L, 1), 0) < S
        sub0 = lax.broadcasted_iota(jnp.int32, (S, 1), 0) == 0

        def prev_state(s):
            up = pltpu.roll(s, S, 0)
            return jnp.where(head, 0.0, up), jnp.where(sub0, 0.0, pltpu.roll(up[0:S], 1, 0))

        du = d_ref[...] * dys
        for k in range(SLAB_NC):
            a_re = ar_ref[k]
            a_im = ai_ref[k]
            du = du + _dot(a_re, btr_ref[k], "nn") + _dot(a_im, bti_ref[k], "nn")
            dbtr_ref[k] = _dot(a_re, u, "tn")
            dbti_ref[k] = _dot(a_im, u, "tn")
            s_re = sr_ref[k]
            s_im = si_ref[k]
            dcbr_ref[k] = _dot(s_re, dys, "tn")
            dcbi_ref[k] = -_dot(s_im, dys, "tn")
            p_re, q_re = prev_state(s_re)
            p_im, q_im = prev_state(s_im)
            b_re, b_im = a_re[0:S], a_im[0:S]
            dlr_ref[k] = (jnp.sum(p_re * a_re + p_im * a_im, axis=0, keepdims=True)
                          + jnp.sum(q_re * b_re + q_im * b_im, axis=0, keepdims=True))
            dli_ref[k] = (jnp.sum(p_re * a_im - p_im * a_re, axis=0, keepdims=True)
                          + jnp.sum(q_re * b_im - q_im * b_re, axis=0, keepdims=True))
        du_ref[...] = du
        dd_ref[...] = jnp.sum(dys * u, axis=0, keepdims=True)

    slab, wspec, lspec, sspec, dspec = _s5_specs(L)
    wshape = jax.ShapeDtypeStruct((N_SLAB * SLAB_NC, 128, SLAB), F32)
    lshape = jax.ShapeDtypeStruct((N_SLAB * SLAB_NC, 1, 128), F32)
    return pl.pallas_call(
        body, name=name, grid=(N_SLAB,),
        in_specs=([slab, slab, sspec, sspec, wspec, wspec, wspec, wspec, lspec, lspec, dspec]
                  + [pl.BlockSpec(memory_space=pl.ANY)] * len(extra)),
        out_specs=[slab, wspec, wspec, wspec, wspec, lspec, lspec, dspec],
        out_shape=[jax.ShapeDtypeStruct((L, SSM_WIDTH), F32), wshape, wshape, wshape, wshape, lshape, lshape,
                   jax.ShapeDtypeStruct((1, SSM_WIDTH), F32)],
        scratch_shapes=[pltpu.VMEM((SLAB_NC, L, 128), F32)] * 4,
        compiler_params=_cp(("parallel",), 56),
    )(dys, u, sr, si, btr, bti, cbr, cbi, lbr, lbi, dvec, *extra)


_SLAB_MASK = (np.arange(SLAB_CH)[:, None] // SSM_STATE == np.arange(SLAB)[None, :] // SSM_GROUP)


def _expand_bd(x):
    t = jnp.tile(x.reshape(N_SLAB, SLAB_CH, SSM_GROUP), (1, 1, SLAB // SSM_GROUP))
    return jnp.where(_SLAB_MASK[None], t, 0.0).astype(MXU).reshape(N_SLAB * SLAB_NC, 128, SLAB)


def _contract_bd(dx):
    t = jnp.where(_SLAB_MASK[None], dx.reshape(N_SLAB, SLAB_CH, SLAB), 0.0)
    return jnp.sum(t.reshape(N_SLAB, SLAB_CH, SLAB // SSM_GROUP, SSM_GROUP), axis=2).reshape(SSM_CH, SSM_GROUP)


def _glu_ew(ys, zlin, za):
    a1 = jax.nn.gelu(ys)
    return a1 * jax.nn.sigmoid(zlin) * _silu(za)


def _glu_fwd(ys, main, gw, gb, *, name, tm=256):
    L = ys.shape[0]
    tm = min(tm, L)
    W = SSM_WIDTH

    def body(ys_ref, za_ref, gw_ref, gb_ref, ya_ref):
        ys = ys_ref[...]
        a1 = jax.nn.gelu(ys)
        zlin = _dot(a1, gw_ref[...], "nn") + gb_ref[...]
        ya_ref[...] = _glu_ew(ys, zlin, za_ref[...]).astype(MXU)

    return pl.pallas_call(
        body, name=name, grid=(L // tm,),
        in_specs=[pl.BlockSpec((tm, W), lambda i: (i, 0)), pl.BlockSpec((tm, W), lambda i: (i, 1)),
                  pl.BlockSpec((W, W), lambda i: (0, 0)), pl.BlockSpec((1, W), lambda i: (0, 0))],
        out_specs=pl.BlockSpec((tm, W), lambda i: (i, 0)),
        out_shape=jax.ShapeDtypeStruct((L, W), MXU),
        compiler_params=_cp(("parallel",)),
    )(ys, main, gw, gb.reshape(1, W))


def _glu_bwd(dya, ys, main, gw, gb, *, name, tm=256):
    L = ys.shape[0]
    tm = min(tm, L)
    W = SSM_WIDTH

    def body(dya_ref, ys_ref, za_ref, gw_ref, gb_ref, dys_ref, dza_ref, a1_ref, dzl_ref, db_ref):
        ys = ys_ref[...]
        a1, gelu_vjp = jax.vjp(jax.nn.gelu, ys)
        zlin = _dot(a1, gw_ref[...], "nn") + gb_ref[...]
        _, vjp = jax.vjp(lambda a, z, za: a * jax.nn.sigmoid(z) * _silu(za), a1, zlin, za_ref[...])
        da1, dzlin, dza = vjp(dya_ref[...].astype(F32))
        da1 = da1 + _dot(dzlin, gw_ref[...], "nt")
        dys_ref[...] = gelu_vjp(da1)[0]
        dza_ref[...] = dza
        a1_ref[...] = a1.astype(MXU)
        dzl_ref[...] = dzlin.astype(MXU)

        @pl.when(pl.program_id(0) == 0)
        def _():
            db_ref[...] = jnp.zeros_like(db_ref)

        db_ref[...] += jnp.sum(dzlin, axis=0, keepdims=True)

    row = pl.BlockSpec((tm, W), lambda i: (i, 0))
    vec = pl.BlockSpec((1, W), lambda i: (0, 0))
    return pl.pallas_call(
        body, name=name, grid=(L // tm,),
        in_specs=[row, row, pl.BlockSpec((tm, W), lambda i: (i, 1)), pl.BlockSpec((W, W), lambda i: (0, 0)), vec],
        out_specs=[row, row, row, row, vec],
        out_shape=[jax.ShapeDtypeStruct((L, W), F32), jax.ShapeDtypeStruct((L, W), F32),
                   jax.ShapeDtypeStruct((L, W), MXU), jax.ShapeDtypeStruct((L, W), MXU),
                   jax.ShapeDtypeStruct((1, W), F32)],
        compiler_params=_cp(("arbitrary",)),
    )(dya, ys, main, gw, gb.reshape(1, W))


def _sg_fn(ub, vb, zb, lnw, lnb, ws, bs):
    u = jax.nn.gelu(ub)
    v = _layernorm(jax.nn.gelu(vb), lnw, lnb)
    r = lax.broadcasted_iota(jnp.int32, (SG_CHUNK, SG_CHUNK), 0)
    c = lax.broadcasted_iota(jnp.int32, (SG_CHUNK, SG_CHUNK), 1)
    tri = r >= c
    outs = []
    for h in range(SG_HEADS):
        wh = jnp.where(tri, ws[h], 0.0)
        outs.append(_mm_nn(wh, v[:, h * 128:(h + 1) * 128]) + bs[h])
    mixed = jnp.concatenate(outs, axis=1)
    return u * mixed * _silu(zb)


def _sg_specs(L):
    W = SSM_WIDTH
    blk = lambda c: pl.BlockSpec((SG_CHUNK, W), lambda i, c=c: (i, c))
    vec = pl.BlockSpec((1, W), lambda i: (0, 0))
    wspec = pl.BlockSpec((SG_HEADS, SG_CHUNK, SG_CHUNK), lambda i: (0, 0, 0))
    bspec = pl.BlockSpec((SG_HEADS, SG_CHUNK, 1), lambda i: (0, 0, 0))
    return blk, vec, wspec, bspec


def _sg_fwd(main, lnw, lnb, sgw, sgb, *, name):
    L = main.shape[0]
    W = SSM_WIDTH
    blk, vec, wspec, bspec = _sg_specs(L)

    def body(ub_ref, vb_ref, zb_ref, lnw_ref, lnb_ref, w_ref, b_ref, yb_ref):
        ws = [w_ref[h] for h in range(SG_HEADS)]
        bs = [b_ref[h] for h in range(SG_HEADS)]
        yb_ref[...] = _sg_fn(ub_ref[...], vb_ref[...], zb_ref[...], lnw_ref[...], lnb_ref[...], ws, bs).astype(MXU)

    return pl.pallas_call(
        body, name=name, grid=(L // SG_CHUNK,),
        in_specs=[blk(2), blk(3), blk(4), vec, vec, wspec, bspec],
        out_specs=pl.BlockSpec((SG_CHUNK, W), lambda i: (i, 0)),
        out_shape=jax.ShapeDtypeStruct((L, W), MXU),
        compiler_params=_cp(("parallel",)),
    )(main, main, main, lnw.reshape(1, W), lnb.reshape(1, W), sgw, sgb.reshape(SG_HEADS, SG_CHUNK, 1))


def _sg_bwd(dyb, main, lnw, lnb, sgw, sgb, *, name):
    L = main.shape[0]
    W = SSM_WIDTH
    blk, vec, wspec, bspec = _sg_specs(L)

    def body(dyb_ref, ub_ref, vb_ref, zb_ref, lnw_ref, lnb_ref, w_ref, b_ref,
             dub_ref, dvb_ref, dzb_ref, dlnw_ref, dlnb_ref, dw_ref, db_ref):
        ws = [w_ref[h] for h in range(SG_HEADS)]
        bs = [b_ref[h] for h in range(SG_HEADS)]
        _, vjp = jax.vjp(_sg_fn, ub_ref[...], vb_ref[...], zb_ref[...], lnw_ref[...], lnb_ref[...], ws, bs)
        dub, dvb, dzb, dlnw, dlnb, dws, dbs = vjp(dyb_ref[...])

        @pl.when(pl.program_id(0) == 0)
        def _():
            dlnw_ref[...] = jnp.zeros_like(dlnw_ref)
            dlnb_ref[...] = jnp.zeros_like(dlnb_ref)
            dw_ref[...] = jnp.zeros_like(dw_ref)
            db_ref[...] = jnp.zeros_like(db_ref)

        dub_ref[...] = dub
        dvb_ref[...] = dvb
        dzb_ref[...] = dzb
        dlnw_ref[...] += dlnw
        dlnb_ref[...] += dlnb
        for h in range(SG_HEADS):
            dw_ref[h] += dws[h]
            db_ref[h] += dbs[h]

    row = pl.BlockSpec((SG_CHUNK, W), lambda i: (i, 0))
    out = jax.ShapeDtypeStruct((L, W), F32)
    return pl.pallas_call(
        body, name=name, grid=(L // SG_CHUNK,),
        in_specs=[row, blk(2), blk(3), blk(4), vec, vec, wspec, bspec],
        out_specs=[row, row, row, vec, vec, wspec, bspec],
        out_shape=[out, out, out, jax.ShapeDtypeStruct((1, W), F32), jax.ShapeDtypeStruct((1, W), F32),
                   jax.ShapeDtypeStruct((SG_HEADS, SG_CHUNK, SG_CHUNK), F32),
                   jax.ShapeDtypeStruct((SG_HEADS, SG_CHUNK, 1), F32)],
        compiler_params=_cp(("arbitrary",)),
    )(dyb, main, main, main, lnw.reshape(1, W), lnb.reshape(1, W), sgw, sgb.reshape(SG_HEADS, SG_CHUNK, 1))


def _rope_tables(L):
    half = ROT_DIM // 2
    inv_freq = ROPE_THETA ** (-jnp.arange(0, ROT_DIM, 2, dtype=F32) / ROT_DIM)
    ang = jnp.arange(L, dtype=F32)[:, None] * inv_freq[None, :]
    cos = jnp.cos(ang)
    sin = jnp.sin(ang)
    ones = jnp.ones((L, HEAD_DIM - ROT_DIM), F32)
    cosf = jnp.concatenate([cos, cos, ones], axis=1)
    sinf = jnp.concatenate([sin, sin, 0.0 * ones], axis=1)
    rot = np.zeros((HEAD_DIM, HEAD_DIM), np.float32)
    for d in range(half):
        rot[d + half, d] = -1.0
        rot[d, d + half] = 1.0
    return cosf, sinf, jnp.asarray(rot)


def _rope(t, cosf, sinf, rot):
    shp = t.shape
    t2 = t.reshape(-1, HEAD_DIM)
    sw = lax.dot_general(t2, rot, _DIMS["nn"], precision=lax.Precision.HIGH, preferred_element_type=F32).reshape(shp)
    return t * cosf + sw * sinf


def _softmax_sink_parts(s, sink):
    m = jnp.maximum(jnp.max(s, axis=-1, keepdims=True), sink)
    e = jnp.exp(s - m)
    es = jnp.exp(sink - m)
    r = 1.0 / (jnp.sum(e, axis=-1, keepdims=True) + es)
    return e * r, es * r


@jax.custom_vjp
def _softmax_sink(s, sink):
    return _softmax_sink_parts(s, sink)[0]


def _softmax_sink_fwd(s, sink):
    p, p_sink = _softmax_sink_parts(s, sink)
    return p, (p, p_sink)


def _softmax_sink_bwd(res, dp):
    p, p_sink = res
    t = jnp.sum(p * dp, axis=-1, keepdims=True)
    return p * (dp - t), -jnp.sum(p_sink * t, axis=1, keepdims=True)


_softmax_sink.defvjp(_softmax_sink_fwd, _softmax_sink_bwd)


def _attn_block_fn(q, kw, vw, sinks, cq, sq, ck, sk, rot, q0, k0):
    nk = kw.shape[1]
    qr = _rope(q, cq, sq, rot)
    kr = _rope(kw, ck, sk, rot)
    qpos = q0 + lax.broadcasted_iota(jnp.int32, (1, ATT_BLOCK, nk), 1)
    kpos = k0 + lax.broadcasted_iota(jnp.int32, (1, ATT_BLOCK, nk), 2)
    diff = qpos - kpos
    allowed = (diff >= 0) & (diff < WINDOW)
    outs = []
    for kh in range(ATT_KV_HEADS):
        qh = qr[kh * GQA_GROUP:(kh + 1) * GQA_GROUP].reshape(GQA_GROUP * ATT_BLOCK, HEAD_DIM)
        s = _mm_nt(qh, kr[kh]).reshape(GQA_GROUP, ATT_BLOCK, nk) * (HEAD_DIM ** -0.5)
        s = jnp.where(allowed, s, NEG_INF)
        p = _softmax_sink(s, sinks[kh * GQA_GROUP:(kh + 1) * GQA_GROUP])
        o = _mm_nn(p.reshape(GQA_GROUP * ATT_BLOCK, nk), vw[kh])
        outs.append(o.reshape(GQA_GROUP, ATT_BLOCK, HEAD_DIM))
    return jnp.concatenate(outs, axis=0)


def _attn_common(L):
    nwin = min(2 * ATT_BLOCK, L)
    qspec = pl.BlockSpec((ATT_HEADS, ATT_BLOCK, HEAD_DIM), lambda n: (0, n, 0))
    kvspec = pl.BlockSpec((ATT_KV_HEADS, L, HEAD_DIM), lambda n: (0, 0, 0))
    sspec = pl.BlockSpec((ATT_HEADS, 1, 1), lambda n: (0, 0, 0))
    tq = pl.BlockSpec((ATT_BLOCK, HEAD_DIM), lambda n: (n, 0))
    tk = pl.BlockSpec((L, HEAD_DIM), lambda n: (0, 0))
    rspec = pl.BlockSpec((HEAD_DIM, HEAD_DIM), lambda n: (0, 0))
    return nwin, qspec, kvspec, sspec, tq, tk, rspec


def _attn_fwd(qh, kh, vh, sinks, cosf, sinf, rot, *, name):
    L = qh.shape[1]
    nwin, qspec, kvspec, sspec, tq, tk, rspec = _attn_common(L)

    def body(q_ref, k_ref, v_ref, s_ref, cq_ref, sq_ref, ck_ref, sk_ref, r_ref, o_ref):
        n = pl.program_id(0)
        k0 = pl.multiple_of(jnp.maximum(n - 1, 0) * ATT_BLOCK, ATT_BLOCK)
        win = pl.ds(k0, nwin)
        o_ref[...] = _attn_block_fn(q_ref[...], k_ref[:, win, :], v_ref[:, win, :], s_ref[...],
                                    cq_ref[...], sq_ref[...], ck_ref[win, :], sk_ref[win, :], r_ref[...],
                                    n * ATT_BLOCK, k0)

    return pl.pallas_call(
        body, name=name, grid=(L // ATT_BLOCK,),
        in_specs=[qspec, kvspec, kvspec, sspec, tq, tq, tk, tk, rspec],
        out_specs=qspec,
        out_shape=jax.ShapeDtypeStruct((ATT_HEADS, L, HEAD_DIM), F32),
        compiler_params=_cp(("parallel",)),
    )(qh, kh, vh, sinks.reshape(ATT_HEADS, 1, 1), cosf, sinf, cosf, sinf, rot)


def _attn_bwd(do, qh, kh, vh, sinks, cosf, sinf, rot, *, name):
    L = qh.shape[1]
    nwin, qspec, kvspec, sspec, tq, tk, rspec = _attn_common(L)

    def body(do_ref, q_ref, k_ref, v_ref, s_ref, cq_ref, sq_ref, ck_ref, sk_ref, r_ref,
             dq_ref, dk_ref, dv_ref, ds_ref):
        n = pl.program_id(0)
        k0 = pl.multiple_of(jnp.maximum(n - 1, 0) * ATT_BLOCK, ATT_BLOCK)
        win = pl.ds(k0, nwin)
        cq, sq, ck, sk, rt = cq_ref[...], sq_ref[...], ck_ref[win, :], sk_ref[win, :], r_ref[...]
        q0 = n * ATT_BLOCK
        _, vjp = jax.vjp(lambda q, kw, vw, s: _attn_block_fn(q, kw, vw, s, cq, sq, ck, sk, rt, q0, k0),
                         q_ref[...], k_ref[:, win, :], v_ref[:, win, :], s_ref[...])
        dq, dkw, dvw, ds = vjp(do_ref[...])

        @pl.when(n == 0)
        def _():
            dk_ref[...] = jnp.zeros_like(dk_ref)
            dv_ref[...] = jnp.zeros_like(dv_ref)
            ds_ref[...] = jnp.zeros_like(ds_ref)

        dq_ref[...] = dq
        dk_ref[:, win, :] += dkw
        dv_ref[:, win, :] += dvw
        ds_ref[...] += ds

    return pl.pallas_call(
        body, name=name, grid=(L // ATT_BLOCK,),
        in_specs=[qspec, qspec, kvspec, kvspec, sspec, tq, tq, tk, tk, rspec],
        out_specs=[qspec, kvspec, kvspec, sspec],
        out_shape=[jax.ShapeDtypeStruct((ATT_HEADS, L, HEAD_DIM), F32),
                   jax.ShapeDtypeStruct((ATT_KV_HEADS, L, HEAD_DIM), F32),
                   jax.ShapeDtypeStruct((ATT_KV_HEADS, L, HEAD_DIM), F32),
                   jax.ShapeDtypeStruct((ATT_HEADS, 1, 1), F32)],
        compiler_params=_cp(("arbitrary",)),
    )(do, qh, kh, vh, sinks.reshape(ATT_HEADS, 1, 1), cosf, sinf, cosf, sinf, rot)


def _to_heads(t, nh):
    L = t.shape[0]
    return t.reshape(L, nh, HEAD_DIM).transpose(1, 0, 2)


def _from_heads(t):
    nh, L, _ = t.shape
    return t.transpose(1, 0, 2).reshape(L, nh * HEAD_DIM)


def _branch_fwd(ya, yb, o2d, zc, gates, wa, wb, wc, *, name, tm=256):
    L = ya.shape[0]
    tm = min(tm, L)
    W, D = SSM_WIDTH, D_MODEL

    def body(ya_ref, yb_ref, o_ref, zc_ref, g0_ref, g1_ref, g2_ref, wa_ref, wb_ref, wc_ref,
             mg_ref, ta_ref, tb_ref, tc_ref, yc_ref):
        yc = (o_ref[...] * _silu(zc_ref[...])).astype(MXU)
        ta = _dot(ya_ref[...], wa_ref[...], "nt")
        tb = _dot(yb_ref[...], wb_ref[...], "nt")
        tc = _dot(yc, wc_ref[...], "nt")
        ta_ref[...] = ta
        tb_ref[...] = tb
        tc_ref[...] = tc
        yc_ref[...] = yc
        mg_ref[...] = (jax.nn.sigmoid(g0_ref[...]) * ta + jax.nn.sigmoid(g1_ref[...]) * tb
                       + jax.nn.sigmoid(g2_ref[...]) * tc).astype(MXU)

    row = pl.BlockSpec((tm, W), lambda i: (i, 0))
    wide = pl.BlockSpec((tm, D), lambda i: (i, 0))
    gate = lambda c: pl.BlockSpec((tm, D), lambda i, c=c: (i, c))
    wspec = pl.BlockSpec((D, W), lambda i: (0, 0))
    return pl.pallas_call(
        body, name=name, grid=(L // tm,),
        in_specs=[row, row, row, row, gate(0), gate(1), gate(2), wspec, wspec, wspec],
        out_specs=[wide, wide, wide, wide, row],
        out_shape=[jax.ShapeDtypeStruct((L, D), MXU), jax.ShapeDtypeStruct((L, D), F32),
                   jax.ShapeDtypeStruct((L, D), F32), jax.ShapeDtypeStruct((L, D), F32),
                   jax.ShapeDtypeStruct((L, W), MXU)],
        compiler_params=_cp(("parallel",), 56),
    )(ya, yb, o2d, zc, gates, gates, gates, wa, wb, wc)


def _branch_bwd(dmg, ta, tb, tc, gates, *, name, tm=256):
    L = dmg.shape[0]
    tm = min(tm, L)
    D = D_MODEL

    def body(dm_ref, ta_ref, tb_ref, tc_ref, g0_ref, g1_ref, g2_ref, da_ref, db_ref, dc_ref, dg_ref):
        dm = dm_ref[...]
        for i, (t_ref, g_ref, d_ref) in enumerate(((ta_ref, g0_ref, da_ref), (tb_ref, g1_ref, db_ref),
                                                   (tc_ref, g2_ref, dc_ref))):
            sg = jax.nn.sigmoid(g_ref[...])
            d_ref[...] = (sg * dm).astype(MXU)
            dg_ref[:, i * D:(i + 1) * D] = (dm * t_ref[...] * sg * (1.0 - sg)).astype(MXU)

    wide = pl.BlockSpec((tm, D), lambda i: (i, 0))
    gate = lambda c: pl.BlockSpec((tm, D), lambda i, c=c: (i, c))
    bf = jax.ShapeDtypeStruct((L, D), MXU)
    return pl.pallas_call(
        body, name=name, grid=(L // tm,),
        in_specs=[wide, wide, wide, wide, gate(0), gate(1), gate(2)],
        out_specs=[wide, wide, wide, pl.BlockSpec((tm, 3 * D), lambda i: (i, 0))],
        out_shape=[bf, bf, bf, jax.ShapeDtypeStruct((L, 3 * D), MXU)],
        compiler_params=_cp(("parallel",), 56),
    )(dmg, ta, tb, tc, gates, gates, gates)


def _gate_c_bwd(dyc, o2d, zc, *, name, tm=256):
    L, W = dyc.shape
    tm = min(tm, L)

    def body(dy_ref, o_ref, z_ref, do_ref, dz_ref):
        _, vjp = jax.vjp(lambda o, z: o * _silu(z), o_ref[...], z_ref[...])
        do, dz = vjp(dy_ref[...])
        do_ref[...] = do
        dz_ref[...] = dz.astype(MXU)

    row = pl.BlockSpec((tm, W), lambda i: (i, 0))
    return pl.pallas_call(body, name=name, grid=(L // tm,), in_specs=[row, row, row], out_specs=[row, row],
                          out_shape=[jax.ShapeDtypeStruct((L, W), F32), jax.ShapeDtypeStruct((L, W), MXU)],
                          compiler_params=_cp(("parallel",)))(dyc, o2d, zc)


def _adamw(w, g, m, v, *, name):
    shape = w.shape
    cols = shape[-1]
    w2, g2, m2, v2 = (t.reshape(-1, cols) for t in (w, g, m, v))
    rows = w2.shape[0]
    tc = 1024 if cols % 1024 == 0 else cols
    lane_cols = -(-tc // 128) * 128
    tr = rows
    while tr % 16 == 0 and tr * lane_cols * 4 > 2 * _MB:
        tr //= 2

    def body(w_ref, g_ref, m_ref, v_ref, d_ref, nm_ref, nv_ref):
        gv = g_ref[...]
        nm = ADAM_B1 * m_ref[...] + (1.0 - ADAM_B1) * gv
        nv = ADAM_B2 * v_ref[...] + (1.0 - ADAM_B2) * jnp.square(gv)
        m_hat = nm / (1.0 - ADAM_B1 ** ADAM_STEP)
        v_hat = nv / (1.0 - ADAM_B2 ** ADAM_STEP)
        d_ref[...] = -ADAM_LR * (m_hat / (jnp.sqrt(v_hat) + ADAM_EPS) + ADAM_WD * w_ref[...])
        nm_ref[...] = nm
        nv_ref[...] = nv

    spec = pl.BlockSpec((tr, tc), lambda i, j: (i, j))
    out = jax.ShapeDtypeStruct((rows, cols), F32)
    d, nm, nv = pl.pallas_call(body, name=name, grid=(rows // tr, cols // tc), in_specs=[spec] * 4,
                               out_specs=[spec] * 3, out_shape=[out, out, out],
                               compiler_params=_cp(("parallel", "parallel")))(w2, g2, m2, v2)
    return d.reshape(shape), nm.reshape(shape), nv.reshape(shape)


def _adamw_layer(w, g, m, v, l, prev, *, name):
    _, rows, cols = w.shape
    tc = 1024 if cols % 1024 == 0 else cols
    tr = rows
    while tr % 16 == 0 and tr * tc * 4 > 2 * _MB:
        tr //= 2

    def body(w_ref, g_ref, m_ref, v_ref, *rest):
        go_ref, d_ref, nm_ref, nv_ref = rest[-4:]
        gv = g_ref[...]
        nm = ADAM_B1 * m_ref[...] + (1.0 - ADAM_B1) * gv
        nv = ADAM_B2 * v_ref[...] + (1.0 - ADAM_B2) * jnp.square(gv)
        m_hat = nm / (1.0 - ADAM_B1 ** ADAM_STEP)
        v_hat = nv / (1.0 - ADAM_B2 ** ADAM_STEP)
        d_ref[...] = -ADAM_LR * (m_hat / (jnp.sqrt(v_hat) + ADAM_EPS) + ADAM_WD * w_ref[...])
        nm_ref[...] = nm
        nv_ref[...] = nv
        go_ref[...] = gv

    lspec = pl.BlockSpec((None, tr, tc), lambda i, j: (l, i, j))
    gspec = pl.BlockSpec((tr, tc), lambda i, j: (i, j))
    out = jax.ShapeDtypeStruct(w.shape, F32)
    extra = [] if prev is None else list(prev)
    return pl.pallas_call(
        body, name=name, grid=(rows // tr, cols // tc),
        in_specs=[lspec, gspec, lspec, lspec] + [_ANY] * len(extra),
        out_specs=[lspec] * 4, out_shape=[out] * 4,
        input_output_aliases={4 + i: i for i in range(len(extra))},
        compiler_params=_cp(("parallel", "parallel")),
    )(w, g, m, v, *extra)


def _prep_layer(p, l):
    are = p["ssm_a_re"][l].reshape(SSM_CH, 1)
    aim = p["ssm_a_im"][l].reshape(SSM_CH, 1)
    ldt = p["ssm_log_dt"][l].reshape(1, SSM_GROUPS)
    bre = p["ssm_b_re"][l].reshape(SSM_CH, SSM_GROUP)
    bim = p["ssm_b_im"][l].reshape(SSM_CH, SSM_GROUP)
    lbr, lbi, bbr, bbi = _s5_params_fwd(are, aim, ldt, bre, bim, name=f"s5_params_fwd_{l}")
    cre = p["ssm_c_re"][l].transpose(0, 2, 1).reshape(SSM_CH, SSM_GROUP)
    cim = p["ssm_c_im"][l].transpose(0, 2, 1).reshape(SSM_CH, SSM_GROUP)
    return dict(raw=(are, aim, ldt, bre, bim),
                lbr=lbr.reshape(N_SLAB * SLAB_NC, 1, 128), lbi=lbi.reshape(N_SLAB * SLAB_NC, 1, 128),
                btr=_expand_bd(bbr), bti=_expand_bd(bbi), cbr=_expand_bd(cre), cbi=_expand_bd(cim),
                dvec=p["ssm_d"][l].reshape(1, SSM_WIDTH))


def _layer_fwd(x, h, p, sp, winT, rest_of, l, tabs, proj_after=None):
    L = x.shape[0]
    cosf, sinf, rot = tabs
    mm = functools.partial(_matmul, h, winT, "nt", tm=L, tn=256, tk=D_MODEL, after=proj_after)
    main = mm(name=f"proj_main_{l}", shape=(L, N_MAIN, D_MODEL))
    zc = mm(name=f"proj_zc_{l}", shape=(L, N_ZC, D_MODEL), b_off=(N_MAIN // 256, 0))
    gates = mm(name=f"proj_gates_{l}", shape=(L, N_GATES, D_MODEL), b_off=((N_MAIN + N_ZC) // 256, 0))
    big, token = rest_of([main, zc, gates])
    ua = _time_interleave(main[:, :SSM_WIDTH])
    ys, sr, si = _s5_fwd(ua, sp["btr"], sp["bti"], sp["cbr"], sp["cbi"], sp["lbr"], sp["lbi"], sp["dvec"],
                         name=f"s5_fwd_{l}", after=token)
    ys = _time_deinterleave(ys)
    ya = _glu_fwd(ys, main, big["glu_w"], p["ssm_glu_b"][l], name=f"glu_fwd_{l}")
    yb = _sg_fwd(main, p["sg_ln_w"][l], p["sg_ln_b"][l], p["sg_w"][l], p["sg_b"][l], name=f"sg_fwd_{l}")
    qh = _to_heads(main[:, 5120:6144], ATT_HEADS)
    kh = _to_heads(main[:, 6144:6272], ATT_KV_HEADS)
    vh = _to_heads(main[:, 6272:6400], ATT_KV_HEADS)
    oh = _attn_fwd(qh, kh, vh, p["attn_sinks"][l], cosf, sinf, rot, name=f"attn_fwd_{l}")
    o2d = _from_heads(oh)
    mg, ta, tb, tc, yc = _branch_fwd(ya, yb, o2d, zc, gates, big["wbaT"], big["wbbT"], big["wbcT"],
                                     name=f"branch_fwd_{l}")
    xn = _matmul(mg, big["w_out"], "nn", name=f"out_fwd_{l}", shape=(L, D_MODEL, D_MODEL), tm=512, tn=512,
                 tk=D_MODEL, add=x)
    saved = dict(x=x, h=h, main=main, zc=zc, gates=gates, ua=ua, ys=ys, sr=sr, si=si, ya=ya, yb=yb, yc=yc, o2d=o2d,
                 qh=qh, kh=kh, vh=vh, mg=mg, ta=ta, tb=tb, tc=tc, sp=sp)
    return xn, saved, big


def _layer_bwd(dxn, s, p, big, l, tabs, early):
    L = dxn.shape[0]
    D, W = D_MODEL, SSM_WIDTH
    cosf, sinf, rot = tabs
    sp = s["sp"]
    g = {}
    dmg = _matmul(dxn, big["w_out"], "nt", name=f"out_bwd_dm_{l}", shape=(L, D, D), tm=512, tn=512, tk=D)
    g["w_out"] = _matmul(s["mg"], dxn, "tn", name=f"out_bwd_dw_{l}", shape=(D, D, L), tm=512, tn=512, tk=L,
                         out_dtype=MXU)
    dta, dtb, dtc, dgates = _branch_bwd(dmg, s["ta"], s["tb"], s["tc"], s["gates"], name=f"branch_bwd_{l}")
    dys_ = {}
    for nm, dt, y, wt in (("a", dta, s["ya"], big["wbaT"]), ("b", dtb, s["yb"], big["wbbT"]),
                          ("c", dtc, s["yc"], big["wbcT"])):
        dys_[nm] = _matmul(dt, wt, "nn", name=f"branch_bwd_dy{nm}_{l}", shape=(L, W, D), tm=512, tn=512, tk=D)
        g["wb" + nm + "T"] = _matmul(dt, y, "tn", name=f"branch_bwd_dw{nm}_{l}", shape=(D, W, L),
                                     tm=512, tn=512, tk=L, out_dtype=MXU)
    do2d, dzc = _gate_c_bwd(dys_["c"], s["o2d"], s["zc"], name=f"gate_c_bwd_{l}")
    dqh, dkh, dvh, dsinks = _attn_bwd(_to_heads(do2d, ATT_HEADS), s["qh"], s["kh"], s["vh"], p["attn_sinks"][l],
                                      cosf, sinf, rot, name=f"attn_bwd_{l}")
    g["attn_sinks"] = dsinks.reshape(ATT_HEADS)
    dub, dvb, dzb, dlnw, dlnb, dsgw, dsgb = _sg_bwd(dys_["b"], s["main"], p["sg_ln_w"][l], p["sg_ln_b"][l],
                                                    p["sg_w"][l], p["sg_b"][l], name=f"sg_bwd_{l}")
    g["sg_ln_w"], g["sg_ln_b"] = dlnw.reshape(W), dlnb.reshape(W)
    g["sg_w"], g["sg_b"] = dsgw, dsgb.reshape(SG_HEADS, SG_CHUNK)
    dys, dza, a1, dzl, dgb = _glu_bwd(dys_["a"], s["ys"], s["main"], big["glu_w"], p["ssm_glu_b"][l],
                                      name=f"glu_bwd_{l}")
    g["ssm_glu_b"] = dgb.reshape(W)
    g["glu_w"] = _matmul(a1, dzl, "tn", name=f"glu_bwd_dw_{l}", shape=(W, W, L), tm=512, tn=512, tk=L, out_dtype=MXU)
    token = early(g)
    dua, dbtr, dbti, dcbr, dcbi, dlr, dli, dd = _s5_bwd(_time_interleave(dys), s["ua"], s["sr"], s["si"], sp["btr"],
                                                        sp["bti"], sp["cbr"], sp["cbi"], sp["lbr"], sp["lbi"],
                                                        sp["dvec"], name=f"s5_bwd_{l}", after=token)
    dua = _time_deinterleave(dua)
    g["ssm_d"] = dd.reshape(W)
    to_c = lambda t: _contract_bd(t).reshape(SSM_GROUPS, SSM_STATE, SSM_GROUP).transpose(0, 2, 1)
    g["ssm_c_re"], g["ssm_c_im"] = to_c(dcbr), to_c(dcbi)
    dare, daim, dldt, dbre, dbim = _s5_params_bwd(*sp["raw"], dlr.reshape(SSM_CH, 1), dli.reshape(SSM_CH, 1),
                                                  _contract_bd(dbtr), _contract_bd(dbti),
                                                  name=f"s5_params_bwd_{l}")
    g["ssm_a_re"] = dare.reshape(SSM_GROUPS, SSM_STATE)
    g["ssm_a_im"] = daim.reshape(SSM_GROUPS, SSM_STATE)
    g["ssm_log_dt"] = dldt.reshape(SSM_GROUPS)
    g["ssm_b_re"] = dbre.reshape(SSM_GROUPS, SSM_STATE, SSM_GROUP)
    g["ssm_b_im"] = dbim.reshape(SSM_GROUPS, SSM_STATE, SSM_GROUP)
    dproj = jnp.concatenate([t.astype(MXU) for t in (dua, dza, dub, dvb, dzb, _from_heads(dqh), _from_heads(dkh),
                                                     _from_heads(dvh), dzc, dgates)], axis=1)
    g["winT"] = _matmul(dproj, s["h"], "tn", name=f"proj_bwd_dw_{l}", shape=(D_IN, D, L), tm=256, tn=D, tk=L,
                        out_dtype=MXU)
    return dproj, g


def _proj_bwd_dh(dproj, winT, l, after):
    return _matmul(dproj, winT, "nn", name=f"proj_bwd_dh_{l}", shape=(dproj.shape[0], D_MODEL, D_IN), tm=512, tn=512,
                   tk=D_IN // 2, after=after)


MESH = pl.DeviceIdType.MESH
_ANY = pl.BlockSpec(memory_space=pl.ANY)
ROW_ALIGN = 16


def _coords():
    return lax.axis_index("x"), lax.axis_index("y"), lax.axis_index("c")


def _gather8(arrs, *, name):
    n = len(arrs)
    rows = [a.shape[0] for a in arrs]
    for r in rows:
        assert r % ROW_ALIGN == 0

    def body(*refs):
        ins, outs = refs[:n], refs[n:2 * n]
        send, recv, lsem = refs[2 * n:]
        x, y, c = _coords()
        me, sibling = (x, y, c), (x, y, 1 - c)
        chips = [(1 - x, y), (x, 1 - y), (1 - x, 1 - y)]

        def blk(a, px, py, pc):
            return outs[a].at[pl.ds(pl.multiple_of((4 * px + 2 * py + pc) * rows[a], ROW_ALIGN), rows[a]), :]

        def own(a):
            return ins[a]

        def copy(a, k, block, to, src=None):
            return pltpu.make_async_remote_copy(
                src_ref=blk(a, *block) if src is None else src, dst_ref=blk(a, *block),
                send_sem=send.at[a, k], recv_sem=recv.at[a, k], device_id=to, device_id_type=MESH)

        mine, first, passed = [], [], []
        for a in range(n):
            mine.append(pltpu.make_async_copy(own(a), blk(a, *me), lsem.at[a]))
            mine[a].start()
            f = [copy(a, 0, me, sibling, src=own(a))]
            f += [copy(a, 1 + j, me, (*chip, c), src=own(a)) for j, chip in enumerate(chips)]
            for cp in f:
                cp.start()
            first.append(f)
        for a in range(n):
            ps = [copy(a, 4 + j, (*chip, c), sibling) for j, chip in enumerate(chips)]
            for j, chip in enumerate(chips):
                copy(a, 1 + j, (*chip, c), me).wait_recv()
                ps[j].start()
            passed.append(ps)
        for a in range(n):
            copy(a, 0, sibling, me).wait_recv()
            for j, chip in enumerate(chips):
                copy(a, 4 + j, (*chip, 1 - c), me).wait_recv()
            for cp in first[a] + passed[a]:
                cp.wait_send()
            mine[a].wait()

    return pl.pallas_call(
        body, name=name,
        in_specs=[_ANY] * n, out_specs=[_ANY] * n,
        out_shape=[jax.ShapeDtypeStruct((8 * r,) + a.shape[1:], a.dtype) for r, a in zip(rows, arrs)],
        scratch_shapes=[pltpu.SemaphoreType.DMA((n, 7)), pltpu.SemaphoreType.DMA((n, 7)), pltpu.SemaphoreType.DMA((n,))],
    )(*arrs)


def _sibling_swap(arrs, *, name):
    n = len(arrs)

    def body(*refs):
        ins, outs = refs[:n], refs[n:2 * n]
        send, recv = refs[2 * n:]
        x, y, c = _coords()
        cps = [pltpu.make_async_remote_copy(src_ref=ins[a].at[:, 1 - c], dst_ref=outs[a], send_sem=send.at[a],
                                            recv_sem=recv.at[a], device_id=(x, y, 1 - c), device_id_type=MESH)
               for a in range(n)]
        for cp in cps:
            cp.start()
        for cp in cps:
            cp.wait_recv()
        for cp in cps:
            cp.wait_send()

    return pl.pallas_call(
        body, name=name, in_specs=[_ANY] * n, out_specs=[_ANY] * n,
        out_shape=[jax.ShapeDtypeStruct((a.shape[0],) + a.shape[2:], a.dtype) for a in arrs],
        scratch_shapes=[pltpu.SemaphoreType.DMA((n,)), pltpu.SemaphoreType.DMA((n,))],
    )(*arrs)


def _col_tile(lead, rows, cols, itemsize=4, cap=4 * _MB):
    tc = cols
    while tc % 256 == 0 and lead * rows * tc * itemsize > cap:
        tc //= 2
    return tc


def _pair_sum(mine, theirs, *, name):
    _, _, rows, cols = mine.shape
    tc = _col_tile(1, rows, cols)
    c = lax.axis_index("c")

    def body(c_ref, a_ref, b_ref, o_ref):
        o_ref[...] = (a_ref[...].astype(F32) + b_ref[...].astype(F32)).astype(MXU)

    return pl.pallas_call(
        body, name=name,
        grid_spec=pltpu.PrefetchScalarGridSpec(
            num_scalar_prefetch=1, grid=(4, cols // tc),
            in_specs=[pl.BlockSpec((None, None, rows, tc), lambda j, i, cr: (j, cr[0], 0, i)),
                      pl.BlockSpec((None, rows, tc), lambda j, i, cr: (j, 0, i))],
            out_specs=pl.BlockSpec((None, rows, tc), lambda j, i, cr: (j, 0, i))),
        out_shape=jax.ShapeDtypeStruct((4, rows, cols), MXU),
        compiler_params=_cp(("parallel", "parallel")),
    )(c.reshape(1).astype(jnp.int32), mine, theirs)


_HBM = pl.BlockSpec(memory_space=pltpu.HBM)
_SEM = pl.BlockSpec(memory_space=pltpu.SEMAPHORE)
_EFFECT = pltpu.SideEffectType.DATAFLOW_SIDE_EFFECTING
N_PEER_CHIPS = 3


def _peer_chips(x, y):
    return [(1 - x, y), (x, 1 - y), (1 - x, 1 - y)]


def _split_start(srcs, lands, src_slot, dst_slot, *, name):
    n = len(srcs)
    ns = n * N_PEER_CHIPS

    def body(*refs):
        src_refs, land_refs = refs[:n], refs[n:2 * n]
        send, recv, token = refs[2 * n:2 * n + ns], refs[2 * n + ns:2 * n + 2 * ns], refs[-1]
        x, y, c = _coords()
        for a in range(n):
            for k, (px, py) in enumerate(_peer_chips(x, y)):
                pltpu.make_async_remote_copy(
                    src_ref=src_refs[a].at[src_slot(x, y, c, px, py)], dst_ref=land_refs[a].at[dst_slot(x, y, c)],
                    send_sem=send[a * N_PEER_CHIPS + k], recv_sem=recv[a * N_PEER_CHIPS + k],
                    device_id=(px, py, c), device_id_type=MESH).start()
        token[...] = jnp.zeros_like(token)

    bufs = list(srcs) + list(lands)
    res = pl.pallas_call(
        body, name=name,
        out_shape=(*[pltpu.SemaphoreType.DMA(())] * (2 * ns), *[pltpu.HBM(b.shape, b.dtype) for b in bufs],
                   jax.ShapeDtypeStruct((8, 128), F32)),
        in_specs=[_HBM] * (2 * n),
        out_specs=(*[_SEM] * (2 * ns), *[_HBM] * (2 * n), pl.BlockSpec(memory_space=pltpu.VMEM)),
        input_output_aliases={i: 2 * ns + i for i in range(2 * n)},
        compiler_params=pltpu.CompilerParams(has_side_effects=_EFFECT),
    )(*[pltpu.with_memory_space_constraint(b, pltpu.HBM) for b in bufs])
    sems = list(res[:2 * ns])
    return sems, list(res[2 * ns:2 * ns + n]), list(res[2 * ns + n:2 * ns + 2 * n]), res[-1]


def _split_wait(sems, srcs, lands, after, *, name):
    n = len(srcs)
    ns = n * N_PEER_CHIPS

    def body(*refs):
        src_refs, land_refs = refs[:n], refs[n:2 * n]
        send, recv = refs[2 * n:2 * n + ns], refs[2 * n + ns:2 * n + 2 * ns]
        x, y, c = _coords()
        for a in range(n):
            for k in range(N_PEER_CHIPS):
                cp = pltpu.make_async_remote_copy(
                    src_ref=src_refs[a].at[0], dst_ref=land_refs[a].at[0], send_sem=send[a * N_PEER_CHIPS + k],
                    recv_sem=recv[a * N_PEER_CHIPS + k], device_id=(x, y, 1 - c), device_id_type=MESH)
                cp.wait_send()
                cp.wait_recv()

    bufs = list(srcs) + list(lands)
    res = pl.pallas_call(
        body, name=name,
        out_shape=tuple(pltpu.HBM(b.shape, b.dtype) for b in bufs),
        in_specs=[_HBM] * (2 * n) + [_SEM] * (2 * ns) + [_ANY] * len(after),
        out_specs=tuple([_HBM] * (2 * n)),
        input_output_aliases={i: i for i in range(2 * n)},
        compiler_params=pltpu.CompilerParams(has_side_effects=_EFFECT),
    )(*bufs, *sems, *after)
    return list(res[:n]), list(res[n:])


def _fill_own(shard2, *, name):
    _, rows, cols = shard2.shape
    tc = _col_tile(1, rows, cols, itemsize=shard2.dtype.itemsize)
    j = 2 * lax.axis_index("x") + lax.axis_index("y")

    def body(j_ref, s_ref, o_ref):
        o_ref[...] = s_ref[...]

    return pl.pallas_call(
        body, name=name,
        grid_spec=pltpu.PrefetchScalarGridSpec(
            num_scalar_prefetch=1, grid=(2, cols // tc),
            in_specs=[pl.BlockSpec((None, rows, tc), lambda h, i, jr: (h, 0, i))],
            out_specs=pl.BlockSpec((None, rows, tc), lambda h, i, jr: (2 * jr[0] + h, 0, i))),
        out_shape=jax.ShapeDtypeStruct((8, rows, cols), shard2.dtype),
        compiler_params=_cp(("parallel", "parallel")),
    )(j.reshape(1).astype(jnp.int32), shard2)


def _pass_to_sibling(lands, *, name):
    n = len(lands)

    def body(*refs):
        outs = refs[n:2 * n]
        send, recv = refs[2 * n:]
        x, y, c = _coords()
        cps = []
        for a in range(n):
            for k, (px, py) in enumerate(_peer_chips(x, y)):
                slot = 4 * px + 2 * py + c
                cps.append(pltpu.make_async_remote_copy(
                    src_ref=outs[a].at[slot], dst_ref=outs[a].at[slot], send_sem=send.at[a, k], recv_sem=recv.at[a, k],
                    device_id=(x, y, 1 - c), device_id_type=MESH))
        for cp in cps:
            cp.start()
        for cp in cps:
            cp.wait_recv()
        for cp in cps:
            cp.wait_send()

    return pl.pallas_call(
        body, name=name, in_specs=[_ANY] * n, out_specs=[_ANY] * n,
        out_shape=[jax.ShapeDtypeStruct(b.shape, b.dtype) for b in lands],
        input_output_aliases={a: a for a in range(n)},
        scratch_shapes=[pltpu.SemaphoreType.DMA((n, N_PEER_CHIPS)), pltpu.SemaphoreType.DMA((n, N_PEER_CHIPS))],
    )(*lands)


def _sum_parts(parts, got, *, name):
    _, rows, cols = parts.shape
    tc = _col_tile(4, rows, cols, itemsize=parts.dtype.itemsize)
    x, y, c = _coords()
    idx = jnp.stack([2 * x + y, 2 * (1 - x) + y, 2 * x + (1 - y), 2 * (1 - x) + (1 - y), c]).astype(jnp.int32)

    def body(i_ref, p_ref, g0_ref, g1_ref, g2_ref, o_ref):
        o_ref[...] = ((p_ref[...].astype(F32) + g0_ref[...].astype(F32)) + g1_ref[...].astype(F32)) + g2_ref[...].astype(F32)

    slot = lambda s: pl.BlockSpec((None, rows, tc), lambda i, ir, s=s: (ir[s], 0, i))
    return pl.pallas_call(
        body, name=name,
        grid_spec=pltpu.PrefetchScalarGridSpec(
            num_scalar_prefetch=1, grid=(cols // tc,),
            in_specs=[slot(0), slot(1), slot(2), slot(3)],
            out_specs=pl.BlockSpec((None, rows, tc), lambda i, ir: (ir[4], 0, i))),
        out_shape=jax.ShapeDtypeStruct((2, rows, cols), F32),
        compiler_params=_cp(("parallel",)),
    )(idx, parts, got, got, got)


def _sum_slots(t, *, name):
    S, rows, cols = t.shape
    tc = _col_tile(S, rows, cols)

    def body(t_ref, o_ref):
        acc = t_ref[0].astype(F32)
        for s in range(1, S):
            acc = acc + t_ref[s].astype(F32)
        o_ref[...] = acc

    return pl.pallas_call(
        body, name=name, grid=(cols // tc,),
        in_specs=[pl.BlockSpec((S, rows, tc), lambda i: (0, 0, i))],
        out_specs=pl.BlockSpec((rows, tc), lambda i: (0, i)),
        out_shape=jax.ShapeDtypeStruct((rows, cols), F32),
        compiler_params=_cp(("parallel",)),
    )(t)


def _halves_join(bufs, *, name):
    n = len(bufs)

    def body(*refs):
        outs = refs[n:2 * n]
        send, recv = refs[2 * n:]
        x, y, c = _coords()
        cps = [pltpu.make_async_remote_copy(src_ref=outs[a].at[c], dst_ref=outs[a].at[c], send_sem=send.at[a],
                                            recv_sem=recv.at[a], device_id=(x, y, 1 - c), device_id_type=MESH)
               for a in range(n)]
        for cp in cps:
            cp.start()
        for cp in cps:
            cp.wait_recv()
        for cp in cps:
            cp.wait_send()

    return pl.pallas_call(
        body, name=name, in_specs=[_ANY] * n, out_specs=[_ANY] * n,
        out_shape=[jax.ShapeDtypeStruct(b.shape, b.dtype) for b in bufs],
        input_output_aliases={a: a for a in range(n)},
        scratch_shapes=[pltpu.SemaphoreType.DMA((n,)), pltpu.SemaphoreType.DMA((n,))],
    )(*bufs)


def _swap_start(srcs, *, name):
    n = len(srcs)
    lands = [lax.empty((s.shape[0],) + s.shape[2:], s.dtype) for s in srcs]

    def body(*refs):
        src_refs, land_refs = refs[:n], refs[n:2 * n]
        send, recv, token = refs[2 * n:3 * n], refs[3 * n:4 * n], refs[-1]
        x, y, c = _coords()
        for a in range(n):
            pltpu.make_async_remote_copy(src_ref=src_refs[a].at[:, 1 - c], dst_ref=land_refs[a], send_sem=send[a],
                                         recv_sem=recv[a], device_id=(x, y, 1 - c), device_id_type=MESH).start()
        token[...] = jnp.zeros_like(token)

    bufs = list(srcs) + lands
    res = pl.pallas_call(
        body, name=name,
        out_shape=(*[pltpu.SemaphoreType.DMA(())] * (2 * n), *[pltpu.HBM(b.shape, b.dtype) for b in bufs],
                   jax.ShapeDtypeStruct((8, 128), F32)),
        in_specs=[_HBM] * (2 * n),
        out_specs=(*[_SEM] * (2 * n), *[_HBM] * (2 * n), pl.BlockSpec(memory_space=pltpu.VMEM)),
        input_output_aliases={i: 2 * n + i for i in range(2 * n)},
        compiler_params=pltpu.CompilerParams(has_side_effects=_EFFECT),
    )(*[pltpu.with_memory_space_constraint(b, pltpu.HBM) for b in bufs])
    return list(res[:2 * n]), list(res[2 * n:3 * n]), list(res[3 * n:4 * n]), res[-1]


def _swap_wait(sems, srcs, lands, after, *, name):
    n = len(srcs)

    def body(*refs):
        src_refs, land_refs = refs[:n], refs[n:2 * n]
        send, recv = refs[2 * n:3 * n], refs[3 * n:4 * n]
        x, y, c = _coords()
        for a in range(n):
            cp = pltpu.make_async_remote_copy(
                src_ref=src_refs[a].at[:, 0], dst_ref=land_refs[a], send_sem=send[a], recv_sem=recv[a],
                device_id=(x, y, 1 - c), device_id_type=MESH)
            cp.wait_send()
            cp.wait_recv()

    bufs = list(srcs) + list(lands)
    res = pl.pallas_call(
        body, name=name,
        out_shape=tuple(pltpu.HBM(b.shape, b.dtype) for b in bufs),
        in_specs=[_HBM] * (2 * n) + [_SEM] * (2 * n) + [_ANY] * len(after),
        out_specs=tuple([_HBM] * (2 * n)),
        input_output_aliases={i: i for i in range(2 * n)},
        compiler_params=pltpu.CompilerParams(has_side_effects=_EFFECT),
    )(*bufs, *sems, *after)
    return list(res[:n]), list(res[n:])


def _grad_views(grads):
    return [g.reshape(4, 2, g.shape[0] // 8, g.shape[1]) for g in grads]


def _scatter_begin(views, theirs, *, tag):
    parts = [_pair_sum(v, t, name=f"rs_pair_{tag}_{i}") for i, (v, t) in enumerate(zip(views, theirs))]
    got = [lax.empty(p.shape, p.dtype) for p in parts]
    sems, parts, got, token = _split_start(
        parts, got, lambda x, y, c, px, py: 2 * px + py, lambda x, y, c: 2 * x + y, name=f"rs_start_{tag}")
    return (sems, parts, got), token


def _reduce_scatter_begin(grads, *, tag):
    views = _grad_views(grads)
    theirs = _sibling_swap(views, name=f"rs_swap_{tag}")
    return _scatter_begin(views, theirs, tag=tag)


def _reduce_scatter_end(state, after, *, tag):
    sems, parts, got = state
    parts, got = _split_wait(sems, parts, got, after, name=f"rs_wait_{tag}")
    halves = [_sum_parts(p, t, name=f"rs_sum_{tag}_{i}") for i, (p, t) in enumerate(zip(parts, got))]
    joined = _halves_join(halves, name=f"rs_join_{tag}")
    return [j.reshape(2 * j.shape[1], j.shape[2]) for j in joined]


_SMALL = ("norm_w", "ssm_a_re", "ssm_a_im", "ssm_log_dt", "ssm_b_re", "ssm_b_im", "ssm_c_re", "ssm_c_im", "ssm_d",
          "ssm_glu_b", "sg_ln_w", "sg_ln_b", "sg_w", "sg_b", "attn_sinks", "final_norm_w")
_BIG = ("w_in", "ssm_glu_w", "w_branch_a", "w_branch_b", "w_branch_c", "w_out")
_WEIGHTS = ("norm_w", "w_in", "ssm_a_re", "ssm_a_im", "ssm_log_dt", "ssm_b_re", "ssm_b_im", "ssm_c_re", "ssm_c_im",
            "ssm_d", "ssm_glu_w", "ssm_glu_b", "sg_ln_w", "sg_ln_b", "sg_w", "sg_b", "attn_sinks", "w_branch_a",
            "w_branch_b", "w_branch_c", "w_out", "final_norm_w")
_PACK_COLS = 1024
_PACK_ALIGN = 8 * ROW_ALIGN * _PACK_COLS


def _slice_exchange(buf, *, name):
    def body(in_ref, out_ref, send, recv, lsem):
        x, y, c = _coords()
        me = 4 * x + 2 * y + c
        own = pltpu.make_async_copy(in_ref.at[me], out_ref.at[me], lsem)
        own.start()
        cps = []
        for k in range(1, 8):
            px, py, pc = x ^ (k >> 2), y ^ ((k >> 1) & 1), c ^ (k & 1)
            cps.append(pltpu.make_async_remote_copy(
                src_ref=in_ref.at[4 * px + 2 * py + pc], dst_ref=out_ref.at[me], send_sem=send.at[k - 1],
                recv_sem=recv.at[k - 1], device_id=(px, py, pc), device_id_type=MESH))
        for cp in cps:
            cp.start()
        for cp in cps:
            cp.wait_recv()
        for cp in cps:
            cp.wait_send()
        own.wait()

    return pl.pallas_call(
        body, name=name, in_specs=[_ANY], out_specs=_ANY,
        out_shape=jax.ShapeDtypeStruct(buf.shape, buf.dtype),
        scratch_shapes=[pltpu.SemaphoreType.DMA((7,)), pltpu.SemaphoreType.DMA((7,)), pltpu.SemaphoreType.DMA],
    )(buf)


def _allreduce_small(packed):
    rows, cols = packed.shape
    got = _slice_exchange(packed.reshape(8, rows // 8, cols), name="small_grads_exchange")
    mine = _sum_slots(got, name="small_grads_sum")
    return _gather8([mine], name="small_grads_gather")[0]


def _pack(ts):
    flat = jnp.concatenate([t.reshape(-1) for t in ts])
    pad = (-flat.shape[0]) % _PACK_ALIGN
    return jnp.pad(flat, (0, pad)).reshape(-1, _PACK_COLS)


def _unpack(buf, like):
    flat = buf.reshape(-1)
    out, pos = [], 0
    for t in like:
        out.append(flat[pos:pos + t.size].reshape(t.shape))
        pos += t.size
    return out


def kernel(x, norm_w, w_in, ssm_a_re, ssm_a_im, ssm_log_dt, ssm_b_re, ssm_b_im, ssm_c_re, ssm_c_im, ssm_d, ssm_glu_w, ssm_glu_b, sg_ln_w, sg_ln_b, sg_w, sg_b, attn_sinks, w_branch_a, w_branch_b, w_branch_c, w_out, final_norm_w, loss_target, m_norm_w, m_w_in, m_ssm_a_re, m_ssm_a_im, m_ssm_log_dt, m_ssm_b_re, m_ssm_b_im, m_ssm_c_re, m_ssm_c_im, m_ssm_d, m_ssm_glu_w, m_ssm_glu_b, m_sg_ln_w, m_sg_ln_b, m_sg_w, m_sg_b, m_attn_sinks, m_w_branch_a, m_w_branch_b, m_w_branch_c, m_w_out, m_final_norm_w, v_norm_w, v_w_in, v_ssm_a_re, v_ssm_a_im, v_ssm_log_dt, v_ssm_b_re, v_ssm_b_im, v_ssm_c_re, v_ssm_c_im, v_ssm_d, v_ssm_glu_w, v_ssm_glu_b, v_sg_ln_w, v_sg_ln_b, v_sg_w, v_sg_b, v_attn_sinks, v_w_branch_a, v_w_branch_b, v_w_branch_c, v_w_out, v_final_norm_w):
    w = dict(norm_w=norm_w, w_in=w_in, ssm_a_re=ssm_a_re, ssm_a_im=ssm_a_im, ssm_log_dt=ssm_log_dt, ssm_b_re=ssm_b_re,
             ssm_b_im=ssm_b_im, ssm_c_re=ssm_c_re, ssm_c_im=ssm_c_im, ssm_d=ssm_d, ssm_glu_w=ssm_glu_w,
             ssm_glu_b=ssm_glu_b, sg_ln_w=sg_ln_w, sg_ln_b=sg_ln_b, sg_w=sg_w, sg_b=sg_b, attn_sinks=attn_sinks,
             w_branch_a=w_branch_a, w_branch_b=w_branch_b, w_branch_c=w_branch_c, w_out=w_out,
             final_norm_w=final_norm_w)
    m = dict(norm_w=m_norm_w, w_in=m_w_in, ssm_a_re=m_ssm_a_re, ssm_a_im=m_ssm_a_im, ssm_log_dt=m_ssm_log_dt,
             ssm_b_re=m_ssm_b_re, ssm_b_im=m_ssm_b_im, ssm_c_re=m_ssm_c_re, ssm_c_im=m_ssm_c_im, ssm_d=m_ssm_d,
             ssm_glu_w=m_ssm_glu_w, ssm_glu_b=m_ssm_glu_b, sg_ln_w=m_sg_ln_w, sg_ln_b=m_sg_ln_b, sg_w=m_sg_w,
             sg_b=m_sg_b, attn_sinks=m_attn_sinks, w_branch_a=m_w_branch_a, w_branch_b=m_w_branch_b,
             w_branch_c=m_w_branch_c, w_out=m_w_out, final_norm_w=m_final_norm_w)
    v = dict(norm_w=v_norm_w, w_in=v_w_in, ssm_a_re=v_ssm_a_re, ssm_a_im=v_ssm_a_im, ssm_log_dt=v_ssm_log_dt,
             ssm_b_re=v_ssm_b_re, ssm_b_im=v_ssm_b_im, ssm_c_re=v_ssm_c_re, ssm_c_im=v_ssm_c_im, ssm_d=v_ssm_d,
             ssm_glu_w=v_ssm_glu_w, ssm_glu_b=v_ssm_glu_b, sg_ln_w=v_sg_ln_w, sg_ln_b=v_sg_ln_b, sg_w=v_sg_w,
             sg_b=v_sg_b, attn_sinks=v_attn_sinks, w_branch_a=v_w_branch_a, w_branch_b=v_w_branch_b,
             w_branch_c=v_w_branch_c, w_out=v_w_out, final_norm_w=v_final_norm_w)

    big_names = ("winT", "glu_w", "wbaT", "wbbT", "wbcT", "w_out")
    L = x.shape[1]
    tabs = _rope_tables(L)
    p = {k: w[k] for k in _SMALL}

    def shard_halves(ws):
        w_in_l, glu_l, wba_l, wbb_l, wbc_l, w_out_l = ws
        return [t.reshape(2, t.shape[0] // 2, t.shape[1]) for t in
                (w_in_l.T.astype(MXU), glu_l.astype(MXU), wba_l.T.astype(MXU), wbb_l.T.astype(MXU),
                 wbc_l.T.astype(MXU), w_out_l.astype(MXU))]

    my_half = lambda x_, y_, c_, px, py: c_
    my_block = lambda x_, y_, c_: 4 * x_ + 2 * y_ + c_
    rows_of = lambda lands: [t.reshape(8 * t.shape[1], t.shape[2]) for t in lands]
    fill = lambda l, hs: [_fill_own(s, name=f"gather_fill_{l}_{i}") for i, s in enumerate(hs)]
    halves, lands = [None] * DEPTH, [None] * DEPTH
    saved = [None] * DEPTH

    halves[0] = shard_halves([w[k][0] for k in _BIG])
    lands[0] = fill(0, halves[0])
    sems_a, src_a, land_a, token_a = _split_start(halves[0][:1], lands[0][:1], my_half, my_block,
                                                  name="gather_start_0a")
    halves[1] = shard_halves(lax.optimization_barrier(([w[k][1] for k in _BIG], token_a))[0])
    lands[1] = fill(1, halves[1])
    sp = [_prep_layer(p, l) for l in range(DEPTH)]
    h0 = _rms_fwd(x[0], p["norm_w"][0], name="rms_fwd_0")
    _, land_a = _split_wait(sems_a, src_a, land_a, [h0, sp[0]["btr"], sp[1]["btr"]] + lands[1], name="gather_wait_0a")
    land_a = _pass_to_sibling(land_a, name="gather_pass_0a")
    land_a, src_b = lax.optimization_barrier((land_a, halves[0][1:]))
    sems_b, src_b, land_b, token_b = _split_start(src_b, lands[0][1:], my_half, my_block, name="gather_start_0b")
    split1 = {}

    def rest0(t):
        _, got = _split_wait(sems_b, src_b, land_b, t, name="gather_wait_0b")
        got = _pass_to_sibling(got, name="gather_pass_0b")
        got, src1 = lax.optimization_barrier((got, halves[1]))
        split1["sems"], split1["src"], split1["land"], token1 = _split_start(src1, lands[1], my_half, my_block,
                                                                             name="gather_start_1")
        return dict(zip(big_names, rows_of(land_a + got))), token1

    x1, saved[0], big0 = _layer_fwd(x[0], h0, p, sp[0], rows_of(land_a)[0], rest0, 0, tabs, proj_after=token_b)
    _, lands1 = _split_wait(split1["sems"], split1["src"], split1["land"], [x1], name="gather_wait_1")
    big1 = dict(zip(big_names, rows_of(_pass_to_sibling(lands1, name="gather_pass_1"))))
    bigs = [big0, big1]
    h1 = _rms_fwd(x1, p["norm_w"][1], name="rms_fwd_1")
    x2, saved[1], _ = _layer_fwd(x1, h1, p, sp[1], big1["winT"], lambda t: (big1, None), 1, tabs)
    loss, dx, dfw = _final_loss(x2, p["final_norm_w"], loss_target[0], name="final_loss")

    grads = [None] * DEPTH
    rs = {}

    def early(l):
        def begin(g):
            rs[f"{l}a"], token_a = _reduce_scatter_begin([g[k] for k in big_names[1:]], tag=f"{l}a")
            return token_a
        return begin

    def late(l, dproj, dx):
        sems, views, lands, token_b = _swap_start(_grad_views([grads[l]["winT"]]), name=f"rs_swap_start_{l}b")
        dh = _proj_bwd_dh(dproj, bigs[l]["winT"], l, token_b)
        views, theirs = _swap_wait(sems, views, lands, [dh], name=f"rs_swap_wait_{l}b")
        rs[f"{l}b"], _ = _scatter_begin(views, theirs, tag=f"{l}b")
        return _rms_bwd(saved[l]["x"], p["norm_w"][l], dh, dx, name=f"rms_bwd_{l}")

    def reduced(l, after):
        return _reduce_scatter_end(rs[f"{l}b"], after, tag=f"{l}b") + _reduce_scatter_end(rs[f"{l}a"], after, tag=f"{l}a")

    dproj, grads[1] = _layer_bwd(dx, saved[1], p, bigs[1], 1, tabs, early(1))
    dx, grads[1]["norm_w"] = late(1, dproj, dx)
    dproj, grads[0] = _layer_bwd(dx, saved[0], p, bigs[0], 0, tabs, early(0))
    red1 = reduced(1, [dproj])
    dx, grads[0]["norm_w"] = late(0, dproj, dx)

    tr = lambda t: t.transpose(0, 2, 1)
    view = {k: (tr if k == "w_in" else (lambda t: t)) for k in _BIG}
    shard_grads = lambda red: dict(zip(_BIG, (red[0], red[1], red[2].T, red[3].T, red[4].T, red[5])))
    outs = {k: None for k in _BIG}

    def adamw_big(l, red):
        for k, g in shard_grads(red).items():
            outs[k] = _adamw_layer(view[k](w[k]), g, view[k](m[k]), view[k](v[k]), l, outs[k], name=f"adamw_{k}_{l}")

    adamw_big(1, red1)

    small_like = [w[k] for k in _SMALL]
    gs = [jnp.stack([grads[l][k] for l in range(DEPTH)]) if k != "final_norm_w" else dfw for k in _SMALL]
    gsum = _allreduce_small(_pack(gs + [loss.reshape(1)]))
    adamw_big(0, reduced(0, [gsum] + [outs[k][0] for k in _BIG]))

    gfull, delta, new_m, new_v = {}, {}, {}, {}
    for k in _BIG:
        gfull[k], delta[k], new_m[k], new_v[k] = (view[k](t) for t in outs[k])
    *small_sums, loss = _unpack(gsum, small_like + [loss])
    for k, t in zip(_SMALL, small_sums):
        gfull[k] = t
        delta[k], new_m[k], new_v[k] = _adamw(w[k], t, m[k], v[k], name=f"adamw_{k}")

    return (loss, dx[None], *[gfull[k] for k in _WEIGHTS], *[delta[k] for k in _WEIGHTS],
            *[new_m[k] for k in _WEIGHTS], *[new_v[k] for k in _WEIGHTS])
```

```python
import functools
import math

import numpy as np
import jax
import jax.numpy as jnp
from jax import lax
from jax.experimental import pallas as pl
from jax.experimental.pallas import tpu as pltpu

F32 = jnp.float32
MXU = jnp.bfloat16
HIGHEST = lax.Precision.HIGHEST

D_MODEL = 2048
DEPTH = 2
EPS = 1e-6
NEG_INF = -1e30
SSM_WIDTH = 1024
SSM_GROUP = 16
SSM_GROUPS = 64
SSM_STATE = 64
SSM_CH = SSM_GROUPS * SSM_STATE
SLAB = 128
SLAB_CH = (SLAB // SSM_GROUP) * SSM_STATE
N_SLAB = SSM_WIDTH // SLAB
SCAN_SEG = 8
SCAN_STEPS = 4
SG_HEADS = 8
SG_CHUNK = 128
HEAD_DIM = 64
ATT_HEADS = 16
ATT_KV_HEADS = 2
GQA_GROUP = 8
ATT_BLOCK = 128
WINDOW = 128
ROT_DIM = 16
ROPE_THETA = 500000.0
N_MAIN = 6400
N_ZC = 1024
N_GATES = 6144
D_IN = N_MAIN + N_ZC + N_GATES

ADAM_LR = 0.001
ADAM_B1 = 0.9
ADAM_B2 = 0.999
ADAM_EPS = 1e-08
ADAM_WD = 0.01
ADAM_STEP = 10

_DIMS = {"nn": (((1,), (0,)), ((), ())), "nt": (((1,), (1,)), ((), ())), "tn": (((0,), (0,)), ((), ()))}
_MB = 1024 * 1024


def _cp(sem, vmem_mb=48):
    return pltpu.CompilerParams(dimension_semantics=sem, vmem_limit_bytes=vmem_mb * _MB)


def _dot(a, b, mode):
    return lax.dot_general(a.astype(MXU), b.astype(MXU), _DIMS[mode], preferred_element_type=F32)


@jax.custom_vjp
def _mm_nn(a, b):
    return _dot(a, b, "nn")


def _mm_nn_fwd(a, b):
    return _dot(a, b, "nn"), (a, b)


def _mm_nn_bwd(res, g):
    a, b = res
    return _dot(g, b, "nt"), _dot(a, g, "tn")


_mm_nn.defvjp(_mm_nn_fwd, _mm_nn_bwd)


@jax.custom_vjp
def _mm_nt(a, bt):
    return _dot(a, bt, "nt")


def _mm_nt_fwd(a, bt):
    return _dot(a, bt, "nt"), (a, bt)


def _mm_nt_bwd(res, g):
    a, bt = res
    return _dot(g, bt, "nn"), _dot(g, a, "tn")


_mm_nt.defvjp(_mm_nt_fwd, _mm_nt_bwd)


def _rmsnorm(x, w):
    return x * lax.rsqrt(jnp.mean(x * x, axis=-1, keepdims=True) + EPS) * w


def _layernorm(x, w, b):
    mu = jnp.mean(x, axis=-1, keepdims=True)
    var = jnp.mean(jnp.square(x - mu), axis=-1, keepdims=True)
    return (x - mu) * lax.rsqrt(var + EPS) * w + b


def _silu(x):
    return x * jax.nn.sigmoid(x)


def _matmul(a, b, mode, *, name, shape, tm, tn, tk, out_dtype=F32, add=None, a_off=(0, 0), b_off=(0, 0), after=None):
    m, n, k = shape
    tm, tn, tk = min(tm, m), min(tn, n), min(tk, k)
    assert m % tm == 0 and n % tn == 0 and k % tk == 0, (name, shape, tm, tn, tk)
    nk = k // tk
    has_add, has_after = add is not None, after is not None

    def body(*refs):
        a_ref, b_ref = refs[0], refs[1]
        pos = 2
        add_ref = None
        if has_add:
            add_ref = refs[pos]
            pos += 1
        if has_after:
            pos += 1
        o_ref = refs[pos]
        p = _dot(a_ref[...], b_ref[...], mode)
        if nk == 1:
            if has_add:
                p = p + add_ref[...].astype(F32)
            o_ref[...] = p.astype(out_dtype)
            return
        acc_ref = refs[pos + 1]
        kk = pl.program_id(2)

        @pl.when(kk == 0)
        def _():
            acc_ref[...] = p

        @pl.when(kk > 0)
        def _():
            acc_ref[...] += p

        @pl.when(kk == nk - 1)
        def _():
            r = acc_ref[...]
            if has_add:
                r = r + add_ref[...].astype(F32)
            o_ref[...] = r.astype(out_dtype)

    a0, a1 = a_off
    b0, b1 = b_off
    if mode == "tn":
        a_spec = pl.BlockSpec((tk, tm), lambda i, j, kk: (kk + a0, i + a1))
    else:
        a_spec = pl.BlockSpec((tm, tk), lambda i, j, kk: (i + a0, kk + a1))
    if mode == "nt":
        b_spec = pl.BlockSpec((tn, tk), lambda i, j, kk: (j + b0, kk + b1))
    else:
        b_spec = pl.BlockSpec((tk, tn), lambda i, j, kk: (kk + b0, j + b1))
    in_specs = [a_spec, b_spec]
    args = [a, b]
    if has_add:
        in_specs.append(pl.BlockSpec((tm, tn), lambda i, j, kk: (i, j)))
        args.append(add)
    if has_after:
        in_specs.append(pl.BlockSpec(memory_space=pl.ANY))
        args.append(after)
    return pl.pallas_call(
        body, name=name, grid=(m // tm, n // tn, nk),
        in_specs=in_specs,
        out_specs=pl.BlockSpec((tm, tn), lambda i, j, kk: (i, j)),
        out_shape=jax.ShapeDtypeStruct((m, n), out_dtype),
        scratch_shapes=[pltpu.VMEM((tm, tn), F32)] if nk > 1 else [],
        compiler_params=_cp(("parallel", "parallel", "arbitrary")),
    )(*args)


def _rms_fwd(x, w, *, name, tm=256, after=None):
    L, d = x.shape
    tm = min(tm, L)
    extra = [] if after is None else [after]

    def body(x_ref, w_ref, *rest):
        rest[-1][...] = _rmsnorm(x_ref[...], w_ref[...]).astype(MXU)

    return pl.pallas_call(
        body, name=name, grid=(L // tm,),
        in_specs=([pl.BlockSpec((tm, d), lambda i: (i, 0)), pl.BlockSpec((1, d), lambda i: (0, 0))]
                  + [pl.BlockSpec(memory_space=pl.ANY)] * len(extra)),
        out_specs=pl.BlockSpec((tm, d), lambda i: (i, 0)),
        out_shape=jax.ShapeDtypeStruct((L, d), MXU),
        compiler_params=_cp(("parallel",)),
    )(x, w.reshape(1, d), *extra)


def _rms_bwd(x, w, dh, dxn, *, name, tm=256, after=None):
    L, d = x.shape
    tm = min(tm, L)
    extra = [] if after is None else [after]

    def body(x_ref, w_ref, dh_ref, dxn_ref, *rest):
        dx_ref, dw_ref = rest[-2:]
        _, vjp = jax.vjp(_rmsnorm, x_ref[...], w_ref[...])
        dx, dw = vjp(dh_ref[...])
        dx_ref[...] = dx + dxn_ref[...]

        @pl.when(pl.program_id(0) == 0)
        def _():
            dw_ref[...] = jnp.zeros_like(dw_ref)

        dw_ref[...] += dw

    row = pl.BlockSpec((tm, d), lambda i: (i, 0))
    vec = pl.BlockSpec((1, d), lambda i: (0, 0))
    dx, dw = pl.pallas_call(
        body, name=name, grid=(L // tm,),
        in_specs=[row, vec, row, row] + [pl.BlockSpec(memory_space=pl.ANY)] * len(extra), out_specs=[row, vec],
        out_shape=[jax.ShapeDtypeStruct((L, d), F32), jax.ShapeDtypeStruct((1, d), F32)],
        compiler_params=_cp(("arbitrary",)),
    )(x, w.reshape(1, d), dh, dxn, *extra)
    return dx, dw.reshape(d)


def _final_loss(x, w, tgt, *, name, tm=256):
    L, d = x.shape
    tm = min(tm, L)

    def loss_fn(xv, wv, tv):
        err = jnp.square(_rmsnorm(xv, wv) - tv)
        return 0.5 * jnp.sum(jnp.mean(err, axis=-1, keepdims=True), axis=0, keepdims=True)

    def body(x_ref, w_ref, t_ref, loss_ref, dx_ref, dw_ref):
        tv = t_ref[...]
        val, vjp = jax.vjp(lambda xv, wv: loss_fn(xv, wv, tv), x_ref[...], w_ref[...])
        dx, dw = vjp(jnp.ones((1, 1), F32))
        dx_ref[...] = dx

        @pl.when(pl.program_id(0) == 0)
        def _():
            dw_ref[...] = jnp.zeros_like(dw_ref)
            loss_ref[...] = jnp.zeros_like(loss_ref)

        dw_ref[...] += dw
        loss_ref[...] += jnp.broadcast_to(val, loss_ref.shape)

    row = pl.BlockSpec((tm, d), lambda i: (i, 0))
    vec = pl.BlockSpec((1, d), lambda i: (0, 0))
    loss, dx, dw = pl.pallas_call(
        body, name=name, grid=(L // tm,),
        in_specs=[row, vec, row],
        out_specs=[pl.BlockSpec((8, 128), lambda i: (0, 0)), row, vec],
        out_shape=[jax.ShapeDtypeStruct((8, 128), F32), jax.ShapeDtypeStruct((L, d), F32),
                   jax.ShapeDtypeStruct((1, d), F32)],
        compiler_params=_cp(("arbitrary",)),
    )(x, w.reshape(1, d), tgt)
    return loss[0, 0], dx, dw.reshape(d)


PARAM_ROWS = 512


def _s5_param_fn(are, aim, ldt, bre, bim, row0):
    n = are.shape[0]
    grp = (row0 + lax.broadcasted_iota(jnp.int32, (n, SSM_GROUPS), 0)) // SSM_STATE
    col = lax.broadcasted_iota(jnp.int32, (n, SSM_GROUPS), 1)
    sel = (grp == col).astype(F32)
    dt = jnp.sum(sel * jnp.exp(ldt), axis=-1, keepdims=True)
    mag = jnp.exp(are * dt)
    ang = aim * dt
    lbr = mag * jnp.cos(ang)
    lbi = mag * jnp.sin(ang)
    den = are * are + aim * aim
    nr = lbr - 1.0
    kr = (nr * are + lbi * aim) / den
    ki = (lbi * are - nr * aim) / den
    return lbr, lbi, kr * bre - ki * bim, kr * bim + ki * bre


def _s5_param_specs():
    col = pl.BlockSpec((PARAM_ROWS, 1), lambda i: (i, 0))
    mat = pl.BlockSpec((PARAM_ROWS, SSM_GROUP), lambda i: (i, 0))
    vec = pl.BlockSpec((1, SSM_GROUPS), lambda i: (0, 0))
    return col, mat, vec


def _s5_params_fwd(are, aim, ldt, bre, bim, *, name, after=None):
    n = are.shape[0]
    col, mat, vec = _s5_param_specs()
    extra = [] if after is None else [after]

    def body(are_ref, aim_ref, ldt_ref, bre_ref, bim_ref, *rest):
        lbr_ref, lbi_ref, bbr_ref, bbi_ref = rest[-4:]
        row0 = pl.program_id(0) * PARAM_ROWS
        lbr, lbi, bbr, bbi = _s5_param_fn(are_ref[...], aim_ref[...], ldt_ref[...], bre_ref[...], bim_ref[...], row0)
        lbr_ref[...] = lbr
        lbi_ref[...] = lbi
        bbr_ref[...] = bbr
        bbi_ref[...] = bbi

    cshape = jax.ShapeDtypeStruct((n, 1), F32)
    mshape = jax.ShapeDtypeStruct((n, SSM_GROUP), F32)
    return pl.pallas_call(body, name=name, grid=(n // PARAM_ROWS,),
                          in_specs=[col, col, vec, mat, mat] + [pl.BlockSpec(memory_space=pl.ANY)] * len(extra),
                          out_specs=[col, col, mat, mat], out_shape=[cshape, cshape, mshape, mshape],
                          compiler_params=_cp(("parallel",)))(are, aim, ldt, bre, bim, *extra)


def _s5_params_bwd(are, aim, ldt, bre, bim, dlbr, dlbi, dbbr, dbbi, *, name):
    n = are.shape[0]
    col, mat, vec = _s5_param_specs()

    def body(are_ref, aim_ref, ldt_ref, bre_ref, bim_ref, g0, g1, g2, g3, o0, o1, o2, o3, o4):
        row0 = pl.program_id(0) * PARAM_ROWS
        _, vjp = jax.vjp(lambda a, b, c, d, e: _s5_param_fn(a, b, c, d, e, row0),
                         are_ref[...], aim_ref[...], ldt_ref[...], bre_ref[...], bim_ref[...])
        dare, daim, dldt, dbre, dbim = vjp((g0[...], g1[...], g2[...], g3[...]))
        o0[...] = dare
        o1[...] = daim
        o3[...] = dbre
        o4[...] = dbim

        @pl.when(pl.program_id(0) == 0)
        def _():
            o2[...] = jnp.zeros_like(o2)

        o2[...] += dldt

    cshape = jax.ShapeDtypeStruct((n, 1), F32)
    mshape = jax.ShapeDtypeStruct((n, SSM_GROUP), F32)
    return pl.pallas_call(body, name=name, grid=(n // PARAM_ROWS,),
                          in_specs=[col, col, vec, mat, mat, col, col, mat, mat],
                          out_specs=[col, col, vec, mat, mat],
                          out_shape=[cshape, cshape, jax.ShapeDtypeStruct((1, SSM_GROUPS), F32), mshape, mshape],
                          compiler_params=_cp(("arbitrary",)))(are, aim, ldt, bre, bim, dlbr, dlbi, dbbr, dbbi)


SLAB_NC = SLAB_CH // 128


def _s5_specs(L):
    slab = pl.BlockSpec((L, SLAB), lambda s: (0, s))
    wspec = pl.BlockSpec((SLAB_NC, 128, SLAB), lambda s: (s, 0, 0))
    lspec = pl.BlockSpec((SLAB_NC, 1, 128), lambda s: (s, 0, 0))
    sspec = pl.BlockSpec((SLAB_NC, L, 128), lambda s: (s, 0, 0))
    dspec = pl.BlockSpec((1, SLAB), lambda s: (0, s))
    return slab, wspec, lspec, sspec, dspec


def _scan_inplace(sr_ref, si_ref, lr, li, pr_ref, pi_ref, *, reverse):
    NC, L, W = sr_ref.shape
    S = SCAN_SEG
    T = L // S
    lr8 = [jnp.broadcast_to(lr[k], (S, W)) for k in range(NC)]
    li8 = [jnp.broadcast_to(li[k], (S, W)) for k in range(NC)]

    def tiles(first, count):
        return pl.ds(first * S, count * S)

    for k in range(NC):
        pr_ref[k, tiles(T - 1 if reverse else 0, 1), :] = lr8[k]
        pi_ref[k, tiles(T - 1 if reverse else 0, 1), :] = li8[k]
        n = 1
        while n < T:
            have = tiles(T - n, n) if reverse else tiles(0, n)
            new = tiles(T - 2 * n, n) if reverse else tiles(n, n)
            top = tiles(T - n, 1) if reverse else tiles(n - 1, 1)
            ar, ai = pr_ref[k, top, :][None], pi_ref[k, top, :][None]
            hr, hi = pr_ref[k, have, :].reshape(n, S, W), pi_ref[k, have, :].reshape(n, S, W)
            pr_ref[k, new, :] = (hr * ar - hi * ai).reshape(n * S, W)
            pi_ref[k, new, :] = (hr * ai + hi * ar).reshape(n * S, W)
            n *= 2

    def step(i, carry):
        for u in range(SCAN_STEPS):
            jj = i * SCAN_STEPS + u
            rows = pl.ds(pl.multiple_of(((T - 1 - jj) if reverse else jj) * S, S), S)
            out = []
            for k in range(NC):
                sr, si = carry[k]
                nsr = lr8[k] * sr - li8[k] * si + sr_ref[k, rows, :]
                nsi = lr8[k] * si + li8[k] * sr + si_ref[k, rows, :]
                sr_ref[k, rows, :] = nsr
                si_ref[k, rows, :] = nsi
                out.append((nsr, nsi))
            carry = tuple(out)
        return carry

    zero = jnp.zeros((S, W), F32)
    ends = lax.fori_loop(0, T // SCAN_STEPS, step, tuple((zero, zero) for k in range(NC)))
    sub = lax.broadcasted_iota(jnp.int32, (S, W), 0)
    order = range(S - 1, -1, -1) if reverse else range(S)
    for k in range(NC):
        er, ei = ends[k]
        full = tiles(0 if reverse else T - 1, 1)
        ltr = pr_ref[k, full, :][0:1]
        lti = pi_ref[k, full, :][0:1]
        cr = jnp.zeros((1, W), F32)
        ci = jnp.zeros((1, W), F32)
        ctr = jnp.zeros((S, W), F32)
        cti = jnp.zeros((S, W), F32)
        for seg in order:
            ctr = jnp.where(sub == seg, cr, ctr)
            cti = jnp.where(sub == seg, ci, cti)
            cr, ci = (er[seg:seg + 1, :] + ltr * cr - lti * ci, ei[seg:seg + 1, :] + ltr * ci + lti * cr)
        pr = pr_ref[k].reshape(T, S, W)
        pi = pi_ref[k].reshape(T, S, W)
        sr_ref[k] += (pr * ctr[None] - pi * cti[None]).reshape(L, W)
        si_ref[k] += (pr * cti[None] + pi * ctr[None]).reshape(L, W)


def _time_interleave(a):
    L, W = a.shape
    return a.reshape(SCAN_SEG, L // SCAN_SEG, W).transpose(1, 0, 2).reshape(L, W)


def _time_deinterleave(a):
    L, W = a.shape
    return a.reshape(L // SCAN_SEG, SCAN_SEG, W).transpose(1, 0, 2).reshape(L, W)


def _s5_fwd(u, btr, bti, cbr, cbi, lbr, lbi, dvec, *, name, after=None):
    L = u.shape[0]
    extra = [] if after is None else [after]

    def body(u_ref, btr_ref, bti_ref, cbr_ref, cbi_ref, lr_ref, li_ref, d_ref, *rest):
        ys_ref, sr_ref, si_ref, pr_ref, pi_ref = rest[-5:]
        u = u_ref[...]
        for k in range(SLAB_NC):
            sr_ref[k] = _dot(u, btr_ref[k], "nt")
            si_ref[k] = _dot(u, bti_ref[k], "nt")
        _scan_inplace(sr_ref, si_ref, lr_ref[...], li_ref[...], pr_ref, pi_ref, reverse=False)
        ys = d_ref[...] * u
        for k in range(SLAB_NC):
            ys = ys + _dot(sr_ref[k], cbr_ref[k], "nn") - _dot(si_ref[k], cbi_ref[k], "nn")
        ys_ref[...] = ys

    slab, wspec, lspec, sspec, dspec = _s5_specs(L)
    sshape = jax.ShapeDtypeStruct((N_SLAB * SLAB_NC, L, 128), F32)
    return pl.pallas_call(
        body, name=name, grid=(N_SLAB,),
        in_specs=[slab, wspec, wspec, wspec, wspec, lspec, lspec, dspec] + [pl.BlockSpec(memory_space=pl.ANY)] * len(extra),
        out_specs=[slab, sspec, sspec],
        out_shape=[jax.ShapeDtypeStruct((L, SSM_WIDTH), F32), sshape, sshape],
        scratch_shapes=[pltpu.VMEM((SLAB_NC, L, 128), F32), pltpu.VMEM((SLAB_NC, L, 128), F32)],
        compiler_params=_cp(("parallel",), 56),
    )(u, btr, bti, cbr, cbi, lbr, lbi, dvec, *extra)


def _s5_bwd(dys, u, sr, si, btr, bti, cbr, cbi, lbr, lbi, dvec, *, name, after=None):
    L = u.shape[0]
    S = SCAN_SEG
    extra = [] if after is None else [after]

    def body(dys_ref, u_ref, sr_ref, si_ref, btr_ref, bti_ref, cbr_ref, cbi_ref, lr_ref, li_ref, d_ref, *rest):
        (du_ref, dbtr_ref, dbti_ref, dcbr_ref, dcbi_ref, dlr_ref, dli_ref, dd_ref,
         ar_ref, ai_ref, pr_ref, pi_ref) = rest[-12:]
        dys = dys_ref[...]
        u = u_ref[...]
        for k in range(SLAB_NC):
            ar_ref[k] = _dot(dys, cbr_ref[k], "nt")
            ai_ref[k] = -_dot(dys, cbi_ref[k], "nt")
        _scan_inplace(ar_ref, ai_ref, lr_ref[...], -li_ref[...], pr_ref, pi_ref, reverse=True)
        head = lax.broadcasted_iota(jnp.int32, (L, 1), 0) < S
        sub0 = lax.broadcasted_iota(jnp.int32, (S, 1), 0) == 0

        def prev_state(s):
            up = pltpu.roll(s, S, 0)
            return jnp.where(head, 0.0, up), jnp.where(sub0, 0.0, pltpu.roll(up[0:S], 1, 0))

        du = d_ref[...] * dys
        for k in range(SLAB_NC):
            a_re = ar_ref[k]
            a_im = ai_ref[k]
            du = du + _dot(a_re, btr_ref[k], "nn") + _dot(a_im, bti_ref[k], "nn")
            dbtr_ref[k] = _dot(a_re, u, "tn")
            dbti_ref[k] = _dot(a_im, u, "tn")
            s_re = sr_ref[k]
            s_im = si_ref[k]
            dcbr_ref[k] = _dot(s_re, dys, "tn")
            dcbi_ref[k] = -_dot(s_im, dys, "tn")
            p_re, q_re = prev_state(s_re)
            p_im, q_im = prev_state(s_im)
            b_re, b_im = a_re[0:S], a_im[0:S]
            dlr_ref[k] = (jnp.sum(p_re * a_re + p_im * a_im, axis=0, keepdims=True)
                          + jnp.sum(q_re * b_re + q_im * b_im, axis=0, keepdims=True))
            dli_ref[k] = (jnp.sum(p_re * a_im - p_im * a_re, axis=0, keepdims=True)
                          + jnp.sum(q_re * b_im - q_im * b_re, axis=0, keepdims=True))
        du_ref[...] = du
        dd_ref[...] = jnp.sum(dys * u, axis=0, keepdims=True)

    slab, wspec, lspec, sspec, dspec = _s5_specs(L)
    wshape = jax.ShapeDtypeStruct((N_SLAB * SLAB_NC, 128, SLAB), F32)
    lshape = jax.ShapeDtypeStruct((N_SLAB * SLAB_NC, 1, 128), F32)
    return pl.pallas_call(
        body, name=name, grid=(N_SLAB,),
        in_specs=([slab, slab, sspec, sspec, wspec, wspec, wspec, wspec, lspec, lspec, dspec]
                  + [pl.BlockSpec(memory_space=pl.ANY)] * len(extra)),
        out_specs=[slab, wspec, wspec, wspec, wspec, lspec, lspec, dspec],
        out_shape=[jax.ShapeDtypeStruct((L, SSM_WIDTH), F32), wshape, wshape, wshape, wshape, lshape, lshape,
                   jax.ShapeDtypeStruct((1, SSM_WIDTH), F32)],
        scratch_shapes=[pltpu.VMEM((SLAB_NC, L, 128), F32)] * 4,
        compiler_params=_cp(("parallel",), 56),
    )(dys, u, sr, si, btr, bti, cbr, cbi, lbr, lbi, dvec, *extra)


_SLAB_MASK = (np.arange(SLAB_CH)[:, None] // SSM_STATE == np.arange(SLAB)[None, :] // SSM_GROUP)


def _expand_bd(x):
    t = jnp.tile(x.reshape(N_SLAB, SLAB_CH, SSM_GROUP), (1, 1, SLAB // SSM_GROUP))
    return jnp.where(_SLAB_MASK[None], t, 0.0).astype(MXU).reshape(N_SLAB * SLAB_NC, 128, SLAB)


def _contract_bd(dx):
    t = jnp.where(_SLAB_MASK[None], dx.reshape(N_SLAB, SLAB_CH, SLAB), 0.0)
    return jnp.sum(t.reshape(N_SLAB, SLAB_CH, SLAB // SSM_GROUP, SSM_GROUP), axis=2).reshape(SSM_CH, SSM_GROUP)


def _glu_ew(ys, zlin, za):
    a1 = jax.nn.gelu(ys)
    return a1 * jax.nn.sigmoid(zlin) * _silu(za)


def _glu_fwd(ys, main, gw, gb, *, name, tm=256):
    L = ys.shape[0]
    tm = min(tm, L)
    W = SSM_WIDTH

    def body(ys_ref, za_ref, gw_ref, gb_ref, ya_ref):
        ys = ys_ref[...]
        a1 = jax.nn.gelu(ys)
        zlin = _dot(a1, gw_ref[...], "nn") + gb_ref[...]
        ya_ref[...] = _glu_ew(ys, zlin, za_ref[...]).astype(MXU)

    return pl.pallas_call(
        body, name=name, grid=(L // tm,),
        in_specs=[pl.BlockSpec((tm, W), lambda i: (i, 0)), pl.BlockSpec((tm, W), lambda i: (i, 1)),
                  pl.BlockSpec((W, W), lambda i: (0, 0)), pl.BlockSpec((1, W), lambda i: (0, 0))],
        out_specs=pl.BlockSpec((tm, W), lambda i: (i, 0)),
        out_shape=jax.ShapeDtypeStruct((L, W), MXU),
        compiler_params=_cp(("parallel",)),
    )(ys, main, gw, gb.reshape(1, W))


def _glu_bwd(dya, ys, main, gw, gb, *, name, tm=256):
    L = ys.shape[0]
    tm = min(tm, L)
    W = SSM_WIDTH

    def body(dya_ref, ys_ref, za_ref, gw_ref, gb_ref, dys_ref, dza_ref, a1_ref, dzl_ref, db_ref):
        ys = ys_ref[...]
        a1, gelu_vjp = jax.vjp(jax.nn.gelu, ys)
        zlin = _dot(a1, gw_ref[...], "nn") + gb_ref[...]
        _, vjp = jax.vjp(lambda a, z, za: a * jax.nn.sigmoid(z) * _silu(za), a1, zlin, za_ref[...])
        da1, dzlin, dza = vjp(dya_ref[...].astype(F32))
        da1 = da1 + _dot(dzlin, gw_ref[...], "nt")
        dys_ref[...] = gelu_vjp(da1)[0]
        dza_ref[...] = dza
        a1_ref[...] = a1.astype(MXU)
        dzl_ref[...] = dzlin.astype(MXU)

        @pl.when(pl.program_id(0) == 0)
        def _():
            db_ref[...] = jnp.zeros_like(db_ref)

        db_ref[...] += jnp.sum(dzlin, axis=0, keepdims=True)

    row = pl.BlockSpec((tm, W), lambda i: (i, 0))
    vec = pl.BlockSpec((1, W), lambda i: (0, 0))
    return pl.pallas_call(
        body, name=name, grid=(L // tm,),
        in_specs=[row, row, pl.BlockSpec((tm, W), lambda i: (i, 1)), pl.BlockSpec((W, W), lambda i: (0, 0)), vec],
        out_specs=[row, row, row, row, vec],
        out_shape=[jax.ShapeDtypeStruct((L, W), F32), jax.ShapeDtypeStruct((L, W), F32),
                   jax.ShapeDtypeStruct((L, W), MXU), jax.ShapeDtypeStruct((L, W), MXU),
                   jax.ShapeDtypeStruct((1, W), F32)],
        compiler_params=_cp(("arbitrary",)),
    )(dya, ys, main, gw, gb.reshape(1, W))


def _sg_fn(ub, vb, zb, lnw, lnb, ws, bs):
    u = jax.nn.gelu(ub)
    v = _layernorm(jax.nn.gelu(vb), lnw, lnb)
    r = lax.broadcasted_iota(jnp.int32, (SG_CHUNK, SG_CHUNK), 0)
    c = lax.broadcasted_iota(jnp.int32, (SG_CHUNK, SG_CHUNK), 1)
    tri = r >= c
    outs = []
    for h in range(SG_HEADS):
        wh = jnp.where(tri, ws[h], 0.0)
        outs.append(_mm_nn(wh, v[:, h * 128:(h + 1) * 128]) + bs[h])
    mixed = jnp.concatenate(outs, axis=1)
    return u * mixed * _silu(zb)


def _sg_specs(L):
    W = SSM_WIDTH
    blk = lambda c: pl.BlockSpec((SG_CHUNK, W), lambda i, c=c: (i, c))
    vec = pl.BlockSpec((1, W), lambda i: (0, 0))
    wspec = pl.BlockSpec((SG_HEADS, SG_CHUNK, SG_CHUNK), lambda i: (0, 0, 0))
    bspec = pl.BlockSpec((SG_HEADS, SG_CHUNK, 1), lambda i: (0, 0, 0))
    return blk, vec, wspec, bspec


def _sg_fwd(main, lnw, lnb, sgw, sgb, *, name):
    L = main.shape[0]
    W = SSM_WIDTH
    blk, vec, wspec, bspec = _sg_specs(L)

    def body(ub_ref, vb_ref, zb_ref, lnw_ref, lnb_ref, w_ref, b_ref, yb_ref):
        ws = [w_ref[h] for h in range(SG_HEADS)]
        bs = [b_ref[h] for h in range(SG_HEADS)]
        yb_ref[...] = _sg_fn(ub_ref[...], vb_ref[...], zb_ref[...], lnw_ref[...], lnb_ref[...], ws, bs).astype(MXU)

    return pl.pallas_call(
        body, name=name, grid=(L // SG_CHUNK,),
        in_specs=[blk(2), blk(3), blk(4), vec, vec, wspec, bspec],
        out_specs=pl.BlockSpec((SG_CHUNK, W), lambda i: (i, 0)),
        out_shape=jax.ShapeDtypeStruct((L, W), MXU),
        compiler_params=_cp(("parallel",)),
    )(main, main, main, lnw.reshape(1, W), lnb.reshape(1, W), sgw, sgb.reshape(SG_HEADS, SG_CHUNK, 1))


def _sg_bwd(dyb, main, lnw, lnb, sgw, sgb, *, name):
    L = main.shape[0]
    W = SSM_WIDTH
    blk, vec, wspec, bspec = _sg_specs(L)

    def body(dyb_ref, ub_ref, vb_ref, zb_ref, lnw_ref, lnb_ref, w_ref, b_ref,
             dub_ref, dvb_ref, dzb_ref, dlnw_ref, dlnb_ref, dw_ref, db_ref):
        ws = [w_ref[h] for h in range(SG_HEADS)]
        bs = [b_ref[h] for h in range(SG_HEADS)]
        _, vjp = jax.vjp(_sg_fn, ub_ref[...], vb_ref[...], zb_ref[...], lnw_ref[...], lnb_ref[...], ws, bs)
        dub, dvb, dzb, dlnw, dlnb, dws, dbs = vjp(dyb_ref[...])

        @pl.when(pl.program_id(0) == 0)
        def _():
            dlnw_ref[...] = jnp.zeros_like(dlnw_ref)
            dlnb_ref[...] = jnp.zeros_like(dlnb_ref)
            dw_ref[...] = jnp.zeros_like(dw_ref)
            db_ref[...] = jnp.zeros_like(db_ref)

        dub_ref[...] = dub
        dvb_ref[...] = dvb
        dzb_ref[...] = dzb
        dlnw_ref[...] += dlnw
        dlnb_ref[...] += dlnb
        for h in range(SG_HEADS):
            dw_ref[h] += dws[h]
            db_ref[h] += dbs[h]

    row = pl.BlockSpec((SG_CHUNK, W), lambda i: (i, 0))
    out = jax.ShapeDtypeStruct((L, W), F32)
    return pl.pallas_call(
        body, name=name, grid=(L // SG_CHUNK,),
        in_specs=[row, blk(2), blk(3), blk(4), vec, vec, wspec, bspec],
        out_specs=[row, row, row, vec, vec, wspec, bspec],
        out_shape=[out, out, out, jax.ShapeDtypeStruct((1, W), F32), jax.ShapeDtypeStruct((1, W), F32),
                   jax.ShapeDtypeStruct((SG_HEADS, SG_CHUNK, SG_CHUNK), F32),
                   jax.ShapeDtypeStruct((SG_HEADS, SG_CHUNK, 1), F32)],
        compiler_params=_cp(("arbitrary",)),
    )(dyb, main, main, main, lnw.reshape(1, W), lnb.reshape(1, W), sgw, sgb.reshape(SG_HEADS, SG_CHUNK, 1))


def _rope_tables(L):
    half = ROT_DIM // 2
    inv_freq = ROPE_THETA ** (-jnp.arange(0, ROT_DIM, 2, dtype=F32) / ROT_DIM)
    ang = jnp.arange(L, dtype=F32)[:, None] * inv_freq[None, :]
    cos = jnp.cos(ang)
    sin = jnp.sin(ang)
    ones = jnp.ones((L, HEAD_DIM - ROT_DIM), F32)
    cosf = jnp.concatenate([cos, cos, ones], axis=1)
    sinf = jnp.concatenate([sin, sin, 0.0 * ones], axis=1)
    rot = np.zeros((HEAD_DIM, HEAD_DIM), np.float32)
    for d in range(half):
        rot[d + half, d] = -1.0
        rot[d, d + half] = 1.0
    return cosf, sinf, jnp.asarray(rot)


def _rope(t, cosf, sinf, rot):
    shp = t.shape
    t2 = t.reshape(-1, HEAD_DIM)
    sw = lax.dot_general(t2, rot, _DIMS["nn"], precision=lax.Precision.HIGH, preferred_element_type=F32).reshape(shp)
    return t * cosf + sw * sinf


def _softmax_sink_parts(s, sink):
    m = jnp.maximum(jnp.max(s, axis=-1, keepdims=True), sink)
    e = jnp.exp(s - m)
    es = jnp.exp(sink - m)
    r = 1.0 / (jnp.sum(e, axis=-1, keepdims=True) + es)
    return e * r, es * r


@jax.custom_vjp
def _softmax_sink(s, sink):
    return _softmax_sink_parts(s, sink)[0]


def _softmax_sink_fwd(s, sink):
    p, p_sink = _softmax_sink_parts(s, sink)
    return p, (p, p_sink)


def _softmax_sink_bwd(res, dp):
    p, p_sink = res
    t = jnp.sum(p * dp, axis=-1, keepdims=True)
    return p * (dp - t), -jnp.sum(p_sink * t, axis=1, keepdims=True)


_softmax_sink.defvjp(_softmax_sink_fwd, _softmax_sink_bwd)


def _attn_block_fn(q, kw, vw, sinks, cq, sq, ck, sk, rot, q0, k0):
    nk = kw.shape[1]
    qr = _rope(q, cq, sq, rot)
    kr = _rope(kw, ck, sk, rot)
    qpos = q0 + lax.broadcasted_iota(jnp.int32, (1, ATT_BLOCK, nk), 1)
    kpos = k0 + lax.broadcasted_iota(jnp.int32, (1, ATT_BLOCK, nk), 2)
    diff = qpos - kpos
    allowed = (diff >= 0) & (diff < WINDOW)
    outs = []
    for kh in range(ATT_KV_HEADS):
        qh = qr[kh * GQA_GROUP:(kh + 1) * GQA_GROUP].reshape(GQA_GROUP * ATT_BLOCK, HEAD_DIM)
        s = _mm_nt(qh, kr[kh]).reshape(GQA_GROUP, ATT_BLOCK, nk) * (HEAD_DIM ** -0.5)
        s = jnp.where(allowed, s, NEG_INF)
        p = _softmax_sink(s, sinks[kh * GQA_GROUP:(kh + 1) * GQA_GROUP])
        o = _mm_nn(p.reshape(GQA_GROUP * ATT_BLOCK, nk), vw[kh])
        outs.append(o.reshape(GQA_GROUP, ATT_BLOCK, HEAD_DIM))
    return jnp.concatenate(outs, axis=0)


def _attn_common(L):
    nwin = min(2 * ATT_BLOCK, L)
    qspec = pl.BlockSpec((ATT_HEADS, ATT_BLOCK, HEAD_DIM), lambda n: (0, n, 0))
    kvspec = pl.BlockSpec((ATT_KV_HEADS, L, HEAD_DIM), lambda n: (0, 0, 0))
    sspec = pl.BlockSpec((ATT_HEADS, 1, 1), lambda n: (0, 0, 0))
    tq = pl.BlockSpec((ATT_BLOCK, HEAD_DIM), lambda n: (n, 0))
    tk = pl.BlockSpec((L, HEAD_DIM), lambda n: (0, 0))
    rspec = pl.BlockSpec((HEAD_DIM, HEAD_DIM), lambda n: (0, 0))
    return nwin, qspec, kvspec, sspec, tq, tk, rspec


def _attn_fwd(qh, kh, vh, sinks, cosf, sinf, rot, *, name):
    L = qh.shape[1]
    nwin, qspec, kvspec, sspec, tq, tk, rspec = _attn_common(L)

    def body(q_ref, k_ref, v_ref, s_ref, cq_ref, sq_ref, ck_ref, sk_ref, r_ref, o_ref):
        n = pl.program_id(0)
        k0 = pl.multiple_of(jnp.maximum(n - 1, 0) * ATT_BLOCK, ATT_BLOCK)
        win = pl.ds(k0, nwin)
        o_ref[...] = _attn_block_fn(q_ref[...], k_ref[:, win, :], v_ref[:, win, :], s_ref[...],
                                    cq_ref[...], sq_ref[...], ck_ref[win, :], sk_ref[win, :], r_ref[...],
                                    n * ATT_BLOCK, k0)

    return pl.pallas_call(
        body, name=name, grid=(L // ATT_BLOCK,),
        in_specs=[qspec, kvspec, kvspec, sspec, tq, tq, tk, tk, rspec],
        out_specs=qspec,
        out_shape=jax.ShapeDtypeStruct((ATT_HEADS, L, HEAD_DIM), F32),
        compiler_params=_cp(("parallel",)),
    )(qh, kh, vh, sinks.reshape(ATT_HEADS, 1, 1), cosf, sinf, cosf, sinf, rot)


def _attn_bwd(do, qh, kh, vh, sinks, cosf, sinf, rot, *, name):
    L = qh.shape[1]
    nwin, qspec, kvspec, sspec, tq, tk, rspec = _attn_common(L)

    def body(do_ref, q_ref, k_ref, v_ref, s_ref, cq_ref, sq_ref, ck_ref, sk_ref, r_ref,
             dq_ref, dk_ref, dv_ref, ds_ref):
        n = pl.program_id(0)
        k0 = pl.multiple_of(jnp.maximum(n - 1, 0) * ATT_BLOCK, ATT_BLOCK)
        win = pl.ds(k0, nwin)
        cq, sq, ck, sk, rt = cq_ref[...], sq_ref[...], ck_ref[win, :], sk_ref[win, :], r_ref[...]
        q0 = n * ATT_BLOCK
        _, vjp = jax.vjp(lambda q, kw, vw, s: _attn_block_fn(q, kw, vw, s, cq, sq, ck, sk, rt, q0, k0),
                         q_ref[...], k_ref[:, win, :], v_ref[:, win, :], s_ref[...])
        dq, dkw, dvw, ds = vjp(do_ref[...])

        @pl.when(n == 0)
        def _():
            dk_ref[...] = jnp.zeros_like(dk_ref)
            dv_ref[...] = jnp.zeros_like(dv_ref)
            ds_ref[...] = jnp.zeros_like(ds_ref)

        dq_ref[...] = dq
        dk_ref[:, win, :] += dkw
        dv_ref[:, win, :] += dvw
        ds_ref[...] += ds

    return pl.pallas_call(
        body, name=name, grid=(L // ATT_BLOCK,),
        in_specs=[qspec, qspec, kvspec, kvspec, sspec, tq, tq, tk, tk, rspec],
        out_specs=[qspec, kvspec, kvspec, sspec],
        out_shape=[jax.ShapeDtypeStruct((ATT_HEADS, L, HEAD_DIM), F32),
                   jax.ShapeDtypeStruct((ATT_KV_HEADS, L, HEAD_DIM), F32),
                   jax.ShapeDtypeStruct((ATT_KV_HEADS, L, HEAD_DIM), F32),
                   jax.ShapeDtypeStruct((ATT_HEADS, 1, 1), F32)],
        compiler_params=_cp(("arbitrary",)),
    )(do, qh, kh, vh, sinks.reshape(ATT_HEADS, 1, 1), cosf, sinf, cosf, sinf, rot)


def _to_heads(t, nh):
    L = t.shape[0]
    return t.reshape(L, nh, HEAD_DIM).transpose(1, 0, 2)


def _from_heads(t):
    nh, L, _ = t.shape
    return t.transpose(1, 0, 2).reshape(L, nh * HEAD_DIM)


def _branch_fwd(ya, yb, o2d, zc, gates, wa, wb, wc, *, name, tm=256):
    L = ya.shape[0]
    tm = min(tm, L)
    W, D = SSM_WIDTH, D_MODEL

    def body(ya_ref, yb_ref, o_ref, zc_ref, g0_ref, g1_ref, g2_ref, wa_ref, wb_ref, wc_ref,
             mg_ref, ta_ref, tb_ref, tc_ref, yc_ref):
        yc = (o_ref[...] * _silu(zc_ref[...])).astype(MXU)
        ta = _dot(ya_ref[...], wa_ref[...], "nt")
        tb = _dot(yb_ref[...], wb_ref[...], "nt")
        tc = _dot(yc, wc_ref[...], "nt")
        ta_ref[...] = ta
        tb_ref[...] = tb
        tc_ref[...] = tc
        yc_ref[...] = yc
        mg_ref[...] = (jax.nn.sigmoid(g0_ref[...]) * ta + jax.nn.sigmoid(g1_ref[...]) * tb
                       + jax.nn.sigmoid(g2_ref[...]) * tc).astype(MXU)

    row = pl.BlockSpec((tm, W), lambda i: (i, 0))
    wide = pl.BlockSpec((tm, D), lambda i: (i, 0))
    gate = lambda c: pl.BlockSpec((tm, D), lambda i, c=c: (i, c))
    wspec = pl.BlockSpec((D, W), lambda i: (0, 0))
    return pl.pallas_call(
        body, name=name, grid=(L // tm,),
        in_specs=[row, row, row, row, gate(0), gate(1), gate(2), wspec, wspec, wspec],
        out_specs=[wide, wide, wide, wide, row],
        out_shape=[jax.ShapeDtypeStruct((L, D), MXU), jax.ShapeDtypeStruct((L, D), F32),
                   jax.ShapeDtypeStruct((L, D), F32), jax.ShapeDtypeStruct((L, D), F32),
                   jax.ShapeDtypeStruct((L, W), MXU)],
        compiler_params=_cp(("parallel",), 56),
    )(ya, yb, o2d, zc, gates, gates, gates, wa, wb, wc)


def _branch_bwd(dmg, ta, tb, tc, gates, *, name, tm=256):
    L = dmg.shape[0]
    tm = min(tm, L)
    D = D_MODEL

    def body(dm_ref, ta_ref, tb_ref, tc_ref, g0_ref, g1_ref, g2_ref, da_ref, db_ref, dc_ref, dg_ref):
        dm = dm_ref[...]
        for i, (t_ref, g_ref, d_ref) in enumerate(((ta_ref, g0_ref, da_ref), (tb_ref, g1_ref, db_ref),
                                                   (tc_ref, g2_ref, dc_ref))):
            sg = jax.nn.sigmoid(g_ref[...])
            d_ref[...] = (sg * dm).astype(MXU)
            dg_ref[:, i * D:(i + 1) * D] = (dm * t_ref[...] * sg * (1.0 - sg)).astype(MXU)

    wide = pl.BlockSpec((tm, D), lambda i: (i, 0))
    gate = lambda c: pl.BlockSpec((tm, D), lambda i, c=c: (i, c))
    bf = jax.ShapeDtypeStruct((L, D), MXU)
    return pl.pallas_call(
        body, name=name, grid=(L // tm,),
        in_specs=[wide, wide, wide, wide, gate(0), gate(1), gate(2)],
        out_specs=[wide, wide, wide, pl.BlockSpec((tm, 3 * D), lambda i: (i, 0))],
        out_shape=[bf, bf, bf, jax.ShapeDtypeStruct((L, 3 * D), MXU)],
        compiler_params=_cp(("parallel",), 56),
    )(dmg, ta, tb, tc, gates, gates, gates)


def _gate_c_bwd(dyc, o2d, zc, *, name, tm=256):
    L, W = dyc.shape
    tm = min(tm, L)

    def body(dy_ref, o_ref, z_ref, do_ref, dz_ref):
        _, vjp = jax.vjp(lambda o, z: o * _silu(z), o_ref[...], z_ref[...])
        do, dz = vjp(dy_ref[...])
        do_ref[...] = do
        dz_ref[...] = dz.astype(MXU)

    row = pl.BlockSpec((tm, W), lambda i: (i, 0))
    return pl.pallas_call(body, name=name, grid=(L // tm,), in_specs=[row, row, row], out_specs=[row, row],
                          out_shape=[jax.ShapeDtypeStruct((L, W), F32), jax.ShapeDtypeStruct((L, W), MXU)],
                          compiler_params=_cp(("parallel",)))(dyc, o2d, zc)


def _adamw(w, g, m, v, *, name):
    shape = w.shape
    cols = shape[-1]
    w2, g2, m2, v2 = (t.reshape(-1, cols) for t in (w, g, m, v))
    rows = w2.shape[0]
    tc = 1024 if cols % 1024 == 0 else cols
    lane_cols = -(-tc // 128) * 128
    tr = rows
    while tr % 16 == 0 and tr * lane_cols * 4 > 2 * _MB:
        tr //= 2

    def body(w_ref, g_ref, m_ref, v_ref, d_ref, nm_ref, nv_ref):
        gv = g_ref[...]
        nm = ADAM_B1 * m_ref[...] + (1.0 - ADAM_B1) * gv
        nv = ADAM_B2 * v_ref[...] + (1.0 - ADAM_B2) * jnp.square(gv)
        m_hat = nm / (1.0 - ADAM_B1 ** ADAM_STEP)
        v_hat = nv / (1.0 - ADAM_B2 ** ADAM_STEP)
        d_ref[...] = -ADAM_LR * (m_hat / (jnp.sqrt(v_hat) + ADAM_EPS) + ADAM_WD * w_ref[...])
        nm_ref[...] = nm
        nv_ref[...] = nv

    spec = pl.BlockSpec((tr, tc), lambda i, j: (i, j))
    out = jax.ShapeDtypeStruct((rows, cols), F32)
    d, nm, nv = pl.pallas_call(body, name=name, grid=(rows // tr, cols // tc), in_specs=[spec] * 4,
                               out_specs=[spec] * 3, out_shape=[out, out, out],
                               compiler_params=_cp(("parallel", "parallel")))(w2, g2, m2, v2)
    return d.reshape(shape), nm.reshape(shape), nv.reshape(shape)


def _adamw_layer(w, g, m, v, l, prev, *, name):
    _, rows, cols = w.shape
    tc = 1024 if cols % 1024 == 0 else cols
    tr = rows
    while tr % 16 == 0 and tr * tc * 4 > 2 * _MB:
        tr //= 2

    def body(w_ref, g_ref, m_ref, v_ref, *rest):
        go_ref, d_ref, nm_ref, nv_ref = rest[-4:]
        gv = g_ref[...]
        nm = ADAM_B1 * m_ref[...] + (1.0 - ADAM_B1) * gv
        nv = ADAM_B2 * v_ref[...] + (1.0 - ADAM_B2) * jnp.square(gv)
        m_hat = nm / (1.0 - ADAM_B1 ** ADAM_STEP)
        v_hat = nv / (1.0 - ADAM_B2 ** ADAM_STEP)
        d_ref[...] = -ADAM_LR * (m_hat / (jnp.sqrt(v_hat) + ADAM_EPS) + ADAM_WD * w_ref[...])
        nm_ref[...] = nm
        nv_ref[...] = nv
        go_ref[...] = gv

    lspec = pl.BlockSpec((None, tr, tc), lambda i, j: (l, i, j))
    gspec = pl.BlockSpec((tr, tc), lambda i, j: (i, j))
    out = jax.ShapeDtypeStruct(w.shape, F32)
    extra = [] if prev is None else list(prev)
    return pl.pallas_call(
        body, name=name, grid=(rows // tr, cols // tc),
        in_specs=[lspec, gspec, lspec, lspec] + [_ANY] * len(extra),
        out_specs=[lspec] * 4, out_shape=[out] * 4,
        input_output_aliases={4 + i: i for i in range(len(extra))},
        compiler_params=_cp(("parallel", "parallel")),
    )(w, g, m, v, *extra)


def _prep_layer(p, l, after=None):
    are = p["ssm_a_re"][l].reshape(SSM_CH, 1)
    aim = p["ssm_a_im"][l].reshape(SSM_CH, 1)
    ldt = p["ssm_log_dt"][l].reshape(1, SSM_GROUPS)
    bre = p["ssm_b_re"][l].reshape(SSM_CH, SSM_GROUP)
    bim = p["ssm_b_im"][l].reshape(SSM_CH, SSM_GROUP)
    lbr, lbi, bbr, bbi = _s5_params_fwd(are, aim, ldt, bre, bim, name=f"s5_params_fwd_{l}", after=after)
    cre = p["ssm_c_re"][l].transpose(0, 2, 1).reshape(SSM_CH, SSM_GROUP)
    cim = p["ssm_c_im"][l].transpose(0, 2, 1).reshape(SSM_CH, SSM_GROUP)
    return dict(raw=(are, aim, ldt, bre, bim),
                lbr=lbr.reshape(N_SLAB * SLAB_NC, 1, 128), lbi=lbi.reshape(N_SLAB * SLAB_NC, 1, 128),
                btr=_expand_bd(bbr), bti=_expand_bd(bbi), cbr=_expand_bd(cre), cbi=_expand_bd(cim),
                dvec=p["ssm_d"][l].reshape(1, SSM_WIDTH))


def _layer_fwd(x, h, p, sp, winT, rest_of, l, tabs, proj_after=None):
    L = x.shape[0]
    cosf, sinf, rot = tabs
    mm = functools.partial(_matmul, h, winT, "nt", tm=L, tn=256, tk=D_MODEL, after=proj_after)
    main = mm(name=f"proj_main_{l}", shape=(L, N_MAIN, D_MODEL))
    zc = mm(name=f"proj_zc_{l}", shape=(L, N_ZC, D_MODEL), b_off=(N_MAIN // 256, 0))
    gates = mm(name=f"proj_gates_{l}", shape=(L, N_GATES, D_MODEL), b_off=((N_MAIN + N_ZC) // 256, 0))
    big, token = rest_of([main, zc, gates])
    ua = _time_interleave(main[:, :SSM_WIDTH])
    ys, sr, si = _s5_fwd(ua, sp["btr"], sp["bti"], sp["cbr"], sp["cbi"], sp["lbr"], sp["lbi"], sp["dvec"],
                         name=f"s5_fwd_{l}", after=token)
    ys = _time_deinterleave(ys)
    ya = _glu_fwd(ys, main, big["glu_w"], p["ssm_glu_b"][l], name=f"glu_fwd_{l}")
    yb = _sg_fwd(main, p["sg_ln_w"][l], p["sg_ln_b"][l], p["sg_w"][l], p["sg_b"][l], name=f"sg_fwd_{l}")
    qh = _to_heads(main[:, 5120:6144], ATT_HEADS)
    kh = _to_heads(main[:, 6144:6272], ATT_KV_HEADS)
    vh = _to_heads(main[:, 6272:6400], ATT_KV_HEADS)
    oh = _attn_fwd(qh, kh, vh, p["attn_sinks"][l], cosf, sinf, rot, name=f"attn_fwd_{l}")
    o2d = _from_heads(oh)
    mg, ta, tb, tc, yc = _branch_fwd(ya, yb, o2d, zc, gates, big["wbaT"], big["wbbT"], big["wbcT"],
                                     name=f"branch_fwd_{l}")
    xn = _matmul(mg, big["w_out"], "nn", name=f"out_fwd_{l}", shape=(L, D_MODEL, D_MODEL), tm=512, tn=512,
                 tk=D_MODEL, add=x)
    saved = dict(x=x, h=h, main=main, zc=zc, gates=gates, ua=ua, ys=ys, sr=sr, si=si, ya=ya, yb=yb, yc=yc, o2d=o2d,
                 qh=qh, kh=kh, vh=vh, mg=mg, ta=ta, tb=tb, tc=tc, sp=sp)
    return xn, saved, big


def _layer_bwd(dxn, s, p, big, l, tabs, early):
    L = dxn.shape[0]
    D, W = D_MODEL, SSM_WIDTH
    cosf, sinf, rot = tabs
    sp = s["sp"]
    g = {}
    dmg = _matmul(dxn, big["w_out"], "nt", name=f"out_bwd_dm_{l}", shape=(L, D, D), tm=512, tn=512, tk=D)
    g["w_out"] = _matmul(s["mg"], dxn, "tn", name=f"out_bwd_dw_{l}", shape=(D, D, L), tm=512, tn=512, tk=L,
                         out_dtype=MXU)
    dta, dtb, dtc, dgates = _branch_bwd(dmg, s["ta"], s["tb"], s["tc"], s["gates"], name=f"branch_bwd_{l}")
    dys_ = {}
    for nm, dt, y, wt in (("a", dta, s["ya"], big["wbaT"]), ("b", dtb, s["yb"], big["wbbT"]),
                          ("c", dtc, s["yc"], big["wbcT"])):
        dys_[nm] = _matmul(dt, wt, "nn", name=f"branch_bwd_dy{nm}_{l}", shape=(L, W, D), tm=512, tn=512, tk=D)
        g["wb" + nm + "T"] = _matmul(dt, y, "tn", name=f"branch_bwd_dw{nm}_{l}", shape=(D, W, L),
                                     tm=512, tn=512, tk=L, out_dtype=MXU)
    do2d, dzc = _gate_c_bwd(dys_["c"], s["o2d"], s["zc"], name=f"gate_c_bwd_{l}")
    dqh, dkh, dvh, dsinks = _attn_bwd(_to_heads(do2d, ATT_HEADS), s["qh"], s["kh"], s["vh"], p["attn_sinks"][l],
                                      cosf, sinf, rot, name=f"attn_bwd_{l}")
    g["attn_sinks"] = dsinks.reshape(ATT_HEADS)
    dub, dvb, dzb, dlnw, dlnb, dsgw, dsgb = _sg_bwd(dys_["b"], s["main"], p["sg_ln_w"][l], p["sg_ln_b"][l],
                                                    p["sg_w"][l], p["sg_b"][l], name=f"sg_bwd_{l}")
    g["sg_ln_w"], g["sg_ln_b"] = dlnw.reshape(W), dlnb.reshape(W)
    g["sg_w"], g["sg_b"] = dsgw, dsgb.reshape(SG_HEADS, SG_CHUNK)
    dys, dza, a1, dzl, dgb = _glu_bwd(dys_["a"], s["ys"], s["main"], big["glu_w"], p["ssm_glu_b"][l],
                                      name=f"glu_bwd_{l}")
    g["ssm_glu_b"] = dgb.reshape(W)
    g["glu_w"] = _matmul(a1, dzl, "tn", name=f"glu_bwd_dw_{l}", shape=(W, W, L), tm=512, tn=512, tk=L, out_dtype=MXU)
    token = early(g)
    dua, dbtr, dbti, dcbr, dcbi, dlr, dli, dd = _s5_bwd(_time_interleave(dys), s["ua"], s["sr"], s["si"], sp["btr"],
                                                        sp["bti"], sp["cbr"], sp["cbi"], sp["lbr"], sp["lbi"],
                                                        sp["dvec"], name=f"s5_bwd_{l}", after=token)
    dua = _time_deinterleave(dua)
    g["ssm_d"] = dd.reshape(W)
    to_c = lambda t: _contract_bd(t).reshape(SSM_GROUPS, SSM_STATE, SSM_GROUP).transpose(0, 2, 1)
    g["ssm_c_re"], g["ssm_c_im"] = to_c(dcbr), to_c(dcbi)
    dare, daim, dldt, dbre, dbim = _s5_params_bwd(*sp["raw"], dlr.reshape(SSM_CH, 1), dli.reshape(SSM_CH, 1),
                                                  _contract_bd(dbtr), _contract_bd(dbti),
                                                  name=f"s5_params_bwd_{l}")
    g["ssm_a_re"] = dare.reshape(SSM_GROUPS, SSM_STATE)
    g["ssm_a_im"] = daim.reshape(SSM_GROUPS, SSM_STATE)
    g["ssm_log_dt"] = dldt.reshape(SSM_GROUPS)
    g["ssm_b_re"] = dbre.reshape(SSM_GROUPS, SSM_STATE, SSM_GROUP)
    g["ssm_b_im"] = dbim.reshape(SSM_GROUPS, SSM_STATE, SSM_GROUP)
    dproj = jnp.concatenate([t.astype(MXU) for t in (dua, dza, dub, dvb, dzb, _from_heads(dqh), _from_heads(dkh),
                                                     _from_heads(dvh), dzc, dgates)], axis=1)
    g["winT"] = _matmul(dproj, s["h"], "tn", name=f"proj_bwd_dw_{l}", shape=(D_IN, D, L), tm=256, tn=D, tk=L,
                        out_dtype=MXU)
    return dproj, g


def _proj_bwd_dh(dproj, winT, l, after):
    return _matmul(dproj, winT, "nn", name=f"proj_bwd_dh_{l}", shape=(dproj.shape[0], D_MODEL, D_IN), tm=512, tn=512,
                   tk=D_IN // 2, after=after)


MESH = pl.DeviceIdType.MESH
_ANY = pl.BlockSpec(memory_space=pl.ANY)
ROW_ALIGN = 16


def _coords():
    return lax.axis_index("x"), lax.axis_index("y"), lax.axis_index("c")


def _gather8(arrs, *, name):
    n = len(arrs)
    rows = [a.shape[0] for a in arrs]
    for r in rows:
        assert r % ROW_ALIGN == 0

    def body(*refs):
        ins, outs = refs[:n], refs[n:2 * n]
        send, recv, lsem = refs[2 * n:]
        x, y, c = _coords()
        me, sibling = (x, y, c), (x, y, 1 - c)
        chips = [(1 - x, y), (x, 1 - y), (1 - x, 1 - y)]

        def blk(a, px, py, pc):
            return outs[a].at[pl.ds(pl.multiple_of((4 * px + 2 * py + pc) * rows[a], ROW_ALIGN), rows[a]), :]

        def own(a):
            return ins[a]

        def copy(a, k, block, to, src=None):
            return pltpu.make_async_remote_copy(
                src_ref=blk(a, *block) if src is None else src, dst_ref=blk(a, *block),
                send_sem=send.at[a, k], recv_sem=recv.at[a, k], device_id=to, device_id_type=MESH)

        mine, first, passed = [], [], []
        for a in range(n):
            mine.append(pltpu.make_async_copy(own(a), blk(a, *me), lsem.at[a]))
            mine[a].start()
            f = [copy(a, 0, me, sibling, src=own(a))]
            f += [copy(a, 1 + j, me, (*chip, c), src=own(a)) for j, chip in enumerate(chips)]
            for cp in f:
                cp.start()
            first.append(f)
        for a in range(n):
            ps = [copy(a, 4 + j, (*chip, c), sibling) for j, chip in enumerate(chips)]
            for j, chip in enumerate(chips):
                copy(a, 1 + j, (*chip, c), me).wait_recv()
                ps[j].start()
            passed.append(ps)
        for a in range(n):
            copy(a, 0, sibling, me).wait_recv()
            for j, chip in enumerate(chips):
                copy(a, 4 + j, (*chip, 1 - c), me).wait_recv()
            for cp in first[a] + passed[a]:
                cp.wait_send()
            mine[a].wait()

    return pl.pallas_call(
        body, name=name,
        in_specs=[_ANY] * n, out_specs=[_ANY] * n,
        out_shape=[jax.ShapeDtypeStruct((8 * r,) + a.shape[1:], a.dtype) for r, a in zip(rows, arrs)],
        scratch_shapes=[pltpu.SemaphoreType.DMA((n, 7)), pltpu.SemaphoreType.DMA((n, 7)), pltpu.SemaphoreType.DMA((n,))],
    )(*arrs)


def _sibling_swap(arrs, *, name):
    n = len(arrs)

    def body(*refs):
        ins, outs = refs[:n], refs[n:2 * n]
        send, recv = refs[2 * n:]
        x, y, c = _coords()
        cps = [pltpu.make_async_remote_copy(src_ref=ins[a].at[:, 1 - c], dst_ref=outs[a], send_sem=send.at[a],
                                            recv_sem=recv.at[a], device_id=(x, y, 1 - c), device_id_type=MESH)
               for a in range(n)]
        for cp in cps:
            cp.start()
        for cp in cps:
            cp.wait_recv()
        for cp in cps:
            cp.wait_send()

    return pl.pallas_call(
        body, name=name, in_specs=[_ANY] * n, out_specs=[_ANY] * n,
        out_shape=[jax.ShapeDtypeStruct((a.shape[0],) + a.shape[2:], a.dtype) for a in arrs],
        scratch_shapes=[pltpu.SemaphoreType.DMA((n,)), pltpu.SemaphoreType.DMA((n,))],
    )(*arrs)


def _col_tile(lead, rows, cols, itemsize=4, cap=4 * _MB):
    tc = cols
    while tc % 256 == 0 and lead * rows * tc * itemsize > cap:
        tc //= 2
    return tc


def _pair_sum(mine, theirs, *, name):
    _, _, rows, cols = mine.shape
    tc = _col_tile(1, rows, cols)
    c = lax.axis_index("c")

    def body(c_ref, a_ref, b_ref, o_ref):
        o_ref[...] = (a_ref[...].astype(F32) + b_ref[...].astype(F32)).astype(MXU)

    return pl.pallas_call(
        body, name=name,
        grid_spec=pltpu.PrefetchScalarGridSpec(
            num_scalar_prefetch=1, grid=(4, cols // tc),
            in_specs=[pl.BlockSpec((None, None, rows, tc), lambda j, i, cr: (j, cr[0], 0, i)),
                      pl.BlockSpec((None, rows, tc), lambda j, i, cr: (j, 0, i))],
            out_specs=pl.BlockSpec((None, rows, tc), lambda j, i, cr: (j, 0, i))),
        out_shape=jax.ShapeDtypeStruct((4, rows, cols), MXU),
        compiler_params=_cp(("parallel", "parallel")),
    )(c.reshape(1).astype(jnp.int32), mine, theirs)


_HBM = pl.BlockSpec(memory_space=pltpu.HBM)
_SEM = pl.BlockSpec(memory_space=pltpu.SEMAPHORE)
_EFFECT = pltpu.SideEffectType.DATAFLOW_SIDE_EFFECTING
N_PEER_CHIPS = 3


def _peer_chips(x, y):
    return [(1 - x, y), (x, 1 - y), (1 - x, 1 - y)]


def _split_start(srcs, lands, src_slot, dst_slot, *, name, after=()):
    n = len(srcs)
    ns = n * N_PEER_CHIPS
    first = 2 * n + len(after)

    def body(*refs):
        src_refs, land_refs = refs[:n], refs[n:2 * n]
        send, recv, token = refs[first:first + ns], refs[first + ns:first + 2 * ns], refs[-1]
        x, y, c = _coords()
        for a in range(n):
            for k, (px, py) in enumerate(_peer_chips(x, y)):
                pltpu.make_async_remote_copy(
                    src_ref=src_refs[a].at[src_slot(x, y, c, px, py)], dst_ref=land_refs[a].at[dst_slot(x, y, c)],
                    send_sem=send[a * N_PEER_CHIPS + k], recv_sem=recv[a * N_PEER_CHIPS + k],
                    device_id=(px, py, c), device_id_type=MESH).start()
        token[...] = jnp.zeros_like(token)

    bufs = list(srcs) + list(lands)
    res = pl.pallas_call(
        body, name=name,
        out_shape=(*[pltpu.SemaphoreType.DMA(())] * (2 * ns), *[pltpu.HBM(b.shape, b.dtype) for b in bufs],
                   jax.ShapeDtypeStruct((8, 128), F32)),
        in_specs=[_HBM] * (2 * n) + [_ANY] * len(after),
        out_specs=(*[_SEM] * (2 * ns), *[_HBM] * (2 * n), pl.BlockSpec(memory_space=pltpu.VMEM)),
        input_output_aliases={i: 2 * ns + i for i in range(2 * n)},
        compiler_params=pltpu.CompilerParams(has_side_effects=_EFFECT),
    )(*[pltpu.with_memory_space_constraint(b, pltpu.HBM) for b in bufs], *after)
    sems = list(res[:2 * ns])
    return sems, list(res[2 * ns:2 * ns + n]), list(res[2 * ns + n:2 * ns + 2 * n]), res[-1]


def _split_wait(sems, srcs, lands, after, *, name):
    n = len(srcs)
    ns = n * N_PEER_CHIPS

    def body(*refs):
        src_refs, land_refs = refs[:n], refs[n:2 * n]
        send, recv = refs[2 * n:2 * n + ns], refs[2 * n + ns:2 * n + 2 * ns]
        x, y, c = _coords()
        for a in range(n):
            for k in range(N_PEER_CHIPS):
                cp = pltpu.make_async_remote_copy(
                    src_ref=src_refs[a].at[0], dst_ref=land_refs[a].at[0], send_sem=send[a * N_PEER_CHIPS + k],
                    recv_sem=recv[a * N_PEER_CHIPS + k], device_id=(x, y, 1 - c), device_id_type=MESH)
                cp.wait_send()
                cp.wait_recv()

    bufs = list(srcs) + list(lands)
    res = pl.pallas_call(
        body, name=name,
        out_shape=tuple(pltpu.HBM(b.shape, b.dtype) for b in bufs),
        in_specs=[_HBM] * (2 * n) + [_SEM] * (2 * ns) + [_ANY] * len(after),
        out_specs=tuple([_HBM] * (2 * n)),
        input_output_aliases={i: i for i in range(2 * n)},
        compiler_params=pltpu.CompilerParams(has_side_effects=_EFFECT),
    )(*bufs, *sems, *after)
    return list(res[:n]), list(res[n:])


def _fill_own(shard2, *, name, after=None):
    _, rows, cols = shard2.shape
    tc = _col_tile(1, rows, cols, itemsize=shard2.dtype.itemsize)
    j = 2 * lax.axis_index("x") + lax.axis_index("y")
    extra = [] if after is None else [after]

    def body(j_ref, s_ref, *rest):
        rest[-1][...] = s_ref[...]

    return pl.pallas_call(
        body, name=name,
        grid_spec=pltpu.PrefetchScalarGridSpec(
            num_scalar_prefetch=1, grid=(2, cols // tc),
            in_specs=([pl.BlockSpec((None, rows, tc), lambda h, i, jr: (h, 0, i))]
                      + [pl.BlockSpec(memory_space=pl.ANY)] * len(extra)),
            out_specs=pl.BlockSpec((None, rows, tc), lambda h, i, jr: (2 * jr[0] + h, 0, i))),
        out_shape=jax.ShapeDtypeStruct((8, rows, cols), shard2.dtype),
        compiler_params=_cp(("parallel", "parallel")),
    )(j.reshape(1).astype(jnp.int32), shard2, *extra)


def _pass_to_sibling(lands, *, name):
    n = len(lands)

    def body(*refs):
        outs = refs[n:2 * n]
        send, recv = refs[2 * n:]
        x, y, c = _coords()
        cps = []
        for a in range(n):
            for k, (px, py) in enumerate(_peer_chips(x, y)):
                slot = 4 * px + 2 * py + c
                cps.append(pltpu.make_async_remote_copy(
                    src_ref=outs[a].at[slot], dst_ref=outs[a].at[slot], send_sem=send.at[a, k], recv_sem=recv.at[a, k],
                    device_id=(x, y, 1 - c), device_id_type=MESH))
        for cp in cps:
            cp.start()
        for cp in cps:
            cp.wait_recv()
        for cp in cps:
            cp.wait_send()

    return pl.pallas_call(
        body, name=name, in_specs=[_ANY] * n, out_specs=[_ANY] * n,
        out_shape=[jax.ShapeDtypeStruct(b.shape, b.dtype) for b in lands],
        input_output_aliases={a: a for a in range(n)},
        scratch_shapes=[pltpu.SemaphoreType.DMA((n, N_PEER_CHIPS)), pltpu.SemaphoreType.DMA((n, N_PEER_CHIPS))],
    )(*lands)


def _sum_parts(parts, got, *, name):
    _, rows, cols = parts.shape
    tc = _col_tile(4, rows, cols, itemsize=parts.dtype.itemsize)
    x, y, c = _coords()
    idx = jnp.stack([2 * x + y, 2 * (1 - x) + y, 2 * x + (1 - y), 2 * (1 - x) + (1 - y), c]).astype(jnp.int32)

    def body(i_ref, p_ref, g0_ref, g1_ref, g2_ref, o_ref):
        o_ref[...] = ((p_ref[...].astype(F32) + g0_ref[...].astype(F32)) + g1_ref[...].astype(F32)) + g2_ref[...].astype(F32)

    slot = lambda s: pl.BlockSpec((None, rows, tc), lambda i, ir, s=s: (ir[s], 0, i))
    return pl.pallas_call(
        body, name=name,
        grid_spec=pltpu.PrefetchScalarGridSpec(
            num_scalar_prefetch=1, grid=(cols // tc,),
            in_specs=[slot(0), slot(1), slot(2), slot(3)],
            out_specs=pl.BlockSpec((None, rows, tc), lambda i, ir: (ir[4], 0, i))),
        out_shape=jax.ShapeDtypeStruct((2, rows, cols), F32),
        compiler_params=_cp(("parallel",)),
    )(idx, parts, got, got, got)


def _sum_slots(t, *, name):
    S, rows, cols = t.shape
    tc = _col_tile(S, rows, cols)

    def body(t_ref, o_ref):
        acc = t_ref[0].astype(F32)
        for s in range(1, S):
            acc = acc + t_ref[s].astype(F32)
        o_ref[...] = acc

    return pl.pallas_call(
        body, name=name, grid=(cols // tc,),
        in_specs=[pl.BlockSpec((S, rows, tc), lambda i: (0, 0, i))],
        out_specs=pl.BlockSpec((rows, tc), lambda i: (0, i)),
        out_shape=jax.ShapeDtypeStruct((rows, cols), F32),
        compiler_params=_cp(("parallel",)),
    )(t)


def _halves_join(bufs, *, name):
    n = len(bufs)

    def body(*refs):
        outs = refs[n:2 * n]
        send, recv = refs[2 * n:]
        x, y, c = _coords()
        cps = [pltpu.make_async_remote_copy(src_ref=outs[a].at[c], dst_ref=outs[a].at[c], send_sem=send.at[a],
                                            recv_sem=recv.at[a], device_id=(x, y, 1 - c), device_id_type=MESH)
               for a in range(n)]
        for cp in cps:
            cp.start()
        for cp in cps:
            cp.wait_recv()
        for cp in cps:
            cp.wait_send()

    return pl.pallas_call(
        body, name=name, in_specs=[_ANY] * n, out_specs=[_ANY] * n,
        out_shape=[jax.ShapeDtypeStruct(b.shape, b.dtype) for b in bufs],
        input_output_aliases={a: a for a in range(n)},
        scratch_shapes=[pltpu.SemaphoreType.DMA((n,)), pltpu.SemaphoreType.DMA((n,))],
    )(*bufs)


def _swap_start(srcs, *, name):
    n = len(srcs)
    lands = [lax.empty((s.shape[0],) + s.shape[2:], s.dtype) for s in srcs]

    def body(*refs):
        src_refs, land_refs = refs[:n], refs[n:2 * n]
        send, recv, token = refs[2 * n:3 * n], refs[3 * n:4 * n], refs[-1]
        x, y, c = _coords()
        for a in range(n):
            pltpu.make_async_remote_copy(src_ref=src_refs[a].at[:, 1 - c], dst_ref=land_refs[a], send_sem=send[a],
                                         recv_sem=recv[a], device_id=(x, y, 1 - c), device_id_type=MESH).start()
        token[...] = jnp.zeros_like(token)

    bufs = list(srcs) + lands
    res = pl.pallas_call(
        body, name=name,
        out_shape=(*[pltpu.SemaphoreType.DMA(())] * (2 * n), *[pltpu.HBM(b.shape, b.dtype) for b in bufs],
                   jax.ShapeDtypeStruct((8, 128), F32)),
        in_specs=[_HBM] * (2 * n),
        out_specs=(*[_SEM] * (2 * n), *[_HBM] * (2 * n), pl.BlockSpec(memory_space=pltpu.VMEM)),
        input_output_aliases={i: 2 * n + i for i in range(2 * n)},
        compiler_params=pltpu.CompilerParams(has_side_effects=_EFFECT),
    )(*[pltpu.with_memory_space_constraint(b, pltpu.HBM) for b in bufs])
    return list(res[:2 * n]), list(res[2 * n:3 * n]), list(res[3 * n:4 * n]), res[-1]


def _swap_wait(sems, srcs, lands, after, *, name):
    n = len(srcs)

    def body(*refs):
        src_refs, land_refs = refs[:n], refs[n:2 * n]
        send, recv = refs[2 * n:3 * n], refs[3 * n:4 * n]
        x, y, c = _coords()
        for a in range(n):
            cp = pltpu.make_async_remote_copy(
                src_ref=src_refs[a].at[:, 0], dst_ref=land_refs[a], send_sem=send[a], recv_sem=recv[a],
                device_id=(x, y, 1 - c), device_id_type=MESH)
            cp.wait_send()
            cp.wait_recv()

    bufs = list(srcs) + list(lands)
    res = pl.pallas_call(
        body, name=name,
        out_shape=tuple(pltpu.HBM(b.shape, b.dtype) for b in bufs),
        in_specs=[_HBM] * (2 * n) + [_SEM] * (2 * n) + [_ANY] * len(after),
        out_specs=tuple([_HBM] * (2 * n)),
        input_output_aliases={i: i for i in range(2 * n)},
        compiler_params=pltpu.CompilerParams(has_side_effects=_EFFECT),
    )(*bufs, *sems, *after)
    return list(res[:n]), list(res[n:])


def _grad_views(grads):
    return [g.reshape(4, 2, g.shape[0] // 8, g.shape[1]) for g in grads]


def _scatter_begin(views, theirs, *, tag):
    parts = [_pair_sum(v, t, name=f"rs_pair_{tag}_{i}") for i, (v, t) in enumerate(zip(views, theirs))]
    got = [lax.empty(p.shape, p.dtype) for p in parts]
    sems, parts, got, token = _split_start(
        parts, got, lambda x, y, c, px, py: 2 * px + py, lambda x, y, c: 2 * x + y, name=f"rs_start_{tag}")
    return (sems, parts, got), token


def _reduce_scatter_begin(grads, *, tag):
    views = _grad_views(grads)
    theirs = _sibling_swap(views, name=f"rs_swap_{tag}")
    return _scatter_begin(views, theirs, tag=tag)


def _reduce_scatter_end(state, after, *, tag):
    sems, parts, got = state
    parts, got = _split_wait(sems, parts, got, after, name=f"rs_wait_{tag}")
    halves = [_sum_parts(p, t, name=f"rs_sum_{tag}_{i}") for i, (p, t) in enumerate(zip(parts, got))]
    joined = _halves_join(halves, name=f"rs_join_{tag}")
    return [j.reshape(2 * j.shape[1], j.shape[2]) for j in joined]


_SMALL = ("norm_w", "ssm_a_re", "ssm_a_im", "ssm_log_dt", "ssm_b_re", "ssm_b_im", "ssm_c_re", "ssm_c_im", "ssm_d",
          "ssm_glu_b", "sg_ln_w", "sg_ln_b", "sg_w", "sg_b", "attn_sinks", "final_norm_w")
_BIG = ("w_in", "ssm_glu_w", "w_branch_a", "w_branch_b", "w_branch_c", "w_out")
_WEIGHTS = ("norm_w", "w_in", "ssm_a_re", "ssm_a_im", "ssm_log_dt", "ssm_b_re", "ssm_b_im", "ssm_c_re", "ssm_c_im",
            "ssm_d", "ssm_glu_w", "ssm_glu_b", "sg_ln_w", "sg_ln_b", "sg_w", "sg_b", "attn_sinks", "w_branch_a",
            "w_branch_b", "w_branch_c", "w_out", "final_norm_w")
_PACK_COLS = 1024
_PACK_ALIGN = 8 * ROW_ALIGN * _PACK_COLS


def _slice_exchange(buf, *, name):
    def body(in_ref, out_ref, send, recv, lsem):
        x, y, c = _coords()
        me = 4 * x + 2 * y + c
        own = pltpu.make_async_copy(in_ref.at[me], out_ref.at[me], lsem)
        own.start()
        cps = []
        for k in range(1, 8):
            px, py, pc = x ^ (k >> 2), y ^ ((k >> 1) & 1), c ^ (k & 1)
            cps.append(pltpu.make_async_remote_copy(
                src_ref=in_ref.at[4 * px + 2 * py + pc], dst_ref=out_ref.at[me], send_sem=send.at[k - 1],
                recv_sem=recv.at[k - 1], device_id=(px, py, pc), device_id_type=MESH))
        for cp in cps:
            cp.start()
        for cp in cps:
            cp.wait_recv()
        for cp in cps:
            cp.wait_send()
        own.wait()

    return pl.pallas_call(
        body, name=name, in_specs=[_ANY], out_specs=_ANY,
        out_shape=jax.ShapeDtypeStruct(buf.shape, buf.dtype),
        scratch_shapes=[pltpu.SemaphoreType.DMA((7,)), pltpu.SemaphoreType.DMA((7,)), pltpu.SemaphoreType.DMA],
    )(buf)


def _allreduce_small(packed):
    rows, cols = packed.shape
    got = _slice_exchange(packed.reshape(8, rows // 8, cols), name="small_grads_exchange")
    mine = _sum_slots(got, name="small_grads_sum")
    return _gather8([mine], name="small_grads_gather")[0]


def _pack(ts):
    flat = jnp.concatenate([t.reshape(-1) for t in ts])
    pad = (-flat.shape[0]) % _PACK_ALIGN
    return jnp.pad(flat, (0, pad)).reshape(-1, _PACK_COLS)


def _unpack(buf, like):
    flat = buf.reshape(-1)
    out, pos = [], 0
    for t in like:
        out.append(flat[pos:pos + t.size].reshape(t.shape))
        pos += t.size
    return out


def kernel(x, norm_w, w_in, ssm_a_re, ssm_a_im, ssm_log_dt, ssm_b_re, ssm_b_im, ssm_c_re, ssm_c_im, ssm_d, ssm_glu_w, ssm_glu_b, sg_ln_w, sg_ln_b, sg_w, sg_b, attn_sinks, w_branch_a, w_branch_b, w_branch_c, w_out, final_norm_w, loss_target, m_norm_w, m_w_in, m_ssm_a_re, m_ssm_a_im, m_ssm_log_dt, m_ssm_b_re, m_ssm_b_im, m_ssm_c_re, m_ssm_c_im, m_ssm_d, m_ssm_glu_w, m_ssm_glu_b, m_sg_ln_w, m_sg_ln_b, m_sg_w, m_sg_b, m_attn_sinks, m_w_branch_a, m_w_branch_b, m_w_branch_c, m_w_out, m_final_norm_w, v_norm_w, v_w_in, v_ssm_a_re, v_ssm_a_im, v_ssm_log_dt, v_ssm_b_re, v_ssm_b_im, v_ssm_c_re, v_ssm_c_im, v_ssm_d, v_ssm_glu_w, v_ssm_glu_b, v_sg_ln_w, v_sg_ln_b, v_sg_w, v_sg_b, v_attn_sinks, v_w_branch_a, v_w_branch_b, v_w_branch_c, v_w_out, v_final_norm_w):
    w = dict(norm_w=norm_w, w_in=w_in, ssm_a_re=ssm_a_re, ssm_a_im=ssm_a_im, ssm_log_dt=ssm_log_dt, ssm_b_re=ssm_b_re,
             ssm_b_im=ssm_b_im, ssm_c_re=ssm_c_re, ssm_c_im=ssm_c_im, ssm_d=ssm_d, ssm_glu_w=ssm_glu_w,
             ssm_glu_b=ssm_glu_b, sg_ln_w=sg_ln_w, sg_ln_b=sg_ln_b, sg_w=sg_w, sg_b=sg_b, attn_sinks=attn_sinks,
             w_branch_a=w_branch_a, w_branch_b=w_branch_b, w_branch_c=w_branch_c, w_out=w_out,
             final_norm_w=final_norm_w)
    m = dict(norm_w=m_norm_w, w_in=m_w_in, ssm_a_re=m_ssm_a_re, ssm_a_im=m_ssm_a_im, ssm_log_dt=m_ssm_log_dt,
             ssm_b_re=m_ssm_b_re, ssm_b_im=m_ssm_b_im, ssm_c_re=m_ssm_c_re, ssm_c_im=m_ssm_c_im, ssm_d=m_ssm_d,
             ssm_glu_w=m_ssm_glu_w, ssm_glu_b=m_ssm_glu_b, sg_ln_w=m_sg_ln_w, sg_ln_b=m_sg_ln_b, sg_w=m_sg_w,
             sg_b=m_sg_b, attn_sinks=m_attn_sinks, w_branch_a=m_w_branch_a, w_branch_b=m_w_branch_b,
             w_branch_c=m_w_branch_c, w_out=m_w_out, final_norm_w=m_final_norm_w)
    v = dict(norm_w=v_norm_w, w_in=v_w_in, ssm_a_re=v_ssm_a_re, ssm_a_im=v_ssm_a_im, ssm_log_dt=v_ssm_log_dt,
             ssm_b_re=v_ssm_b_re, ssm_b_im=v_ssm_b_im, ssm_c_re=v_ssm_c_re, ssm_c_im=v_ssm_c_im, ssm_d=v_ssm_d,
             ssm_glu_w=v_ssm_glu_w, ssm_glu_b=v_ssm_glu_b, sg_ln_w=v_sg_ln_w, sg_ln_b=v_sg_ln_b, sg_w=v_sg_w,
             sg_b=v_sg_b, attn_sinks=v_attn_sinks, w_branch_a=v_w_branch_a, w_branch_b=v_w_branch_b,
             w_branch_c=v_w_branch_c, w_out=v_w_out, final_norm_w=v_final_norm_w)

    big_names = ("winT", "glu_w", "wbaT", "wbbT", "wbcT", "w_out")
    L = x.shape[1]
    tabs = _rope_tables(L)
    p = {k: w[k] for k in _SMALL}

    column_sharded = ("w_in", "w_branch_a", "w_branch_b", "w_branch_c")

    def shard_halves(ws, names=_BIG):
        ts = [(t.T if k in column_sharded else t).astype(MXU) for k, t in zip(names, ws)]
        return [t.reshape(2, t.shape[0] // 2, t.shape[1]) for t in ts]

    my_half = lambda x_, y_, c_, px, py: c_
    my_block = lambda x_, y_, c_: 4 * x_ + 2 * y_ + c_
    rows_of = lambda lands: [t.reshape(8 * t.shape[1], t.shape[2]) for t in lands]
    fill = lambda l, hs, i0=0: [_fill_own(s, name=f"gather_fill_{l}_{i0 + i}") for i, s in enumerate(hs)]
    saved = [None] * DEPTH

    halves = [shard_halves([w[k][l] for k in _BIG]) for l in range(DEPTH)]
    w_in_0 = halves[0][:1]
    sems_a, src_a, land_a, token_a = _split_start(w_in_0, fill(0, w_in_0), my_half, my_block, name="gather_start_0a")
    fill_after = lambda l, hs, i0: [_fill_own(s, name=f"gather_fill_{l}_{i0 + i}", after=token_a)
                                    for i, s in enumerate(hs)]
    lands = [[None] + fill_after(0, halves[0][1:], 1), fill_after(1, halves[1], 0)]
    sp = [_prep_layer(p, l, after=token_a) for l in range(DEPTH)]
    h0 = _rms_fwd(x[0], p["norm_w"][0], name="rms_fwd_0", after=token_a)
    _, land_a = _split_wait(sems_a, src_a, land_a, [h0, sp[0]["btr"], sp[1]["btr"]] + lands[0][1:] + lands[1],
                            name="gather_wait_0a")
    land_a = _pass_to_sibling(land_a, name="gather_pass_0a")
    sems_b, src_b, land_b, token_b = _split_start(halves[0][1:], lands[0][1:], my_half, my_block,
                                                  name="gather_start_0b", after=land_a)
    split1 = {}

    def rest0(t):
        _, got = _split_wait(sems_b, src_b, land_b, t, name="gather_wait_0b")
        got = _pass_to_sibling(got, name="gather_pass_0b")
        split1["sems"], split1["src"], split1["land"], token1 = _split_start(halves[1], lands[1], my_half, my_block,
                                                                             name="gather_start_1", after=got)
        return dict(zip(big_names, rows_of(land_a + got))), token1

    x1, saved[0], big0 = _layer_fwd(x[0], h0, p, sp[0], rows_of(land_a)[0], rest0, 0, tabs, proj_after=token_b)
    _, lands1 = _split_wait(split1["sems"], split1["src"], split1["land"], [x1], name="gather_wait_1")
    big1 = dict(zip(big_names, rows_of(_pass_to_sibling(lands1, name="gather_pass_1"))))
    bigs = [big0, big1]
    h1 = _rms_fwd(x1, p["norm_w"][1], name="rms_fwd_1")
    x2, saved[1], _ = _layer_fwd(x1, h1, p, sp[1], big1["winT"], lambda t: (big1, None), 1, tabs)
    loss, dx, dfw = _final_loss(x2, p["final_norm_w"], loss_target[0], name="final_loss")

    grads = [None] * DEPTH
    rs = {}

    def early(l):
        def begin(g):
            rs[f"{l}a"], token_a = _reduce_scatter_begin([g[k] for k in big_names[1:]], tag=f"{l}a")
            return token_a
        return begin

    def late(l, dproj, dx):
        sems, views, lands, token_b = _swap_start(_grad_views([grads[l]["winT"]]), name=f"rs_swap_start_{l}b")
        dh = _proj_bwd_dh(dproj, bigs[l]["winT"], l, token_b)
        views, theirs = _swap_wait(sems, views, lands, [dh], name=f"rs_swap_wait_{l}b")
        rs[f"{l}b"], token_s = _scatter_begin(views, theirs, tag=f"{l}b")
        return _rms_bwd(saved[l]["x"], p["norm_w"][l], dh, dx, name=f"rms_bwd_{l}", after=token_s)

    def reduced(l, after):
        return _reduce_scatter_end(rs[f"{l}b"], after, tag=f"{l}b") + _reduce_scatter_end(rs[f"{l}a"], after, tag=f"{l}a")

    dproj, grads[1] = _layer_bwd(dx, saved[1], p, bigs[1], 1, tabs, early(1))
    dx, grads[1]["norm_w"] = late(1, dproj, dx)
    dproj, grads[0] = _layer_bwd(dx, saved[0], p, bigs[0], 0, tabs, early(0))
    red1 = reduced(1, [dproj])
    dx, grads[0]["norm_w"] = late(0, dproj, dx)

    tr = lambda t: t.transpose(0, 2, 1)
    view = {k: (tr if k == "w_in" else (lambda t: t)) for k in _BIG}
    shard_grads = lambda red: dict(zip(_BIG, (red[0], red[1], red[2].T, red[3].T, red[4].T, red[5])))
    outs = {k: None for k in _BIG}

    def adamw_big(l, red):
        for k, g in shard_grads(red).items():
            outs[k] = _adamw_layer(view[k](w[k]), g, view[k](m[k]), view[k](v[k]), l, outs[k], name=f"adamw_{k}_{l}")

    adamw_big(1, red1)

    small_like = [w[k] for k in _SMALL]
    gs = [jnp.stack([grads[l][k] for l in range(DEPTH)]) if k != "final_norm_w" else dfw for k in _SMALL]
    gsum = _allreduce_small(_pack(gs + [loss.reshape(1)]))
    adamw_big(0, reduced(0, [gsum] + [outs[k][0] for k in _BIG]))

    gfull, delta, new_m, new_v = {}, {}, {}, {}
    for k in _BIG:
        gfull[k], delta[k], new_m[k], new_v[k] = (view[k](t) for t in outs[k])
    *small_sums, loss = _unpack(gsum, small_like + [loss])
    for k, t in zip(_SMALL, small_sums):
        gfull[k] = t
        delta[k], new_m[k], new_v[k] = _adamw(w[k], t, m[k], v[k], name=f"adamw_{k}")

    return (loss, dx[None], *[gfull[k] for k in _WEIGHTS], *[delta[k] for k in _WEIGHTS],
            *[new_m[k] for k in _WEIGHTS], *[new_v[k] for k in _WEIGHTS])
```

```python
import functools
import math

import numpy as np
import jax
import jax.numpy as jnp
from jax import lax
from jax.experimental import pallas as pl
from jax.experimental.pallas import tpu as pltpu

F32 = jnp.float32
MXU = jnp.bfloat16
HIGHEST = lax.Precision.HIGHEST

D_MODEL = 2048
DEPTH = 2
EPS = 1e-6
NEG_INF = -1e30
SSM_WIDTH = 1024
SSM_GROUP = 16
SSM_GROUPS = 64
SSM_STATE = 64
SSM_CH = SSM_GROUPS * SSM_STATE
SLAB = 128
SLAB_CH = (SLAB // SSM_GROUP) * SSM_STATE
N_SLAB = SSM_WIDTH // SLAB
SCAN_SEG = 8
SCAN_STEPS = 4
SG_HEADS = 8
SG_CHUNK = 128
HEAD_DIM = 64
ATT_HEADS = 16
ATT_KV_HEADS = 2
GQA_GROUP = 8
ATT_BLOCK = 128
WINDOW = 128
ROT_DIM = 16
ROPE_THETA = 500000.0
N_MAIN = 6400
N_ZC = 1024
N_GATES = 6144
D_IN = N_MAIN + N_ZC + N_GATES

ADAM_LR = 0.001
ADAM_B1 = 0.9
ADAM_B2 = 0.999
ADAM_EPS = 1e-08
ADAM_WD = 0.01
ADAM_STEP = 10

_DIMS = {"nn": (((1,), (0,)), ((), ())), "nt": (((1,), (1,)), ((), ())), "tn": (((0,), (0,)), ((), ()))}
_MB = 1024 * 1024


def _cp(sem, vmem_mb=48):
    return pltpu.CompilerParams(dimension_semantics=sem, vmem_limit_bytes=vmem_mb * _MB)


def _dot(a, b, mode):
    return lax.dot_general(a.astype(MXU), b.astype(MXU), _DIMS[mode], preferred_element_type=F32)


@jax.custom_vjp
def _mm_nn(a, b):
    return _dot(a, b, "nn")


def _mm_nn_fwd(a, b):
    return _dot(a, b, "nn"), (a, b)


def _mm_nn_bwd(res, g):
    a, b = res
    return _dot(g, b, "nt"), _dot(a, g, "tn")


_mm_nn.defvjp(_mm_nn_fwd, _mm_nn_bwd)


@jax.custom_vjp
def _mm_nt(a, bt):
    return _dot(a, bt, "nt")


def _mm_nt_fwd(a, bt):
    return _dot(a, bt, "nt"), (a, bt)


def _mm_nt_bwd(res, g):
    a, bt = res
    return _dot(g, bt, "nn"), _dot(g, a, "tn")


_mm_nt.defvjp(_mm_nt_fwd, _mm_nt_bwd)


def _rmsnorm(x, w):
    return x * lax.rsqrt(jnp.mean(x * x, axis=-1, keepdims=True) + EPS) * w


def _layernorm(x, w, b):
    mu = jnp.mean(x, axis=-1, keepdims=True)
    var = jnp.mean(jnp.square(x - mu), axis=-1, keepdims=True)
    return (x - mu) * lax.rsqrt(var + EPS) * w + b


def _silu(x):
    return x * jax.nn.sigmoid(x)


def _matmul(a, b, mode, *, name, shape, tm, tn, tk, out_dtype=F32, add=None, a_off=(0, 0), b_off=(0, 0), after=None):
    m, n, k = shape
    tm, tn, tk = min(tm, m), min(tn, n), min(tk, k)
    assert m % tm == 0 and n % tn == 0 and k % tk == 0, (name, shape, tm, tn, tk)
    nk = k // tk
    has_add, has_after = add is not None, after is not None

    def body(*refs):
        a_ref, b_ref = refs[0], refs[1]
        pos = 2
        add_ref = None
        if has_add:
            add_ref = refs[pos]
            pos += 1
        if has_after:
            pos += 1
        o_ref = refs[pos]
        p = _dot(a_ref[...], b_ref[...], mode)
        if nk == 1:
            if has_add:
                p = p + add_ref[...].astype(F32)
            o_ref[...] = p.astype(out_dtype)
            return
        acc_ref = refs[pos + 1]
        kk = pl.program_id(2)

        @pl.when(kk == 0)
        def _():
            acc_ref[...] = p

        @pl.when(kk > 0)
        def _():
            acc_ref[...] += p

        @pl.when(kk == nk - 1)
        def _():
            r = acc_ref[...]
            if has_add:
                r = r + add_ref[...].astype(F32)
            o_ref[...] = r.astype(out_dtype)

    a0, a1 = a_off
    b0, b1 = b_off
    if mode == "tn":
        a_spec = pl.BlockSpec((tk, tm), lambda i, j, kk: (kk + a0, i + a1))
    else:
        a_spec = pl.BlockSpec((tm, tk), lambda i, j, kk: (i + a0, kk + a1))
    if mode == "nt":
        b_spec = pl.BlockSpec((tn, tk), lambda i, j, kk: (j + b0, kk + b1))
    else:
        b_spec = pl.BlockSpec((tk, tn), lambda i, j, kk: (kk + b0, j + b1))
    in_specs = [a_spec, b_spec]
    args = [a, b]
    if has_add:
        in_specs.append(pl.BlockSpec((tm, tn), lambda i, j, kk: (i, j)))
        args.append(add)
    if has_after:
        in_specs.append(pl.BlockSpec(memory_space=pl.ANY))
        args.append(after)
    return pl.pallas_call(
        body, name=name, grid=(m // tm, n // tn, nk),
        in_specs=in_specs,
        out_specs=pl.BlockSpec((tm, tn), lambda i, j, kk: (i, j)),
        out_shape=jax.ShapeDtypeStruct((m, n), out_dtype),
        scratch_shapes=[pltpu.VMEM((tm, tn), F32)] if nk > 1 else [],
        compiler_params=_cp(("parallel", "parallel", "arbitrary")),
    )(*args)


def _rms_fwd(x, w, *, name, tm=256, after=None):
    L, d = x.shape
    tm = min(tm, L)
    extra = [] if after is None else [after]

    def body(x_ref, w_ref, *rest):
        rest[-1][...] = _rmsnorm(x_ref[...], w_ref[...]).astype(MXU)

    return pl.pallas_call(
        body, name=name, grid=(L // tm,),
        in_specs=([pl.BlockSpec((tm, d), lambda i: (i, 0)), pl.BlockSpec((1, d), lambda i: (0, 0))]
                  + [pl.BlockSpec(memory_space=pl.ANY)] * len(extra)),
        out_specs=pl.BlockSpec((tm, d), lambda i: (i, 0)),
        out_shape=jax.ShapeDtypeStruct((L, d), MXU),
        compiler_params=_cp(("parallel",)),
    )(x, w.reshape(1, d), *extra)


def _rms_bwd(x, w, dh, dxn, *, name, tm=256, after=None):
    L, d = x.shape
    tm = min(tm, L)
    extra = [] if after is None else [after]

    def body(x_ref, w_ref, dh_ref, dxn_ref, *rest):
        dx_ref, dw_ref = rest[-2:]
        _, vjp = jax.vjp(_rmsnorm, x_ref[...], w_ref[...])
        dx, dw = vjp(dh_ref[...])
        dx_ref[...] = dx + dxn_ref[...]

        @pl.when(pl.program_id(0) == 0)
        def _():
            dw_ref[...] = jnp.zeros_like(dw_ref)

        dw_ref[...] += dw

    row = pl.BlockSpec((tm, d), lambda i: (i, 0))
    vec = pl.BlockSpec((1, d), lambda i: (0, 0))
    dx, dw = pl.pallas_call(
        body, name=name, grid=(L // tm,),
        in_specs=[row, vec, row, row] + [pl.BlockSpec(memory_space=pl.ANY)] * len(extra), out_specs=[row, vec],
        out_shape=[jax.ShapeDtypeStruct((L, d), F32), jax.ShapeDtypeStruct((1, d), F32)],
        compiler_params=_cp(("arbitrary",)),
    )(x, w.reshape(1, d), dh, dxn, *extra)
    return dx, dw.reshape(d)


def _final_loss(x, w, tgt, *, name, tm=256):
    L, d = x.shape
    tm = min(tm, L)

    def loss_fn(xv, wv, tv):
        err = jnp.square(_rmsnorm(xv, wv) - tv)
        return 0.5 * jnp.sum(jnp.mean(err, axis=-1, keepdims=True), axis=0, keepdims=True)

    def body(x_ref, w_ref, t_ref, loss_ref, dx_ref, dw_ref):
        tv = t_ref[...]
        val, vjp = jax.vjp(lambda xv, wv: loss_fn(xv, wv, tv), x_ref[...], w_ref[...])
        dx, dw = vjp(jnp.ones((1, 1), F32))
        dx_ref[...] = dx

        @pl.when(pl.program_id(0) == 0)
        def _():
            dw_ref[...] = jnp.zeros_like(dw_ref)
            loss_ref[...] = jnp.zeros_like(loss_ref)

        dw_ref[...] += dw
        loss_ref[...] += jnp.broadcast_to(val, loss_ref.shape)

    row = pl.BlockSpec((tm, d), lambda i: (i, 0))
    vec = pl.BlockSpec((1, d), lambda i: (0, 0))
    loss, dx, dw = pl.pallas_call(
        body, name=name, grid=(L // tm,),
        in_specs=[row, vec, row],
        out_specs=[pl.BlockSpec((8, 128), lambda i: (0, 0)), row, vec],
        out_shape=[jax.ShapeDtypeStruct((8, 128), F32), jax.ShapeDtypeStruct((L, d), F32),
                   jax.ShapeDtypeStruct((1, d), F32)],
        compiler_params=_cp(("arbitrary",)),
    )(x, w.reshape(1, d), tgt)
    return loss[0, 0], dx, dw.reshape(d)


PARAM_ROWS = 512


def _s5_param_fn(are, aim, ldt, bre, bim, row0):
    n = are.shape[0]
    grp = (row0 + lax.broadcasted_iota(jnp.int32, (n, SSM_GROUPS), 0)) // SSM_STATE
    col = lax.broadcasted_iota(jnp.int32, (n, SSM_GROUPS), 1)
    sel = (grp == col).astype(F32)
    dt = jnp.sum(sel * jnp.exp(ldt), axis=-1, keepdims=True)
    mag = jnp.exp(are * dt)
    ang = aim * dt
    lbr = mag * jnp.cos(ang)
    lbi = mag * jnp.sin(ang)
    den = are * are + aim * aim
    nr = lbr - 1.0
    kr = (nr * are + lbi * aim) / den
    ki = (lbi * are - nr * aim) / den
    return lbr, lbi, kr * bre - ki * bim, kr * bim + ki * bre


def _s5_param_specs():
    col = pl.BlockSpec((PARAM_ROWS, 1), lambda i: (i, 0))
    mat = pl.BlockSpec((PARAM_ROWS, SSM_GROUP), lambda i: (i, 0))
    vec = pl.BlockSpec((1, SSM_GROUPS), lambda i: (0, 0))
    return col, mat, vec


def _s5_params_fwd(are, aim, ldt, bre, bim, *, name, after=None):
    n = are.shape[0]
    col, mat, vec = _s5_param_specs()
    extra = [] if after is None else [after]

    def body(are_ref, aim_ref, ldt_ref, bre_ref, bim_ref, *rest):
        lbr_ref, lbi_ref, bbr_ref, bbi_ref = rest[-4:]
        row0 = pl.program_id(0) * PARAM_ROWS
        lbr, lbi, bbr, bbi = _s5_param_fn(are_ref[...], aim_ref[...], ldt_ref[...], bre_ref[...], bim_ref[...], row0)
        lbr_ref[...] = lbr
        lbi_ref[...] = lbi
        bbr_ref[...] = bbr
        bbi_ref[...] = bbi

    cshape = jax.ShapeDtypeStruct((n, 1), F32)
    mshape = jax.ShapeDtypeStruct((n, SSM_GROUP), F32)
    return pl.pallas_call(body, name=name, grid=(n // PARAM_ROWS,),
                          in_specs=[col, col, vec, mat, mat] + [pl.BlockSpec(memory_space=pl.ANY)] * len(extra),
                          out_specs=[col, col, mat, mat], out_shape=[cshape, cshape, mshape, mshape],
                          compiler_params=_cp(("parallel",)))(are, aim, ldt, bre, bim, *extra)


def _s5_params_bwd(are, aim, ldt, bre, bim, dlbr, dlbi, dbbr, dbbi, *, name):
    n = are.shape[0]
    col, mat, vec = _s5_param_specs()

    def body(are_ref, aim_ref, ldt_ref, bre_ref, bim_ref, g0, g1, g2, g3, o0, o1, o2, o3, o4):
        row0 = pl.program_id(0) * PARAM_ROWS
        _, vjp = jax.vjp(lambda a, b, c, d, e: _s5_param_fn(a, b, c, d, e, row0),
                         are_ref[...], aim_ref[...], ldt_ref[...], bre_ref[...], bim_ref[...])
        dare, daim, dldt, dbre, dbim = vjp((g0[...], g1[...], g2[...], g3[...]))
        o0[...] = dare
        o1[...] = daim
        o3[...] = dbre
        o4[...] = dbim

        @pl.when(pl.program_id(0) == 0)
        def _():
            o2[...] = jnp.zeros_like(o2)

        o2[...] += dldt

    cshape = jax.ShapeDtypeStruct((n, 1), F32)
    mshape = jax.ShapeDtypeStruct((n, SSM_GROUP), F32)
    return pl.pallas_call(body, name=name, grid=(n // PARAM_ROWS,),
                          in_specs=[col, col, vec, mat, mat, col, col, mat, mat],
                          out_specs=[col, col, vec, mat, mat],
                          out_shape=[cshape, cshape, jax.ShapeDtypeStruct((1, SSM_GROUPS), F32), mshape, mshape],
                          compiler_params=_cp(("arbitrary",)))(are, aim, ldt, bre, bim, dlbr, dlbi, dbbr, dbbi)


SLAB_NC = SLAB_CH // 128


def _s5_specs(L):
    slab = pl.BlockSpec((L, SLAB), lambda s: (0, s))
    wspec = pl.BlockSpec((SLAB_NC, 128, SLAB), lambda s: (s, 0, 0))
    lspec = pl.BlockSpec((SLAB_NC, 1, 128), lambda s: (s, 0, 0))
    sspec = pl.BlockSpec((SLAB_NC, L, 128), lambda s: (s, 0, 0))
    dspec = pl.BlockSpec((1, SLAB), lambda s: (0, s))
    return slab, wspec, lspec, sspec, dspec


def _scan_inplace(sr_ref, si_ref, lr, li, pr_ref, pi_ref, *, reverse):
    NC, L, W = sr_ref.shape
    S = SCAN_SEG
    T = L // S
    lr8 = [jnp.broadcast_to(lr[k], (S, W)) for k in range(NC)]
    li8 = [jnp.broadcast_to(li[k], (S, W)) for k in range(NC)]

    def tiles(first, count):
        return pl.ds(first * S, count * S)

    for k in range(NC):
        pr_ref[k, tiles(T - 1 if reverse else 0, 1), :] = lr8[k]
        pi_ref[k, tiles(T - 1 if reverse else 0, 1), :] = li8[k]
        n = 1
        while n < T:
            have = tiles(T - n, n) if reverse else tiles(0, n)
            new = tiles(T - 2 * n, n) if reverse else tiles(n, n)
            top = tiles(T - n, 1) if reverse else tiles(n - 1, 1)
            ar, ai = pr_ref[k, top, :][None], pi_ref[k, top, :][None]
            hr, hi = pr_ref[k, have, :].reshape(n, S, W), pi_ref[k, have, :].reshape(n, S, W)
            pr_ref[k, new, :] = (hr * ar - hi * ai).reshape(n * S, W)
            pi_ref[k, new, :] = (hr * ai + hi * ar).reshape(n * S, W)
            n *= 2

    def step(i, carry):
        for u in range(SCAN_STEPS):
            jj = i * SCAN_STEPS + u
            rows = pl.ds(pl.multiple_of(((T - 1 - jj) if reverse else jj) * S, S), S)
            out = []
            for k in range(NC):
                sr, si = carry[k]
                nsr = lr8[k] * sr - li8[k] * si + sr_ref[k, rows, :]
                nsi = lr8[k] * si + li8[k] * sr + si_ref[k, rows, :]
                sr_ref[k, rows, :] = nsr
                si_ref[k, rows, :] = nsi
                out.append((nsr, nsi))
            carry = tuple(out)
        return carry

    zero = jnp.zeros((S, W), F32)
    ends = lax.fori_loop(0, T // SCAN_STEPS, step, tuple((zero, zero) for k in range(NC)))
    sub = lax.broadcasted_iota(jnp.int32, (S, W), 0)
    order = range(S - 1, -1, -1) if reverse else range(S)
    for k in range(NC):
        er, ei = ends[k]
        full = tiles(0 if reverse else T - 1, 1)
        ltr = pr_ref[k, full, :][0:1]
        lti = pi_ref[k, full, :][0:1]
        cr = jnp.zeros((1, W), F32)
        ci = jnp.zeros((1, W), F32)
        ctr = jnp.zeros((S, W), F32)
        cti = jnp.zeros((S, W), F32)
        for seg in order:
            ctr = jnp.where(sub == seg, cr, ctr)
            cti = jnp.where(sub == seg, ci, cti)
            cr, ci = (er[seg:seg + 1, :] + ltr * cr - lti * ci, ei[seg:seg + 1, :] + ltr * ci + lti * cr)
        pr = pr_ref[k].reshape(T, S, W)
        pi = pi_ref[k].reshape(T, S, W)
        sr_ref[k] += (pr * ctr[None] - pi * cti[None]).reshape(L, W)
        si_ref[k] += (pr * cti[None] + pi * ctr[None]).reshape(L, W)


def _time_interleave(a):
    L, W = a.shape
    return a.reshape(SCAN_SEG, L // SCAN_SEG, W).transpose(1, 0, 2).reshape(L, W)


def _time_deinterleave(a):
    L, W = a.shape
    return a.reshape(L // SCAN_SEG, SCAN_SEG, W).transpose(1, 0, 2).reshape(L, W)


def _s5_fwd(u, btr, bti, cbr, cbi, lbr, lbi, dvec, *, name, after=None):
    L = u.shape[0]
    extra = [] if after is None else [after]

    def body(u_ref, btr_ref, bti_ref, cbr_ref, cbi_ref, lr_ref, li_ref, d_ref, *rest):
        ys_ref, sr_ref, si_ref, pr_ref, pi_ref = rest[-5:]
        u = u_ref[...]
        for k in range(SLAB_NC):
            sr_ref[k] = _dot(u, btr_ref[k], "nt")
            si_ref[k] = _dot(u, bti_ref[k], "nt")
        _scan_inplace(sr_ref, si_ref, lr_ref[...], li_ref[...], pr_ref, pi_ref, reverse=False)
        ys = d_ref[...] * u
        for k in range(SLAB_NC):
            ys = ys + _dot(sr_ref[k], cbr_ref[k], "nn") - _dot(si_ref[k], cbi_ref[k], "nn")
        ys_ref[...] = ys

    slab, wspec, lspec, sspec, dspec = _s5_specs(L)
    sshape = jax.ShapeDtypeStruct((N_SLAB * SLAB_NC, L, 128), F32)
    return pl.pallas_call(
        body, name=name, grid=(N_SLAB,),
        in_specs=[slab, wspec, wspec, wspec, wspec, lspec, lspec, dspec] + [pl.BlockSpec(memory_space=pl.ANY)] * len(extra),
        out_specs=[slab, sspec, sspec],
        out_shape=[jax.ShapeDtypeStruct((L, SSM_WIDTH), F32), sshape, sshape],
        scratch_shapes=[pltpu.VMEM((SLAB_NC, L, 128), F32), pltpu.VMEM((SLAB_NC, L, 128), F32)],
        compiler_params=_cp(("parallel",), 56),
    )(u, btr, bti, cbr, cbi, lbr, lbi, dvec, *extra)


def _s5_bwd(dys, u, sr, si, btr, bti, cbr, cbi, lbr, lbi, dvec, *, name, after=None):
    L = u.shape[0]
    S = SCAN_SEG
    extra = [] if after is None else [after]

    def body(dys_ref, u_ref, sr_ref, si_ref, btr_ref, bti_ref, cbr_ref, cbi_ref, lr_ref, li_ref, d_ref, *rest):
        (du_ref, dbtr_ref, dbti_ref, dcbr_ref, dcbi_ref, dlr_ref, dli_ref, dd_ref,
         ar_ref, ai_ref, pr_ref, pi_ref) = rest[-12:]
        dys = dys_ref[...]
        u = u_ref[...]
        for k in range(SLAB_NC):
            ar_ref[k] = _dot(dys, cbr_ref[k], "nt")
            ai_ref[k] = -_dot(dys, cbi_ref[k], "nt")
        _scan_inplace(ar_ref, ai_ref, lr_ref[...], -li_ref[...], pr_ref, pi_ref, reverse=True)
        head = lax.broadcasted_iota(jnp.int32, (L, 1), 0) < S
        sub0 = lax.broadcasted_iota(jnp.int32, (S, 1), 0) == 0

        def prev_state(s):
            up = pltpu.roll(s, S, 0)
            return jnp.where(head, 0.0, up), jnp.where(sub0, 0.0, pltpu.roll(up[0:S], 1, 0))

        du = d_ref[...] * dys
        for k in range(SLAB_NC):
            a_re = ar_ref[k]
            a_im = ai_ref[k]
            du = du + _dot(a_re, btr_ref[k], "nn") + _dot(a_im, bti_ref[k], "nn")
            dbtr_ref[k] = _dot(a_re, u, "tn")
            dbti_ref[k] = _dot(a_im, u, "tn")
            s_re = sr_ref[k]
            s_im = si_ref[k]
            dcbr_ref[k] = _dot(s_re, dys, "tn")
            dcbi_ref[k] = -_dot(s_im, dys, "tn")
            p_re, q_re = prev_state(s_re)
            p_im, q_im = prev_state(s_im)
            b_re, b_im = a_re[0:S], a_im[0:S]
            dlr_ref[k] = (jnp.sum(p_re * a_re + p_im * a_im, axis=0, keepdims=True)
                          + jnp.sum(q_re * b_re + q_im * b_im, axis=0, keepdims=True))
            dli_ref[k] = (jnp.sum(p_re * a_im - p_im * a_re, axis=0, keepdims=True)
                          + jnp.sum(q_re * b_im - q_im * b_re, axis=0, keepdims=True))
        du_ref[...] = du
        dd_ref[...] = jnp.sum(dys * u, axis=0, keepdims=True)

    slab, wspec, lspec, sspec, dspec = _s5_specs(L)
    wshape = jax.ShapeDtypeStruct((N_SLAB * SLAB_NC, 128, SLAB), F32)
    lshape = jax.ShapeDtypeStruct((N_SLAB * SLAB_NC, 1, 128), F32)
    return pl.pallas_call(
        body, name=name, grid=(N_SLAB,),
        in_specs=([slab, slab, sspec, sspec, wspec, wspec, wspec, wspec, lspec, lspec, dspec]
                  + [pl.BlockSpec(memory_space=pl.ANY)] * len(extra)),
        out_specs=[slab, wspec, wspec, wspec, wspec, lspec, lspec, dspec],
        out_shape=[jax.ShapeDtypeStruct((L, SSM_WIDTH), F32), wshape, wshape, wshape, wshape, lshape, lshape,
                   jax.ShapeDtypeStruct((1, SSM_WIDTH), F32)],
        scratch_shapes=[pltpu.VMEM((SLAB_NC, L, 128), F32)] * 4,
        compiler_params=_cp(("parallel",), 56),
    )(dys, u, sr, si, btr, bti, cbr, cbi, lbr, lbi, dvec, *extra)


_SLAB_MASK = (np.arange(SLAB_CH)[:, None] // SSM_STATE == np.arange(SLAB)[None, :] // SSM_GROUP)


def _expand_bd(x):
    t = jnp.tile(x.reshape(N_SLAB, SLAB_CH, SSM_GROUP), (1, 1, SLAB // SSM_GROUP))
    return jnp.where(_SLAB_MASK[None], t, 0.0).astype(MXU).reshape(N_SLAB * SLAB_NC, 128, SLAB)


def _contract_bd(dx):
    t = jnp.where(_SLAB_MASK[None], dx.reshape(N_SLAB, SLAB_CH, SLAB), 0.0)
    return jnp.sum(t.reshape(N_SLAB, SLAB_CH, SLAB // SSM_GROUP, SSM_GROUP), axis=2).reshape(SSM_CH, SSM_GROUP)


def _glu_ew(ys, zlin, za):
    a1 = jax.nn.gelu(ys)
    return a1 * jax.nn.sigmoid(zlin) * _silu(za)


def _glu_fwd(ys, main, gw, gb, *, name, tm=256):
    L = ys.shape[0]
    tm = min(tm, L)
    W = SSM_WIDTH

    def body(ys_ref, za_ref, gw_ref, gb_ref, ya_ref):
        ys = ys_ref[...]
        a1 = jax.nn.gelu(ys)
        zlin = _dot(a1, gw_ref[...], "nn") + gb_ref[...]
        ya_ref[...] = _glu_ew(ys, zlin, za_ref[...]).astype(MXU)

    return pl.pallas_call(
        body, name=name, grid=(L // tm,),
        in_specs=[pl.BlockSpec((tm, W), lambda i: (i, 0)), pl.BlockSpec((tm, W), lambda i: (i, 1)),
                  pl.BlockSpec((W, W), lambda i: (0, 0)), pl.BlockSpec((1, W), lambda i: (0, 0))],
        out_specs=pl.BlockSpec((tm, W), lambda i: (i, 0)),
        out_shape=jax.ShapeDtypeStruct((L, W), MXU),
        compiler_params=_cp(("parallel",)),
    )(ys, main, gw, gb.reshape(1, W))


def _glu_bwd(dya, ys, main, gw, gb, *, name, tm=256):
    L = ys.shape[0]
    tm = min(tm, L)
    W = SSM_WIDTH

    def body(dya_ref, ys_ref, za_ref, gw_ref, gb_ref, dys_ref, dza_ref, a1_ref, dzl_ref, db_ref):
        ys = ys_ref[...]
        a1, gelu_vjp = jax.vjp(jax.nn.gelu, ys)
        zlin = _dot(a1, gw_ref[...], "nn") + gb_ref[...]
        _, vjp = jax.vjp(lambda a, z, za: a * jax.nn.sigmoid(z) * _silu(za), a1, zlin, za_ref[...])
        da1, dzlin, dza = vjp(dya_ref[...].astype(F32))
        da1 = da1 + _dot(dzlin, gw_ref[...], "nt")
        dys_ref[...] = gelu_vjp(da1)[0]
        dza_ref[...] = dza
        a1_ref[...] = a1.astype(MXU)
        dzl_ref[...] = dzlin.astype(MXU)

        @pl.when(pl.program_id(0) == 0)
        def _():
            db_ref[...] = jnp.zeros_like(db_ref)

        db_ref[...] += jnp.sum(dzlin, axis=0, keepdims=True)

    row = pl.BlockSpec((tm, W), lambda i: (i, 0))
    vec = pl.BlockSpec((1, W), lambda i: (0, 0))
    return pl.pallas_call(
        body, name=name, grid=(L // tm,),
        in_specs=[row, row, pl.BlockSpec((tm, W), lambda i: (i, 1)), pl.BlockSpec((W, W), lambda i: (0, 0)), vec],
        out_specs=[row, row, row, row, vec],
        out_shape=[jax.ShapeDtypeStruct((L, W), F32), jax.ShapeDtypeStruct((L, W), F32),
                   jax.ShapeDtypeStruct((L, W), MXU), jax.ShapeDtypeStruct((L, W), MXU),
                   jax.ShapeDtypeStruct((1, W), F32)],
        compiler_params=_cp(("arbitrary",)),
    )(dya, ys, main, gw, gb.reshape(1, W))


def _sg_fn(ub, vb, zb, lnw, lnb, ws, bs):
    u = jax.nn.gelu(ub)
    v = _layernorm(jax.nn.gelu(vb), lnw, lnb)
    r = lax.broadcasted_iota(jnp.int32, (SG_CHUNK, SG_CHUNK), 0)
    c = lax.broadcasted_iota(jnp.int32, (SG_CHUNK, SG_CHUNK), 1)
    tri = r >= c
    outs = []
    for h in range(SG_HEADS):
        wh = jnp.where(tri, ws[h], 0.0)
        outs.append(_mm_nn(wh, v[:, h * 128:(h + 1) * 128]) + bs[h])
    mixed = jnp.concatenate(outs, axis=1)
    return u * mixed * _silu(zb)


def _sg_specs(L):
    W = SSM_WIDTH
    blk = lambda c: pl.BlockSpec((SG_CHUNK, W), lambda i, c=c: (i, c))
    vec = pl.BlockSpec((1, W), lambda i: (0, 0))
    wspec = pl.BlockSpec((SG_HEADS, SG_CHUNK, SG_CHUNK), lambda i: (0, 0, 0))
    bspec = pl.BlockSpec((SG_HEADS, SG_CHUNK, 1), lambda i: (0, 0, 0))
    return blk, vec, wspec, bspec


def _sg_fwd(main, lnw, lnb, sgw, sgb, *, name):
    L = main.shape[0]
    W = SSM_WIDTH
    blk, vec, wspec, bspec = _sg_specs(L)

    def body(ub_ref, vb_ref, zb_ref, lnw_ref, lnb_ref, w_ref, b_ref, yb_ref):
        ws = [w_ref[h] for h in range(SG_HEADS)]
        bs = [b_ref[h] for h in range(SG_HEADS)]
        yb_ref[...] = _sg_fn(ub_ref[...], vb_ref[...], zb_ref[...], lnw_ref[...], lnb_ref[...], ws, bs).astype(MXU)

    return pl.pallas_call(
        body, name=name, grid=(L // SG_CHUNK,),
        in_specs=[blk(2), blk(3), blk(4), vec, vec, wspec, bspec],
        out_specs=pl.BlockSpec((SG_CHUNK, W), lambda i: (i, 0)),
        out_shape=jax.ShapeDtypeStruct((L, W), MXU),
        compiler_params=_cp(("parallel",)),
    )(main, main, main, lnw.reshape(1, W), lnb.reshape(1, W), sgw, sgb.reshape(SG_HEADS, SG_CHUNK, 1))


def _sg_bwd(dyb, main, lnw, lnb, sgw, sgb, *, name):
    L = main.shape[0]
    W = SSM_WIDTH
    blk, vec, wspec, bspec = _sg_specs(L)

    def body(dyb_ref, ub_ref, vb_ref, zb_ref, lnw_ref, lnb_ref, w_ref, b_ref,
             dub_ref, dvb_ref, dzb_ref, dlnw_ref, dlnb_ref, dw_ref, db_ref):
        ws = [w_ref[h] for h in range(SG_HEADS)]
        bs = [b_ref[h] for h in range(SG_HEADS)]
        _, vjp = jax.vjp(_sg_fn, ub_ref[...], vb_ref[...], zb_ref[...], lnw_ref[...], lnb_ref[...], ws, bs)
        dub, dvb, dzb, dlnw, dlnb, dws, dbs = vjp(dyb_ref[...])

        @pl.when(pl.program_id(0) == 0)
        def _():
            dlnw_ref[...] = jnp.zeros_like(dlnw_ref)
            dlnb_ref[...] = jnp.zeros_like(dlnb_ref)
            dw_ref[...] = jnp.zeros_like(dw_ref)
            db_ref[...] = jnp.zeros_like(db_ref)

        dub_ref[...] = dub
        dvb_ref[...] = dvb
        dzb_ref[...] = dzb
        dlnw_ref[...] += dlnw
        dlnb_ref[...] += dlnb
        for h in range(SG_HEADS):
            dw_ref[h] += dws[h]
            db_ref[h] += dbs[h]

    row = pl.BlockSpec((SG_CHUNK, W), lambda i: (i, 0))
    out = jax.ShapeDtypeStruct((L, W), F32)
    return pl.pallas_call(
        body, name=name, grid=(L // SG_CHUNK,),
        in_specs=[row, blk(2), blk(3), blk(4), vec, vec, wspec, bspec],
        out_specs=[row, row, row, vec, vec, wspec, bspec],
        out_shape=[out, out, out, jax.ShapeDtypeStruct((1, W), F32), jax.ShapeDtypeStruct((1, W), F32),
                   jax.ShapeDtypeStruct((SG_HEADS, SG_CHUNK, SG_CHUNK), F32),
                   jax.ShapeDtypeStruct((SG_HEADS, SG_CHUNK, 1), F32)],
        compiler_params=_cp(("arbitrary",)),
    )(dyb, main, main, main, lnw.reshape(1, W), lnb.reshape(1, W), sgw, sgb.reshape(SG_HEADS, SG_CHUNK, 1))


def _rope_tables(L):
    half = ROT_DIM // 2
    inv_freq = ROPE_THETA ** (-jnp.arange(0, ROT_DIM, 2, dtype=F32) / ROT_DIM)
    ang = jnp.arange(L, dtype=F32)[:, None] * inv_freq[None, :]
    cos = jnp.cos(ang)
    sin = jnp.sin(ang)
    ones = jnp.ones((L, HEAD_DIM - ROT_DIM), F32)
    cosf = jnp.concatenate([cos, cos, ones], axis=1)
    sinf = jnp.concatenate([sin, sin, 0.0 * ones], axis=1)
    rot = np.zeros((HEAD_DIM, HEAD_DIM), np.float32)
    for d in range(half):
        rot[d + half, d] = -1.0
        rot[d, d + half] = 1.0
    return cosf, sinf, jnp.asarray(rot)


def _rope(t, cosf, sinf, rot):
    shp = t.shape
    t2 = t.reshape(-1, HEAD_DIM)
    sw = lax.dot_general(t2, rot, _DIMS["nn"], precision=lax.Precision.HIGH, preferred_element_type=F32).reshape(shp)
    return t * cosf + sw * sinf


def _softmax_sink_parts(s, sink):
    m = jnp.maximum(jnp.max(s, axis=-1, keepdims=True), sink)
    e = jnp.exp(s - m)
    es = jnp.exp(sink - m)
    r = 1.0 / (jnp.sum(e, axis=-1, keepdims=True) + es)
    return e * r, es * r


@jax.custom_vjp
def _softmax_sink(s, sink):
    return _softmax_sink_parts(s, sink)[0]


def _softmax_sink_fwd(s, sink):
    p, p_sink = _softmax_sink_parts(s, sink)
    return p, (p, p_sink)


def _softmax_sink_bwd(res, dp):
    p, p_sink = res
    t = jnp.sum(p * dp, axis=-1, keepdims=True)
    return p * (dp - t), -jnp.sum(p_sink * t, axis=1, keepdims=True)


_softmax_sink.defvjp(_softmax_sink_fwd, _softmax_sink_bwd)


def _attn_block_fn(q, kw, vw, sinks, cq, sq, ck, sk, rot, q0, k0):
    nk = kw.shape[1]
    qr = _rope(q, cq, sq, rot)
    kr = _rope(kw, ck, sk, rot)
    qpos = q0 + lax.broadcasted_iota(jnp.int32, (1, ATT_BLOCK, nk), 1)
    kpos = k0 + lax.broadcasted_iota(jnp.int32, (1, ATT_BLOCK, nk), 2)
    diff = qpos - kpos
    allowed = (diff >= 0) & (diff < WINDOW)
    outs = []
    for kh in range(ATT_KV_HEADS):
        qh = qr[kh * GQA_GROUP:(kh + 1) * GQA_GROUP].reshape(GQA_GROUP * ATT_BLOCK, HEAD_DIM)
        s = _mm_nt(qh, kr[kh]).reshape(GQA_GROUP, ATT_BLOCK, nk) * (HEAD_DIM ** -0.5)
        s = jnp.where(allowed, s, NEG_INF)
        p = _softmax_sink(s, sinks[kh * GQA_GROUP:(kh + 1) * GQA_GROUP])
        o = _mm_nn(p.reshape(GQA_GROUP * ATT_BLOCK, nk), vw[kh])
        outs.append(o.reshape(GQA_GROUP, ATT_BLOCK, HEAD_DIM))
    return jnp.concatenate(outs, axis=0)


def _attn_common(L):
    nwin = min(2 * ATT_BLOCK, L)
    qspec = pl.BlockSpec((ATT_HEADS, ATT_BLOCK, HEAD_DIM), lambda n: (0, n, 0))
    kvspec = pl.BlockSpec((ATT_KV_HEADS, L, HEAD_DIM), lambda n: (0, 0, 0))
    sspec = pl.BlockSpec((ATT_HEADS, 1, 1), lambda n: (0, 0, 0))
    tq = pl.BlockSpec((ATT_BLOCK, HEAD_DIM), lambda n: (n, 0))
    tk = pl.BlockSpec((L, HEAD_DIM), lambda n: (0, 0))
    rspec = pl.BlockSpec((HEAD_DIM, HEAD_DIM), lambda n: (0, 0))
    return nwin, qspec, kvspec, sspec, tq, tk, rspec


def _attn_fwd(qh, kh, vh, sinks, cosf, sinf, rot, *, name):
    L = qh.shape[1]
    nwin, qspec, kvspec, sspec, tq, tk, rspec = _attn_common(L)

    def body(q_ref, k_ref, v_ref, s_ref, cq_ref, sq_ref, ck_ref, sk_ref, r_ref, o_ref):
        n = pl.program_id(0)
        k0 = pl.multiple_of(jnp.maximum(n - 1, 0) * ATT_BLOCK, ATT_BLOCK)
        win = pl.ds(k0, nwin)
        o_ref[...] = _attn_block_fn(q_ref[...], k_ref[:, win, :], v_ref[:, win, :], s_ref[...],
                                    cq_ref[...], sq_ref[...], ck_ref[win, :], sk_ref[win, :], r_ref[...],
                                    n * ATT_BLOCK, k0)

    return pl.pallas_call(
        body, name=name, grid=(L // ATT_BLOCK,),
        in_specs=[qspec, kvspec, kvspec, sspec, tq, tq, tk, tk, rspec],
        out_specs=qspec,
        out_shape=jax.ShapeDtypeStruct((ATT_HEADS, L, HEAD_DIM), F32),
        compiler_params=_cp(("parallel",)),
    )(qh, kh, vh, sinks.reshape(ATT_HEADS, 1, 1), cosf, sinf, cosf, sinf, rot)


def _attn_bwd(do, qh, kh, vh, sinks, cosf, sinf, rot, *, name):
    L = qh.shape[1]
    nwin, qspec, kvspec, sspec, tq, tk, rspec = _attn_common(L)

    def body(do_ref, q_ref, k_ref, v_ref, s_ref, cq_ref, sq_ref, ck_ref, sk_ref, r_ref,
             dq_ref, dk_ref, dv_ref, ds_ref):
        n = pl.program_id(0)
        k0 = pl.multiple_of(jnp.maximum(n - 1, 0) * ATT_BLOCK, ATT_BLOCK)
        win = pl.ds(k0, nwin)
        cq, sq, ck, sk, rt = cq_ref[...], sq_ref[...], ck_ref[win, :], sk_ref[win, :], r_ref[...]
        q0 = n * ATT_BLOCK
        _, vjp = jax.vjp(lambda q, kw, vw, s: _attn_block_fn(q, kw, vw, s, cq, sq, ck, sk, rt, q0, k0),
                         q_ref[...], k_ref[:, win, :], v_ref[:, win, :], s_ref[...])
        dq, dkw, dvw, ds = vjp(do_ref[...])

        @pl.when(n == 0)
        def _():
            dk_ref[...] = jnp.zeros_like(dk_ref)
            dv_ref[...] = jnp.zeros_like(dv_ref)
            ds_ref[...] = jnp.zeros_like(ds_ref)

        dq_ref[...] = dq
        dk_ref[:, win, :] += dkw
        dv_ref[:, win, :] += dvw
        ds_ref[...] += ds

    return pl.pallas_call(
        body, name=name, grid=(L // ATT_BLOCK,),
        in_specs=[qspec, qspec, kvspec, kvspec, sspec, tq, tq, tk, tk, rspec],
        out_specs=[qspec, kvspec, kvspec, sspec],
        out_shape=[jax.ShapeDtypeStruct((ATT_HEADS, L, HEAD_DIM), F32),
                   jax.ShapeDtypeStruct((ATT_KV_HEADS, L, HEAD_DIM), F32),
                   jax.ShapeDtypeStruct((ATT_KV_HEADS, L, HEAD_DIM), F32),
                   jax.ShapeDtypeStruct((ATT_HEADS, 1, 1), F32)],
        compiler_params=_cp(("arbitrary",)),
    )(do, qh, kh, vh, sinks.reshape(ATT_HEADS, 1, 1), cosf, sinf, cosf, sinf, rot)


def _to_heads(t, nh):
    L = t.shape[0]
    return t.reshape(L, nh, HEAD_DIM).transpose(1, 0, 2)


def _from_heads(t):
    nh, L, _ = t.shape
    return t.transpose(1, 0, 2).reshape(L, nh * HEAD_DIM)


def _branch_fwd(ya, yb, o2d, zc, gates, wa, wb, wc, *, name, tm=256):
    L = ya.shape[0]
    tm = min(tm, L)
    W, D = SSM_WIDTH, D_MODEL

    def body(ya_ref, yb_ref, o_ref, zc_ref, g0_ref, g1_ref, g2_ref, wa_ref, wb_ref, wc_ref,
             mg_ref, ta_ref, tb_ref, tc_ref, yc_ref):
        yc = (o_ref[...] * _silu(zc_ref[...])).astype(MXU)
        ta = _dot(ya_ref[...], wa_ref[...], "nt")
        tb = _dot(yb_ref[...], wb_ref[...], "nt")
        tc = _dot(yc, wc_ref[...], "nt")
        ta_ref[...] = ta
        tb_ref[...] = tb
        tc_ref[...] = tc
        yc_ref[...] = yc
        mg_ref[...] = (jax.nn.sigmoid(g0_ref[...]) * ta + jax.nn.sigmoid(g1_ref[...]) * tb
                       + jax.nn.sigmoid(g2_ref[...]) * tc).astype(MXU)

    row = pl.BlockSpec((tm, W), lambda i: (i, 0))
    wide = pl.BlockSpec((tm, D), lambda i: (i, 0))
    gate = lambda c: pl.BlockSpec((tm, D), lambda i, c=c: (i, c))
    wspec = pl.BlockSpec((D, W), lambda i: (0, 0))
    return pl.pallas_call(
        body, name=name, grid=(L // tm,),
        in_specs=[row, row, row, row, gate(0), gate(1), gate(2), wspec, wspec, wspec],
        out_specs=[wide, wide, wide, wide, row],
        out_shape=[jax.ShapeDtypeStruct((L, D), MXU), jax.ShapeDtypeStruct((L, D), F32),
                   jax.ShapeDtypeStruct((L, D), F32), jax.ShapeDtypeStruct((L, D), F32),
                   jax.ShapeDtypeStruct((L, W), MXU)],
        compiler_params=_cp(("parallel",), 56),
    )(ya, yb, o2d, zc, gates, gates, gates, wa, wb, wc)


def _branch_bwd(dmg, ta, tb, tc, gates, *, name, tm=256):
    L = dmg.shape[0]
    tm = min(tm, L)
    D = D_MODEL

    def body(dm_ref, ta_ref, tb_ref, tc_ref, g0_ref, g1_ref, g2_ref, da_ref, db_ref, dc_ref, dg_ref):
        dm = dm_ref[...]
        for i, (t_ref, g_ref, d_ref) in enumerate(((ta_ref, g0_ref, da_ref), (tb_ref, g1_ref, db_ref),
                                                   (tc_ref, g2_ref, dc_ref))):
            sg = jax.nn.sigmoid(g_ref[...])
            d_ref[...] = (sg * dm).astype(MXU)
            dg_ref[:, i * D:(i + 1) * D] = (dm * t_ref[...] * sg * (1.0 - sg)).astype(MXU)

    wide = pl.BlockSpec((tm, D), lambda i: (i, 0))
    gate = lambda c: pl.BlockSpec((tm, D), lambda i, c=c: (i, c))
    bf = jax.ShapeDtypeStruct((L, D), MXU)
    return pl.pallas_call(
        body, name=name, grid=(L // tm,),
        in_specs=[wide, wide, wide, wide, gate(0), gate(1), gate(2)],
        out_specs=[wide, wide, wide, pl.BlockSpec((tm, 3 * D), lambda i: (i, 0))],
        out_shape=[bf, bf, bf, jax.ShapeDtypeStruct((L, 3 * D), MXU)],
        compiler_params=_cp(("parallel",), 56),
    )(dmg, ta, tb, tc, gates, gates, gates)


def _gate_c_bwd(dyc, o2d, zc, *, name, tm=256):
    L, W = dyc.shape
    tm = min(tm, L)

    def body(dy_ref, o_ref, z_ref, do_ref, dz_ref):
        _, vjp = jax.vjp(lambda o, z: o * _silu(z), o_ref[...], z_ref[...])
        do, dz = vjp(dy_ref[...])
        do_ref[...] = do
        dz_ref[...] = dz.astype(MXU)

    row = pl.BlockSpec((tm, W), lambda i: (i, 0))
    return pl.pallas_call(body, name=name, grid=(L // tm,), in_specs=[row, row, row], out_specs=[row, row],
                          out_shape=[jax.ShapeDtypeStruct((L, W), F32), jax.ShapeDtypeStruct((L, W), MXU)],
                          compiler_params=_cp(("parallel",)))(dyc, o2d, zc)


def _adamw(w, g, m, v, *, name):
    shape = w.shape
    cols = shape[-1]
    w2, g2, m2, v2 = (t.reshape(-1, cols) for t in (w, g, m, v))
    rows = w2.shape[0]
    tc = 1024 if cols % 1024 == 0 else cols
    lane_cols = -(-tc // 128) * 128
    tr = rows
    while tr % 16 == 0 and tr * lane_cols * 4 > 2 * _MB:
        tr //= 2

    def body(w_ref, g_ref, m_ref, v_ref, d_ref, nm_ref, nv_ref):
        gv = g_ref[...]
        nm = ADAM_B1 * m_ref[...] + (1.0 - ADAM_B1) * gv
        nv = ADAM_B2 * v_ref[...] + (1.0 - ADAM_B2) * jnp.square(gv)
        m_hat = nm / (1.0 - ADAM_B1 ** ADAM_STEP)
        v_hat = nv / (1.0 - ADAM_B2 ** ADAM_STEP)
        d_ref[...] = -ADAM_LR * (m_hat / (jnp.sqrt(v_hat) + ADAM_EPS) + ADAM_WD * w_ref[...])
        nm_ref[...] = nm
        nv_ref[...] = nv

    spec = pl.BlockSpec((tr, tc), lambda i, j: (i, j))
    out = jax.ShapeDtypeStruct((rows, cols), F32)
    d, nm, nv = pl.pallas_call(body, name=name, grid=(rows // tr, cols // tc), in_specs=[spec] * 4,
                               out_specs=[spec] * 3, out_shape=[out, out, out],
                               compiler_params=_cp(("parallel", "parallel")))(w2, g2, m2, v2)
    return d.reshape(shape), nm.reshape(shape), nv.reshape(shape)


def _adamw_layer(w, g, m, v, l, prev, *, name):
    _, rows, cols = w.shape
    tc = 1024 if cols % 1024 == 0 else cols
    tr = rows
    while tr % 16 == 0 and tr * tc * 4 > 2 * _MB:
        tr //= 2

    def body(w_ref, g_ref, m_ref, v_ref, *rest):
        go_ref, d_ref, nm_ref, nv_ref = rest[-4:]
        gv = g_ref[...]
        nm = ADAM_B1 * m_ref[...] + (1.0 - ADAM_B1) * gv
        nv = ADAM_B2 * v_ref[...] + (1.0 - ADAM_B2) * jnp.square(gv)
        m_hat = nm / (1.0 - ADAM_B1 ** ADAM_STEP)
        v_hat = nv / (1.0 - ADAM_B2 ** ADAM_STEP)
        d_ref[...] = -ADAM_LR * (m_hat / (jnp.sqrt(v_hat) + ADAM_EPS) + ADAM_WD * w_ref[...])
        nm_ref[...] = nm
        nv_ref[...] = nv
        go_ref[...] = gv

    lspec = pl.BlockSpec((None, tr, tc), lambda i, j: (l, i, j))
    gspec = pl.BlockSpec((tr, tc), lambda i, j: (i, j))
    out = jax.ShapeDtypeStruct(w.shape, F32)
    extra = [] if prev is None else list(prev)
    return pl.pallas_call(
        body, name=name, grid=(rows // tr, cols // tc),
        in_specs=[lspec, gspec, lspec, lspec] + [_ANY] * len(extra),
        out_specs=[lspec] * 4, out_shape=[out] * 4,
        input_output_aliases={4 + i: i for i in range(len(extra))},
        compiler_params=_cp(("parallel", "parallel")),
    )(w, g, m, v, *extra)


def _prep_layer(p, l, after=None):
    are = p["ssm_a_re"][l].reshape(SSM_CH, 1)
    aim = p["ssm_a_im"][l].reshape(SSM_CH, 1)
    ldt = p["ssm_log_dt"][l].reshape(1, SSM_GROUPS)
    bre = p["ssm_b_re"][l].reshape(SSM_CH, SSM_GROUP)
    bim = p["ssm_b_im"][l].reshape(SSM_CH, SSM_GROUP)
    lbr, lbi, bbr, bbi = _s5_params_fwd(are, aim, ldt, bre, bim, name=f"s5_params_fwd_{l}", after=after)
    cre = p["ssm_c_re"][l].transpose(0, 2, 1).reshape(SSM_CH, SSM_GROUP)
    cim = p["ssm_c_im"][l].transpose(0, 2, 1).reshape(SSM_CH, SSM_GROUP)
    return dict(raw=(are, aim, ldt, bre, bim),
                lbr=lbr.reshape(N_SLAB * SLAB_NC, 1, 128), lbi=lbi.reshape(N_SLAB * SLAB_NC, 1, 128),
                btr=_expand_bd(bbr), bti=_expand_bd(bbi), cbr=_expand_bd(cre), cbi=_expand_bd(cim),
                dvec=p["ssm_d"][l].reshape(1, SSM_WIDTH))


def _layer_fwd(x, h, p, sp, winT, rest_of, l, tabs, proj_after=None):
    L = x.shape[0]
    cosf, sinf, rot = tabs
    mm = functools.partial(_matmul, h, winT, "nt", tm=L, tn=256, tk=D_MODEL, after=proj_after)
    main = mm(name=f"proj_main_{l}", shape=(L, N_MAIN, D_MODEL))
    zc = mm(name=f"proj_zc_{l}", shape=(L, N_ZC, D_MODEL), b_off=(N_MAIN // 256, 0))
    gates = mm(name=f"proj_gates_{l}", shape=(L, N_GATES, D_MODEL), b_off=((N_MAIN + N_ZC) // 256, 0))
    big, token = rest_of([main, zc, gates])
    ua = _time_interleave(main[:, :SSM_WIDTH])
    ys, sr, si = _s5_fwd(ua, sp["btr"], sp["bti"], sp["cbr"], sp["cbi"], sp["lbr"], sp["lbi"], sp["dvec"],
                         name=f"s5_fwd_{l}", after=token)
    ys = _time_deinterleave(ys)
    ya = _glu_fwd(ys, main, big["glu_w"], p["ssm_glu_b"][l], name=f"glu_fwd_{l}")
    yb = _sg_fwd(main, p["sg_ln_w"][l], p["sg_ln_b"][l], p["sg_w"][l], p["sg_b"][l], name=f"sg_fwd_{l}")
    qh = _to_heads(main[:, 5120:6144], ATT_HEADS)
    kh = _to_heads(main[:, 6144:6272], ATT_KV_HEADS)
    vh = _to_heads(main[:, 6272:6400], ATT_KV_HEADS)
    oh = _attn_fwd(qh, kh, vh, p["attn_sinks"][l], cosf, sinf, rot, name=f"attn_fwd_{l}")
    o2d = _from_heads(oh)
    mg, ta, tb, tc, yc = _branch_fwd(ya, yb, o2d, zc, gates, big["wbaT"], big["wbbT"], big["wbcT"],
                                     name=f"branch_fwd_{l}")
    xn = _matmul(mg, big["w_out"], "nn", name=f"out_fwd_{l}", shape=(L, D_MODEL, D_MODEL), tm=512, tn=512,
                 tk=D_MODEL, add=x)
    saved = dict(x=x, h=h, main=main, zc=zc, gates=gates, ua=ua, ys=ys, sr=sr, si=si, ya=ya, yb=yb, yc=yc, o2d=o2d,
                 qh=qh, kh=kh, vh=vh, mg=mg, ta=ta, tb=tb, tc=tc, sp=sp)
    return xn, saved, big


def _layer_bwd(dxn, s, p, big, l, tabs, early):
    L = dxn.shape[0]
    D, W = D_MODEL, SSM_WIDTH
    cosf, sinf, rot = tabs
    sp = s["sp"]
    g = {}
    dmg = _matmul(dxn, big["w_out"], "nt", name=f"out_bwd_dm_{l}", shape=(L, D, D), tm=512, tn=512, tk=D)
    g["w_out"] = _matmul(s["mg"], dxn, "tn", name=f"out_bwd_dw_{l}", shape=(D, D, L), tm=512, tn=512, tk=L,
                         out_dtype=MXU)
    dta, dtb, dtc, dgates = _branch_bwd(dmg, s["ta"], s["tb"], s["tc"], s["gates"], name=f"branch_bwd_{l}")
    dys_ = {}
    for nm, dt, y, wt in (("a", dta, s["ya"], big["wbaT"]), ("b", dtb, s["yb"], big["wbbT"]),
                          ("c", dtc, s["yc"], big["wbcT"])):
        dys_[nm] = _matmul(dt, wt, "nn", name=f"branch_bwd_dy{nm}_{l}", shape=(L, W, D), tm=512, tn=512, tk=D)
        g["wb" + nm + "T"] = _matmul(dt, y, "tn", name=f"branch_bwd_dw{nm}_{l}", shape=(D, W, L),
                                     tm=512, tn=512, tk=L, out_dtype=MXU)
    do2d, dzc = _gate_c_bwd(dys_["c"], s["o2d"], s["zc"], name=f"gate_c_bwd_{l}")
    dqh, dkh, dvh, dsinks = _attn_bwd(_to_heads(do2d, ATT_HEADS), s["qh"], s["kh"], s["vh"], p["attn_sinks"][l],
                                      cosf, sinf, rot, name=f"attn_bwd_{l}")
    g["attn_sinks"] = dsinks.reshape(ATT_HEADS)
    dub, dvb, dzb, dlnw, dlnb, dsgw, dsgb = _sg_bwd(dys_["b"], s["main"], p["sg_ln_w"][l], p["sg_ln_b"][l],
                                                    p["sg_w"][l], p["sg_b"][l], name=f"sg_bwd_{l}")
    g["sg_ln_w"], g["sg_ln_b"] = dlnw.reshape(W), dlnb.reshape(W)
    g["sg_w"], g["sg_b"] = dsgw, dsgb.reshape(SG_HEADS, SG_CHUNK)
    dys, dza, a1, dzl, dgb = _glu_bwd(dys_["a"], s["ys"], s["main"], big["glu_w"], p["ssm_glu_b"][l],
                                      name=f"glu_bwd_{l}")
    g["ssm_glu_b"] = dgb.reshape(W)
    g["glu_w"] = _matmul(a1, dzl, "tn", name=f"glu_bwd_dw_{l}", shape=(W, W, L), tm=512, tn=512, tk=L, out_dtype=MXU)
    token = early(g)
    dua, dbtr, dbti, dcbr, dcbi, dlr, dli, dd = _s5_bwd(_time_interleave(dys), s["ua"], s["sr"], s["si"], sp["btr"],
                                                        sp["bti"], sp["cbr"], sp["cbi"], sp["lbr"], sp["lbi"],
                                                        sp["dvec"], name=f"s5_bwd_{l}", after=token)
    dua = _time_deinterleave(dua)
    g["ssm_d"] = dd.reshape(W)
    to_c = lambda t: _contract_bd(t).reshape(SSM_GROUPS, SSM_STATE, SSM_GROUP).transpose(0, 2, 1)
    g["ssm_c_re"], g["ssm_c_im"] = to_c(dcbr), to_c(dcbi)
    dare, daim, dldt, dbre, dbim = _s5_params_bwd(*sp["raw"], dlr.reshape(SSM_CH, 1), dli.reshape(SSM_CH, 1),
                                                  _contract_bd(dbtr), _contract_bd(dbti),
                                                  name=f"s5_params_bwd_{l}")
    g["ssm_a_re"] = dare.reshape(SSM_GROUPS, SSM_STATE)
    g["ssm_a_im"] = daim.reshape(SSM_GROUPS, SSM_STATE)
    g["ssm_log_dt"] = dldt.reshape(SSM_GROUPS)
    g["ssm_b_re"] = dbre.reshape(SSM_GROUPS, SSM_STATE, SSM_GROUP)
    g["ssm_b_im"] = dbim.reshape(SSM_GROUPS, SSM_STATE, SSM_GROUP)
    dproj = jnp.concatenate([t.astype(MXU) for t in (dua, dza, dub, dvb, dzb, _from_heads(dqh), _from_heads(dkh),
                                                     _from_heads(dvh), dzc, dgates)], axis=1)
    g["winT"] = _matmul(dproj, s["h"], "tn", name=f"proj_bwd_dw_{l}", shape=(D_IN, D, L), tm=256, tn=D, tk=L,
                        out_dtype=MXU)
    return dproj, g


def _proj_bwd_dh(dproj, winT, l, after):
    return _matmul(dproj, winT, "nn", name=f"proj_bwd_dh_{l}", shape=(dproj.shape[0], D_MODEL, D_IN), tm=512, tn=512,
                   tk=D_IN // 2, after=after)


MESH = pl.DeviceIdType.MESH
_ANY = pl.BlockSpec(memory_space=pl.ANY)
ROW_ALIGN = 16


def _coords():
    return lax.axis_index("x"), lax.axis_index("y"), lax.axis_index("c")


def _gather8(arrs, *, name):
    n = len(arrs)
    rows = [a.shape[0] for a in arrs]
    for r in rows:
        assert r % ROW_ALIGN == 0

    def body(*refs):
        ins, outs = refs[:n], refs[n:2 * n]
        send, recv, lsem = refs[2 * n:]
        x, y, c = _coords()
        me, sibling = (x, y, c), (x, y, 1 - c)
        chips = [(1 - x, y), (x, 1 - y), (1 - x, 1 - y)]

        def blk(a, px, py, pc):
            return outs[a].at[pl.ds(pl.multiple_of((4 * px + 2 * py + pc) * rows[a], ROW_ALIGN), rows[a]), :]

        def own(a):
            return ins[a]

        def copy(a, k, block, to, src=None):
            return pltpu.make_async_remote_copy(
                src_ref=blk(a, *block) if src is None else src, dst_ref=blk(a, *block),
                send_sem=send.at[a, k], recv_sem=recv.at[a, k], device_id=to, device_id_type=MESH)

        mine, first, passed = [], [], []
        for a in range(n):
            mine.append(pltpu.make_async_copy(own(a), blk(a, *me), lsem.at[a]))
            mine[a].start()
            f = [copy(a, 0, me, sibling, src=own(a))]
            f += [copy(a, 1 + j, me, (*chip, c), src=own(a)) for j, chip in enumerate(chips)]
            for cp in f:
                cp.start()
            first.append(f)
        for a in range(n):
            ps = [copy(a, 4 + j, (*chip, c), sibling) for j, chip in enumerate(chips)]
            for j, chip in enumerate(chips):
                copy(a, 1 + j, (*chip, c), me).wait_recv()
                ps[j].start()
            passed.append(ps)
        for a in range(n):
            copy(a, 0, sibling, me).wait_recv()
            for j, chip in enumerate(chips):
                copy(a, 4 + j, (*chip, 1 - c), me).wait_recv()
            for cp in first[a] + passed[a]:
                cp.wait_send()
            mine[a].wait()

    return pl.pallas_call(
        body, name=name,
        in_specs=[_ANY] * n, out_specs=[_ANY] * n,
        out_shape=[jax.ShapeDtypeStruct((8 * r,) + a.shape[1:], a.dtype) for r, a in zip(rows, arrs)],
        scratch_shapes=[pltpu.SemaphoreType.DMA((n, 7)), pltpu.SemaphoreType.DMA((n, 7)), pltpu.SemaphoreType.DMA((n,))],
    )(*arrs)


def _sibling_swap(arrs, *, name):
    n = len(arrs)

    def body(*refs):
        ins, outs = refs[:n], refs[n:2 * n]
        send, recv = refs[2 * n:]
        x, y, c = _coords()
        cps = [pltpu.make_async_remote_copy(src_ref=ins[a].at[:, 1 - c], dst_ref=outs[a], send_sem=send.at[a],
                                            recv_sem=recv.at[a], device_id=(x, y, 1 - c), device_id_type=MESH)
               for a in range(n)]
        for cp in cps:
            cp.start()
        for cp in cps:
            cp.wait_recv()
        for cp in cps:
            cp.wait_send()

    return pl.pallas_call(
        body, name=name, in_specs=[_ANY] * n, out_specs=[_ANY] * n,
        out_shape=[jax.ShapeDtypeStruct((a.shape[0],) + a.shape[2:], a.dtype) for a in arrs],
        scratch_shapes=[pltpu.SemaphoreType.DMA((n,)), pltpu.SemaphoreType.DMA((n,))],
    )(*arrs)


def _col_tile(lead, rows, cols, itemsize=4, cap=4 * _MB):
    tc = cols
    while tc % 256 == 0 and lead * rows * tc * itemsize > cap:
        tc //= 2
    return tc


def _pair_sum(mine, theirs, *, name):
    _, _, rows, cols = mine.shape
    tc = _col_tile(1, rows, cols)
    c = lax.axis_index("c")

    def body(c_ref, a_ref, b_ref, o_ref):
        o_ref[...] = (a_ref[...].astype(F32) + b_ref[...].astype(F32)).astype(MXU)

    return pl.pallas_call(
        body, name=name,
        grid_spec=pltpu.PrefetchScalarGridSpec(
            num_scalar_prefetch=1, grid=(4, cols // tc),
            in_specs=[pl.BlockSpec((None, None, rows, tc), lambda j, i, cr: (j, cr[0], 0, i)),
                      pl.BlockSpec((None, rows, tc), lambda j, i, cr: (j, 0, i))],
            out_specs=pl.BlockSpec((None, rows, tc), lambda j, i, cr: (j, 0, i))),
        out_shape=jax.ShapeDtypeStruct((4, rows, cols), MXU),
        compiler_params=_cp(("parallel", "parallel")),
    )(c.reshape(1).astype(jnp.int32), mine, theirs)


_HBM = pl.BlockSpec(memory_space=pltpu.HBM)
_SEM = pl.BlockSpec(memory_space=pltpu.SEMAPHORE)
_EFFECT = pltpu.SideEffectType.DATAFLOW_SIDE_EFFECTING
N_PEER_CHIPS = 3


def _peer_chips(x, y):
    return [(1 - x, y), (x, 1 - y), (1 - x, 1 - y)]


def _split_start(srcs, lands, src_slot, dst_slot, *, name, after=()):
    n = len(srcs)
    ns = n * N_PEER_CHIPS
    first = 2 * n + len(after)

    def body(*refs):
        src_refs, land_refs = refs[:n], refs[n:2 * n]
        send, recv, token = refs[first:first + ns], refs[first + ns:first + 2 * ns], refs[-1]
        x, y, c = _coords()
        for a in range(n):
            for k, (px, py) in enumerate(_peer_chips(x, y)):
                pltpu.make_async_remote_copy(
                    src_ref=src_refs[a].at[src_slot(x, y, c, px, py)], dst_ref=land_refs[a].at[dst_slot(x, y, c)],
                    send_sem=send[a * N_PEER_CHIPS + k], recv_sem=recv[a * N_PEER_CHIPS + k],
                    device_id=(px, py, c), device_id_type=MESH).start()
        token[...] = jnp.zeros_like(token)

    bufs = list(srcs) + list(lands)
    res = pl.pallas_call(
        body, name=name,
        out_shape=(*[pltpu.SemaphoreType.DMA(())] * (2 * ns), *[pltpu.HBM(b.shape, b.dtype) for b in bufs],
                   jax.ShapeDtypeStruct((8, 128), F32)),
        in_specs=[_HBM] * (2 * n) + [_ANY] * len(after),
        out_specs=(*[_SEM] * (2 * ns), *[_HBM] * (2 * n), pl.BlockSpec(memory_space=pltpu.VMEM)),
        input_output_aliases={i: 2 * ns + i for i in range(2 * n)},
        compiler_params=pltpu.CompilerParams(has_side_effects=_EFFECT),
    )(*[pltpu.with_memory_space_constraint(b, pltpu.HBM) for b in bufs], *after)
    sems = list(res[:2 * ns])
    return sems, list(res[2 * ns:2 * ns + n]), list(res[2 * ns + n:2 * ns + 2 * n]), res[-1]


def _split_wait(sems, srcs, lands, after, *, name):
    n = len(srcs)
    ns = n * N_PEER_CHIPS

    def body(*refs):
        src_refs, land_refs = refs[:n], refs[n:2 * n]
        send, recv = refs[2 * n:2 * n + ns], refs[2 * n + ns:2 * n + 2 * ns]
        x, y, c = _coords()
        for a in range(n):
            for k in range(N_PEER_CHIPS):
                cp = pltpu.make_async_remote_copy(
                    src_ref=src_refs[a].at[0], dst_ref=land_refs[a].at[0], send_sem=send[a * N_PEER_CHIPS + k],
                    recv_sem=recv[a * N_PEER_CHIPS + k], device_id=(x, y, 1 - c), device_id_type=MESH)
                cp.wait_send()
                cp.wait_recv()

    bufs = list(srcs) + list(lands)
    res = pl.pallas_call(
        body, name=name,
        out_shape=tuple(pltpu.HBM(b.shape, b.dtype) for b in bufs),
        in_specs=[_HBM] * (2 * n) + [_SEM] * (2 * ns) + [_ANY] * len(after),
        out_specs=tuple([_HBM] * (2 * n)),
        input_output_aliases={i: i for i in range(2 * n)},
        compiler_params=pltpu.CompilerParams(has_side_effects=_EFFECT),
    )(*bufs, *sems, *after)
    return list(res[:n]), list(res[n:])


def _fill_own(shard2, *, name, after=None):
    _, rows, cols = shard2.shape
    tc = _col_tile(1, rows, cols, itemsize=shard2.dtype.itemsize)
    j = 2 * lax.axis_index("x") + lax.axis_index("y")
    extra = [] if after is None else [after]

    def body(j_ref, s_ref, *rest):
        rest[-1][...] = s_ref[...]

    return pl.pallas_call(
        body, name=name,
        grid_spec=pltpu.PrefetchScalarGridSpec(
            num_scalar_prefetch=1, grid=(2, cols // tc),
            in_specs=([pl.BlockSpec((None, rows, tc), lambda h, i, jr: (h, 0, i))]
                      + [pl.BlockSpec(memory_space=pl.ANY)] * len(extra)),
            out_specs=pl.BlockSpec((None, rows, tc), lambda h, i, jr: (2 * jr[0] + h, 0, i))),
        out_shape=jax.ShapeDtypeStruct((8, rows, cols), shard2.dtype),
        compiler_params=_cp(("parallel", "parallel")),
    )(j.reshape(1).astype(jnp.int32), shard2, *extra)


def _pass_to_sibling(lands, *, name):
    n = len(lands)

    def body(*refs):
        outs = refs[n:2 * n]
        send, recv = refs[2 * n:]
        x, y, c = _coords()
        cps = []
        for a in range(n):
            for k, (px, py) in enumerate(_peer_chips(x, y)):
                slot = 4 * px + 2 * py + c
                cps.append(pltpu.make_async_remote_copy(
                    src_ref=outs[a].at[slot], dst_ref=outs[a].at[slot], send_sem=send.at[a, k], recv_sem=recv.at[a, k],
                    device_id=(x, y, 1 - c), device_id_type=MESH))
        for cp in cps:
            cp.start()
        for cp in cps:
            cp.wait_recv()
        for cp in cps:
            cp.wait_send()

    return pl.pallas_call(
        body, name=name, in_specs=[_ANY] * n, out_specs=[_ANY] * n,
        out_shape=[jax.ShapeDtypeStruct(b.shape, b.dtype) for b in lands],
        input_output_aliases={a: a for a in range(n)},
        scratch_shapes=[pltpu.SemaphoreType.DMA((n, N_PEER_CHIPS)), pltpu.SemaphoreType.DMA((n, N_PEER_CHIPS))],
    )(*lands)


def _sum_parts(parts, got, *, name):
    _, rows, cols = parts.shape
    tc = _col_tile(4, rows, cols, itemsize=parts.dtype.itemsize)
    x, y, c = _coords()
    idx = jnp.stack([2 * x + y, 2 * (1 - x) + y, 2 * x + (1 - y), 2 * (1 - x) + (1 - y), c]).astype(jnp.int32)

    def body(i_ref, p_ref, g0_ref, g1_ref, g2_ref, o_ref):
        o_ref[...] = ((p_ref[...].astype(F32) + g0_ref[...].astype(F32)) + g1_ref[...].astype(F32)) + g2_ref[...].astype(F32)

    slot = lambda s: pl.BlockSpec((None, rows, tc), lambda i, ir, s=s: (ir[s], 0, i))
    return pl.pallas_call(
        body, name=name,
        grid_spec=pltpu.PrefetchScalarGridSpec(
            num_scalar_prefetch=1, grid=(cols // tc,),
            in_specs=[slot(0), slot(1), slot(2), slot(3)],
            out_specs=pl.BlockSpec((None, rows, tc), lambda i, ir: (ir[4], 0, i))),
        out_shape=jax.ShapeDtypeStruct((2, rows, cols), F32),
        compiler_params=_cp(("parallel",)),
    )(idx, parts, got, got, got)


def _sum_slots(t, *, name):
    S, rows, cols = t.shape
    tc = _col_tile(S, rows, cols)

    def body(t_ref, o_ref):
        acc = t_ref[0].astype(F32)
        for s in range(1, S):
            acc = acc + t_ref[s].astype(F32)
        o_ref[...] = acc

    return pl.pallas_call(
        body, name=name, grid=(cols // tc,),
        in_specs=[pl.BlockSpec((S, rows, tc), lambda i: (0, 0, i))],
        out_specs=pl.BlockSpec((rows, tc), lambda i: (0, i)),
        out_shape=jax.ShapeDtypeStruct((rows, cols), F32),
        compiler_params=_cp(("parallel",)),
    )(t)


def _halves_join(bufs, *, name):
    n = len(bufs)

    def body(*refs):
        outs = refs[n:2 * n]
        send, recv = refs[2 * n:]
        x, y, c = _coords()
        cps = [pltpu.make_async_remote_copy(src_ref=outs[a].at[c], dst_ref=outs[a].at[c], send_sem=send.at[a],
                                            recv_sem=recv.at[a], device_id=(x, y, 1 - c), device_id_type=MESH)
               for a in range(n)]
        for cp in cps:
            cp.start()
        for cp in cps:
            cp.wait_recv()
        for cp in cps:
            cp.wait_send()

    return pl.pallas_call(
        body, name=name, in_specs=[_ANY] * n, out_specs=[_ANY] * n,
        out_shape=[jax.ShapeDtypeStruct(b.shape, b.dtype) for b in bufs],
        input_output_aliases={a: a for a in range(n)},
        scratch_shapes=[pltpu.SemaphoreType.DMA((n,)), pltpu.SemaphoreType.DMA((n,))],
    )(*bufs)


def _swap_start(srcs, *, name):
    n = len(srcs)
    lands = [lax.empty((s.shape[0],) + s.shape[2:], s.dtype) for s in srcs]

    def body(*refs):
        src_refs, land_refs = refs[:n], refs[n:2 * n]
        send, recv, token = refs[2 * n:3 * n], refs[3 * n:4 * n], refs[-1]
        x, y, c = _coords()
        for a in range(n):
            pltpu.make_async_remote_copy(src_ref=src_refs[a].at[:, 1 - c], dst_ref=land_refs[a], send_sem=send[a],
                                         recv_sem=recv[a], device_id=(x, y, 1 - c), device_id_type=MESH).start()
        token[...] = jnp.zeros_like(token)

    bufs = list(srcs) + lands
    res = pl.pallas_call(
        body, name=name,
        out_shape=(*[pltpu.SemaphoreType.DMA(())] * (2 * n), *[pltpu.HBM(b.shape, b.dtype) for b in bufs],
                   jax.ShapeDtypeStruct((8, 128), F32)),
        in_specs=[_HBM] * (2 * n),
        out_specs=(*[_SEM] * (2 * n), *[_HBM] * (2 * n), pl.BlockSpec(memory_space=pltpu.VMEM)),
        input_output_aliases={i: 2 * n + i for i in range(2 * n)},
        compiler_params=pltpu.CompilerParams(has_side_effects=_EFFECT),
    )(*[pltpu.with_memory_space_constraint(b, pltpu.HBM) for b in bufs])
    return list(res[:2 * n]), list(res[2 * n:3 * n]), list(res[3 * n:4 * n]), res[-1]


def _swap_wait(sems, srcs, lands, after, *, name):
    n = len(srcs)

    def body(*refs):
        src_refs, land_refs = refs[:n], refs[n:2 * n]
        send, recv = refs[2 * n:3 * n], refs[3 * n:4 * n]
        x, y, c = _coords()
        for a in range(n):
            cp = pltpu.make_async_remote_copy(
                src_ref=src_refs[a].at[:, 0], dst_ref=land_refs[a], send_sem=send[a], recv_sem=recv[a],
                device_id=(x, y, 1 - c), device_id_type=MESH)
            cp.wait_send()
            cp.wait_recv()

    bufs = list(srcs) + list(lands)
    res = pl.pallas_call(
        body, name=name,
        out_shape=tuple(pltpu.HBM(b.shape, b.dtype) for b in bufs),
        in_specs=[_HBM] * (2 * n) + [_SEM] * (2 * n) + [_ANY] * len(after),
        out_specs=tuple([_HBM] * (2 * n)),
        input_output_aliases={i: i for i in range(2 * n)},
        compiler_params=pltpu.CompilerParams(has_side_effects=_EFFECT),
    )(*bufs, *sems, *after)
    return list(res[:n]), list(res[n:])


def _grad_views(grads):
    return [g.reshape(4, 2, g.shape[0] // 8, g.shape[1]) for g in grads]


def _scatter_begin(views, theirs, *, tag):
    parts = [_pair_sum(v, t, name=f"rs_pair_{tag}_{i}") for i, (v, t) in enumerate(zip(views, theirs))]
    got = [lax.empty(p.shape, p.dtype) for p in parts]
    sems, parts, got, token = _split_start(
        parts, got, lambda x, y, c, px, py: 2 * px + py, lambda x, y, c: 2 * x + y, name=f"rs_start_{tag}")
    return (sems, parts, got), token


def _reduce_scatter_begin(grads, *, tag):
    views = _grad_views(grads)
    theirs = _sibling_swap(views, name=f"rs_swap_{tag}")
    return _scatter_begin(views, theirs, tag=tag)


def _reduce_scatter_end(state, after, *, tag):
    sems, parts, got = state
    parts, got = _split_wait(sems, parts, got, after, name=f"rs_wait_{tag}")
    halves = [_sum_parts(p, t, name=f"rs_sum_{tag}_{i}") for i, (p, t) in enumerate(zip(parts, got))]
    joined = _halves_join(halves, name=f"rs_join_{tag}")
    return [j.reshape(2 * j.shape[1], j.shape[2]) for j in joined]


_SMALL = ("norm_w", "ssm_a_re", "ssm_a_im", "ssm_log_dt", "ssm_b_re", "ssm_b_im", "ssm_c_re", "ssm_c_im", "ssm_d",
          "ssm_glu_b", "sg_ln_w", "sg_ln_b", "sg_w", "sg_b", "attn_sinks", "final_norm_w")
_BIG = ("w_in", "ssm_glu_w", "w_branch_a", "w_branch_b", "w_branch_c", "w_out")
_WEIGHTS = ("norm_w", "w_in", "ssm_a_re", "ssm_a_im", "ssm_log_dt", "ssm_b_re", "ssm_b_im", "ssm_c_re", "ssm_c_im",
            "ssm_d", "ssm_glu_w", "ssm_glu_b", "sg_ln_w", "sg_ln_b", "sg_w", "sg_b", "attn_sinks", "w_branch_a",
            "w_branch_b", "w_branch_c", "w_out", "final_norm_w")
_PACK_COLS = 1024
_PACK_ALIGN = 8 * ROW_ALIGN * _PACK_COLS


def _slice_exchange(buf, *, name, after=()):
    def body(in_ref, *rest):
        out_ref, send, recv, lsem = rest[-4:]
        x, y, c = _coords()
        me = 4 * x + 2 * y + c
        own = pltpu.make_async_copy(in_ref.at[me], out_ref.at[me], lsem)
        own.start()
        cps = []
        for k in range(1, 8):
            px, py, pc = x ^ (k >> 2), y ^ ((k >> 1) & 1), c ^ (k & 1)
            cps.append(pltpu.make_async_remote_copy(
                src_ref=in_ref.at[4 * px + 2 * py + pc], dst_ref=out_ref.at[me], send_sem=send.at[k - 1],
                recv_sem=recv.at[k - 1], device_id=(px, py, pc), device_id_type=MESH))
        for cp in cps:
            cp.start()
        for cp in cps:
            cp.wait_recv()
        for cp in cps:
            cp.wait_send()
        own.wait()

    return pl.pallas_call(
        body, name=name, in_specs=[_ANY] * (1 + len(after)), out_specs=_ANY,
        out_shape=jax.ShapeDtypeStruct(buf.shape, buf.dtype),
        scratch_shapes=[pltpu.SemaphoreType.DMA((7,)), pltpu.SemaphoreType.DMA((7,)), pltpu.SemaphoreType.DMA],
    )(buf, *after)


def _allreduce_small(packed, after=()):
    rows, cols = packed.shape
    got = _slice_exchange(packed.reshape(8, rows // 8, cols), name="small_grads_exchange", after=after)
    mine = _sum_slots(got, name="small_grads_sum")
    return _gather8([mine], name="small_grads_gather")[0]


def _pack(ts):
    flat = jnp.concatenate([t.reshape(-1) for t in ts])
    pad = (-flat.shape[0]) % _PACK_ALIGN
    return jnp.pad(flat, (0, pad)).reshape(-1, _PACK_COLS)


def _unpack(buf, like):
    flat = buf.reshape(-1)
    out, pos = [], 0
    for t in like:
        out.append(flat[pos:pos + t.size].reshape(t.shape))
        pos += t.size
    return out


def kernel(x, norm_w, w_in, ssm_a_re, ssm_a_im, ssm_log_dt, ssm_b_re, ssm_b_im, ssm_c_re, ssm_c_im, ssm_d, ssm_glu_w, ssm_glu_b, sg_ln_w, sg_ln_b, sg_w, sg_b, attn_sinks, w_branch_a, w_branch_b, w_branch_c, w_out, final_norm_w, loss_target, m_norm_w, m_w_in, m_ssm_a_re, m_ssm_a_im, m_ssm_log_dt, m_ssm_b_re, m_ssm_b_im, m_ssm_c_re, m_ssm_c_im, m_ssm_d, m_ssm_glu_w, m_ssm_glu_b, m_sg_ln_w, m_sg_ln_b, m_sg_w, m_sg_b, m_attn_sinks, m_w_branch_a, m_w_branch_b, m_w_branch_c, m_w_out, m_final_norm_w, v_norm_w, v_w_in, v_ssm_a_re, v_ssm_a_im, v_ssm_log_dt, v_ssm_b_re, v_ssm_b_im, v_ssm_c_re, v_ssm_c_im, v_ssm_d, v_ssm_glu_w, v_ssm_glu_b, v_sg_ln_w, v_sg_ln_b, v_sg_w, v_sg_b, v_attn_sinks, v_w_branch_a, v_w_branch_b, v_w_branch_c, v_w_out, v_final_norm_w):
    w = dict(norm_w=norm_w, w_in=w_in, ssm_a_re=ssm_a_re, ssm_a_im=ssm_a_im, ssm_log_dt=ssm_log_dt, ssm_b_re=ssm_b_re,
             ssm_b_im=ssm_b_im, ssm_c_re=ssm_c_re, ssm_c_im=ssm_c_im, ssm_d=ssm_d, ssm_glu_w=ssm_glu_w,
             ssm_glu_b=ssm_glu_b, sg_ln_w=sg_ln_w, sg_ln_b=sg_ln_b, sg_w=sg_w, sg_b=sg_b, attn_sinks=attn_sinks,
             w_branch_a=w_branch_a, w_branch_b=w_branch_b, w_branch_c=w_branch_c, w_out=w_out,
             final_norm_w=final_norm_w)
    m = dict(norm_w=m_norm_w, w_in=m_w_in, ssm_a_re=m_ssm_a_re, ssm_a_im=m_ssm_a_im, ssm_log_dt=m_ssm_log_dt,
             ssm_b_re=m_ssm_b_re, ssm_b_im=m_ssm_b_im, ssm_c_re=m_ssm_c_re, ssm_c_im=m_ssm_c_im, ssm_d=m_ssm_d,
             ssm_glu_w=m_ssm_glu_w, ssm_glu_b=m_ssm_glu_b, sg_ln_w=m_sg_ln_w, sg_ln_b=m_sg_ln_b, sg_w=m_sg_w,
             sg_b=m_sg_b, attn_sinks=m_attn_sinks, w_branch_a=m_w_branch_a, w_branch_b=m_w_branch_b,
             w_branch_c=m_w_branch_c, w_out=m_w_out, final_norm_w=m_final_norm_w)
    v = dict(norm_w=v_norm_w, w_in=v_w_in, ssm_a_re=v_ssm_a_re, ssm_a_im=v_ssm_a_im, ssm_log_dt=v_ssm_log_dt,
             ssm_b_re=v_ssm_b_re, ssm_b_im=v_ssm_b_im, ssm_c_re=v_ssm_c_re, ssm_c_im=v_ssm_c_im, ssm_d=v_ssm_d,
             ssm_glu_w=v_ssm_glu_w, ssm_glu_b=v_ssm_glu_b, sg_ln_w=v_sg_ln_w, sg_ln_b=v_sg_ln_b, sg_w=v_sg_w,
             sg_b=v_sg_b, attn_sinks=v_attn_sinks, w_branch_a=v_w_branch_a, w_branch_b=v_w_branch_b,
             w_branch_c=v_w_branch_c, w_out=v_w_out, final_norm_w=v_final_norm_w)

    big_names = ("winT", "glu_w", "wbaT", "wbbT", "wbcT", "w_out")
    L = x.shape[1]
    tabs = _rope_tables(L)
    p = {k: w[k] for k in _SMALL}

    column_sharded = ("w_in", "w_branch_a", "w_branch_b", "w_branch_c")

    def shard_halves(ws, names=_BIG):
        ts = [(t.T if k in column_sharded else t).astype(MXU) for k, t in zip(names, ws)]
        return [t.reshape(2, t.shape[0] // 2, t.shape[1]) for t in ts]

    my_half = lambda x_, y_, c_, px, py: c_
    my_block = lambda x_, y_, c_: 4 * x_ + 2 * y_ + c_
    rows_of = lambda lands: [t.reshape(8 * t.shape[1], t.shape[2]) for t in lands]
    fill = lambda l, hs, i0=0: [_fill_own(s, name=f"gather_fill_{l}_{i0 + i}") for i, s in enumerate(hs)]
    saved = [None] * DEPTH

    halves = [shard_halves([w[k][l] for k in _BIG]) for l in range(DEPTH)]
    w_in_0 = halves[0][:1]
    sems_a, src_a, land_a, token_a = _split_start(w_in_0, fill(0, w_in_0), my_half, my_block, name="gather_start_0a")
    fill_after = lambda l, hs, i0: [_fill_own(s, name=f"gather_fill_{l}_{i0 + i}", after=token_a)
                                    for i, s in enumerate(hs)]
    lands = [[None] + fill_after(0, halves[0][1:], 1), fill_after(1, halves[1], 0)]
    sp = [_prep_layer(p, l, after=token_a) for l in range(DEPTH)]
    h0 = _rms_fwd(x[0], p["norm_w"][0], name="rms_fwd_0", after=token_a)
    _, land_a = _split_wait(sems_a, src_a, land_a, [h0, sp[0]["btr"], sp[1]["btr"]] + lands[0][1:] + lands[1],
                            name="gather_wait_0a")
    land_a = _pass_to_sibling(land_a, name="gather_pass_0a")
    sems_b, src_b, land_b, token_b = _split_start(halves[0][1:], lands[0][1:], my_half, my_block,
                                                  name="gather_start_0b", after=land_a)
    split1 = {}

    def rest0(t):
        _, got = _split_wait(sems_b, src_b, land_b, t, name="gather_wait_0b")
        got = _pass_to_sibling(got, name="gather_pass_0b")
        split1["sems"], split1["src"], split1["land"], token1 = _split_start(halves[1], lands[1], my_half, my_block,
                                                                             name="gather_start_1", after=got)
        return dict(zip(big_names, rows_of(land_a + got))), token1

    x1, saved[0], big0 = _layer_fwd(x[0], h0, p, sp[0], rows_of(land_a)[0], rest0, 0, tabs, proj_after=token_b)
    _, lands1 = _split_wait(split1["sems"], split1["src"], split1["land"], [x1], name="gather_wait_1")
    big1 = dict(zip(big_names, rows_of(_pass_to_sibling(lands1, name="gather_pass_1"))))
    bigs = [big0, big1]
    h1 = _rms_fwd(x1, p["norm_w"][1], name="rms_fwd_1")
    x2, saved[1], _ = _layer_fwd(x1, h1, p, sp[1], big1["winT"], lambda t: (big1, None), 1, tabs)
    loss, dx, dfw = _final_loss(x2, p["final_norm_w"], loss_target[0], name="final_loss")

    grads = [None] * DEPTH
    rs = {}

    def early(l):
        def begin(g):
            rs[f"{l}a"], token_a = _reduce_scatter_begin([g[k] for k in big_names[1:]], tag=f"{l}a")
            return token_a
        return begin

    def late(l, dproj, dx):
        if l == 0:
            rs["0b"], token_s = _reduce_scatter_begin([grads[0]["winT"]], tag="0b")
            dh = _proj_bwd_dh(dproj, bigs[0]["winT"], 0, token_s)
            return _rms_bwd(saved[0]["x"], p["norm_w"][0], dh, dx, name="rms_bwd_0")
        sems, views, lands, token_b = _swap_start(_grad_views([grads[l]["winT"]]), name=f"rs_swap_start_{l}b")
        dh = _proj_bwd_dh(dproj, bigs[l]["winT"], l, token_b)
        views, theirs = _swap_wait(sems, views, lands, [dh], name=f"rs_swap_wait_{l}b")
        rs[f"{l}b"], token_s = _scatter_begin(views, theirs, tag=f"{l}b")
        return _rms_bwd(saved[l]["x"], p["norm_w"][l], dh, dx, name=f"rms_bwd_{l}", after=token_s)

    def reduced(l, after):
        return _reduce_scatter_end(rs[f"{l}b"], after, tag=f"{l}b") + _reduce_scatter_end(rs[f"{l}a"], after, tag=f"{l}a")

    dproj, grads[1] = _layer_bwd(dx, saved[1], p, bigs[1], 1, tabs, early(1))
    dx, grads[1]["norm_w"] = late(1, dproj, dx)
    dproj, grads[0] = _layer_bwd(dx, saved[0], p, bigs[0], 0, tabs, early(0))
    dx, grads[0]["norm_w"] = late(0, dproj, dx)
    red1 = reduced(1, [dx])

    tr = lambda t: t.transpose(0, 2, 1)
    view = {k: (tr if k == "w_in" else (lambda t: t)) for k in _BIG}
    shard_grads = lambda red: dict(zip(_BIG, (red[0], red[1], red[2].T, red[3].T, red[4].T, red[5])))
    outs = {k: None for k in _BIG}

    def adamw_big(l, red):
        for k, g in shard_grads(red).items():
            outs[k] = _adamw_layer(view[k](w[k]), g, view[k](m[k]), view[k](v[k]), l, outs[k], name=f"adamw_{k}_{l}")

    adamw_big(1, red1)

    small_like = [w[k] for k in _SMALL]
    gs = [jnp.stack([grads[l][k] for l in range(DEPTH)]) if k != "final_norm_w" else dfw for k in _SMALL]
    gsum = _allreduce_small(_pack(gs + [loss.reshape(1)]), after=[outs[k][0] for k in _BIG])
    adamw_big(0, reduced(0, [gsum]))

    gfull, delta, new_m, new_v = {}, {}, {}, {}
    for k in _BIG:
        gfull[k], delta[k], new_m[k], new_v[k] = (view[k](t) for t in outs[k])
    *small_sums, loss = _unpack(gsum, small_like + [loss])
    for k, t in zip(_SMALL, small_sums):
        gfull[k] = t
        delta[k], new_m[k], new_v[k] = _adamw(w[k], t, m[k], v[k], name=f"adamw_{k}")

    return (loss, dx[None], *[gfull[k] for k in _WEIGHTS], *[delta[k] for k in _WEIGHTS],
            *[new_m[k] for k in _WEIGHTS], *[new_v[k] for k in _WEIGHTS])
```

```python
import functools
import math

import numpy as np
import jax
import jax.numpy as jnp
from jax import lax
from jax.experimental import pallas as pl
from jax.experimental.pallas import tpu as pltpu

F32 = jnp.float32
MXU = jnp.bfloat16
HIGHEST = lax.Precision.HIGHEST

D_MODEL = 2048
DEPTH = 2
EPS = 1e-6
NEG_INF = -1e30
SSM_WIDTH = 1024
SSM_GROUP = 16
SSM_GROUPS = 64
SSM_STATE = 64
SSM_CH = SSM_GROUPS * SSM_STATE
SLAB = 128
SLAB_CH = (SLAB // SSM_GROUP) * SSM_STATE
N_SLAB = SSM_WIDTH // SLAB
SCAN_SEG = 8
SCAN_STEPS = 4
SG_HEADS = 8
SG_CHUNK = 128
HEAD_DIM = 64
ATT_HEADS = 16
ATT_KV_HEADS = 2
GQA_GROUP = 8
ATT_BLOCK = 128
WINDOW = 128
ROT_DIM = 16
ROPE_THETA = 500000.0
N_MAIN = 6400
N_ZC = 1024
N_GATES = 6144
D_IN = N_MAIN + N_ZC + N_GATES

ADAM_LR = 0.001
ADAM_B1 = 0.9
ADAM_B2 = 0.999
ADAM_EPS = 1e-08
ADAM_WD = 0.01
ADAM_STEP = 10

_DIMS = {"nn": (((1,), (0,)), ((), ())), "nt": (((1,), (1,)), ((), ())), "tn": (((0,), (0,)), ((), ()))}
_MB = 1024 * 1024


def _cp(sem, vmem_mb=48):
    return pltpu.CompilerParams(dimension_semantics=sem, vmem_limit_bytes=vmem_mb * _MB)


def _dot(a, b, mode):
    return lax.dot_general(a.astype(MXU), b.astype(MXU), _DIMS[mode], preferred_element_type=F32)


@jax.custom_vjp
def _mm_nn(a, b):
    return _dot(a, b, "nn")


def _mm_nn_fwd(a, b):
    return _dot(a, b, "nn"), (a, b)


def _mm_nn_bwd(res, g):
    a, b = res
    return _dot(g, b, "nt"), _dot(a, g, "tn")


_mm_nn.defvjp(_mm_nn_fwd, _mm_nn_bwd)


@jax.custom_vjp
def _mm_nt(a, bt):
    return _dot(a, bt, "nt")


def _mm_nt_fwd(a, bt):
    return _dot(a, bt, "nt"), (a, bt)


def _mm_nt_bwd(res, g):
    a, bt = res
    return _dot(g, bt, "nn"), _dot(g, a, "tn")


_mm_nt.defvjp(_mm_nt_fwd, _mm_nt_bwd)


def _rmsnorm(x, w):
    return x * lax.rsqrt(jnp.mean(x * x, axis=-1, keepdims=True) + EPS) * w


def _layernorm(x, w, b):
    mu = jnp.mean(x, axis=-1, keepdims=True)
    var = jnp.mean(jnp.square(x - mu), axis=-1, keepdims=True)
    return (x - mu) * lax.rsqrt(var + EPS) * w + b


def _silu(x):
    return x * jax.nn.sigmoid(x)


def _matmul(a, b, mode, *, name, shape, tm, tn, tk, out_dtype=F32, add=None, a_off=(0, 0), b_off=(0, 0), after=None):
    m, n, k = shape
    tm, tn, tk = min(tm, m), min(tn, n), min(tk, k)
    assert m % tm == 0 and n % tn == 0 and k % tk == 0, (name, shape, tm, tn, tk)
    nk = k // tk
    has_add, has_after = add is not None, after is not None

    def body(*refs):
        a_ref, b_ref = refs[0], refs[1]
        pos = 2
        add_ref = None
        if has_add:
            add_ref = refs[pos]
            pos += 1
        if has_after:
            pos += 1
        o_ref = refs[pos]
        p = _dot(a_ref[...], b_ref[...], mode)
        if nk == 1:
            if has_add:
                p = p + add_ref[...].astype(F32)
            o_ref[...] = p.astype(out_dtype)
            return
        acc_ref = refs[pos + 1]
        kk = pl.program_id(2)

        @pl.when(kk == 0)
        def _():
            acc_ref[...] = p

        @pl.when(kk > 0)
        def _():
            acc_ref[...] += p

        @pl.when(kk == nk - 1)
        def _():
            r = acc_ref[...]
            if has_add:
                r = r + add_ref[...].astype(F32)
            o_ref[...] = r.astype(out_dtype)

    a0, a1 = a_off
    b0, b1 = b_off
    if mode == "tn":
        a_spec = pl.BlockSpec((tk, tm), lambda i, j, kk: (kk + a0, i + a1))
    else:
        a_spec = pl.BlockSpec((tm, tk), lambda i, j, kk: (i + a0, kk + a1))
    if mode == "nt":
        b_spec = pl.BlockSpec((tn, tk), lambda i, j, kk: (j + b0, kk + b1))
    else:
        b_spec = pl.BlockSpec((tk, tn), lambda i, j, kk: (kk + b0, j + b1))
    in_specs = [a_spec, b_spec]
    args = [a, b]
    if has_add:
        in_specs.append(pl.BlockSpec((tm, tn), lambda i, j, kk: (i, j)))
        args.append(add)
    if has_after:
        in_specs.append(pl.BlockSpec(memory_space=pl.ANY))
        args.append(after)
    return pl.pallas_call(
        body, name=name, grid=(m // tm, n // tn, nk),
        in_specs=in_specs,
        out_specs=pl.BlockSpec((tm, tn), lambda i, j, kk: (i, j)),
        out_shape=jax.ShapeDtypeStruct((m, n), out_dtype),
        scratch_shapes=[pltpu.VMEM((tm, tn), F32)] if nk > 1 else [],
        compiler_params=_cp(("parallel", "parallel", "arbitrary")),
    )(*args)


def _rms_fwd(x, w, *, name, tm=256, after=None):
    L, d = x.shape
    tm = min(tm, L)
    extra = [] if after is None else [after]

    def body(x_ref, w_ref, *rest):
        rest[-1][...] = _rmsnorm(x_ref[...], w_ref[...]).astype(MXU)

    return pl.pallas_call(
        body, name=name, grid=(L // tm,),
        in_specs=([pl.BlockSpec((tm, d), lambda i: (i, 0)), pl.BlockSpec((1, d), lambda i: (0, 0))]
                  + [pl.BlockSpec(memory_space=pl.ANY)] * len(extra)),
        out_specs=pl.BlockSpec((tm, d), lambda i: (i, 0)),
        out_shape=jax.ShapeDtypeStruct((L, d), MXU),
        compiler_params=_cp(("parallel",)),
    )(x, w.reshape(1, d), *extra)


def _rms_bwd(x, w, dh, dxn, *, name, tm=256, after=None):
    L, d = x.shape
    tm = min(tm, L)
    extra = [] if after is None else [after]

    def body(x_ref, w_ref, dh_ref, dxn_ref, *rest):
        dx_ref, dw_ref = rest[-2:]
        _, vjp = jax.vjp(_rmsnorm, x_ref[...], w_ref[...])
        dx, dw = vjp(dh_ref[...])
        dx_ref[...] = dx + dxn_ref[...]

        @pl.when(pl.program_id(0) == 0)
        def _():
            dw_ref[...] = jnp.zeros_like(dw_ref)

        dw_ref[...] += dw

    row = pl.BlockSpec((tm, d), lambda i: (i, 0))
    vec = pl.BlockSpec((1, d), lambda i: (0, 0))
    dx, dw = pl.pallas_call(
        body, name=name, grid=(L // tm,),
        in_specs=[row, vec, row, row] + [pl.BlockSpec(memory_space=pl.ANY)] * len(extra), out_specs=[row, vec],
        out_shape=[jax.ShapeDtypeStruct((L, d), F32), jax.ShapeDtypeStruct((1, d), F32)],
        compiler_params=_cp(("arbitrary",)),
    )(x, w.reshape(1, d), dh, dxn, *extra)
    return dx, dw.reshape(d)


def _final_loss(x, w, tgt, *, name, tm=256):
    L, d = x.shape
    tm = min(tm, L)

    def loss_fn(xv, wv, tv):
        err = jnp.square(_rmsnorm(xv, wv) - tv)
        return 0.5 * jnp.sum(jnp.mean(err, axis=-1, keepdims=True), axis=0, keepdims=True)

    def body(x_ref, w_ref, t_ref, loss_ref, dx_ref, dw_ref):
        tv = t_ref[...]
        val, vjp = jax.vjp(lambda xv, wv: loss_fn(xv, wv, tv), x_ref[...], w_ref[...])
        dx, dw = vjp(jnp.ones((1, 1), F32))
        dx_ref[...] = dx

        @pl.when(pl.program_id(0) == 0)
        def _():
            dw_ref[...] = jnp.zeros_like(dw_ref)
            loss_ref[...] = jnp.zeros_like(loss_ref)

        dw_ref[...] += dw
        loss_ref[...] += jnp.broadcast_to(val, loss_ref.shape)

    row = pl.BlockSpec((tm, d), lambda i: (i, 0))
    vec = pl.BlockSpec((1, d), lambda i: (0, 0))
    loss, dx, dw = pl.pallas_call(
        body, name=name, grid=(L // tm,),
        in_specs=[row, vec, row],
        out_specs=[pl.BlockSpec((8, 128), lambda i: (0, 0)), row, vec],
        out_shape=[jax.ShapeDtypeStruct((8, 128), F32), jax.ShapeDtypeStruct((L, d), F32),
                   jax.ShapeDtypeStruct((1, d), F32)],
        compiler_params=_cp(("arbitrary",)),
    )(x, w.reshape(1, d), tgt)
    return loss[0, 0], dx, dw.reshape(d)


PARAM_ROWS = 512


def _s5_param_fn(are, aim, ldt, bre, bim, row0):
    n = are.shape[0]
    grp = (row0 + lax.broadcasted_iota(jnp.int32, (n, SSM_GROUPS), 0)) // SSM_STATE
    col = lax.broadcasted_iota(jnp.int32, (n, SSM_GROUPS), 1)
    sel = (grp == col).astype(F32)
    dt = jnp.sum(sel * jnp.exp(ldt), axis=-1, keepdims=True)
    mag = jnp.exp(are * dt)
    ang = aim * dt
    lbr = mag * jnp.cos(ang)
    lbi = mag * jnp.sin(ang)
    den = are * are + aim * aim
    nr = lbr - 1.0
    kr = (nr * are + lbi * aim) / den
    ki = (lbi * are - nr * aim) / den
    return lbr, lbi, kr * bre - ki * bim, kr * bim + ki * bre


def _s5_param_specs():
    col = pl.BlockSpec((PARAM_ROWS, 1), lambda i: (i, 0))
    mat = pl.BlockSpec((PARAM_ROWS, SSM_GROUP), lambda i: (i, 0))
    vec = pl.BlockSpec((1, SSM_GROUPS), lambda i: (0, 0))
    return col, mat, vec


def _s5_params_fwd(are, aim, ldt, bre, bim, *, name, after=None):
    n = are.shape[0]
    col, mat, vec = _s5_param_specs()
    extra = [] if after is None else [after]

    def body(are_ref, aim_ref, ldt_ref, bre_ref, bim_ref, *rest):
        lbr_ref, lbi_ref, bbr_ref, bbi_ref = rest[-4:]
        row0 = pl.program_id(0) * PARAM_ROWS
        lbr, lbi, bbr, bbi = _s5_param_fn(are_ref[...], aim_ref[...], ldt_ref[...], bre_ref[...], bim_ref[...], row0)
        lbr_ref[...] = lbr
        lbi_ref[...] = lbi
        bbr_ref[...] = bbr
        bbi_ref[...] = bbi

    cshape = jax.ShapeDtypeStruct((n, 1), F32)
    mshape = jax.ShapeDtypeStruct((n, SSM_GROUP), F32)
    return pl.pallas_call(body, name=name, grid=(n // PARAM_ROWS,),
                          in_specs=[col, col, vec, mat, mat] + [pl.BlockSpec(memory_space=pl.ANY)] * len(extra),
                          out_specs=[col, col, mat, mat], out_shape=[cshape, cshape, mshape, mshape],
                          compiler_params=_cp(("parallel",)))(are, aim, ldt, bre, bim, *extra)


def _s5_params_bwd(are, aim, ldt, bre, bim, dlbr, dlbi, dbbr, dbbi, *, name):
    n = are.shape[0]
    col, mat, vec = _s5_param_specs()

    def body(are_ref, aim_ref, ldt_ref, bre_ref, bim_ref, g0, g1, g2, g3, o0, o1, o2, o3, o4):
        row0 = pl.program_id(0) * PARAM_ROWS
        _, vjp = jax.vjp(lambda a, b, c, d, e: _s5_param_fn(a, b, c, d, e, row0),
                         are_ref[...], aim_ref[...], ldt_ref[...], bre_ref[...], bim_ref[...])
        dare, daim, dldt, dbre, dbim = vjp((g0[...], g1[...], g2[...], g3[...]))
        o0[...] = dare
        o1[...] = daim
        o3[...] = dbre
        o4[...] = dbim

        @pl.when(pl.program_id(0) == 0)
        def _():
            o2[...] = jnp.zeros_like(o2)

        o2[...] += dldt

    cshape = jax.ShapeDtypeStruct((n, 1), F32)
    mshape = jax.ShapeDtypeStruct((n, SSM_GROUP), F32)
    return pl.pallas_call(body, name=name, grid=(n // PARAM_ROWS,),
                          in_specs=[col, col, vec, mat, mat, col, col, mat, mat],
                          out_specs=[col, col, vec, mat, mat],
                          out_shape=[cshape, cshape, jax.ShapeDtypeStruct((1, SSM_GROUPS), F32), mshape, mshape],
                          compiler_params=_cp(("arbitrary",)))(are, aim, ldt, bre, bim, dlbr, dlbi, dbbr, dbbi)


SLAB_NC = SLAB_CH // 128


def _s5_specs(L):
    slab = pl.BlockSpec((L, SLAB), lambda s: (0, s))
    wspec = pl.BlockSpec((SLAB_NC, 128, SLAB), lambda s: (s, 0, 0))
    lspec = pl.BlockSpec((SLAB_NC, 1, 128), lambda s: (s, 0, 0))
    sspec = pl.BlockSpec((SLAB_NC, L, 128), lambda s: (s, 0, 0))
    dspec = pl.BlockSpec((1, SLAB), lambda s: (0, s))
    return slab, wspec, lspec, sspec, dspec


def _scan_inplace(sr_ref, si_ref, lr, li, pr_ref, pi_ref, *, reverse):
    NC, L, W = sr_ref.shape
    S = SCAN_SEG
    T = L // S
    lr8 = [jnp.broadcast_to(lr[k], (S, W)) for k in range(NC)]
    li8 = [jnp.broadcast_to(li[k], (S, W)) for k in range(NC)]

    def tiles(first, count):
        return pl.ds(first * S, count * S)

    for k in range(NC):
        pr_ref[k, tiles(T - 1 if reverse else 0, 1), :] = lr8[k]
        pi_ref[k, tiles(T - 1 if reverse else 0, 1), :] = li8[k]
        n = 1
        while n < T:
            have = tiles(T - n, n) if reverse else tiles(0, n)
            new = tiles(T - 2 * n, n) if reverse else tiles(n, n)
            top = tiles(T - n, 1) if reverse else tiles(n - 1, 1)
            ar, ai = pr_ref[k, top, :][None], pi_ref[k, top, :][None]
            hr, hi = pr_ref[k, have, :].reshape(n, S, W), pi_ref[k, have, :].reshape(n, S, W)
            pr_ref[k, new, :] = (hr * ar - hi * ai).reshape(n * S, W)
            pi_ref[k, new, :] = (hr * ai + hi * ar).reshape(n * S, W)
            n *= 2

    def step(i, carry):
        for u in range(SCAN_STEPS):
            jj = i * SCAN_STEPS + u
            rows = pl.ds(pl.multiple_of(((T - 1 - jj) if reverse else jj) * S, S), S)
            out = []
            for k in range(NC):
                sr, si = carry[k]
                nsr = lr8[k] * sr - li8[k] * si + sr_ref[k, rows, :]
                nsi = lr8[k] * si + li8[k] * sr + si_ref[k, rows, :]
                sr_ref[k, rows, :] = nsr
                si_ref[k, rows, :] = nsi
                out.append((nsr, nsi))
            carry = tuple(out)
        return carry

    zero = jnp.zeros((S, W), F32)
    ends = lax.fori_loop(0, T // SCAN_STEPS, step, tuple((zero, zero) for k in range(NC)))
    sub = lax.broadcasted_iota(jnp.int32, (S, W), 0)
    order = range(S - 1, -1, -1) if reverse else range(S)
    for k in range(NC):
        er, ei = ends[k]
        full = tiles(0 if reverse else T - 1, 1)
        ltr = pr_ref[k, full, :][0:1]
        lti = pi_ref[k, full, :][0:1]
        cr = jnp.zeros((1, W), F32)
        ci = jnp.zeros((1, W), F32)
        ctr = jnp.zeros((S, W), F32)
        cti = jnp.zeros((S, W), F32)
        for seg in order:
            ctr = jnp.where(sub == seg, cr, ctr)
            cti = jnp.where(sub == seg, ci, cti)
            cr, ci = (er[seg:seg + 1, :] + ltr * cr - lti * ci, ei[seg:seg + 1, :] + ltr * ci + lti * cr)
        pr = pr_ref[k].reshape(T, S, W)
        pi = pi_ref[k].reshape(T, S, W)
        sr_ref[k] += (pr * ctr[None] - pi * cti[None]).reshape(L, W)
        si_ref[k] += (pr * cti[None] + pi * ctr[None]).reshape(L, W)


def _time_interleave(a):
    L, W = a.shape
    return a.reshape(SCAN_SEG, L // SCAN_SEG, W).transpose(1, 0, 2).reshape(L, W)


def _time_deinterleave(a):
    L, W = a.shape
    return a.reshape(L // SCAN_SEG, SCAN_SEG, W).transpose(1, 0, 2).reshape(L, W)


def _s5_fwd(u, btr, bti, cbr, cbi, lbr, lbi, dvec, *, name, after=None):
    L = u.shape[0]
    extra = [] if after is None else [after]

    def body(u_ref, btr_ref, bti_ref, cbr_ref, cbi_ref, lr_ref, li_ref, d_ref, *rest):
        ys_ref, sr_ref, si_ref, pr_ref, pi_ref = rest[-5:]
        u = u_ref[...]
        for k in range(SLAB_NC):
            sr_ref[k] = _dot(u, btr_ref[k], "nt")
            si_ref[k] = _dot(u, bti_ref[k], "nt")
        _scan_inplace(sr_ref, si_ref, lr_ref[...], li_ref[...], pr_ref, pi_ref, reverse=False)
        ys = d_ref[...] * u
        for k in range(SLAB_NC):
            ys = ys + _dot(sr_ref[k], cbr_ref[k], "nn") - _dot(si_ref[k], cbi_ref[k], "nn")
        ys_ref[...] = ys

    slab, wspec, lspec, sspec, dspec = _s5_specs(L)
    sshape = jax.ShapeDtypeStruct((N_SLAB * SLAB_NC, L, 128), F32)
    return pl.pallas_call(
        body, name=name, grid=(N_SLAB,),
        in_specs=[slab, wspec, wspec, wspec, wspec, lspec, lspec, dspec] + [pl.BlockSpec(memory_space=pl.ANY)] * len(extra),
        out_specs=[slab, sspec, sspec],
        out_shape=[jax.ShapeDtypeStruct((L, SSM_WIDTH), F32), sshape, sshape],
        scratch_shapes=[pltpu.VMEM((SLAB_NC, L, 128), F32), pltpu.VMEM((SLAB_NC, L, 128), F32)],
        compiler_params=_cp(("parallel",), 56),
    )(u, btr, bti, cbr, cbi, lbr, lbi, dvec, *extra)


def _s5_bwd(dys, u, sr, si, btr, bti, cbr, cbi, lbr, lbi, dvec, *, name, after=None):
    L = u.shape[0]
    S = SCAN_SEG
    extra = [] if after is None else [after]

    def body(dys_ref, u_ref, sr_ref, si_ref, btr_ref, bti_ref, cbr_ref, cbi_ref, lr_ref, li_ref, d_ref, *rest):
        (du_ref, dbtr_ref, dbti_ref, dcbr_ref, dcbi_ref, dlr_ref, dli_ref, dd_ref,
         ar_ref, ai_ref, pr_ref, pi_ref) = rest[-12:]
        dys = dys_ref[...]
        u = u_ref[...]
        for k in range(SLAB_NC):
            ar_ref[k] = _dot(dys, cbr_ref[k], "nt")
            ai_ref[k] = -_dot(dys, cbi_ref[k], "nt")
        _scan_inplace(ar_ref, ai_ref, lr_ref[...], -li_ref[...], pr_ref, pi_ref, reverse=True)
        head = lax.broadcasted_iota(jnp.int32, (L, 1), 0) < S
        sub0 = lax.broadcasted_iota(jnp.int32, (S, 1), 0) == 0

        def prev_state(s):
            up = pltpu.roll(s, S, 0)
            return jnp.where(head, 0.0, up), jnp.where(sub0, 0.0, pltpu.roll(up[0:S], 1, 0))

        du = d_ref[...] * dys
        for k in range(SLAB_NC):
            a_re = ar_ref[k]
            a_im = ai_ref[k]
            du = du + _dot(a_re, btr_ref[k], "nn") + _dot(a_im, bti_ref[k], "nn")
            dbtr_ref[k] = _dot(a_re, u, "tn")
            dbti_ref[k] = _dot(a_im, u, "tn")
            s_re = sr_ref[k]
            s_im = si_ref[k]
            dcbr_ref[k] = _dot(s_re, dys, "tn")
            dcbi_ref[k] = -_dot(s_im, dys, "tn")
            p_re, q_re = prev_state(s_re)
            p_im, q_im = prev_state(s_im)
            b_re, b_im = a_re[0:S], a_im[0:S]
            dlr_ref[k] = (jnp.sum(p_re * a_re + p_im * a_im, axis=0, keepdims=True)
                          + jnp.sum(q_re * b_re + q_im * b_im, axis=0, keepdims=True))
            dli_ref[k] = (jnp.sum(p_re * a_im - p_im * a_re, axis=0, keepdims=True)
                          + jnp.sum(q_re * b_im - q_im * b_re, axis=0, keepdims=True))
        du_ref[...] = du
        dd_ref[...] = jnp.sum(dys * u, axis=0, keepdims=True)

    slab, wspec, lspec, sspec, dspec = _s5_specs(L)
    wshape = jax.ShapeDtypeStruct((N_SLAB * SLAB_NC, 128, SLAB), F32)
    lshape = jax.ShapeDtypeStruct((N_SLAB * SLAB_NC, 1, 128), F32)
    return pl.pallas_call(
        body, name=name, grid=(N_SLAB,),
        in_specs=([slab, slab, sspec, sspec, wspec, wspec, wspec, wspec, lspec, lspec, dspec]
                  + [pl.BlockSpec(memory_space=pl.ANY)] * len(extra)),
        out_specs=[slab, wspec, wspec, wspec, wspec, lspec, lspec, dspec],
        out_shape=[jax.ShapeDtypeStruct((L, SSM_WIDTH), F32), wshape, wshape, wshape, wshape, lshape, lshape,
                   jax.ShapeDtypeStruct((1, SSM_WIDTH), F32)],
        scratch_shapes=[pltpu.VMEM((SLAB_NC, L, 128), F32)] * 4,
        compiler_params=_cp(("parallel",), 56),
    )(dys, u, sr, si, btr, bti, cbr, cbi, lbr, lbi, dvec, *extra)


_SLAB_MASK = (np.arange(SLAB_CH)[:, None] // SSM_STATE == np.arange(SLAB)[None, :] // SSM_GROUP)


def _expand_bd(x):
    t = jnp.tile(x.reshape(N_SLAB, SLAB_CH, SSM_GROUP), (1, 1, SLAB // SSM_GROUP))
    return jnp.where(_SLAB_MASK[None], t, 0.0).astype(MXU).reshape(N_SLAB * SLAB_NC, 128, SLAB)


def _contract_bd(dx):
    t = jnp.where(_SLAB_MASK[None], dx.reshape(N_SLAB, SLAB_CH, SLAB), 0.0)
    return jnp.sum(t.reshape(N_SLAB, SLAB_CH, SLAB // SSM_GROUP, SSM_GROUP), axis=2).reshape(SSM_CH, SSM_GROUP)


def _glu_ew(ys, zlin, za):
    a1 = jax.nn.gelu(ys)
    return a1 * jax.nn.sigmoid(zlin) * _silu(za)


def _glu_fwd(ys, main, gw, gb, *, name, tm=256):
    L = ys.shape[0]
    tm = min(tm, L)
    W = SSM_WIDTH

    def body(ys_ref, za_ref, gw_ref, gb_ref, ya_ref):
        ys = ys_ref[...]
        a1 = jax.nn.gelu(ys)
        zlin = _dot(a1, gw_ref[...], "nn") + gb_ref[...]
        ya_ref[...] = _glu_ew(ys, zlin, za_ref[...]).astype(MXU)

    return pl.pallas_call(
        body, name=name, grid=(L // tm,),
        in_specs=[pl.BlockSpec((tm, W), lambda i: (i, 0)), pl.BlockSpec((tm, W), lambda i: (i, 1)),
                  pl.BlockSpec((W, W), lambda i: (0, 0)), pl.BlockSpec((1, W), lambda i: (0, 0))],
        out_specs=pl.BlockSpec((tm, W), lambda i: (i, 0)),
        out_shape=jax.ShapeDtypeStruct((L, W), MXU),
        compiler_params=_cp(("parallel",)),
    )(ys, main, gw, gb.reshape(1, W))


def _glu_bwd(dya, ys, main, gw, gb, *, name, tm=256):
    L = ys.shape[0]
    tm = min(tm, L)
    W = SSM_WIDTH

    def body(dya_ref, ys_ref, za_ref, gw_ref, gb_ref, dys_ref, dza_ref, a1_ref, dzl_ref, db_ref):
        ys = ys_ref[...]
        a1, gelu_vjp = jax.vjp(jax.nn.gelu, ys)
        zlin = _dot(a1, gw_ref[...], "nn") + gb_ref[...]
        _, vjp = jax.vjp(lambda a, z, za: a * jax.nn.sigmoid(z) * _silu(za), a1, zlin, za_ref[...])
        da1, dzlin, dza = vjp(dya_ref[...].astype(F32))
        da1 = da1 + _dot(dzlin, gw_ref[...], "nt")
        dys_ref[...] = gelu_vjp(da1)[0]
        dza_ref[...] = dza
        a1_ref[...] = a1.astype(MXU)
        dzl_ref[...] = dzlin.astype(MXU)

        @pl.when(pl.program_id(0) == 0)
        def _():
            db_ref[...] = jnp.zeros_like(db_ref)

        db_ref[...] += jnp.sum(dzlin, axis=0, keepdims=True)

    row = pl.BlockSpec((tm, W), lambda i: (i, 0))
    vec = pl.BlockSpec((1, W), lambda i: (0, 0))
    return pl.pallas_call(
        body, name=name, grid=(L // tm,),
        in_specs=[row, row, pl.BlockSpec((tm, W), lambda i: (i, 1)), pl.BlockSpec((W, W), lambda i: (0, 0)), vec],
        out_specs=[row, row, row, row, vec],
        out_shape=[jax.ShapeDtypeStruct((L, W), F32), jax.ShapeDtypeStruct((L, W), F32),
                   jax.ShapeDtypeStruct((L, W), MXU), jax.ShapeDtypeStruct((L, W), MXU),
                   jax.ShapeDtypeStruct((1, W), F32)],
        compiler_params=_cp(("arbitrary",)),
    )(dya, ys, main, gw, gb.reshape(1, W))


def _sg_fn(ub, vb, zb, lnw, lnb, ws, bs):
    u = jax.nn.gelu(ub)
    v = _layernorm(jax.nn.gelu(vb), lnw, lnb)
    r = lax.broadcasted_iota(jnp.int32, (SG_CHUNK, SG_CHUNK), 0)
    c = lax.broadcasted_iota(jnp.int32, (SG_CHUNK, SG_CHUNK), 1)
    tri = r >= c
    outs = []
    for h in range(SG_HEADS):
        wh = jnp.where(tri, ws[h], 0.0)
        outs.append(_mm_nn(wh, v[:, h * 128:(h + 1) * 128]) + bs[h])
    mixed = jnp.concatenate(outs, axis=1)
    return u * mixed * _silu(zb)


def _sg_specs(L):
    W = SSM_WIDTH
    blk = lambda c: pl.BlockSpec((SG_CHUNK, W), lambda i, c=c: (i, c))
    vec = pl.BlockSpec((1, W), lambda i: (0, 0))
    wspec = pl.BlockSpec((SG_HEADS, SG_CHUNK, SG_CHUNK), lambda i: (0, 0, 0))
    bspec = pl.BlockSpec((SG_HEADS, SG_CHUNK, 1), lambda i: (0, 0, 0))
    return blk, vec, wspec, bspec


def _sg_fwd(main, lnw, lnb, sgw, sgb, *, name):
    L = main.shape[0]
    W = SSM_WIDTH
    blk, vec, wspec, bspec = _sg_specs(L)

    def body(ub_ref, vb_ref, zb_ref, lnw_ref, lnb_ref, w_ref, b_ref, yb_ref):
        ws = [w_ref[h] for h in range(SG_HEADS)]
        bs = [b_ref[h] for h in range(SG_HEADS)]
        yb_ref[...] = _sg_fn(ub_ref[...], vb_ref[...], zb_ref[...], lnw_ref[...], lnb_ref[...], ws, bs).astype(MXU)

    return pl.pallas_call(
        body, name=name, grid=(L // SG_CHUNK,),
        in_specs=[blk(2), blk(3), blk(4), vec, vec, wspec, bspec],
        out_specs=pl.BlockSpec((SG_CHUNK, W), lambda i: (i, 0)),
        out_shape=jax.ShapeDtypeStruct((L, W), MXU),
        compiler_params=_cp(("parallel",)),
    )(main, main, main, lnw.reshape(1, W), lnb.reshape(1, W), sgw, sgb.reshape(SG_HEADS, SG_CHUNK, 1))


def _sg_bwd(dyb, main, lnw, lnb, sgw, sgb, *, name):
    L = main.shape[0]
    W = SSM_WIDTH
    blk, vec, wspec, bspec = _sg_specs(L)

    def body(dyb_ref, ub_ref, vb_ref, zb_ref, lnw_ref, lnb_ref, w_ref, b_ref,
             dub_ref, dvb_ref, dzb_ref, dlnw_ref, dlnb_ref, dw_ref, db_ref):
        ws = [w_ref[h] for h in range(SG_HEADS)]
        bs = [b_ref[h] for h in range(SG_HEADS)]
        _, vjp = jax.vjp(_sg_fn, ub_ref[...], vb_ref[...], zb_ref[...], lnw_ref[...], lnb_ref[...], ws, bs)
        dub, dvb, dzb, dlnw, dlnb, dws, dbs = vjp(dyb_ref[...])

        @pl.when(pl.program_id(0) == 0)
        def _():
            dlnw_ref[...] = jnp.zeros_like(dlnw_ref)
            dlnb_ref[...] = jnp.zeros_like(dlnb_ref)
            dw_ref[...] = jnp.zeros_like(dw_ref)
            db_ref[...] = jnp.zeros_like(db_ref)

        dub_ref[...] = dub
        dvb_ref[...] = dvb
        dzb_ref[...] = dzb
        dlnw_ref[...] += dlnw
        dlnb_ref[...] += dlnb
        for h in range(SG_HEADS):
            dw_ref[h] += dws[h]
            db_ref[h] += dbs[h]

    row = pl.BlockSpec((SG_CHUNK, W), lambda i: (i, 0))
    out = jax.ShapeDtypeStruct((L, W), F32)
    return pl.pallas_call(
        body, name=name, grid=(L // SG_CHUNK,),
        in_specs=[row, blk(2), blk(3), blk(4), vec, vec, wspec, bspec],
        out_specs=[row, row, row, vec, vec, wspec, bspec],
        out_shape=[out, out, out, jax.ShapeDtypeStruct((1, W), F32), jax.ShapeDtypeStruct((1, W), F32),
                   jax.ShapeDtypeStruct((SG_HEADS, SG_CHUNK, SG_CHUNK), F32),
                   jax.ShapeDtypeStruct((SG_HEADS, SG_CHUNK, 1), F32)],
        compiler_params=_cp(("arbitrary",)),
    )(dyb, main, main, main, lnw.reshape(1, W), lnb.reshape(1, W), sgw, sgb.reshape(SG_HEADS, SG_CHUNK, 1))


def _rope_tables(L):
    half = ROT_DIM // 2
    inv_freq = ROPE_THETA ** (-jnp.arange(0, ROT_DIM, 2, dtype=F32) / ROT_DIM)
    ang = jnp.arange(L, dtype=F32)[:, None] * inv_freq[None, :]
    cos = jnp.cos(ang)
    sin = jnp.sin(ang)
    ones = jnp.ones((L, HEAD_DIM - ROT_DIM), F32)
    cosf = jnp.concatenate([cos, cos, ones], axis=1)
    sinf = jnp.concatenate([sin, sin, 0.0 * ones], axis=1)
    rot = np.zeros((HEAD_DIM, HEAD_DIM), np.float32)
    for d in range(half):
        rot[d + half, d] = -1.0
        rot[d, d + half] = 1.0
    return cosf, sinf, jnp.asarray(rot)


def _rope(t, cosf, sinf, rot):
    shp = t.shape
    t2 = t.reshape(-1, HEAD_DIM)
    sw = lax.dot_general(t2, rot, _DIMS["nn"], precision=lax.Precision.HIGH, preferred_element_type=F32).reshape(shp)
    return t * cosf + sw * sinf


def _softmax_sink_parts(s, sink):
    m = jnp.maximum(jnp.max(s, axis=-1, keepdims=True), sink)
    e = jnp.exp(s - m)
    es = jnp.exp(sink - m)
    r = 1.0 / (jnp.sum(e, axis=-1, keepdims=True) + es)
    return e * r, es * r


@jax.custom_vjp
def _softmax_sink(s, sink):
    return _softmax_sink_parts(s, sink)[0]


def _softmax_sink_fwd(s, sink):
    p, p_sink = _softmax_sink_parts(s, sink)
    return p, (p, p_sink)


def _softmax_sink_bwd(res, dp):
    p, p_sink = res
    t = jnp.sum(p * dp, axis=-1, keepdims=True)
    return p * (dp - t), -jnp.sum(p_sink * t, axis=1, keepdims=True)


_softmax_sink.defvjp(_softmax_sink_fwd, _softmax_sink_bwd)


def _attn_block_fn(q, kw, vw, sinks, cq, sq, ck, sk, rot, q0, k0):
    nk = kw.shape[1]
    qr = _rope(q, cq, sq, rot)
    kr = _rope(kw, ck, sk, rot)
    qpos = q0 + lax.broadcasted_iota(jnp.int32, (1, ATT_BLOCK, nk), 1)
    kpos = k0 + lax.broadcasted_iota(jnp.int32, (1, ATT_BLOCK, nk), 2)
    diff = qpos - kpos
    allowed = (diff >= 0) & (diff < WINDOW)
    outs = []
    for kh in range(ATT_KV_HEADS):
        qh = qr[kh * GQA_GROUP:(kh + 1) * GQA_GROUP].reshape(GQA_GROUP * ATT_BLOCK, HEAD_DIM)
        s = _mm_nt(qh, kr[kh]).reshape(GQA_GROUP, ATT_BLOCK, nk) * (HEAD_DIM ** -0.5)
        s = jnp.where(allowed, s, NEG_INF)
        p = _softmax_sink(s, sinks[kh * GQA_GROUP:(kh + 1) * GQA_GROUP])
        o = _mm_nn(p.reshape(GQA_GROUP * ATT_BLOCK, nk), vw[kh])
        outs.append(o.reshape(GQA_GROUP, ATT_BLOCK, HEAD_DIM))
    return jnp.concatenate(outs, axis=0)


def _attn_common(L):
    nwin = min(2 * ATT_BLOCK, L)
    qspec = pl.BlockSpec((ATT_HEADS, ATT_BLOCK, HEAD_DIM), lambda n: (0, n, 0))
    kvspec = pl.BlockSpec((ATT_KV_HEADS, L, HEAD_DIM), lambda n: (0, 0, 0))
    sspec = pl.BlockSpec((ATT_HEADS, 1, 1), lambda n: (0, 0, 0))
    tq = pl.BlockSpec((ATT_BLOCK, HEAD_DIM), lambda n: (n, 0))
    tk = pl.BlockSpec((L, HEAD_DIM), lambda n: (0, 0))
    rspec = pl.BlockSpec((HEAD_DIM, HEAD_DIM), lambda n: (0, 0))
    return nwin, qspec, kvspec, sspec, tq, tk, rspec


def _attn_fwd(qh, kh, vh, sinks, cosf, sinf, rot, *, name):
    L = qh.shape[1]
    nwin, qspec, kvspec, sspec, tq, tk, rspec = _attn_common(L)

    def body(q_ref, k_ref, v_ref, s_ref, cq_ref, sq_ref, ck_ref, sk_ref, r_ref, o_ref):
        n = pl.program_id(0)
        k0 = pl.multiple_of(jnp.maximum(n - 1, 0) * ATT_BLOCK, ATT_BLOCK)
        win = pl.ds(k0, nwin)
        o_ref[...] = _attn_block_fn(q_ref[...], k_ref[:, win, :], v_ref[:, win, :], s_ref[...],
                                    cq_ref[...], sq_ref[...], ck_ref[win, :], sk_ref[win, :], r_ref[...],
                                    n * ATT_BLOCK, k0)

    return pl.pallas_call(
        body, name=name, grid=(L // ATT_BLOCK,),
        in_specs=[qspec, kvspec, kvspec, sspec, tq, tq, tk, tk, rspec],
        out_specs=qspec,
        out_shape=jax.ShapeDtypeStruct((ATT_HEADS, L, HEAD_DIM), F32),
        compiler_params=_cp(("parallel",)),
    )(qh, kh, vh, sinks.reshape(ATT_HEADS, 1, 1), cosf, sinf, cosf, sinf, rot)


def _attn_bwd(do, qh, kh, vh, sinks, cosf, sinf, rot, *, name):
    L = qh.shape[1]
    nwin, qspec, kvspec, sspec, tq, tk, rspec = _attn_common(L)

    def body(do_ref, q_ref, k_ref, v_ref, s_ref, cq_ref, sq_ref, ck_ref, sk_ref, r_ref,
             dq_ref, dk_ref, dv_ref, ds_ref):
        n = pl.program_id(0)
        k0 = pl.multiple_of(jnp.maximum(n - 1, 0) * ATT_BLOCK, ATT_BLOCK)
        win = pl.ds(k0, nwin)
        cq, sq, ck, sk, rt = cq_ref[...], sq_ref[...], ck_ref[win, :], sk_ref[win, :], r_ref[...]
        q0 = n * ATT_BLOCK
        _, vjp = jax.vjp(lambda q, kw, vw, s: _attn_block_fn(q, kw, vw, s, cq, sq, ck, sk, rt, q0, k0),
                         q_ref[...], k_ref[:, win, :], v_ref[:, win, :], s_ref[...])
        dq, dkw, dvw, ds = vjp(do_ref[...])

        @pl.when(n == 0)
        def _():
            dk_ref[...] = jnp.zeros_like(dk_ref)
            dv_ref[...] = jnp.zeros_like(dv_ref)
            ds_ref[...] = jnp.zeros_like(ds_ref)

        dq_ref[...] = dq
        dk_ref[:, win, :] += dkw
        dv_ref[:, win, :] += dvw
        ds_ref[...] += ds

    return pl.pallas_call(
        body, name=name, grid=(L // ATT_BLOCK,),
        in_specs=[qspec, qspec, kvspec, kvspec, sspec, tq, tq, tk, tk, rspec],
        out_specs=[qspec, kvspec, kvspec, sspec],
        out_shape=[jax.ShapeDtypeStruct((ATT_HEADS, L, HEAD_DIM), F32),
                   jax.ShapeDtypeStruct((ATT_KV_HEADS, L, HEAD_DIM), F32),
                   jax.ShapeDtypeStruct((ATT_KV_HEADS, L, HEAD_DIM), F32),
                   jax.ShapeDtypeStruct((ATT_HEADS, 1, 1), F32)],
        compiler_params=_cp(("arbitrary",)),
    )(do, qh, kh, vh, sinks.reshape(ATT_HEADS, 1, 1), cosf, sinf, cosf, sinf, rot)


def _to_heads(t, nh):
    L = t.shape[0]
    return t.reshape(L, nh, HEAD_DIM).transpose(1, 0, 2)


def _from_heads(t):
    nh, L, _ = t.shape
    return t.transpose(1, 0, 2).reshape(L, nh * HEAD_DIM)


def _branch_fwd(ya, yb, o2d, zc, gates, wa, wb, wc, *, name, tm=256):
    L = ya.shape[0]
    tm = min(tm, L)
    W, D = SSM_WIDTH, D_MODEL

    def body(ya_ref, yb_ref, o_ref, zc_ref, g0_ref, g1_ref, g2_ref, wa_ref, wb_ref, wc_ref,
             mg_ref, ta_ref, tb_ref, tc_ref, yc_ref):
        yc = (o_ref[...] * _silu(zc_ref[...])).astype(MXU)
        ta = _dot(ya_ref[...], wa_ref[...], "nt")
        tb = _dot(yb_ref[...], wb_ref[...], "nt")
        tc = _dot(yc, wc_ref[...], "nt")
        ta_ref[...] = ta
        tb_ref[...] = tb
        tc_ref[...] = tc
        yc_ref[...] = yc
        mg_ref[...] = (jax.nn.sigmoid(g0_ref[...]) * ta + jax.nn.sigmoid(g1_ref[...]) * tb
                       + jax.nn.sigmoid(g2_ref[...]) * tc).astype(MXU)

    row = pl.BlockSpec((tm, W), lambda i: (i, 0))
    wide = pl.BlockSpec((tm, D), lambda i: (i, 0))
    gate = lambda c: pl.BlockSpec((tm, D), lambda i, c=c: (i, c))
    wspec = pl.BlockSpec((D, W), lambda i: (0, 0))
    return pl.pallas_call(
        body, name=name, grid=(L // tm,),
        in_specs=[row, row, row, row, gate(0), gate(1), gate(2), wspec, wspec, wspec],
        out_specs=[wide, wide, wide, wide, row],
        out_shape=[jax.ShapeDtypeStruct((L, D), MXU), jax.ShapeDtypeStruct((L, D), F32),
                   jax.ShapeDtypeStruct((L, D), F32), jax.ShapeDtypeStruct((L, D), F32),
                   jax.ShapeDtypeStruct((L, W), MXU)],
        compiler_params=_cp(("parallel",), 56),
    )(ya, yb, o2d, zc, gates, gates, gates, wa, wb, wc)


def _branch_bwd(dmg, ta, tb, tc, gates, *, name, tm=256):
    L = dmg.shape[0]
    tm = min(tm, L)
    D = D_MODEL

    def body(dm_ref, ta_ref, tb_ref, tc_ref, g0_ref, g1_ref, g2_ref, da_ref, db_ref, dc_ref, dg_ref):
        dm = dm_ref[...]
        for i, (t_ref, g_ref, d_ref) in enumerate(((ta_ref, g0_ref, da_ref), (tb_ref, g1_ref, db_ref),
                                                   (tc_ref, g2_ref, dc_ref))):
            sg = jax.nn.sigmoid(g_ref[...])
            d_ref[...] = (sg * dm).astype(MXU)
            dg_ref[:, i * D:(i + 1) * D] = (dm * t_ref[...] * sg * (1.0 - sg)).astype(MXU)

    wide = pl.BlockSpec((tm, D), lambda i: (i, 0))
    gate = lambda c: pl.BlockSpec((tm, D), lambda i, c=c: (i, c))
    bf = jax.ShapeDtypeStruct((L, D), MXU)
    return pl.pallas_call(
        body, name=name, grid=(L // tm,),
        in_specs=[wide, wide, wide, wide, gate(0), gate(1), gate(2)],
        out_specs=[wide, wide, wide, pl.BlockSpec((tm, 3 * D), lambda i: (i, 0))],
        out_shape=[bf, bf, bf, jax.ShapeDtypeStruct((L, 3 * D), MXU)],
        compiler_params=_cp(("parallel",), 56),
    )(dmg, ta, tb, tc, gates, gates, gates)


def _gate_c_bwd(dyc, o2d, zc, *, name, tm=256):
    L, W = dyc.shape
    tm = min(tm, L)

    def body(dy_ref, o_ref, z_ref, do_ref, dz_ref):
        _, vjp = jax.vjp(lambda o, z: o * _silu(z), o_ref[...], z_ref[...])
        do, dz = vjp(dy_ref[...])
        do_ref[...] = do
        dz_ref[...] = dz.astype(MXU)

    row = pl.BlockSpec((tm, W), lambda i: (i, 0))
    return pl.pallas_call(body, name=name, grid=(L // tm,), in_specs=[row, row, row], out_specs=[row, row],
                          out_shape=[jax.ShapeDtypeStruct((L, W), F32), jax.ShapeDtypeStruct((L, W), MXU)],
                          compiler_params=_cp(("parallel",)))(dyc, o2d, zc)


def _adamw(w, g, m, v, *, name):
    shape = w.shape
    cols = shape[-1]
    w2, g2, m2, v2 = (t.reshape(-1, cols) for t in (w, g, m, v))
    rows = w2.shape[0]
    tc = 1024 if cols % 1024 == 0 else cols
    lane_cols = -(-tc // 128) * 128
    tr = rows
    while tr % 16 == 0 and tr * lane_cols * 4 > 2 * _MB:
        tr //= 2

    def body(w_ref, g_ref, m_ref, v_ref, d_ref, nm_ref, nv_ref):
        gv = g_ref[...]
        nm = ADAM_B1 * m_ref[...] + (1.0 - ADAM_B1) * gv
        nv = ADAM_B2 * v_ref[...] + (1.0 - ADAM_B2) * jnp.square(gv)
        m_hat = nm / (1.0 - ADAM_B1 ** ADAM_STEP)
        v_hat = nv / (1.0 - ADAM_B2 ** ADAM_STEP)
        d_ref[...] = -ADAM_LR * (m_hat / (jnp.sqrt(v_hat) + ADAM_EPS) + ADAM_WD * w_ref[...])
        nm_ref[...] = nm
        nv_ref[...] = nv

    spec = pl.BlockSpec((tr, tc), lambda i, j: (i, j))
    out = jax.ShapeDtypeStruct((rows, cols), F32)
    d, nm, nv = pl.pallas_call(body, name=name, grid=(rows // tr, cols // tc), in_specs=[spec] * 4,
                               out_specs=[spec] * 3, out_shape=[out, out, out],
                               compiler_params=_cp(("parallel", "parallel")))(w2, g2, m2, v2)
    return d.reshape(shape), nm.reshape(shape), nv.reshape(shape)


def _adamw_layer(w, g, m, v, l, prev, *, name):
    _, rows, cols = w.shape
    tc = 1024 if cols % 1024 == 0 else cols
    tr = rows
    while tr % 16 == 0 and tr * tc * 4 > 2 * _MB:
        tr //= 2

    def body(w_ref, g_ref, m_ref, v_ref, *rest):
        go_ref, d_ref, nm_ref, nv_ref = rest[-4:]
        gv = g_ref[...]
        nm = ADAM_B1 * m_ref[...] + (1.0 - ADAM_B1) * gv
        nv = ADAM_B2 * v_ref[...] + (1.0 - ADAM_B2) * jnp.square(gv)
        m_hat = nm / (1.0 - ADAM_B1 ** ADAM_STEP)
        v_hat = nv / (1.0 - ADAM_B2 ** ADAM_STEP)
        d_ref[...] = -ADAM_LR * (m_hat / (jnp.sqrt(v_hat) + ADAM_EPS) + ADAM_WD * w_ref[...])
        nm_ref[...] = nm
        nv_ref[...] = nv
        go_ref[...] = gv

    lspec = pl.BlockSpec((None, tr, tc), lambda i, j: (l, i, j))
    gspec = pl.BlockSpec((tr, tc), lambda i, j: (i, j))
    out = jax.ShapeDtypeStruct(w.shape, F32)
    extra = [] if prev is None else list(prev)
    return pl.pallas_call(
        body, name=name, grid=(rows // tr, cols // tc),
        in_specs=[lspec, gspec, lspec, lspec] + [_ANY] * len(extra),
        out_specs=[lspec] * 4, out_shape=[out] * 4,
        input_output_aliases={4 + i: i for i in range(len(extra))},
        compiler_params=_cp(("parallel", "parallel")),
    )(w, g, m, v, *extra)


def _prep_layer(p, l, after=None):
    are = p["ssm_a_re"][l].reshape(SSM_CH, 1)
    aim = p["ssm_a_im"][l].reshape(SSM_CH, 1)
    ldt = p["ssm_log_dt"][l].reshape(1, SSM_GROUPS)
    bre = p["ssm_b_re"][l].reshape(SSM_CH, SSM_GROUP)
    bim = p["ssm_b_im"][l].reshape(SSM_CH, SSM_GROUP)
    lbr, lbi, bbr, bbi = _s5_params_fwd(are, aim, ldt, bre, bim, name=f"s5_params_fwd_{l}", after=after)
    cre = p["ssm_c_re"][l].transpose(0, 2, 1).reshape(SSM_CH, SSM_GROUP)
    cim = p["ssm_c_im"][l].transpose(0, 2, 1).reshape(SSM_CH, SSM_GROUP)
    return dict(raw=(are, aim, ldt, bre, bim),
                lbr=lbr.reshape(N_SLAB * SLAB_NC, 1, 128), lbi=lbi.reshape(N_SLAB * SLAB_NC, 1, 128),
                btr=_expand_bd(bbr), bti=_expand_bd(bbi), cbr=_expand_bd(cre), cbi=_expand_bd(cim),
                dvec=p["ssm_d"][l].reshape(1, SSM_WIDTH))


def _layer_fwd(x, h, p, sp, winT, rest_of, l, tabs, proj_after=None, after_main=None):
    L = x.shape[0]
    cosf, sinf, rot = tabs
    mm = functools.partial(_matmul, h, winT, "nt", tm=L, tn=256, tk=D_MODEL)
    main = mm(name=f"proj_main_{l}", shape=(L, N_MAIN, D_MODEL), after=proj_after)
    then = proj_after if after_main is None else after_main(main)
    zc = mm(name=f"proj_zc_{l}", shape=(L, N_ZC, D_MODEL), b_off=(N_MAIN // 256, 0), after=then)
    gates = mm(name=f"proj_gates_{l}", shape=(L, N_GATES, D_MODEL), b_off=((N_MAIN + N_ZC) // 256, 0), after=then)
    big, token = rest_of([main, zc, gates])
    ua = _time_interleave(main[:, :SSM_WIDTH])
    ys, sr, si = _s5_fwd(ua, sp["btr"], sp["bti"], sp["cbr"], sp["cbi"], sp["lbr"], sp["lbi"], sp["dvec"],
                         name=f"s5_fwd_{l}", after=token)
    ys = _time_deinterleave(ys)
    ya = _glu_fwd(ys, main, big["glu_w"], p["ssm_glu_b"][l], name=f"glu_fwd_{l}")
    yb = _sg_fwd(main, p["sg_ln_w"][l], p["sg_ln_b"][l], p["sg_w"][l], p["sg_b"][l], name=f"sg_fwd_{l}")
    qh = _to_heads(main[:, 5120:6144], ATT_HEADS)
    kh = _to_heads(main[:, 6144:6272], ATT_KV_HEADS)
    vh = _to_heads(main[:, 6272:6400], ATT_KV_HEADS)
    oh = _attn_fwd(qh, kh, vh, p["attn_sinks"][l], cosf, sinf, rot, name=f"attn_fwd_{l}")
    o2d = _from_heads(oh)
    mg, ta, tb, tc, yc = _branch_fwd(ya, yb, o2d, zc, gates, big["wbaT"], big["wbbT"], big["wbcT"],
                                     name=f"branch_fwd_{l}")
    xn = _matmul(mg, big["w_out"], "nn", name=f"out_fwd_{l}", shape=(L, D_MODEL, D_MODEL), tm=512, tn=512,
                 tk=D_MODEL, add=x)
    saved = dict(x=x, h=h, main=main, zc=zc, gates=gates, ua=ua, ys=ys, sr=sr, si=si, ya=ya, yb=yb, yc=yc, o2d=o2d,
                 qh=qh, kh=kh, vh=vh, mg=mg, ta=ta, tb=tb, tc=tc, sp=sp)
    return xn, saved, big


def _layer_bwd(dxn, s, p, big, l, tabs, early, mid):
    L = dxn.shape[0]
    D, W = D_MODEL, SSM_WIDTH
    cosf, sinf, rot = tabs
    sp = s["sp"]
    g = {}
    dmg = _matmul(dxn, big["w_out"], "nt", name=f"out_bwd_dm_{l}", shape=(L, D, D), tm=512, tn=512, tk=D)
    g["w_out"] = _matmul(s["mg"], dxn, "tn", name=f"out_bwd_dw_{l}", shape=(D, D, L), tm=512, tn=512, tk=L,
                         out_dtype=MXU)
    dta, dtb, dtc, dgates = _branch_bwd(dmg, s["ta"], s["tb"], s["tc"], s["gates"], name=f"branch_bwd_{l}")
    dys_ = {}
    for nm, dt, y, wt in (("a", dta, s["ya"], big["wbaT"]), ("b", dtb, s["yb"], big["wbbT"]),
                          ("c", dtc, s["yc"], big["wbcT"])):
        dys_[nm] = _matmul(dt, wt, "nn", name=f"branch_bwd_dy{nm}_{l}", shape=(L, W, D), tm=512, tn=512, tk=D)
        g["wb" + nm + "T"] = _matmul(dt, y, "tn", name=f"branch_bwd_dw{nm}_{l}", shape=(D, W, L),
                                     tm=512, tn=512, tk=L, out_dtype=MXU)
    do2d, dzc = _gate_c_bwd(dys_["c"], s["o2d"], s["zc"], name=f"gate_c_bwd_{l}")
    dqh, dkh, dvh, dsinks = _attn_bwd(_to_heads(do2d, ATT_HEADS), s["qh"], s["kh"], s["vh"], p["attn_sinks"][l],
                                      cosf, sinf, rot, name=f"attn_bwd_{l}")
    g["attn_sinks"] = dsinks.reshape(ATT_HEADS)
    dub, dvb, dzb, dlnw, dlnb, dsgw, dsgb = _sg_bwd(dys_["b"], s["main"], p["sg_ln_w"][l], p["sg_ln_b"][l],
                                                    p["sg_w"][l], p["sg_b"][l], name=f"sg_bwd_{l}")
    g["sg_ln_w"], g["sg_ln_b"] = dlnw.reshape(W), dlnb.reshape(W)
    g["sg_w"], g["sg_b"] = dsgw, dsgb.reshape(SG_HEADS, SG_CHUNK)
    dys, dza, a1, dzl, dgb = _glu_bwd(dys_["a"], s["ys"], s["main"], big["glu_w"], p["ssm_glu_b"][l],
                                      name=f"glu_bwd_{l}")
    g["ssm_glu_b"] = dgb.reshape(W)
    g["glu_w"] = _matmul(a1, dzl, "tn", name=f"glu_bwd_dw_{l}", shape=(W, W, L), tm=512, tn=512, tk=L, out_dtype=MXU)
    token = early(g)
    dua, dbtr, dbti, dcbr, dcbi, dlr, dli, dd = _s5_bwd(_time_interleave(dys), s["ua"], s["sr"], s["si"], sp["btr"],
                                                        sp["bti"], sp["cbr"], sp["cbi"], sp["lbr"], sp["lbi"],
                                                        sp["dvec"], name=f"s5_bwd_{l}", after=token)
    token = mid(dua)
    dua = _time_deinterleave(dua)
    g["ssm_d"] = dd.reshape(W)
    to_c = lambda t: _contract_bd(t).reshape(SSM_GROUPS, SSM_STATE, SSM_GROUP).transpose(0, 2, 1)
    g["ssm_c_re"], g["ssm_c_im"] = to_c(dcbr), to_c(dcbi)
    dare, daim, dldt, dbre, dbim = _s5_params_bwd(*sp["raw"], dlr.reshape(SSM_CH, 1), dli.reshape(SSM_CH, 1),
                                                  _contract_bd(dbtr), _contract_bd(dbti),
                                                  name=f"s5_params_bwd_{l}")
    g["ssm_a_re"] = dare.reshape(SSM_GROUPS, SSM_STATE)
    g["ssm_a_im"] = daim.reshape(SSM_GROUPS, SSM_STATE)
    g["ssm_log_dt"] = dldt.reshape(SSM_GROUPS)
    g["ssm_b_re"] = dbre.reshape(SSM_GROUPS, SSM_STATE, SSM_GROUP)
    g["ssm_b_im"] = dbim.reshape(SSM_GROUPS, SSM_STATE, SSM_GROUP)
    dproj = jnp.concatenate([t.astype(MXU) for t in (dua, dza, dub, dvb, dzb, _from_heads(dqh), _from_heads(dkh),
                                                     _from_heads(dvh), dzc, dgates)], axis=1)
    g["winT"] = _matmul(dproj, s["h"], "tn", name=f"proj_bwd_dw_{l}", shape=(D_IN, D, L), tm=256, tn=D, tk=L,
                        out_dtype=MXU, after=token)
    return dproj, g


def _proj_bwd_dh(dproj, winT, l, after):
    return _matmul(dproj, winT, "nn", name=f"proj_bwd_dh_{l}", shape=(dproj.shape[0], D_MODEL, D_IN), tm=512, tn=512,
                   tk=D_IN // 2, after=after)


MESH = pl.DeviceIdType.MESH
_ANY = pl.BlockSpec(memory_space=pl.ANY)
ROW_ALIGN = 16


def _coords():
    return lax.axis_index("x"), lax.axis_index("y"), lax.axis_index("c")


def _gather8(arrs, *, name):
    n = len(arrs)
    rows = [a.shape[0] for a in arrs]
    for r in rows:
        assert r % ROW_ALIGN == 0

    def body(*refs):
        ins, outs = refs[:n], refs[n:2 * n]
        send, recv, lsem = refs[2 * n:]
        x, y, c = _coords()
        me, sibling = (x, y, c), (x, y, 1 - c)
        chips = [(1 - x, y), (x, 1 - y), (1 - x, 1 - y)]

        def blk(a, px, py, pc):
            return outs[a].at[pl.ds(pl.multiple_of((4 * px + 2 * py + pc) * rows[a], ROW_ALIGN), rows[a]), :]

        def own(a):
            return ins[a]

        def copy(a, k, block, to, src=None):
            return pltpu.make_async_remote_copy(
                src_ref=blk(a, *block) if src is None else src, dst_ref=blk(a, *block),
                send_sem=send.at[a, k], recv_sem=recv.at[a, k], device_id=to, device_id_type=MESH)

        mine, first, passed = [], [], []
        for a in range(n):
            mine.append(pltpu.make_async_copy(own(a), blk(a, *me), lsem.at[a]))
            mine[a].start()
            f = [copy(a, 0, me, sibling, src=own(a))]
            f += [copy(a, 1 + j, me, (*chip, c), src=own(a)) for j, chip in enumerate(chips)]
            for cp in f:
                cp.start()
            first.append(f)
        for a in range(n):
            ps = [copy(a, 4 + j, (*chip, c), sibling) for j, chip in enumerate(chips)]
            for j, chip in enumerate(chips):
                copy(a, 1 + j, (*chip, c), me).wait_recv()
                ps[j].start()
            passed.append(ps)
        for a in range(n):
            copy(a, 0, sibling, me).wait_recv()
            for j, chip in enumerate(chips):
                copy(a, 4 + j, (*chip, 1 - c), me).wait_recv()
            for cp in first[a] + passed[a]:
                cp.wait_send()
            mine[a].wait()

    return pl.pallas_call(
        body, name=name,
        in_specs=[_ANY] * n, out_specs=[_ANY] * n,
        out_shape=[jax.ShapeDtypeStruct((8 * r,) + a.shape[1:], a.dtype) for r, a in zip(rows, arrs)],
        scratch_shapes=[pltpu.SemaphoreType.DMA((n, 7)), pltpu.SemaphoreType.DMA((n, 7)), pltpu.SemaphoreType.DMA((n,))],
    )(*arrs)


def _sibling_swap(arrs, *, name):
    n = len(arrs)

    def body(*refs):
        ins, outs = refs[:n], refs[n:2 * n]
        send, recv = refs[2 * n:]
        x, y, c = _coords()
        cps = [pltpu.make_async_remote_copy(src_ref=ins[a].at[:, 1 - c], dst_ref=outs[a], send_sem=send.at[a],
                                            recv_sem=recv.at[a], device_id=(x, y, 1 - c), device_id_type=MESH)
               for a in range(n)]
        for cp in cps:
            cp.start()
        for cp in cps:
            cp.wait_recv()
        for cp in cps:
            cp.wait_send()

    return pl.pallas_call(
        body, name=name, in_specs=[_ANY] * n, out_specs=[_ANY] * n,
        out_shape=[jax.ShapeDtypeStruct((a.shape[0],) + a.shape[2:], a.dtype) for a in arrs],
        scratch_shapes=[pltpu.SemaphoreType.DMA((n,)), pltpu.SemaphoreType.DMA((n,))],
    )(*arrs)


def _col_tile(lead, rows, cols, itemsize=4, cap=4 * _MB):
    tc = cols
    while tc % 256 == 0 and lead * rows * tc * itemsize > cap:
        tc //= 2
    return tc


def _pair_sum(mine, theirs, *, name):
    _, _, rows, cols = mine.shape
    tc = _col_tile(1, rows, cols)
    c = lax.axis_index("c")

    def body(c_ref, a_ref, b_ref, o_ref):
        o_ref[...] = (a_ref[...].astype(F32) + b_ref[...].astype(F32)).astype(MXU)

    return pl.pallas_call(
        body, name=name,
        grid_spec=pltpu.PrefetchScalarGridSpec(
            num_scalar_prefetch=1, grid=(4, cols // tc),
            in_specs=[pl.BlockSpec((None, None, rows, tc), lambda j, i, cr: (j, cr[0], 0, i)),
                      pl.BlockSpec((None, rows, tc), lambda j, i, cr: (j, 0, i))],
            out_specs=pl.BlockSpec((None, rows, tc), lambda j, i, cr: (j, 0, i))),
        out_shape=jax.ShapeDtypeStruct((4, rows, cols), MXU),
        compiler_params=_cp(("parallel", "parallel")),
    )(c.reshape(1).astype(jnp.int32), mine, theirs)


_HBM = pl.BlockSpec(memory_space=pltpu.HBM)
_SEM = pl.BlockSpec(memory_space=pltpu.SEMAPHORE)
_EFFECT = pltpu.SideEffectType.DATAFLOW_SIDE_EFFECTING
N_PEER_CHIPS = 3


def _peer_chips(x, y):
    return [(1 - x, y), (x, 1 - y), (1 - x, 1 - y)]


def _split_start(srcs, lands, src_slot, dst_slot, *, name, after=()):
    n = len(srcs)
    ns = n * N_PEER_CHIPS
    first = 2 * n + len(after)

    def body(*refs):
        src_refs, land_refs = refs[:n], refs[n:2 * n]
        send, recv, token = refs[first:first + ns], refs[first + ns:first + 2 * ns], refs[-1]
        x, y, c = _coords()
        for a in range(n):
            for k, (px, py) in enumerate(_peer_chips(x, y)):
                pltpu.make_async_remote_copy(
                    src_ref=src_refs[a].at[src_slot(x, y, c, px, py)], dst_ref=land_refs[a].at[dst_slot(x, y, c)],
                    send_sem=send[a * N_PEER_CHIPS + k], recv_sem=recv[a * N_PEER_CHIPS + k],
                    device_id=(px, py, c), device_id_type=MESH).start()
        token[...] = jnp.zeros_like(token)

    bufs = list(srcs) + list(lands)
    res = pl.pallas_call(
        body, name=name,
        out_shape=(*[pltpu.SemaphoreType.DMA(())] * (2 * ns), *[pltpu.HBM(b.shape, b.dtype) for b in bufs],
                   jax.ShapeDtypeStruct((8, 128), F32)),
        in_specs=[_HBM] * (2 * n) + [_ANY] * len(after),
        out_specs=(*[_SEM] * (2 * ns), *[_HBM] * (2 * n), pl.BlockSpec(memory_space=pltpu.VMEM)),
        input_output_aliases={i: 2 * ns + i for i in range(2 * n)},
        compiler_params=pltpu.CompilerParams(has_side_effects=_EFFECT),
    )(*[pltpu.with_memory_space_constraint(b, pltpu.HBM) for b in bufs], *after)
    sems = list(res[:2 * ns])
    return sems, list(res[2 * ns:2 * ns + n]), list(res[2 * ns + n:2 * ns + 2 * n]), res[-1]


def _split_wait(sems, srcs, lands, after, *, name):
    n = len(srcs)
    ns = n * N_PEER_CHIPS

    def body(*refs):
        src_refs, land_refs = refs[:n], refs[n:2 * n]
        send, recv = refs[2 * n:2 * n + ns], refs[2 * n + ns:2 * n + 2 * ns]
        x, y, c = _coords()
        for a in range(n):
            for k in range(N_PEER_CHIPS):
                cp = pltpu.make_async_remote_copy(
                    src_ref=src_refs[a].at[0], dst_ref=land_refs[a].at[0], send_sem=send[a * N_PEER_CHIPS + k],
                    recv_sem=recv[a * N_PEER_CHIPS + k], device_id=(x, y, 1 - c), device_id_type=MESH)
                cp.wait_send()
                cp.wait_recv()

    bufs = list(srcs) + list(lands)
    res = pl.pallas_call(
        body, name=name,
        out_shape=tuple(pltpu.HBM(b.shape, b.dtype) for b in bufs),
        in_specs=[_HBM] * (2 * n) + [_SEM] * (2 * ns) + [_ANY] * len(after),
        out_specs=tuple([_HBM] * (2 * n)),
        input_output_aliases={i: i for i in range(2 * n)},
        compiler_params=pltpu.CompilerParams(has_side_effects=_EFFECT),
    )(*bufs, *sems, *after)
    return list(res[:n]), list(res[n:])


def _fill_own(shard2, *, name, after=None):
    _, rows, cols = shard2.shape
    tc = _col_tile(1, rows, cols, itemsize=shard2.dtype.itemsize)
    j = 2 * lax.axis_index("x") + lax.axis_index("y")
    extra = [] if after is None else [after]

    def body(j_ref, s_ref, *rest):
        rest[-1][...] = s_ref[...]

    return pl.pallas_call(
        body, name=name,
        grid_spec=pltpu.PrefetchScalarGridSpec(
            num_scalar_prefetch=1, grid=(2, cols // tc),
            in_specs=([pl.BlockSpec((None, rows, tc), lambda h, i, jr: (h, 0, i))]
                      + [pl.BlockSpec(memory_space=pl.ANY)] * len(extra)),
            out_specs=pl.BlockSpec((None, rows, tc), lambda h, i, jr: (2 * jr[0] + h, 0, i))),
        out_shape=jax.ShapeDtypeStruct((8, rows, cols), shard2.dtype),
        compiler_params=_cp(("parallel", "parallel")),
    )(j.reshape(1).astype(jnp.int32), shard2, *extra)


def _pass_to_sibling(lands, *, name):
    n = len(lands)

    def body(*refs):
        outs = refs[n:2 * n]
        send, recv = refs[2 * n:]
        x, y, c = _coords()
        cps = []
        for a in range(n):
            for k, (px, py) in enumerate(_peer_chips(x, y)):
                slot = 4 * px + 2 * py + c
                cps.append(pltpu.make_async_remote_copy(
                    src_ref=outs[a].at[slot], dst_ref=outs[a].at[slot], send_sem=send.at[a, k], recv_sem=recv.at[a, k],
                    device_id=(x, y, 1 - c), device_id_type=MESH))
        for cp in cps:
            cp.start()
        for cp in cps:
            cp.wait_recv()
        for cp in cps:
            cp.wait_send()

    return pl.pallas_call(
        body, name=name, in_specs=[_ANY] * n, out_specs=[_ANY] * n,
        out_shape=[jax.ShapeDtypeStruct(b.shape, b.dtype) for b in lands],
        input_output_aliases={a: a for a in range(n)},
        scratch_shapes=[pltpu.SemaphoreType.DMA((n, N_PEER_CHIPS)), pltpu.SemaphoreType.DMA((n, N_PEER_CHIPS))],
    )(*lands)


def _sum_parts(parts, got, *, name):
    _, rows, cols = parts.shape
    tc = _col_tile(4, rows, cols, itemsize=parts.dtype.itemsize)
    x, y, c = _coords()
    idx = jnp.stack([2 * x + y, 2 * (1 - x) + y, 2 * x + (1 - y), 2 * (1 - x) + (1 - y), c]).astype(jnp.int32)

    def body(i_ref, p_ref, g0_ref, g1_ref, g2_ref, o_ref):
        o_ref[...] = ((p_ref[...].astype(F32) + g0_ref[...].astype(F32)) + g1_ref[...].astype(F32)) + g2_ref[...].astype(F32)

    slot = lambda s: pl.BlockSpec((None, rows, tc), lambda i, ir, s=s: (ir[s], 0, i))
    return pl.pallas_call(
        body, name=name,
        grid_spec=pltpu.PrefetchScalarGridSpec(
            num_scalar_prefetch=1, grid=(cols // tc,),
            in_specs=[slot(0), slot(1), slot(2), slot(3)],
            out_specs=pl.BlockSpec((None, rows, tc), lambda i, ir: (ir[4], 0, i))),
        out_shape=jax.ShapeDtypeStruct((2, rows, cols), F32),
        compiler_params=_cp(("parallel",)),
    )(idx, parts, got, got, got)


def _sum_slots(t, *, name):
    S, rows, cols = t.shape
    tc = _col_tile(S, rows, cols)

    def body(t_ref, o_ref):
        acc = t_ref[0].astype(F32)
        for s in range(1, S):
            acc = acc + t_ref[s].astype(F32)
        o_ref[...] = acc

    return pl.pallas_call(
        body, name=name, grid=(cols // tc,),
        in_specs=[pl.BlockSpec((S, rows, tc), lambda i: (0, 0, i))],
        out_specs=pl.BlockSpec((rows, tc), lambda i: (0, i)),
        out_shape=jax.ShapeDtypeStruct((rows, cols), F32),
        compiler_params=_cp(("parallel",)),
    )(t)


def _halves_join(bufs, *, name):
    n = len(bufs)

    def body(*refs):
        outs = refs[n:2 * n]
        send, recv = refs[2 * n:]
        x, y, c = _coords()
        cps = [pltpu.make_async_remote_copy(src_ref=outs[a].at[c], dst_ref=outs[a].at[c], send_sem=send.at[a],
                                            recv_sem=recv.at[a], device_id=(x, y, 1 - c), device_id_type=MESH)
               for a in range(n)]
        for cp in cps:
            cp.start()
        for cp in cps:
            cp.wait_recv()
        for cp in cps:
            cp.wait_send()

    return pl.pallas_call(
        body, name=name, in_specs=[_ANY] * n, out_specs=[_ANY] * n,
        out_shape=[jax.ShapeDtypeStruct(b.shape, b.dtype) for b in bufs],
        input_output_aliases={a: a for a in range(n)},
        scratch_shapes=[pltpu.SemaphoreType.DMA((n,)), pltpu.SemaphoreType.DMA((n,))],
    )(*bufs)


def _swap_start(srcs, *, name):
    n = len(srcs)
    lands = [lax.empty((s.shape[0],) + s.shape[2:], s.dtype) for s in srcs]

    def body(*refs):
        src_refs, land_refs = refs[:n], refs[n:2 * n]
        send, recv, token = refs[2 * n:3 * n], refs[3 * n:4 * n], refs[-1]
        x, y, c = _coords()
        for a in range(n):
            pltpu.make_async_remote_copy(src_ref=src_refs[a].at[:, 1 - c], dst_ref=land_refs[a], send_sem=send[a],
                                         recv_sem=recv[a], device_id=(x, y, 1 - c), device_id_type=MESH).start()
        token[...] = jnp.zeros_like(token)

    bufs = list(srcs) + lands
    res = pl.pallas_call(
        body, name=name,
        out_shape=(*[pltpu.SemaphoreType.DMA(())] * (2 * n), *[pltpu.HBM(b.shape, b.dtype) for b in bufs],
                   jax.ShapeDtypeStruct((8, 128), F32)),
        in_specs=[_HBM] * (2 * n),
        out_specs=(*[_SEM] * (2 * n), *[_HBM] * (2 * n), pl.BlockSpec(memory_space=pltpu.VMEM)),
        input_output_aliases={i: 2 * n + i for i in range(2 * n)},
        compiler_params=pltpu.CompilerParams(has_side_effects=_EFFECT),
    )(*[pltpu.with_memory_space_constraint(b, pltpu.HBM) for b in bufs])
    return list(res[:2 * n]), list(res[2 * n:3 * n]), list(res[3 * n:4 * n]), res[-1]


def _swap_wait(sems, srcs, lands, after, *, name):
    n = len(srcs)

    def body(*refs):
        src_refs, land_refs = refs[:n], refs[n:2 * n]
        send, recv = refs[2 * n:3 * n], refs[3 * n:4 * n]
        x, y, c = _coords()
        for a in range(n):
            cp = pltpu.make_async_remote_copy(
                src_ref=src_refs[a].at[:, 0], dst_ref=land_refs[a], send_sem=send[a], recv_sem=recv[a],
                device_id=(x, y, 1 - c), device_id_type=MESH)
            cp.wait_send()
            cp.wait_recv()

    bufs = list(srcs) + list(lands)
    res = pl.pallas_call(
        body, name=name,
        out_shape=tuple(pltpu.HBM(b.shape, b.dtype) for b in bufs),
        in_specs=[_HBM] * (2 * n) + [_SEM] * (2 * n) + [_ANY] * len(after),
        out_specs=tuple([_HBM] * (2 * n)),
        input_output_aliases={i: i for i in range(2 * n)},
        compiler_params=pltpu.CompilerParams(has_side_effects=_EFFECT),
    )(*bufs, *sems, *after)
    return list(res[:n]), list(res[n:])


def _grad_views(grads):
    return [g.reshape(4, 2, g.shape[0] // 8, g.shape[1]) for g in grads]


def _scatter_begin(views, theirs, *, tag):
    parts = [_pair_sum(v, t, name=f"rs_pair_{tag}_{i}") for i, (v, t) in enumerate(zip(views, theirs))]
    got = [lax.empty(p.shape, p.dtype) for p in parts]
    sems, parts, got, token = _split_start(
        parts, got, lambda x, y, c, px, py: 2 * px + py, lambda x, y, c: 2 * x + y, name=f"rs_start_{tag}")
    return (sems, parts, got), token


def _reduce_scatter_begin(grads, *, tag):
    views = _grad_views(grads)
    theirs = _sibling_swap(views, name=f"rs_swap_{tag}")
    return _scatter_begin(views, theirs, tag=tag)


def _reduce_scatter_end(state, after, *, tag):
    sems, parts, got = state
    parts, got = _split_wait(sems, parts, got, after, name=f"rs_wait_{tag}")
    halves = [_sum_parts(p, t, name=f"rs_sum_{tag}_{i}") for i, (p, t) in enumerate(zip(parts, got))]
    joined = _halves_join(halves, name=f"rs_join_{tag}")
    return [j.reshape(2 * j.shape[1], j.shape[2]) for j in joined]


_SMALL = ("norm_w", "ssm_a_re", "ssm_a_im", "ssm_log_dt", "ssm_b_re", "ssm_b_im", "ssm_c_re", "ssm_c_im", "ssm_d",
          "ssm_glu_b", "sg_ln_w", "sg_ln_b", "sg_w", "sg_b", "attn_sinks", "final_norm_w")
_BIG = ("w_in", "ssm_glu_w", "w_branch_a", "w_branch_b", "w_branch_c", "w_out")
_WEIGHTS = ("norm_w", "w_in", "ssm_a_re", "ssm_a_im", "ssm_log_dt", "ssm_b_re", "ssm_b_im", "ssm_c_re", "ssm_c_im",
            "ssm_d", "ssm_glu_w", "ssm_glu_b", "sg_ln_w", "sg_ln_b", "sg_w", "sg_b", "attn_sinks", "w_branch_a",
            "w_branch_b", "w_branch_c", "w_out", "final_norm_w")
_PACK_COLS = 1024
_PACK_ALIGN = 8 * ROW_ALIGN * _PACK_COLS


def _slice_exchange(buf, *, name, after=()):
    def body(in_ref, *rest):
        out_ref, send, recv, lsem = rest[-4:]
        x, y, c = _coords()
        me = 4 * x + 2 * y + c
        own = pltpu.make_async_copy(in_ref.at[me], out_ref.at[me], lsem)
        own.start()
        cps = []
        for k in range(1, 8):
            px, py, pc = x ^ (k >> 2), y ^ ((k >> 1) & 1), c ^ (k & 1)
            cps.append(pltpu.make_async_remote_copy(
                src_ref=in_ref.at[4 * px + 2 * py + pc], dst_ref=out_ref.at[me], send_sem=send.at[k - 1],
                recv_sem=recv.at[k - 1], device_id=(px, py, pc), device_id_type=MESH))
        for cp in cps:
            cp.start()
        for cp in cps:
            cp.wait_recv()
        for cp in cps:
            cp.wait_send()
        own.wait()

    return pl.pallas_call(
        body, name=name, in_specs=[_ANY] * (1 + len(after)), out_specs=_ANY,
        out_shape=jax.ShapeDtypeStruct(buf.shape, buf.dtype),
        scratch_shapes=[pltpu.SemaphoreType.DMA((7,)), pltpu.SemaphoreType.DMA((7,)), pltpu.SemaphoreType.DMA],
    )(buf, *after)


def _allreduce_small(packed, after=()):
    rows, cols = packed.shape
    got = _slice_exchange(packed.reshape(8, rows // 8, cols), name="small_grads_exchange", after=after)
    mine = _sum_slots(got, name="small_grads_sum")
    return _gather8([mine], name="small_grads_gather")[0]


def _pack(ts):
    flat = jnp.concatenate([t.reshape(-1) for t in ts])
    pad = (-flat.shape[0]) % _PACK_ALIGN
    return jnp.pad(flat, (0, pad)).reshape(-1, _PACK_COLS)


def _unpack(buf, like):
    flat = buf.reshape(-1)
    out, pos = [], 0
    for t in like:
        out.append(flat[pos:pos + t.size].reshape(t.shape))
        pos += t.size
    return out


def kernel(x, norm_w, w_in, ssm_a_re, ssm_a_im, ssm_log_dt, ssm_b_re, ssm_b_im, ssm_c_re, ssm_c_im, ssm_d, ssm_glu_w, ssm_glu_b, sg_ln_w, sg_ln_b, sg_w, sg_b, attn_sinks, w_branch_a, w_branch_b, w_branch_c, w_out, final_norm_w, loss_target, m_norm_w, m_w_in, m_ssm_a_re, m_ssm_a_im, m_ssm_log_dt, m_ssm_b_re, m_ssm_b_im, m_ssm_c_re, m_ssm_c_im, m_ssm_d, m_ssm_glu_w, m_ssm_glu_b, m_sg_ln_w, m_sg_ln_b, m_sg_w, m_sg_b, m_attn_sinks, m_w_branch_a, m_w_branch_b, m_w_branch_c, m_w_out, m_final_norm_w, v_norm_w, v_w_in, v_ssm_a_re, v_ssm_a_im, v_ssm_log_dt, v_ssm_b_re, v_ssm_b_im, v_ssm_c_re, v_ssm_c_im, v_ssm_d, v_ssm_glu_w, v_ssm_glu_b, v_sg_ln_w, v_sg_ln_b, v_sg_w, v_sg_b, v_attn_sinks, v_w_branch_a, v_w_branch_b, v_w_branch_c, v_w_out, v_final_norm_w):
    w = dict(norm_w=norm_w, w_in=w_in, ssm_a_re=ssm_a_re, ssm_a_im=ssm_a_im, ssm_log_dt=ssm_log_dt, ssm_b_re=ssm_b_re,
             ssm_b_im=ssm_b_im, ssm_c_re=ssm_c_re, ssm_c_im=ssm_c_im, ssm_d=ssm_d, ssm_glu_w=ssm_glu_w,
             ssm_glu_b=ssm_glu_b, sg_ln_w=sg_ln_w, sg_ln_b=sg_ln_b, sg_w=sg_w, sg_b=sg_b, attn_sinks=attn_sinks,
             w_branch_a=w_branch_a, w_branch_b=w_branch_b, w_branch_c=w_branch_c, w_out=w_out,
             final_norm_w=final_norm_w)
    m = dict(norm_w=m_norm_w, w_in=m_w_in, ssm_a_re=m_ssm_a_re, ssm_a_im=m_ssm_a_im, ssm_log_dt=m_ssm_log_dt,
             ssm_b_re=m_ssm_b_re, ssm_b_im=m_ssm_b_im, ssm_c_re=m_ssm_c_re, ssm_c_im=m_ssm_c_im, ssm_d=m_ssm_d,
             ssm_glu_w=m_ssm_glu_w, ssm_glu_b=m_ssm_glu_b, sg_ln_w=m_sg_ln_w, sg_ln_b=m_sg_ln_b, sg_w=m_sg_w,
             sg_b=m_sg_b, attn_sinks=m_attn_sinks, w_branch_a=m_w_branch_a, w_branch_b=m_w_branch_b,
             w_branch_c=m_w_branch_c, w_out=m_w_out, final_norm_w=m_final_norm_w)
    v = dict(norm_w=v_norm_w, w_in=v_w_in, ssm_a_re=v_ssm_a_re, ssm_a_im=v_ssm_a_im, ssm_log_dt=v_ssm_log_dt,
             ssm_b_re=v_ssm_b_re, ssm_b_im=v_ssm_b_im, ssm_c_re=v_ssm_c_re, ssm_c_im=v_ssm_c_im, ssm_d=v_ssm_d,
             ssm_glu_w=v_ssm_glu_w, ssm_glu_b=v_ssm_glu_b, sg_ln_w=v_sg_ln_w, sg_ln_b=v_sg_ln_b, sg_w=v_sg_w,
             sg_b=v_sg_b, attn_sinks=v_attn_sinks, w_branch_a=v_w_branch_a, w_branch_b=v_w_branch_b,
             w_branch_c=v_w_branch_c, w_out=v_w_out, final_norm_w=v_final_norm_w)

    big_names = ("winT", "glu_w", "wbaT", "wbbT", "wbcT", "w_out")
    L = x.shape[1]
    tabs = _rope_tables(L)
    p = {k: w[k] for k in _SMALL}

    column_sharded = ("w_in", "w_branch_a", "w_branch_b", "w_branch_c")

    def shard_halves(ws, names=_BIG):
        ts = [(t.T if k in column_sharded else t).astype(MXU) for k, t in zip(names, ws)]
        return [t.reshape(2, t.shape[0] // 2, t.shape[1]) for t in ts]

    my_half = lambda x_, y_, c_, px, py: c_
    my_block = lambda x_, y_, c_: 4 * x_ + 2 * y_ + c_
    rows_of = lambda lands: [t.reshape(8 * t.shape[1], t.shape[2]) for t in lands]
    fill = lambda l, hs, i0=0: [_fill_own(s, name=f"gather_fill_{l}_{i0 + i}") for i, s in enumerate(hs)]
    saved = [None] * DEPTH

    halves = [shard_halves([w[k][l] for k in _BIG]) for l in range(DEPTH)]
    w_in_0 = halves[0][:1]
    sems_a, src_a, land_a, token_a = _split_start(w_in_0, fill(0, w_in_0), my_half, my_block, name="gather_start_0a")
    fill_after = lambda l, hs, i0: [_fill_own(s, name=f"gather_fill_{l}_{i0 + i}", after=token_a)
                                    for i, s in enumerate(hs)]
    lands = [[None] + fill_after(0, halves[0][1:], 1), fill_after(1, halves[1], 0)]
    sp = [_prep_layer(p, l, after=token_a) for l in range(DEPTH)]
    h0 = _rms_fwd(x[0], p["norm_w"][0], name="rms_fwd_0", after=token_a)
    _, land_a = _split_wait(sems_a, src_a, land_a, [h0, sp[0]["btr"], sp[1]["btr"]] + lands[0][1:] + lands[1],
                            name="gather_wait_0a")
    land_a = _pass_to_sibling(land_a, name="gather_pass_0a")
    sems_b, src_b, land_b, token_b = _split_start(halves[0][1:], lands[0][1:], my_half, my_block,
                                                  name="gather_start_0b", after=land_a)
    split1 = {}

    def start1(main):
        split1["sems"], split1["src"], split1["land"], token1 = _split_start(halves[1], lands[1], my_half, my_block,
                                                                             name="gather_start_1", after=[main])
        return token1

    def rest0(t):
        _, got = _split_wait(sems_b, src_b, land_b, t, name="gather_wait_0b")
        got = _pass_to_sibling(got, name="gather_pass_0b")
        return dict(zip(big_names, rows_of(land_a + got))), None

    x1, saved[0], big0 = _layer_fwd(x[0], h0, p, sp[0], rows_of(land_a)[0], rest0, 0, tabs, proj_after=token_b,
                                    after_main=start1)
    _, lands1 = _split_wait(split1["sems"], split1["src"], split1["land"], [x1], name="gather_wait_1")
    big1 = dict(zip(big_names, rows_of(_pass_to_sibling(lands1, name="gather_pass_1"))))
    bigs = [big0, big1]
    h1 = _rms_fwd(x1, p["norm_w"][1], name="rms_fwd_1")
    x2, saved[1], _ = _layer_fwd(x1, h1, p, sp[1], big1["winT"], lambda t: (big1, None), 1, tabs)
    loss, dx, dfw = _final_loss(x2, p["final_norm_w"], loss_target[0], name="final_loss")

    grads = [None] * DEPTH
    rs = {}

    def early(l):
        def begin(g):
            *rs[f"{l}a"], token_a = _swap_start(_grad_views([g[k] for k in big_names[1:]]), name=f"rs_swap_start_{l}a")
            return token_a
        return begin

    def mid(l):
        def go_on(t):
            sems, views, lands = rs[f"{l}a"]
            views, theirs = _swap_wait(sems, views, lands, [t], name=f"rs_swap_wait_{l}a")
            rs[f"{l}a"], token_a = _scatter_begin(views, theirs, tag=f"{l}a")
            return token_a
        return go_on

    def late(l, dproj, dx):
        if l == 0:
            rs["0b"], token_s = _reduce_scatter_begin([grads[0]["winT"]], tag="0b")
            dh = _proj_bwd_dh(dproj, bigs[0]["winT"], 0, token_s)
            return _rms_bwd(saved[0]["x"], p["norm_w"][0], dh, dx, name="rms_bwd_0")
        sems, views, lands, token_b = _swap_start(_grad_views([grads[l]["winT"]]), name=f"rs_swap_start_{l}b")
        dh = _proj_bwd_dh(dproj, bigs[l]["winT"], l, token_b)
        views, theirs = _swap_wait(sems, views, lands, [dh], name=f"rs_swap_wait_{l}b")
        rs[f"{l}b"], token_s = _scatter_begin(views, theirs, tag=f"{l}b")
        return _rms_bwd(saved[l]["x"], p["norm_w"][l], dh, dx, name=f"rms_bwd_{l}", after=token_s)

    def reduced(l, after):
        return _reduce_scatter_end(rs[f"{l}b"], after, tag=f"{l}b") + _reduce_scatter_end(rs[f"{l}a"], after, tag=f"{l}a")

    dproj, grads[1] = _layer_bwd(dx, saved[1], p, bigs[1], 1, tabs, early(1), mid(1))
    dx, grads[1]["norm_w"] = late(1, dproj, dx)
    dproj, grads[0] = _layer_bwd(dx, saved[0], p, bigs[0], 0, tabs, early(0), mid(0))
    dx, grads[0]["norm_w"] = late(0, dproj, dx)
    red1 = reduced(1, [dx])

    tr = lambda t: t.transpose(0, 2, 1)
    view = {k: (tr if k == "w_in" else (lambda t: t)) for k in _BIG}
    shard_grads = lambda red: dict(zip(_BIG, (red[0], red[1], red[2].T, red[3].T, red[4].T, red[5])))
    outs = {k: None for k in _BIG}

    def adamw_big(l, red):
        for k, g in shard_grads(red).items():
            outs[k] = _adamw_layer(view[k](w[k]), g, view[k](m[k]), view[k](v[k]), l, outs[k], name=f"adamw_{k}_{l}")

    adamw_big(1, red1)

    small_like = [w[k] for k in _SMALL]
    gs = [jnp.stack([grads[l][k] for l in range(DEPTH)]) if k != "final_norm_w" else dfw for k in _SMALL]
    gsum = _allreduce_small(_pack(gs + [loss.reshape(1)]), after=[outs[k][0] for k in _BIG])
    adamw_big(0, reduced(0, [gsum]))

    gfull, delta, new_m, new_v = {}, {}, {}, {}
    for k in _BIG:
        gfull[k], delta[k], new_m[k], new_v[k] = (view[k](t) for t in outs[k])
    *small_sums, loss = _unpack(gsum, small_like + [loss])
    for k, t in zip(_SMALL, small_sums):
        gfull[k] = t
        delta[k], new_m[k], new_v[k] = _adamw(w[k], t, m[k], v[k], name=f"adamw_{k}")

    return (loss, dx[None], *[gfull[k] for k in _WEIGHTS], *[delta[k] for k in _WEIGHTS],
            *[new_m[k] for k in _WEIGHTS], *[new_v[k] for k in _WEIGHTS])
```

```python
import functools
import math

import numpy as np
import jax
import jax.numpy as jnp
from jax import lax
from jax.experimental import pallas as pl
from jax.experimental.pallas import tpu as pltpu

F32 = jnp.float32
MXU = jnp.bfloat16
HIGHEST = lax.Precision.HIGHEST

D_MODEL = 2048
DEPTH = 2
EPS = 1e-6
NEG_INF = -1e30
SSM_WIDTH = 1024
SSM_GROUP = 16
SSM_GROUPS = 64
SSM_STATE = 64
SSM_CH = SSM_GROUPS * SSM_STATE
SLAB = 128
SLAB_CH = (SLAB // SSM_GROUP) * SSM_STATE
N_SLAB = SSM_WIDTH // SLAB
SCAN_SEG = 8
SCAN_STEPS = 4
SG_HEADS = 8
SG_CHUNK = 128
HEAD_DIM = 64
ATT_HEADS = 16
ATT_KV_HEADS = 2
GQA_GROUP = 8
ATT_BLOCK = 128
WINDOW = 128
ROT_DIM = 16
ROPE_THETA = 500000.0
N_MAIN = 6400
N_ZC = 1024
N_GATES = 6144
D_IN = N_MAIN + N_ZC + N_GATES

ADAM_LR = 0.001
ADAM_B1 = 0.9
ADAM_B2 = 0.999
ADAM_EPS = 1e-08
ADAM_WD = 0.01
ADAM_STEP = 10

_DIMS = {"nn": (((1,), (0,)), ((), ())), "nt": (((1,), (1,)), ((), ())), "tn": (((0,), (0,)), ((), ()))}
_MB = 1024 * 1024


def _cp(sem, vmem_mb=48):
    return pltpu.CompilerParams(dimension_semantics=sem, vmem_limit_bytes=vmem_mb * _MB)


def _dot(a, b, mode):
    return lax.dot_general(a.astype(MXU), b.astype(MXU), _DIMS[mode], preferred_element_type=F32)


@jax.custom_vjp
def _mm_nn(a, b):
    return _dot(a, b, "nn")


def _mm_nn_fwd(a, b):
    return _dot(a, b, "nn"), (a, b)


def _mm_nn_bwd(res, g):
    a, b = res
    return _dot(g, b, "nt"), _dot(a, g, "tn")


_mm_nn.defvjp(_mm_nn_fwd, _mm_nn_bwd)


@jax.custom_vjp
def _mm_nt(a, bt):
    return _dot(a, bt, "nt")


def _mm_nt_fwd(a, bt):
    return _dot(a, bt, "nt"), (a, bt)


def _mm_nt_bwd(res, g):
    a, bt = res
    return _dot(g, bt, "nn"), _dot(g, a, "tn")


_mm_nt.defvjp(_mm_nt_fwd, _mm_nt_bwd)


def _rmsnorm(x, w):
    return x * lax.rsqrt(jnp.mean(x * x, axis=-1, keepdims=True) + EPS) * w


def _layernorm(x, w, b):
    mu = jnp.mean(x, axis=-1, keepdims=True)
    var = jnp.mean(jnp.square(x - mu), axis=-1, keepdims=True)
    return (x - mu) * lax.rsqrt(var + EPS) * w + b


def _silu(x):
    return x * jax.nn.sigmoid(x)


def _matmul(a, b, mode, *, name, shape, tm, tn, tk, out_dtype=F32, add=None, a_off=(0, 0), b_off=(0, 0), after=None):
    m, n, k = shape
    tm, tn, tk = min(tm, m), min(tn, n), min(tk, k)
    assert m % tm == 0 and n % tn == 0 and k % tk == 0, (name, shape, tm, tn, tk)
    nk = k // tk
    has_add, has_after = add is not None, after is not None

    def body(*refs):
        a_ref, b_ref = refs[0], refs[1]
        pos = 2
        add_ref = None
        if has_add:
            add_ref = refs[pos]
            pos += 1
        if has_after:
            pos += 1
        o_ref = refs[pos]
        p = _dot(a_ref[...], b_ref[...], mode)
        if nk == 1:
            if has_add:
                p = p + add_ref[...].astype(F32)
            o_ref[...] = p.astype(out_dtype)
            return
        acc_ref = refs[pos + 1]
        kk = pl.program_id(2)

        @pl.when(kk == 0)
        def _():
            acc_ref[...] = p

        @pl.when(kk > 0)
        def _():
            acc_ref[...] += p

        @pl.when(kk == nk - 1)
        def _():
            r = acc_ref[...]
            if has_add:
                r = r + add_ref[...].astype(F32)
            o_ref[...] = r.astype(out_dtype)

    a0, a1 = a_off
    b0, b1 = b_off
    if mode == "tn":
        a_spec = pl.BlockSpec((tk, tm), lambda i, j, kk: (kk + a0, i + a1))
    else:
        a_spec = pl.BlockSpec((tm, tk), lambda i, j, kk: (i + a0, kk + a1))
    if mode == "nt":
        b_spec = pl.BlockSpec((tn, tk), lambda i, j, kk: (j + b0, kk + b1))
    else:
        b_spec = pl.BlockSpec((tk, tn), lambda i, j, kk: (kk + b0, j + b1))
    in_specs = [a_spec, b_spec]
    args = [a, b]
    if has_add:
        in_specs.append(pl.BlockSpec((tm, tn), lambda i, j, kk: (i, j)))
        args.append(add)
    if has_after:
        in_specs.append(pl.BlockSpec(memory_space=pl.ANY))
        args.append(after)
    return pl.pallas_call(
        body, name=name, grid=(m // tm, n // tn, nk),
        in_specs=in_specs,
        out_specs=pl.BlockSpec((tm, tn), lambda i, j, kk: (i, j)),
        out_shape=jax.ShapeDtypeStruct((m, n), out_dtype),
        scratch_shapes=[pltpu.VMEM((tm, tn), F32)] if nk > 1 else [],
        compiler_params=_cp(("parallel", "parallel", "arbitrary")),
    )(*args)


def _rms_fwd(x, w, *, name, tm=256, after=None):
    L, d = x.shape
    tm = min(tm, L)
    extra = [] if after is None else [after]

    def body(x_ref, w_ref, *rest):
        rest[-1][...] = _rmsnorm(x_ref[...], w_ref[...]).astype(MXU)

    return pl.pallas_call(
        body, name=name, grid=(L // tm,),
        in_specs=([pl.BlockSpec((tm, d), lambda i: (i, 0)), pl.BlockSpec((1, d), lambda i: (0, 0))]
                  + [pl.BlockSpec(memory_space=pl.ANY)] * len(extra)),
        out_specs=pl.BlockSpec((tm, d), lambda i: (i, 0)),
        out_shape=jax.ShapeDtypeStruct((L, d), MXU),
        compiler_params=_cp(("parallel",)),
    )(x, w.reshape(1, d), *extra)


def _rms_bwd(x, w, dh, dxn, *, name, tm=256, after=None):
    L, d = x.shape
    tm = min(tm, L)
    extra = [] if after is None else [after]

    def body(x_ref, w_ref, dh_ref, dxn_ref, *rest):
        dx_ref, dw_ref = rest[-2:]
        _, vjp = jax.vjp(_rmsnorm, x_ref[...], w_ref[...])
        dx, dw = vjp(dh_ref[...])
        dx_ref[...] = dx + dxn_ref[...]

        @pl.when(pl.program_id(0) == 0)
        def _():
            dw_ref[...] = jnp.zeros_like(dw_ref)

        dw_ref[...] += dw

    row = pl.BlockSpec((tm, d), lambda i: (i, 0))
    vec = pl.BlockSpec((1, d), lambda i: (0, 0))
    dx, dw = pl.pallas_call(
        body, name=name, grid=(L // tm,),
        in_specs=[row, vec, row, row] + [pl.BlockSpec(memory_space=pl.ANY)] * len(extra), out_specs=[row, vec],
        out_shape=[jax.ShapeDtypeStruct((L, d), F32), jax.ShapeDtypeStruct((1, d), F32)],
        compiler_params=_cp(("arbitrary",)),
    )(x, w.reshape(1, d), dh, dxn, *extra)
    return dx, dw.reshape(d)


def _final_loss(x, w, tgt, *, name, tm=256):
    L, d = x.shape
    tm = min(tm, L)

    def loss_fn(xv, wv, tv):
        err = jnp.square(_rmsnorm(xv, wv) - tv)
        return 0.5 * jnp.sum(jnp.mean(err, axis=-1, keepdims=True), axis=0, keepdims=True)

    def body(x_ref, w_ref, t_ref, loss_ref, dx_ref, dw_ref):
        tv = t_ref[...]
        val, vjp = jax.vjp(lambda xv, wv: loss_fn(xv, wv, tv), x_ref[...], w_ref[...])
        dx, dw = vjp(jnp.ones((1, 1), F32))
        dx_ref[...] = dx

        @pl.when(pl.program_id(0) == 0)
        def _():
            dw_ref[...] = jnp.zeros_like(dw_ref)
            loss_ref[...] = jnp.zeros_like(loss_ref)

        dw_ref[...] += dw
        loss_ref[...] += jnp.broadcast_to(val, loss_ref.shape)

    row = pl.BlockSpec((tm, d), lambda i: (i, 0))
    vec = pl.BlockSpec((1, d), lambda i: (0, 0))
    loss, dx, dw = pl.pallas_call(
        body, name=name, grid=(L // tm,),
        in_specs=[row, vec, row],
        out_specs=[pl.BlockSpec((8, 128), lambda i: (0, 0)), row, vec],
        out_shape=[jax.ShapeDtypeStruct((8, 128), F32), jax.ShapeDtypeStruct((L, d), F32),
                   jax.ShapeDtypeStruct((1, d), F32)],
        compiler_params=_cp(("arbitrary",)),
    )(x, w.reshape(1, d), tgt)
    return loss[0, 0], dx, dw.reshape(d)


PARAM_ROWS = 512


def _s5_param_fn(are, aim, ldt, bre, bim, row0):
    n = are.shape[0]
    grp = (row0 + lax.broadcasted_iota(jnp.int32, (n, SSM_GROUPS), 0)) // SSM_STATE
    col = lax.broadcasted_iota(jnp.int32, (n, SSM_GROUPS), 1)
    sel = (grp == col).astype(F32)
    dt = jnp.sum(sel * jnp.exp(ldt), axis=-1, keepdims=True)
    mag = jnp.exp(are * dt)
    ang = aim * dt
    lbr = mag * jnp.cos(ang)
    lbi = mag * jnp.sin(ang)
    den = are * are + aim * aim
    nr = lbr - 1.0
    kr = (nr * are + lbi * aim) / den
    ki = (lbi * are - nr * aim) / den
    return lbr, lbi, kr * bre - ki * bim, kr * bim + ki * bre


def _s5_param_specs():
    col = pl.BlockSpec((PARAM_ROWS, 1), lambda i: (i, 0))
    mat = pl.BlockSpec((PARAM_ROWS, SSM_GROUP), lambda i: (i, 0))
    vec = pl.BlockSpec((1, SSM_GROUPS), lambda i: (0, 0))
    return col, mat, vec


def _s5_params_fwd(are, aim, ldt, bre, bim, *, name, after=None):
    n = are.shape[0]
    col, mat, vec = _s5_param_specs()
    extra = [] if after is None else [after]

    def body(are_ref, aim_ref, ldt_ref, bre_ref, bim_ref, *rest):
        lbr_ref, lbi_ref, bbr_ref, bbi_ref = rest[-4:]
        row0 = pl.program_id(0) * PARAM_ROWS
        lbr, lbi, bbr, bbi = _s5_param_fn(are_ref[...], aim_ref[...], ldt_ref[...], bre_ref[...], bim_ref[...], row0)
        lbr_ref[...] = lbr
        lbi_ref[...] = lbi
        bbr_ref[...] = bbr
        bbi_ref[...] = bbi

    cshape = jax.ShapeDtypeStruct((n, 1), F32)
    mshape = jax.ShapeDtypeStruct((n, SSM_GROUP), F32)
    return pl.pallas_call(body, name=name, grid=(n // PARAM_ROWS,),
                          in_specs=[col, col, vec, mat, mat] + [pl.BlockSpec(memory_space=pl.ANY)] * len(extra),
                          out_specs=[col, col, mat, mat], out_shape=[cshape, cshape, mshape, mshape],
                          compiler_params=_cp(("parallel",)))(are, aim, ldt, bre, bim, *extra)


def _s5_params_bwd(are, aim, ldt, bre, bim, dlbr, dlbi, dbbr, dbbi, *, name):
    n = are.shape[0]
    col, mat, vec = _s5_param_specs()

    def body(are_ref, aim_ref, ldt_ref, bre_ref, bim_ref, g0, g1, g2, g3, o0, o1, o2, o3, o4):
        row0 = pl.program_id(0) * PARAM_ROWS
        _, vjp = jax.vjp(lambda a, b, c, d, e: _s5_param_fn(a, b, c, d, e, row0),
                         are_ref[...], aim_ref[...], ldt_ref[...], bre_ref[...], bim_ref[...])
        dare, daim, dldt, dbre, dbim = vjp((g0[...], g1[...], g2[...], g3[...]))
        o0[...] = dare
        o1[...] = daim
        o3[...] = dbre
        o4[...] = dbim

        @pl.when(pl.program_id(0) == 0)
        def _():
            o2[...] = jnp.zeros_like(o2)

        o2[...] += dldt

    cshape = jax.ShapeDtypeStruct((n, 1), F32)
    mshape = jax.ShapeDtypeStruct((n, SSM_GROUP), F32)
    return pl.pallas_call(body, name=name, grid=(n // PARAM_ROWS,),
                          in_specs=[col, col, vec, mat, mat, col, col, mat, mat],
                          out_specs=[col, col, vec, mat, mat],
                          out_shape=[cshape, cshape, jax.ShapeDtypeStruct((1, SSM_GROUPS), F32), mshape, mshape],
                          compiler_params=_cp(("arbitrary",)))(are, aim, ldt, bre, bim, dlbr, dlbi, dbbr, dbbi)


SLAB_NC = SLAB_CH // 128


def _s5_specs(L):
    slab = pl.BlockSpec((L, SLAB), lambda s: (0, s))
    wspec = pl.BlockSpec((SLAB_NC, 128, SLAB), lambda s: (s, 0, 0))
    lspec = pl.BlockSpec((SLAB_NC, 1, 128), lambda s: (s, 0, 0))
    sspec = pl.BlockSpec((SLAB_NC, L, 128), lambda s: (s, 0, 0))
    dspec = pl.BlockSpec((1, SLAB), lambda s: (0, s))
    return slab, wspec, lspec, sspec, dspec


def _scan_inplace(sr_ref, si_ref, lr, li, pr_ref, pi_ref, *, reverse):
    NC, L, W = sr_ref.shape
    S = SCAN_SEG
    T = L // S
    lr8 = [jnp.broadcast_to(lr[k], (S, W)) for k in range(NC)]
    li8 = [jnp.broadcast_to(li[k], (S, W)) for k in range(NC)]

    def tiles(first, count):
        return pl.ds(first * S, count * S)

    for k in range(NC):
        pr_ref[k, tiles(T - 1 if reverse else 0, 1), :] = lr8[k]
        pi_ref[k, tiles(T - 1 if reverse else 0, 1), :] = li8[k]
        n = 1
        while n < T:
            have = tiles(T - n, n) if reverse else tiles(0, n)
            new = tiles(T - 2 * n, n) if reverse else tiles(n, n)
            top = tiles(T - n, 1) if reverse else tiles(n - 1, 1)
            ar, ai = pr_ref[k, top, :][None], pi_ref[k, top, :][None]
            hr, hi = pr_ref[k, have, :].reshape(n, S, W), pi_ref[k, have, :].reshape(n, S, W)
            pr_ref[k, new, :] = (hr * ar - hi * ai).reshape(n * S, W)
            pi_ref[k, new, :] = (hr * ai + hi * ar).reshape(n * S, W)
            n *= 2

    def step(i, carry):
        for u in range(SCAN_STEPS):
            jj = i * SCAN_STEPS + u
            rows = pl.ds(pl.multiple_of(((T - 1 - jj) if reverse else jj) * S, S), S)
            out = []
            for k in range(NC):
                sr, si = carry[k]
                nsr = lr8[k] * sr - li8[k] * si + sr_ref[k, rows, :]
                nsi = lr8[k] * si + li8[k] * sr + si_ref[k, rows, :]
                sr_ref[k, rows, :] = nsr
                si_ref[k, rows, :] = nsi
                out.append((nsr, nsi))
            carry = tuple(out)
        return carry

    zero = jnp.zeros((S, W), F32)
    ends = lax.fori_loop(0, T // SCAN_STEPS, step, tuple((zero, zero) for k in range(NC)))
    sub = lax.broadcasted_iota(jnp.int32, (S, W), 0)
    order = range(S - 1, -1, -1) if reverse else range(S)
    for k in range(NC):
        er, ei = ends[k]
        full = tiles(0 if reverse else T - 1, 1)
        ltr = pr_ref[k, full, :][0:1]
        lti = pi_ref[k, full, :][0:1]
        cr = jnp.zeros((1, W), F32)
        ci = jnp.zeros((1, W), F32)
        ctr = jnp.zeros((S, W), F32)
        cti = jnp.zeros((S, W), F32)
        for seg in order:
            ctr = jnp.where(sub == seg, cr, ctr)
            cti = jnp.where(sub == seg, ci, cti)
            cr, ci = (er[seg:seg + 1, :] + ltr * cr - lti * ci, ei[seg:seg + 1, :] + ltr * ci + lti * cr)
        pr = pr_ref[k].reshape(T, S, W)
        pi = pi_ref[k].reshape(T, S, W)
        sr_ref[k] += (pr * ctr[None] - pi * cti[None]).reshape(L, W)
        si_ref[k] += (pr * cti[None] + pi * ctr[None]).reshape(L, W)


def _time_interleave(a):
    L, W = a.shape
    return a.reshape(SCAN_SEG, L // SCAN_SEG, W).transpose(1, 0, 2).reshape(L, W)


def _time_deinterleave(a):
    L, W = a.shape
    return a.reshape(L // SCAN_SEG, SCAN_SEG, W).transpose(1, 0, 2).reshape(L, W)


def _s5_fwd(u, btr, bti, cbr, cbi, lbr, lbi, dvec, *, name, after=None):
    L = u.shape[0]
    extra = [] if after is None else [after]

    def body(u_ref, btr_ref, bti_ref, cbr_ref, cbi_ref, lr_ref, li_ref, d_ref, *rest):
        ys_ref, sr_ref, si_ref, pr_ref, pi_ref = rest[-5:]
        u = u_ref[...]
        for k in range(SLAB_NC):
            sr_ref[k] = _dot(u, btr_ref[k], "nt")
            si_ref[k] = _dot(u, bti_ref[k], "nt")
        _scan_inplace(sr_ref, si_ref, lr_ref[...], li_ref[...], pr_ref, pi_ref, reverse=False)
        ys = d_ref[...] * u
        for k in range(SLAB_NC):
            ys = ys + _dot(sr_ref[k], cbr_ref[k], "nn") - _dot(si_ref[k], cbi_ref[k], "nn")
        ys_ref[...] = ys

    slab, wspec, lspec, sspec, dspec = _s5_specs(L)
    sshape = jax.ShapeDtypeStruct((N_SLAB * SLAB_NC, L, 128), F32)
    return pl.pallas_call(
        body, name=name, grid=(N_SLAB,),
        in_specs=[slab, wspec, wspec, wspec, wspec, lspec, lspec, dspec] + [pl.BlockSpec(memory_space=pl.ANY)] * len(extra),
        out_specs=[slab, sspec, sspec],
        out_shape=[jax.ShapeDtypeStruct((L, SSM_WIDTH), F32), sshape, sshape],
        scratch_shapes=[pltpu.VMEM((SLAB_NC, L, 128), F32), pltpu.VMEM((SLAB_NC, L, 128), F32)],
        compiler_params=_cp(("parallel",), 56),
    )(u, btr, bti, cbr, cbi, lbr, lbi, dvec, *extra)


def _s5_bwd(dys, u, sr, si, btr, bti, cbr, cbi, lbr, lbi, dvec, *, name, after=None):
    L = u.shape[0]
    S = SCAN_SEG
    extra = [] if after is None else [after]

    def body(dys_ref, u_ref, sr_ref, si_ref, btr_ref, bti_ref, cbr_ref, cbi_ref, lr_ref, li_ref, d_ref, *rest):
        (du_ref, dbtr_ref, dbti_ref, dcbr_ref, dcbi_ref, dlr_ref, dli_ref, dd_ref,
         ar_ref, ai_ref, pr_ref, pi_ref) = rest[-12:]
        dys = dys_ref[...]
        u = u_ref[...]
        for k in range(SLAB_NC):
            ar_ref[k] = _dot(dys, cbr_ref[k], "nt")
            ai_ref[k] = -_dot(dys, cbi_ref[k], "nt")
        _scan_inplace(ar_ref, ai_ref, lr_ref[...], -li_ref[...], pr_ref, pi_ref, reverse=True)
        head = lax.broadcasted_iota(jnp.int32, (L, 1), 0) < S
        sub0 = lax.broadcasted_iota(jnp.int32, (S, 1), 0) == 0

        def prev_state(s):
            up = pltpu.roll(s, S, 0)
            return jnp.where(head, 0.0, up), jnp.where(sub0, 0.0, pltpu.roll(up[0:S], 1, 0))

        du = d_ref[...] * dys
        for k in range(SLAB_NC):
            a_re = ar_ref[k]
            a_im = ai_ref[k]
            du = du + _dot(a_re, btr_ref[k], "nn") + _dot(a_im, bti_ref[k], "nn")
            dbtr_ref[k] = _dot(a_re, u, "tn")
            dbti_ref[k] = _dot(a_im, u, "tn")
            s_re = sr_ref[k]
            s_im = si_ref[k]
            dcbr_ref[k] = _dot(s_re, dys, "tn")
            dcbi_ref[k] = -_dot(s_im, dys, "tn")
            p_re, q_re = prev_state(s_re)
            p_im, q_im = prev_state(s_im)
            b_re, b_im = a_re[0:S], a_im[0:S]
            dlr_ref[k] = (jnp.sum(p_re * a_re + p_im * a_im, axis=0, keepdims=True)
                          + jnp.sum(q_re * b_re + q_im * b_im, axis=0, keepdims=True))
            dli_ref[k] = (jnp.sum(p_re * a_im - p_im * a_re, axis=0, keepdims=True)
                          + jnp.sum(q_re * b_im - q_im * b_re, axis=0, keepdims=True))
        du_ref[...] = du
        dd_ref[...] = jnp.sum(dys * u, axis=0, keepdims=True)

    slab, wspec, lspec, sspec, dspec = _s5_specs(L)
    wshape = jax.ShapeDtypeStruct((N_SLAB * SLAB_NC, 128, SLAB), F32)
    lshape = jax.ShapeDtypeStruct((N_SLAB * SLAB_NC, 1, 128), F32)
    return pl.pallas_call(
        body, name=name, grid=(N_SLAB,),
        in_specs=([slab, slab, sspec, sspec, wspec, wspec, wspec, wspec, lspec, lspec, dspec]
                  + [pl.BlockSpec(memory_space=pl.ANY)] * len(extra)),
        out_specs=[slab, wspec, wspec, wspec, wspec, lspec, lspec, dspec],
        out_shape=[jax.ShapeDtypeStruct((L, SSM_WIDTH), F32), wshape, wshape, wshape, wshape, lshape, lshape,
                   jax.ShapeDtypeStruct((1, SSM_WIDTH), F32)],
        scratch_shapes=[pltpu.VMEM((SLAB_NC, L, 128), F32)] * 4,
        compiler_params=_cp(("parallel",), 56),
    )(dys, u, sr, si, btr, bti, cbr, cbi, lbr, lbi, dvec, *extra)


_SLAB_MASK = (np.arange(SLAB_CH)[:, None] // SSM_STATE == np.arange(SLAB)[None, :] // SSM_GROUP)


def _expand_bd(x):
    t = jnp.tile(x.reshape(N_SLAB, SLAB_CH, SSM_GROUP), (1, 1, SLAB // SSM_GROUP))
    return jnp.where(_SLAB_MASK[None], t, 0.0).astype(MXU).reshape(N_SLAB * SLAB_NC, 128, SLAB)


def _contract_bd(dx):
    t = jnp.where(_SLAB_MASK[None], dx.reshape(N_SLAB, SLAB_CH, SLAB), 0.0)
    return jnp.sum(t.reshape(N_SLAB, SLAB_CH, SLAB // SSM_GROUP, SSM_GROUP), axis=2).reshape(SSM_CH, SSM_GROUP)


def _glu_ew(ys, zlin, za):
    a1 = jax.nn.gelu(ys)
    return a1 * jax.nn.sigmoid(zlin) * _silu(za)


def _glu_fwd(ys, main, gw, gb, *, name, tm=256):
    L = ys.shape[0]
    tm = min(tm, L)
    W = SSM_WIDTH

    def body(ys_ref, za_ref, gw_ref, gb_ref, ya_ref):
        ys = ys_ref[...]
        a1 = jax.nn.gelu(ys)
        zlin = _dot(a1, gw_ref[...], "nn") + gb_ref[...]
        ya_ref[...] = _glu_ew(ys, zlin, za_ref[...]).astype(MXU)

    return pl.pallas_call(
        body, name=name, grid=(L // tm,),
        in_specs=[pl.BlockSpec((tm, W), lambda i: (i, 0)), pl.BlockSpec((tm, W), lambda i: (i, 1)),
                  pl.BlockSpec((W, W), lambda i: (0, 0)), pl.BlockSpec((1, W), lambda i: (0, 0))],
        out_specs=pl.BlockSpec((tm, W), lambda i: (i, 0)),
        out_shape=jax.ShapeDtypeStruct((L, W), MXU),
        compiler_params=_cp(("parallel",)),
    )(ys, main, gw, gb.reshape(1, W))


def _glu_bwd(dya, ys, main, gw, gb, *, name, tm=256):
    L = ys.shape[0]
    tm = min(tm, L)
    W = SSM_WIDTH

    def body(dya_ref, ys_ref, za_ref, gw_ref, gb_ref, dys_ref, dza_ref, a1_ref, dzl_ref, db_ref):
        ys = ys_ref[...]
        a1, gelu_vjp = jax.vjp(jax.nn.gelu, ys)
        zlin = _dot(a1, gw_ref[...], "nn") + gb_ref[...]
        _, vjp = jax.vjp(lambda a, z, za: a * jax.nn.sigmoid(z) * _silu(za), a1, zlin, za_ref[...])
        da1, dzlin, dza = vjp(dya_ref[...].astype(F32))
        da1 = da1 + _dot(dzlin, gw_ref[...], "nt")
        dys_ref[...] = gelu_vjp(da1)[0]
        dza_ref[...] = dza
        a1_ref[...] = a1.astype(MXU)
        dzl_ref[...] = dzlin.astype(MXU)

        @pl.when(pl.program_id(0) == 0)
        def _():
            db_ref[...] = jnp.zeros_like(db_ref)

        db_ref[...] += jnp.sum(dzlin, axis=0, keepdims=True)

    row = pl.BlockSpec((tm, W), lambda i: (i, 0))
    vec = pl.BlockSpec((1, W), lambda i: (0, 0))
    return pl.pallas_call(
        body, name=name, grid=(L // tm,),
        in_specs=[row, row, pl.BlockSpec((tm, W), lambda i: (i, 1)), pl.BlockSpec((W, W), lambda i: (0, 0)), vec],
        out_specs=[row, row, row, row, vec],
        out_shape=[jax.ShapeDtypeStruct((L, W), F32), jax.ShapeDtypeStruct((L, W), F32),
                   jax.ShapeDtypeStruct((L, W), MXU), jax.ShapeDtypeStruct((L, W), MXU),
                   jax.ShapeDtypeStruct((1, W), F32)],
        compiler_params=_cp(("arbitrary",)),
    )(dya, ys, main, gw, gb.reshape(1, W))


def _sg_fn(ub, vb, zb, lnw, lnb, ws, bs):
    u = jax.nn.gelu(ub)
    v = _layernorm(jax.nn.gelu(vb), lnw, lnb)
    r = lax.broadcasted_iota(jnp.int32, (SG_CHUNK, SG_CHUNK), 0)
    c = lax.broadcasted_iota(jnp.int32, (SG_CHUNK, SG_CHUNK), 1)
    tri = r >= c
    outs = []
    for h in range(SG_HEADS):
        wh = jnp.where(tri, ws[h], 0.0)
        outs.append(_mm_nn(wh, v[:, h * 128:(h + 1) * 128]) + bs[h])
    mixed = jnp.concatenate(outs, axis=1)
    return u * mixed * _silu(zb)


def _sg_specs(L):
    W = SSM_WIDTH
    blk = lambda c: pl.BlockSpec((SG_CHUNK, W), lambda i, c=c: (i, c))
    vec = pl.BlockSpec((1, W), lambda i: (0, 0))
    wspec = pl.BlockSpec((SG_HEADS, SG_CHUNK, SG_CHUNK), lambda i: (0, 0, 0))
    bspec = pl.BlockSpec((SG_HEADS, SG_CHUNK, 1), lambda i: (0, 0, 0))
    return blk, vec, wspec, bspec


def _sg_fwd(main, lnw, lnb, sgw, sgb, *, name):
    L = main.shape[0]
    W = SSM_WIDTH
    blk, vec, wspec, bspec = _sg_specs(L)

    def body(ub_ref, vb_ref, zb_ref, lnw_ref, lnb_ref, w_ref, b_ref, yb_ref):
        ws = [w_ref[h] for h in range(SG_HEADS)]
        bs = [b_ref[h] for h in range(SG_HEADS)]
        yb_ref[...] = _sg_fn(ub_ref[...], vb_ref[...], zb_ref[...], lnw_ref[...], lnb_ref[...], ws, bs).astype(MXU)

    return pl.pallas_call(
        body, name=name, grid=(L // SG_CHUNK,),
        in_specs=[blk(2), blk(3), blk(4), vec, vec, wspec, bspec],
        out_specs=pl.BlockSpec((SG_CHUNK, W), lambda i: (i, 0)),
        out_shape=jax.ShapeDtypeStruct((L, W), MXU),
        compiler_params=_cp(("parallel",)),
    )(main, main, main, lnw.reshape(1, W), lnb.reshape(1, W), sgw, sgb.reshape(SG_HEADS, SG_CHUNK, 1))


def _sg_bwd(dyb, main, lnw, lnb, sgw, sgb, *, name):
    L = main.shape[0]
    W = SSM_WIDTH
    blk, vec, wspec, bspec = _sg_specs(L)

    def body(dyb_ref, ub_ref, vb_ref, zb_ref, lnw_ref, lnb_ref, w_ref, b_ref,
             dub_ref, dvb_ref, dzb_ref, dlnw_ref, dlnb_ref, dw_ref, db_ref):
        ws = [w_ref[h] for h in range(SG_HEADS)]
        bs = [b_ref[h] for h in range(SG_HEADS)]
        _, vjp = jax.vjp(_sg_fn, ub_ref[...], vb_ref[...], zb_ref[...], lnw_ref[...], lnb_ref[...], ws, bs)
        dub, dvb, dzb, dlnw, dlnb, dws, dbs = vjp(dyb_ref[...])

        @pl.when(pl.program_id(0) == 0)
        def _():
            dlnw_ref[...] = jnp.zeros_like(dlnw_ref)
            dlnb_ref[...] = jnp.zeros_like(dlnb_ref)
            dw_ref[...] = jnp.zeros_like(dw_ref)
            db_ref[...] = jnp.zeros_like(db_ref)

        dub_ref[...] = dub
        dvb_ref[...] = dvb
        dzb_ref[...] = dzb
        dlnw_ref[...] += dlnw
        dlnb_ref[...] += dlnb
        for h in range(SG_HEADS):
            dw_ref[h] += dws[h]
            db_ref[h] += dbs[h]

    row = pl.BlockSpec((SG_CHUNK, W), lambda i: (i, 0))
    out = jax.ShapeDtypeStruct((L, W), F32)
    return pl.pallas_call(
        body, name=name, grid=(L // SG_CHUNK,),
        in_specs=[row, blk(2), blk(3), blk(4), vec, vec, wspec, bspec],
        out_specs=[row, row, row, vec, vec, wspec, bspec],
        out_shape=[out, out, out, jax.ShapeDtypeStruct((1, W), F32), jax.ShapeDtypeStruct((1, W), F32),
                   jax.ShapeDtypeStruct((SG_HEADS, SG_CHUNK, SG_CHUNK), F32),
                   jax.ShapeDtypeStruct((SG_HEADS, SG_CHUNK, 1), F32)],
        compiler_params=_cp(("arbitrary",)),
    )(dyb, main, main, main, lnw.reshape(1, W), lnb.reshape(1, W), sgw, sgb.reshape(SG_HEADS, SG_CHUNK, 1))


def _rope_tables(L):
    half = ROT_DIM // 2
    inv_freq = ROPE_THETA ** (-jnp.arange(0, ROT_DIM, 2, dtype=F32) / ROT_DIM)
    ang = jnp.arange(L, dtype=F32)[:, None] * inv_freq[None, :]
    cos = jnp.cos(ang)
    sin = jnp.sin(ang)
    ones = jnp.ones((L, HEAD_DIM - ROT_DIM), F32)
    cosf = jnp.concatenate([cos, cos, ones], axis=1)
    sinf = jnp.concatenate([sin, sin, 0.0 * ones], axis=1)
    rot = np.zeros((HEAD_DIM, HEAD_DIM), np.float32)
    for d in range(half):
        rot[d + half, d] = -1.0
        rot[d, d + half] = 1.0
    return cosf, sinf, jnp.asarray(rot)


def _rope(t, cosf, sinf, rot):
    shp = t.shape
    t2 = t.reshape(-1, HEAD_DIM)
    sw = lax.dot_general(t2, rot, _DIMS["nn"], precision=lax.Precision.HIGH, preferred_element_type=F32).reshape(shp)
    return t * cosf + sw * sinf


def _attn_core_parts(s, va, sink):
    h, q, k = s.shape
    m = jnp.maximum(jnp.max(s, axis=-1, keepdims=True), sink)
    e = jnp.exp(s - m)
    es = jnp.exp(sink - m)
    ev = _dot(e.reshape(h * q, k), va, "nn")
    r = 1.0 / (ev[:, HEAD_DIM:HEAD_DIM + 1].reshape(h, q, 1) + es)
    return ev[:, :HEAD_DIM] * r.reshape(h * q, 1), e, r, es


@jax.custom_vjp
def _attn_core(s, v, va, sink):
    return _attn_core_parts(s, va, sink)[0]


def _attn_core_fwd(s, v, va, sink):
    o, e, r, es = _attn_core_parts(s, va, sink)
    return o, (o, e, r, es, v, va)


def _attn_core_bwd(res, do):
    o, e, r, es, v, va = res
    h, q, k = e.shape
    p = e * r
    t = jnp.sum(o * do, axis=-1, keepdims=True).reshape(h, q, 1)
    dp = _dot(do, v, "nt").reshape(h, q, k)
    dv = _dot(p.reshape(h * q, k), do, "tn")
    dsink = -jnp.sum(es * r * t, axis=1, keepdims=True)
    return p * (dp - t), dv, jnp.zeros_like(va), dsink


_attn_core.defvjp(_attn_core_fwd, _attn_core_bwd)


def _attn_block_fn(q, kw, vw, sinks, vaw, cq, sq, ck, sk, rot, q0, k0):
    nk = kw.shape[1]
    qr = _rope(q, cq, sq, rot)
    kr = _rope(kw, ck, sk, rot)
    qpos = q0 + lax.broadcasted_iota(jnp.int32, (1, ATT_BLOCK, nk), 1)
    kpos = k0 + lax.broadcasted_iota(jnp.int32, (1, ATT_BLOCK, nk), 2)
    diff = qpos - kpos
    allowed = (diff >= 0) & (diff < WINDOW)
    outs = []
    for kh in range(ATT_KV_HEADS):
        qh = qr[kh * GQA_GROUP:(kh + 1) * GQA_GROUP].reshape(GQA_GROUP * ATT_BLOCK, HEAD_DIM)
        s = _mm_nt(qh, kr[kh]).reshape(GQA_GROUP, ATT_BLOCK, nk) * (HEAD_DIM ** -0.5)
        s = jnp.where(allowed, s, NEG_INF)
        o = _attn_core(s, vw[kh], vaw[kh], sinks[kh * GQA_GROUP:(kh + 1) * GQA_GROUP])
        outs.append(o.reshape(GQA_GROUP, ATT_BLOCK, HEAD_DIM))
    return jnp.concatenate(outs, axis=0)


def _attn_common(L):
    nwin = min(2 * ATT_BLOCK, L)
    qspec = pl.BlockSpec((ATT_HEADS, ATT_BLOCK, HEAD_DIM), lambda n: (0, n, 0))
    kvspec = pl.BlockSpec((ATT_KV_HEADS, L, HEAD_DIM), lambda n: (0, 0, 0))
    sspec = pl.BlockSpec((ATT_HEADS, 1, 1), lambda n: (0, 0, 0))
    tq = pl.BlockSpec((ATT_BLOCK, HEAD_DIM), lambda n: (n, 0))
    tk = pl.BlockSpec((L, HEAD_DIM), lambda n: (0, 0))
    rspec = pl.BlockSpec((HEAD_DIM, HEAD_DIM), lambda n: (0, 0))
    vaspec = pl.BlockSpec((ATT_KV_HEADS, L, 2 * HEAD_DIM), lambda n: (0, 0, 0))
    return nwin, qspec, kvspec, sspec, tq, tk, rspec, vaspec


def _v_with_ones(vh):
    return jnp.concatenate([vh, jnp.ones_like(vh)], axis=-1).astype(MXU)


def _attn_fwd(qh, kh, vh, sinks, cosf, sinf, rot, *, name):
    L = qh.shape[1]
    nwin, qspec, kvspec, sspec, tq, tk, rspec, vaspec = _attn_common(L)

    def body(q_ref, k_ref, v_ref, s_ref, va_ref, cq_ref, sq_ref, ck_ref, sk_ref, r_ref, o_ref):
        n = pl.program_id(0)
        k0 = pl.multiple_of(jnp.maximum(n - 1, 0) * ATT_BLOCK, ATT_BLOCK)
        win = pl.ds(k0, nwin)
        o_ref[...] = _attn_block_fn(q_ref[...], k_ref[:, win, :], v_ref[:, win, :], s_ref[...], va_ref[:, win, :],
                                    cq_ref[...], sq_ref[...], ck_ref[win, :], sk_ref[win, :], r_ref[...],
                                    n * ATT_BLOCK, k0)

    return pl.pallas_call(
        body, name=name, grid=(L // ATT_BLOCK,),
        in_specs=[qspec, kvspec, kvspec, sspec, vaspec, tq, tq, tk, tk, rspec],
        out_specs=qspec,
        out_shape=jax.ShapeDtypeStruct((ATT_HEADS, L, HEAD_DIM), F32),
        compiler_params=_cp(("parallel",)),
    )(qh, kh, vh, sinks.reshape(ATT_HEADS, 1, 1), _v_with_ones(vh), cosf, sinf, cosf, sinf, rot)


def _attn_bwd(do, qh, kh, vh, sinks, cosf, sinf, rot, *, name):
    L = qh.shape[1]
    nwin, qspec, kvspec, sspec, tq, tk, rspec, vaspec = _attn_common(L)

    def body(do_ref, q_ref, k_ref, v_ref, s_ref, va_ref, cq_ref, sq_ref, ck_ref, sk_ref, r_ref,
             dq_ref, dk_ref, dv_ref, ds_ref):
        n = pl.program_id(0)
        k0 = pl.multiple_of(jnp.maximum(n - 1, 0) * ATT_BLOCK, ATT_BLOCK)
        win = pl.ds(k0, nwin)
        cq, sq, ck, sk, rt = cq_ref[...], sq_ref[...], ck_ref[win, :], sk_ref[win, :], r_ref[...]
        vaw = va_ref[:, win, :]
        q0 = n * ATT_BLOCK
        _, vjp = jax.vjp(lambda q, kw, vw, s: _attn_block_fn(q, kw, vw, s, vaw, cq, sq, ck, sk, rt, q0, k0),
                         q_ref[...], k_ref[:, win, :], v_ref[:, win, :], s_ref[...])
        dq, dkw, dvw, ds = vjp(do_ref[...])

        @pl.when(n == 0)
        def _():
            dk_ref[...] = jnp.zeros_like(dk_ref)
            dv_ref[...] = jnp.zeros_like(dv_ref)
            ds_ref[...] = jnp.zeros_like(ds_ref)

        dq_ref[...] = dq
        dk_ref[:, win, :] += dkw
        dv_ref[:, win, :] += dvw
        ds_ref[...] += ds

    return pl.pallas_call(
        body, name=name, grid=(L // ATT_BLOCK,),
        in_specs=[qspec, qspec, kvspec, kvspec, sspec, vaspec, tq, tq, tk, tk, rspec],
        out_specs=[qspec, kvspec, kvspec, sspec],
        out_shape=[jax.ShapeDtypeStruct((ATT_HEADS, L, HEAD_DIM), F32),
                   jax.ShapeDtypeStruct((ATT_KV_HEADS, L, HEAD_DIM), F32),
                   jax.ShapeDtypeStruct((ATT_KV_HEADS, L, HEAD_DIM), F32),
                   jax.ShapeDtypeStruct((ATT_HEADS, 1, 1), F32)],
        compiler_params=_cp(("arbitrary",)),
    )(do, qh, kh, vh, sinks.reshape(ATT_HEADS, 1, 1), _v_with_ones(vh), cosf, sinf, cosf, sinf, rot)


def _to_heads(t, nh):
    L = t.shape[0]
    return t.reshape(L, nh, HEAD_DIM).transpose(1, 0, 2)


def _from_heads(t):
    nh, L, _ = t.shape
    return t.transpose(1, 0, 2).reshape(L, nh * HEAD_DIM)


def _branch_fwd(ya, yb, o2d, zc, gates, wa, wb, wc, *, name, tm=256):
    L = ya.shape[0]
    tm = min(tm, L)
    W, D = SSM_WIDTH, D_MODEL

    def body(ya_ref, yb_ref, o_ref, zc_ref, g0_ref, g1_ref, g2_ref, wa_ref, wb_ref, wc_ref,
             mg_ref, ta_ref, tb_ref, tc_ref, yc_ref):
        yc = (o_ref[...] * _silu(zc_ref[...])).astype(MXU)
        ta = _dot(ya_ref[...], wa_ref[...], "nt")
        tb = _dot(yb_ref[...], wb_ref[...], "nt")
        tc = _dot(yc, wc_ref[...], "nt")
        ta_ref[...] = ta
        tb_ref[...] = tb
        tc_ref[...] = tc
        yc_ref[...] = yc
        mg_ref[...] = (jax.nn.sigmoid(g0_ref[...]) * ta + jax.nn.sigmoid(g1_ref[...]) * tb
                       + jax.nn.sigmoid(g2_ref[...]) * tc).astype(MXU)

    row = pl.BlockSpec((tm, W), lambda i: (i, 0))
    wide = pl.BlockSpec((tm, D), lambda i: (i, 0))
    gate = lambda c: pl.BlockSpec((tm, D), lambda i, c=c: (i, c))
    wspec = pl.BlockSpec((D, W), lambda i: (0, 0))
    return pl.pallas_call(
        body, name=name, grid=(L // tm,),
        in_specs=[row, row, row, row, gate(0), gate(1), gate(2), wspec, wspec, wspec],
        out_specs=[wide, wide, wide, wide, row],
        out_shape=[jax.ShapeDtypeStruct((L, D), MXU), jax.ShapeDtypeStruct((L, D), F32),
                   jax.ShapeDtypeStruct((L, D), F32), jax.ShapeDtypeStruct((L, D), F32),
                   jax.ShapeDtypeStruct((L, W), MXU)],
        compiler_params=_cp(("parallel",), 56),
    )(ya, yb, o2d, zc, gates, gates, gates, wa, wb, wc)


def _branch_bwd(dmg, ta, tb, tc, gates, *, name, tm=256):
    L = dmg.shape[0]
    tm = min(tm, L)
    D = D_MODEL

    def body(dm_ref, ta_ref, tb_ref, tc_ref, g0_ref, g1_ref, g2_ref, da_ref, db_ref, dc_ref, dg_ref):
        dm = dm_ref[...]
        for i, (t_ref, g_ref, d_ref) in enumerate(((ta_ref, g0_ref, da_ref), (tb_ref, g1_ref, db_ref),
                                                   (tc_ref, g2_ref, dc_ref))):
            sg = jax.nn.sigmoid(g_ref[...])
            d_ref[...] = (sg * dm).astype(MXU)
            dg_ref[:, i * D:(i + 1) * D] = (dm * t_ref[...] * sg * (1.0 - sg)).astype(MXU)

    wide = pl.BlockSpec((tm, D), lambda i: (i, 0))
    gate = lambda c: pl.BlockSpec((tm, D), lambda i, c=c: (i, c))
    bf = jax.ShapeDtypeStruct((L, D), MXU)
    return pl.pallas_call(
        body, name=name, grid=(L // tm,),
        in_specs=[wide, wide, wide, wide, gate(0), gate(1), gate(2)],
        out_specs=[wide, wide, wide, pl.BlockSpec((tm, 3 * D), lambda i: (i, 0))],
        out_shape=[bf, bf, bf, jax.ShapeDtypeStruct((L, 3 * D), MXU)],
        compiler_params=_cp(("parallel",), 56),
    )(dmg, ta, tb, tc, gates, gates, gates)


def _gate_c_bwd(dyc, o2d, zc, *, name, tm=256):
    L, W = dyc.shape
    tm = min(tm, L)

    def body(dy_ref, o_ref, z_ref, do_ref, dz_ref):
        _, vjp = jax.vjp(lambda o, z: o * _silu(z), o_ref[...], z_ref[...])
        do, dz = vjp(dy_ref[...])
        do_ref[...] = do
        dz_ref[...] = dz.astype(MXU)

    row = pl.BlockSpec((tm, W), lambda i: (i, 0))
    return pl.pallas_call(body, name=name, grid=(L // tm,), in_specs=[row, row, row], out_specs=[row, row],
                          out_shape=[jax.ShapeDtypeStruct((L, W), F32), jax.ShapeDtypeStruct((L, W), MXU)],
                          compiler_params=_cp(("parallel",)))(dyc, o2d, zc)


def _adamw(w, g, m, v, *, name):
    shape = w.shape
    cols = shape[-1]
    w2, g2, m2, v2 = (t.reshape(-1, cols) for t in (w, g, m, v))
    rows = w2.shape[0]
    tc = 1024 if cols % 1024 == 0 else cols
    lane_cols = -(-tc // 128) * 128
    tr = rows
    while tr % 16 == 0 and tr * lane_cols * 4 > 2 * _MB:
        tr //= 2

    def body(w_ref, g_ref, m_ref, v_ref, d_ref, nm_ref, nv_ref):
        gv = g_ref[...]
        nm = ADAM_B1 * m_ref[...] + (1.0 - ADAM_B1) * gv
        nv = ADAM_B2 * v_ref[...] + (1.0 - ADAM_B2) * jnp.square(gv)
        m_hat = nm / (1.0 - ADAM_B1 ** ADAM_STEP)
        v_hat = nv / (1.0 - ADAM_B2 ** ADAM_STEP)
        d_ref[...] = -ADAM_LR * (m_hat / (jnp.sqrt(v_hat) + ADAM_EPS) + ADAM_WD * w_ref[...])
        nm_ref[...] = nm
        nv_ref[...] = nv

    spec = pl.BlockSpec((tr, tc), lambda i, j: (i, j))
    out = jax.ShapeDtypeStruct((rows, cols), F32)
    d, nm, nv = pl.pallas_call(body, name=name, grid=(rows // tr, cols // tc), in_specs=[spec] * 4,
                               out_specs=[spec] * 3, out_shape=[out, out, out],
                               compiler_params=_cp(("parallel", "parallel")))(w2, g2, m2, v2)
    return d.reshape(shape), nm.reshape(shape), nv.reshape(shape)


def _adamw_layer(w, g, m, v, l, prev, *, name):
    _, rows, cols = w.shape
    tc = 1024 if cols % 1024 == 0 else cols
    tr = rows
    while tr % 16 == 0 and tr * tc * 4 > 2 * _MB:
        tr //= 2

    def body(w_ref, g_ref, m_ref, v_ref, *rest):
        go_ref, d_ref, nm_ref, nv_ref = rest[-4:]
        gv = g_ref[...]
        nm = ADAM_B1 * m_ref[...] + (1.0 - ADAM_B1) * gv
        nv = ADAM_B2 * v_ref[...] + (1.0 - ADAM_B2) * jnp.square(gv)
        m_hat = nm / (1.0 - ADAM_B1 ** ADAM_STEP)
        v_hat = nv / (1.0 - ADAM_B2 ** ADAM_STEP)
        d_ref[...] = -ADAM_LR * (m_hat / (jnp.sqrt(v_hat) + ADAM_EPS) + ADAM_WD * w_ref[...])
        nm_ref[...] = nm
        nv_ref[...] = nv
        go_ref[...] = gv

    lspec = pl.BlockSpec((None, tr, tc), lambda i, j: (l, i, j))
    gspec = pl.BlockSpec((tr, tc), lambda i, j: (i, j))
    out = jax.ShapeDtypeStruct(w.shape, F32)
    extra = [] if prev is None else list(prev)
    return pl.pallas_call(
        body, name=name, grid=(rows // tr, cols // tc),
        in_specs=[lspec, gspec, lspec, lspec] + [_ANY] * len(extra),
        out_specs=[lspec] * 4, out_shape=[out] * 4,
        input_output_aliases={4 + i: i for i in range(len(extra))},
        compiler_params=_cp(("parallel", "parallel")),
    )(w, g, m, v, *extra)


def _prep_layer(p, l, after=None):
    are = p["ssm_a_re"][l].reshape(SSM_CH, 1)
    aim = p["ssm_a_im"][l].reshape(SSM_CH, 1)
    ldt = p["ssm_log_dt"][l].reshape(1, SSM_GROUPS)
    bre = p["ssm_b_re"][l].reshape(SSM_CH, SSM_GROUP)
    bim = p["ssm_b_im"][l].reshape(SSM_CH, SSM_GROUP)
    lbr, lbi, bbr, bbi = _s5_params_fwd(are, aim, ldt, bre, bim, name=f"s5_params_fwd_{l}", after=after)
    cre = p["ssm_c_re"][l].transpose(0, 2, 1).reshape(SSM_CH, SSM_GROUP)
    cim = p["ssm_c_im"][l].transpose(0, 2, 1).reshape(SSM_CH, SSM_GROUP)
    return dict(raw=(are, aim, ldt, bre, bim),
                lbr=lbr.reshape(N_SLAB * SLAB_NC, 1, 128), lbi=lbi.reshape(N_SLAB * SLAB_NC, 1, 128),
                btr=_expand_bd(bbr), bti=_expand_bd(bbi), cbr=_expand_bd(cre), cbi=_expand_bd(cim),
                dvec=p["ssm_d"][l].reshape(1, SSM_WIDTH))


def _layer_fwd(x, h, p, sp, winT, rest_of, l, tabs, proj_after=None, after_main=None):
    L = x.shape[0]
    cosf, sinf, rot = tabs
    mm = functools.partial(_matmul, h, winT, "nt", tm=L, tn=256, tk=D_MODEL)
    main = mm(name=f"proj_main_{l}", shape=(L, N_MAIN, D_MODEL), after=proj_after)
    then = proj_after if after_main is None else after_main(main)
    zc = mm(name=f"proj_zc_{l}", shape=(L, N_ZC, D_MODEL), b_off=(N_MAIN // 256, 0), after=then)
    gates = mm(name=f"proj_gates_{l}", shape=(L, N_GATES, D_MODEL), b_off=((N_MAIN + N_ZC) // 256, 0), after=then)
    big, token = rest_of([main, zc, gates])
    ua = _time_interleave(main[:, :SSM_WIDTH])
    ys, sr, si = _s5_fwd(ua, sp["btr"], sp["bti"], sp["cbr"], sp["cbi"], sp["lbr"], sp["lbi"], sp["dvec"],
                         name=f"s5_fwd_{l}", after=token)
    ys = _time_deinterleave(ys)
    ya = _glu_fwd(ys, main, big["glu_w"], p["ssm_glu_b"][l], name=f"glu_fwd_{l}")
    yb = _sg_fwd(main, p["sg_ln_w"][l], p["sg_ln_b"][l], p["sg_w"][l], p["sg_b"][l], name=f"sg_fwd_{l}")
    qh = _to_heads(main[:, 5120:6144], ATT_HEADS)
    kh = _to_heads(main[:, 6144:6272], ATT_KV_HEADS)
    vh = _to_heads(main[:, 6272:6400], ATT_KV_HEADS)
    oh = _attn_fwd(qh, kh, vh, p["attn_sinks"][l], cosf, sinf, rot, name=f"attn_fwd_{l}")
    o2d = _from_heads(oh)
    mg, ta, tb, tc, yc = _branch_fwd(ya, yb, o2d, zc, gates, big["wbaT"], big["wbbT"], big["wbcT"],
                                     name=f"branch_fwd_{l}")
    xn = _matmul(mg, big["w_out"], "nn", name=f"out_fwd_{l}", shape=(L, D_MODEL, D_MODEL), tm=512, tn=512,
                 tk=D_MODEL, add=x)
    saved = dict(x=x, h=h, main=main, zc=zc, gates=gates, ua=ua, ys=ys, sr=sr, si=si, ya=ya, yb=yb, yc=yc, o2d=o2d,
                 qh=qh, kh=kh, vh=vh, mg=mg, ta=ta, tb=tb, tc=tc, sp=sp)
    return xn, saved, big


def _layer_bwd(dxn, s, p, big, l, tabs, early, mid):
    L = dxn.shape[0]
    D, W = D_MODEL, SSM_WIDTH
    cosf, sinf, rot = tabs
    sp = s["sp"]
    g = {}
    dmg = _matmul(dxn, big["w_out"], "nt", name=f"out_bwd_dm_{l}", shape=(L, D, D), tm=512, tn=512, tk=D)
    g["w_out"] = _matmul(s["mg"], dxn, "tn", name=f"out_bwd_dw_{l}", shape=(D, D, L), tm=512, tn=512, tk=L,
                         out_dtype=MXU)
    dta, dtb, dtc, dgates = _branch_bwd(dmg, s["ta"], s["tb"], s["tc"], s["gates"], name=f"branch_bwd_{l}")
    dys_ = {}
    for nm, dt, y, wt in (("a", dta, s["ya"], big["wbaT"]), ("b", dtb, s["yb"], big["wbbT"]),
                          ("c", dtc, s["yc"], big["wbcT"])):
        dys_[nm] = _matmul(dt, wt, "nn", name=f"branch_bwd_dy{nm}_{l}", shape=(L, W, D), tm=512, tn=512, tk=D)
        g["wb" + nm + "T"] = _matmul(dt, y, "tn", name=f"branch_bwd_dw{nm}_{l}", shape=(D, W, L),
                                     tm=512, tn=512, tk=L, out_dtype=MXU)
    do2d, dzc = _gate_c_bwd(dys_["c"], s["o2d"], s["zc"], name=f"gate_c_bwd_{l}")
    dqh, dkh, dvh, dsinks = _attn_bwd(_to_heads(do2d, ATT_HEADS), s["qh"], s["kh"], s["vh"], p["attn_sinks"][l],
                                      cosf, sinf, rot, name=f"attn_bwd_{l}")
    g["attn_sinks"] = dsinks.reshape(ATT_HEADS)
    dub, dvb, dzb, dlnw, dlnb, dsgw, dsgb = _sg_bwd(dys_["b"], s["main"], p["sg_ln_w"][l], p["sg_ln_b"][l],
                                                    p["sg_w"][l], p["sg_b"][l], name=f"sg_bwd_{l}")
    g["sg_ln_w"], g["sg_ln_b"] = dlnw.reshape(W), dlnb.reshape(W)
    g["sg_w"], g["sg_b"] = dsgw, dsgb.reshape(SG_HEADS, SG_CHUNK)
    dys, dza, a1, dzl, dgb = _glu_bwd(dys_["a"], s["ys"], s["main"], big["glu_w"], p["ssm_glu_b"][l],
                                      name=f"glu_bwd_{l}")
    g["ssm_glu_b"] = dgb.reshape(W)
    g["glu_w"] = _matmul(a1, dzl, "tn", name=f"glu_bwd_dw_{l}", shape=(W, W, L), tm=512, tn=512, tk=L, out_dtype=MXU)
    token = early(g)
    dua, dbtr, dbti, dcbr, dcbi, dlr, dli, dd = _s5_bwd(_time_interleave(dys), s["ua"], s["sr"], s["si"], sp["btr"],
                                                        sp["bti"], sp["cbr"], sp["cbi"], sp["lbr"], sp["lbi"],
                                                        sp["dvec"], name=f"s5_bwd_{l}", after=token)
    token = mid(dua)
    dua = _time_deinterleave(dua)
    g["ssm_d"] = dd.reshape(W)
    to_c = lambda t: _contract_bd(t).reshape(SSM_GROUPS, SSM_STATE, SSM_GROUP).transpose(0, 2, 1)
    g["ssm_c_re"], g["ssm_c_im"] = to_c(dcbr), to_c(dcbi)
    dare, daim, dldt, dbre, dbim = _s5_params_bwd(*sp["raw"], dlr.reshape(SSM_CH, 1), dli.reshape(SSM_CH, 1),
                                                  _contract_bd(dbtr), _contract_bd(dbti),
                                                  name=f"s5_params_bwd_{l}")
    g["ssm_a_re"] = dare.reshape(SSM_GROUPS, SSM_STATE)
    g["ssm_a_im"] = daim.reshape(SSM_GROUPS, SSM_STATE)
    g["ssm_log_dt"] = dldt.reshape(SSM_GROUPS)
    g["ssm_b_re"] = dbre.reshape(SSM_GROUPS, SSM_STATE, SSM_GROUP)
    g["ssm_b_im"] = dbim.reshape(SSM_GROUPS, SSM_STATE, SSM_GROUP)
    dproj = jnp.concatenate([t.astype(MXU) for t in (dua, dza, dub, dvb, dzb, _from_heads(dqh), _from_heads(dkh),
                                                     _from_heads(dvh), dzc, dgates)], axis=1)
    g["winT"] = _matmul(dproj, s["h"], "tn", name=f"proj_bwd_dw_{l}", shape=(D_IN, D, L), tm=256, tn=D, tk=L,
                        out_dtype=MXU, after=token)
    return dproj, g


def _proj_bwd_dh(dproj, winT, l, after):
    return _matmul(dproj, winT, "nn", name=f"proj_bwd_dh_{l}", shape=(dproj.shape[0], D_MODEL, D_IN), tm=512, tn=512,
                   tk=D_IN // 2, after=after)


MESH = pl.DeviceIdType.MESH
_ANY = pl.BlockSpec(memory_space=pl.ANY)
ROW_ALIGN = 16


def _coords():
    return lax.axis_index("x"), lax.axis_index("y"), lax.axis_index("c")


def _gather8(arrs, *, name):
    n = len(arrs)
    rows = [a.shape[0] for a in arrs]
    for r in rows:
        assert r % ROW_ALIGN == 0

    def body(*refs):
        ins, outs = refs[:n], refs[n:2 * n]
        send, recv, lsem = refs[2 * n:]
        x, y, c = _coords()
        me, sibling = (x, y, c), (x, y, 1 - c)
        chips = [(1 - x, y), (x, 1 - y), (1 - x, 1 - y)]

        def blk(a, px, py, pc):
            return outs[a].at[pl.ds(pl.multiple_of((4 * px + 2 * py + pc) * rows[a], ROW_ALIGN), rows[a]), :]

        def own(a):
            return ins[a]

        def copy(a, k, block, to, src=None):
            return pltpu.make_async_remote_copy(
                src_ref=blk(a, *block) if src is None else src, dst_ref=blk(a, *block),
                send_sem=send.at[a, k], recv_sem=recv.at[a, k], device_id=to, device_id_type=MESH)

        mine, first, passed = [], [], []
        for a in range(n):
            mine.append(pltpu.make_async_copy(own(a), blk(a, *me), lsem.at[a]))
            mine[a].start()
            f = [copy(a, 0, me, sibling, src=own(a))]
            f += [copy(a, 1 + j, me, (*chip, c), src=own(a)) for j, chip in enumerate(chips)]
            for cp in f:
                cp.start()
            first.append(f)
        for a in range(n):
            ps = [copy(a, 4 + j, (*chip, c), sibling) for j, chip in enumerate(chips)]
            for j, chip in enumerate(chips):
                copy(a, 1 + j, (*chip, c), me).wait_recv()
                ps[j].start()
            passed.append(ps)
        for a in range(n):
            copy(a, 0, sibling, me).wait_recv()
            for j, chip in enumerate(chips):
                copy(a, 4 + j, (*chip, 1 - c), me).wait_recv()
            for cp in first[a] + passed[a]:
                cp.wait_send()
            mine[a].wait()

    return pl.pallas_call(
        body, name=name,
        in_specs=[_ANY] * n, out_specs=[_ANY] * n,
        out_shape=[jax.ShapeDtypeStruct((8 * r,) + a.shape[1:], a.dtype) for r, a in zip(rows, arrs)],
        scratch_shapes=[pltpu.SemaphoreType.DMA((n, 7)), pltpu.SemaphoreType.DMA((n, 7)), pltpu.SemaphoreType.DMA((n,))],
    )(*arrs)


def _sibling_swap(arrs, *, name):
    n = len(arrs)

    def body(*refs):
        ins, outs = refs[:n], refs[n:2 * n]
        send, recv = refs[2 * n:]
        x, y, c = _coords()
        cps = [pltpu.make_async_remote_copy(src_ref=ins[a].at[:, 1 - c], dst_ref=outs[a], send_sem=send.at[a],
                                            recv_sem=recv.at[a], device_id=(x, y, 1 - c), device_id_type=MESH)
               for a in range(n)]
        for cp in cps:
            cp.start()
        for cp in cps:
            cp.wait_recv()
        for cp in cps:
            cp.wait_send()

    return pl.pallas_call(
        body, name=name, in_specs=[_ANY] * n, out_specs=[_ANY] * n,
        out_shape=[jax.ShapeDtypeStruct((a.shape[0],) + a.shape[2:], a.dtype) for a in arrs],
        scratch_shapes=[pltpu.SemaphoreType.DMA((n,)), pltpu.SemaphoreType.DMA((n,))],
    )(*arrs)


def _col_tile(lead, rows, cols, itemsize=4, cap=4 * _MB):
    tc = cols
    while tc % 256 == 0 and lead * rows * tc * itemsize > cap:
        tc //= 2
    return tc


def _pair_sum(mine, theirs, *, name):
    _, _, rows, cols = mine.shape
    tc = _col_tile(1, rows, cols)
    c = lax.axis_index("c")

    def body(c_ref, a_ref, b_ref, o_ref):
        o_ref[...] = (a_ref[...].astype(F32) + b_ref[...].astype(F32)).astype(MXU)

    return pl.pallas_call(
        body, name=name,
        grid_spec=pltpu.PrefetchScalarGridSpec(
            num_scalar_prefetch=1, grid=(4, cols // tc),
            in_specs=[pl.BlockSpec((None, None, rows, tc), lambda j, i, cr: (j, cr[0], 0, i)),
                      pl.BlockSpec((None, rows, tc), lambda j, i, cr: (j, 0, i))],
            out_specs=pl.BlockSpec((None, rows, tc), lambda j, i, cr: (j, 0, i))),
        out_shape=jax.ShapeDtypeStruct((4, rows, cols), MXU),
        compiler_params=_cp(("parallel", "parallel")),
    )(c.reshape(1).astype(jnp.int32), mine, theirs)


_HBM = pl.BlockSpec(memory_space=pltpu.HBM)
_SEM = pl.BlockSpec(memory_space=pltpu.SEMAPHORE)
_EFFECT = pltpu.SideEffectType.DATAFLOW_SIDE_EFFECTING
N_PEER_CHIPS = 3


def _peer_chips(x, y):
    return [(1 - x, y), (x, 1 - y), (1 - x, 1 - y)]


def _split_start(srcs, lands, src_slot, dst_slot, *, name, after=()):
    n = len(srcs)
    ns = n * N_PEER_CHIPS
    first = 2 * n + len(after)

    def body(*refs):
        src_refs, land_refs = refs[:n], refs[n:2 * n]
        send, recv, token = refs[first:first + ns], refs[first + ns:first + 2 * ns], refs[-1]
        x, y, c = _coords()
        for a in range(n):
            for k, (px, py) in enumerate(_peer_chips(x, y)):
                pltpu.make_async_remote_copy(
                    src_ref=src_refs[a].at[src_slot(x, y, c, px, py)], dst_ref=land_refs[a].at[dst_slot(x, y, c)],
                    send_sem=send[a * N_PEER_CHIPS + k], recv_sem=recv[a * N_PEER_CHIPS + k],
                    device_id=(px, py, c), device_id_type=MESH).start()
        token[...] = jnp.zeros_like(token)

    bufs = list(srcs) + list(lands)
    res = pl.pallas_call(
        body, name=name,
        out_shape=(*[pltpu.SemaphoreType.DMA(())] * (2 * ns), *[pltpu.HBM(b.shape, b.dtype) for b in bufs],
                   jax.ShapeDtypeStruct((8, 128), F32)),
        in_specs=[_HBM] * (2 * n) + [_ANY] * len(after),
        out_specs=(*[_SEM] * (2 * ns), *[_HBM] * (2 * n), pl.BlockSpec(memory_space=pltpu.VMEM)),
        input_output_aliases={i: 2 * ns + i for i in range(2 * n)},
        compiler_params=pltpu.CompilerParams(has_side_effects=_EFFECT),
    )(*[pltpu.with_memory_space_constraint(b, pltpu.HBM) for b in bufs], *after)
    sems = list(res[:2 * ns])
    return sems, list(res[2 * ns:2 * ns + n]), list(res[2 * ns + n:2 * ns + 2 * n]), res[-1]


def _split_wait(sems, srcs, lands, after, *, name):
    n = len(srcs)
    ns = n * N_PEER_CHIPS

    def body(*refs):
        src_refs, land_refs = refs[:n], refs[n:2 * n]
        send, recv = refs[2 * n:2 * n + ns], refs[2 * n + ns:2 * n + 2 * ns]
        x, y, c = _coords()
        for a in range(n):
            for k in range(N_PEER_CHIPS):
                cp = pltpu.make_async_remote_copy(
                    src_ref=src_refs[a].at[0], dst_ref=land_refs[a].at[0], send_sem=send[a * N_PEER_CHIPS + k],
                    recv_sem=recv[a * N_PEER_CHIPS + k], device_id=(x, y, 1 - c), device_id_type=MESH)
                cp.wait_send()
                cp.wait_recv()

    bufs = list(srcs) + list(lands)
    res = pl.pallas_call(
        body, name=name,
        out_shape=tuple(pltpu.HBM(b.shape, b.dtype) for b in bufs),
        in_specs=[_HBM] * (2 * n) + [_SEM] * (2 * ns) + [_ANY] * len(after),
        out_specs=tuple([_HBM] * (2 * n)),
        input_output_aliases={i: i for i in range(2 * n)},
        compiler_params=pltpu.CompilerParams(has_side_effects=_EFFECT),
    )(*bufs, *sems, *after)
    return list(res[:n]), list(res[n:])


def _fill_own(shard2, *, name, after=None):
    _, rows, cols = shard2.shape
    tc = _col_tile(1, rows, cols, itemsize=shard2.dtype.itemsize)
    j = 2 * lax.axis_index("x") + lax.axis_index("y")
    extra = [] if after is None else [after]

    def body(j_ref, s_ref, *rest):
        rest[-1][...] = s_ref[...]

    return pl.pallas_call(
        body, name=name,
        grid_spec=pltpu.PrefetchScalarGridSpec(
            num_scalar_prefetch=1, grid=(2, cols // tc),
            in_specs=([pl.BlockSpec((None, rows, tc), lambda h, i, jr: (h, 0, i))]
                      + [pl.BlockSpec(memory_space=pl.ANY)] * len(extra)),
            out_specs=pl.BlockSpec((None, rows, tc), lambda h, i, jr: (2 * jr[0] + h, 0, i))),
        out_shape=jax.ShapeDtypeStruct((8, rows, cols), shard2.dtype),
        compiler_params=_cp(("parallel", "parallel")),
    )(j.reshape(1).astype(jnp.int32), shard2, *extra)


def _pass_to_sibling(lands, *, name):
    n = len(lands)

    def body(*refs):
        outs = refs[n:2 * n]
        send, recv = refs[2 * n:]
        x, y, c = _coords()
        cps = []
        for a in range(n):
            for k, (px, py) in enumerate(_peer_chips(x, y)):
                slot = 4 * px + 2 * py + c
                cps.append(pltpu.make_async_remote_copy(
                    src_ref=outs[a].at[slot], dst_ref=outs[a].at[slot], send_sem=send.at[a, k], recv_sem=recv.at[a, k],
                    device_id=(x, y, 1 - c), device_id_type=MESH))
        for cp in cps:
            cp.start()
        for cp in cps:
            cp.wait_recv()
        for cp in cps:
            cp.wait_send()

    return pl.pallas_call(
        body, name=name, in_specs=[_ANY] * n, out_specs=[_ANY] * n,
        out_shape=[jax.ShapeDtypeStruct(b.shape, b.dtype) for b in lands],
        input_output_aliases={a: a for a in range(n)},
        scratch_shapes=[pltpu.SemaphoreType.DMA((n, N_PEER_CHIPS)), pltpu.SemaphoreType.DMA((n, N_PEER_CHIPS))],
    )(*lands)


def _sum_parts(parts, got, *, name):
    _, rows, cols = parts.shape
    tc = _col_tile(4, rows, cols, itemsize=parts.dtype.itemsize)
    x, y, c = _coords()
    idx = jnp.stack([2 * x + y, 2 * (1 - x) + y, 2 * x + (1 - y), 2 * (1 - x) + (1 - y), c]).astype(jnp.int32)

    def body(i_ref, p_ref, g0_ref, g1_ref, g2_ref, o_ref):
        o_ref[...] = ((p_ref[...].astype(F32) + g0_ref[...].astype(F32)) + g1_ref[...].astype(F32)) + g2_ref[...].astype(F32)

    slot = lambda s: pl.BlockSpec((None, rows, tc), lambda i, ir, s=s: (ir[s], 0, i))
    return pl.pallas_call(
        body, name=name,
        grid_spec=pltpu.PrefetchScalarGridSpec(
            num_scalar_prefetch=1, grid=(cols // tc,),
            in_specs=[slot(0), slot(1), slot(2), slot(3)],
            out_specs=pl.BlockSpec((None, rows, tc), lambda i, ir: (ir[4], 0, i))),
        out_shape=jax.ShapeDtypeStruct((2, rows, cols), F32),
        compiler_params=_cp(("parallel",)),
    )(idx, parts, got, got, got)


def _sum_slots(t, *, name):
    S, rows, cols = t.shape
    tc = _col_tile(S, rows, cols)

    def body(t_ref, o_ref):
        acc = t_ref[0].astype(F32)
        for s in range(1, S):
            acc = acc + t_ref[s].astype(F32)
        o_ref[...] = acc

    return pl.pallas_call(
        body, name=name, grid=(cols // tc,),
        in_specs=[pl.BlockSpec((S, rows, tc), lambda i: (0, 0, i))],
        out_specs=pl.BlockSpec((rows, tc), lambda i: (0, i)),
        out_shape=jax.ShapeDtypeStruct((rows, cols), F32),
        compiler_params=_cp(("parallel",)),
    )(t)


def _halves_join(bufs, *, name):
    n = len(bufs)

    def body(*refs):
        outs = refs[n:2 * n]
        send, recv = refs[2 * n:]
        x, y, c = _coords()
        cps = [pltpu.make_async_remote_copy(src_ref=outs[a].at[c], dst_ref=outs[a].at[c], send_sem=send.at[a],
                                            recv_sem=recv.at[a], device_id=(x, y, 1 - c), device_id_type=MESH)
               for a in range(n)]
        for cp in cps:
            cp.start()
        for cp in cps:
            cp.wait_recv()
        for cp in cps:
            cp.wait_send()

    return pl.pallas_call(
        body, name=name, in_specs=[_ANY] * n, out_specs=[_ANY] * n,
        out_shape=[jax.ShapeDtypeStruct(b.shape, b.dtype) for b in bufs],
        input_output_aliases={a: a for a in range(n)},
        scratch_shapes=[pltpu.SemaphoreType.DMA((n,)), pltpu.SemaphoreType.DMA((n,))],
    )(*bufs)


def _swap_start(srcs, *, name):
    n = len(srcs)
    lands = [lax.empty((s.shape[0],) + s.shape[2:], s.dtype) for s in srcs]

    def body(*refs):
        src_refs, land_refs = refs[:n], refs[n:2 * n]
        send, recv, token = refs[2 * n:3 * n], refs[3 * n:4 * n], refs[-1]
        x, y, c = _coords()
        for a in range(n):
            pltpu.make_async_remote_copy(src_ref=src_refs[a].at[:, 1 - c], dst_ref=land_refs[a], send_sem=send[a],
                                         recv_sem=recv[a], device_id=(x, y, 1 - c), device_id_type=MESH).start()
        token[...] = jnp.zeros_like(token)

    bufs = list(srcs) + lands
    res = pl.pallas_call(
        body, name=name,
        out_shape=(*[pltpu.SemaphoreType.DMA(())] * (2 * n), *[pltpu.HBM(b.shape, b.dtype) for b in bufs],
                   jax.ShapeDtypeStruct((8, 128), F32)),
        in_specs=[_HBM] * (2 * n),
        out_specs=(*[_SEM] * (2 * n), *[_HBM] * (2 * n), pl.BlockSpec(memory_space=pltpu.VMEM)),
        input_output_aliases={i: 2 * n + i for i in range(2 * n)},
        compiler_params=pltpu.CompilerParams(has_side_effects=_EFFECT),
    )(*[pltpu.with_memory_space_constraint(b, pltpu.HBM) for b in bufs])
    return list(res[:2 * n]), list(res[2 * n:3 * n]), list(res[3 * n:4 * n]), res[-1]


def _swap_wait(sems, srcs, lands, after, *, name):
    n = len(srcs)

    def body(*refs):
        src_refs, land_refs = refs[:n], refs[n:2 * n]
        send, recv = refs[2 * n:3 * n], refs[3 * n:4 * n]
        x, y, c = _coords()
        for a in range(n):
            cp = pltpu.make_async_remote_copy(
                src_ref=src_refs[a].at[:, 0], dst_ref=land_refs[a], send_sem=send[a], recv_sem=recv[a],
                device_id=(x, y, 1 - c), device_id_type=MESH)
            cp.wait_send()
            cp.wait_recv()

    bufs = list(srcs) + list(lands)
    res = pl.pallas_call(
        body, name=name,
        out_shape=tuple(pltpu.HBM(b.shape, b.dtype) for b in bufs),
        in_specs=[_HBM] * (2 * n) + [_SEM] * (2 * n) + [_ANY] * len(after),
        out_specs=tuple([_HBM] * (2 * n)),
        input_output_aliases={i: i for i in range(2 * n)},
        compiler_params=pltpu.CompilerParams(has_side_effects=_EFFECT),
    )(*bufs, *sems, *after)
    return list(res[:n]), list(res[n:])


def _grad_views(grads):
    return [g.reshape(4, 2, g.shape[0] // 8, g.shape[1]) for g in grads]


def _scatter_begin(views, theirs, *, tag):
    parts = [_pair_sum(v, t, name=f"rs_pair_{tag}_{i}") for i, (v, t) in enumerate(zip(views, theirs))]
    got = [lax.empty(p.shape, p.dtype) for p in parts]
    sems, parts, got, token = _split_start(
        parts, got, lambda x, y, c, px, py: 2 * px + py, lambda x, y, c: 2 * x + y, name=f"rs_start_{tag}")
    return (sems, parts, got), token


def _reduce_scatter_begin(grads, *, tag):
    views = _grad_views(grads)
    theirs = _sibling_swap(views, name=f"rs_swap_{tag}")
    return _scatter_begin(views, theirs, tag=tag)


def _reduce_scatter_end(state, after, *, tag):
    sems, parts, got = state
    parts, got = _split_wait(sems, parts, got, after, name=f"rs_wait_{tag}")
    halves = [_sum_parts(p, t, name=f"rs_sum_{tag}_{i}") for i, (p, t) in enumerate(zip(parts, got))]
    joined = _halves_join(halves, name=f"rs_join_{tag}")
    return [j.reshape(2 * j.shape[1], j.shape[2]) for j in joined]


_SMALL = ("norm_w", "ssm_a_re", "ssm_a_im", "ssm_log_dt", "ssm_b_re", "ssm_b_im", "ssm_c_re", "ssm_c_im", "ssm_d",
          "ssm_glu_b", "sg_ln_w", "sg_ln_b", "sg_w", "sg_b", "attn_sinks", "final_norm_w")
_BIG = ("w_in", "ssm_glu_w", "w_branch_a", "w_branch_b", "w_branch_c", "w_out")
_WEIGHTS = ("norm_w", "w_in", "ssm_a_re", "ssm_a_im", "ssm_log_dt", "ssm_b_re", "ssm_b_im", "ssm_c_re", "ssm_c_im",
            "ssm_d", "ssm_glu_w", "ssm_glu_b", "sg_ln_w", "sg_ln_b", "sg_w", "sg_b", "attn_sinks", "w_branch_a",
            "w_branch_b", "w_branch_c", "w_out", "final_norm_w")
_PACK_COLS = 1024
_PACK_ALIGN = 8 * ROW_ALIGN * _PACK_COLS


def _slice_exchange(buf, *, name, after=()):
    def body(in_ref, *rest):
        out_ref, send, recv, lsem = rest[-4:]
        x, y, c = _coords()
        me = 4 * x + 2 * y + c
        own = pltpu.make_async_copy(in_ref.at[me], out_ref.at[me], lsem)
        own.start()
        cps = []
        for k in range(1, 8):
            px, py, pc = x ^ (k >> 2), y ^ ((k >> 1) & 1), c ^ (k & 1)
            cps.append(pltpu.make_async_remote_copy(
                src_ref=in_ref.at[4 * px + 2 * py + pc], dst_ref=out_ref.at[me], send_sem=send.at[k - 1],
                recv_sem=recv.at[k - 1], device_id=(px, py, pc), device_id_type=MESH))
        for cp in cps:
            cp.start()
        for cp in cps:
            cp.wait_recv()
        for cp in cps:
            cp.wait_send()
        own.wait()

    return pl.pallas_call(
        body, name=name, in_specs=[_ANY] * (1 + len(after)), out_specs=_ANY,
        out_shape=jax.ShapeDtypeStruct(buf.shape, buf.dtype),
        scratch_shapes=[pltpu.SemaphoreType.DMA((7,)), pltpu.SemaphoreType.DMA((7,)), pltpu.SemaphoreType.DMA],
    )(buf, *after)


def _allreduce_small(packed, after=()):
    rows, cols = packed.shape
    got = _slice_exchange(packed.reshape(8, rows // 8, cols), name="small_grads_exchange", after=after)
    mine = _sum_slots(got, name="small_grads_sum")
    return _gather8([mine], name="small_grads_gather")[0]


def _pack(ts):
    flat = jnp.concatenate([t.reshape(-1) for t in ts])
    pad = (-flat.shape[0]) % _PACK_ALIGN
    return jnp.pad(flat, (0, pad)).reshape(-1, _PACK_COLS)


def _unpack(buf, like):
    flat = buf.reshape(-1)
    out, pos = [], 0
    for t in like:
        out.append(flat[pos:pos + t.size].reshape(t.shape))
        pos += t.size
    return out


def kernel(x, norm_w, w_in, ssm_a_re, ssm_a_im, ssm_log_dt, ssm_b_re, ssm_b_im, ssm_c_re, ssm_c_im, ssm_d, ssm_glu_w, ssm_glu_b, sg_ln_w, sg_ln_b, sg_w, sg_b, attn_sinks, w_branch_a, w_branch_b, w_branch_c, w_out, final_norm_w, loss_target, m_norm_w, m_w_in, m_ssm_a_re, m_ssm_a_im, m_ssm_log_dt, m_ssm_b_re, m_ssm_b_im, m_ssm_c_re, m_ssm_c_im, m_ssm_d, m_ssm_glu_w, m_ssm_glu_b, m_sg_ln_w, m_sg_ln_b, m_sg_w, m_sg_b, m_attn_sinks, m_w_branch_a, m_w_branch_b, m_w_branch_c, m_w_out, m_final_norm_w, v_norm_w, v_w_in, v_ssm_a_re, v_ssm_a_im, v_ssm_log_dt, v_ssm_b_re, v_ssm_b_im, v_ssm_c_re, v_ssm_c_im, v_ssm_d, v_ssm_glu_w, v_ssm_glu_b, v_sg_ln_w, v_sg_ln_b, v_sg_w, v_sg_b, v_attn_sinks, v_w_branch_a, v_w_branch_b, v_w_branch_c, v_w_out, v_final_norm_w):
    w = dict(norm_w=norm_w, w_in=w_in, ssm_a_re=ssm_a_re, ssm_a_im=ssm_a_im, ssm_log_dt=ssm_log_dt, ssm_b_re=ssm_b_re,
             ssm_b_im=ssm_b_im, ssm_c_re=ssm_c_re, ssm_c_im=ssm_c_im, ssm_d=ssm_d, ssm_glu_w=ssm_glu_w,
             ssm_glu_b=ssm_glu_b, sg_ln_w=sg_ln_w, sg_ln_b=sg_ln_b, sg_w=sg_w, sg_b=sg_b, attn_sinks=attn_sinks,
             w_branch_a=w_branch_a, w_branch_b=w_branch_b, w_branch_c=w_branch_c, w_out=w_out,
             final_norm_w=final_norm_w)
    m = dict(norm_w=m_norm_w, w_in=m_w_in, ssm_a_re=m_ssm_a_re, ssm_a_im=m_ssm_a_im, ssm_log_dt=m_ssm_log_dt,
             ssm_b_re=m_ssm_b_re, ssm_b_im=m_ssm_b_im, ssm_c_re=m_ssm_c_re, ssm_c_im=m_ssm_c_im, ssm_d=m_ssm_d,
             ssm_glu_w=m_ssm_glu_w, ssm_glu_b=m_ssm_glu_b, sg_ln_w=m_sg_ln_w, sg_ln_b=m_sg_ln_b, sg_w=m_sg_w,
             sg_b=m_sg_b, attn_sinks=m_attn_sinks, w_branch_a=m_w_branch_a, w_branch_b=m_w_branch_b,
             w_branch_c=m_w_branch_c, w_out=m_w_out, final_norm_w=m_final_norm_w)
    v = dict(norm_w=v_norm_w, w_in=v_w_in, ssm_a_re=v_ssm_a_re, ssm_a_im=v_ssm_a_im, ssm_log_dt=v_ssm_log_dt,
             ssm_b_re=v_ssm_b_re, ssm_b_im=v_ssm_b_im, ssm_c_re=v_ssm_c_re, ssm_c_im=v_ssm_c_im, ssm_d=v_ssm_d,
             ssm_glu_w=v_ssm_glu_w, ssm_glu_b=v_ssm_glu_b, sg_ln_w=v_sg_ln_w, sg_ln_b=v_sg_ln_b, sg_w=v_sg_w,
             sg_b=v_sg_b, attn_sinks=v_attn_sinks, w_branch_a=v_w_branch_a, w_branch_b=v_w_branch_b,
             w_branch_c=v_w_branch_c, w_out=v_w_out, final_norm_w=v_final_norm_w)

    big_names = ("winT", "glu_w", "wbaT", "wbbT", "wbcT", "w_out")
    L = x.shape[1]
    tabs = _rope_tables(L)
    p = {k: w[k] for k in _SMALL}

    column_sharded = ("w_in", "w_branch_a", "w_branch_b", "w_branch_c")

    def shard_halves(ws, names=_BIG):
        ts = [(t.T if k in column_sharded else t).astype(MXU) for k, t in zip(names, ws)]
        return [t.reshape(2, t.shape[0] // 2, t.shape[1]) for t in ts]

    my_half = lambda x_, y_, c_, px, py: c_
    my_block = lambda x_, y_, c_: 4 * x_ + 2 * y_ + c_
    rows_of = lambda lands: [t.reshape(8 * t.shape[1], t.shape[2]) for t in lands]
    fill = lambda l, hs, i0=0: [_fill_own(s, name=f"gather_fill_{l}_{i0 + i}") for i, s in enumerate(hs)]
    saved = [None] * DEPTH

    halves = [shard_halves([w[k][l] for k in _BIG]) for l in range(DEPTH)]
    w_in_0 = halves[0][:1]
    sems_a, src_a, land_a, token_a = _split_start(w_in_0, fill(0, w_in_0), my_half, my_block, name="gather_start_0a")
    fill_after = lambda l, hs, i0: [_fill_own(s, name=f"gather_fill_{l}_{i0 + i}", after=token_a)
                                    for i, s in enumerate(hs)]
    lands = [[None] + fill_after(0, halves[0][1:], 1), fill_after(1, halves[1], 0)]
    sp = [_prep_layer(p, l, after=token_a) for l in range(DEPTH)]
    h0 = _rms_fwd(x[0], p["norm_w"][0], name="rms_fwd_0", after=token_a)
    _, land_a = _split_wait(sems_a, src_a, land_a, [h0, sp[0]["btr"], sp[1]["btr"]] + lands[0][1:] + lands[1],
                            name="gather_wait_0a")
    land_a = _pass_to_sibling(land_a, name="gather_pass_0a")
    sems_b, src_b, land_b, token_b = _split_start(halves[0][1:], lands[0][1:], my_half, my_block,
                                                  name="gather_start_0b", after=land_a)
    split1 = {}

    def start1(main):
        split1["sems"], split1["src"], split1["land"], token1 = _split_start(halves[1], lands[1], my_half, my_block,
                                                                             name="gather_start_1", after=[main])
        return token1

    def rest0(t):
        _, got = _split_wait(sems_b, src_b, land_b, t, name="gather_wait_0b")
        got = _pass_to_sibling(got, name="gather_pass_0b")
        return dict(zip(big_names, rows_of(land_a + got))), None

    x1, saved[0], big0 = _layer_fwd(x[0], h0, p, sp[0], rows_of(land_a)[0], rest0, 0, tabs, proj_after=token_b,
                                    after_main=start1)
    _, lands1 = _split_wait(split1["sems"], split1["src"], split1["land"], [x1], name="gather_wait_1")
    big1 = dict(zip(big_names, rows_of(_pass_to_sibling(lands1, name="gather_pass_1"))))
    bigs = [big0, big1]
    h1 = _rms_fwd(x1, p["norm_w"][1], name="rms_fwd_1")
    x2, saved[1], _ = _layer_fwd(x1, h1, p, sp[1], big1["winT"], lambda t: (big1, None), 1, tabs)
    loss, dx, dfw = _final_loss(x2, p["final_norm_w"], loss_target[0], name="final_loss")

    grads = [None] * DEPTH
    rs = {}

    def early(l):
        def begin(g):
            *rs[f"{l}a"], token_a = _swap_start(_grad_views([g[k] for k in big_names[1:]]), name=f"rs_swap_start_{l}a")
            return token_a
        return begin

    def mid(l):
        def go_on(t):
            sems, views, lands = rs[f"{l}a"]
            views, theirs = _swap_wait(sems, views, lands, [t], name=f"rs_swap_wait_{l}a")
            rs[f"{l}a"], token_a = _scatter_begin(views, theirs, tag=f"{l}a")
            return token_a
        return go_on

    def late(l, dproj, dx):
        if l == 0:
            rs["0b"], token_s = _reduce_scatter_begin([grads[0]["winT"]], tag="0b")
            dh = _proj_bwd_dh(dproj, bigs[0]["winT"], 0, token_s)
            return _rms_bwd(saved[0]["x"], p["norm_w"][0], dh, dx, name="rms_bwd_0")
        sems, views, lands, token_b = _swap_start(_grad_views([grads[l]["winT"]]), name=f"rs_swap_start_{l}b")
        dh = _proj_bwd_dh(dproj, bigs[l]["winT"], l, token_b)
        views, theirs = _swap_wait(sems, views, lands, [dh], name=f"rs_swap_wait_{l}b")
        rs[f"{l}b"], token_s = _scatter_begin(views, theirs, tag=f"{l}b")
        return _rms_bwd(saved[l]["x"], p["norm_w"][l], dh, dx, name=f"rms_bwd_{l}", after=token_s)

    def reduced(l, after):
        return _reduce_scatter_end(rs[f"{l}b"], after, tag=f"{l}b") + _reduce_scatter_end(rs[f"{l}a"], after, tag=f"{l}a")

    dproj, grads[1] = _layer_bwd(dx, saved[1], p, bigs[1], 1, tabs, early(1), mid(1))
    dx, grads[1]["norm_w"] = late(1, dproj, dx)
    dproj, grads[0] = _layer_bwd(dx, saved[0], p, bigs[0], 0, tabs, early(0), mid(0))
    dx, grads[0]["norm_w"] = late(0, dproj, dx)
    red1 = reduced(1, [dx])

    tr = lambda t: t.transpose(0, 2, 1)
    view = {k: (tr if k == "w_in" else (lambda t: t)) for k in _BIG}
    shard_grads = lambda red: dict(zip(_BIG, (red[0], red[1], red[2].T, red[3].T, red[4].T, red[5])))
    outs = {k: None for k in _BIG}

    def adamw_big(l, red):
        for k, g in shard_grads(red).items():
            outs[k] = _adamw_layer(view[k](w[k]), g, view[k](m[k]), view[k](v[k]), l, outs[k], name=f"adamw_{k}_{l}")

    adamw_big(1, red1)

    small_like = [w[k] for k in _SMALL]
    gs = [jnp.stack([grads[l][k] for l in range(DEPTH)]) if k != "final_norm_w" else dfw for k in _SMALL]
    gsum = _allreduce_small(_pack(gs + [loss.reshape(1)]), after=[outs[k][0] for k in _BIG])
    adamw_big(0, reduced(0, [gsum]))

    gfull, delta, new_m, new_v = {}, {}, {}, {}
    for k in _BIG:
        gfull[k], delta[k], new_m[k], new_v[k] = (view[k](t) for t in outs[k])
    *small_sums, loss = _unpack(gsum, small_like + [loss])
    for k, t in zip(_SMALL, small_sums):
        gfull[k] = t
        delta[k], new_m[k], new_v[k] = _adamw(w[k], t, m[k], v[k], name=f"adamw_{k}")

    return (loss, dx[None], *[gfull[k] for k in _WEIGHTS], *[delta[k] for k in _WEIGHTS],
            *[new_m[k] for k in _WEIGHTS], *[new_v[k] for k in _WEIGHTS])
```

```python
import functools
import math

import numpy as np
import jax
import jax.numpy as jnp
from jax import lax
from jax.experimental import pallas as pl
from jax.experimental.pallas import tpu as pltpu

F32 = jnp.float32
MXU = jnp.bfloat16
HIGHEST = lax.Precision.HIGHEST

D_MODEL = 2048
DEPTH = 2
EPS = 1e-6
NEG_INF = -1e30
SSM_WIDTH = 1024
SSM_GROUP = 16
SSM_GROUPS = 64
SSM_STATE = 64
SSM_CH = SSM_GROUPS * SSM_STATE
SLAB = 128
SLAB_CH = (SLAB // SSM_GROUP) * SSM_STATE
N_SLAB = SSM_WIDTH // SLAB
SCAN_SEG = 8
SCAN_STEPS = 4
SG_HEADS = 8
SG_CHUNK = 128
HEAD_DIM = 64
ATT_HEADS = 16
ATT_KV_HEADS = 2
GQA_GROUP = 8
ATT_BLOCK = 128
WINDOW = 128
ROT_DIM = 16
ROPE_THETA = 500000.0
N_MAIN = 6400
N_ZC = 1024
N_GATES = 6144
D_IN = N_MAIN + N_ZC + N_GATES

ADAM_LR = 0.001
ADAM_B1 = 0.9
ADAM_B2 = 0.999
ADAM_EPS = 1e-08
ADAM_WD = 0.01
ADAM_STEP = 10

_DIMS = {"nn": (((1,), (0,)), ((), ())), "nt": (((1,), (1,)), ((), ())), "tn": (((0,), (0,)), ((), ()))}
_MB = 1024 * 1024


def _cp(sem, vmem_mb=48):
    return pltpu.CompilerParams(dimension_semantics=sem, vmem_limit_bytes=vmem_mb * _MB)


def _dot(a, b, mode):
    return lax.dot_general(a.astype(MXU), b.astype(MXU), _DIMS[mode], preferred_element_type=F32)


@jax.custom_vjp
def _mm_nn(a, b):
    return _dot(a, b, "nn")


def _mm_nn_fwd(a, b):
    return _dot(a, b, "nn"), (a, b)


def _mm_nn_bwd(res, g):
    a, b = res
    return _dot(g, b, "nt"), _dot(a, g, "tn")


_mm_nn.defvjp(_mm_nn_fwd, _mm_nn_bwd)


@jax.custom_vjp
def _mm_nt(a, bt):
    return _dot(a, bt, "nt")


def _mm_nt_fwd(a, bt):
    return _dot(a, bt, "nt"), (a, bt)


def _mm_nt_bwd(res, g):
    a, bt = res
    return _dot(g, bt, "nn"), _dot(g, a, "tn")


_mm_nt.defvjp(_mm_nt_fwd, _mm_nt_bwd)


def _rmsnorm(x, w):
    return x * lax.rsqrt(jnp.mean(x * x, axis=-1, keepdims=True) + EPS) * w


def _layernorm(x, w, b):
    mu = jnp.mean(x, axis=-1, keepdims=True)
    var = jnp.mean(jnp.square(x - mu), axis=-1, keepdims=True)
    return (x - mu) * lax.rsqrt(var + EPS) * w + b


def _silu(x):
    return x * jax.nn.sigmoid(x)


def _matmul(a, b, mode, *, name, shape, tm, tn, tk, out_dtype=F32, add=None, a_off=(0, 0), b_off=(0, 0), after=None):
    m, n, k = shape
    tm, tn, tk = min(tm, m), min(tn, n), min(tk, k)
    assert m % tm == 0 and n % tn == 0 and k % tk == 0, (name, shape, tm, tn, tk)
    nk = k // tk
    has_add, has_after = add is not None, after is not None

    def body(*refs):
        a_ref, b_ref = refs[0], refs[1]
        pos = 2
        add_ref = None
        if has_add:
            add_ref = refs[pos]
            pos += 1
        if has_after:
            pos += 1
        o_ref = refs[pos]
        p = _dot(a_ref[...], b_ref[...], mode)
        if nk == 1:
            if has_add:
                p = p + add_ref[...].astype(F32)
            o_ref[...] = p.astype(out_dtype)
            return
        acc_ref = refs[pos + 1]
        kk = pl.program_id(2)

        @pl.when(kk == 0)
        def _():
            acc_ref[...] = p

        @pl.when(kk > 0)
        def _():
            acc_ref[...] += p

        @pl.when(kk == nk - 1)
        def _():
            r = acc_ref[...]
            if has_add:
                r = r + add_ref[...].astype(F32)
            o_ref[...] = r.astype(out_dtype)

    a0, a1 = a_off
    b0, b1 = b_off
    if mode == "tn":
        a_spec = pl.BlockSpec((tk, tm), lambda i, j, kk: (kk + a0, i + a1))
    else:
        a_spec = pl.BlockSpec((tm, tk), lambda i, j, kk: (i + a0, kk + a1))
    if mode == "nt":
        b_spec = pl.BlockSpec((tn, tk), lambda i, j, kk: (j + b0, kk + b1))
    else:
        b_spec = pl.BlockSpec((tk, tn), lambda i, j, kk: (kk + b0, j + b1))
    in_specs = [a_spec, b_spec]
    args = [a, b]
    if has_add:
        in_specs.append(pl.BlockSpec((tm, tn), lambda i, j, kk: (i, j)))
        args.append(add)
    if has_after:
        in_specs.append(pl.BlockSpec(memory_space=pl.ANY))
        args.append(after)
    return pl.pallas_call(
        body, name=name, grid=(m // tm, n // tn, nk),
        in_specs=in_specs,
        out_specs=pl.BlockSpec((tm, tn), lambda i, j, kk: (i, j)),
        out_shape=jax.ShapeDtypeStruct((m, n), out_dtype),
        scratch_shapes=[pltpu.VMEM((tm, tn), F32)] if nk > 1 else [],
        compiler_params=_cp(("parallel", "parallel", "arbitrary")),
    )(*args)


def _rms_fwd(x, w, *, name, tm=256, after=None):
    L, d = x.shape
    tm = min(tm, L)
    extra = [] if after is None else [after]

    def body(x_ref, w_ref, *rest):
        rest[-1][...] = _rmsnorm(x_ref[...], w_ref[...]).astype(MXU)

    return pl.pallas_call(
        body, name=name, grid=(L // tm,),
        in_specs=([pl.BlockSpec((tm, d), lambda i: (i, 0)), pl.BlockSpec((1, d), lambda i: (0, 0))]
                  + [pl.BlockSpec(memory_space=pl.ANY)] * len(extra)),
        out_specs=pl.BlockSpec((tm, d), lambda i: (i, 0)),
        out_shape=jax.ShapeDtypeStruct((L, d), MXU),
        compiler_params=_cp(("parallel",)),
    )(x, w.reshape(1, d), *extra)


def _rms_bwd(x, w, dh, dxn, *, name, tm=256, after=None):
    L, d = x.shape
    tm = min(tm, L)
    extra = [] if after is None else [after]

    def body(x_ref, w_ref, dh_ref, dxn_ref, *rest):
        dx_ref, dw_ref = rest[-2:]
        _, vjp = jax.vjp(_rmsnorm, x_ref[...], w_ref[...])
        dx, dw = vjp(dh_ref[...])
        dx_ref[...] = dx + dxn_ref[...]

        @pl.when(pl.program_id(0) == 0)
        def _():
            dw_ref[...] = jnp.zeros_like(dw_ref)

        dw_ref[...] += dw

    row = pl.BlockSpec((tm, d), lambda i: (i, 0))
    vec = pl.BlockSpec((1, d), lambda i: (0, 0))
    dx, dw = pl.pallas_call(
        body, name=name, grid=(L // tm,),
        in_specs=[row, vec, row, row] + [pl.BlockSpec(memory_space=pl.ANY)] * len(extra), out_specs=[row, vec],
        out_shape=[jax.ShapeDtypeStruct((L, d), F32), jax.ShapeDtypeStruct((1, d), F32)],
        compiler_params=_cp(("arbitrary",)),
    )(x, w.reshape(1, d), dh, dxn, *extra)
    return dx, dw.reshape(d)


def _final_loss(x, w, tgt, *, name, tm=256):
    L, d = x.shape
    tm = min(tm, L)

    def loss_fn(xv, wv, tv):
        err = jnp.square(_rmsnorm(xv, wv) - tv)
        return 0.5 * jnp.sum(jnp.mean(err, axis=-1, keepdims=True), axis=0, keepdims=True)

    def body(x_ref, w_ref, t_ref, loss_ref, dx_ref, dw_ref):
        tv = t_ref[...]
        val, vjp = jax.vjp(lambda xv, wv: loss_fn(xv, wv, tv), x_ref[...], w_ref[...])
        dx, dw = vjp(jnp.ones((1, 1), F32))
        dx_ref[...] = dx

        @pl.when(pl.program_id(0) == 0)
        def _():
            dw_ref[...] = jnp.zeros_like(dw_ref)
            loss_ref[...] = jnp.zeros_like(loss_ref)

        dw_ref[...] += dw
        loss_ref[...] += jnp.broadcast_to(val, loss_ref.shape)

    row = pl.BlockSpec((tm, d), lambda i: (i, 0))
    vec = pl.BlockSpec((1, d), lambda i: (0, 0))
    loss, dx, dw = pl.pallas_call(
        body, name=name, grid=(L // tm,),
        in_specs=[row, vec, row],
        out_specs=[pl.BlockSpec((8, 128), lambda i: (0, 0)), row, vec],
        out_shape=[jax.ShapeDtypeStruct((8, 128), F32), jax.ShapeDtypeStruct((L, d), F32),
                   jax.ShapeDtypeStruct((1, d), F32)],
        compiler_params=_cp(("arbitrary",)),
    )(x, w.reshape(1, d), tgt)
    return loss[0, 0], dx, dw.reshape(d)


PARAM_ROWS = 512


def _s5_param_fn(are, aim, ldt, bre, bim, row0):
    n = are.shape[0]
    grp = (row0 + lax.broadcasted_iota(jnp.int32, (n, SSM_GROUPS), 0)) // SSM_STATE
    col = lax.broadcasted_iota(jnp.int32, (n, SSM_GROUPS), 1)
    sel = (grp == col).astype(F32)
    dt = jnp.sum(sel * jnp.exp(ldt), axis=-1, keepdims=True)
    mag = jnp.exp(are * dt)
    ang = aim * dt
    lbr = mag * jnp.cos(ang)
    lbi = mag * jnp.sin(ang)
    den = are * are + aim * aim
    nr = lbr - 1.0
    kr = (nr * are + lbi * aim) / den
    ki = (lbi * are - nr * aim) / den
    return lbr, lbi, kr * bre - ki * bim, kr * bim + ki * bre


def _s5_param_specs():
    col = pl.BlockSpec((PARAM_ROWS, 1), lambda i: (i, 0))
    mat = pl.BlockSpec((PARAM_ROWS, SSM_GROUP), lambda i: (i, 0))
    vec = pl.BlockSpec((1, SSM_GROUPS), lambda i: (0, 0))
    return col, mat, vec


def _s5_params_fwd(are, aim, ldt, bre, bim, *, name, after=None):
    n = are.shape[0]
    col, mat, vec = _s5_param_specs()
    extra = [] if after is None else [after]

    def body(are_ref, aim_ref, ldt_ref, bre_ref, bim_ref, *rest):
        lbr_ref, lbi_ref, bbr_ref, bbi_ref = rest[-4:]
        row0 = pl.program_id(0) * PARAM_ROWS
        lbr, lbi, bbr, bbi = _s5_param_fn(are_ref[...], aim_ref[...], ldt_ref[...], bre_ref[...], bim_ref[...], row0)
        lbr_ref[...] = lbr
        lbi_ref[...] = lbi
        bbr_ref[...] = bbr
        bbi_ref[...] = bbi

    cshape = jax.ShapeDtypeStruct((n, 1), F32)
    mshape = jax.ShapeDtypeStruct((n, SSM_GROUP), F32)
    return pl.pallas_call(body, name=name, grid=(n // PARAM_ROWS,),
                          in_specs=[col, col, vec, mat, mat] + [pl.BlockSpec(memory_space=pl.ANY)] * len(extra),
                          out_specs=[col, col, mat, mat], out_shape=[cshape, cshape, mshape, mshape],
                          compiler_params=_cp(("parallel",)))(are, aim, ldt, bre, bim, *extra)


def _s5_params_bwd(are, aim, ldt, bre, bim, dlbr, dlbi, dbbr, dbbi, *, name):
    n = are.shape[0]
    col, mat, vec = _s5_param_specs()

    def body(are_ref, aim_ref, ldt_ref, bre_ref, bim_ref, g0, g1, g2, g3, o0, o1, o2, o3, o4):
        row0 = pl.program_id(0) * PARAM_ROWS
        _, vjp = jax.vjp(lambda a, b, c, d, e: _s5_param_fn(a, b, c, d, e, row0),
                         are_ref[...], aim_ref[...], ldt_ref[...], bre_ref[...], bim_ref[...])
        dare, daim, dldt, dbre, dbim = vjp((g0[...], g1[...], g2[...], g3[...]))
        o0[...] = dare
        o1[...] = daim
        o3[...] = dbre
        o4[...] = dbim

        @pl.when(pl.program_id(0) == 0)
        def _():
            o2[...] = jnp.zeros_like(o2)

        o2[...] += dldt

    cshape = jax.ShapeDtypeStruct((n, 1), F32)
    mshape = jax.ShapeDtypeStruct((n, SSM_GROUP), F32)
    return pl.pallas_call(body, name=name, grid=(n // PARAM_ROWS,),
                          in_specs=[col, col, vec, mat, mat, col, col, mat, mat],
                          out_specs=[col, col, vec, mat, mat],
                          out_shape=[cshape, cshape, jax.ShapeDtypeStruct((1, SSM_GROUPS), F32), mshape, mshape],
                          compiler_params=_cp(("arbitrary",)))(are, aim, ldt, bre, bim, dlbr, dlbi, dbbr, dbbi)


SLAB_NC = SLAB_CH // 128


def _s5_specs(L):
    slab = pl.BlockSpec((L, SLAB), lambda s: (0, s))
    wspec = pl.BlockSpec((SLAB_NC, 128, SLAB), lambda s: (s, 0, 0))
    lspec = pl.BlockSpec((SLAB_NC, 1, 128), lambda s: (s, 0, 0))
    sspec = pl.BlockSpec((SLAB_NC, L, 128), lambda s: (s, 0, 0))
    dspec = pl.BlockSpec((1, SLAB), lambda s: (0, s))
    return slab, wspec, lspec, sspec, dspec


def _scan_inplace(sr_ref, si_ref, lr, li, pr_ref, pi_ref, *, reverse):
    NC, L, W = sr_ref.shape
    S = SCAN_SEG
    T = L // S
    lr8 = [jnp.broadcast_to(lr[k], (S, W)) for k in range(NC)]
    li8 = [jnp.broadcast_to(li[k], (S, W)) for k in range(NC)]

    def tiles(first, count):
        return pl.ds(first * S, count * S)

    for k in range(NC):
        pr_ref[k, tiles(T - 1 if reverse else 0, 1), :] = lr8[k]
        pi_ref[k, tiles(T - 1 if reverse else 0, 1), :] = li8[k]
        n = 1
        while n < T:
            have = tiles(T - n, n) if reverse else tiles(0, n)
            new = tiles(T - 2 * n, n) if reverse else tiles(n, n)
            top = tiles(T - n, 1) if reverse else tiles(n - 1, 1)
            ar, ai = pr_ref[k, top, :][None], pi_ref[k, top, :][None]
            hr, hi = pr_ref[k, have, :].reshape(n, S, W), pi_ref[k, have, :].reshape(n, S, W)
            pr_ref[k, new, :] = (hr * ar - hi * ai).reshape(n * S, W)
            pi_ref[k, new, :] = (hr * ai + hi * ar).reshape(n * S, W)
            n *= 2

    def step(i, carry):
        for u in range(SCAN_STEPS):
            jj = i * SCAN_STEPS + u
            rows = pl.ds(pl.multiple_of(((T - 1 - jj) if reverse else jj) * S, S), S)
            out = []
            for k in range(NC):
                sr, si = carry[k]
                nsr = lr8[k] * sr - li8[k] * si + sr_ref[k, rows, :]
                nsi = lr8[k] * si + li8[k] * sr + si_ref[k, rows, :]
                sr_ref[k, rows, :] = nsr
                si_ref[k, rows, :] = nsi
                out.append((nsr, nsi))
            carry = tuple(out)
        return carry

    zero = jnp.zeros((S, W), F32)
    ends = lax.fori_loop(0, T // SCAN_STEPS, step, tuple((zero, zero) for k in range(NC)))
    sub = lax.broadcasted_iota(jnp.int32, (S, W), 0)
    order = range(S - 1, -1, -1) if reverse else range(S)
    for k in range(NC):
        er, ei = ends[k]
        full = tiles(0 if reverse else T - 1, 1)
        ltr = pr_ref[k, full, :][0:1]
        lti = pi_ref[k, full, :][0:1]
        cr = jnp.zeros((1, W), F32)
        ci = jnp.zeros((1, W), F32)
        ctr = jnp.zeros((S, W), F32)
        cti = jnp.zeros((S, W), F32)
        for seg in order:
            ctr = jnp.where(sub == seg, cr, ctr)
            cti = jnp.where(sub == seg, ci, cti)
            cr, ci = (er[seg:seg + 1, :] + ltr * cr - lti * ci, ei[seg:seg + 1, :] + ltr * ci + lti * cr)
        pr = pr_ref[k].reshape(T, S, W)
        pi = pi_ref[k].reshape(T, S, W)
        sr_ref[k] += (pr * ctr[None] - pi * cti[None]).reshape(L, W)
        si_ref[k] += (pr * cti[None] + pi * ctr[None]).reshape(L, W)


def _time_interleave(a):
    L, W = a.shape
    return a.reshape(SCAN_SEG, L // SCAN_SEG, W).transpose(1, 0, 2).reshape(L, W)


def _time_deinterleave(a):
    L, W = a.shape
    return a.reshape(L // SCAN_SEG, SCAN_SEG, W).transpose(1, 0, 2).reshape(L, W)


def _s5_fwd(u, btr, bti, cbr, cbi, lbr, lbi, dvec, *, name, after=None):
    L = u.shape[0]
    extra = [] if after is None else [after]

    def body(u_ref, btr_ref, bti_ref, cbr_ref, cbi_ref, lr_ref, li_ref, d_ref, *rest):
        ys_ref, sr_ref, si_ref, pr_ref, pi_ref = rest[-5:]
        u = u_ref[...]
        for k in range(SLAB_NC):
            sr_ref[k] = _dot(u, btr_ref[k], "nt")
            si_ref[k] = _dot(u, bti_ref[k], "nt")
        _scan_inplace(sr_ref, si_ref, lr_ref[...], li_ref[...], pr_ref, pi_ref, reverse=False)
        ys = d_ref[...] * u
        for k in range(SLAB_NC):
            ys = ys + _dot(sr_ref[k], cbr_ref[k], "nn") - _dot(si_ref[k], cbi_ref[k], "nn")
        ys_ref[...] = ys

    slab, wspec, lspec, sspec, dspec = _s5_specs(L)
    sshape = jax.ShapeDtypeStruct((N_SLAB * SLAB_NC, L, 128), F32)
    return pl.pallas_call(
        body, name=name, grid=(N_SLAB,),
        in_specs=[slab, wspec, wspec, wspec, wspec, lspec, lspec, dspec] + [pl.BlockSpec(memory_space=pl.ANY)] * len(extra),
        out_specs=[slab, sspec, sspec],
        out_shape=[jax.ShapeDtypeStruct((L, SSM_WIDTH), F32), sshape, sshape],
        scratch_shapes=[pltpu.VMEM((SLAB_NC, L, 128), F32), pltpu.VMEM((SLAB_NC, L, 128), F32)],
        compiler_params=_cp(("parallel",), 56),
    )(u, btr, bti, cbr, cbi, lbr, lbi, dvec, *extra)


def _s5_bwd(dys, u, sr, si, btr, bti, cbr, cbi, lbr, lbi, dvec, *, name, after=None):
    L = u.shape[0]
    S = SCAN_SEG
    extra = [] if after is None else [after]

    def body(dys_ref, u_ref, sr_ref, si_ref, btr_ref, bti_ref, cbr_ref, cbi_ref, lr_ref, li_ref, d_ref, *rest):
        (du_ref, dbtr_ref, dbti_ref, dcbr_ref, dcbi_ref, dlr_ref, dli_ref, dd_ref,
         ar_ref, ai_ref, pr_ref, pi_ref) = rest[-12:]
        dys = dys_ref[...]
        u = u_ref[...]
        for k in range(SLAB_NC):
            ar_ref[k] = _dot(dys, cbr_ref[k], "nt")
            ai_ref[k] = -_dot(dys, cbi_ref[k], "nt")
        _scan_inplace(ar_ref, ai_ref, lr_ref[...], -li_ref[...], pr_ref, pi_ref, reverse=True)
        head = lax.broadcasted_iota(jnp.int32, (L, 1), 0) < S
        sub0 = lax.broadcasted_iota(jnp.int32, (S, 1), 0) == 0

        def prev_state(s):
            up = pltpu.roll(s, S, 0)
            return jnp.where(head, 0.0, up), jnp.where(sub0, 0.0, pltpu.roll(up[0:S], 1, 0))

        du = d_ref[...] * dys
        for k in range(SLAB_NC):
            a_re = ar_ref[k]
            a_im = ai_ref[k]
            du = du + _dot(a_re, btr_ref[k], "nn") + _dot(a_im, bti_ref[k], "nn")
            dbtr_ref[k] = _dot(a_re, u, "tn")
            dbti_ref[k] = _dot(a_im, u, "tn")
            s_re = sr_ref[k]
            s_im = si_ref[k]
            dcbr_ref[k] = _dot(s_re, dys, "tn")
            dcbi_ref[k] = -_dot(s_im, dys, "tn")
            p_re, q_re = prev_state(s_re)
            p_im, q_im = prev_state(s_im)
            b_re, b_im = a_re[0:S], a_im[0:S]
            dlr_ref[k] = (jnp.sum(p_re * a_re + p_im * a_im, axis=0, keepdims=True)
                          + jnp.sum(q_re * b_re + q_im * b_im, axis=0, keepdims=True))
            dli_ref[k] = (jnp.sum(p_re * a_im - p_im * a_re, axis=0, keepdims=True)
                          + jnp.sum(q_re * b_im - q_im * b_re, axis=0, keepdims=True))
        du_ref[...] = du
        dd_ref[...] = jnp.sum(dys * u, axis=0, keepdims=True)

    slab, wspec, lspec, sspec, dspec = _s5_specs(L)
    wshape = jax.ShapeDtypeStruct((N_SLAB * SLAB_NC, 128, SLAB), F32)
    lshape = jax.ShapeDtypeStruct((N_SLAB * SLAB_NC, 1, 128), F32)
    return pl.pallas_call(
        body, name=name, grid=(N_SLAB,),
        in_specs=([slab, slab, sspec, sspec, wspec, wspec, wspec, wspec, lspec, lspec, dspec]
                  + [pl.BlockSpec(memory_space=pl.ANY)] * len(extra)),
        out_specs=[slab, wspec, wspec, wspec, wspec, lspec, lspec, dspec],
        out_shape=[jax.ShapeDtypeStruct((L, SSM_WIDTH), F32), wshape, wshape, wshape, wshape, lshape, lshape,
                   jax.ShapeDtypeStruct((1, SSM_WIDTH), F32)],
        scratch_shapes=[pltpu.VMEM((SLAB_NC, L, 128), F32)] * 4,
        compiler_params=_cp(("parallel",), 56),
    )(dys, u, sr, si, btr, bti, cbr, cbi, lbr, lbi, dvec, *extra)


_SLAB_MASK = (np.arange(SLAB_CH)[:, None] // SSM_STATE == np.arange(SLAB)[None, :] // SSM_GROUP)


def _expand_bd(x):
    t = jnp.tile(x.reshape(N_SLAB, SLAB_CH, SSM_GROUP), (1, 1, SLAB // SSM_GROUP))
    return jnp.where(_SLAB_MASK[None], t, 0.0).astype(MXU).reshape(N_SLAB * SLAB_NC, 128, SLAB)


def _contract_bd(dx):
    t = jnp.where(_SLAB_MASK[None], dx.reshape(N_SLAB, SLAB_CH, SLAB), 0.0)
    return jnp.sum(t.reshape(N_SLAB, SLAB_CH, SLAB // SSM_GROUP, SSM_GROUP), axis=2).reshape(SSM_CH, SSM_GROUP)


def _glu_ew(ys, zlin, za):
    a1 = jax.nn.gelu(ys)
    return a1 * jax.nn.sigmoid(zlin) * _silu(za)


def _glu_fwd(ys, main, gw, gb, *, name, tm=256):
    L = ys.shape[0]
    tm = min(tm, L)
    W = SSM_WIDTH

    def body(ys_ref, za_ref, gw_ref, gb_ref, ya_ref):
        ys = ys_ref[...]
        a1 = jax.nn.gelu(ys)
        zlin = _dot(a1, gw_ref[...], "nn") + gb_ref[...]
        ya_ref[...] = _glu_ew(ys, zlin, za_ref[...]).astype(MXU)

    return pl.pallas_call(
        body, name=name, grid=(L // tm,),
        in_specs=[pl.BlockSpec((tm, W), lambda i: (i, 0)), pl.BlockSpec((tm, W), lambda i: (i, 1)),
                  pl.BlockSpec((W, W), lambda i: (0, 0)), pl.BlockSpec((1, W), lambda i: (0, 0))],
        out_specs=pl.BlockSpec((tm, W), lambda i: (i, 0)),
        out_shape=jax.ShapeDtypeStruct((L, W), MXU),
        compiler_params=_cp(("parallel",)),
    )(ys, main, gw, gb.reshape(1, W))


def _glu_bwd(dya, ys, main, gw, gb, *, name, tm=256):
    L = ys.shape[0]
    tm = min(tm, L)
    W = SSM_WIDTH

    def body(dya_ref, ys_ref, za_ref, gw_ref, gb_ref, dys_ref, dza_ref, a1_ref, dzl_ref, db_ref):
        ys = ys_ref[...]
        a1, gelu_vjp = jax.vjp(jax.nn.gelu, ys)
        zlin = _dot(a1, gw_ref[...], "nn") + gb_ref[...]
        _, vjp = jax.vjp(lambda a, z, za: a * jax.nn.sigmoid(z) * _silu(za), a1, zlin, za_ref[...])
        da1, dzlin, dza = vjp(dya_ref[...].astype(F32))
        da1 = da1 + _dot(dzlin, gw_ref[...], "nt")
        dys_ref[...] = gelu_vjp(da1)[0]
        dza_ref[...] = dza
        a1_ref[...] = a1.astype(MXU)
        dzl_ref[...] = dzlin.astype(MXU)

        @pl.when(pl.program_id(0) == 0)
        def _():
            db_ref[...] = jnp.zeros_like(db_ref)

        db_ref[...] += jnp.sum(dzlin, axis=0, keepdims=True)

    row = pl.BlockSpec((tm, W), lambda i: (i, 0))
    vec = pl.BlockSpec((1, W), lambda i: (0, 0))
    return pl.pallas_call(
        body, name=name, grid=(L // tm,),
        in_specs=[row, row, pl.BlockSpec((tm, W), lambda i: (i, 1)), pl.BlockSpec((W, W), lambda i: (0, 0)), vec],
        out_specs=[row, row, row, row, vec],
        out_shape=[jax.ShapeDtypeStruct((L, W), F32), jax.ShapeDtypeStruct((L, W), F32),
                   jax.ShapeDtypeStruct((L, W), MXU), jax.ShapeDtypeStruct((L, W), MXU),
                   jax.ShapeDtypeStruct((1, W), F32)],
        compiler_params=_cp(("arbitrary",)),
    )(dya, ys, main, gw, gb.reshape(1, W))


def _sg_fn(ub, vb, zb, lnw, lnb, ws, bs):
    u = jax.nn.gelu(ub)
    v = _layernorm(jax.nn.gelu(vb), lnw, lnb)
    r = lax.broadcasted_iota(jnp.int32, (SG_CHUNK, SG_CHUNK), 0)
    c = lax.broadcasted_iota(jnp.int32, (SG_CHUNK, SG_CHUNK), 1)
    tri = r >= c
    outs = []
    for h in range(SG_HEADS):
        wh = jnp.where(tri, ws[h], 0.0)
        outs.append(_mm_nn(wh, v[:, h * 128:(h + 1) * 128]) + bs[h])
    mixed = jnp.concatenate(outs, axis=1)
    return u * mixed * _silu(zb)


def _sg_specs(L):
    W = SSM_WIDTH
    blk = lambda c: pl.BlockSpec((SG_CHUNK, W), lambda i, c=c: (i, c))
    vec = pl.BlockSpec((1, W), lambda i: (0, 0))
    wspec = pl.BlockSpec((SG_HEADS, SG_CHUNK, SG_CHUNK), lambda i: (0, 0, 0))
    bspec = pl.BlockSpec((SG_HEADS, SG_CHUNK, 1), lambda i: (0, 0, 0))
    return blk, vec, wspec, bspec


def _sg_fwd(main, lnw, lnb, sgw, sgb, *, name):
    L = main.shape[0]
    W = SSM_WIDTH
    blk, vec, wspec, bspec = _sg_specs(L)

    def body(ub_ref, vb_ref, zb_ref, lnw_ref, lnb_ref, w_ref, b_ref, yb_ref):
        ws = [w_ref[h] for h in range(SG_HEADS)]
        bs = [b_ref[h] for h in range(SG_HEADS)]
        yb_ref[...] = _sg_fn(ub_ref[...], vb_ref[...], zb_ref[...], lnw_ref[...], lnb_ref[...], ws, bs).astype(MXU)

    return pl.pallas_call(
        body, name=name, grid=(L // SG_CHUNK,),
        in_specs=[blk(2), blk(3), blk(4), vec, vec, wspec, bspec],
        out_specs=pl.BlockSpec((SG_CHUNK, W), lambda i: (i, 0)),
        out_shape=jax.ShapeDtypeStruct((L, W), MXU),
        compiler_params=_cp(("parallel",)),
    )(main, main, main, lnw.reshape(1, W), lnb.reshape(1, W), sgw, sgb.reshape(SG_HEADS, SG_CHUNK, 1))


def _sg_bwd(dyb, main, lnw, lnb, sgw, sgb, *, name):
    L = main.shape[0]
    W = SSM_WIDTH
    blk, vec, wspec, bspec = _sg_specs(L)

    def body(dyb_ref, ub_ref, vb_ref, zb_ref, lnw_ref, lnb_ref, w_ref, b_ref,
             dub_ref, dvb_ref, dzb_ref, dlnw_ref, dlnb_ref, dw_ref, db_ref):
        ws = [w_ref[h] for h in range(SG_HEADS)]
        bs = [b_ref[h] for h in range(SG_HEADS)]
        _, vjp = jax.vjp(_sg_fn, ub_ref[...], vb_ref[...], zb_ref[...], lnw_ref[...], lnb_ref[...], ws, bs)
        dub, dvb, dzb, dlnw, dlnb, dws, dbs = vjp(dyb_ref[...])

        @pl.when(pl.program_id(0) == 0)
        def _():
            dlnw_ref[...] = jnp.zeros_like(dlnw_ref)
            dlnb_ref[...] = jnp.zeros_like(dlnb_ref)
            dw_ref[...] = jnp.zeros_like(dw_ref)
            db_ref[...] = jnp.zeros_like(db_ref)

        dub_ref[...] = dub
        dvb_ref[...] = dvb
        dzb_ref[...] = dzb
        dlnw_ref[...] += dlnw
        dlnb_ref[...] += dlnb
        for h in range(SG_HEADS):
            dw_ref[h] += dws[h]
            db_ref[h] += dbs[h]

    row = pl.BlockSpec((SG_CHUNK, W), lambda i: (i, 0))
    out = jax.ShapeDtypeStruct((L, W), F32)
    return pl.pallas_call(
        body, name=name, grid=(L // SG_CHUNK,),
        in_specs=[row, blk(2), blk(3), blk(4), vec, vec, wspec, bspec],
        out_specs=[row, row, row, vec, vec, wspec, bspec],
        out_shape=[out, out, out, jax.ShapeDtypeStruct((1, W), F32), jax.ShapeDtypeStruct((1, W), F32),
                   jax.ShapeDtypeStruct((SG_HEADS, SG_CHUNK, SG_CHUNK), F32),
                   jax.ShapeDtypeStruct((SG_HEADS, SG_CHUNK, 1), F32)],
        compiler_params=_cp(("arbitrary",)),
    )(dyb, main, main, main, lnw.reshape(1, W), lnb.reshape(1, W), sgw, sgb.reshape(SG_HEADS, SG_CHUNK, 1))


def _rope_tables(L):
    half = ROT_DIM // 2
    inv_freq = ROPE_THETA ** (-jnp.arange(0, ROT_DIM, 2, dtype=F32) / ROT_DIM)
    ang = jnp.arange(L, dtype=F32)[:, None] * inv_freq[None, :]
    cos = jnp.cos(ang)
    sin = jnp.sin(ang)
    ones = jnp.ones((L, HEAD_DIM - ROT_DIM), F32)
    cosf = jnp.concatenate([cos, cos, ones], axis=1)
    sinf = jnp.concatenate([sin, sin, 0.0 * ones], axis=1)
    rot = np.zeros((HEAD_DIM, HEAD_DIM), np.float32)
    for d in range(half):
        rot[d + half, d] = -1.0
        rot[d, d + half] = 1.0
    return cosf, sinf, jnp.asarray(rot)


def _rope(t, cosf, sinf, rot):
    shp = t.shape
    t2 = t.reshape(-1, HEAD_DIM)
    sw = lax.dot_general(t2, rot, _DIMS["nn"], precision=lax.Precision.HIGH, preferred_element_type=F32).reshape(shp)
    return t * cosf + sw * sinf


def _attn_core_parts(s, va, sink):
    h, q, k = s.shape
    m = jnp.maximum(jnp.max(s, axis=-1, keepdims=True), sink)
    e = jnp.exp(s - m)
    es = jnp.exp(sink - m)
    ev = _dot(e.reshape(h * q, k), va, "nn")
    r = 1.0 / (ev[:, HEAD_DIM:HEAD_DIM + 1].reshape(h, q, 1) + es)
    return ev[:, :HEAD_DIM] * r.reshape(h * q, 1), e, r, es


@jax.custom_vjp
def _attn_core(s, v, va, sink):
    return _attn_core_parts(s, va, sink)[0]


def _attn_core_fwd(s, v, va, sink):
    o, e, r, es = _attn_core_parts(s, va, sink)
    return o, (o, e, r, es, v, va)


def _attn_core_bwd(res, do):
    o, e, r, es, v, va = res
    h, q, k = e.shape
    p = e * r
    t = jnp.sum(o * do, axis=-1, keepdims=True).reshape(h, q, 1)
    dp = _dot(do, v, "nt").reshape(h, q, k)
    dv = _dot(p.reshape(h * q, k), do, "tn")
    dsink = -jnp.sum(es * r * t, axis=1, keepdims=True)
    return p * (dp - t), dv, jnp.zeros_like(va), dsink


_attn_core.defvjp(_attn_core_fwd, _attn_core_bwd)


def _attn_block_fn(q, kw, vw, sinks, vaw, cq, sq, ck, sk, rot, q0, k0):
    nk = kw.shape[1]
    qr = _rope(q, cq, sq, rot)
    kr = _rope(kw, ck, sk, rot)
    qpos = q0 + lax.broadcasted_iota(jnp.int32, (1, ATT_BLOCK, nk), 1)
    kpos = k0 + lax.broadcasted_iota(jnp.int32, (1, ATT_BLOCK, nk), 2)
    diff = qpos - kpos
    allowed = (diff >= 0) & (diff < WINDOW)
    outs = []
    for kh in range(ATT_KV_HEADS):
        qh = qr[kh * GQA_GROUP:(kh + 1) * GQA_GROUP].reshape(GQA_GROUP * ATT_BLOCK, HEAD_DIM)
        s = _mm_nt(qh, kr[kh]).reshape(GQA_GROUP, ATT_BLOCK, nk) * (HEAD_DIM ** -0.5)
        s = jnp.where(allowed, s, NEG_INF)
        o = _attn_core(s, vw[kh], vaw[kh], sinks[kh * GQA_GROUP:(kh + 1) * GQA_GROUP])
        outs.append(o.reshape(GQA_GROUP, ATT_BLOCK, HEAD_DIM))
    return jnp.concatenate(outs, axis=0)


def _attn_common(L):
    nwin = min(2 * ATT_BLOCK, L)
    qspec = pl.BlockSpec((ATT_HEADS, ATT_BLOCK, HEAD_DIM), lambda n: (0, n, 0))
    kvspec = pl.BlockSpec((ATT_KV_HEADS, L, HEAD_DIM), lambda n: (0, 0, 0))
    sspec = pl.BlockSpec((ATT_HEADS, 1, 1), lambda n: (0, 0, 0))
    tq = pl.BlockSpec((ATT_BLOCK, HEAD_DIM), lambda n: (n, 0))
    tk = pl.BlockSpec((L, HEAD_DIM), lambda n: (0, 0))
    rspec = pl.BlockSpec((HEAD_DIM, HEAD_DIM), lambda n: (0, 0))
    vaspec = pl.BlockSpec((ATT_KV_HEADS, L, 2 * HEAD_DIM), lambda n: (0, 0, 0))
    return nwin, qspec, kvspec, sspec, tq, tk, rspec, vaspec


def _v_with_ones(vh):
    return jnp.concatenate([vh, jnp.ones_like(vh)], axis=-1).astype(MXU)


def _attn_fwd(qh, kh, vh, sinks, cosf, sinf, rot, *, name):
    L = qh.shape[1]
    nwin, qspec, kvspec, sspec, tq, tk, rspec, vaspec = _attn_common(L)

    def body(q_ref, k_ref, v_ref, s_ref, va_ref, cq_ref, sq_ref, ck_ref, sk_ref, r_ref, o_ref):
        n = pl.program_id(0)
        k0 = pl.multiple_of(jnp.maximum(n - 1, 0) * ATT_BLOCK, ATT_BLOCK)
        win = pl.ds(k0, nwin)
        o_ref[...] = _attn_block_fn(q_ref[...], k_ref[:, win, :], v_ref[:, win, :], s_ref[...], va_ref[:, win, :],
                                    cq_ref[...], sq_ref[...], ck_ref[win, :], sk_ref[win, :], r_ref[...],
                                    n * ATT_BLOCK, k0)

    return pl.pallas_call(
        body, name=name, grid=(L // ATT_BLOCK,),
        in_specs=[qspec, kvspec, kvspec, sspec, vaspec, tq, tq, tk, tk, rspec],
        out_specs=qspec,
        out_shape=jax.ShapeDtypeStruct((ATT_HEADS, L, HEAD_DIM), F32),
        compiler_params=_cp(("parallel",)),
    )(qh, kh, vh, sinks.reshape(ATT_HEADS, 1, 1), _v_with_ones(vh), cosf, sinf, cosf, sinf, rot)


def _attn_bwd(do, qh, kh, vh, sinks, cosf, sinf, rot, *, name):
    L = qh.shape[1]
    nwin, qspec, kvspec, sspec, tq, tk, rspec, vaspec = _attn_common(L)

    def body(do_ref, q_ref, k_ref, v_ref, s_ref, va_ref, cq_ref, sq_ref, ck_ref, sk_ref, r_ref,
             dq_ref, dk_ref, dv_ref, ds_ref):
        n = pl.program_id(0)
        k0 = pl.multiple_of(jnp.maximum(n - 1, 0) * ATT_BLOCK, ATT_BLOCK)
        win = pl.ds(k0, nwin)
        cq, sq, ck, sk, rt = cq_ref[...], sq_ref[...], ck_ref[win, :], sk_ref[win, :], r_ref[...]
        vaw = va_ref[:, win, :]
        q0 = n * ATT_BLOCK
        _, vjp = jax.vjp(lambda q, kw, vw, s: _attn_block_fn(q, kw, vw, s, vaw, cq, sq, ck, sk, rt, q0, k0),
                         q_ref[...], k_ref[:, win, :], v_ref[:, win, :], s_ref[...])
        dq, dkw, dvw, ds = vjp(do_ref[...])

        @pl.when(n == 0)
        def _():
            dk_ref[...] = jnp.zeros_like(dk_ref)
            dv_ref[...] = jnp.zeros_like(dv_ref)
            ds_ref[...] = jnp.zeros_like(ds_ref)

        dq_ref[...] = dq
        dk_ref[:, win, :] += dkw
        dv_ref[:, win, :] += dvw
        ds_ref[...] += ds

    return pl.pallas_call(
        body, name=name, grid=(L // ATT_BLOCK,),
        in_specs=[qspec, qspec, kvspec, kvspec, sspec, vaspec, tq, tq, tk, tk, rspec],
        out_specs=[qspec, kvspec, kvspec, sspec],
        out_shape=[jax.ShapeDtypeStruct((ATT_HEADS, L, HEAD_DIM), F32),
                   jax.ShapeDtypeStruct((ATT_KV_HEADS, L, HEAD_DIM), F32),
                   jax.ShapeDtypeStruct((ATT_KV_HEADS, L, HEAD_DIM), F32),
                   jax.ShapeDtypeStruct((ATT_HEADS, 1, 1), F32)],
        compiler_params=_cp(("arbitrary",)),
    )(do, qh, kh, vh, sinks.reshape(ATT_HEADS, 1, 1), _v_with_ones(vh), cosf, sinf, cosf, sinf, rot)


def _to_heads(t, nh):
    L = t.shape[0]
    return t.reshape(L, nh, HEAD_DIM).transpose(1, 0, 2)


def _from_heads(t):
    nh, L, _ = t.shape
    return t.transpose(1, 0, 2).reshape(L, nh * HEAD_DIM)


def _branch_fwd(ya, yb, o2d, zc, gates, wa, wb, wc, *, name, tm=256):
    L = ya.shape[0]
    tm = min(tm, L)
    W, D = SSM_WIDTH, D_MODEL

    def body(ya_ref, yb_ref, o_ref, zc_ref, g0_ref, g1_ref, g2_ref, wa_ref, wb_ref, wc_ref,
             mg_ref, ta_ref, tb_ref, tc_ref, yc_ref):
        yc = (o_ref[...] * _silu(zc_ref[...])).astype(MXU)
        ta = _dot(ya_ref[...], wa_ref[...], "nt")
        tb = _dot(yb_ref[...], wb_ref[...], "nt")
        tc = _dot(yc, wc_ref[...], "nt")
        ta_ref[...] = ta
        tb_ref[...] = tb
        tc_ref[...] = tc
        yc_ref[...] = yc
        mg_ref[...] = (jax.nn.sigmoid(g0_ref[...]) * ta + jax.nn.sigmoid(g1_ref[...]) * tb
                       + jax.nn.sigmoid(g2_ref[...]) * tc).astype(MXU)

    row = pl.BlockSpec((tm, W), lambda i: (i, 0))
    wide = pl.BlockSpec((tm, D), lambda i: (i, 0))
    gate = lambda c: pl.BlockSpec((tm, D), lambda i, c=c: (i, c))
    wspec = pl.BlockSpec((D, W), lambda i: (0, 0))
    return pl.pallas_call(
        body, name=name, grid=(L // tm,),
        in_specs=[row, row, row, row, gate(0), gate(1), gate(2), wspec, wspec, wspec],
        out_specs=[wide, wide, wide, wide, row],
        out_shape=[jax.ShapeDtypeStruct((L, D), MXU), jax.ShapeDtypeStruct((L, D), F32),
                   jax.ShapeDtypeStruct((L, D), F32), jax.ShapeDtypeStruct((L, D), F32),
                   jax.ShapeDtypeStruct((L, W), MXU)],
        compiler_params=_cp(("parallel",), 56),
    )(ya, yb, o2d, zc, gates, gates, gates, wa, wb, wc)


def _branch_bwd(dmg, ta, tb, tc, gates, *, name, tm=256):
    L = dmg.shape[0]
    tm = min(tm, L)
    D = D_MODEL

    def body(dm_ref, ta_ref, tb_ref, tc_ref, g0_ref, g1_ref, g2_ref, da_ref, db_ref, dc_ref, dg_ref):
        dm = dm_ref[...]
        for i, (t_ref, g_ref, d_ref) in enumerate(((ta_ref, g0_ref, da_ref), (tb_ref, g1_ref, db_ref),
                                                   (tc_ref, g2_ref, dc_ref))):
            sg = jax.nn.sigmoid(g_ref[...])
            d_ref[...] = (sg * dm).astype(MXU)
            dg_ref[:, i * D:(i + 1) * D] = (dm * t_ref[...] * sg * (1.0 - sg)).astype(MXU)

    wide = pl.BlockSpec((tm, D), lambda i: (i, 0))
    gate = lambda c: pl.BlockSpec((tm, D), lambda i, c=c: (i, c))
    bf = jax.ShapeDtypeStruct((L, D), MXU)
    return pl.pallas_call(
        body, name=name, grid=(L // tm,),
        in_specs=[wide, wide, wide, wide, gate(0), gate(1), gate(2)],
        out_specs=[wide, wide, wide, pl.BlockSpec((tm, 3 * D), lambda i: (i, 0))],
        out_shape=[bf, bf, bf, jax.ShapeDtypeStruct((L, 3 * D), MXU)],
        compiler_params=_cp(("parallel",), 56),
    )(dmg, ta, tb, tc, gates, gates, gates)


def _gate_c_bwd(dyc, o2d, zc, *, name, tm=256):
    L, W = dyc.shape
    tm = min(tm, L)

    def body(dy_ref, o_ref, z_ref, do_ref, dz_ref):
        _, vjp = jax.vjp(lambda o, z: o * _silu(z), o_ref[...], z_ref[...])
        do, dz = vjp(dy_ref[...])
        do_ref[...] = do
        dz_ref[...] = dz.astype(MXU)

    row = pl.BlockSpec((tm, W), lambda i: (i, 0))
    return pl.pallas_call(body, name=name, grid=(L // tm,), in_specs=[row, row, row], out_specs=[row, row],
                          out_shape=[jax.ShapeDtypeStruct((L, W), F32), jax.ShapeDtypeStruct((L, W), MXU)],
                          compiler_params=_cp(("parallel",)))(dyc, o2d, zc)


def _adamw(w, g, m, v, *, name):
    shape = w.shape
    cols = shape[-1]
    w2, g2, m2, v2 = (t.reshape(-1, cols) for t in (w, g, m, v))
    rows = w2.shape[0]
    tc = 1024 if cols % 1024 == 0 else cols
    lane_cols = -(-tc // 128) * 128
    tr = rows
    while tr % 16 == 0 and tr * lane_cols * 4 > 2 * _MB:
        tr //= 2

    def body(w_ref, g_ref, m_ref, v_ref, d_ref, nm_ref, nv_ref):
        gv = g_ref[...]
        nm = ADAM_B1 * m_ref[...] + (1.0 - ADAM_B1) * gv
        nv = ADAM_B2 * v_ref[...] + (1.0 - ADAM_B2) * jnp.square(gv)
        m_hat = nm / (1.0 - ADAM_B1 ** ADAM_STEP)
        v_hat = nv / (1.0 - ADAM_B2 ** ADAM_STEP)
        d_ref[...] = -ADAM_LR * (m_hat / (jnp.sqrt(v_hat) + ADAM_EPS) + ADAM_WD * w_ref[...])
        nm_ref[...] = nm
        nv_ref[...] = nv

    spec = pl.BlockSpec((tr, tc), lambda i, j: (i, j))
    out = jax.ShapeDtypeStruct((rows, cols), F32)
    d, nm, nv = pl.pallas_call(body, name=name, grid=(rows // tr, cols // tc), in_specs=[spec] * 4,
                               out_specs=[spec] * 3, out_shape=[out, out, out],
                               compiler_params=_cp(("parallel", "parallel")))(w2, g2, m2, v2)
    return d.reshape(shape), nm.reshape(shape), nv.reshape(shape)


def _adamw_layer(w, g, m, v, l, prev, *, name):
    _, rows, cols = w.shape
    tc = 1024 if cols % 1024 == 0 else cols
    tr = rows
    while tr % 16 == 0 and tr * tc * 4 > 2 * _MB:
        tr //= 2

    def body(w_ref, g_ref, m_ref, v_ref, *rest):
        go_ref, d_ref, nm_ref, nv_ref = rest[-4:]
        gv = g_ref[...]
        nm = ADAM_B1 * m_ref[...] + (1.0 - ADAM_B1) * gv
        nv = ADAM_B2 * v_ref[...] + (1.0 - ADAM_B2) * jnp.square(gv)
        m_hat = nm / (1.0 - ADAM_B1 ** ADAM_STEP)
        v_hat = nv / (1.0 - ADAM_B2 ** ADAM_STEP)
        d_ref[...] = -ADAM_LR * (m_hat / (jnp.sqrt(v_hat) + ADAM_EPS) + ADAM_WD * w_ref[...])
        nm_ref[...] = nm
        nv_ref[...] = nv
        go_ref[...] = gv

    lspec = pl.BlockSpec((None, tr, tc), lambda i, j: (l, i, j))
    gspec = pl.BlockSpec((tr, tc), lambda i, j: (i, j))
    out = jax.ShapeDtypeStruct(w.shape, F32)
    extra = [] if prev is None else list(prev)
    return pl.pallas_call(
        body, name=name, grid=(rows // tr, cols // tc),
        in_specs=[lspec, gspec, lspec, lspec] + [_ANY] * len(extra),
        out_specs=[lspec] * 4, out_shape=[out] * 4,
        input_output_aliases={4 + i: i for i in range(len(extra))},
        compiler_params=_cp(("parallel", "parallel")),
    )(w, g, m, v, *extra)


def _prep_layer(p, l, after=None):
    are = p["ssm_a_re"][l].reshape(SSM_CH, 1)
    aim = p["ssm_a_im"][l].reshape(SSM_CH, 1)
    ldt = p["ssm_log_dt"][l].reshape(1, SSM_GROUPS)
    bre = p["ssm_b_re"][l].reshape(SSM_CH, SSM_GROUP)
    bim = p["ssm_b_im"][l].reshape(SSM_CH, SSM_GROUP)
    lbr, lbi, bbr, bbi = _s5_params_fwd(are, aim, ldt, bre, bim, name=f"s5_params_fwd_{l}", after=after)
    cre = p["ssm_c_re"][l].transpose(0, 2, 1).reshape(SSM_CH, SSM_GROUP)
    cim = p["ssm_c_im"][l].transpose(0, 2, 1).reshape(SSM_CH, SSM_GROUP)
    return dict(raw=(are, aim, ldt, bre, bim),
                lbr=lbr.reshape(N_SLAB * SLAB_NC, 1, 128), lbi=lbi.reshape(N_SLAB * SLAB_NC, 1, 128),
                btr=_expand_bd(bbr), bti=_expand_bd(bbi), cbr=_expand_bd(cre), cbi=_expand_bd(cim),
                dvec=p["ssm_d"][l].reshape(1, SSM_WIDTH))


def _layer_fwd(x, h, p, sp, winT, rest_of, l, tabs, proj_after=None, after_main=None):
    L = x.shape[0]
    cosf, sinf, rot = tabs
    mm = functools.partial(_matmul, h, winT, "nt", tm=L, tn=256, tk=D_MODEL)
    main = mm(name=f"proj_main_{l}", shape=(L, N_MAIN, D_MODEL), after=proj_after)
    then = proj_after if after_main is None else after_main(main)
    zc = mm(name=f"proj_zc_{l}", shape=(L, N_ZC, D_MODEL), b_off=(N_MAIN // 256, 0), after=then)
    gates = mm(name=f"proj_gates_{l}", shape=(L, N_GATES, D_MODEL), b_off=((N_MAIN + N_ZC) // 256, 0), after=then)
    big, token = rest_of([main, zc, gates])
    ua = _time_interleave(main[:, :SSM_WIDTH])
    ys, sr, si = _s5_fwd(ua, sp["btr"], sp["bti"], sp["cbr"], sp["cbi"], sp["lbr"], sp["lbi"], sp["dvec"],
                         name=f"s5_fwd_{l}", after=token)
    ys = _time_deinterleave(ys)
    ya = _glu_fwd(ys, main, big["glu_w"], p["ssm_glu_b"][l], name=f"glu_fwd_{l}")
    yb = _sg_fwd(main, p["sg_ln_w"][l], p["sg_ln_b"][l], p["sg_w"][l], p["sg_b"][l], name=f"sg_fwd_{l}")
    qh = _to_heads(main[:, 5120:6144], ATT_HEADS)
    kh = _to_heads(main[:, 6144:6272], ATT_KV_HEADS)
    vh = _to_heads(main[:, 6272:6400], ATT_KV_HEADS)
    oh = _attn_fwd(qh, kh, vh, p["attn_sinks"][l], cosf, sinf, rot, name=f"attn_fwd_{l}")
    o2d = _from_heads(oh)
    mg, ta, tb, tc, yc = _branch_fwd(ya, yb, o2d, zc, gates, big["wbaT"], big["wbbT"], big["wbcT"],
                                     name=f"branch_fwd_{l}")
    xn = _matmul(mg, big["w_out"], "nn", name=f"out_fwd_{l}", shape=(L, D_MODEL, D_MODEL), tm=512, tn=512,
                 tk=D_MODEL, add=x)
    saved = dict(x=x, h=h, main=main, zc=zc, gates=gates, ua=ua, ys=ys, sr=sr, si=si, ya=ya, yb=yb, yc=yc, o2d=o2d,
                 qh=qh, kh=kh, vh=vh, mg=mg, ta=ta, tb=tb, tc=tc, sp=sp)
    return xn, saved, big


def _layer_bwd(dxn, s, p, big, l, tabs, early, mid):
    L = dxn.shape[0]
    D, W = D_MODEL, SSM_WIDTH
    cosf, sinf, rot = tabs
    sp = s["sp"]
    g = {}
    dmg = _matmul(dxn, big["w_out"], "nt", name=f"out_bwd_dm_{l}", shape=(L, D, D), tm=512, tn=512, tk=D)
    g["w_out"] = _matmul(s["mg"], dxn, "tn", name=f"out_bwd_dw_{l}", shape=(D, D, L), tm=512, tn=512, tk=L,
                         out_dtype=MXU)
    dta, dtb, dtc, dgates = _branch_bwd(dmg, s["ta"], s["tb"], s["tc"], s["gates"], name=f"branch_bwd_{l}")
    dys_ = {}
    for nm, dt, y, wt in (("a", dta, s["ya"], big["wbaT"]), ("b", dtb, s["yb"], big["wbbT"]),
                          ("c", dtc, s["yc"], big["wbcT"])):
        dys_[nm] = _matmul(dt, wt, "nn", name=f"branch_bwd_dy{nm}_{l}", shape=(L, W, D), tm=512, tn=512, tk=D)
        g["wb" + nm + "T"] = _matmul(dt, y, "tn", name=f"branch_bwd_dw{nm}_{l}", shape=(D, W, L),
                                     tm=512, tn=512, tk=L, out_dtype=MXU)
    do2d, dzc = _gate_c_bwd(dys_["c"], s["o2d"], s["zc"], name=f"gate_c_bwd_{l}")
    dqh, dkh, dvh, dsinks = _attn_bwd(_to_heads(do2d, ATT_HEADS), s["qh"], s["kh"], s["vh"], p["attn_sinks"][l],
                                      cosf, sinf, rot, name=f"attn_bwd_{l}")
    g["attn_sinks"] = dsinks.reshape(ATT_HEADS)
    dub, dvb, dzb, dlnw, dlnb, dsgw, dsgb = _sg_bwd(dys_["b"], s["main"], p["sg_ln_w"][l], p["sg_ln_b"][l],
                                                    p["sg_w"][l], p["sg_b"][l], name=f"sg_bwd_{l}")
    g["sg_ln_w"], g["sg_ln_b"] = dlnw.reshape(W), dlnb.reshape(W)
    g["sg_w"], g["sg_b"] = dsgw, dsgb.reshape(SG_HEADS, SG_CHUNK)
    dys, dza, a1, dzl, dgb = _glu_bwd(dys_["a"], s["ys"], s["main"], big["glu_w"], p["ssm_glu_b"][l],
                                      name=f"glu_bwd_{l}")
    g["ssm_glu_b"] = dgb.reshape(W)
    g["glu_w"] = _matmul(a1, dzl, "tn", name=f"glu_bwd_dw_{l}", shape=(W, W, L), tm=512, tn=512, tk=L, out_dtype=MXU)
    token = early(g)
    dua, dbtr, dbti, dcbr, dcbi, dlr, dli, dd = _s5_bwd(_time_interleave(dys), s["ua"], s["sr"], s["si"], sp["btr"],
                                                        sp["bti"], sp["cbr"], sp["cbi"], sp["lbr"], sp["lbi"],
                                                        sp["dvec"], name=f"s5_bwd_{l}", after=token)
    token = mid(dua)
    dua = _time_deinterleave(dua)
    g["ssm_d"] = dd.reshape(W)
    to_c = lambda t: _contract_bd(t).reshape(SSM_GROUPS, SSM_STATE, SSM_GROUP).transpose(0, 2, 1)
    g["ssm_c_re"], g["ssm_c_im"] = to_c(dcbr), to_c(dcbi)
    dare, daim, dldt, dbre, dbim = _s5_params_bwd(*sp["raw"], dlr.reshape(SSM_CH, 1), dli.reshape(SSM_CH, 1),
                                                  _contract_bd(dbtr), _contract_bd(dbti),
                                                  name=f"s5_params_bwd_{l}")
    g["ssm_a_re"] = dare.reshape(SSM_GROUPS, SSM_STATE)
    g["ssm_a_im"] = daim.reshape(SSM_GROUPS, SSM_STATE)
    g["ssm_log_dt"] = dldt.reshape(SSM_GROUPS)
    g["ssm_b_re"] = dbre.reshape(SSM_GROUPS, SSM_STATE, SSM_GROUP)
    g["ssm_b_im"] = dbim.reshape(SSM_GROUPS, SSM_STATE, SSM_GROUP)
    dproj = jnp.concatenate([t.astype(MXU) for t in (dua, dza, dub, dvb, dzb, _from_heads(dqh), _from_heads(dkh),
                                                     _from_heads(dvh), dzc, dgates)], axis=1)
    g["winT"] = _matmul(dproj, s["h"], "tn", name=f"proj_bwd_dw_{l}", shape=(D_IN, D, L), tm=256, tn=D, tk=L,
                        out_dtype=MXU, after=token)
    return dproj, g


def _proj_bwd_dh(dproj, winT, l, after):
    return _matmul(dproj, winT, "nn", name=f"proj_bwd_dh_{l}", shape=(dproj.shape[0], D_MODEL, D_IN), tm=512, tn=512,
                   tk=D_IN // 2, after=after)


MESH = pl.DeviceIdType.MESH
_ANY = pl.BlockSpec(memory_space=pl.ANY)
ROW_ALIGN = 16


def _coords():
    return lax.axis_index("x"), lax.axis_index("y"), lax.axis_index("c")


def _gather8(arrs, *, name):
    n = len(arrs)
    rows = [a.shape[0] for a in arrs]
    for r in rows:
        assert r % ROW_ALIGN == 0

    def body(*refs):
        ins, outs = refs[:n], refs[n:2 * n]
        send, recv, lsem = refs[2 * n:]
        x, y, c = _coords()
        me, sibling = (x, y, c), (x, y, 1 - c)
        chips = [(1 - x, y), (x, 1 - y), (1 - x, 1 - y)]

        def blk(a, px, py, pc):
            return outs[a].at[pl.ds(pl.multiple_of((4 * px + 2 * py + pc) * rows[a], ROW_ALIGN), rows[a]), :]

        def own(a):
            return ins[a]

        def copy(a, k, block, to, src=None):
            return pltpu.make_async_remote_copy(
                src_ref=blk(a, *block) if src is None else src, dst_ref=blk(a, *block),
                send_sem=send.at[a, k], recv_sem=recv.at[a, k], device_id=to, device_id_type=MESH)

        mine, first, passed = [], [], []
        for a in range(n):
            mine.append(pltpu.make_async_copy(own(a), blk(a, *me), lsem.at[a]))
            mine[a].start()
            f = [copy(a, 0, me, sibling, src=own(a))]
            f += [copy(a, 1 + j, me, (*chip, c), src=own(a)) for j, chip in enumerate(chips)]
            for cp in f:
                cp.start()
            first.append(f)
        for a in range(n):
            ps = [copy(a, 4 + j, (*chip, c), sibling) for j, chip in enumerate(chips)]
            for j, chip in enumerate(chips):
                copy(a, 1 + j, (*chip, c), me).wait_recv()
                ps[j].start()
            passed.append(ps)
        for a in range(n):
            copy(a, 0, sibling, me).wait_recv()
            for j, chip in enumerate(chips):
                copy(a, 4 + j, (*chip, 1 - c), me).wait_recv()
            for cp in first[a] + passed[a]:
                cp.wait_send()
            mine[a].wait()

    return pl.pallas_call(
        body, name=name,
        in_specs=[_ANY] * n, out_specs=[_ANY] * n,
        out_shape=[jax.ShapeDtypeStruct((8 * r,) + a.shape[1:], a.dtype) for r, a in zip(rows, arrs)],
        scratch_shapes=[pltpu.SemaphoreType.DMA((n, 7)), pltpu.SemaphoreType.DMA((n, 7)), pltpu.SemaphoreType.DMA((n,))],
    )(*arrs)


def _sibling_swap(arrs, *, name):
    n = len(arrs)

    def body(*refs):
        ins, outs = refs[:n], refs[n:2 * n]
        send, recv = refs[2 * n:]
        x, y, c = _coords()
        cps = [pltpu.make_async_remote_copy(src_ref=ins[a].at[:, 1 - c], dst_ref=outs[a], send_sem=send.at[a],
                                            recv_sem=recv.at[a], device_id=(x, y, 1 - c), device_id_type=MESH)
               for a in range(n)]
        for cp in cps:
            cp.start()
        for cp in cps:
            cp.wait_recv()
        for cp in cps:
            cp.wait_send()

    return pl.pallas_call(
        body, name=name, in_specs=[_ANY] * n, out_specs=[_ANY] * n,
        out_shape=[jax.ShapeDtypeStruct((a.shape[0],) + a.shape[2:], a.dtype) for a in arrs],
        scratch_shapes=[pltpu.SemaphoreType.DMA((n,)), pltpu.SemaphoreType.DMA((n,))],
    )(*arrs)


def _col_tile(lead, rows, cols, itemsize=4, cap=4 * _MB):
    tc = cols
    while tc % 256 == 0 and lead * rows * tc * itemsize > cap:
        tc //= 2
    return tc


def _pair_sum(mine, theirs, *, name):
    _, _, rows, cols = mine.shape
    tc = _col_tile(1, rows, cols)
    c = lax.axis_index("c")

    def body(c_ref, a_ref, b_ref, o_ref):
        o_ref[...] = (a_ref[...].astype(F32) + b_ref[...].astype(F32)).astype(MXU)

    return pl.pallas_call(
        body, name=name,
        grid_spec=pltpu.PrefetchScalarGridSpec(
            num_scalar_prefetch=1, grid=(4, cols // tc),
            in_specs=[pl.BlockSpec((None, None, rows, tc), lambda j, i, cr: (j, cr[0], 0, i)),
                      pl.BlockSpec((None, rows, tc), lambda j, i, cr: (j, 0, i))],
            out_specs=pl.BlockSpec((None, rows, tc), lambda j, i, cr: (j, 0, i))),
        out_shape=jax.ShapeDtypeStruct((4, rows, cols), MXU),
        compiler_params=_cp(("parallel", "parallel")),
    )(c.reshape(1).astype(jnp.int32), mine, theirs)


_HBM = pl.BlockSpec(memory_space=pltpu.HBM)
_SEM = pl.BlockSpec(memory_space=pltpu.SEMAPHORE)
_EFFECT = pltpu.SideEffectType.DATAFLOW_SIDE_EFFECTING
N_PEER_CHIPS = 3


def _peer_chips(x, y):
    return [(1 - x, y), (x, 1 - y), (1 - x, 1 - y)]


def _split_start(srcs, lands, src_slot, dst_slot, *, name, after=()):
    n = len(srcs)
    ns = n * N_PEER_CHIPS
    first = 2 * n + len(after)

    def body(*refs):
        src_refs, land_refs = refs[:n], refs[n:2 * n]
        send, recv, token = refs[first:first + ns], refs[first + ns:first + 2 * ns], refs[-1]
        x, y, c = _coords()
        for a in range(n):
            for k, (px, py) in enumerate(_peer_chips(x, y)):
                pltpu.make_async_remote_copy(
                    src_ref=src_refs[a].at[src_slot(x, y, c, px, py)], dst_ref=land_refs[a].at[dst_slot(x, y, c)],
                    send_sem=send[a * N_PEER_CHIPS + k], recv_sem=recv[a * N_PEER_CHIPS + k],
                    device_id=(px, py, c), device_id_type=MESH).start()
        token[...] = jnp.zeros_like(token)

    bufs = list(srcs) + list(lands)
    res = pl.pallas_call(
        body, name=name,
        out_shape=(*[pltpu.SemaphoreType.DMA(())] * (2 * ns), *[pltpu.HBM(b.shape, b.dtype) for b in bufs],
                   jax.ShapeDtypeStruct((8, 128), F32)),
        in_specs=[_HBM] * (2 * n) + [_ANY] * len(after),
        out_specs=(*[_SEM] * (2 * ns), *[_HBM] * (2 * n), pl.BlockSpec(memory_space=pltpu.VMEM)),
        input_output_aliases={i: 2 * ns + i for i in range(2 * n)},
        compiler_params=pltpu.CompilerParams(has_side_effects=_EFFECT),
    )(*[pltpu.with_memory_space_constraint(b, pltpu.HBM) for b in bufs], *after)
    sems = list(res[:2 * ns])
    return sems, list(res[2 * ns:2 * ns + n]), list(res[2 * ns + n:2 * ns + 2 * n]), res[-1]


def _split_wait(sems, srcs, lands, after, *, name):
    n = len(srcs)
    ns = n * N_PEER_CHIPS

    def body(*refs):
        src_refs, land_refs = refs[:n], refs[n:2 * n]
        send, recv = refs[2 * n:2 * n + ns], refs[2 * n + ns:2 * n + 2 * ns]
        x, y, c = _coords()
        for a in range(n):
            for k in range(N_PEER_CHIPS):
                cp = pltpu.make_async_remote_copy(
                    src_ref=src_refs[a].at[0], dst_ref=land_refs[a].at[0], send_sem=send[a * N_PEER_CHIPS + k],
                    recv_sem=recv[a * N_PEER_CHIPS + k], device_id=(x, y, 1 - c), device_id_type=MESH)
                cp.wait_send()
                cp.wait_recv()

    bufs = list(srcs) + list(lands)
    res = pl.pallas_call(
        body, name=name,
        out_shape=tuple(pltpu.HBM(b.shape, b.dtype) for b in bufs),
        in_specs=[_HBM] * (2 * n) + [_SEM] * (2 * ns) + [_ANY] * len(after),
        out_specs=tuple([_HBM] * (2 * n)),
        input_output_aliases={i: i for i in range(2 * n)},
        compiler_params=pltpu.CompilerParams(has_side_effects=_EFFECT),
    )(*bufs, *sems, *after)
    return list(res[:n]), list(res[n:])


def _fill_own(shard2, *, name, after=None):
    _, rows, cols = shard2.shape
    tc = _col_tile(1, rows, cols, itemsize=shard2.dtype.itemsize)
    j = 2 * lax.axis_index("x") + lax.axis_index("y")
    extra = [] if after is None else [after]

    def body(j_ref, s_ref, *rest):
        rest[-1][...] = s_ref[...]

    return pl.pallas_call(
        body, name=name,
        grid_spec=pltpu.PrefetchScalarGridSpec(
            num_scalar_prefetch=1, grid=(2, cols // tc),
            in_specs=([pl.BlockSpec((None, rows, tc), lambda h, i, jr: (h, 0, i))]
                      + [pl.BlockSpec(memory_space=pl.ANY)] * len(extra)),
            out_specs=pl.BlockSpec((None, rows, tc), lambda h, i, jr: (2 * jr[0] + h, 0, i))),
        out_shape=jax.ShapeDtypeStruct((8, rows, cols), shard2.dtype),
        compiler_params=_cp(("parallel", "parallel")),
    )(j.reshape(1).astype(jnp.int32), shard2, *extra)


def _pass_to_sibling(lands, *, name):
    n = len(lands)

    def body(*refs):
        outs = refs[n:2 * n]
        send, recv = refs[2 * n:]
        x, y, c = _coords()
        cps = []
        for a in range(n):
            for k, (px, py) in enumerate(_peer_chips(x, y)):
                slot = 4 * px + 2 * py + c
                cps.append(pltpu.make_async_remote_copy(
                    src_ref=outs[a].at[slot], dst_ref=outs[a].at[slot], send_sem=send.at[a, k], recv_sem=recv.at[a, k],
                    device_id=(x, y, 1 - c), device_id_type=MESH))
        for cp in cps:
            cp.start()
        for cp in cps:
            cp.wait_recv()
        for cp in cps:
            cp.wait_send()

    return pl.pallas_call(
        body, name=name, in_specs=[_ANY] * n, out_specs=[_ANY] * n,
        out_shape=[jax.ShapeDtypeStruct(b.shape, b.dtype) for b in lands],
        input_output_aliases={a: a for a in range(n)},
        scratch_shapes=[pltpu.SemaphoreType.DMA((n, N_PEER_CHIPS)), pltpu.SemaphoreType.DMA((n, N_PEER_CHIPS))],
    )(*lands)


def _sum_parts(parts, got, *, name):
    _, rows, cols = parts.shape
    tc = _col_tile(4, rows, cols, itemsize=parts.dtype.itemsize)
    x, y, c = _coords()
    idx = jnp.stack([2 * x + y, 2 * (1 - x) + y, 2 * x + (1 - y), 2 * (1 - x) + (1 - y), c]).astype(jnp.int32)

    def body(i_ref, p_ref, g0_ref, g1_ref, g2_ref, o_ref):
        o_ref[...] = ((p_ref[...].astype(F32) + g0_ref[...].astype(F32)) + g1_ref[...].astype(F32)) + g2_ref[...].astype(F32)

    slot = lambda s: pl.BlockSpec((None, rows, tc), lambda i, ir, s=s: (ir[s], 0, i))
    return pl.pallas_call(
        body, name=name,
        grid_spec=pltpu.PrefetchScalarGridSpec(
            num_scalar_prefetch=1, grid=(cols // tc,),
            in_specs=[slot(0), slot(1), slot(2), slot(3)],
            out_specs=pl.BlockSpec((None, rows, tc), lambda i, ir: (ir[4], 0, i))),
        out_shape=jax.ShapeDtypeStruct((2, rows, cols), F32),
        compiler_params=_cp(("parallel",)),
    )(idx, parts, got, got, got)


def _sum_slots(t, *, name):
    S, rows, cols = t.shape
    tc = _col_tile(S, rows, cols)

    def body(t_ref, o_ref):
        acc = t_ref[0].astype(F32)
        for s in range(1, S):
            acc = acc + t_ref[s].astype(F32)
        o_ref[...] = acc

    return pl.pallas_call(
        body, name=name, grid=(cols // tc,),
        in_specs=[pl.BlockSpec((S, rows, tc), lambda i: (0, 0, i))],
        out_specs=pl.BlockSpec((rows, tc), lambda i: (0, i)),
        out_shape=jax.ShapeDtypeStruct((rows, cols), F32),
        compiler_params=_cp(("parallel",)),
    )(t)


def _halves_join(bufs, *, name):
    n = len(bufs)

    def body(*refs):
        outs = refs[n:2 * n]
        send, recv = refs[2 * n:]
        x, y, c = _coords()
        cps = [pltpu.make_async_remote_copy(src_ref=outs[a].at[c], dst_ref=outs[a].at[c], send_sem=send.at[a],
                                            recv_sem=recv.at[a], device_id=(x, y, 1 - c), device_id_type=MESH)
               for a in range(n)]
        for cp in cps:
            cp.start()
        for cp in cps:
            cp.wait_recv()
        for cp in cps:
            cp.wait_send()

    return pl.pallas_call(
        body, name=name, in_specs=[_ANY] * n, out_specs=[_ANY] * n,
        out_shape=[jax.ShapeDtypeStruct(b.shape, b.dtype) for b in bufs],
        input_output_aliases={a: a for a in range(n)},
        scratch_shapes=[pltpu.SemaphoreType.DMA((n,)), pltpu.SemaphoreType.DMA((n,))],
    )(*bufs)


def _swap_start(srcs, *, name):
    n = len(srcs)
    lands = [lax.empty((s.shape[0],) + s.shape[2:], s.dtype) for s in srcs]

    def body(*refs):
        src_refs, land_refs = refs[:n], refs[n:2 * n]
        send, recv, token = refs[2 * n:3 * n], refs[3 * n:4 * n], refs[-1]
        x, y, c = _coords()
        for a in range(n):
            pltpu.make_async_remote_copy(src_ref=src_refs[a].at[:, 1 - c], dst_ref=land_refs[a], send_sem=send[a],
                                         recv_sem=recv[a], device_id=(x, y, 1 - c), device_id_type=MESH).start()
        token[...] = jnp.zeros_like(token)

    bufs = list(srcs) + lands
    res = pl.pallas_call(
        body, name=name,
        out_shape=(*[pltpu.SemaphoreType.DMA(())] * (2 * n), *[pltpu.HBM(b.shape, b.dtype) for b in bufs],
                   jax.ShapeDtypeStruct((8, 128), F32)),
        in_specs=[_HBM] * (2 * n),
        out_specs=(*[_SEM] * (2 * n), *[_HBM] * (2 * n), pl.BlockSpec(memory_space=pltpu.VMEM)),
        input_output_aliases={i: 2 * n + i for i in range(2 * n)},
        compiler_params=pltpu.CompilerParams(has_side_effects=_EFFECT),
    )(*[pltpu.with_memory_space_constraint(b, pltpu.HBM) for b in bufs])
    return list(res[:2 * n]), list(res[2 * n:3 * n]), list(res[3 * n:4 * n]), res[-1]


def _swap_wait(sems, srcs, lands, after, *, name):
    n = len(srcs)

    def body(*refs):
        src_refs, land_refs = refs[:n], refs[n:2 * n]
        send, recv = refs[2 * n:3 * n], refs[3 * n:4 * n]
        x, y, c = _coords()
        for a in range(n):
            cp = pltpu.make_async_remote_copy(
                src_ref=src_refs[a].at[:, 0], dst_ref=land_refs[a], send_sem=send[a], recv_sem=recv[a],
                device_id=(x, y, 1 - c), device_id_type=MESH)
            cp.wait_send()
            cp.wait_recv()

    bufs = list(srcs) + list(lands)
    res = pl.pallas_call(
        body, name=name,
        out_shape=tuple(pltpu.HBM(b.shape, b.dtype) for b in bufs),
        in_specs=[_HBM] * (2 * n) + [_SEM] * (2 * n) + [_ANY] * len(after),
        out_specs=tuple([_HBM] * (2 * n)),
        input_output_aliases={i: i for i in range(2 * n)},
        compiler_params=pltpu.CompilerParams(has_side_effects=_EFFECT),
    )(*bufs, *sems, *after)
    return list(res[:n]), list(res[n:])


def _grad_views(grads):
    return [g.reshape(4, 2, g.shape[0] // 8, g.shape[1]) for g in grads]


def _scatter_begin(views, theirs, *, tag):
    parts = [_pair_sum(v, t, name=f"rs_pair_{tag}_{i}") for i, (v, t) in enumerate(zip(views, theirs))]
    got = [lax.empty(p.shape, p.dtype) for p in parts]
    sems, parts, got, token = _split_start(
        parts, got, lambda x, y, c, px, py: 2 * px + py, lambda x, y, c: 2 * x + y, name=f"rs_start_{tag}")
    return (sems, parts, got), token


def _reduce_scatter_begin(grads, *, tag):
    views = _grad_views(grads)
    theirs = _sibling_swap(views, name=f"rs_swap_{tag}")
    return _scatter_begin(views, theirs, tag=tag)


def _reduce_scatter_end(state, after, *, tag):
    sems, parts, got = state
    parts, got = _split_wait(sems, parts, got, after, name=f"rs_wait_{tag}")
    halves = [_sum_parts(p, t, name=f"rs_sum_{tag}_{i}") for i, (p, t) in enumerate(zip(parts, got))]
    joined = _halves_join(halves, name=f"rs_join_{tag}")
    return [j.reshape(2 * j.shape[1], j.shape[2]) for j in joined]


_SMALL = ("norm_w", "ssm_a_re", "ssm_a_im", "ssm_log_dt", "ssm_b_re", "ssm_b_im", "ssm_c_re", "ssm_c_im", "ssm_d",
          "ssm_glu_b", "sg_ln_w", "sg_ln_b", "sg_w", "sg_b", "attn_sinks", "final_norm_w")
_BIG = ("w_in", "ssm_glu_w", "w_branch_a", "w_branch_b", "w_branch_c", "w_out")
_WEIGHTS = ("norm_w", "w_in", "ssm_a_re", "ssm_a_im", "ssm_log_dt", "ssm_b_re", "ssm_b_im", "ssm_c_re", "ssm_c_im",
            "ssm_d", "ssm_glu_w", "ssm_glu_b", "sg_ln_w", "sg_ln_b", "sg_w", "sg_b", "attn_sinks", "w_branch_a",
            "w_branch_b", "w_branch_c", "w_out", "final_norm_w")
_PACK_COLS = 1024
_PACK_ALIGN = 8 * ROW_ALIGN * _PACK_COLS


def _slice_exchange(buf, *, name, after=()):
    def body(in_ref, *rest):
        out_ref, send, recv, lsem = rest[-4:]
        x, y, c = _coords()
        me = 4 * x + 2 * y + c
        own = pltpu.make_async_copy(in_ref.at[me], out_ref.at[me], lsem)
        own.start()
        cps = []
        for k in range(1, 8):
            px, py, pc = x ^ (k >> 2), y ^ ((k >> 1) & 1), c ^ (k & 1)
            cps.append(pltpu.make_async_remote_copy(
                src_ref=in_ref.at[4 * px + 2 * py + pc], dst_ref=out_ref.at[me], send_sem=send.at[k - 1],
                recv_sem=recv.at[k - 1], device_id=(px, py, pc), device_id_type=MESH))
        for cp in cps:
            cp.start()
        for cp in cps:
            cp.wait_recv()
        for cp in cps:
            cp.wait_send()
        own.wait()

    return pl.pallas_call(
        body, name=name, in_specs=[_ANY] * (1 + len(after)), out_specs=_ANY,
        out_shape=jax.ShapeDtypeStruct(buf.shape, buf.dtype),
        scratch_shapes=[pltpu.SemaphoreType.DMA((7,)), pltpu.SemaphoreType.DMA((7,)), pltpu.SemaphoreType.DMA],
    )(buf, *after)


def _allreduce_small(packed, after=()):
    rows, cols = packed.shape
    got = _slice_exchange(packed.reshape(8, rows // 8, cols), name="small_grads_exchange", after=after)
    mine = _sum_slots(got, name="small_grads_sum")
    return _gather8([mine], name="small_grads_gather")[0]


def _pack(ts):
    flat = jnp.concatenate([t.reshape(-1) for t in ts])
    pad = (-flat.shape[0]) % _PACK_ALIGN
    return jnp.pad(flat, (0, pad)).reshape(-1, _PACK_COLS)


def _unpack(buf, like):
    flat = buf.reshape(-1)
    out, pos = [], 0
    for t in like:
        out.append(flat[pos:pos + t.size].reshape(t.shape))
        pos += t.size
    return out


def kernel(x, norm_w, w_in, ssm_a_re, ssm_a_im, ssm_log_dt, ssm_b_re, ssm_b_im, ssm_c_re, ssm_c_im, ssm_d, ssm_glu_w, ssm_glu_b, sg_ln_w, sg_ln_b, sg_w, sg_b, attn_sinks, w_branch_a, w_branch_b, w_branch_c, w_out, final_norm_w, loss_target, m_norm_w, m_w_in, m_ssm_a_re, m_ssm_a_im, m_ssm_log_dt, m_ssm_b_re, m_ssm_b_im, m_ssm_c_re, m_ssm_c_im, m_ssm_d, m_ssm_glu_w, m_ssm_glu_b, m_sg_ln_w, m_sg_ln_b, m_sg_w, m_sg_b, m_attn_sinks, m_w_branch_a, m_w_branch_b, m_w_branch_c, m_w_out, m_final_norm_w, v_norm_w, v_w_in, v_ssm_a_re, v_ssm_a_im, v_ssm_log_dt, v_ssm_b_re, v_ssm_b_im, v_ssm_c_re, v_ssm_c_im, v_ssm_d, v_ssm_glu_w, v_ssm_glu_b, v_sg_ln_w, v_sg_ln_b, v_sg_w, v_sg_b, v_attn_sinks, v_w_branch_a, v_w_branch_b, v_w_branch_c, v_w_out, v_final_norm_w):
    w = dict(norm_w=norm_w, w_in=w_in, ssm_a_re=ssm_a_re, ssm_a_im=ssm_a_im, ssm_log_dt=ssm_log_dt, ssm_b_re=ssm_b_re,
             ssm_b_im=ssm_b_im, ssm_c_re=ssm_c_re, ssm_c_im=ssm_c_im, ssm_d=ssm_d, ssm_glu_w=ssm_glu_w,
             ssm_glu_b=ssm_glu_b, sg_ln_w=sg_ln_w, sg_ln_b=sg_ln_b, sg_w=sg_w, sg_b=sg_b, attn_sinks=attn_sinks,
             w_branch_a=w_branch_a, w_branch_b=w_branch_b, w_branch_c=w_branch_c, w_out=w_out,
             final_norm_w=final_norm_w)
    m = dict(norm_w=m_norm_w, w_in=m_w_in, ssm_a_re=m_ssm_a_re, ssm_a_im=m_ssm_a_im, ssm_log_dt=m_ssm_log_dt,
             ssm_b_re=m_ssm_b_re, ssm_b_im=m_ssm_b_im, ssm_c_re=m_ssm_c_re, ssm_c_im=m_ssm_c_im, ssm_d=m_ssm_d,
             ssm_glu_w=m_ssm_glu_w, ssm_glu_b=m_ssm_glu_b, sg_ln_w=m_sg_ln_w, sg_ln_b=m_sg_ln_b, sg_w=m_sg_w,
             sg_b=m_sg_b, attn_sinks=m_attn_sinks, w_branch_a=m_w_branch_a, w_branch_b=m_w_branch_b,
             w_branch_c=m_w_branch_c, w_out=m_w_out, final_norm_w=m_final_norm_w)
    v = dict(norm_w=v_norm_w, w_in=v_w_in, ssm_a_re=v_ssm_a_re, ssm_a_im=v_ssm_a_im, ssm_log_dt=v_ssm_log_dt,
             ssm_b_re=v_ssm_b_re, ssm_b_im=v_ssm_b_im, ssm_c_re=v_ssm_c_re, ssm_c_im=v_ssm_c_im, ssm_d=v_ssm_d,
             ssm_glu_w=v_ssm_glu_w, ssm_glu_b=v_ssm_glu_b, sg_ln_w=v_sg_ln_w, sg_ln_b=v_sg_ln_b, sg_w=v_sg_w,
             sg_b=v_sg_b, attn_sinks=v_attn_sinks, w_branch_a=v_w_branch_a, w_branch_b=v_w_branch_b,
             w_branch_c=v_w_branch_c, w_out=v_w_out, final_norm_w=v_final_norm_w)

    big_names = ("winT", "glu_w", "wbaT", "wbbT", "wbcT", "w_out")
    L = x.shape[1]
    tabs = _rope_tables(L)
    p = {k: w[k] for k in _SMALL}

    column_sharded = ("w_in", "w_branch_a", "w_branch_b", "w_branch_c")

    def shard_halves(ws, names=_BIG):
        ts = [(t.T if k in column_sharded else t).astype(MXU) for k, t in zip(names, ws)]
        return [t.reshape(2, t.shape[0] // 2, t.shape[1]) for t in ts]

    my_half = lambda x_, y_, c_, px, py: c_
    my_block = lambda x_, y_, c_: 4 * x_ + 2 * y_ + c_
    rows_of = lambda lands: [t.reshape(8 * t.shape[1], t.shape[2]) for t in lands]
    fill = lambda l, hs, i0=0: [_fill_own(s, name=f"gather_fill_{l}_{i0 + i}") for i, s in enumerate(hs)]
    saved = [None] * DEPTH

    halves = [shard_halves([w[k][l] for k in _BIG]) for l in range(DEPTH)]
    w_in_0 = halves[0][:1]
    sems_a, src_a, land_a, token_a = _split_start(w_in_0, fill(0, w_in_0), my_half, my_block, name="gather_start_0a")
    fill_after = lambda l, hs, i0: [_fill_own(s, name=f"gather_fill_{l}_{i0 + i}", after=token_a)
                                    for i, s in enumerate(hs)]
    lands = [[None] + fill_after(0, halves[0][1:], 1), fill_after(1, halves[1], 0)]
    sp = [_prep_layer(p, l, after=token_a) for l in range(DEPTH)]
    h0 = _rms_fwd(x[0], p["norm_w"][0], name="rms_fwd_0", after=token_a)
    _, land_a = _split_wait(sems_a, src_a, land_a, [h0, sp[0]["btr"], sp[1]["btr"]] + lands[0][1:] + lands[1],
                            name="gather_wait_0a")
    land_a = _pass_to_sibling(land_a, name="gather_pass_0a")
    sems_b, src_b, land_b, token_b = _split_start(halves[0][1:], lands[0][1:], my_half, my_block,
                                                  name="gather_start_0b", after=land_a)
    split1 = {}

    def start1(main):
        split1["sems"], split1["src"], split1["land"], token1 = _split_start(
            halves[1][:1], lands[1][:1], my_half, my_block, name="gather_start_1a", after=[main])
        return token1

    def rest(l, sems, src, land, first):
        def arrived(t):
            _, got = _split_wait(sems, src, land, t, name=f"gather_wait_{l}b")
            got = _pass_to_sibling(got, name=f"gather_pass_{l}b")
            return dict(zip(big_names, rows_of(first + got))), None
        return arrived

    x1, saved[0], big0 = _layer_fwd(x[0], h0, p, sp[0], rows_of(land_a)[0], rest(0, sems_b, src_b, land_b, land_a), 0,
                                    tabs, proj_after=token_b, after_main=start1)
    _, land_1a = _split_wait(split1["sems"], split1["src"], split1["land"], [x1], name="gather_wait_1a")
    land_1a = _pass_to_sibling(land_1a, name="gather_pass_1a")
    sems_1b, src_1b, land_1b, token_1b = _split_start(halves[1][1:], lands[1][1:], my_half, my_block,
                                                      name="gather_start_1b", after=land_1a)
    h1 = _rms_fwd(x1, p["norm_w"][1], name="rms_fwd_1")
    x2, saved[1], big1 = _layer_fwd(x1, h1, p, sp[1], rows_of(land_1a)[0], rest(1, sems_1b, src_1b, land_1b, land_1a),
                                    1, tabs, proj_after=token_1b)
    bigs = [big0, big1]
    loss, dx, dfw = _final_loss(x2, p["final_norm_w"], loss_target[0], name="final_loss")

    grads = [None] * DEPTH
    rs = {}

    def early(l):
        def begin(g):
            *rs[f"{l}a"], token_a = _swap_start(_grad_views([g[k] for k in big_names[1:]]), name=f"rs_swap_start_{l}a")
            return token_a
        return begin

    def mid(l):
        def go_on(t):
            sems, views, lands = rs[f"{l}a"]
            views, theirs = _swap_wait(sems, views, lands, [t], name=f"rs_swap_wait_{l}a")
            rs[f"{l}a"], token_a = _scatter_begin(views, theirs, tag=f"{l}a")
            return token_a
        return go_on

    def late(l, dproj, dx):
        if l == 0:
            rs["0b"], token_s = _reduce_scatter_begin([grads[0]["winT"]], tag="0b")
            dh = _proj_bwd_dh(dproj, bigs[0]["winT"], 0, token_s)
            return _rms_bwd(saved[0]["x"], p["norm_w"][0], dh, dx, name="rms_bwd_0")
        sems, views, lands, token_b = _swap_start(_grad_views([grads[l]["winT"]]), name=f"rs_swap_start_{l}b")
        dh = _proj_bwd_dh(dproj, bigs[l]["winT"], l, token_b)
        views, theirs = _swap_wait(sems, views, lands, [dh], name=f"rs_swap_wait_{l}b")
        rs[f"{l}b"], token_s = _scatter_begin(views, theirs, tag=f"{l}b")
        return _rms_bwd(saved[l]["x"], p["norm_w"][l], dh, dx, name=f"rms_bwd_{l}", after=token_s)

    def reduced(l, after):
        return _reduce_scatter_end(rs[f"{l}b"], after, tag=f"{l}b") + _reduce_scatter_end(rs[f"{l}a"], after, tag=f"{l}a")

    dproj, grads[1] = _layer_bwd(dx, saved[1], p, bigs[1], 1, tabs, early(1), mid(1))
    dx, grads[1]["norm_w"] = late(1, dproj, dx)
    dproj, grads[0] = _layer_bwd(dx, saved[0], p, bigs[0], 0, tabs, early(0), mid(0))
    dx, grads[0]["norm_w"] = late(0, dproj, dx)
    red1 = reduced(1, [dx])

    tr = lambda t: t.transpose(0, 2, 1)
    view = {k: (tr if k == "w_in" else (lambda t: t)) for k in _BIG}
    shard_grads = lambda red: dict(zip(_BIG, (red[0], red[1], red[2].T, red[3].T, red[4].T, red[5])))
    outs = {k: None for k in _BIG}

    def adamw_big(l, red):
        for k, g in shard_grads(red).items():
            outs[k] = _adamw_layer(view[k](w[k]), g, view[k](m[k]), view[k](v[k]), l, outs[k], name=f"adamw_{k}_{l}")

    adamw_big(1, red1)

    small_like = [w[k] for k in _SMALL]
    gs = [jnp.stack([grads[l][k] for l in range(DEPTH)]) if k != "final_norm_w" else dfw for k in _SMALL]
    gsum = _allreduce_small(_pack(gs + [loss.reshape(1)]), after=[outs[k][0] for k in _BIG])
    adamw_big(0, reduced(0, [gsum]))

    gfull, delta, new_m, new_v = {}, {}, {}, {}
    for k in _BIG:
        gfull[k], delta[k], new_m[k], new_v[k] = (view[k](t) for t in outs[k])
    *small_sums, loss = _unpack(gsum, small_like + [loss])
    for k, t in zip(_SMALL, small_sums):
        gfull[k] = t
        delta[k], new_m[k], new_v[k] = _adamw(w[k], t, m[k], v[k], name=f"adamw_{k}")

    return (loss, dx[None], *[gfull[k] for k in _WEIGHTS], *[delta[k] for k in _WEIGHTS],
            *[new_m[k] for k in _WEIGHTS], *[new_v[k] for k in _WEIGHTS])
```

```python
import functools
import math

import numpy as np
import jax
import jax.numpy as jnp
from jax import lax
from jax.experimental import pallas as pl
from jax.experimental.pallas import tpu as pltpu

F32 = jnp.float32
MXU = jnp.bfloat16
HIGHEST = lax.Precision.HIGHEST

D_MODEL = 2048
DEPTH = 2
EPS = 1e-6
NEG_INF = -1e30
SSM_WIDTH = 1024
SSM_GROUP = 16
SSM_GROUPS = 64
SSM_STATE = 64
SSM_CH = SSM_GROUPS * SSM_STATE
SLAB = 128
SLAB_CH = (SLAB // SSM_GROUP) * SSM_STATE
N_SLAB = SSM_WIDTH // SLAB
SCAN_SEG = 8
SCAN_STEPS = 4
SG_HEADS = 8
SG_CHUNK = 128
HEAD_DIM = 64
ATT_HEADS = 16
ATT_KV_HEADS = 2
GQA_GROUP = 8
ATT_BLOCK = 128
WINDOW = 128
ROT_DIM = 16
ROPE_THETA = 500000.0
N_MAIN = 6400
N_ZC = 1024
N_GATES = 6144
D_IN = N_MAIN + N_ZC + N_GATES

ADAM_LR = 0.001
ADAM_B1 = 0.9
ADAM_B2 = 0.999
ADAM_EPS = 1e-08
ADAM_WD = 0.01
ADAM_STEP = 10

_DIMS = {"nn": (((1,), (0,)), ((), ())), "nt": (((1,), (1,)), ((), ())), "tn": (((0,), (0,)), ((), ()))}
_MB = 1024 * 1024


def _cp(sem, vmem_mb=48):
    return pltpu.CompilerParams(dimension_semantics=sem, vmem_limit_bytes=vmem_mb * _MB)


def _dot(a, b, mode):
    return lax.dot_general(a.astype(MXU), b.astype(MXU), _DIMS[mode], preferred_element_type=F32)


@jax.custom_vjp
def _mm_nn(a, b):
    return _dot(a, b, "nn")


def _mm_nn_fwd(a, b):
    return _dot(a, b, "nn"), (a, b)


def _mm_nn_bwd(res, g):
    a, b = res
    return _dot(g, b, "nt"), _dot(a, g, "tn")


_mm_nn.defvjp(_mm_nn_fwd, _mm_nn_bwd)


@jax.custom_vjp
def _mm_nt(a, bt):
    return _dot(a, bt, "nt")


def _mm_nt_fwd(a, bt):
    return _dot(a, bt, "nt"), (a, bt)


def _mm_nt_bwd(res, g):
    a, bt = res
    return _dot(g, bt, "nn"), _dot(g, a, "tn")


_mm_nt.defvjp(_mm_nt_fwd, _mm_nt_bwd)


def _rmsnorm(x, w):
    return x * lax.rsqrt(jnp.mean(x * x, axis=-1, keepdims=True) + EPS) * w


def _layernorm(x, w, b):
    mu = jnp.mean(x, axis=-1, keepdims=True)
    var = jnp.mean(jnp.square(x - mu), axis=-1, keepdims=True)
    return (x - mu) * lax.rsqrt(var + EPS) * w + b


def _silu(x):
    return x * jax.nn.sigmoid(x)


def _matmul(a, b, mode, *, name, shape, tm, tn, tk, out_dtype=F32, add=None, a_off=(0, 0), b_off=(0, 0), after=None):
    m, n, k = shape
    tm, tn, tk = min(tm, m), min(tn, n), min(tk, k)
    assert m % tm == 0 and n % tn == 0 and k % tk == 0, (name, shape, tm, tn, tk)
    nk = k // tk
    has_add, has_after = add is not None, after is not None

    def body(*refs):
        a_ref, b_ref = refs[0], refs[1]
        pos = 2
        add_ref = None
        if has_add:
            add_ref = refs[pos]
            pos += 1
        if has_after:
            pos += 1
        o_ref = refs[pos]
        p = _dot(a_ref[...], b_ref[...], mode)
        if nk == 1:
            if has_add:
                p = p + add_ref[...].astype(F32)
            o_ref[...] = p.astype(out_dtype)
            return
        acc_ref = refs[pos + 1]
        kk = pl.program_id(2)

        @pl.when(kk == 0)
        def _():
            acc_ref[...] = p

        @pl.when(kk > 0)
        def _():
            acc_ref[...] += p

        @pl.when(kk == nk - 1)
        def _():
            r = acc_ref[...]
            if has_add:
                r = r + add_ref[...].astype(F32)
            o_ref[...] = r.astype(out_dtype)

    a0, a1 = a_off
    b0, b1 = b_off
    if mode == "tn":
        a_spec = pl.BlockSpec((tk, tm), lambda i, j, kk: (kk + a0, i + a1))
    else:
        a_spec = pl.BlockSpec((tm, tk), lambda i, j, kk: (i + a0, kk + a1))
    if mode == "nt":
        b_spec = pl.BlockSpec((tn, tk), lambda i, j, kk: (j + b0, kk + b1))
    else:
        b_spec = pl.BlockSpec((tk, tn), lambda i, j, kk: (kk + b0, j + b1))
    in_specs = [a_spec, b_spec]
    args = [a, b]
    if has_add:
        in_specs.append(pl.BlockSpec((tm, tn), lambda i, j, kk: (i, j)))
        args.append(add)
    if has_after:
        in_specs.append(pl.BlockSpec(memory_space=pl.ANY))
        args.append(after)
    return pl.pallas_call(
        body, name=name, grid=(m // tm, n // tn, nk),
        in_specs=in_specs,
        out_specs=pl.BlockSpec((tm, tn), lambda i, j, kk: (i, j)),
        out_shape=jax.ShapeDtypeStruct((m, n), out_dtype),
        scratch_shapes=[pltpu.VMEM((tm, tn), F32)] if nk > 1 else [],
        compiler_params=_cp(("parallel", "parallel", "arbitrary")),
    )(*args)


def _rms_fwd(x, w, *, name, tm=256, after=None):
    L, d = x.shape
    tm = min(tm, L)
    extra = [] if after is None else [after]

    def body(x_ref, w_ref, *rest):
        rest[-1][...] = _rmsnorm(x_ref[...], w_ref[...]).astype(MXU)

    return pl.pallas_call(
        body, name=name, grid=(L // tm,),
        in_specs=([pl.BlockSpec((tm, d), lambda i: (i, 0)), pl.BlockSpec((1, d), lambda i: (0, 0))]
                  + [pl.BlockSpec(memory_space=pl.ANY)] * len(extra)),
        out_specs=pl.BlockSpec((tm, d), lambda i: (i, 0)),
        out_shape=jax.ShapeDtypeStruct((L, d), MXU),
        compiler_params=_cp(("parallel",)),
    )(x, w.reshape(1, d), *extra)


def _rms_bwd(x, w, dh, dxn, *, name, tm=256, after=None):
    L, d = x.shape
    tm = min(tm, L)
    extra = [] if after is None else [after]

    def body(x_ref, w_ref, dh_ref, dxn_ref, *rest):
        dx_ref, dw_ref = rest[-2:]
        _, vjp = jax.vjp(_rmsnorm, x_ref[...], w_ref[...])
        dx, dw = vjp(dh_ref[...])
        dx_ref[...] = dx + dxn_ref[...]

        @pl.when(pl.program_id(0) == 0)
        def _():
            dw_ref[...] = jnp.zeros_like(dw_ref)

        dw_ref[...] += dw

    row = pl.BlockSpec((tm, d), lambda i: (i, 0))
    vec = pl.BlockSpec((1, d), lambda i: (0, 0))
    dx, dw = pl.pallas_call(
        body, name=name, grid=(L // tm,),
        in_specs=[row, vec, row, row] + [pl.BlockSpec(memory_space=pl.ANY)] * len(extra), out_specs=[row, vec],
        out_shape=[jax.ShapeDtypeStruct((L, d), F32), jax.ShapeDtypeStruct((1, d), F32)],
        compiler_params=_cp(("arbitrary",)),
    )(x, w.reshape(1, d), dh, dxn, *extra)
    return dx, dw.reshape(d)


def _final_loss(x, w, tgt, *, name, tm=256):
    L, d = x.shape
    tm = min(tm, L)

    def loss_fn(xv, wv, tv):
        err = jnp.square(_rmsnorm(xv, wv) - tv)
        return 0.5 * jnp.sum(jnp.mean(err, axis=-1, keepdims=True), axis=0, keepdims=True)

    def body(x_ref, w_ref, t_ref, loss_ref, dx_ref, dw_ref):
        tv = t_ref[...]
        val, vjp = jax.vjp(lambda xv, wv: loss_fn(xv, wv, tv), x_ref[...], w_ref[...])
        dx, dw = vjp(jnp.ones((1, 1), F32))
        dx_ref[...] = dx

        @pl.when(pl.program_id(0) == 0)
        def _():
            dw_ref[...] = jnp.zeros_like(dw_ref)
            loss_ref[...] = jnp.zeros_like(loss_ref)

        dw_ref[...] += dw
        loss_ref[...] += jnp.broadcast_to(val, loss_ref.shape)

    row = pl.BlockSpec((tm, d), lambda i: (i, 0))
    vec = pl.BlockSpec((1, d), lambda i: (0, 0))
    loss, dx, dw = pl.pallas_call(
        body, name=name, grid=(L // tm,),
        in_specs=[row, vec, row],
        out_specs=[pl.BlockSpec((8, 128), lambda i: (0, 0)), row, vec],
        out_shape=[jax.ShapeDtypeStruct((8, 128), F32), jax.ShapeDtypeStruct((L, d), F32),
                   jax.ShapeDtypeStruct((1, d), F32)],
        compiler_params=_cp(("arbitrary",)),
    )(x, w.reshape(1, d), tgt)
    return loss[0, 0], dx, dw.reshape(d)


PARAM_ROWS = 512


def _s5_param_fn(are, aim, ldt, bre, bim, row0):
    n = are.shape[0]
    grp = (row0 + lax.broadcasted_iota(jnp.int32, (n, SSM_GROUPS), 0)) // SSM_STATE
    col = lax.broadcasted_iota(jnp.int32, (n, SSM_GROUPS), 1)
    sel = (grp == col).astype(F32)
    dt = jnp.sum(sel * jnp.exp(ldt), axis=-1, keepdims=True)
    mag = jnp.exp(are * dt)
    ang = aim * dt
    lbr = mag * jnp.cos(ang)
    lbi = mag * jnp.sin(ang)
    den = are * are + aim * aim
    nr = lbr - 1.0
    kr = (nr * are + lbi * aim) / den
    ki = (lbi * are - nr * aim) / den
    return lbr, lbi, kr * bre - ki * bim, kr * bim + ki * bre


def _s5_param_specs():
    col = pl.BlockSpec((PARAM_ROWS, 1), lambda i: (i, 0))
    mat = pl.BlockSpec((PARAM_ROWS, SSM_GROUP), lambda i: (i, 0))
    vec = pl.BlockSpec((1, SSM_GROUPS), lambda i: (0, 0))
    return col, mat, vec


def _s5_params_fwd(are, aim, ldt, bre, bim, *, name, after=None):
    n = are.shape[0]
    col, mat, vec = _s5_param_specs()
    extra = [] if after is None else [after]

    def body(are_ref, aim_ref, ldt_ref, bre_ref, bim_ref, *rest):
        lbr_ref, lbi_ref, bbr_ref, bbi_ref = rest[-4:]
        row0 = pl.program_id(0) * PARAM_ROWS
        lbr, lbi, bbr, bbi = _s5_param_fn(are_ref[...], aim_ref[...], ldt_ref[...], bre_ref[...], bim_ref[...], row0)
        lbr_ref[...] = lbr
        lbi_ref[...] = lbi
        bbr_ref[...] = bbr
        bbi_ref[...] = bbi

    cshape = jax.ShapeDtypeStruct((n, 1), F32)
    mshape = jax.ShapeDtypeStruct((n, SSM_GROUP), F32)
    return pl.pallas_call(body, name=name, grid=(n // PARAM_ROWS,),
                          in_specs=[col, col, vec, mat, mat] + [pl.BlockSpec(memory_space=pl.ANY)] * len(extra),
                          out_specs=[col, col, mat, mat], out_shape=[cshape, cshape, mshape, mshape],
                          compiler_params=_cp(("parallel",)))(are, aim, ldt, bre, bim, *extra)


def _s5_params_bwd(are, aim, ldt, bre, bim, dlbr, dlbi, dbbr, dbbi, *, name):
    n = are.shape[0]
    col, mat, vec = _s5_param_specs()

    def body(are_ref, aim_ref, ldt_ref, bre_ref, bim_ref, g0, g1, g2, g3, o0, o1, o2, o3, o4):
        row0 = pl.program_id(0) * PARAM_ROWS
        _, vjp = jax.vjp(lambda a, b, c, d, e: _s5_param_fn(a, b, c, d, e, row0),
                         are_ref[...], aim_ref[...], ldt_ref[...], bre_ref[...], bim_ref[...])
        dare, daim, dldt, dbre, dbim = vjp((g0[...], g1[...], g2[...], g3[...]))
        o0[...] = dare
        o1[...] = daim
        o3[...] = dbre
        o4[...] = dbim

        @pl.when(pl.program_id(0) == 0)
        def _():
            o2[...] = jnp.zeros_like(o2)

        o2[...] += dldt

    cshape = jax.ShapeDtypeStruct((n, 1), F32)
    mshape = jax.ShapeDtypeStruct((n, SSM_GROUP), F32)
    return pl.pallas_call(body, name=name, grid=(n // PARAM_ROWS,),
                          in_specs=[col, col, vec, mat, mat, col, col, mat, mat],
                          out_specs=[col, col, vec, mat, mat],
                          out_shape=[cshape, cshape, jax.ShapeDtypeStruct((1, SSM_GROUPS), F32), mshape, mshape],
                          compiler_params=_cp(("arbitrary",)))(are, aim, ldt, bre, bim, dlbr, dlbi, dbbr, dbbi)


SLAB_NC = SLAB_CH // 128


def _s5_specs(L):
    slab = pl.BlockSpec((L, SLAB), lambda s: (0, s))
    wspec = pl.BlockSpec((SLAB_NC, 128, SLAB), lambda s: (s, 0, 0))
    lspec = pl.BlockSpec((SLAB_NC, 1, 128), lambda s: (s, 0, 0))
    sspec = pl.BlockSpec((SLAB_NC, L, 128), lambda s: (s, 0, 0))
    dspec = pl.BlockSpec((1, SLAB), lambda s: (0, s))
    return slab, wspec, lspec, sspec, dspec


def _scan_inplace(sr_ref, si_ref, lr, li, pr_ref, pi_ref, *, reverse):
    NC, L, W = sr_ref.shape
    S = SCAN_SEG
    T = L // S
    lr8 = [jnp.broadcast_to(lr[k], (S, W)) for k in range(NC)]
    li8 = [jnp.broadcast_to(li[k], (S, W)) for k in range(NC)]

    def tiles(first, count):
        return pl.ds(first * S, count * S)

    for k in range(NC):
        pr_ref[k, tiles(T - 1 if reverse else 0, 1), :] = lr8[k]
        pi_ref[k, tiles(T - 1 if reverse else 0, 1), :] = li8[k]
        n = 1
        while n < T:
            have = tiles(T - n, n) if reverse else tiles(0, n)
            new = tiles(T - 2 * n, n) if reverse else tiles(n, n)
            top = tiles(T - n, 1) if reverse else tiles(n - 1, 1)
            ar, ai = pr_ref[k, top, :][None], pi_ref[k, top, :][None]
            hr, hi = pr_ref[k, have, :].reshape(n, S, W), pi_ref[k, have, :].reshape(n, S, W)
            pr_ref[k, new, :] = (hr * ar - hi * ai).reshape(n * S, W)
            pi_ref[k, new, :] = (hr * ai + hi * ar).reshape(n * S, W)
            n *= 2

    def step(i, carry):
        for u in range(SCAN_STEPS):
            jj = i * SCAN_STEPS + u
            rows = pl.ds(pl.multiple_of(((T - 1 - jj) if reverse else jj) * S, S), S)
            out = []
            for k in range(NC):
                sr, si = carry[k]
                nsr = lr8[k] * sr - li8[k] * si + sr_ref[k, rows, :]
                nsi = lr8[k] * si + li8[k] * sr + si_ref[k, rows, :]
                sr_ref[k, rows, :] = nsr
                si_ref[k, rows, :] = nsi
                out.append((nsr, nsi))
            carry = tuple(out)
        return carry

    zero = jnp.zeros((S, W), F32)
    ends = lax.fori_loop(0, T // SCAN_STEPS, step, tuple((zero, zero) for k in range(NC)))
    sub = lax.broadcasted_iota(jnp.int32, (S, W), 0)
    order = range(S - 1, -1, -1) if reverse else range(S)
    for k in range(NC):
        er, ei = ends[k]
        full = tiles(0 if reverse else T - 1, 1)
        ltr = pr_ref[k, full, :][0:1]
        lti = pi_ref[k, full, :][0:1]
        cr = jnp.zeros((1, W), F32)
        ci = jnp.zeros((1, W), F32)
        ctr = jnp.zeros((S, W), F32)
        cti = jnp.zeros((S, W), F32)
        for seg in order:
            ctr = jnp.where(sub == seg, cr, ctr)
            cti = jnp.where(sub == seg, ci, cti)
            cr, ci = (er[seg:seg + 1, :] + ltr * cr - lti * ci, ei[seg:seg + 1, :] + ltr * ci + lti * cr)
        pr = pr_ref[k].reshape(T, S, W)
        pi = pi_ref[k].reshape(T, S, W)
        sr_ref[k] += (pr * ctr[None] - pi * cti[None]).reshape(L, W)
        si_ref[k] += (pr * cti[None] + pi * ctr[None]).reshape(L, W)


def _time_interleave(a):
    L, W = a.shape
    return a.reshape(SCAN_SEG, L // SCAN_SEG, W).transpose(1, 0, 2).reshape(L, W)


def _time_deinterleave(a):
    L, W = a.shape
    return a.reshape(L // SCAN_SEG, SCAN_SEG, W).transpose(1, 0, 2).reshape(L, W)


def _s5_fwd(u, btr, bti, cbr, cbi, lbr, lbi, dvec, *, name, after=None):
    L = u.shape[0]
    extra = [] if after is None else [after]

    def body(u_ref, btr_ref, bti_ref, cbr_ref, cbi_ref, lr_ref, li_ref, d_ref, *rest):
        ys_ref, sr_ref, si_ref, pr_ref, pi_ref = rest[-5:]
        u = u_ref[...]
        for k in range(SLAB_NC):
            sr_ref[k] = _dot(u, btr_ref[k], "nt")
            si_ref[k] = _dot(u, bti_ref[k], "nt")
        _scan_inplace(sr_ref, si_ref, lr_ref[...], li_ref[...], pr_ref, pi_ref, reverse=False)
        ys = d_ref[...] * u
        for k in range(SLAB_NC):
            ys = ys + _dot(sr_ref[k], cbr_ref[k], "nn") - _dot(si_ref[k], cbi_ref[k], "nn")
        ys_ref[...] = ys

    slab, wspec, lspec, sspec, dspec = _s5_specs(L)
    sshape = jax.ShapeDtypeStruct((N_SLAB * SLAB_NC, L, 128), F32)
    return pl.pallas_call(
        body, name=name, grid=(N_SLAB,),
        in_specs=[slab, wspec, wspec, wspec, wspec, lspec, lspec, dspec] + [pl.BlockSpec(memory_space=pl.ANY)] * len(extra),
        out_specs=[slab, sspec, sspec],
        out_shape=[jax.ShapeDtypeStruct((L, SSM_WIDTH), F32), sshape, sshape],
        scratch_shapes=[pltpu.VMEM((SLAB_NC, L, 128), F32), pltpu.VMEM((SLAB_NC, L, 128), F32)],
        compiler_params=_cp(("parallel",), 56),
    )(u, btr, bti, cbr, cbi, lbr, lbi, dvec, *extra)


def _s5_bwd(dys, u, sr, si, btr, bti, cbr, cbi, lbr, lbi, dvec, *, name, after=None):
    L = u.shape[0]
    S = SCAN_SEG
    extra = [] if after is None else [after]

    def body(dys_ref, u_ref, sr_ref, si_ref, btr_ref, bti_ref, cbr_ref, cbi_ref, lr_ref, li_ref, d_ref, *rest):
        (du_ref, dbtr_ref, dbti_ref, dcbr_ref, dcbi_ref, dlr_ref, dli_ref, dd_ref,
         ar_ref, ai_ref, pr_ref, pi_ref) = rest[-12:]
        dys = dys_ref[...]
        u = u_ref[...]
        for k in range(SLAB_NC):
            ar_ref[k] = _dot(dys, cbr_ref[k], "nt")
            ai_ref[k] = -_dot(dys, cbi_ref[k], "nt")
        _scan_inplace(ar_ref, ai_ref, lr_ref[...], -li_ref[...], pr_ref, pi_ref, reverse=True)
        head = lax.broadcasted_iota(jnp.int32, (L, 1), 0) < S
        sub0 = lax.broadcasted_iota(jnp.int32, (S, 1), 0) == 0

        def prev_state(s):
            up = pltpu.roll(s, S, 0)
            return jnp.where(head, 0.0, up), jnp.where(sub0, 0.0, pltpu.roll(up[0:S], 1, 0))

        du = d_ref[...] * dys
        for k in range(SLAB_NC):
            a_re = ar_ref[k]
            a_im = ai_ref[k]
            du = du + _dot(a_re, btr_ref[k], "nn") + _dot(a_im, bti_ref[k], "nn")
            dbtr_ref[k] = _dot(a_re, u, "tn")
            dbti_ref[k] = _dot(a_im, u, "tn")
            s_re = sr_ref[k]
            s_im = si_ref[k]
            dcbr_ref[k] = _dot(s_re, dys, "tn")
            dcbi_ref[k] = -_dot(s_im, dys, "tn")
            p_re, q_re = prev_state(s_re)
            p_im, q_im = prev_state(s_im)
            b_re, b_im = a_re[0:S], a_im[0:S]
            dlr_ref[k] = (jnp.sum(p_re * a_re + p_im * a_im, axis=0, keepdims=True)
                          + jnp.sum(q_re * b_re + q_im * b_im, axis=0, keepdims=True))
            dli_ref[k] = (jnp.sum(p_re * a_im - p_im * a_re, axis=0, keepdims=True)
                          + jnp.sum(q_re * b_im - q_im * b_re, axis=0, keepdims=True))
        du_ref[...] = du
        dd_ref[...] = jnp.sum(dys * u, axis=0, keepdims=True)

    slab, wspec, lspec, sspec, dspec = _s5_specs(L)
    wshape = jax.ShapeDtypeStruct((N_SLAB * SLAB_NC, 128, SLAB), F32)
    lshape = jax.ShapeDtypeStruct((N_SLAB * SLAB_NC, 1, 128), F32)
    return pl.pallas_call(
        body, name=name, grid=(N_SLAB,),
        in_specs=([slab, slab, sspec, sspec, wspec, wspec, wspec, wspec, lspec, lspec, dspec]
                  + [pl.BlockSpec(memory_space=pl.ANY)] * len(extra)),
        out_specs=[slab, wspec, wspec, wspec, wspec, lspec, lspec, dspec],
        out_shape=[jax.ShapeDtypeStruct((L, SSM_WIDTH), F32), wshape, wshape, wshape, wshape, lshape, lshape,
                   jax.ShapeDtypeStruct((1, SSM_WIDTH), F32)],
        scratch_shapes=[pltpu.VMEM((SLAB_NC, L, 128), F32)] * 4,
        compiler_params=_cp(("parallel",), 56),
    )(dys, u, sr, si, btr, bti, cbr, cbi, lbr, lbi, dvec, *extra)


_SLAB_MASK = (np.arange(SLAB_CH)[:, None] // SSM_STATE == np.arange(SLAB)[None, :] // SSM_GROUP)


def _expand_bd(x):
    t = jnp.tile(x.reshape(N_SLAB, SLAB_CH, SSM_GROUP), (1, 1, SLAB // SSM_GROUP))
    return jnp.where(_SLAB_MASK[None], t, 0.0).astype(MXU).reshape(N_SLAB * SLAB_NC, 128, SLAB)


def _contract_bd(dx):
    t = jnp.where(_SLAB_MASK[None], dx.reshape(N_SLAB, SLAB_CH, SLAB), 0.0)
    return jnp.sum(t.reshape(N_SLAB, SLAB_CH, SLAB // SSM_GROUP, SSM_GROUP), axis=2).reshape(SSM_CH, SSM_GROUP)


def _glu_ew(ys, zlin, za):
    a1 = jax.nn.gelu(ys)
    return a1 * jax.nn.sigmoid(zlin) * _silu(za)


def _glu_fwd(ys, main, gw, gb, *, name, tm=256):
    L = ys.shape[0]
    tm = min(tm, L)
    W = SSM_WIDTH

    def body(ys_ref, za_ref, gw_ref, gb_ref, ya_ref):
        ys = ys_ref[...]
        a1 = jax.nn.gelu(ys)
        zlin = _dot(a1, gw_ref[...], "nn") + gb_ref[...]
        ya_ref[...] = _glu_ew(ys, zlin, za_ref[...]).astype(MXU)

    return pl.pallas_call(
        body, name=name, grid=(L // tm,),
        in_specs=[pl.BlockSpec((tm, W), lambda i: (i, 0)), pl.BlockSpec((tm, W), lambda i: (i, 1)),
                  pl.BlockSpec((W, W), lambda i: (0, 0)), pl.BlockSpec((1, W), lambda i: (0, 0))],
        out_specs=pl.BlockSpec((tm, W), lambda i: (i, 0)),
        out_shape=jax.ShapeDtypeStruct((L, W), MXU),
        compiler_params=_cp(("parallel",)),
    )(ys, main, gw, gb.reshape(1, W))


def _glu_bwd(dya, ys, main, gw, gb, *, name, tm=256):
    L = ys.shape[0]
    tm = min(tm, L)
    W = SSM_WIDTH

    def body(dya_ref, ys_ref, za_ref, gw_ref, gb_ref, dys_ref, dza_ref, a1_ref, dzl_ref, db_ref):
        ys = ys_ref[...]
        a1, gelu_vjp = jax.vjp(jax.nn.gelu, ys)
        zlin = _dot(a1, gw_ref[...], "nn") + gb_ref[...]
        _, vjp = jax.vjp(lambda a, z, za: a * jax.nn.sigmoid(z) * _silu(za), a1, zlin, za_ref[...])
        da1, dzlin, dza = vjp(dya_ref[...].astype(F32))
        da1 = da1 + _dot(dzlin, gw_ref[...], "nt")
        dys_ref[...] = gelu_vjp(da1)[0]
        dza_ref[...] = dza
        a1_ref[...] = a1.astype(MXU)
        dzl_ref[...] = dzlin.astype(MXU)

        @pl.when(pl.program_id(0) == 0)
        def _():
            db_ref[...] = jnp.zeros_like(db_ref)

        db_ref[...] += jnp.sum(dzlin, axis=0, keepdims=True)

    row = pl.BlockSpec((tm, W), lambda i: (i, 0))
    vec = pl.BlockSpec((1, W), lambda i: (0, 0))
    return pl.pallas_call(
        body, name=name, grid=(L // tm,),
        in_specs=[row, row, pl.BlockSpec((tm, W), lambda i: (i, 1)), pl.BlockSpec((W, W), lambda i: (0, 0)), vec],
        out_specs=[row, row, row, row, vec],
        out_shape=[jax.ShapeDtypeStruct((L, W), F32), jax.ShapeDtypeStruct((L, W), F32),
                   jax.ShapeDtypeStruct((L, W), MXU), jax.ShapeDtypeStruct((L, W), MXU),
                   jax.ShapeDtypeStruct((1, W), F32)],
        compiler_params=_cp(("arbitrary",)),
    )(dya, ys, main, gw, gb.reshape(1, W))


def _sg_fn(ub, vb, zb, lnw, lnb, ws, bs):
    u = jax.nn.gelu(ub)
    v = _layernorm(jax.nn.gelu(vb), lnw, lnb)
    r = lax.broadcasted_iota(jnp.int32, (SG_CHUNK, SG_CHUNK), 0)
    c = lax.broadcasted_iota(jnp.int32, (SG_CHUNK, SG_CHUNK), 1)
    tri = r >= c
    outs = []
    for h in range(SG_HEADS):
        wh = jnp.where(tri, ws[h], 0.0)
        outs.append(_mm_nn(wh, v[:, h * 128:(h + 1) * 128]) + bs[h])
    mixed = jnp.concatenate(outs, axis=1)
    return u * mixed * _silu(zb)


def _sg_specs(L):
    W = SSM_WIDTH
    blk = lambda c: pl.BlockSpec((SG_CHUNK, W), lambda i, c=c: (i, c))
    vec = pl.BlockSpec((1, W), lambda i: (0, 0))
    wspec = pl.BlockSpec((SG_HEADS, SG_CHUNK, SG_CHUNK), lambda i: (0, 0, 0))
    bspec = pl.BlockSpec((SG_HEADS, SG_CHUNK, 1), lambda i: (0, 0, 0))
    return blk, vec, wspec, bspec


def _sg_fwd(main, lnw, lnb, sgw, sgb, *, name):
    L = main.shape[0]
    W = SSM_WIDTH
    blk, vec, wspec, bspec = _sg_specs(L)

    def body(ub_ref, vb_ref, zb_ref, lnw_ref, lnb_ref, w_ref, b_ref, yb_ref):
        ws = [w_ref[h] for h in range(SG_HEADS)]
        bs = [b_ref[h] for h in range(SG_HEADS)]
        yb_ref[...] = _sg_fn(ub_ref[...], vb_ref[...], zb_ref[...], lnw_ref[...], lnb_ref[...], ws, bs).astype(MXU)

    return pl.pallas_call(
        body, name=name, grid=(L // SG_CHUNK,),
        in_specs=[blk(2), blk(3), blk(4), vec, vec, wspec, bspec],
        out_specs=pl.BlockSpec((SG_CHUNK, W), lambda i: (i, 0)),
        out_shape=jax.ShapeDtypeStruct((L, W), MXU),
        compiler_params=_cp(("parallel",)),
    )(main, main, main, lnw.reshape(1, W), lnb.reshape(1, W), sgw, sgb.reshape(SG_HEADS, SG_CHUNK, 1))


def _sg_bwd(dyb, main, lnw, lnb, sgw, sgb, *, name):
    L = main.shape[0]
    W = SSM_WIDTH
    blk, vec, wspec, bspec = _sg_specs(L)

    def body(dyb_ref, ub_ref, vb_ref, zb_ref, lnw_ref, lnb_ref, w_ref, b_ref,
             dub_ref, dvb_ref, dzb_ref, dlnw_ref, dlnb_ref, dw_ref, db_ref):
        ws = [w_ref[h] for h in range(SG_HEADS)]
        bs = [b_ref[h] for h in range(SG_HEADS)]
        _, vjp = jax.vjp(_sg_fn, ub_ref[...], vb_ref[...], zb_ref[...], lnw_ref[...], lnb_ref[...], ws, bs)
        dub, dvb, dzb, dlnw, dlnb, dws, dbs = vjp(dyb_ref[...])

        @pl.when(pl.program_id(0) == 0)
        def _():
            dlnw_ref[...] = jnp.zeros_like(dlnw_ref)
            dlnb_ref[...] = jnp.zeros_like(dlnb_ref)
            dw_ref[...] = jnp.zeros_like(dw_ref)
            db_ref[...] = jnp.zeros_like(db_ref)

        dub_ref[...] = dub
        dvb_ref[...] = dvb
        dzb_ref[...] = dzb
        dlnw_ref[...] += dlnw
        dlnb_ref[...] += dlnb
        for h in range(SG_HEADS):
            dw_ref[h] += dws[h]
            db_ref[h] += dbs[h]

    row = pl.BlockSpec((SG_CHUNK, W), lambda i: (i, 0))
    out = jax.ShapeDtypeStruct((L, W), F32)
    return pl.pallas_call(
        body, name=name, grid=(L // SG_CHUNK,),
        in_specs=[row, blk(2), blk(3), blk(4), vec, vec, wspec, bspec],
        out_specs=[row, row, row, vec, vec, wspec, bspec],
        out_shape=[out, out, out, jax.ShapeDtypeStruct((1, W), F32), jax.ShapeDtypeStruct((1, W), F32),
                   jax.ShapeDtypeStruct((SG_HEADS, SG_CHUNK, SG_CHUNK), F32),
                   jax.ShapeDtypeStruct((SG_HEADS, SG_CHUNK, 1), F32)],
        compiler_params=_cp(("arbitrary",)),
    )(dyb, main, main, main, lnw.reshape(1, W), lnb.reshape(1, W), sgw, sgb.reshape(SG_HEADS, SG_CHUNK, 1))


def _rope_tables(L):
    half = ROT_DIM // 2
    inv_freq = ROPE_THETA ** (-jnp.arange(0, ROT_DIM, 2, dtype=F32) / ROT_DIM)
    ang = jnp.arange(L, dtype=F32)[:, None] * inv_freq[None, :]
    cos = jnp.cos(ang)
    sin = jnp.sin(ang)
    ones = jnp.ones((L, HEAD_DIM - ROT_DIM), F32)
    cosf = jnp.concatenate([cos, cos, ones], axis=1)
    sinf = jnp.concatenate([sin, sin, 0.0 * ones], axis=1)
    rot = np.zeros((HEAD_DIM, HEAD_DIM), np.float32)
    for d in range(half):
        rot[d + half, d] = -1.0
        rot[d, d + half] = 1.0
    return cosf, sinf, jnp.asarray(rot)


def _rope(t, cosf, sinf, rot):
    shp = t.shape
    t2 = t.reshape(-1, HEAD_DIM)
    sw = lax.dot_general(t2, rot, _DIMS["nn"], precision=lax.Precision.HIGH, preferred_element_type=F32).reshape(shp)
    return t * cosf + sw * sinf


def _attn_core_parts(s, va, sink):
    h, q, k = s.shape
    m = jnp.maximum(jnp.max(s, axis=-1, keepdims=True), sink)
    e = jnp.exp(s - m)
    es = jnp.exp(sink - m)
    ev = _dot(e.reshape(h * q, k), va, "nn")
    r = 1.0 / (ev[:, HEAD_DIM:HEAD_DIM + 1].reshape(h, q, 1) + es)
    return ev[:, :HEAD_DIM] * r.reshape(h * q, 1), e, r, es


@jax.custom_vjp
def _attn_core(s, v, va, sink):
    return _attn_core_parts(s, va, sink)[0]


def _attn_core_fwd(s, v, va, sink):
    o, e, r, es = _attn_core_parts(s, va, sink)
    return o, (o, e, r, es, v, va)


def _attn_core_bwd(res, do):
    o, e, r, es, v, va = res
    h, q, k = e.shape
    p = e * r
    t = jnp.sum(o * do, axis=-1, keepdims=True).reshape(h, q, 1)
    dp = _dot(do, v, "nt").reshape(h, q, k)
    dv = _dot(p.reshape(h * q, k), do, "tn")
    dsink = -jnp.sum(es * r * t, axis=1, keepdims=True)
    return p * (dp - t), dv, jnp.zeros_like(va), dsink


_attn_core.defvjp(_attn_core_fwd, _attn_core_bwd)


def _attn_block_fn(q, kw, vw, sinks, vaw, cq, sq, ck, sk, rot, q0, k0):
    nk = kw.shape[1]
    qr = _rope(q, cq, sq, rot)
    kr = _rope(kw, ck, sk, rot)
    qpos = q0 + lax.broadcasted_iota(jnp.int32, (1, ATT_BLOCK, nk), 1)
    kpos = k0 + lax.broadcasted_iota(jnp.int32, (1, ATT_BLOCK, nk), 2)
    diff = qpos - kpos
    allowed = (diff >= 0) & (diff < WINDOW)
    outs = []
    for kh in range(ATT_KV_HEADS):
        qh = qr[kh * GQA_GROUP:(kh + 1) * GQA_GROUP].reshape(GQA_GROUP * ATT_BLOCK, HEAD_DIM)
        s = _mm_nt(qh, kr[kh]).reshape(GQA_GROUP, ATT_BLOCK, nk) * (HEAD_DIM ** -0.5)
        s = jnp.where(allowed, s, NEG_INF)
        o = _attn_core(s, vw[kh], vaw[kh], sinks[kh * GQA_GROUP:(kh + 1) * GQA_GROUP])
        outs.append(o.reshape(GQA_GROUP, ATT_BLOCK, HEAD_DIM))
    return jnp.concatenate(outs, axis=0)


def _attn_common(L):
    nwin = min(2 * ATT_BLOCK, L)
    qspec = pl.BlockSpec((ATT_HEADS, ATT_BLOCK, HEAD_DIM), lambda n: (0, n, 0))
    kvspec = pl.BlockSpec((ATT_KV_HEADS, L, HEAD_DIM), lambda n: (0, 0, 0))
    sspec = pl.BlockSpec((ATT_HEADS, 1, 1), lambda n: (0, 0, 0))
    tq = pl.BlockSpec((ATT_BLOCK, HEAD_DIM), lambda n: (n, 0))
    tk = pl.BlockSpec((L, HEAD_DIM), lambda n: (0, 0))
    rspec = pl.BlockSpec((HEAD_DIM, HEAD_DIM), lambda n: (0, 0))
    vaspec = pl.BlockSpec((ATT_KV_HEADS, L, 2 * HEAD_DIM), lambda n: (0, 0, 0))
    return nwin, qspec, kvspec, sspec, tq, tk, rspec, vaspec


def _v_with_ones(vh):
    return jnp.concatenate([vh, jnp.ones_like(vh)], axis=-1).astype(MXU)


def _attn_fwd(qh, kh, vh, sinks, cosf, sinf, rot, *, name):
    L = qh.shape[1]
    nwin, qspec, kvspec, sspec, tq, tk, rspec, vaspec = _attn_common(L)

    def body(q_ref, k_ref, v_ref, s_ref, va_ref, cq_ref, sq_ref, ck_ref, sk_ref, r_ref, o_ref):
        n = pl.program_id(0)
        k0 = pl.multiple_of(jnp.maximum(n - 1, 0) * ATT_BLOCK, ATT_BLOCK)
        win = pl.ds(k0, nwin)
        o_ref[...] = _attn_block_fn(q_ref[...], k_ref[:, win, :], v_ref[:, win, :], s_ref[...], va_ref[:, win, :],
                                    cq_ref[...], sq_ref[...], ck_ref[win, :], sk_ref[win, :], r_ref[...],
                                    n * ATT_BLOCK, k0)

    return pl.pallas_call(
        body, name=name, grid=(L // ATT_BLOCK,),
        in_specs=[qspec, kvspec, kvspec, sspec, vaspec, tq, tq, tk, tk, rspec],
        out_specs=qspec,
        out_shape=jax.ShapeDtypeStruct((ATT_HEADS, L, HEAD_DIM), F32),
        compiler_params=_cp(("parallel",)),
    )(qh, kh, vh, sinks.reshape(ATT_HEADS, 1, 1), _v_with_ones(vh), cosf, sinf, cosf, sinf, rot)


def _attn_bwd(do, qh, kh, vh, sinks, cosf, sinf, rot, *, name):
    L = qh.shape[1]
    nwin, qspec, kvspec, sspec, tq, tk, rspec, vaspec = _attn_common(L)

    def body(do_ref, q_ref, k_ref, v_ref, s_ref, va_ref, cq_ref, sq_ref, ck_ref, sk_ref, r_ref,
             dq_ref, dk_ref, dv_ref, ds_ref):
        n = pl.program_id(0)
        k0 = pl.multiple_of(jnp.maximum(n - 1, 0) * ATT_BLOCK, ATT_BLOCK)
        win = pl.ds(k0, nwin)
        cq, sq, ck, sk, rt = cq_ref[...], sq_ref[...], ck_ref[win, :], sk_ref[win, :], r_ref[...]
        vaw = va_ref[:, win, :]
        q0 = n * ATT_BLOCK
        _, vjp = jax.vjp(lambda q, kw, vw, s: _attn_block_fn(q, kw, vw, s, vaw, cq, sq, ck, sk, rt, q0, k0),
                         q_ref[...], k_ref[:, win, :], v_ref[:, win, :], s_ref[...])
        dq, dkw, dvw, ds = vjp(do_ref[...])

        @pl.when(n == 0)
        def _():
            dk_ref[...] = jnp.zeros_like(dk_ref)
            dv_ref[...] = jnp.zeros_like(dv_ref)
            ds_ref[...] = jnp.zeros_like(ds_ref)

        dq_ref[...] = dq
        dk_ref[:, win, :] += dkw
        dv_ref[:, win, :] += dvw
        ds_ref[...] += ds

    return pl.pallas_call(
        body, name=name, grid=(L // ATT_BLOCK,),
        in_specs=[qspec, qspec, kvspec, kvspec, sspec, vaspec, tq, tq, tk, tk, rspec],
        out_specs=[qspec, kvspec, kvspec, sspec],
        out_shape=[jax.ShapeDtypeStruct((ATT_HEADS, L, HEAD_DIM), F32),
                   jax.ShapeDtypeStruct((ATT_KV_HEADS, L, HEAD_DIM), F32),
                   jax.ShapeDtypeStruct((ATT_KV_HEADS, L, HEAD_DIM), F32),
                   jax.ShapeDtypeStruct((ATT_HEADS, 1, 1), F32)],
        compiler_params=_cp(("arbitrary",)),
    )(do, qh, kh, vh, sinks.reshape(ATT_HEADS, 1, 1), _v_with_ones(vh), cosf, sinf, cosf, sinf, rot)


def _to_heads(t, nh):
    L = t.shape[0]
    return t.reshape(L, nh, HEAD_DIM).transpose(1, 0, 2)


def _from_heads(t):
    nh, L, _ = t.shape
    return t.transpose(1, 0, 2).reshape(L, nh * HEAD_DIM)


def _branch_fwd(ya, yb, o2d, zc, gates, wa, wb, wc, *, name, tm=256):
    L = ya.shape[0]
    tm = min(tm, L)
    W, D = SSM_WIDTH, D_MODEL

    def body(ya_ref, yb_ref, o_ref, zc_ref, g0_ref, g1_ref, g2_ref, wa_ref, wb_ref, wc_ref,
             mg_ref, ta_ref, tb_ref, tc_ref, yc_ref):
        yc = (o_ref[...] * _silu(zc_ref[...])).astype(MXU)
        ta = _dot(ya_ref[...], wa_ref[...], "nt")
        tb = _dot(yb_ref[...], wb_ref[...], "nt")
        tc = _dot(yc, wc_ref[...], "nt")
        ta_ref[...] = ta
        tb_ref[...] = tb
        tc_ref[...] = tc
        yc_ref[...] = yc
        mg_ref[...] = (jax.nn.sigmoid(g0_ref[...]) * ta + jax.nn.sigmoid(g1_ref[...]) * tb
                       + jax.nn.sigmoid(g2_ref[...]) * tc).astype(MXU)

    row = pl.BlockSpec((tm, W), lambda i: (i, 0))
    wide = pl.BlockSpec((tm, D), lambda i: (i, 0))
    gate = lambda c: pl.BlockSpec((tm, D), lambda i, c=c: (i, c))
    wspec = pl.BlockSpec((D, W), lambda i: (0, 0))
    return pl.pallas_call(
        body, name=name, grid=(L // tm,),
        in_specs=[row, row, row, row, gate(0), gate(1), gate(2), wspec, wspec, wspec],
        out_specs=[wide, wide, wide, wide, row],
        out_shape=[jax.ShapeDtypeStruct((L, D), MXU), jax.ShapeDtypeStruct((L, D), F32),
                   jax.ShapeDtypeStruct((L, D), F32), jax.ShapeDtypeStruct((L, D), F32),
                   jax.ShapeDtypeStruct((L, W), MXU)],
        compiler_params=_cp(("parallel",), 56),
    )(ya, yb, o2d, zc, gates, gates, gates, wa, wb, wc)


def _branch_bwd(dmg, ta, tb, tc, gates, *, name, tm=256):
    L = dmg.shape[0]
    tm = min(tm, L)
    D = D_MODEL

    def body(dm_ref, ta_ref, tb_ref, tc_ref, g0_ref, g1_ref, g2_ref, da_ref, db_ref, dc_ref, dg_ref):
        dm = dm_ref[...]
        for i, (t_ref, g_ref, d_ref) in enumerate(((ta_ref, g0_ref, da_ref), (tb_ref, g1_ref, db_ref),
                                                   (tc_ref, g2_ref, dc_ref))):
            sg = jax.nn.sigmoid(g_ref[...])
            d_ref[...] = (sg * dm).astype(MXU)
            dg_ref[:, i * D:(i + 1) * D] = (dm * t_ref[...] * sg * (1.0 - sg)).astype(MXU)

    wide = pl.BlockSpec((tm, D), lambda i: (i, 0))
    gate = lambda c: pl.BlockSpec((tm, D), lambda i, c=c: (i, c))
    bf = jax.ShapeDtypeStruct((L, D), MXU)
    return pl.pallas_call(
        body, name=name, grid=(L // tm,),
        in_specs=[wide, wide, wide, wide, gate(0), gate(1), gate(2)],
        out_specs=[wide, wide, wide, pl.BlockSpec((tm, 3 * D), lambda i: (i, 0))],
        out_shape=[bf, bf, bf, jax.ShapeDtypeStruct((L, 3 * D), MXU)],
        compiler_params=_cp(("parallel",), 56),
    )(dmg, ta, tb, tc, gates, gates, gates)


def _gate_c_bwd(dyc, o2d, zc, *, name, tm=256):
    L, W = dyc.shape
    tm = min(tm, L)

    def body(dy_ref, o_ref, z_ref, do_ref, dz_ref):
        _, vjp = jax.vjp(lambda o, z: o * _silu(z), o_ref[...], z_ref[...])
        do, dz = vjp(dy_ref[...])
        do_ref[...] = do
        dz_ref[...] = dz.astype(MXU)

    row = pl.BlockSpec((tm, W), lambda i: (i, 0))
    return pl.pallas_call(body, name=name, grid=(L // tm,), in_specs=[row, row, row], out_specs=[row, row],
                          out_shape=[jax.ShapeDtypeStruct((L, W), F32), jax.ShapeDtypeStruct((L, W), MXU)],
                          compiler_params=_cp(("parallel",)))(dyc, o2d, zc)


def _adamw(w, g, m, v, *, name):
    shape = w.shape
    cols = shape[-1]
    w2, g2, m2, v2 = (t.reshape(-1, cols) for t in (w, g, m, v))
    rows = w2.shape[0]
    tc = 1024 if cols % 1024 == 0 else cols
    lane_cols = -(-tc // 128) * 128
    tr = rows
    while tr % 16 == 0 and tr * lane_cols * 4 > 2 * _MB:
        tr //= 2

    def body(w_ref, g_ref, m_ref, v_ref, d_ref, nm_ref, nv_ref):
        gv = g_ref[...]
        nm = ADAM_B1 * m_ref[...] + (1.0 - ADAM_B1) * gv
        nv = ADAM_B2 * v_ref[...] + (1.0 - ADAM_B2) * jnp.square(gv)
        m_hat = nm / (1.0 - ADAM_B1 ** ADAM_STEP)
        v_hat = nv / (1.0 - ADAM_B2 ** ADAM_STEP)
        d_ref[...] = -ADAM_LR * (m_hat / (jnp.sqrt(v_hat) + ADAM_EPS) + ADAM_WD * w_ref[...])
        nm_ref[...] = nm
        nv_ref[...] = nv

    spec = pl.BlockSpec((tr, tc), lambda i, j: (i, j))
    out = jax.ShapeDtypeStruct((rows, cols), F32)
    d, nm, nv = pl.pallas_call(body, name=name, grid=(rows // tr, cols // tc), in_specs=[spec] * 4,
                               out_specs=[spec] * 3, out_shape=[out, out, out],
                               compiler_params=_cp(("parallel", "parallel")))(w2, g2, m2, v2)
    return d.reshape(shape), nm.reshape(shape), nv.reshape(shape)


def _adamw_layer(w, g, m, v, l, prev, *, name):
    _, rows, cols = w.shape
    tc = 1024 if cols % 1024 == 0 else cols
    tr = rows
    while tr % 16 == 0 and tr * tc * 4 > 2 * _MB:
        tr //= 2

    def body(w_ref, g_ref, m_ref, v_ref, *rest):
        go_ref, d_ref, nm_ref, nv_ref = rest[-4:]
        gv = g_ref[...]
        nm = ADAM_B1 * m_ref[...] + (1.0 - ADAM_B1) * gv
        nv = ADAM_B2 * v_ref[...] + (1.0 - ADAM_B2) * jnp.square(gv)
        m_hat = nm / (1.0 - ADAM_B1 ** ADAM_STEP)
        v_hat = nv / (1.0 - ADAM_B2 ** ADAM_STEP)
        d_ref[...] = -ADAM_LR * (m_hat / (jnp.sqrt(v_hat) + ADAM_EPS) + ADAM_WD * w_ref[...])
        nm_ref[...] = nm
        nv_ref[...] = nv
        go_ref[...] = gv

    lspec = pl.BlockSpec((None, tr, tc), lambda i, j: (l, i, j))
    gspec = pl.BlockSpec((tr, tc), lambda i, j: (i, j))
    out = jax.ShapeDtypeStruct(w.shape, F32)
    extra = [] if prev is None else list(prev)
    return pl.pallas_call(
        body, name=name, grid=(rows // tr, cols // tc),
        in_specs=[lspec, gspec, lspec, lspec] + [_ANY] * len(extra),
        out_specs=[lspec] * 4, out_shape=[out] * 4,
        input_output_aliases={4 + i: i for i in range(len(extra))},
        compiler_params=_cp(("parallel", "parallel")),
    )(w, g, m, v, *extra)


def _prep_layer(p, l, after=None):
    are = p["ssm_a_re"][l].reshape(SSM_CH, 1)
    aim = p["ssm_a_im"][l].reshape(SSM_CH, 1)
    ldt = p["ssm_log_dt"][l].reshape(1, SSM_GROUPS)
    bre = p["ssm_b_re"][l].reshape(SSM_CH, SSM_GROUP)
    bim = p["ssm_b_im"][l].reshape(SSM_CH, SSM_GROUP)
    lbr, lbi, bbr, bbi = _s5_params_fwd(are, aim, ldt, bre, bim, name=f"s5_params_fwd_{l}", after=after)
    cre = p["ssm_c_re"][l].transpose(0, 2, 1).reshape(SSM_CH, SSM_GROUP)
    cim = p["ssm_c_im"][l].transpose(0, 2, 1).reshape(SSM_CH, SSM_GROUP)
    return dict(raw=(are, aim, ldt, bre, bim),
                lbr=lbr.reshape(N_SLAB * SLAB_NC, 1, 128), lbi=lbi.reshape(N_SLAB * SLAB_NC, 1, 128),
                btr=_expand_bd(bbr), bti=_expand_bd(bbi), cbr=_expand_bd(cre), cbi=_expand_bd(cim),
                dvec=p["ssm_d"][l].reshape(1, SSM_WIDTH))


def _layer_fwd(x, h, p, sp, winT, rest_of, l, tabs, proj_after=None, after_main=None):
    L = x.shape[0]
    cosf, sinf, rot = tabs
    mm = functools.partial(_matmul, h, winT, "nt", tm=L, tn=256, tk=D_MODEL)
    main = mm(name=f"proj_main_{l}", shape=(L, N_MAIN, D_MODEL), after=proj_after)
    then = proj_after if after_main is None else after_main(main)
    zc = mm(name=f"proj_zc_{l}", shape=(L, N_ZC, D_MODEL), b_off=(N_MAIN // 256, 0), after=then)
    gates = mm(name=f"proj_gates_{l}", shape=(L, N_GATES, D_MODEL), b_off=((N_MAIN + N_ZC) // 256, 0), after=then)
    big, token = rest_of([main, zc, gates])
    ua = _time_interleave(main[:, :SSM_WIDTH])
    ys, sr, si = _s5_fwd(ua, sp["btr"], sp["bti"], sp["cbr"], sp["cbi"], sp["lbr"], sp["lbi"], sp["dvec"],
                         name=f"s5_fwd_{l}", after=token)
    ys = _time_deinterleave(ys)
    ya = _glu_fwd(ys, main, big["glu_w"], p["ssm_glu_b"][l], name=f"glu_fwd_{l}")
    yb = _sg_fwd(main, p["sg_ln_w"][l], p["sg_ln_b"][l], p["sg_w"][l], p["sg_b"][l], name=f"sg_fwd_{l}")
    qh = _to_heads(main[:, 5120:6144], ATT_HEADS)
    kh = _to_heads(main[:, 6144:6272], ATT_KV_HEADS)
    vh = _to_heads(main[:, 6272:6400], ATT_KV_HEADS)
    oh = _attn_fwd(qh, kh, vh, p["attn_sinks"][l], cosf, sinf, rot, name=f"attn_fwd_{l}")
    o2d = _from_heads(oh)
    mg, ta, tb, tc, yc = _branch_fwd(ya, yb, o2d, zc, gates, big["wbaT"], big["wbbT"], big["wbcT"],
                                     name=f"branch_fwd_{l}")
    xn = _matmul(mg, big["w_out"], "nn", name=f"out_fwd_{l}", shape=(L, D_MODEL, D_MODEL), tm=512, tn=512,
                 tk=D_MODEL, add=x)
    saved = dict(x=x, h=h, main=main, zc=zc, gates=gates, ua=ua, ys=ys, sr=sr, si=si, ya=ya, yb=yb, yc=yc, o2d=o2d,
                 qh=qh, kh=kh, vh=vh, mg=mg, ta=ta, tb=tb, tc=tc, sp=sp)
    return xn, saved, big


def _layer_bwd(dxn, s, p, big, l, tabs, early, mid):
    L = dxn.shape[0]
    D, W = D_MODEL, SSM_WIDTH
    cosf, sinf, rot = tabs
    sp = s["sp"]
    g = {}
    dmg = _matmul(dxn, big["w_out"], "nt", name=f"out_bwd_dm_{l}", shape=(L, D, D), tm=512, tn=512, tk=D)
    g["w_out"] = _matmul(s["mg"], dxn, "tn", name=f"out_bwd_dw_{l}", shape=(D, D, L), tm=512, tn=512, tk=L,
                         out_dtype=MXU)
    dta, dtb, dtc, dgates = _branch_bwd(dmg, s["ta"], s["tb"], s["tc"], s["gates"], name=f"branch_bwd_{l}")
    dys_ = {}
    for nm, dt, y, wt in (("a", dta, s["ya"], big["wbaT"]), ("b", dtb, s["yb"], big["wbbT"]),
                          ("c", dtc, s["yc"], big["wbcT"])):
        dys_[nm] = _matmul(dt, wt, "nn", name=f"branch_bwd_dy{nm}_{l}", shape=(L, W, D), tm=512, tn=512, tk=D)
        g["wb" + nm + "T"] = _matmul(dt, y, "tn", name=f"branch_bwd_dw{nm}_{l}", shape=(D, W, L),
                                     tm=512, tn=512, tk=L, out_dtype=MXU)
    do2d, dzc = _gate_c_bwd(dys_["c"], s["o2d"], s["zc"], name=f"gate_c_bwd_{l}")
    dqh, dkh, dvh, dsinks = _attn_bwd(_to_heads(do2d, ATT_HEADS), s["qh"], s["kh"], s["vh"], p["attn_sinks"][l],
                                      cosf, sinf, rot, name=f"attn_bwd_{l}")
    g["attn_sinks"] = dsinks.reshape(ATT_HEADS)
    dub, dvb, dzb, dlnw, dlnb, dsgw, dsgb = _sg_bwd(dys_["b"], s["main"], p["sg_ln_w"][l], p["sg_ln_b"][l],
                                                    p["sg_w"][l], p["sg_b"][l], name=f"sg_bwd_{l}")
    g["sg_ln_w"], g["sg_ln_b"] = dlnw.reshape(W), dlnb.reshape(W)
    g["sg_w"], g["sg_b"] = dsgw, dsgb.reshape(SG_HEADS, SG_CHUNK)
    dys, dza, a1, dzl, dgb = _glu_bwd(dys_["a"], s["ys"], s["main"], big["glu_w"], p["ssm_glu_b"][l],
                                      name=f"glu_bwd_{l}")
    g["ssm_glu_b"] = dgb.reshape(W)
    g["glu_w"] = _matmul(a1, dzl, "tn", name=f"glu_bwd_dw_{l}", shape=(W, W, L), tm=512, tn=512, tk=L, out_dtype=MXU)
    token = early(g)
    dua, dbtr, dbti, dcbr, dcbi, dlr, dli, dd = _s5_bwd(_time_interleave(dys), s["ua"], s["sr"], s["si"], sp["btr"],
                                                        sp["bti"], sp["cbr"], sp["cbi"], sp["lbr"], sp["lbi"],
                                                        sp["dvec"], name=f"s5_bwd_{l}", after=token)
    token = mid(dua)
    dua = _time_deinterleave(dua)
    g["ssm_d"] = dd.reshape(W)
    to_c = lambda t: _contract_bd(t).reshape(SSM_GROUPS, SSM_STATE, SSM_GROUP).transpose(0, 2, 1)
    g["ssm_c_re"], g["ssm_c_im"] = to_c(dcbr), to_c(dcbi)
    dare, daim, dldt, dbre, dbim = _s5_params_bwd(*sp["raw"], dlr.reshape(SSM_CH, 1), dli.reshape(SSM_CH, 1),
                                                  _contract_bd(dbtr), _contract_bd(dbti),
                                                  name=f"s5_params_bwd_{l}")
    g["ssm_a_re"] = dare.reshape(SSM_GROUPS, SSM_STATE)
    g["ssm_a_im"] = daim.reshape(SSM_GROUPS, SSM_STATE)
    g["ssm_log_dt"] = dldt.reshape(SSM_GROUPS)
    g["ssm_b_re"] = dbre.reshape(SSM_GROUPS, SSM_STATE, SSM_GROUP)
    g["ssm_b_im"] = dbim.reshape(SSM_GROUPS, SSM_STATE, SSM_GROUP)
    dproj = jnp.concatenate([t.astype(MXU) for t in (dua, dza, dub, dvb, dzb, _from_heads(dqh), _from_heads(dkh),
                                                     _from_heads(dvh), dzc, dgates)], axis=1)
    g["winT"] = _matmul(dproj, s["h"], "tn", name=f"proj_bwd_dw_{l}", shape=(D_IN, D, L), tm=256, tn=D, tk=L,
                        out_dtype=MXU, after=token)
    return dproj, g


def _proj_bwd_dh(dproj, winT, l, after):
    return _matmul(dproj, winT, "nn", name=f"proj_bwd_dh_{l}", shape=(dproj.shape[0], D_MODEL, D_IN), tm=512, tn=512,
                   tk=D_IN // 2, after=after)


MESH = pl.DeviceIdType.MESH
_ANY = pl.BlockSpec(memory_space=pl.ANY)
ROW_ALIGN = 16


def _coords():
    return lax.axis_index("x"), lax.axis_index("y"), lax.axis_index("c")


def _gather8(arrs, *, name):
    n = len(arrs)
    rows = [a.shape[0] for a in arrs]
    for r in rows:
        assert r % ROW_ALIGN == 0

    def body(*refs):
        ins, outs = refs[:n], refs[n:2 * n]
        send, recv, lsem = refs[2 * n:]
        x, y, c = _coords()
        me, sibling = (x, y, c), (x, y, 1 - c)
        chips = [(1 - x, y), (x, 1 - y), (1 - x, 1 - y)]

        def blk(a, px, py, pc):
            return outs[a].at[pl.ds(pl.multiple_of((4 * px + 2 * py + pc) * rows[a], ROW_ALIGN), rows[a]), :]

        def own(a):
            return ins[a]

        def copy(a, k, block, to, src=None):
            return pltpu.make_async_remote_copy(
                src_ref=blk(a, *block) if src is None else src, dst_ref=blk(a, *block),
                send_sem=send.at[a, k], recv_sem=recv.at[a, k], device_id=to, device_id_type=MESH)

        mine, first, passed = [], [], []
        for a in range(n):
            mine.append(pltpu.make_async_copy(own(a), blk(a, *me), lsem.at[a]))
            mine[a].start()
            f = [copy(a, 0, me, sibling, src=own(a))]
            f += [copy(a, 1 + j, me, (*chip, c), src=own(a)) for j, chip in enumerate(chips)]
            for cp in f:
                cp.start()
            first.append(f)
        for a in range(n):
            ps = [copy(a, 4 + j, (*chip, c), sibling) for j, chip in enumerate(chips)]
            for j, chip in enumerate(chips):
                copy(a, 1 + j, (*chip, c), me).wait_recv()
                ps[j].start()
            passed.append(ps)
        for a in range(n):
            copy(a, 0, sibling, me).wait_recv()
            for j, chip in enumerate(chips):
                copy(a, 4 + j, (*chip, 1 - c), me).wait_recv()
            for cp in first[a] + passed[a]:
                cp.wait_send()
            mine[a].wait()

    return pl.pallas_call(
        body, name=name,
        in_specs=[_ANY] * n, out_specs=[_ANY] * n,
        out_shape=[jax.ShapeDtypeStruct((8 * r,) + a.shape[1:], a.dtype) for r, a in zip(rows, arrs)],
        scratch_shapes=[pltpu.SemaphoreType.DMA((n, 7)), pltpu.SemaphoreType.DMA((n, 7)), pltpu.SemaphoreType.DMA((n,))],
    )(*arrs)


def _sibling_swap(arrs, *, name):
    n = len(arrs)

    def body(*refs):
        ins, outs = refs[:n], refs[n:2 * n]
        send, recv = refs[2 * n:]
        x, y, c = _coords()
        cps = [pltpu.make_async_remote_copy(src_ref=ins[a].at[:, 1 - c], dst_ref=outs[a], send_sem=send.at[a],
                                            recv_sem=recv.at[a], device_id=(x, y, 1 - c), device_id_type=MESH)
               for a in range(n)]
        for cp in cps:
            cp.start()
        for cp in cps:
            cp.wait_recv()
        for cp in cps:
            cp.wait_send()

    return pl.pallas_call(
        body, name=name, in_specs=[_ANY] * n, out_specs=[_ANY] * n,
        out_shape=[jax.ShapeDtypeStruct((a.shape[0],) + a.shape[2:], a.dtype) for a in arrs],
        scratch_shapes=[pltpu.SemaphoreType.DMA((n,)), pltpu.SemaphoreType.DMA((n,))],
    )(*arrs)


def _col_tile(lead, rows, cols, itemsize=4, cap=4 * _MB):
    tc = cols
    while tc % 256 == 0 and lead * rows * tc * itemsize > cap:
        tc //= 2
    return tc


def _pair_sum(mine, theirs, *, name):
    _, _, rows, cols = mine.shape
    tc = _col_tile(1, rows, cols)
    c = lax.axis_index("c")

    def body(c_ref, a_ref, b_ref, o_ref):
        o_ref[...] = (a_ref[...].astype(F32) + b_ref[...].astype(F32)).astype(MXU)

    return pl.pallas_call(
        body, name=name,
        grid_spec=pltpu.PrefetchScalarGridSpec(
            num_scalar_prefetch=1, grid=(4, cols // tc),
            in_specs=[pl.BlockSpec((None, None, rows, tc), lambda j, i, cr: (j, cr[0], 0, i)),
                      pl.BlockSpec((None, rows, tc), lambda j, i, cr: (j, 0, i))],
            out_specs=pl.BlockSpec((None, rows, tc), lambda j, i, cr: (j, 0, i))),
        out_shape=jax.ShapeDtypeStruct((4, rows, cols), MXU),
        compiler_params=_cp(("parallel", "parallel")),
    )(c.reshape(1).astype(jnp.int32), mine, theirs)


_HBM = pl.BlockSpec(memory_space=pltpu.HBM)
_SEM = pl.BlockSpec(memory_space=pltpu.SEMAPHORE)
_EFFECT = pltpu.SideEffectType.DATAFLOW_SIDE_EFFECTING
N_PEER_CHIPS = 3


def _peer_chips(x, y):
    return [(1 - x, y), (x, 1 - y), (1 - x, 1 - y)]


def _split_start(srcs, lands, src_slot, dst_slot, *, name, after=()):
    n = len(srcs)
    ns = n * N_PEER_CHIPS
    first = 2 * n + len(after)

    def body(*refs):
        src_refs, land_refs = refs[:n], refs[n:2 * n]
        send, recv, token = refs[first:first + ns], refs[first + ns:first + 2 * ns], refs[-1]
        x, y, c = _coords()
        for a in range(n):
            for k, (px, py) in enumerate(_peer_chips(x, y)):
                pltpu.make_async_remote_copy(
                    src_ref=src_refs[a].at[src_slot(x, y, c, px, py)], dst_ref=land_refs[a].at[dst_slot(x, y, c)],
                    send_sem=send[a * N_PEER_CHIPS + k], recv_sem=recv[a * N_PEER_CHIPS + k],
                    device_id=(px, py, c), device_id_type=MESH).start()
        token[...] = jnp.zeros_like(token)

    bufs = list(srcs) + list(lands)
    res = pl.pallas_call(
        body, name=name,
        out_shape=(*[pltpu.SemaphoreType.DMA(())] * (2 * ns), *[pltpu.HBM(b.shape, b.dtype) for b in bufs],
                   jax.ShapeDtypeStruct((8, 128), F32)),
        in_specs=[_HBM] * (2 * n) + [_ANY] * len(after),
        out_specs=(*[_SEM] * (2 * ns), *[_HBM] * (2 * n), pl.BlockSpec(memory_space=pltpu.VMEM)),
        input_output_aliases={i: 2 * ns + i for i in range(2 * n)},
        compiler_params=pltpu.CompilerParams(has_side_effects=_EFFECT),
    )(*[pltpu.with_memory_space_constraint(b, pltpu.HBM) for b in bufs], *after)
    sems = list(res[:2 * ns])
    return sems, list(res[2 * ns:2 * ns + n]), list(res[2 * ns + n:2 * ns + 2 * n]), res[-1]


def _split_wait(sems, srcs, lands, after, *, name):
    n = len(srcs)
    ns = n * N_PEER_CHIPS

    def body(*refs):
        src_refs, land_refs = refs[:n], refs[n:2 * n]
        send, recv = refs[2 * n:2 * n + ns], refs[2 * n + ns:2 * n + 2 * ns]
        x, y, c = _coords()
        for a in range(n):
            for k in range(N_PEER_CHIPS):
                cp = pltpu.make_async_remote_copy(
                    src_ref=src_refs[a].at[0], dst_ref=land_refs[a].at[0], send_sem=send[a * N_PEER_CHIPS + k],
                    recv_sem=recv[a * N_PEER_CHIPS + k], device_id=(x, y, 1 - c), device_id_type=MESH)
                cp.wait_send()
                cp.wait_recv()

    bufs = list(srcs) + list(lands)
    res = pl.pallas_call(
        body, name=name,
        out_shape=tuple(pltpu.HBM(b.shape, b.dtype) for b in bufs),
        in_specs=[_HBM] * (2 * n) + [_SEM] * (2 * ns) + [_ANY] * len(after),
        out_specs=tuple([_HBM] * (2 * n)),
        input_output_aliases={i: i for i in range(2 * n)},
        compiler_params=pltpu.CompilerParams(has_side_effects=_EFFECT),
    )(*bufs, *sems, *after)
    return list(res[:n]), list(res[n:])


def _gather_start(lands, *, name, after=()):
    n = len(lands)
    ns = n * N_PEER_CHIPS
    first = n + len(after)

    def body(*refs):
        land_refs = refs[:n]
        send, recv, token = refs[first:first + ns], refs[first + ns:first + 2 * ns], refs[-1]
        x, y, c = _coords()
        mine = 4 * x + 2 * y + c
        for a in range(n):
            for k, (px, py) in enumerate(_peer_chips(x, y)):
                pltpu.make_async_remote_copy(
                    src_ref=land_refs[a].at[mine], dst_ref=land_refs[a].at[mine], send_sem=send[a * N_PEER_CHIPS + k],
                    recv_sem=recv[a * N_PEER_CHIPS + k], device_id=(px, py, c), device_id_type=MESH).start()
        token[...] = jnp.zeros_like(token)

    res = pl.pallas_call(
        body, name=name,
        out_shape=(*[pltpu.SemaphoreType.DMA(())] * (2 * ns), *[pltpu.HBM(b.shape, b.dtype) for b in lands],
                   jax.ShapeDtypeStruct((8, 128), F32)),
        in_specs=[_HBM] * n + [_ANY] * len(after),
        out_specs=(*[_SEM] * (2 * ns), *[_HBM] * n, pl.BlockSpec(memory_space=pltpu.VMEM)),
        input_output_aliases={i: 2 * ns + i for i in range(n)},
        compiler_params=pltpu.CompilerParams(has_side_effects=_EFFECT),
    )(*[pltpu.with_memory_space_constraint(b, pltpu.HBM) for b in lands], *after)
    return list(res[:2 * ns]), list(res[2 * ns:2 * ns + n]), res[-1]


def _gather_wait(sems, lands, after, *, name):
    n = len(lands)
    ns = n * N_PEER_CHIPS

    def body(*refs):
        land_refs = refs[:n]
        send, recv = refs[n:n + ns], refs[n + ns:n + 2 * ns]
        x, y, c = _coords()
        for a in range(n):
            for k in range(N_PEER_CHIPS):
                cp = pltpu.make_async_remote_copy(
                    src_ref=land_refs[a].at[0], dst_ref=land_refs[a].at[0], send_sem=send[a * N_PEER_CHIPS + k],
                    recv_sem=recv[a * N_PEER_CHIPS + k], device_id=(x, y, 1 - c), device_id_type=MESH)
                cp.wait_send()
                cp.wait_recv()

    res = pl.pallas_call(
        body, name=name,
        out_shape=tuple(pltpu.HBM(b.shape, b.dtype) for b in lands),
        in_specs=[_HBM] * n + [_SEM] * (2 * ns) + [_ANY] * len(after),
        out_specs=tuple([_HBM] * n),
        input_output_aliases={i: i for i in range(n)},
        compiler_params=pltpu.CompilerParams(has_side_effects=_EFFECT),
    )(*lands, *sems, *after)
    return list(res)


def _fill_own(shard2, *, name, after=None):
    _, rows, cols = shard2.shape
    tc = _col_tile(1, rows, cols, itemsize=shard2.dtype.itemsize)
    j = 2 * lax.axis_index("x") + lax.axis_index("y")
    extra = [] if after is None else [after]

    def body(j_ref, s_ref, *rest):
        rest[-1][...] = s_ref[...].astype(MXU)

    return pl.pallas_call(
        body, name=name,
        grid_spec=pltpu.PrefetchScalarGridSpec(
            num_scalar_prefetch=1, grid=(2, cols // tc),
            in_specs=([pl.BlockSpec((None, rows, tc), lambda h, i, jr: (h, 0, i))]
                      + [pl.BlockSpec(memory_space=pl.ANY)] * len(extra)),
            out_specs=pl.BlockSpec((None, rows, tc), lambda h, i, jr: (2 * jr[0] + h, 0, i))),
        out_shape=jax.ShapeDtypeStruct((8, rows, cols), MXU),
        compiler_params=_cp(("parallel", "parallel")),
    )(j.reshape(1).astype(jnp.int32), shard2, *extra)


def _pass_to_sibling(lands, *, name):
    n = len(lands)

    def body(*refs):
        outs = refs[n:2 * n]
        send, recv = refs[2 * n:]
        x, y, c = _coords()
        cps = []
        for a in range(n):
            for k, (px, py) in enumerate(_peer_chips(x, y)):
                slot = 4 * px + 2 * py + c
                cps.append(pltpu.make_async_remote_copy(
                    src_ref=outs[a].at[slot], dst_ref=outs[a].at[slot], send_sem=send.at[a, k], recv_sem=recv.at[a, k],
                    device_id=(x, y, 1 - c), device_id_type=MESH))
        for cp in cps:
            cp.start()
        for cp in cps:
            cp.wait_recv()
        for cp in cps:
            cp.wait_send()

    return pl.pallas_call(
        body, name=name, in_specs=[_ANY] * n, out_specs=[_ANY] * n,
        out_shape=[jax.ShapeDtypeStruct(b.shape, b.dtype) for b in lands],
        input_output_aliases={a: a for a in range(n)},
        scratch_shapes=[pltpu.SemaphoreType.DMA((n, N_PEER_CHIPS)), pltpu.SemaphoreType.DMA((n, N_PEER_CHIPS))],
    )(*lands)


def _sum_parts(parts, got, *, name):
    _, rows, cols = parts.shape
    tc = _col_tile(4, rows, cols, itemsize=parts.dtype.itemsize)
    x, y, c = _coords()
    idx = jnp.stack([2 * x + y, 2 * (1 - x) + y, 2 * x + (1 - y), 2 * (1 - x) + (1 - y), c]).astype(jnp.int32)

    def body(i_ref, p_ref, g0_ref, g1_ref, g2_ref, o_ref):
        o_ref[...] = ((p_ref[...].astype(F32) + g0_ref[...].astype(F32)) + g1_ref[...].astype(F32)) + g2_ref[...].astype(F32)

    slot = lambda s: pl.BlockSpec((None, rows, tc), lambda i, ir, s=s: (ir[s], 0, i))
    return pl.pallas_call(
        body, name=name,
        grid_spec=pltpu.PrefetchScalarGridSpec(
            num_scalar_prefetch=1, grid=(cols // tc,),
            in_specs=[slot(0), slot(1), slot(2), slot(3)],
            out_specs=pl.BlockSpec((None, rows, tc), lambda i, ir: (ir[4], 0, i))),
        out_shape=jax.ShapeDtypeStruct((2, rows, cols), F32),
        compiler_params=_cp(("parallel",)),
    )(idx, parts, got, got, got)


def _sum_slots(t, *, name):
    S, rows, cols = t.shape
    tc = _col_tile(S, rows, cols)

    def body(t_ref, o_ref):
        acc = t_ref[0].astype(F32)
        for s in range(1, S):
            acc = acc + t_ref[s].astype(F32)
        o_ref[...] = acc

    return pl.pallas_call(
        body, name=name, grid=(cols // tc,),
        in_specs=[pl.BlockSpec((S, rows, tc), lambda i: (0, 0, i))],
        out_specs=pl.BlockSpec((rows, tc), lambda i: (0, i)),
        out_shape=jax.ShapeDtypeStruct((rows, cols), F32),
        compiler_params=_cp(("parallel",)),
    )(t)


def _halves_join(bufs, *, name):
    n = len(bufs)

    def body(*refs):
        outs = refs[n:2 * n]
        send, recv = refs[2 * n:]
        x, y, c = _coords()
        cps = [pltpu.make_async_remote_copy(src_ref=outs[a].at[c], dst_ref=outs[a].at[c], send_sem=send.at[a],
                                            recv_sem=recv.at[a], device_id=(x, y, 1 - c), device_id_type=MESH)
               for a in range(n)]
        for cp in cps:
            cp.start()
        for cp in cps:
            cp.wait_recv()
        for cp in cps:
            cp.wait_send()

    return pl.pallas_call(
        body, name=name, in_specs=[_ANY] * n, out_specs=[_ANY] * n,
        out_shape=[jax.ShapeDtypeStruct(b.shape, b.dtype) for b in bufs],
        input_output_aliases={a: a for a in range(n)},
        scratch_shapes=[pltpu.SemaphoreType.DMA((n,)), pltpu.SemaphoreType.DMA((n,))],
    )(*bufs)


def _swap_start(srcs, *, name):
    n = len(srcs)
    lands = [lax.empty((s.shape[0],) + s.shape[2:], s.dtype) for s in srcs]

    def body(*refs):
        src_refs, land_refs = refs[:n], refs[n:2 * n]
        send, recv, token = refs[2 * n:3 * n], refs[3 * n:4 * n], refs[-1]
        x, y, c = _coords()
        for a in range(n):
            pltpu.make_async_remote_copy(src_ref=src_refs[a].at[:, 1 - c], dst_ref=land_refs[a], send_sem=send[a],
                                         recv_sem=recv[a], device_id=(x, y, 1 - c), device_id_type=MESH).start()
        token[...] = jnp.zeros_like(token)

    bufs = list(srcs) + lands
    res = pl.pallas_call(
        body, name=name,
        out_shape=(*[pltpu.SemaphoreType.DMA(())] * (2 * n), *[pltpu.HBM(b.shape, b.dtype) for b in bufs],
                   jax.ShapeDtypeStruct((8, 128), F32)),
        in_specs=[_HBM] * (2 * n),
        out_specs=(*[_SEM] * (2 * n), *[_HBM] * (2 * n), pl.BlockSpec(memory_space=pltpu.VMEM)),
        input_output_aliases={i: 2 * n + i for i in range(2 * n)},
        compiler_params=pltpu.CompilerParams(has_side_effects=_EFFECT),
    )(*[pltpu.with_memory_space_constraint(b, pltpu.HBM) for b in bufs])
    return list(res[:2 * n]), list(res[2 * n:3 * n]), list(res[3 * n:4 * n]), res[-1]


def _swap_wait(sems, srcs, lands, after, *, name):
    n = len(srcs)

    def body(*refs):
        src_refs, land_refs = refs[:n], refs[n:2 * n]
        send, recv = refs[2 * n:3 * n], refs[3 * n:4 * n]
        x, y, c = _coords()
        for a in range(n):
            cp = pltpu.make_async_remote_copy(
                src_ref=src_refs[a].at[:, 0], dst_ref=land_refs[a], send_sem=send[a], recv_sem=recv[a],
                device_id=(x, y, 1 - c), device_id_type=MESH)
            cp.wait_send()
            cp.wait_recv()

    bufs = list(srcs) + list(lands)
    res = pl.pallas_call(
        body, name=name,
        out_shape=tuple(pltpu.HBM(b.shape, b.dtype) for b in bufs),
        in_specs=[_HBM] * (2 * n) + [_SEM] * (2 * n) + [_ANY] * len(after),
        out_specs=tuple([_HBM] * (2 * n)),
        input_output_aliases={i: i for i in range(2 * n)},
        compiler_params=pltpu.CompilerParams(has_side_effects=_EFFECT),
    )(*bufs, *sems, *after)
    return list(res[:n]), list(res[n:])


def _grad_views(grads):
    return [g.reshape(4, 2, g.shape[0] // 8, g.shape[1]) for g in grads]


def _scatter_begin(views, theirs, *, tag):
    parts = [_pair_sum(v, t, name=f"rs_pair_{tag}_{i}") for i, (v, t) in enumerate(zip(views, theirs))]
    got = [lax.empty(p.shape, p.dtype) for p in parts]
    sems, parts, got, token = _split_start(
        parts, got, lambda x, y, c, px, py: 2 * px + py, lambda x, y, c: 2 * x + y, name=f"rs_start_{tag}")
    return (sems, parts, got), token


def _reduce_scatter_begin(grads, *, tag):
    views = _grad_views(grads)
    theirs = _sibling_swap(views, name=f"rs_swap_{tag}")
    return _scatter_begin(views, theirs, tag=tag)


def _reduce_scatter_end(state, after, *, tag):
    sems, parts, got = state
    parts, got = _split_wait(sems, parts, got, after, name=f"rs_wait_{tag}")
    halves = [_sum_parts(p, t, name=f"rs_sum_{tag}_{i}") for i, (p, t) in enumerate(zip(parts, got))]
    joined = _halves_join(halves, name=f"rs_join_{tag}")
    return [j.reshape(2 * j.shape[1], j.shape[2]) for j in joined]


_SMALL = ("norm_w", "ssm_a_re", "ssm_a_im", "ssm_log_dt", "ssm_b_re", "ssm_b_im", "ssm_c_re", "ssm_c_im", "ssm_d",
          "ssm_glu_b", "sg_ln_w", "sg_ln_b", "sg_w", "sg_b", "attn_sinks", "final_norm_w")
_BIG = ("w_in", "ssm_glu_w", "w_branch_a", "w_branch_b", "w_branch_c", "w_out")
_WEIGHTS = ("norm_w", "w_in", "ssm_a_re", "ssm_a_im", "ssm_log_dt", "ssm_b_re", "ssm_b_im", "ssm_c_re", "ssm_c_im",
            "ssm_d", "ssm_glu_w", "ssm_glu_b", "sg_ln_w", "sg_ln_b", "sg_w", "sg_b", "attn_sinks", "w_branch_a",
            "w_branch_b", "w_branch_c", "w_out", "final_norm_w")
_PACK_COLS = 1024
_PACK_ALIGN = 8 * ROW_ALIGN * _PACK_COLS


def _slice_exchange(buf, *, name, after=()):
    def body(in_ref, *rest):
        out_ref, send, recv, lsem = rest[-4:]
        x, y, c = _coords()
        me = 4 * x + 2 * y + c
        own = pltpu.make_async_copy(in_ref.at[me], out_ref.at[me], lsem)
        own.start()
        cps = []
        for k in range(1, 8):
            px, py, pc = x ^ (k >> 2), y ^ ((k >> 1) & 1), c ^ (k & 1)
            cps.append(pltpu.make_async_remote_copy(
                src_ref=in_ref.at[4 * px + 2 * py + pc], dst_ref=out_ref.at[me], send_sem=send.at[k - 1],
                recv_sem=recv.at[k - 1], device_id=(px, py, pc), device_id_type=MESH))
        for cp in cps:
            cp.start()
        for cp in cps:
            cp.wait_recv()
        for cp in cps:
            cp.wait_send()
        own.wait()

    return pl.pallas_call(
        body, name=name, in_specs=[_ANY] * (1 + len(after)), out_specs=_ANY,
        out_shape=jax.ShapeDtypeStruct(buf.shape, buf.dtype),
        scratch_shapes=[pltpu.SemaphoreType.DMA((7,)), pltpu.SemaphoreType.DMA((7,)), pltpu.SemaphoreType.DMA],
    )(buf, *after)


def _allreduce_small(packed, after=()):
    rows, cols = packed.shape
    got = _slice_exchange(packed.reshape(8, rows // 8, cols), name="small_grads_exchange", after=after)
    mine = _sum_slots(got, name="small_grads_sum")
    return _gather8([mine], name="small_grads_gather")[0]


def _pack(ts):
    flat = jnp.concatenate([t.reshape(-1) for t in ts])
    pad = (-flat.shape[0]) % _PACK_ALIGN
    return jnp.pad(flat, (0, pad)).reshape(-1, _PACK_COLS)


def _unpack(buf, like):
    flat = buf.reshape(-1)
    out, pos = [], 0
    for t in like:
        out.append(flat[pos:pos + t.size].reshape(t.shape))
        pos += t.size
    return out


def kernel(x, norm_w, w_in, ssm_a_re, ssm_a_im, ssm_log_dt, ssm_b_re, ssm_b_im, ssm_c_re, ssm_c_im, ssm_d, ssm_glu_w, ssm_glu_b, sg_ln_w, sg_ln_b, sg_w, sg_b, attn_sinks, w_branch_a, w_branch_b, w_branch_c, w_out, final_norm_w, loss_target, m_norm_w, m_w_in, m_ssm_a_re, m_ssm_a_im, m_ssm_log_dt, m_ssm_b_re, m_ssm_b_im, m_ssm_c_re, m_ssm_c_im, m_ssm_d, m_ssm_glu_w, m_ssm_glu_b, m_sg_ln_w, m_sg_ln_b, m_sg_w, m_sg_b, m_attn_sinks, m_w_branch_a, m_w_branch_b, m_w_branch_c, m_w_out, m_final_norm_w, v_norm_w, v_w_in, v_ssm_a_re, v_ssm_a_im, v_ssm_log_dt, v_ssm_b_re, v_ssm_b_im, v_ssm_c_re, v_ssm_c_im, v_ssm_d, v_ssm_glu_w, v_ssm_glu_b, v_sg_ln_w, v_sg_ln_b, v_sg_w, v_sg_b, v_attn_sinks, v_w_branch_a, v_w_branch_b, v_w_branch_c, v_w_out, v_final_norm_w):
    w = dict(norm_w=norm_w, w_in=w_in, ssm_a_re=ssm_a_re, ssm_a_im=ssm_a_im, ssm_log_dt=ssm_log_dt, ssm_b_re=ssm_b_re,
             ssm_b_im=ssm_b_im, ssm_c_re=ssm_c_re, ssm_c_im=ssm_c_im, ssm_d=ssm_d, ssm_glu_w=ssm_glu_w,
             ssm_glu_b=ssm_glu_b, sg_ln_w=sg_ln_w, sg_ln_b=sg_ln_b, sg_w=sg_w, sg_b=sg_b, attn_sinks=attn_sinks,
             w_branch_a=w_branch_a, w_branch_b=w_branch_b, w_branch_c=w_branch_c, w_out=w_out,
             final_norm_w=final_norm_w)
    m = dict(norm_w=m_norm_w, w_in=m_w_in, ssm_a_re=m_ssm_a_re, ssm_a_im=m_ssm_a_im, ssm_log_dt=m_ssm_log_dt,
             ssm_b_re=m_ssm_b_re, ssm_b_im=m_ssm_b_im, ssm_c_re=m_ssm_c_re, ssm_c_im=m_ssm_c_im, ssm_d=m_ssm_d,
             ssm_glu_w=m_ssm_glu_w, ssm_glu_b=m_ssm_glu_b, sg_ln_w=m_sg_ln_w, sg_ln_b=m_sg_ln_b, sg_w=m_sg_w,
             sg_b=m_sg_b, attn_sinks=m_attn_sinks, w_branch_a=m_w_branch_a, w_branch_b=m_w_branch_b,
             w_branch_c=m_w_branch_c, w_out=m_w_out, final_norm_w=m_final_norm_w)
    v = dict(norm_w=v_norm_w, w_in=v_w_in, ssm_a_re=v_ssm_a_re, ssm_a_im=v_ssm_a_im, ssm_log_dt=v_ssm_log_dt,
             ssm_b_re=v_ssm_b_re, ssm_b_im=v_ssm_b_im, ssm_c_re=v_ssm_c_re, ssm_c_im=v_ssm_c_im, ssm_d=v_ssm_d,
             ssm_glu_w=v_ssm_glu_w, ssm_glu_b=v_ssm_glu_b, sg_ln_w=v_sg_ln_w, sg_ln_b=v_sg_ln_b, sg_w=v_sg_w,
             sg_b=v_sg_b, attn_sinks=v_attn_sinks, w_branch_a=v_w_branch_a, w_branch_b=v_w_branch_b,
             w_branch_c=v_w_branch_c, w_out=v_w_out, final_norm_w=v_final_norm_w)

    big_names = ("winT", "glu_w", "wbaT", "wbbT", "wbcT", "w_out")
    L = x.shape[1]
    tabs = _rope_tables(L)
    p = {k: w[k] for k in _SMALL}

    column_sharded = ("w_in", "w_branch_a", "w_branch_b", "w_branch_c")

    def shard_halves(l, names):
        ts = [w[k][l].T if k in column_sharded else w[k][l] for k in names]
        return [t.reshape(2, t.shape[0] // 2, t.shape[1]) for t in ts]

    rows_of = lambda lands: [t.reshape(8 * t.shape[1], t.shape[2]) for t in lands]
    saved = [None] * DEPTH

    land_a = [_fill_own(shard_halves(0, _BIG[:1])[0], name="gather_fill_0_0")]
    sems_a, land_a, token_a = _gather_start(land_a, name="gather_start_0a")
    fill_after = lambda l, names, i0: [_fill_own(s, name=f"gather_fill_{l}_{i0 + i}", after=token_a)
                                       for i, s in enumerate(shard_halves(l, names))]
    lands = [[None] + fill_after(0, _BIG[1:], 1), fill_after(1, _BIG, 0)]
    sp = [_prep_layer(p, l, after=token_a) for l in range(DEPTH)]
    h0 = _rms_fwd(x[0], p["norm_w"][0], name="rms_fwd_0", after=token_a)
    land_a = _gather_wait(sems_a, land_a, [h0, sp[0]["btr"], sp[1]["btr"]] + lands[0][1:] + lands[1],
                          name="gather_wait_0a")
    land_a = _pass_to_sibling(land_a, name="gather_pass_0a")
    sems_b, land_b, token_b = _gather_start(lands[0][1:], name="gather_start_0b", after=land_a)
    split1 = {}

    def start1(main):
        split1["sems"], split1["land"], token1 = _gather_start(lands[1][:1], name="gather_start_1a", after=[main])
        return token1

    def rest(l, sems, land, first):
        def arrived(t):
            got = _pass_to_sibling(_gather_wait(sems, land, t, name=f"gather_wait_{l}b"), name=f"gather_pass_{l}b")
            return dict(zip(big_names, rows_of(first + got))), None
        return arrived

    x1, saved[0], big0 = _layer_fwd(x[0], h0, p, sp[0], rows_of(land_a)[0], rest(0, sems_b, land_b, land_a), 0,
                                    tabs, proj_after=token_b, after_main=start1)
    land_1a = _gather_wait(split1["sems"], split1["land"], [x1], name="gather_wait_1a")
    land_1a = _pass_to_sibling(land_1a, name="gather_pass_1a")
    sems_1b, land_1b, token_1b = _gather_start(lands[1][1:], name="gather_start_1b", after=land_1a)
    h1 = _rms_fwd(x1, p["norm_w"][1], name="rms_fwd_1")
    x2, saved[1], big1 = _layer_fwd(x1, h1, p, sp[1], rows_of(land_1a)[0], rest(1, sems_1b, land_1b, land_1a),
                                    1, tabs, proj_after=token_1b)
    bigs = [big0, big1]
    loss, dx, dfw = _final_loss(x2, p["final_norm_w"], loss_target[0], name="final_loss")

    grads = [None] * DEPTH
    rs = {}

    def early(l):
        def begin(g):
            *rs[f"{l}a"], token_a = _swap_start(_grad_views([g[k] for k in big_names[1:]]), name=f"rs_swap_start_{l}a")
            return token_a
        return begin

    def mid(l):
        def go_on(t):
            sems, views, lands = rs[f"{l}a"]
            views, theirs = _swap_wait(sems, views, lands, [t], name=f"rs_swap_wait_{l}a")
            rs[f"{l}a"], token_a = _scatter_begin(views, theirs, tag=f"{l}a")
            return token_a
        return go_on

    def late(l, dproj, dx):
        if l == 0:
            rs["0b"], token_s = _reduce_scatter_begin([grads[0]["winT"]], tag="0b")
            dh = _proj_bwd_dh(dproj, bigs[0]["winT"], 0, token_s)
            return _rms_bwd(saved[0]["x"], p["norm_w"][0], dh, dx, name="rms_bwd_0")
        sems, views, lands, token_b = _swap_start(_grad_views([grads[l]["winT"]]), name=f"rs_swap_start_{l}b")
        dh = _proj_bwd_dh(dproj, bigs[l]["winT"], l, token_b)
        views, theirs = _swap_wait(sems, views, lands, [dh], name=f"rs_swap_wait_{l}b")
        rs[f"{l}b"], token_s = _scatter_begin(views, theirs, tag=f"{l}b")
        return _rms_bwd(saved[l]["x"], p["norm_w"][l], dh, dx, name=f"rms_bwd_{l}", after=token_s)

    def reduced(l, after):
        return _reduce_scatter_end(rs[f"{l}b"], after, tag=f"{l}b") + _reduce_scatter_end(rs[f"{l}a"], after, tag=f"{l}a")

    dproj, grads[1] = _layer_bwd(dx, saved[1], p, bigs[1], 1, tabs, early(1), mid(1))
    dx, grads[1]["norm_w"] = late(1, dproj, dx)
    dproj, grads[0] = _layer_bwd(dx, saved[0], p, bigs[0], 0, tabs, early(0), mid(0))
    dx, grads[0]["norm_w"] = late(0, dproj, dx)
    red1 = reduced(1, [dx])

    tr = lambda t: t.transpose(0, 2, 1)
    view = {k: (tr if k == "w_in" else (lambda t: t)) for k in _BIG}
    shard_grads = lambda red: dict(zip(_BIG, (red[0], red[1], red[2].T, red[3].T, red[4].T, red[5])))
    outs = {k: None for k in _BIG}

    def adamw_big(l, red):
        for k, g in shard_grads(red).items():
            outs[k] = _adamw_layer(view[k](w[k]), g, view[k](m[k]), view[k](v[k]), l, outs[k], name=f"adamw_{k}_{l}")

    adamw_big(1, red1)

    small_like = [w[k] for k in _SMALL]
    gs = [jnp.stack([grads[l][k] for l in range(DEPTH)]) if k != "final_norm_w" else dfw for k in _SMALL]
    gsum = _allreduce_small(_pack(gs + [loss.reshape(1)]), after=[outs[k][0] for k in _BIG])
    adamw_big(0, reduced(0, [gsum]))

    gfull, delta, new_m, new_v = {}, {}, {}, {}
    for k in _BIG:
        gfull[k], delta[k], new_m[k], new_v[k] = (view[k](t) for t in outs[k])
    *small_sums, loss = _unpack(gsum, small_like + [loss])
    for k, t in zip(_SMALL, small_sums):
        gfull[k] = t
        delta[k], new_m[k], new_v[k] = _adamw(w[k], t, m[k], v[k], name=f"adamw_{k}")

    return (loss, dx[None], *[gfull[k] for k in _WEIGHTS], *[delta[k] for k in _WEIGHTS],
            *[new_m[k] for k in _WEIGHTS], *[new_v[k] for k in _WEIGHTS])
```

```python
import functools
import math

import numpy as np
import jax
import jax.numpy as jnp
from jax import lax
from jax.experimental import pallas as pl
from jax.experimental.pallas import tpu as pltpu

F32 = jnp.float32
MXU = jnp.bfloat16
HIGHEST = lax.Precision.HIGHEST

D_MODEL = 2048
DEPTH = 2
EPS = 1e-6
NEG_INF = -1e30
SSM_WIDTH = 1024
SSM_GROUP = 16
SSM_GROUPS = 64
SSM_STATE = 64
SSM_CH = SSM_GROUPS * SSM_STATE
SLAB = 128
SLAB_CH = (SLAB // SSM_GROUP) * SSM_STATE
N_SLAB = SSM_WIDTH // SLAB
SCAN_SEG = 8
SCAN_STEPS = 4
SG_HEADS = 8
SG_CHUNK = 128
HEAD_DIM = 64
ATT_HEADS = 16
ATT_KV_HEADS = 2
GQA_GROUP = 8
ATT_BLOCK = 128
WINDOW = 128
ROT_DIM = 16
ROPE_THETA = 500000.0
N_MAIN = 6400
N_ZC = 1024
N_GATES = 6144
D_IN = N_MAIN + N_ZC + N_GATES

ADAM_LR = 0.001
ADAM_B1 = 0.9
ADAM_B2 = 0.999
ADAM_EPS = 1e-08
ADAM_WD = 0.01
ADAM_STEP = 10

_DIMS = {"nn": (((1,), (0,)), ((), ())), "nt": (((1,), (1,)), ((), ())), "tn": (((0,), (0,)), ((), ()))}
_MB = 1024 * 1024


def _cp(sem, vmem_mb=48):
    return pltpu.CompilerParams(dimension_semantics=sem, vmem_limit_bytes=vmem_mb * _MB)


def _dot(a, b, mode):
    return lax.dot_general(a.astype(MXU), b.astype(MXU), _DIMS[mode], preferred_element_type=F32)


@jax.custom_vjp
def _mm_nn(a, b):
    return _dot(a, b, "nn")


def _mm_nn_fwd(a, b):
    return _dot(a, b, "nn"), (a, b)


def _mm_nn_bwd(res, g):
    a, b = res
    return _dot(g, b, "nt"), _dot(a, g, "tn")


_mm_nn.defvjp(_mm_nn_fwd, _mm_nn_bwd)


@jax.custom_vjp
def _mm_nt(a, bt):
    return _dot(a, bt, "nt")


def _mm_nt_fwd(a, bt):
    return _dot(a, bt, "nt"), (a, bt)


def _mm_nt_bwd(res, g):
    a, bt = res
    return _dot(g, bt, "nn"), _dot(g, a, "tn")


_mm_nt.defvjp(_mm_nt_fwd, _mm_nt_bwd)


def _rmsnorm(x, w):
    return x * lax.rsqrt(jnp.mean(x * x, axis=-1, keepdims=True) + EPS) * w


def _layernorm(x, w, b):
    mu = jnp.mean(x, axis=-1, keepdims=True)
    var = jnp.mean(jnp.square(x - mu), axis=-1, keepdims=True)
    return (x - mu) * lax.rsqrt(var + EPS) * w + b


def _silu(x):
    return x * jax.nn.sigmoid(x)


def _matmul(a, b, mode, *, name, shape, tm, tn, tk, out_dtype=F32, add=None, a_off=(0, 0), b_off=(0, 0), after=None):
    m, n, k = shape
    tm, tn, tk = min(tm, m), min(tn, n), min(tk, k)
    assert m % tm == 0 and n % tn == 0 and k % tk == 0, (name, shape, tm, tn, tk)
    nk = k // tk
    has_add, has_after = add is not None, after is not None

    def body(*refs):
        a_ref, b_ref = refs[0], refs[1]
        pos = 2
        add_ref = None
        if has_add:
            add_ref = refs[pos]
            pos += 1
        if has_after:
            pos += 1
        o_ref = refs[pos]
        p = _dot(a_ref[...], b_ref[...], mode)
        if nk == 1:
            if has_add:
                p = p + add_ref[...].astype(F32)
            o_ref[...] = p.astype(out_dtype)
            return
        acc_ref = refs[pos + 1]
        kk = pl.program_id(2)

        @pl.when(kk == 0)
        def _():
            acc_ref[...] = p

        @pl.when(kk > 0)
        def _():
            acc_ref[...] += p

        @pl.when(kk == nk - 1)
        def _():
            r = acc_ref[...]
            if has_add:
                r = r + add_ref[...].astype(F32)
            o_ref[...] = r.astype(out_dtype)

    a0, a1 = a_off
    b0, b1 = b_off
    if mode == "tn":
        a_spec = pl.BlockSpec((tk, tm), lambda i, j, kk: (kk + a0, i + a1))
    else:
        a_spec = pl.BlockSpec((tm, tk), lambda i, j, kk: (i + a0, kk + a1))
    if mode == "nt":
        b_spec = pl.BlockSpec((tn, tk), lambda i, j, kk: (j + b0, kk + b1))
    else:
        b_spec = pl.BlockSpec((tk, tn), lambda i, j, kk: (kk + b0, j + b1))
    in_specs = [a_spec, b_spec]
    args = [a, b]
    if has_add:
        in_specs.append(pl.BlockSpec((tm, tn), lambda i, j, kk: (i, j)))
        args.append(add)
    if has_after:
        in_specs.append(pl.BlockSpec(memory_space=pl.ANY))
        args.append(after)
    return pl.pallas_call(
        body, name=name, grid=(m // tm, n // tn, nk),
        in_specs=in_specs,
        out_specs=pl.BlockSpec((tm, tn), lambda i, j, kk: (i, j)),
        out_shape=jax.ShapeDtypeStruct((m, n), out_dtype),
        scratch_shapes=[pltpu.VMEM((tm, tn), F32)] if nk > 1 else [],
        compiler_params=_cp(("parallel", "parallel", "arbitrary")),
    )(*args)


def _rms_fwd(x, w, *, name, tm=256, after=None):
    L, d = x.shape
    tm = min(tm, L)
    extra = [] if after is None else [after]

    def body(x_ref, w_ref, *rest):
        rest[-1][...] = _rmsnorm(x_ref[...], w_ref[...]).astype(MXU)

    return pl.pallas_call(
        body, name=name, grid=(L // tm,),
        in_specs=([pl.BlockSpec((tm, d), lambda i: (i, 0)), pl.BlockSpec((1, d), lambda i: (0, 0))]
                  + [pl.BlockSpec(memory_space=pl.ANY)] * len(extra)),
        out_specs=pl.BlockSpec((tm, d), lambda i: (i, 0)),
        out_shape=jax.ShapeDtypeStruct((L, d), MXU),
        compiler_params=_cp(("parallel",)),
    )(x, w.reshape(1, d), *extra)


def _rms_bwd(x, w, dh, dxn, *, name, tm=256, after=None):
    L, d = x.shape
    tm = min(tm, L)
    extra = [] if after is None else [after]

    def body(x_ref, w_ref, dh_ref, dxn_ref, *rest):
        dx_ref, dw_ref = rest[-2:]
        _, vjp = jax.vjp(_rmsnorm, x_ref[...], w_ref[...])
        dx, dw = vjp(dh_ref[...])
        dx_ref[...] = dx + dxn_ref[...]

        @pl.when(pl.program_id(0) == 0)
        def _():
            dw_ref[...] = jnp.zeros_like(dw_ref)

        dw_ref[...] += dw

    row = pl.BlockSpec((tm, d), lambda i: (i, 0))
    vec = pl.BlockSpec((1, d), lambda i: (0, 0))
    dx, dw = pl.pallas_call(
        body, name=name, grid=(L // tm,),
        in_specs=[row, vec, row, row] + [pl.BlockSpec(memory_space=pl.ANY)] * len(extra), out_specs=[row, vec],
        out_shape=[jax.ShapeDtypeStruct((L, d), F32), jax.ShapeDtypeStruct((1, d), F32)],
        compiler_params=_cp(("arbitrary",)),
    )(x, w.reshape(1, d), dh, dxn, *extra)
    return dx, dw.reshape(d)


def _final_loss(x, w, tgt, *, name, tm=256):
    L, d = x.shape
    tm = min(tm, L)

    def loss_fn(xv, wv, tv):
        err = jnp.square(_rmsnorm(xv, wv) - tv)
        return 0.5 * jnp.sum(jnp.mean(err, axis=-1, keepdims=True), axis=0, keepdims=True)

    def body(x_ref, w_ref, t_ref, loss_ref, dx_ref, dw_ref):
        tv = t_ref[...]
        val, vjp = jax.vjp(lambda xv, wv: loss_fn(xv, wv, tv), x_ref[...], w_ref[...])
        dx, dw = vjp(jnp.ones((1, 1), F32))
        dx_ref[...] = dx

        @pl.when(pl.program_id(0) == 0)
        def _():
            dw_ref[...] = jnp.zeros_like(dw_ref)
            loss_ref[...] = jnp.zeros_like(loss_ref)

        dw_ref[...] += dw
        loss_ref[...] += jnp.broadcast_to(val, loss_ref.shape)

    row = pl.BlockSpec((tm, d), lambda i: (i, 0))
    vec = pl.BlockSpec((1, d), lambda i: (0, 0))
    loss, dx, dw = pl.pallas_call(
        body, name=name, grid=(L // tm,),
        in_specs=[row, vec, row],
        out_specs=[pl.BlockSpec((8, 128), lambda i: (0, 0)), row, vec],
        out_shape=[jax.ShapeDtypeStruct((8, 128), F32), jax.ShapeDtypeStruct((L, d), F32),
                   jax.ShapeDtypeStruct((1, d), F32)],
        compiler_params=_cp(("arbitrary",)),
    )(x, w.reshape(1, d), tgt)
    return loss[0, 0], dx, dw.reshape(d)


PARAM_ROWS = 512


def _s5_param_fn(are, aim, ldt, bre, bim, row0):
    n = are.shape[0]
    grp = (row0 + lax.broadcasted_iota(jnp.int32, (n, SSM_GROUPS), 0)) // SSM_STATE
    col = lax.broadcasted_iota(jnp.int32, (n, SSM_GROUPS), 1)
    sel = (grp == col).astype(F32)
    dt = jnp.sum(sel * jnp.exp(ldt), axis=-1, keepdims=True)
    mag = jnp.exp(are * dt)
    ang = aim * dt
    lbr = mag * jnp.cos(ang)
    lbi = mag * jnp.sin(ang)
    den = are * are + aim * aim
    nr = lbr - 1.0
    kr = (nr * are + lbi * aim) / den
    ki = (lbi * are - nr * aim) / den
    return lbr, lbi, kr * bre - ki * bim, kr * bim + ki * bre


def _s5_param_specs():
    col = pl.BlockSpec((PARAM_ROWS, 1), lambda i: (i, 0))
    mat = pl.BlockSpec((PARAM_ROWS, SSM_GROUP), lambda i: (i, 0))
    vec = pl.BlockSpec((1, SSM_GROUPS), lambda i: (0, 0))
    return col, mat, vec


def _s5_params_fwd(are, aim, ldt, bre, bim, *, name, after=None):
    n = are.shape[0]
    col, mat, vec = _s5_param_specs()
    extra = [] if after is None else [after]

    def body(are_ref, aim_ref, ldt_ref, bre_ref, bim_ref, *rest):
        lbr_ref, lbi_ref, bbr_ref, bbi_ref = rest[-4:]
        row0 = pl.program_id(0) * PARAM_ROWS
        lbr, lbi, bbr, bbi = _s5_param_fn(are_ref[...], aim_ref[...], ldt_ref[...], bre_ref[...], bim_ref[...], row0)
        lbr_ref[...] = lbr
        lbi_ref[...] = lbi
        bbr_ref[...] = bbr
        bbi_ref[...] = bbi

    cshape = jax.ShapeDtypeStruct((n, 1), F32)
    mshape = jax.ShapeDtypeStruct((n, SSM_GROUP), F32)
    return pl.pallas_call(body, name=name, grid=(n // PARAM_ROWS,),
                          in_specs=[col, col, vec, mat, mat] + [pl.BlockSpec(memory_space=pl.ANY)] * len(extra),
                          out_specs=[col, col, mat, mat], out_shape=[cshape, cshape, mshape, mshape],
                          compiler_params=_cp(("parallel",)))(are, aim, ldt, bre, bim, *extra)


def _s5_params_bwd(are, aim, ldt, bre, bim, dlbr, dlbi, dbbr, dbbi, *, name):
    n = are.shape[0]
    col, mat, vec = _s5_param_specs()

    def body(are_ref, aim_ref, ldt_ref, bre_ref, bim_ref, g0, g1, g2, g3, o0, o1, o2, o3, o4):
        row0 = pl.program_id(0) * PARAM_ROWS
        _, vjp = jax.vjp(lambda a, b, c, d, e: _s5_param_fn(a, b, c, d, e, row0),
                         are_ref[...], aim_ref[...], ldt_ref[...], bre_ref[...], bim_ref[...])
        dare, daim, dldt, dbre, dbim = vjp((g0[...], g1[...], g2[...], g3[...]))
        o0[...] = dare
        o1[...] = daim
        o3[...] = dbre
        o4[...] = dbim

        @pl.when(pl.program_id(0) == 0)
        def _():
            o2[...] = jnp.zeros_like(o2)

        o2[...] += dldt

    cshape = jax.ShapeDtypeStruct((n, 1), F32)
    mshape = jax.ShapeDtypeStruct((n, SSM_GROUP), F32)
    return pl.pallas_call(body, name=name, grid=(n // PARAM_ROWS,),
                          in_specs=[col, col, vec, mat, mat, col, col, mat, mat],
                          out_specs=[col, col, vec, mat, mat],
                          out_shape=[cshape, cshape, jax.ShapeDtypeStruct((1, SSM_GROUPS), F32), mshape, mshape],
                          compiler_params=_cp(("arbitrary",)))(are, aim, ldt, bre, bim, dlbr, dlbi, dbbr, dbbi)


SLAB_NC = SLAB_CH // 128


def _s5_specs(L):
    slab = pl.BlockSpec((L, SLAB), lambda s: (0, s))
    wspec = pl.BlockSpec((SLAB_NC, 128, SLAB), lambda s: (s, 0, 0))
    lspec = pl.BlockSpec((SLAB_NC, 1, 128), lambda s: (s, 0, 0))
    sspec = pl.BlockSpec((SLAB_NC, L, 128), lambda s: (s, 0, 0))
    dspec = pl.BlockSpec((1, SLAB), lambda s: (0, s))
    return slab, wspec, lspec, sspec, dspec


def _scan_inplace(sr_ref, si_ref, lr, li, pr_ref, pi_ref, *, reverse):
    NC, L, W = sr_ref.shape
    S = SCAN_SEG
    T = L // S
    lr8 = [jnp.broadcast_to(lr[k], (S, W)) for k in range(NC)]
    li8 = [jnp.broadcast_to(li[k], (S, W)) for k in range(NC)]

    def tiles(first, count):
        return pl.ds(first * S, count * S)

    for k in range(NC):
        pr_ref[k, tiles(T - 1 if reverse else 0, 1), :] = lr8[k]
        pi_ref[k, tiles(T - 1 if reverse else 0, 1), :] = li8[k]
        n = 1
        while n < T:
            have = tiles(T - n, n) if reverse else tiles(0, n)
            new = tiles(T - 2 * n, n) if reverse else tiles(n, n)
            top = tiles(T - n, 1) if reverse else tiles(n - 1, 1)
            ar, ai = pr_ref[k, top, :][None], pi_ref[k, top, :][None]
            hr, hi = pr_ref[k, have, :].reshape(n, S, W), pi_ref[k, have, :].reshape(n, S, W)
            pr_ref[k, new, :] = (hr * ar - hi * ai).reshape(n * S, W)
            pi_ref[k, new, :] = (hr * ai + hi * ar).reshape(n * S, W)
            n *= 2

    def step(i, carry):
        for u in range(SCAN_STEPS):
            jj = i * SCAN_STEPS + u
            rows = pl.ds(pl.multiple_of(((T - 1 - jj) if reverse else jj) * S, S), S)
            out = []
            for k in range(NC):
                sr, si = carry[k]
                nsr = lr8[k] * sr - li8[k] * si + sr_ref[k, rows, :]
                nsi = lr8[k] * si + li8[k] * sr + si_ref[k, rows, :]
                sr_ref[k, rows, :] = nsr
                si_ref[k, rows, :] = nsi
                out.append((nsr, nsi))
            carry = tuple(out)
        return carry

    zero = jnp.zeros((S, W), F32)
    ends = lax.fori_loop(0, T // SCAN_STEPS, step, tuple((zero, zero) for k in range(NC)))
    sub = lax.broadcasted_iota(jnp.int32, (S, W), 0)
    order = range(S - 1, -1, -1) if reverse else range(S)
    for k in range(NC):
        er, ei = ends[k]
        full = tiles(0 if reverse else T - 1, 1)
        ltr = pr_ref[k, full, :][0:1]
        lti = pi_ref[k, full, :][0:1]
        cr = jnp.zeros((1, W), F32)
        ci = jnp.zeros((1, W), F32)
        ctr = jnp.zeros((S, W), F32)
        cti = jnp.zeros((S, W), F32)
        for seg in order:
            ctr = jnp.where(sub == seg, cr, ctr)
            cti = jnp.where(sub == seg, ci, cti)
            cr, ci = (er[seg:seg + 1, :] + ltr * cr - lti * ci, ei[seg:seg + 1, :] + ltr * ci + lti * cr)
        pr = pr_ref[k].reshape(T, S, W)
        pi = pi_ref[k].reshape(T, S, W)
        sr_ref[k] += (pr * ctr[None] - pi * cti[None]).reshape(L, W)
        si_ref[k] += (pr * cti[None] + pi * ctr[None]).reshape(L, W)


def _time_interleave(a):
    L, W = a.shape
    return a.reshape(SCAN_SEG, L // SCAN_SEG, W).transpose(1, 0, 2).reshape(L, W)


def _time_deinterleave(a):
    L, W = a.shape
    return a.reshape(L // SCAN_SEG, SCAN_SEG, W).transpose(1, 0, 2).reshape(L, W)


def _s5_fwd(u, btr, bti, cbr, cbi, lbr, lbi, dvec, *, name, after=None):
    L = u.shape[0]
    extra = [] if after is None else [after]

    def body(u_ref, btr_ref, bti_ref, cbr_ref, cbi_ref, lr_ref, li_ref, d_ref, *rest):
        ys_ref, sr_ref, si_ref, pr_ref, pi_ref = rest[-5:]
        u = u_ref[...]
        for k in range(SLAB_NC):
            sr_ref[k] = _dot(u, btr_ref[k], "nt")
            si_ref[k] = _dot(u, bti_ref[k], "nt")
        _scan_inplace(sr_ref, si_ref, lr_ref[...], li_ref[...], pr_ref, pi_ref, reverse=False)
        ys = d_ref[...] * u
        for k in range(SLAB_NC):
            ys = ys + _dot(sr_ref[k], cbr_ref[k], "nn") - _dot(si_ref[k], cbi_ref[k], "nn")
        ys_ref[...] = ys

    slab, wspec, lspec, sspec, dspec = _s5_specs(L)
    sshape = jax.ShapeDtypeStruct((N_SLAB * SLAB_NC, L, 128), F32)
    return pl.pallas_call(
        body, name=name, grid=(N_SLAB,),
        in_specs=[slab, wspec, wspec, wspec, wspec, lspec, lspec, dspec] + [pl.BlockSpec(memory_space=pl.ANY)] * len(extra),
        out_specs=[slab, sspec, sspec],
        out_shape=[jax.ShapeDtypeStruct((L, SSM_WIDTH), F32), sshape, sshape],
        scratch_shapes=[pltpu.VMEM((SLAB_NC, L, 128), F32), pltpu.VMEM((SLAB_NC, L, 128), F32)],
        compiler_params=_cp(("parallel",), 56),
    )(u, btr, bti, cbr, cbi, lbr, lbi, dvec, *extra)


def _s5_bwd(dys, u, sr, si, btr, bti, cbr, cbi, lbr, lbi, dvec, *, name, after=None):
    L = u.shape[0]
    S = SCAN_SEG
    extra = [] if after is None else [after]

    def body(dys_ref, u_ref, sr_ref, si_ref, btr_ref, bti_ref, cbr_ref, cbi_ref, lr_ref, li_ref, d_ref, *rest):
        (du_ref, dbtr_ref, dbti_ref, dcbr_ref, dcbi_ref, dlr_ref, dli_ref, dd_ref,
         ar_ref, ai_ref, pr_ref, pi_ref) = rest[-12:]
        dys = dys_ref[...]
        u = u_ref[...]
        for k in range(SLAB_NC):
            ar_ref[k] = _dot(dys, cbr_ref[k], "nt")
            ai_ref[k] = -_dot(dys, cbi_ref[k], "nt")
        _scan_inplace(ar_ref, ai_ref, lr_ref[...], -li_ref[...], pr_ref, pi_ref, reverse=True)
        head = lax.broadcasted_iota(jnp.int32, (L, 1), 0) < S
        sub0 = lax.broadcasted_iota(jnp.int32, (S, 1), 0) == 0

        def prev_state(s):
            up = pltpu.roll(s, S, 0)
            return jnp.where(head, 0.0, up), jnp.where(sub0, 0.0, pltpu.roll(up[0:S], 1, 0))

        du = d_ref[...] * dys
        for k in range(SLAB_NC):
            a_re = ar_ref[k]
            a_im = ai_ref[k]
            du = du + _dot(a_re, btr_ref[k], "nn") + _dot(a_im, bti_ref[k], "nn")
            dbtr_ref[k] = _dot(a_re, u, "tn")
            dbti_ref[k] = _dot(a_im, u, "tn")
            s_re = sr_ref[k]
            s_im = si_ref[k]
            dcbr_ref[k] = _dot(s_re, dys, "tn")
            dcbi_ref[k] = -_dot(s_im, dys, "tn")
            p_re, q_re = prev_state(s_re)
            p_im, q_im = prev_state(s_im)
            b_re, b_im = a_re[0:S], a_im[0:S]
            dlr_ref[k] = (jnp.sum(p_re * a_re + p_im * a_im, axis=0, keepdims=True)
                          + jnp.sum(q_re * b_re + q_im * b_im, axis=0, keepdims=True))
            dli_ref[k] = (jnp.sum(p_re * a_im - p_im * a_re, axis=0, keepdims=True)
                          + jnp.sum(q_re * b_im - q_im * b_re, axis=0, keepdims=True))
        du_ref[...] = du
        dd_ref[...] = jnp.sum(dys * u, axis=0, keepdims=True)

    slab, wspec, lspec, sspec, dspec = _s5_specs(L)
    wshape = jax.ShapeDtypeStruct((N_SLAB * SLAB_NC, 128, SLAB), F32)
    lshape = jax.ShapeDtypeStruct((N_SLAB * SLAB_NC, 1, 128), F32)
    return pl.pallas_call(
        body, name=name, grid=(N_SLAB,),
        in_specs=([slab, slab, sspec, sspec, wspec, wspec, wspec, wspec, lspec, lspec, dspec]
                  + [pl.BlockSpec(memory_space=pl.ANY)] * len(extra)),
        out_specs=[slab, wspec, wspec, wspec, wspec, lspec, lspec, dspec],
        out_shape=[jax.ShapeDtypeStruct((L, SSM_WIDTH), F32), wshape, wshape, wshape, wshape, lshape, lshape,
                   jax.ShapeDtypeStruct((1, SSM_WIDTH), F32)],
        scratch_shapes=[pltpu.VMEM((SLAB_NC, L, 128), F32)] * 4,
        compiler_params=_cp(("parallel",), 56),
    )(dys, u, sr, si, btr, bti, cbr, cbi, lbr, lbi, dvec, *extra)


_SLAB_MASK = (np.arange(SLAB_CH)[:, None] // SSM_STATE == np.arange(SLAB)[None, :] // SSM_GROUP)


def _expand_bd(x):
    t = jnp.tile(x.reshape(N_SLAB, SLAB_CH, SSM_GROUP), (1, 1, SLAB // SSM_GROUP))
    return jnp.where(_SLAB_MASK[None], t, 0.0).astype(MXU).reshape(N_SLAB * SLAB_NC, 128, SLAB)


def _contract_bd(dx):
    t = jnp.where(_SLAB_MASK[None], dx.reshape(N_SLAB, SLAB_CH, SLAB), 0.0)
    return jnp.sum(t.reshape(N_SLAB, SLAB_CH, SLAB // SSM_GROUP, SSM_GROUP), axis=2).reshape(SSM_CH, SSM_GROUP)


def _glu_ew(ys, zlin, za):
    a1 = jax.nn.gelu(ys)
    return a1 * jax.nn.sigmoid(zlin) * _silu(za)


def _glu_fwd(ys, main, gw, gb, *, name, tm=256):
    L = ys.shape[0]
    tm = min(tm, L)
    W = SSM_WIDTH

    def body(ys_ref, za_ref, gw_ref, gb_ref, ya_ref):
        ys = ys_ref[...]
        a1 = jax.nn.gelu(ys)
        zlin = _dot(a1, gw_ref[...], "nn") + gb_ref[...]
        ya_ref[...] = _glu_ew(ys, zlin, za_ref[...]).astype(MXU)

    return pl.pallas_call(
        body, name=name, grid=(L // tm,),
        in_specs=[pl.BlockSpec((tm, W), lambda i: (i, 0)), pl.BlockSpec((tm, W), lambda i: (i, 1)),
                  pl.BlockSpec((W, W), lambda i: (0, 0)), pl.BlockSpec((1, W), lambda i: (0, 0))],
        out_specs=pl.BlockSpec((tm, W), lambda i: (i, 0)),
        out_shape=jax.ShapeDtypeStruct((L, W), MXU),
        compiler_params=_cp(("parallel",)),
    )(ys, main, gw, gb.reshape(1, W))


def _glu_bwd(dya, ys, main, gw, gb, *, name, tm=256):
    L = ys.shape[0]
    tm = min(tm, L)
    W = SSM_WIDTH

    def body(dya_ref, ys_ref, za_ref, gw_ref, gb_ref, dys_ref, dza_ref, a1_ref, dzl_ref, db_ref):
        ys = ys_ref[...]
        a1, gelu_vjp = jax.vjp(jax.nn.gelu, ys)
        zlin = _dot(a1, gw_ref[...], "nn") + gb_ref[...]
        _, vjp = jax.vjp(lambda a, z, za: a * jax.nn.sigmoid(z) * _silu(za), a1, zlin, za_ref[...])
        da1, dzlin, dza = vjp(dya_ref[...].astype(F32))
        da1 = da1 + _dot(dzlin, gw_ref[...], "nt")
        dys_ref[...] = gelu_vjp(da1)[0]
        dza_ref[...] = dza
        a1_ref[...] = a1.astype(MXU)
        dzl_ref[...] = dzlin.astype(MXU)

        @pl.when(pl.program_id(0) == 0)
        def _():
            db_ref[...] = jnp.zeros_like(db_ref)

        db_ref[...] += jnp.sum(dzlin, axis=0, keepdims=True)

    row = pl.BlockSpec((tm, W), lambda i: (i, 0))
    vec = pl.BlockSpec((1, W), lambda i: (0, 0))
    return pl.pallas_call(
        body, name=name, grid=(L // tm,),
        in_specs=[row, row, pl.BlockSpec((tm, W), lambda i: (i, 1)), pl.BlockSpec((W, W), lambda i: (0, 0)), vec],
        out_specs=[row, row, row, row, vec],
        out_shape=[jax.ShapeDtypeStruct((L, W), F32), jax.ShapeDtypeStruct((L, W), F32),
                   jax.ShapeDtypeStruct((L, W), MXU), jax.ShapeDtypeStruct((L, W), MXU),
                   jax.ShapeDtypeStruct((1, W), F32)],
        compiler_params=_cp(("arbitrary",)),
    )(dya, ys, main, gw, gb.reshape(1, W))


def _sg_fn(ub, vb, zb, lnw, lnb, ws, bs):
    u = jax.nn.gelu(ub)
    v = _layernorm(jax.nn.gelu(vb), lnw, lnb)
    r = lax.broadcasted_iota(jnp.int32, (SG_CHUNK, SG_CHUNK), 0)
    c = lax.broadcasted_iota(jnp.int32, (SG_CHUNK, SG_CHUNK), 1)
    tri = r >= c
    outs = []
    for h in range(SG_HEADS):
        wh = jnp.where(tri, ws[h], 0.0)
        outs.append(_mm_nn(wh, v[:, h * 128:(h + 1) * 128]) + bs[h])
    mixed = jnp.concatenate(outs, axis=1)
    return u * mixed * _silu(zb)


def _sg_specs(L):
    W = SSM_WIDTH
    blk = lambda c: pl.BlockSpec((SG_CHUNK, W), lambda i, c=c: (i, c))
    vec = pl.BlockSpec((1, W), lambda i: (0, 0))
    wspec = pl.BlockSpec((SG_HEADS, SG_CHUNK, SG_CHUNK), lambda i: (0, 0, 0))
    bspec = pl.BlockSpec((SG_HEADS, SG_CHUNK, 1), lambda i: (0, 0, 0))
    return blk, vec, wspec, bspec


def _sg_fwd(main, lnw, lnb, sgw, sgb, *, name):
    L = main.shape[0]
    W = SSM_WIDTH
    blk, vec, wspec, bspec = _sg_specs(L)

    def body(ub_ref, vb_ref, zb_ref, lnw_ref, lnb_ref, w_ref, b_ref, yb_ref):
        ws = [w_ref[h] for h in range(SG_HEADS)]
        bs = [b_ref[h] for h in range(SG_HEADS)]
        yb_ref[...] = _sg_fn(ub_ref[...], vb_ref[...], zb_ref[...], lnw_ref[...], lnb_ref[...], ws, bs).astype(MXU)

    return pl.pallas_call(
        body, name=name, grid=(L // SG_CHUNK,),
        in_specs=[blk(2), blk(3), blk(4), vec, vec, wspec, bspec],
        out_specs=pl.BlockSpec((SG_CHUNK, W), lambda i: (i, 0)),
        out_shape=jax.ShapeDtypeStruct((L, W), MXU),
        compiler_params=_cp(("parallel",)),
    )(main, main, main, lnw.reshape(1, W), lnb.reshape(1, W), sgw, sgb.reshape(SG_HEADS, SG_CHUNK, 1))


def _sg_bwd(dyb, main, lnw, lnb, sgw, sgb, *, name):
    L = main.shape[0]
    W = SSM_WIDTH
    blk, vec, wspec, bspec = _sg_specs(L)

    def body(dyb_ref, ub_ref, vb_ref, zb_ref, lnw_ref, lnb_ref, w_ref, b_ref,
             dub_ref, dvb_ref, dzb_ref, dlnw_ref, dlnb_ref, dw_ref, db_ref):
        ws = [w_ref[h] for h in range(SG_HEADS)]
        bs = [b_ref[h] for h in range(SG_HEADS)]
        _, vjp = jax.vjp(_sg_fn, ub_ref[...], vb_ref[...], zb_ref[...], lnw_ref[...], lnb_ref[...], ws, bs)
        dub, dvb, dzb, dlnw, dlnb, dws, dbs = vjp(dyb_ref[...])

        @pl.when(pl.program_id(0) == 0)
        def _():
            dlnw_ref[...] = jnp.zeros_like(dlnw_ref)
            dlnb_ref[...] = jnp.zeros_like(dlnb_ref)
            dw_ref[...] = jnp.zeros_like(dw_ref)
            db_ref[...] = jnp.zeros_like(db_ref)

        dub_ref[...] = dub
        dvb_ref[...] = dvb
        dzb_ref[...] = dzb
        dlnw_ref[...] += dlnw
        dlnb_ref[...] += dlnb
        for h in range(SG_HEADS):
            dw_ref[h] += dws[h]
            db_ref[h] += dbs[h]

    row = pl.BlockSpec((SG_CHUNK, W), lambda i: (i, 0))
    out = jax.ShapeDtypeStruct((L, W), F32)
    return pl.pallas_call(
        body, name=name, grid=(L // SG_CHUNK,),
        in_specs=[row, blk(2), blk(3), blk(4), vec, vec, wspec, bspec],
        out_specs=[row, row, row, vec, vec, wspec, bspec],
        out_shape=[out, out, out, jax.ShapeDtypeStruct((1, W), F32), jax.ShapeDtypeStruct((1, W), F32),
                   jax.ShapeDtypeStruct((SG_HEADS, SG_CHUNK, SG_CHUNK), F32),
                   jax.ShapeDtypeStruct((SG_HEADS, SG_CHUNK, 1), F32)],
        compiler_params=_cp(("arbitrary",)),
    )(dyb, main, main, main, lnw.reshape(1, W), lnb.reshape(1, W), sgw, sgb.reshape(SG_HEADS, SG_CHUNK, 1))


def _rope_tables(L):
    half = ROT_DIM // 2
    inv_freq = ROPE_THETA ** (-jnp.arange(0, ROT_DIM, 2, dtype=F32) / ROT_DIM)
    ang = jnp.arange(L, dtype=F32)[:, None] * inv_freq[None, :]
    cos = jnp.cos(ang)
    sin = jnp.sin(ang)
    ones = jnp.ones((L, HEAD_DIM - ROT_DIM), F32)
    cosf = jnp.concatenate([cos, cos, ones], axis=1)
    sinf = jnp.concatenate([sin, sin, 0.0 * ones], axis=1)
    rot = np.zeros((HEAD_DIM, HEAD_DIM), np.float32)
    for d in range(half):
        rot[d + half, d] = -1.0
        rot[d, d + half] = 1.0
    return cosf, sinf, jnp.asarray(rot)


def _rope(t, cosf, sinf, rot):
    shp = t.shape
    t2 = t.reshape(-1, HEAD_DIM)
    sw = lax.dot_general(t2, rot, _DIMS["nn"], precision=lax.Precision.HIGH, preferred_element_type=F32).reshape(shp)
    return t * cosf + sw * sinf


def _attn_core_parts(s, va, sink):
    h, q, k = s.shape
    m = jnp.maximum(jnp.max(s, axis=-1, keepdims=True), sink)
    e = jnp.exp(s - m)
    es = jnp.exp(sink - m)
    ev = _dot(e.reshape(h * q, k), va, "nn")
    r = 1.0 / (ev[:, HEAD_DIM:HEAD_DIM + 1].reshape(h, q, 1) + es)
    return ev[:, :HEAD_DIM] * r.reshape(h * q, 1), e, r, es


@jax.custom_vjp
def _attn_core(s, v, va, sink):
    return _attn_core_parts(s, va, sink)[0]


def _attn_core_fwd(s, v, va, sink):
    o, e, r, es = _attn_core_parts(s, va, sink)
    return o, (o, e, r, es, v, va)


def _attn_core_bwd(res, do):
    o, e, r, es, v, va = res
    h, q, k = e.shape
    p = e * r
    t = jnp.sum(o * do, axis=-1, keepdims=True).reshape(h, q, 1)
    dp = _dot(do, v, "nt").reshape(h, q, k)
    dv = _dot(p.reshape(h * q, k), do, "tn")
    dsink = -jnp.sum(es * r * t, axis=1, keepdims=True)
    return p * (dp - t), dv, jnp.zeros_like(va), dsink


_attn_core.defvjp(_attn_core_fwd, _attn_core_bwd)


def _attn_block_fn(q, kw, vw, sinks, vaw, cq, sq, ck, sk, rot, q0, k0):
    nk = kw.shape[1]
    qr = _rope(q, cq, sq, rot)
    kr = _rope(kw, ck, sk, rot)
    qpos = q0 + lax.broadcasted_iota(jnp.int32, (1, ATT_BLOCK, nk), 1)
    kpos = k0 + lax.broadcasted_iota(jnp.int32, (1, ATT_BLOCK, nk), 2)
    diff = qpos - kpos
    allowed = (diff >= 0) & (diff < WINDOW)
    outs = []
    for kh in range(ATT_KV_HEADS):
        qh = qr[kh * GQA_GROUP:(kh + 1) * GQA_GROUP].reshape(GQA_GROUP * ATT_BLOCK, HEAD_DIM)
        s = _mm_nt(qh, kr[kh]).reshape(GQA_GROUP, ATT_BLOCK, nk) * (HEAD_DIM ** -0.5)
        s = jnp.where(allowed, s, NEG_INF)
        o = _attn_core(s, vw[kh], vaw[kh], sinks[kh * GQA_GROUP:(kh + 1) * GQA_GROUP])
        outs.append(o.reshape(GQA_GROUP, ATT_BLOCK, HEAD_DIM))
    return jnp.concatenate(outs, axis=0)


def _attn_common(L):
    nwin = min(2 * ATT_BLOCK, L)
    qspec = pl.BlockSpec((ATT_HEADS, ATT_BLOCK, HEAD_DIM), lambda n: (0, n, 0))
    kvspec = pl.BlockSpec((ATT_KV_HEADS, L, HEAD_DIM), lambda n: (0, 0, 0))
    sspec = pl.BlockSpec((ATT_HEADS, 1, 1), lambda n: (0, 0, 0))
    tq = pl.BlockSpec((ATT_BLOCK, HEAD_DIM), lambda n: (n, 0))
    tk = pl.BlockSpec((L, HEAD_DIM), lambda n: (0, 0))
    rspec = pl.BlockSpec((HEAD_DIM, HEAD_DIM), lambda n: (0, 0))
    vaspec = pl.BlockSpec((ATT_KV_HEADS, L, 2 * HEAD_DIM), lambda n: (0, 0, 0))
    return nwin, qspec, kvspec, sspec, tq, tk, rspec, vaspec


def _v_with_ones(vh):
    return jnp.concatenate([vh, jnp.ones_like(vh)], axis=-1).astype(MXU)


def _attn_fwd(qh, kh, vh, sinks, cosf, sinf, rot, *, name):
    L = qh.shape[1]
    nwin, qspec, kvspec, sspec, tq, tk, rspec, vaspec = _attn_common(L)

    def body(q_ref, k_ref, v_ref, s_ref, va_ref, cq_ref, sq_ref, ck_ref, sk_ref, r_ref, o_ref):
        n = pl.program_id(0)
        k0 = pl.multiple_of(jnp.maximum(n - 1, 0) * ATT_BLOCK, ATT_BLOCK)
        win = pl.ds(k0, nwin)
        o_ref[...] = _attn_block_fn(q_ref[...], k_ref[:, win, :], v_ref[:, win, :], s_ref[...], va_ref[:, win, :],
                                    cq_ref[...], sq_ref[...], ck_ref[win, :], sk_ref[win, :], r_ref[...],
                                    n * ATT_BLOCK, k0)

    return pl.pallas_call(
        body, name=name, grid=(L // ATT_BLOCK,),
        in_specs=[qspec, kvspec, kvspec, sspec, vaspec, tq, tq, tk, tk, rspec],
        out_specs=qspec,
        out_shape=jax.ShapeDtypeStruct((ATT_HEADS, L, HEAD_DIM), F32),
        compiler_params=_cp(("parallel",)),
    )(qh, kh, vh, sinks.reshape(ATT_HEADS, 1, 1), _v_with_ones(vh), cosf, sinf, cosf, sinf, rot)


def _attn_bwd(do, qh, kh, vh, sinks, cosf, sinf, rot, *, name):
    L = qh.shape[1]
    nwin, qspec, kvspec, sspec, tq, tk, rspec, vaspec = _attn_common(L)

    def body(do_ref, q_ref, k_ref, v_ref, s_ref, va_ref, cq_ref, sq_ref, ck_ref, sk_ref, r_ref,
             dq_ref, dk_ref, dv_ref, ds_ref):
        n = pl.program_id(0)
        k0 = pl.multiple_of(jnp.maximum(n - 1, 0) * ATT_BLOCK, ATT_BLOCK)
        win = pl.ds(k0, nwin)
        cq, sq, ck, sk, rt = cq_ref[...], sq_ref[...], ck_ref[win, :], sk_ref[win, :], r_ref[...]
        vaw = va_ref[:, win, :]
        q0 = n * ATT_BLOCK
        _, vjp = jax.vjp(lambda q, kw, vw, s: _attn_block_fn(q, kw, vw, s, vaw, cq, sq, ck, sk, rt, q0, k0),
                         q_ref[...], k_ref[:, win, :], v_ref[:, win, :], s_ref[...])
        dq, dkw, dvw, ds = vjp(do_ref[...])

        @pl.when(n == 0)
        def _():
            dk_ref[...] = jnp.zeros_like(dk_ref)
            dv_ref[...] = jnp.zeros_like(dv_ref)
            ds_ref[...] = jnp.zeros_like(ds_ref)

        dq_ref[...] = dq
        dk_ref[:, win, :] += dkw
        dv_ref[:, win, :] += dvw
        ds_ref[...] += ds

    return pl.pallas_call(
        body, name=name, grid=(L // ATT_BLOCK,),
        in_specs=[qspec, qspec, kvspec, kvspec, sspec, vaspec, tq, tq, tk, tk, rspec],
        out_specs=[qspec, kvspec, kvspec, sspec],
        out_shape=[jax.ShapeDtypeStruct((ATT_HEADS, L, HEAD_DIM), F32),
                   jax.ShapeDtypeStruct((ATT_KV_HEADS, L, HEAD_DIM), F32),
                   jax.ShapeDtypeStruct((ATT_KV_HEADS, L, HEAD_DIM), F32),
                   jax.ShapeDtypeStruct((ATT_HEADS, 1, 1), F32)],
        compiler_params=_cp(("arbitrary",)),
    )(do, qh, kh, vh, sinks.reshape(ATT_HEADS, 1, 1), _v_with_ones(vh), cosf, sinf, cosf, sinf, rot)


def _to_heads(t, nh):
    L = t.shape[0]
    return t.reshape(L, nh, HEAD_DIM).transpose(1, 0, 2)


def _from_heads(t):
    nh, L, _ = t.shape
    return t.transpose(1, 0, 2).reshape(L, nh * HEAD_DIM)


def _branch_fwd(ya, yb, o2d, zc, gates, wa, wb, wc, *, name, tm=256):
    L = ya.shape[0]
    tm = min(tm, L)
    W, D = SSM_WIDTH, D_MODEL

    def body(ya_ref, yb_ref, o_ref, zc_ref, g0_ref, g1_ref, g2_ref, wa_ref, wb_ref, wc_ref,
             mg_ref, ta_ref, tb_ref, tc_ref, yc_ref):
        yc = (o_ref[...] * _silu(zc_ref[...])).astype(MXU)
        ta = _dot(ya_ref[...], wa_ref[...], "nt")
        tb = _dot(yb_ref[...], wb_ref[...], "nt")
        tc = _dot(yc, wc_ref[...], "nt")
        ta_ref[...] = ta
        tb_ref[...] = tb
        tc_ref[...] = tc
        yc_ref[...] = yc
        mg_ref[...] = (jax.nn.sigmoid(g0_ref[...]) * ta + jax.nn.sigmoid(g1_ref[...]) * tb
                       + jax.nn.sigmoid(g2_ref[...]) * tc).astype(MXU)

    row = pl.BlockSpec((tm, W), lambda i: (i, 0))
    wide = pl.BlockSpec((tm, D), lambda i: (i, 0))
    gate = lambda c: pl.BlockSpec((tm, D), lambda i, c=c: (i, c))
    wspec = pl.BlockSpec((D, W), lambda i: (0, 0))
    return pl.pallas_call(
        body, name=name, grid=(L // tm,),
        in_specs=[row, row, row, row, gate(0), gate(1), gate(2), wspec, wspec, wspec],
        out_specs=[wide, wide, wide, wide, row],
        out_shape=[jax.ShapeDtypeStruct((L, D), MXU), jax.ShapeDtypeStruct((L, D), F32),
                   jax.ShapeDtypeStruct((L, D), F32), jax.ShapeDtypeStruct((L, D), F32),
                   jax.ShapeDtypeStruct((L, W), MXU)],
        compiler_params=_cp(("parallel",), 56),
    )(ya, yb, o2d, zc, gates, gates, gates, wa, wb, wc)


def _branch_bwd(dmg, ta, tb, tc, gates, *, name, tm=256):
    L = dmg.shape[0]
    tm = min(tm, L)
    D = D_MODEL

    def body(dm_ref, ta_ref, tb_ref, tc_ref, g0_ref, g1_ref, g2_ref, da_ref, db_ref, dc_ref, dg_ref):
        dm = dm_ref[...]
        for i, (t_ref, g_ref, d_ref) in enumerate(((ta_ref, g0_ref, da_ref), (tb_ref, g1_ref, db_ref),
                                                   (tc_ref, g2_ref, dc_ref))):
            sg = jax.nn.sigmoid(g_ref[...])
            d_ref[...] = (sg * dm).astype(MXU)
            dg_ref[:, i * D:(i + 1) * D] = (dm * t_ref[...] * sg * (1.0 - sg)).astype(MXU)

    wide = pl.BlockSpec((tm, D), lambda i: (i, 0))
    gate = lambda c: pl.BlockSpec((tm, D), lambda i, c=c: (i, c))
    bf = jax.ShapeDtypeStruct((L, D), MXU)
    return pl.pallas_call(
        body, name=name, grid=(L // tm,),
        in_specs=[wide, wide, wide, wide, gate(0), gate(1), gate(2)],
        out_specs=[wide, wide, wide, pl.BlockSpec((tm, 3 * D), lambda i: (i, 0))],
        out_shape=[bf, bf, bf, jax.ShapeDtypeStruct((L, 3 * D), MXU)],
        compiler_params=_cp(("parallel",), 56),
    )(dmg, ta, tb, tc, gates, gates, gates)


def _gate_c_bwd(dyc, o2d, zc, *, name, tm=256):
    L, W = dyc.shape
    tm = min(tm, L)

    def body(dy_ref, o_ref, z_ref, do_ref, dz_ref):
        _, vjp = jax.vjp(lambda o, z: o * _silu(z), o_ref[...], z_ref[...])
        do, dz = vjp(dy_ref[...])
        do_ref[...] = do
        dz_ref[...] = dz.astype(MXU)

    row = pl.BlockSpec((tm, W), lambda i: (i, 0))
    return pl.pallas_call(body, name=name, grid=(L // tm,), in_specs=[row, row, row], out_specs=[row, row],
                          out_shape=[jax.ShapeDtypeStruct((L, W), F32), jax.ShapeDtypeStruct((L, W), MXU)],
                          compiler_params=_cp(("parallel",)))(dyc, o2d, zc)


def _adamw(w, g, m, v, *, name):
    shape = w.shape
    cols = shape[-1]
    w2, g2, m2, v2 = (t.reshape(-1, cols) for t in (w, g, m, v))
    rows = w2.shape[0]
    tc = 1024 if cols % 1024 == 0 else cols
    lane_cols = -(-tc // 128) * 128
    tr = rows
    while tr % 16 == 0 and tr * lane_cols * 4 > 2 * _MB:
        tr //= 2

    def body(w_ref, g_ref, m_ref, v_ref, d_ref, nm_ref, nv_ref):
        gv = g_ref[...]
        nm = ADAM_B1 * m_ref[...] + (1.0 - ADAM_B1) * gv
        nv = ADAM_B2 * v_ref[...] + (1.0 - ADAM_B2) * jnp.square(gv)
        m_hat = nm / (1.0 - ADAM_B1 ** ADAM_STEP)
        v_hat = nv / (1.0 - ADAM_B2 ** ADAM_STEP)
        d_ref[...] = -ADAM_LR * (m_hat / (jnp.sqrt(v_hat) + ADAM_EPS) + ADAM_WD * w_ref[...])
        nm_ref[...] = nm
        nv_ref[...] = nv

    spec = pl.BlockSpec((tr, tc), lambda i, j: (i, j))
    out = jax.ShapeDtypeStruct((rows, cols), F32)
    d, nm, nv = pl.pallas_call(body, name=name, grid=(rows // tr, cols // tc), in_specs=[spec] * 4,
                               out_specs=[spec] * 3, out_shape=[out, out, out],
                               compiler_params=_cp(("parallel", "parallel")))(w2, g2, m2, v2)
    return d.reshape(shape), nm.reshape(shape), nv.reshape(shape)


def _adamw_layer(w, g, m, v, l, prev, *, name):
    _, rows, cols = w.shape
    tc = 1024 if cols % 1024 == 0 else cols
    tr = rows
    while tr % 16 == 0 and tr * tc * 4 > 2 * _MB:
        tr //= 2

    def body(w_ref, g_ref, m_ref, v_ref, *rest):
        go_ref, d_ref, nm_ref, nv_ref = rest[-4:]
        gv = g_ref[...]
        nm = ADAM_B1 * m_ref[...] + (1.0 - ADAM_B1) * gv
        nv = ADAM_B2 * v_ref[...] + (1.0 - ADAM_B2) * jnp.square(gv)
        m_hat = nm / (1.0 - ADAM_B1 ** ADAM_STEP)
        v_hat = nv / (1.0 - ADAM_B2 ** ADAM_STEP)
        d_ref[...] = -ADAM_LR * (m_hat / (jnp.sqrt(v_hat) + ADAM_EPS) + ADAM_WD * w_ref[...])
        nm_ref[...] = nm
        nv_ref[...] = nv
        go_ref[...] = gv

    lspec = pl.BlockSpec((None, tr, tc), lambda i, j: (l, i, j))
    gspec = pl.BlockSpec((tr, tc), lambda i, j: (i, j))
    out = jax.ShapeDtypeStruct(w.shape, F32)
    extra = [] if prev is None else list(prev)
    return pl.pallas_call(
        body, name=name, grid=(rows // tr, cols // tc),
        in_specs=[lspec, gspec, lspec, lspec] + [_ANY] * len(extra),
        out_specs=[lspec] * 4, out_shape=[out] * 4,
        input_output_aliases={4 + i: i for i in range(len(extra))},
        compiler_params=_cp(("parallel", "parallel")),
    )(w, g, m, v, *extra)


def _prep_layer(p, l, after=None):
    are = p["ssm_a_re"][l].reshape(SSM_CH, 1)
    aim = p["ssm_a_im"][l].reshape(SSM_CH, 1)
    ldt = p["ssm_log_dt"][l].reshape(1, SSM_GROUPS)
    bre = p["ssm_b_re"][l].reshape(SSM_CH, SSM_GROUP)
    bim = p["ssm_b_im"][l].reshape(SSM_CH, SSM_GROUP)
    lbr, lbi, bbr, bbi = _s5_params_fwd(are, aim, ldt, bre, bim, name=f"s5_params_fwd_{l}", after=after)
    cre = p["ssm_c_re"][l].transpose(0, 2, 1).reshape(SSM_CH, SSM_GROUP)
    cim = p["ssm_c_im"][l].transpose(0, 2, 1).reshape(SSM_CH, SSM_GROUP)
    return dict(raw=(are, aim, ldt, bre, bim),
                lbr=lbr.reshape(N_SLAB * SLAB_NC, 1, 128), lbi=lbi.reshape(N_SLAB * SLAB_NC, 1, 128),
                btr=_expand_bd(bbr), bti=_expand_bd(bbi), cbr=_expand_bd(cre), cbi=_expand_bd(cim),
                dvec=p["ssm_d"][l].reshape(1, SSM_WIDTH))


def _layer_fwd(x, h, p, sp, winT, rest_of, l, tabs, proj_after=None, after_main=None):
    L = x.shape[0]
    cosf, sinf, rot = tabs
    mm = functools.partial(_matmul, h, winT, "nt", tm=L, tn=256, tk=D_MODEL)
    main = mm(name=f"proj_main_{l}", shape=(L, N_MAIN, D_MODEL), after=proj_after)
    then = proj_after if after_main is None else after_main(main)
    zc = mm(name=f"proj_zc_{l}", shape=(L, N_ZC, D_MODEL), b_off=(N_MAIN // 256, 0), after=then)
    gates = mm(name=f"proj_gates_{l}", shape=(L, N_GATES, D_MODEL), b_off=((N_MAIN + N_ZC) // 256, 0), after=then)
    big, token = rest_of([main, zc, gates])
    ua = _time_interleave(main[:, :SSM_WIDTH])
    ys, sr, si = _s5_fwd(ua, sp["btr"], sp["bti"], sp["cbr"], sp["cbi"], sp["lbr"], sp["lbi"], sp["dvec"],
                         name=f"s5_fwd_{l}", after=token)
    ys = _time_deinterleave(ys)
    ya = _glu_fwd(ys, main, big["glu_w"], p["ssm_glu_b"][l], name=f"glu_fwd_{l}")
    yb = _sg_fwd(main, p["sg_ln_w"][l], p["sg_ln_b"][l], p["sg_w"][l], p["sg_b"][l], name=f"sg_fwd_{l}")
    qh = _to_heads(main[:, 5120:6144], ATT_HEADS)
    kh = _to_heads(main[:, 6144:6272], ATT_KV_HEADS)
    vh = _to_heads(main[:, 6272:6400], ATT_KV_HEADS)
    oh = _attn_fwd(qh, kh, vh, p["attn_sinks"][l], cosf, sinf, rot, name=f"attn_fwd_{l}")
    o2d = _from_heads(oh)
    mg, ta, tb, tc, yc = _branch_fwd(ya, yb, o2d, zc, gates, big["wbaT"], big["wbbT"], big["wbcT"],
                                     name=f"branch_fwd_{l}")
    xn = _matmul(mg, big["w_out"], "nn", name=f"out_fwd_{l}", shape=(L, D_MODEL, D_MODEL), tm=512, tn=512,
                 tk=D_MODEL, add=x)
    saved = dict(x=x, h=h, main=main, zc=zc, gates=gates, ua=ua, ys=ys, sr=sr, si=si, ya=ya, yb=yb, yc=yc, o2d=o2d,
                 qh=qh, kh=kh, vh=vh, mg=mg, ta=ta, tb=tb, tc=tc, sp=sp)
    return xn, saved, big


def _layer_bwd(dxn, s, p, big, l, tabs, early, mid):
    L = dxn.shape[0]
    D, W = D_MODEL, SSM_WIDTH
    cosf, sinf, rot = tabs
    sp = s["sp"]
    g = {}
    dmg = _matmul(dxn, big["w_out"], "nt", name=f"out_bwd_dm_{l}", shape=(L, D, D), tm=512, tn=512, tk=D)
    g["w_out"] = _matmul(s["mg"], dxn, "tn", name=f"out_bwd_dw_{l}", shape=(D, D, L), tm=512, tn=512, tk=L,
                         out_dtype=MXU)
    dta, dtb, dtc, dgates = _branch_bwd(dmg, s["ta"], s["tb"], s["tc"], s["gates"], name=f"branch_bwd_{l}")
    dys_ = {}
    for nm, dt, y, wt in (("a", dta, s["ya"], big["wbaT"]), ("b", dtb, s["yb"], big["wbbT"]),
                          ("c", dtc, s["yc"], big["wbcT"])):
        dys_[nm] = _matmul(dt, wt, "nn", name=f"branch_bwd_dy{nm}_{l}", shape=(L, W, D), tm=512, tn=512, tk=D)
        g["wb" + nm + "T"] = _matmul(dt, y, "tn", name=f"branch_bwd_dw{nm}_{l}", shape=(D, W, L),
                                     tm=512, tn=512, tk=L, out_dtype=MXU)
    do2d, dzc = _gate_c_bwd(dys_["c"], s["o2d"], s["zc"], name=f"gate_c_bwd_{l}")
    dqh, dkh, dvh, dsinks = _attn_bwd(_to_heads(do2d, ATT_HEADS), s["qh"], s["kh"], s["vh"], p["attn_sinks"][l],
                                      cosf, sinf, rot, name=f"attn_bwd_{l}")
    g["attn_sinks"] = dsinks.reshape(ATT_HEADS)
    dub, dvb, dzb, dlnw, dlnb, dsgw, dsgb = _sg_bwd(dys_["b"], s["main"], p["sg_ln_w"][l], p["sg_ln_b"][l],
                                                    p["sg_w"][l], p["sg_b"][l], name=f"sg_bwd_{l}")
    g["sg_ln_w"], g["sg_ln_b"] = dlnw.reshape(W), dlnb.reshape(W)
    g["sg_w"], g["sg_b"] = dsgw, dsgb.reshape(SG_HEADS, SG_CHUNK)
    dys, dza, a1, dzl, dgb = _glu_bwd(dys_["a"], s["ys"], s["main"], big["glu_w"], p["ssm_glu_b"][l],
                                      name=f"glu_bwd_{l}")
    g["ssm_glu_b"] = dgb.reshape(W)
    g["glu_w"] = _matmul(a1, dzl, "tn", name=f"glu_bwd_dw_{l}", shape=(W, W, L), tm=512, tn=512, tk=L, out_dtype=MXU)
    token = early(g)
    dua, dbtr, dbti, dcbr, dcbi, dlr, dli, dd = _s5_bwd(_time_interleave(dys), s["ua"], s["sr"], s["si"], sp["btr"],
                                                        sp["bti"], sp["cbr"], sp["cbi"], sp["lbr"], sp["lbi"],
                                                        sp["dvec"], name=f"s5_bwd_{l}", after=token)
    token = mid(dua)
    dua = _time_deinterleave(dua)
    g["ssm_d"] = dd.reshape(W)
    to_c = lambda t: _contract_bd(t).reshape(SSM_GROUPS, SSM_STATE, SSM_GROUP).transpose(0, 2, 1)
    g["ssm_c_re"], g["ssm_c_im"] = to_c(dcbr), to_c(dcbi)
    dare, daim, dldt, dbre, dbim = _s5_params_bwd(*sp["raw"], dlr.reshape(SSM_CH, 1), dli.reshape(SSM_CH, 1),
                                                  _contract_bd(dbtr), _contract_bd(dbti),
                                                  name=f"s5_params_bwd_{l}")
    g["ssm_a_re"] = dare.reshape(SSM_GROUPS, SSM_STATE)
    g["ssm_a_im"] = daim.reshape(SSM_GROUPS, SSM_STATE)
    g["ssm_log_dt"] = dldt.reshape(SSM_GROUPS)
    g["ssm_b_re"] = dbre.reshape(SSM_GROUPS, SSM_STATE, SSM_GROUP)
    g["ssm_b_im"] = dbim.reshape(SSM_GROUPS, SSM_STATE, SSM_GROUP)
    dproj = jnp.concatenate([t.astype(MXU) for t in (dua, dza, dub, dvb, dzb, _from_heads(dqh), _from_heads(dkh),
                                                     _from_heads(dvh), dzc, dgates)], axis=1)
    g["winT"] = _matmul(dproj, s["h"], "tn", name=f"proj_bwd_dw_{l}", shape=(D_IN, D, L), tm=256, tn=D, tk=L,
                        out_dtype=MXU, after=token)
    return dproj, g


def _proj_bwd_dh(dproj, winT, l, after):
    return _matmul(dproj, winT, "nn", name=f"proj_bwd_dh_{l}", shape=(dproj.shape[0], D_MODEL, D_IN), tm=512, tn=512,
                   tk=D_IN // 2, after=after)


MESH = pl.DeviceIdType.MESH
_ANY = pl.BlockSpec(memory_space=pl.ANY)
ROW_ALIGN = 16


def _coords():
    return lax.axis_index("x"), lax.axis_index("y"), lax.axis_index("c")


def _gather8(arrs, *, name):
    n = len(arrs)
    rows = [a.shape[0] for a in arrs]
    for r in rows:
        assert r % ROW_ALIGN == 0

    def body(*refs):
        ins, outs = refs[:n], refs[n:2 * n]
        send, recv, lsem = refs[2 * n:]
        x, y, c = _coords()
        me, sibling = (x, y, c), (x, y, 1 - c)
        chips = [(1 - x, y), (x, 1 - y), (1 - x, 1 - y)]

        def blk(a, px, py, pc):
            return outs[a].at[pl.ds(pl.multiple_of((4 * px + 2 * py + pc) * rows[a], ROW_ALIGN), rows[a]), :]

        def own(a):
            return ins[a]

        def copy(a, k, block, to, src=None):
            return pltpu.make_async_remote_copy(
                src_ref=blk(a, *block) if src is None else src, dst_ref=blk(a, *block),
                send_sem=send.at[a, k], recv_sem=recv.at[a, k], device_id=to, device_id_type=MESH)

        mine, first, passed = [], [], []
        for a in range(n):
            mine.append(pltpu.make_async_copy(own(a), blk(a, *me), lsem.at[a]))
            mine[a].start()
            f = [copy(a, 0, me, sibling, src=own(a))]
            f += [copy(a, 1 + j, me, (*chip, c), src=own(a)) for j, chip in enumerate(chips)]
            for cp in f:
                cp.start()
            first.append(f)
        for a in range(n):
            ps = [copy(a, 4 + j, (*chip, c), sibling) for j, chip in enumerate(chips)]
            for j, chip in enumerate(chips):
                copy(a, 1 + j, (*chip, c), me).wait_recv()
                ps[j].start()
            passed.append(ps)
        for a in range(n):
            copy(a, 0, sibling, me).wait_recv()
            for j, chip in enumerate(chips):
                copy(a, 4 + j, (*chip, 1 - c), me).wait_recv()
            for cp in first[a] + passed[a]:
                cp.wait_send()
            mine[a].wait()

    return pl.pallas_call(
        body, name=name,
        in_specs=[_ANY] * n, out_specs=[_ANY] * n,
        out_shape=[jax.ShapeDtypeStruct((8 * r,) + a.shape[1:], a.dtype) for r, a in zip(rows, arrs)],
        scratch_shapes=[pltpu.SemaphoreType.DMA((n, 7)), pltpu.SemaphoreType.DMA((n, 7)), pltpu.SemaphoreType.DMA((n,))],
    )(*arrs)


def _sibling_swap(arrs, *, name):
    n = len(arrs)

    def body(*refs):
        ins, outs = refs[:n], refs[n:2 * n]
        send, recv = refs[2 * n:]
        x, y, c = _coords()
        cps = [pltpu.make_async_remote_copy(src_ref=ins[a].at[:, 1 - c], dst_ref=outs[a], send_sem=send.at[a],
                                            recv_sem=recv.at[a], device_id=(x, y, 1 - c), device_id_type=MESH)
               for a in range(n)]
        for cp in cps:
            cp.start()
        for cp in cps:
            cp.wait_recv()
        for cp in cps:
            cp.wait_send()

    return pl.pallas_call(
        body, name=name, in_specs=[_ANY] * n, out_specs=[_ANY] * n,
        out_shape=[jax.ShapeDtypeStruct((a.shape[0],) + a.shape[2:], a.dtype) for a in arrs],
        scratch_shapes=[pltpu.SemaphoreType.DMA((n,)), pltpu.SemaphoreType.DMA((n,))],
    )(*arrs)


def _col_tile(lead, rows, cols, itemsize=4, cap=4 * _MB):
    tc = cols
    while tc % 256 == 0 and lead * rows * tc * itemsize > cap:
        tc //= 2
    return tc


def _pair_sum(mine, theirs, *, name):
    _, _, rows, cols = mine.shape
    tc = _col_tile(1, rows, cols)
    c = lax.axis_index("c")

    def body(c_ref, a_ref, b_ref, o_ref):
        o_ref[...] = (a_ref[...].astype(F32) + b_ref[...].astype(F32)).astype(MXU)

    return pl.pallas_call(
        body, name=name,
        grid_spec=pltpu.PrefetchScalarGridSpec(
            num_scalar_prefetch=1, grid=(4, cols // tc),
            in_specs=[pl.BlockSpec((None, None, rows, tc), lambda j, i, cr: (j, cr[0], 0, i)),
                      pl.BlockSpec((None, rows, tc), lambda j, i, cr: (j, 0, i))],
            out_specs=pl.BlockSpec((None, rows, tc), lambda j, i, cr: (j, 0, i))),
        out_shape=jax.ShapeDtypeStruct((4, rows, cols), MXU),
        compiler_params=_cp(("parallel", "parallel")),
    )(c.reshape(1).astype(jnp.int32), mine, theirs)


_HBM = pl.BlockSpec(memory_space=pltpu.HBM)
_SEM = pl.BlockSpec(memory_space=pltpu.SEMAPHORE)
_EFFECT = pltpu.SideEffectType.DATAFLOW_SIDE_EFFECTING
N_PEER_CHIPS = 3


def _peer_chips(x, y):
    return [(1 - x, y), (x, 1 - y), (1 - x, 1 - y)]


def _split_start(srcs, lands, src_slot, dst_slot, *, name, after=()):
    n = len(srcs)
    ns = n * N_PEER_CHIPS
    first = 2 * n + len(after)

    def body(*refs):
        src_refs, land_refs = refs[:n], refs[n:2 * n]
        send, recv, token = refs[first:first + ns], refs[first + ns:first + 2 * ns], refs[-1]
        x, y, c = _coords()
        for a in range(n):
            for k, (px, py) in enumerate(_peer_chips(x, y)):
                pltpu.make_async_remote_copy(
                    src_ref=src_refs[a].at[src_slot(x, y, c, px, py)], dst_ref=land_refs[a].at[dst_slot(x, y, c)],
                    send_sem=send[a * N_PEER_CHIPS + k], recv_sem=recv[a * N_PEER_CHIPS + k],
                    device_id=(px, py, c), device_id_type=MESH).start()
        token[...] = jnp.zeros_like(token)

    bufs = list(srcs) + list(lands)
    res = pl.pallas_call(
        body, name=name,
        out_shape=(*[pltpu.SemaphoreType.DMA(())] * (2 * ns), *[pltpu.HBM(b.shape, b.dtype) for b in bufs],
                   jax.ShapeDtypeStruct((8, 128), F32)),
        in_specs=[_HBM] * (2 * n) + [_ANY] * len(after),
        out_specs=(*[_SEM] * (2 * ns), *[_HBM] * (2 * n), pl.BlockSpec(memory_space=pltpu.VMEM)),
        input_output_aliases={i: 2 * ns + i for i in range(2 * n)},
        compiler_params=pltpu.CompilerParams(has_side_effects=_EFFECT),
    )(*[pltpu.with_memory_space_constraint(b, pltpu.HBM) for b in bufs], *after)
    sems = list(res[:2 * ns])
    return sems, list(res[2 * ns:2 * ns + n]), list(res[2 * ns + n:2 * ns + 2 * n]), res[-1]


def _split_wait(sems, srcs, lands, after, *, name):
    n = len(srcs)
    ns = n * N_PEER_CHIPS

    def body(*refs):
        src_refs, land_refs = refs[:n], refs[n:2 * n]
        send, recv = refs[2 * n:2 * n + ns], refs[2 * n + ns:2 * n + 2 * ns]
        x, y, c = _coords()
        for a in range(n):
            for k in range(N_PEER_CHIPS):
                cp = pltpu.make_async_remote_copy(
                    src_ref=src_refs[a].at[0], dst_ref=land_refs[a].at[0], send_sem=send[a * N_PEER_CHIPS + k],
                    recv_sem=recv[a * N_PEER_CHIPS + k], device_id=(x, y, 1 - c), device_id_type=MESH)
                cp.wait_send()
                cp.wait_recv()

    bufs = list(srcs) + list(lands)
    res = pl.pallas_call(
        body, name=name,
        out_shape=tuple(pltpu.HBM(b.shape, b.dtype) for b in bufs),
        in_specs=[_HBM] * (2 * n) + [_SEM] * (2 * ns) + [_ANY] * len(after),
        out_specs=tuple([_HBM] * (2 * n)),
        input_output_aliases={i: i for i in range(2 * n)},
        compiler_params=pltpu.CompilerParams(has_side_effects=_EFFECT),
    )(*bufs, *sems, *after)
    return list(res[:n]), list(res[n:])


def _gather_start(lands, *, name, after=()):
    n = len(lands)
    ns = n * N_PEER_CHIPS
    first = n + len(after)

    def body(*refs):
        land_refs = refs[:n]
        send, recv, token = refs[first:first + ns], refs[first + ns:first + 2 * ns], refs[-1]
        x, y, c = _coords()
        mine = 4 * x + 2 * y + c
        for a in range(n):
            for k, (px, py) in enumerate(_peer_chips(x, y)):
                pltpu.make_async_remote_copy(
                    src_ref=land_refs[a].at[mine], dst_ref=land_refs[a].at[mine], send_sem=send[a * N_PEER_CHIPS + k],
                    recv_sem=recv[a * N_PEER_CHIPS + k], device_id=(px, py, c), device_id_type=MESH).start()
        token[...] = jnp.zeros_like(token)

    res = pl.pallas_call(
        body, name=name,
        out_shape=(*[pltpu.SemaphoreType.DMA(())] * (2 * ns), *[pltpu.HBM(b.shape, b.dtype) for b in lands],
                   jax.ShapeDtypeStruct((8, 128), F32)),
        in_specs=[_HBM] * n + [_ANY] * len(after),
        out_specs=(*[_SEM] * (2 * ns), *[_HBM] * n, pl.BlockSpec(memory_space=pltpu.VMEM)),
        input_output_aliases={i: 2 * ns + i for i in range(n)},
        compiler_params=pltpu.CompilerParams(has_side_effects=_EFFECT),
    )(*[pltpu.with_memory_space_constraint(b, pltpu.HBM) for b in lands], *after)
    return list(res[:2 * ns]), list(res[2 * ns:2 * ns + n]), res[-1]


def _gather_wait(sems, lands, after, *, name):
    n = len(lands)
    ns = n * N_PEER_CHIPS

    def body(*refs):
        land_refs = refs[:n]
        send, recv = refs[n:n + ns], refs[n + ns:n + 2 * ns]
        x, y, c = _coords()
        for a in range(n):
            for k in range(N_PEER_CHIPS):
                cp = pltpu.make_async_remote_copy(
                    src_ref=land_refs[a].at[0], dst_ref=land_refs[a].at[0], send_sem=send[a * N_PEER_CHIPS + k],
                    recv_sem=recv[a * N_PEER_CHIPS + k], device_id=(x, y, 1 - c), device_id_type=MESH)
                cp.wait_send()
                cp.wait_recv()

    res = pl.pallas_call(
        body, name=name,
        out_shape=tuple(pltpu.HBM(b.shape, b.dtype) for b in lands),
        in_specs=[_HBM] * n + [_SEM] * (2 * ns) + [_ANY] * len(after),
        out_specs=tuple([_HBM] * n),
        input_output_aliases={i: i for i in range(n)},
        compiler_params=pltpu.CompilerParams(has_side_effects=_EFFECT),
    )(*lands, *sems, *after)
    return list(res)


def _fill_own(shards, l, *, name, after=None):
    _, rows2, cols = shards.shape
    rows = rows2 // 2
    tc = _col_tile(1, rows, cols, itemsize=shards.dtype.itemsize)
    j = 2 * lax.axis_index("x") + lax.axis_index("y")
    extra = [] if after is None else [after]

    def body(j_ref, s_ref, *rest):
        rest[-1][...] = s_ref[...].astype(MXU)

    return pl.pallas_call(
        body, name=name,
        grid_spec=pltpu.PrefetchScalarGridSpec(
            num_scalar_prefetch=1, grid=(2, cols // tc),
            in_specs=([pl.BlockSpec((None, rows, tc), lambda h, i, jr: (l, h, i))]
                      + [pl.BlockSpec(memory_space=pl.ANY)] * len(extra)),
            out_specs=pl.BlockSpec((None, rows, tc), lambda h, i, jr: (2 * jr[0] + h, 0, i))),
        out_shape=jax.ShapeDtypeStruct((8, rows, cols), MXU),
        compiler_params=_cp(("parallel", "parallel")),
    )(j.reshape(1).astype(jnp.int32), shards, *extra)


def _pass_to_sibling(lands, *, name):
    n = len(lands)

    def body(*refs):
        outs = refs[n:2 * n]
        send, recv = refs[2 * n:]
        x, y, c = _coords()
        cps = []
        for a in range(n):
            for k, (px, py) in enumerate(_peer_chips(x, y)):
                slot = 4 * px + 2 * py + c
                cps.append(pltpu.make_async_remote_copy(
                    src_ref=outs[a].at[slot], dst_ref=outs[a].at[slot], send_sem=send.at[a, k], recv_sem=recv.at[a, k],
                    device_id=(x, y, 1 - c), device_id_type=MESH))
        for cp in cps:
            cp.start()
        for cp in cps:
            cp.wait_recv()
        for cp in cps:
            cp.wait_send()

    return pl.pallas_call(
        body, name=name, in_specs=[_ANY] * n, out_specs=[_ANY] * n,
        out_shape=[jax.ShapeDtypeStruct(b.shape, b.dtype) for b in lands],
        input_output_aliases={a: a for a in range(n)},
        scratch_shapes=[pltpu.SemaphoreType.DMA((n, N_PEER_CHIPS)), pltpu.SemaphoreType.DMA((n, N_PEER_CHIPS))],
    )(*lands)


def _sum_parts(parts, got, *, name):
    _, rows, cols = parts.shape
    tc = _col_tile(4, rows, cols, itemsize=parts.dtype.itemsize)
    x, y, c = _coords()
    idx = jnp.stack([2 * x + y, 2 * (1 - x) + y, 2 * x + (1 - y), 2 * (1 - x) + (1 - y), c]).astype(jnp.int32)

    def body(i_ref, p_ref, g0_ref, g1_ref, g2_ref, o_ref):
        o_ref[...] = ((p_ref[...].astype(F32) + g0_ref[...].astype(F32)) + g1_ref[...].astype(F32)) + g2_ref[...].astype(F32)

    slot = lambda s: pl.BlockSpec((None, rows, tc), lambda i, ir, s=s: (ir[s], 0, i))
    return pl.pallas_call(
        body, name=name,
        grid_spec=pltpu.PrefetchScalarGridSpec(
            num_scalar_prefetch=1, grid=(cols // tc,),
            in_specs=[slot(0), slot(1), slot(2), slot(3)],
            out_specs=pl.BlockSpec((None, rows, tc), lambda i, ir: (ir[4], 0, i))),
        out_shape=jax.ShapeDtypeStruct((2, rows, cols), F32),
        compiler_params=_cp(("parallel",)),
    )(idx, parts, got, got, got)


def _sum_slots(t, *, name):
    S, rows, cols = t.shape
    tc = _col_tile(S, rows, cols)

    def body(t_ref, o_ref):
        acc = t_ref[0].astype(F32)
        for s in range(1, S):
            acc = acc + t_ref[s].astype(F32)
        o_ref[...] = acc

    return pl.pallas_call(
        body, name=name, grid=(cols // tc,),
        in_specs=[pl.BlockSpec((S, rows, tc), lambda i: (0, 0, i))],
        out_specs=pl.BlockSpec((rows, tc), lambda i: (0, i)),
        out_shape=jax.ShapeDtypeStruct((rows, cols), F32),
        compiler_params=_cp(("parallel",)),
    )(t)


def _halves_join(bufs, *, name):
    n = len(bufs)

    def body(*refs):
        outs = refs[n:2 * n]
        send, recv = refs[2 * n:]
        x, y, c = _coords()
        cps = [pltpu.make_async_remote_copy(src_ref=outs[a].at[c], dst_ref=outs[a].at[c], send_sem=send.at[a],
                                            recv_sem=recv.at[a], device_id=(x, y, 1 - c), device_id_type=MESH)
               for a in range(n)]
        for cp in cps:
            cp.start()
        for cp in cps:
            cp.wait_recv()
        for cp in cps:
            cp.wait_send()

    return pl.pallas_call(
        body, name=name, in_specs=[_ANY] * n, out_specs=[_ANY] * n,
        out_shape=[jax.ShapeDtypeStruct(b.shape, b.dtype) for b in bufs],
        input_output_aliases={a: a for a in range(n)},
        scratch_shapes=[pltpu.SemaphoreType.DMA((n,)), pltpu.SemaphoreType.DMA((n,))],
    )(*bufs)


def _swap_start(srcs, *, name):
    n = len(srcs)
    lands = [lax.empty((s.shape[0],) + s.shape[2:], s.dtype) for s in srcs]

    def body(*refs):
        src_refs, land_refs = refs[:n], refs[n:2 * n]
        send, recv, token = refs[2 * n:3 * n], refs[3 * n:4 * n], refs[-1]
        x, y, c = _coords()
        for a in range(n):
            pltpu.make_async_remote_copy(src_ref=src_refs[a].at[:, 1 - c], dst_ref=land_refs[a], send_sem=send[a],
                                         recv_sem=recv[a], device_id=(x, y, 1 - c), device_id_type=MESH).start()
        token[...] = jnp.zeros_like(token)

    bufs = list(srcs) + lands
    res = pl.pallas_call(
        body, name=name,
        out_shape=(*[pltpu.SemaphoreType.DMA(())] * (2 * n), *[pltpu.HBM(b.shape, b.dtype) for b in bufs],
                   jax.ShapeDtypeStruct((8, 128), F32)),
        in_specs=[_HBM] * (2 * n),
        out_specs=(*[_SEM] * (2 * n), *[_HBM] * (2 * n), pl.BlockSpec(memory_space=pltpu.VMEM)),
        input_output_aliases={i: 2 * n + i for i in range(2 * n)},
        compiler_params=pltpu.CompilerParams(has_side_effects=_EFFECT),
    )(*[pltpu.with_memory_space_constraint(b, pltpu.HBM) for b in bufs])
    return list(res[:2 * n]), list(res[2 * n:3 * n]), list(res[3 * n:4 * n]), res[-1]


def _swap_wait(sems, srcs, lands, after, *, name):
    n = len(srcs)

    def body(*refs):
        src_refs, land_refs = refs[:n], refs[n:2 * n]
        send, recv = refs[2 * n:3 * n], refs[3 * n:4 * n]
        x, y, c = _coords()
        for a in range(n):
            cp = pltpu.make_async_remote_copy(
                src_ref=src_refs[a].at[:, 0], dst_ref=land_refs[a], send_sem=send[a], recv_sem=recv[a],
                device_id=(x, y, 1 - c), device_id_type=MESH)
            cp.wait_send()
            cp.wait_recv()

    bufs = list(srcs) + list(lands)
    res = pl.pallas_call(
        body, name=name,
        out_shape=tuple(pltpu.HBM(b.shape, b.dtype) for b in bufs),
        in_specs=[_HBM] * (2 * n) + [_SEM] * (2 * n) + [_ANY] * len(after),
        out_specs=tuple([_HBM] * (2 * n)),
        input_output_aliases={i: i for i in range(2 * n)},
        compiler_params=pltpu.CompilerParams(has_side_effects=_EFFECT),
    )(*bufs, *sems, *after)
    return list(res[:n]), list(res[n:])


def _grad_views(grads):
    return [g.reshape(4, 2, g.shape[0] // 8, g.shape[1]) for g in grads]


def _scatter_begin(views, theirs, *, tag):
    parts = [_pair_sum(v, t, name=f"rs_pair_{tag}_{i}") for i, (v, t) in enumerate(zip(views, theirs))]
    got = [lax.empty(p.shape, p.dtype) for p in parts]
    sems, parts, got, token = _split_start(
        parts, got, lambda x, y, c, px, py: 2 * px + py, lambda x, y, c: 2 * x + y, name=f"rs_start_{tag}")
    return (sems, parts, got), token


def _reduce_scatter_begin(grads, *, tag):
    views = _grad_views(grads)
    theirs = _sibling_swap(views, name=f"rs_swap_{tag}")
    return _scatter_begin(views, theirs, tag=tag)


def _reduce_scatter_end(state, after, *, tag):
    sems, parts, got = state
    parts, got = _split_wait(sems, parts, got, after, name=f"rs_wait_{tag}")
    halves = [_sum_parts(p, t, name=f"rs_sum_{tag}_{i}") for i, (p, t) in enumerate(zip(parts, got))]
    joined = _halves_join(halves, name=f"rs_join_{tag}")
    return [j.reshape(2 * j.shape[1], j.shape[2]) for j in joined]


_SMALL = ("norm_w", "ssm_a_re", "ssm_a_im", "ssm_log_dt", "ssm_b_re", "ssm_b_im", "ssm_c_re", "ssm_c_im", "ssm_d",
          "ssm_glu_b", "sg_ln_w", "sg_ln_b", "sg_w", "sg_b", "attn_sinks", "final_norm_w")
_BIG = ("w_in", "ssm_glu_w", "w_branch_a", "w_branch_b", "w_branch_c", "w_out")
_WEIGHTS = ("norm_w", "w_in", "ssm_a_re", "ssm_a_im", "ssm_log_dt", "ssm_b_re", "ssm_b_im", "ssm_c_re", "ssm_c_im",
            "ssm_d", "ssm_glu_w", "ssm_glu_b", "sg_ln_w", "sg_ln_b", "sg_w", "sg_b", "attn_sinks", "w_branch_a",
            "w_branch_b", "w_branch_c", "w_out", "final_norm_w")
_PACK_COLS = 1024
_PACK_ALIGN = 8 * ROW_ALIGN * _PACK_COLS


def _slice_exchange(buf, *, name, after=()):
    def body(in_ref, *rest):
        out_ref, send, recv, lsem = rest[-4:]
        x, y, c = _coords()
        me = 4 * x + 2 * y + c
        own = pltpu.make_async_copy(in_ref.at[me], out_ref.at[me], lsem)
        own.start()
        cps = []
        for k in range(1, 8):
            px, py, pc = x ^ (k >> 2), y ^ ((k >> 1) & 1), c ^ (k & 1)
            cps.append(pltpu.make_async_remote_copy(
                src_ref=in_ref.at[4 * px + 2 * py + pc], dst_ref=out_ref.at[me], send_sem=send.at[k - 1],
                recv_sem=recv.at[k - 1], device_id=(px, py, pc), device_id_type=MESH))
        for cp in cps:
            cp.start()
        for cp in cps:
            cp.wait_recv()
        for cp in cps:
            cp.wait_send()
        own.wait()

    return pl.pallas_call(
        body, name=name, in_specs=[_ANY] * (1 + len(after)), out_specs=_ANY,
        out_shape=jax.ShapeDtypeStruct(buf.shape, buf.dtype),
        scratch_shapes=[pltpu.SemaphoreType.DMA((7,)), pltpu.SemaphoreType.DMA((7,)), pltpu.SemaphoreType.DMA],
    )(buf, *after)


def _allreduce_small(packed, after=()):
    rows, cols = packed.shape
    got = _slice_exchange(packed.reshape(8, rows // 8, cols), name="small_grads_exchange", after=after)
    mine = _sum_slots(got, name="small_grads_sum")
    return _gather8([mine], name="small_grads_gather")[0]


def _pack(ts):
    flat = jnp.concatenate([t.reshape(-1) for t in ts])
    pad = (-flat.shape[0]) % _PACK_ALIGN
    return jnp.pad(flat, (0, pad)).reshape(-1, _PACK_COLS)


def _unpack(buf, like):
    flat = buf.reshape(-1)
    out, pos = [], 0
    for t in like:
        out.append(flat[pos:pos + t.size].reshape(t.shape))
        pos += t.size
    return out


def kernel(x, norm_w, w_in, ssm_a_re, ssm_a_im, ssm_log_dt, ssm_b_re, ssm_b_im, ssm_c_re, ssm_c_im, ssm_d, ssm_glu_w, ssm_glu_b, sg_ln_w, sg_ln_b, sg_w, sg_b, attn_sinks, w_branch_a, w_branch_b, w_branch_c, w_out, final_norm_w, loss_target, m_norm_w, m_w_in, m_ssm_a_re, m_ssm_a_im, m_ssm_log_dt, m_ssm_b_re, m_ssm_b_im, m_ssm_c_re, m_ssm_c_im, m_ssm_d, m_ssm_glu_w, m_ssm_glu_b, m_sg_ln_w, m_sg_ln_b, m_sg_w, m_sg_b, m_attn_sinks, m_w_branch_a, m_w_branch_b, m_w_branch_c, m_w_out, m_final_norm_w, v_norm_w, v_w_in, v_ssm_a_re, v_ssm_a_im, v_ssm_log_dt, v_ssm_b_re, v_ssm_b_im, v_ssm_c_re, v_ssm_c_im, v_ssm_d, v_ssm_glu_w, v_ssm_glu_b, v_sg_ln_w, v_sg_ln_b, v_sg_w, v_sg_b, v_attn_sinks, v_w_branch_a, v_w_branch_b, v_w_branch_c, v_w_out, v_final_norm_w):
    w = dict(norm_w=norm_w, w_in=w_in, ssm_a_re=ssm_a_re, ssm_a_im=ssm_a_im, ssm_log_dt=ssm_log_dt, ssm_b_re=ssm_b_re,
             ssm_b_im=ssm_b_im, ssm_c_re=ssm_c_re, ssm_c_im=ssm_c_im, ssm_d=ssm_d, ssm_glu_w=ssm_glu_w,
             ssm_glu_b=ssm_glu_b, sg_ln_w=sg_ln_w, sg_ln_b=sg_ln_b, sg_w=sg_w, sg_b=sg_b, attn_sinks=attn_sinks,
             w_branch_a=w_branch_a, w_branch_b=w_branch_b, w_branch_c=w_branch_c, w_out=w_out,
             final_norm_w=final_norm_w)
    m = dict(norm_w=m_norm_w, w_in=m_w_in, ssm_a_re=m_ssm_a_re, ssm_a_im=m_ssm_a_im, ssm_log_dt=m_ssm_log_dt,
             ssm_b_re=m_ssm_b_re, ssm_b_im=m_ssm_b_im, ssm_c_re=m_ssm_c_re, ssm_c_im=m_ssm_c_im, ssm_d=m_ssm_d,
             ssm_glu_w=m_ssm_glu_w, ssm_glu_b=m_ssm_glu_b, sg_ln_w=m_sg_ln_w, sg_ln_b=m_sg_ln_b, sg_w=m_sg_w,
             sg_b=m_sg_b, attn_sinks=m_attn_sinks, w_branch_a=m_w_branch_a, w_branch_b=m_w_branch_b,
             w_branch_c=m_w_branch_c, w_out=m_w_out, final_norm_w=m_final_norm_w)
    v = dict(norm_w=v_norm_w, w_in=v_w_in, ssm_a_re=v_ssm_a_re, ssm_a_im=v_ssm_a_im, ssm_log_dt=v_ssm_log_dt,
             ssm_b_re=v_ssm_b_re, ssm_b_im=v_ssm_b_im, ssm_c_re=v_ssm_c_re, ssm_c_im=v_ssm_c_im, ssm_d=v_ssm_d,
             ssm_glu_w=v_ssm_glu_w, ssm_glu_b=v_ssm_glu_b, sg_ln_w=v_sg_ln_w, sg_ln_b=v_sg_ln_b, sg_w=v_sg_w,
             sg_b=v_sg_b, attn_sinks=v_attn_sinks, w_branch_a=v_w_branch_a, w_branch_b=v_w_branch_b,
             w_branch_c=v_w_branch_c, w_out=v_w_out, final_norm_w=v_final_norm_w)

    big_names = ("winT", "glu_w", "wbaT", "wbbT", "wbcT", "w_out")
    L = x.shape[1]
    tabs = _rope_tables(L)
    p = {k: w[k] for k in _SMALL}

    column_sharded = ("w_in", "w_branch_a", "w_branch_b", "w_branch_c")

    row_shards = {k: w[k].transpose(0, 2, 1) if k in column_sharded else w[k] for k in _BIG}
    rows_of = lambda lands: [t.reshape(8 * t.shape[1], t.shape[2]) for t in lands]
    saved = [None] * DEPTH

    land_a = [_fill_own(row_shards["w_in"], 0, name="gather_fill_0_0")]
    sems_a, land_a, token_a = _gather_start(land_a, name="gather_start_0a")
    fill_after = lambda l, names, i0: [_fill_own(row_shards[k], l, name=f"gather_fill_{l}_{i0 + i}", after=token_a)
                                       for i, k in enumerate(names)]
    lands = [[None] + fill_after(0, _BIG[1:], 1), fill_after(1, _BIG, 0)]
    sp = [_prep_layer(p, l, after=token_a) for l in range(DEPTH)]
    h0 = _rms_fwd(x[0], p["norm_w"][0], name="rms_fwd_0", after=token_a)
    land_a = _gather_wait(sems_a, land_a, [h0, sp[0]["btr"], sp[1]["btr"]] + lands[0][1:] + lands[1],
                          name="gather_wait_0a")
    land_a = _pass_to_sibling(land_a, name="gather_pass_0a")
    sems_b, land_b, token_b = _gather_start(lands[0][1:], name="gather_start_0b", after=land_a)
    split1 = {}

    def start1(main):
        split1["sems"], split1["land"], token1 = _gather_start(lands[1][:1], name="gather_start_1a", after=[main])
        return token1

    def rest(l, sems, land, first):
        def arrived(t):
            got = _pass_to_sibling(_gather_wait(sems, land, t, name=f"gather_wait_{l}b"), name=f"gather_pass_{l}b")
            return dict(zip(big_names, rows_of(first + got))), None
        return arrived

    x1, saved[0], big0 = _layer_fwd(x[0], h0, p, sp[0], rows_of(land_a)[0], rest(0, sems_b, land_b, land_a), 0,
                                    tabs, proj_after=token_b, after_main=start1)
    land_1a = _gather_wait(split1["sems"], split1["land"], [x1], name="gather_wait_1a")
    land_1a = _pass_to_sibling(land_1a, name="gather_pass_1a")
    sems_1b, land_1b, token_1b = _gather_start(lands[1][1:], name="gather_start_1b", after=land_1a)
    h1 = _rms_fwd(x1, p["norm_w"][1], name="rms_fwd_1")
    x2, saved[1], big1 = _layer_fwd(x1, h1, p, sp[1], rows_of(land_1a)[0], rest(1, sems_1b, land_1b, land_1a),
                                    1, tabs, proj_after=token_1b)
    bigs = [big0, big1]
    loss, dx, dfw = _final_loss(x2, p["final_norm_w"], loss_target[0], name="final_loss")

    grads = [None] * DEPTH
    rs = {}

    def early(l):
        def begin(g):
            *rs[f"{l}a"], token_a = _swap_start(_grad_views([g[k] for k in big_names[1:]]), name=f"rs_swap_start_{l}a")
            return token_a
        return begin

    def mid(l):
        def go_on(t):
            sems, views, lands = rs[f"{l}a"]
            views, theirs = _swap_wait(sems, views, lands, [t], name=f"rs_swap_wait_{l}a")
            rs[f"{l}a"], token_a = _scatter_begin(views, theirs, tag=f"{l}a")
            return token_a
        return go_on

    def late(l, dproj, dx):
        if l == 0:
            rs["0b"], token_s = _reduce_scatter_begin([grads[0]["winT"]], tag="0b")
            dh = _proj_bwd_dh(dproj, bigs[0]["winT"], 0, token_s)
            return _rms_bwd(saved[0]["x"], p["norm_w"][0], dh, dx, name="rms_bwd_0")
        sems, views, lands, token_b = _swap_start(_grad_views([grads[l]["winT"]]), name=f"rs_swap_start_{l}b")
        dh = _proj_bwd_dh(dproj, bigs[l]["winT"], l, token_b)
        views, theirs = _swap_wait(sems, views, lands, [dh], name=f"rs_swap_wait_{l}b")
        rs[f"{l}b"], token_s = _scatter_begin(views, theirs, tag=f"{l}b")
        return _rms_bwd(saved[l]["x"], p["norm_w"][l], dh, dx, name=f"rms_bwd_{l}", after=token_s)

    def reduced(l, after):
        return _reduce_scatter_end(rs[f"{l}b"], after, tag=f"{l}b") + _reduce_scatter_end(rs[f"{l}a"], after, tag=f"{l}a")

    dproj, grads[1] = _layer_bwd(dx, saved[1], p, bigs[1], 1, tabs, early(1), mid(1))
    dx, grads[1]["norm_w"] = late(1, dproj, dx)
    dproj, grads[0] = _layer_bwd(dx, saved[0], p, bigs[0], 0, tabs, early(0), mid(0))
    dx, grads[0]["norm_w"] = late(0, dproj, dx)
    red1 = reduced(1, [dx])

    tr = lambda t: t.transpose(0, 2, 1)
    view = {k: (tr if k == "w_in" else (lambda t: t)) for k in _BIG}
    shard_grads = lambda red: dict(zip(_BIG, (red[0], red[1], red[2].T, red[3].T, red[4].T, red[5])))
    outs = {k: None for k in _BIG}

    def adamw_big(l, red):
        for k, g in shard_grads(red).items():
            outs[k] = _adamw_layer(view[k](w[k]), g, view[k](m[k]), view[k](v[k]), l, outs[k], name=f"adamw_{k}_{l}")

    adamw_big(1, red1)

    small_like = [w[k] for k in _SMALL]
    gs = [jnp.stack([grads[l][k] for l in range(DEPTH)]) if k != "final_norm_w" else dfw for k in _SMALL]
    gsum = _allreduce_small(_pack(gs + [loss.reshape(1)]), after=[outs[k][0] for k in _BIG])
    adamw_big(0, reduced(0, [gsum]))

    gfull, delta, new_m, new_v = {}, {}, {}, {}
    for k in _BIG:
        gfull[k], delta[k], new_m[k], new_v[k] = (view[k](t) for t in outs[k])
    *small_sums, loss = _unpack(gsum, small_like + [loss])
    for k, t in zip(_SMALL, small_sums):
        gfull[k] = t
        delta[k], new_m[k], new_v[k] = _adamw(w[k], t, m[k], v[k], name=f"adamw_{k}")

    return (loss, dx[None], *[gfull[k] for k in _WEIGHTS], *[delta[k] for k in _WEIGHTS],
            *[new_m[k] for k in _WEIGHTS], *[new_v[k] for k in _WEIGHTS])
```

```python
import functools
import math

import numpy as np
import jax
import jax.numpy as jnp
from jax import lax
from jax.experimental import pallas as pl
from jax.experimental.pallas import tpu as pltpu

F32 = jnp.float32
MXU = jnp.bfloat16
HIGHEST = lax.Precision.HIGHEST

D_MODEL = 2048
DEPTH = 2
EPS = 1e-6
NEG_INF = -1e30
SSM_WIDTH = 1024
SSM_GROUP = 16
SSM_GROUPS = 64
SSM_STATE = 64
SSM_CH = SSM_GROUPS * SSM_STATE
SLAB = 128
SLAB_CH = (SLAB // SSM_GROUP) * SSM_STATE
N_SLAB = SSM_WIDTH // SLAB
SCAN_SEG = 8
SCAN_STEPS = 4
SG_HEADS = 8
SG_CHUNK = 128
HEAD_DIM = 64
ATT_HEADS = 16
ATT_KV_HEADS = 2
GQA_GROUP = 8
ATT_BLOCK = 128
WINDOW = 128
ROT_DIM = 16
ROPE_THETA = 500000.0
N_MAIN = 6400
N_ZC = 1024
N_GATES = 6144
D_IN = N_MAIN + N_ZC + N_GATES

ADAM_LR = 0.001
ADAM_B1 = 0.9
ADAM_B2 = 0.999
ADAM_EPS = 1e-08
ADAM_WD = 0.01
ADAM_STEP = 10

_DIMS = {"nn": (((1,), (0,)), ((), ())), "nt": (((1,), (1,)), ((), ())), "tn": (((0,), (0,)), ((), ()))}
_MB = 1024 * 1024


def _cp(sem, vmem_mb=48):
    return pltpu.CompilerParams(dimension_semantics=sem, vmem_limit_bytes=vmem_mb * _MB)


def _dot(a, b, mode):
    return lax.dot_general(a.astype(MXU), b.astype(MXU), _DIMS[mode], preferred_element_type=F32)


@jax.custom_vjp
def _mm_nn(a, b):
    return _dot(a, b, "nn")


def _mm_nn_fwd(a, b):
    return _dot(a, b, "nn"), (a, b)


def _mm_nn_bwd(res, g):
    a, b = res
    return _dot(g, b, "nt"), _dot(a, g, "tn")


_mm_nn.defvjp(_mm_nn_fwd, _mm_nn_bwd)


@jax.custom_vjp
def _mm_nt(a, bt):
    return _dot(a, bt, "nt")


def _mm_nt_fwd(a, bt):
    return _dot(a, bt, "nt"), (a, bt)


def _mm_nt_bwd(res, g):
    a, bt = res
    return _dot(g, bt, "nn"), _dot(g, a, "tn")


_mm_nt.defvjp(_mm_nt_fwd, _mm_nt_bwd)


def _rmsnorm(x, w):
    return x * lax.rsqrt(jnp.mean(x * x, axis=-1, keepdims=True) + EPS) * w


def _layernorm(x, w, b):
    mu = jnp.mean(x, axis=-1, keepdims=True)
    var = jnp.mean(jnp.square(x - mu), axis=-1, keepdims=True)
    return (x - mu) * lax.rsqrt(var + EPS) * w + b


def _silu(x):
    return x * jax.nn.sigmoid(x)


def _matmul(a, b, mode, *, name, shape, tm, tn, tk, out_dtype=F32, add=None, a_off=(0, 0), b_off=(0, 0), after=None,
            vmem_mb=48):
    m, n, k = shape
    tm, tn, tk = min(tm, m), min(tn, n), min(tk, k)
    assert m % tm == 0 and n % tn == 0 and k % tk == 0, (name, shape, tm, tn, tk)
    nk = k // tk
    has_add, has_after = add is not None, after is not None

    def body(*refs):
        a_ref, b_ref = refs[0], refs[1]
        pos = 2
        add_ref = None
        if has_add:
            add_ref = refs[pos]
            pos += 1
        if has_after:
            pos += 1
        o_ref = refs[pos]
        p = _dot(a_ref[...], b_ref[...], mode)
        if nk == 1:
            if has_add:
                p = p + add_ref[...].astype(F32)
            o_ref[...] = p.astype(out_dtype)
            return
        acc_ref = refs[pos + 1]
        kk = pl.program_id(2)

        @pl.when(kk == 0)
        def _():
            acc_ref[...] = p

        @pl.when(kk > 0)
        def _():
            acc_ref[...] += p

        @pl.when(kk == nk - 1)
        def _():
            r = acc_ref[...]
            if has_add:
                r = r + add_ref[...].astype(F32)
            o_ref[...] = r.astype(out_dtype)

    a0, a1 = a_off
    b0, b1 = b_off
    if mode == "tn":
        a_spec = pl.BlockSpec((tk, tm), lambda i, j, kk: (kk + a0, i + a1))
    else:
        a_spec = pl.BlockSpec((tm, tk), lambda i, j, kk: (i + a0, kk + a1))
    if mode == "nt":
        b_spec = pl.BlockSpec((tn, tk), lambda i, j, kk: (j + b0, kk + b1))
    else:
        b_spec = pl.BlockSpec((tk, tn), lambda i, j, kk: (kk + b0, j + b1))
    in_specs = [a_spec, b_spec]
    args = [a, b]
    if has_add:
        in_specs.append(pl.BlockSpec((tm, tn), lambda i, j, kk: (i, j)))
        args.append(add)
    if has_after:
        in_specs.append(pl.BlockSpec(memory_space=pl.ANY))
        args.append(after)
    return pl.pallas_call(
        body, name=name, grid=(m // tm, n // tn, nk),
        in_specs=in_specs,
        out_specs=pl.BlockSpec((tm, tn), lambda i, j, kk: (i, j)),
        out_shape=jax.ShapeDtypeStruct((m, n), out_dtype),
        scratch_shapes=[pltpu.VMEM((tm, tn), F32)] if nk > 1 else [],
        compiler_params=_cp(("parallel", "parallel", "arbitrary"), vmem_mb),
    )(*args)


def _rms_fwd(x, w, *, name, tm=256, after=None):
    L, d = x.shape
    tm = min(tm, L)
    extra = [] if after is None else [after]

    def body(x_ref, w_ref, *rest):
        rest[-1][...] = _rmsnorm(x_ref[...], w_ref[...]).astype(MXU)

    return pl.pallas_call(
        body, name=name, grid=(L // tm,),
        in_specs=([pl.BlockSpec((tm, d), lambda i: (i, 0)), pl.BlockSpec((1, d), lambda i: (0, 0))]
                  + [pl.BlockSpec(memory_space=pl.ANY)] * len(extra)),
        out_specs=pl.BlockSpec((tm, d), lambda i: (i, 0)),
        out_shape=jax.ShapeDtypeStruct((L, d), MXU),
        compiler_params=_cp(("parallel",)),
    )(x, w.reshape(1, d), *extra)


def _rms_bwd(x, w, dh, dxn, *, name, tm=256, after=None):
    L, d = x.shape
    tm = min(tm, L)
    extra = [] if after is None else [after]

    def body(x_ref, w_ref, dh_ref, dxn_ref, *rest):
        dx_ref, dw_ref = rest[-2:]
        _, vjp = jax.vjp(_rmsnorm, x_ref[...], w_ref[...])
        dx, dw = vjp(dh_ref[...])
        dx_ref[...] = dx + dxn_ref[...]

        @pl.when(pl.program_id(0) == 0)
        def _():
            dw_ref[...] = jnp.zeros_like(dw_ref)

        dw_ref[...] += dw

    row = pl.BlockSpec((tm, d), lambda i: (i, 0))
    vec = pl.BlockSpec((1, d), lambda i: (0, 0))
    dx, dw = pl.pallas_call(
        body, name=name, grid=(L // tm,),
        in_specs=[row, vec, row, row] + [pl.BlockSpec(memory_space=pl.ANY)] * len(extra), out_specs=[row, vec],
        out_shape=[jax.ShapeDtypeStruct((L, d), F32), jax.ShapeDtypeStruct((1, d), F32)],
        compiler_params=_cp(("arbitrary",)),
    )(x, w.reshape(1, d), dh, dxn, *extra)
    return dx, dw.reshape(d)


def _final_loss(x, w, tgt, *, name, tm=256):
    L, d = x.shape
    tm = min(tm, L)

    def loss_fn(xv, wv, tv):
        err = jnp.square(_rmsnorm(xv, wv) - tv)
        return 0.5 * jnp.sum(jnp.mean(err, axis=-1, keepdims=True), axis=0, keepdims=True)

    def body(x_ref, w_ref, t_ref, loss_ref, dx_ref, dw_ref):
        tv = t_ref[...]
        val, vjp = jax.vjp(lambda xv, wv: loss_fn(xv, wv, tv), x_ref[...], w_ref[...])
        dx, dw = vjp(jnp.ones((1, 1), F32))
        dx_ref[...] = dx

        @pl.when(pl.program_id(0) == 0)
        def _():
            dw_ref[...] = jnp.zeros_like(dw_ref)
            loss_ref[...] = jnp.zeros_like(loss_ref)

        dw_ref[...] += dw
        loss_ref[...] += jnp.broadcast_to(val, loss_ref.shape)

    row = pl.BlockSpec((tm, d), lambda i: (i, 0))
    vec = pl.BlockSpec((1, d), lambda i: (0, 0))
    loss, dx, dw = pl.pallas_call(
        body, name=name, grid=(L // tm,),
        in_specs=[row, vec, row],
        out_specs=[pl.BlockSpec((8, 128), lambda i: (0, 0)), row, vec],
        out_shape=[jax.ShapeDtypeStruct((8, 128), F32), jax.ShapeDtypeStruct((L, d), F32),
                   jax.ShapeDtypeStruct((1, d), F32)],
        compiler_params=_cp(("arbitrary",)),
    )(x, w.reshape(1, d), tgt)
    return loss[0, 0], dx, dw.reshape(d)


PARAM_ROWS = 512


def _s5_param_fn(are, aim, ldt, bre, bim, row0):
    n = are.shape[0]
    grp = (row0 + lax.broadcasted_iota(jnp.int32, (n, SSM_GROUPS), 0)) // SSM_STATE
    col = lax.broadcasted_iota(jnp.int32, (n, SSM_GROUPS), 1)
    sel = (grp == col).astype(F32)
    dt = jnp.sum(sel * jnp.exp(ldt), axis=-1, keepdims=True)
    mag = jnp.exp(are * dt)
    ang = aim * dt
    lbr = mag * jnp.cos(ang)
    lbi = mag * jnp.sin(ang)
    den = are * are + aim * aim
    nr = lbr - 1.0
    kr = (nr * are + lbi * aim) / den
    ki = (lbi * are - nr * aim) / den
    return lbr, lbi, kr * bre - ki * bim, kr * bim + ki * bre


def _s5_param_specs():
    col = pl.BlockSpec((PARAM_ROWS, 1), lambda i: (i, 0))
    mat = pl.BlockSpec((PARAM_ROWS, SSM_GROUP), lambda i: (i, 0))
    vec = pl.BlockSpec((1, SSM_GROUPS), lambda i: (0, 0))
    return col, mat, vec


def _s5_params_fwd(are, aim, ldt, bre, bim, *, name, after=None):
    n = are.shape[0]
    col, mat, vec = _s5_param_specs()
    extra = [] if after is None else [after]

    def body(are_ref, aim_ref, ldt_ref, bre_ref, bim_ref, *rest):
        lbr_ref, lbi_ref, bbr_ref, bbi_ref = rest[-4:]
        row0 = pl.program_id(0) * PARAM_ROWS
        lbr, lbi, bbr, bbi = _s5_param_fn(are_ref[...], aim_ref[...], ldt_ref[...], bre_ref[...], bim_ref[...], row0)
        lbr_ref[...] = lbr
        lbi_ref[...] = lbi
        bbr_ref[...] = bbr
        bbi_ref[...] = bbi

    cshape = jax.ShapeDtypeStruct((n, 1), F32)
    mshape = jax.ShapeDtypeStruct((n, SSM_GROUP), F32)
    return pl.pallas_call(body, name=name, grid=(n // PARAM_ROWS,),
                          in_specs=[col, col, vec, mat, mat] + [pl.BlockSpec(memory_space=pl.ANY)] * len(extra),
                          out_specs=[col, col, mat, mat], out_shape=[cshape, cshape, mshape, mshape],
                          compiler_params=_cp(("parallel",)))(are, aim, ldt, bre, bim, *extra)


def _s5_params_bwd(are, aim, ldt, bre, bim, dlbr, dlbi, dbbr, dbbi, *, name):
    n = are.shape[0]
    col, mat, vec = _s5_param_specs()

    def body(are_ref, aim_ref, ldt_ref, bre_ref, bim_ref, g0, g1, g2, g3, o0, o1, o2, o3, o4):
        row0 = pl.program_id(0) * PARAM_ROWS
        _, vjp = jax.vjp(lambda a, b, c, d, e: _s5_param_fn(a, b, c, d, e, row0),
                         are_ref[...], aim_ref[...], ldt_ref[...], bre_ref[...], bim_ref[...])
        dare, daim, dldt, dbre, dbim = vjp((g0[...], g1[...], g2[...], g3[...]))
        o0[...] = dare
        o1[...] = daim
        o3[...] = dbre
        o4[...] = dbim

        @pl.when(pl.program_id(0) == 0)
        def _():
            o2[...] = jnp.zeros_like(o2)

        o2[...] += dldt

    cshape = jax.ShapeDtypeStruct((n, 1), F32)
    mshape = jax.ShapeDtypeStruct((n, SSM_GROUP), F32)
    return pl.pallas_call(body, name=name, grid=(n // PARAM_ROWS,),
                          in_specs=[col, col, vec, mat, mat, col, col, mat, mat],
                          out_specs=[col, col, vec, mat, mat],
                          out_shape=[cshape, cshape, jax.ShapeDtypeStruct((1, SSM_GROUPS), F32), mshape, mshape],
                          compiler_params=_cp(("arbitrary",)))(are, aim, ldt, bre, bim, dlbr, dlbi, dbbr, dbbi)


SLAB_NC = SLAB_CH // 128


def _s5_specs(L):
    slab = pl.BlockSpec((L, SLAB), lambda s: (0, s))
    wspec = pl.BlockSpec((SLAB_NC, 128, SLAB), lambda s: (s, 0, 0))
    lspec = pl.BlockSpec((SLAB_NC, 1, 128), lambda s: (s, 0, 0))
    sspec = pl.BlockSpec((SLAB_NC, L, 128), lambda s: (s, 0, 0))
    dspec = pl.BlockSpec((1, SLAB), lambda s: (0, s))
    return slab, wspec, lspec, sspec, dspec


def _scan_inplace(sr_ref, si_ref, lr, li, pr_ref, pi_ref, *, reverse):
    NC, L, W = sr_ref.shape
    S = SCAN_SEG
    T = L // S
    lr8 = [jnp.broadcast_to(lr[k], (S, W)) for k in range(NC)]
    li8 = [jnp.broadcast_to(li[k], (S, W)) for k in range(NC)]

    def tiles(first, count):
        return pl.ds(first * S, count * S)

    for k in range(NC):
        pr_ref[k, tiles(T - 1 if reverse else 0, 1), :] = lr8[k]
        pi_ref[k, tiles(T - 1 if reverse else 0, 1), :] = li8[k]
        n = 1
        while n < T:
            have = tiles(T - n, n) if reverse else tiles(0, n)
            new = tiles(T - 2 * n, n) if reverse else tiles(n, n)
            top = tiles(T - n, 1) if reverse else tiles(n - 1, 1)
            ar, ai = pr_ref[k, top, :][None], pi_ref[k, top, :][None]
            hr, hi = pr_ref[k, have, :].reshape(n, S, W), pi_ref[k, have, :].reshape(n, S, W)
            pr_ref[k, new, :] = (hr * ar - hi * ai).reshape(n * S, W)
            pi_ref[k, new, :] = (hr * ai + hi * ar).reshape(n * S, W)
            n *= 2

    def step(i, carry):
        for u in range(SCAN_STEPS):
            jj = i * SCAN_STEPS + u
            rows = pl.ds(pl.multiple_of(((T - 1 - jj) if reverse else jj) * S, S), S)
            out = []
            for k in range(NC):
                sr, si = carry[k]
                nsr = lr8[k] * sr - li8[k] * si + sr_ref[k, rows, :]
                nsi = lr8[k] * si + li8[k] * sr + si_ref[k, rows, :]
                sr_ref[k, rows, :] = nsr
                si_ref[k, rows, :] = nsi
                out.append((nsr, nsi))
            carry = tuple(out)
        return carry

    zero = jnp.zeros((S, W), F32)
    ends = lax.fori_loop(0, T // SCAN_STEPS, step, tuple((zero, zero) for k in range(NC)))
    sub = lax.broadcasted_iota(jnp.int32, (S, W), 0)
    order = range(S - 1, -1, -1) if reverse else range(S)
    for k in range(NC):
        er, ei = ends[k]
        full = tiles(0 if reverse else T - 1, 1)
        ltr = pr_ref[k, full, :][0:1]
        lti = pi_ref[k, full, :][0:1]
        cr = jnp.zeros((1, W), F32)
        ci = jnp.zeros((1, W), F32)
        ctr = jnp.zeros((S, W), F32)
        cti = jnp.zeros((S, W), F32)
        for seg in order:
            ctr = jnp.where(sub == seg, cr, ctr)
            cti = jnp.where(sub == seg, ci, cti)
            cr, ci = (er[seg:seg + 1, :] + ltr * cr - lti * ci, ei[seg:seg + 1, :] + ltr * ci + lti * cr)
        pr = pr_ref[k].reshape(T, S, W)
        pi = pi_ref[k].reshape(T, S, W)
        sr_ref[k] += (pr * ctr[None] - pi * cti[None]).reshape(L, W)
        si_ref[k] += (pr * cti[None] + pi * ctr[None]).reshape(L, W)


def _time_interleave(a):
    L, W = a.shape
    return a.reshape(SCAN_SEG, L // SCAN_SEG, W).transpose(1, 0, 2).reshape(L, W)


def _time_deinterleave(a):
    L, W = a.shape
    return a.reshape(L // SCAN_SEG, SCAN_SEG, W).transpose(1, 0, 2).reshape(L, W)


def _s5_fwd(u, btr, bti, cbr, cbi, lbr, lbi, dvec, *, name, after=None):
    L = u.shape[0]
    extra = [] if after is None else [after]

    def body(u_ref, btr_ref, bti_ref, cbr_ref, cbi_ref, lr_ref, li_ref, d_ref, *rest):
        ys_ref, sr_ref, si_ref, pr_ref, pi_ref = rest[-5:]
        u = u_ref[...]
        for k in range(SLAB_NC):
            sr_ref[k] = _dot(u, btr_ref[k], "nt")
            si_ref[k] = _dot(u, bti_ref[k], "nt")
        _scan_inplace(sr_ref, si_ref, lr_ref[...], li_ref[...], pr_ref, pi_ref, reverse=False)
        ys = d_ref[...] * u
        for k in range(SLAB_NC):
            ys = ys + _dot(sr_ref[k], cbr_ref[k], "nn") - _dot(si_ref[k], cbi_ref[k], "nn")
        ys_ref[...] = ys

    slab, wspec, lspec, sspec, dspec = _s5_specs(L)
    sshape = jax.ShapeDtypeStruct((N_SLAB * SLAB_NC, L, 128), F32)
    return pl.pallas_call(
        body, name=name, grid=(N_SLAB,),
        in_specs=[slab, wspec, wspec, wspec, wspec, lspec, lspec, dspec] + [pl.BlockSpec(memory_space=pl.ANY)] * len(extra),
        out_specs=[slab, sspec, sspec],
        out_shape=[jax.ShapeDtypeStruct((L, SSM_WIDTH), F32), sshape, sshape],
        scratch_shapes=[pltpu.VMEM((SLAB_NC, L, 128), F32), pltpu.VMEM((SLAB_NC, L, 128), F32)],
        compiler_params=_cp(("parallel",), 56),
    )(u, btr, bti, cbr, cbi, lbr, lbi, dvec, *extra)


def _s5_bwd(dys, u, sr, si, btr, bti, cbr, cbi, lbr, lbi, dvec, *, name, after=None):
    L = u.shape[0]
    S = SCAN_SEG
    extra = [] if after is None else [after]

    def body(dys_ref, u_ref, sr_ref, si_ref, btr_ref, bti_ref, cbr_ref, cbi_ref, lr_ref, li_ref, d_ref, *rest):
        (du_ref, dbtr_ref, dbti_ref, dcbr_ref, dcbi_ref, dlr_ref, dli_ref, dd_ref,
         ar_ref, ai_ref, pr_ref, pi_ref) = rest[-12:]
        dys = dys_ref[...]
        u = u_ref[...]
        for k in range(SLAB_NC):
            ar_ref[k] = _dot(dys, cbr_ref[k], "nt")
            ai_ref[k] = -_dot(dys, cbi_ref[k], "nt")
        _scan_inplace(ar_ref, ai_ref, lr_ref[...], -li_ref[...], pr_ref, pi_ref, reverse=True)
        head = lax.broadcasted_iota(jnp.int32, (L, 1), 0) < S
        sub0 = lax.broadcasted_iota(jnp.int32, (S, 1), 0) == 0

        def prev_state(s):
            up = pltpu.roll(s, S, 0)
            return jnp.where(head, 0.0, up), jnp.where(sub0, 0.0, pltpu.roll(up[0:S], 1, 0))

        du = d_ref[...] * dys
        for k in range(SLAB_NC):
            a_re = ar_ref[k]
            a_im = ai_ref[k]
            du = du + _dot(a_re, btr_ref[k], "nn") + _dot(a_im, bti_ref[k], "nn")
            dbtr_ref[k] = _dot(a_re, u, "tn")
            dbti_ref[k] = _dot(a_im, u, "tn")
            s_re = sr_ref[k]
            s_im = si_ref[k]
            dcbr_ref[k] = _dot(s_re, dys, "tn")
            dcbi_ref[k] = -_dot(s_im, dys, "tn")
            p_re, q_re = prev_state(s_re)
            p_im, q_im = prev_state(s_im)
            b_re, b_im = a_re[0:S], a_im[0:S]
            dlr_ref[k] = (jnp.sum(p_re * a_re + p_im * a_im, axis=0, keepdims=True)
                          + jnp.sum(q_re * b_re + q_im * b_im, axis=0, keepdims=True))
            dli_ref[k] = (jnp.sum(p_re * a_im - p_im * a_re, axis=0, keepdims=True)
                          + jnp.sum(q_re * b_im - q_im * b_re, axis=0, keepdims=True))
        du_ref[...] = du
        dd_ref[...] = jnp.sum(dys * u, axis=0, keepdims=True)

    slab, wspec, lspec, sspec, dspec = _s5_specs(L)
    wshape = jax.ShapeDtypeStruct((N_SLAB * SLAB_NC, 128, SLAB), F32)
    lshape = jax.ShapeDtypeStruct((N_SLAB * SLAB_NC, 1, 128), F32)
    return pl.pallas_call(
        body, name=name, grid=(N_SLAB,),
        in_specs=([slab, slab, sspec, sspec, wspec, wspec, wspec, wspec, lspec, lspec, dspec]
                  + [pl.BlockSpec(memory_space=pl.ANY)] * len(extra)),
        out_specs=[slab, wspec, wspec, wspec, wspec, lspec, lspec, dspec],
        out_shape=[jax.ShapeDtypeStruct((L, SSM_WIDTH), F32), wshape, wshape, wshape, wshape, lshape, lshape,
                   jax.ShapeDtypeStruct((1, SSM_WIDTH), F32)],
        scratch_shapes=[pltpu.VMEM((SLAB_NC, L, 128), F32)] * 4,
        compiler_params=_cp(("parallel",), 56),
    )(dys, u, sr, si, btr, bti, cbr, cbi, lbr, lbi, dvec, *extra)


_SLAB_MASK = (np.arange(SLAB_CH)[:, None] // SSM_STATE == np.arange(SLAB)[None, :] // SSM_GROUP)


def _expand_bd(x):
    t = jnp.tile(x.reshape(N_SLAB, SLAB_CH, SSM_GROUP), (1, 1, SLAB // SSM_GROUP))
    return jnp.where(_SLAB_MASK[None], t, 0.0).astype(MXU).reshape(N_SLAB * SLAB_NC, 128, SLAB)


def _contract_bd(dx):
    t = jnp.where(_SLAB_MASK[None], dx.reshape(N_SLAB, SLAB_CH, SLAB), 0.0)
    return jnp.sum(t.reshape(N_SLAB, SLAB_CH, SLAB // SSM_GROUP, SSM_GROUP), axis=2).reshape(SSM_CH, SSM_GROUP)


def _glu_ew(ys, zlin, za):
    a1 = jax.nn.gelu(ys)
    return a1 * jax.nn.sigmoid(zlin) * _silu(za)


def _glu_fwd(ys, main, gw, gb, *, name, tm=256):
    L = ys.shape[0]
    tm = min(tm, L)
    W = SSM_WIDTH

    def body(ys_ref, za_ref, gw_ref, gb_ref, ya_ref):
        ys = ys_ref[...]
        a1 = jax.nn.gelu(ys)
        zlin = _dot(a1, gw_ref[...], "nn") + gb_ref[...]
        ya_ref[...] = _glu_ew(ys, zlin, za_ref[...]).astype(MXU)

    return pl.pallas_call(
        body, name=name, grid=(L // tm,),
        in_specs=[pl.BlockSpec((tm, W), lambda i: (i, 0)), pl.BlockSpec((tm, W), lambda i: (i, 1)),
                  pl.BlockSpec((W, W), lambda i: (0, 0)), pl.BlockSpec((1, W), lambda i: (0, 0))],
        out_specs=pl.BlockSpec((tm, W), lambda i: (i, 0)),
        out_shape=jax.ShapeDtypeStruct((L, W), MXU),
        compiler_params=_cp(("parallel",)),
    )(ys, main, gw, gb.reshape(1, W))


def _glu_bwd(dya, ys, main, gw, gb, *, name, tm=256):
    L = ys.shape[0]
    tm = min(tm, L)
    W = SSM_WIDTH

    def body(dya_ref, ys_ref, za_ref, gw_ref, gb_ref, dys_ref, dza_ref, a1_ref, dzl_ref, db_ref):
        ys = ys_ref[...]
        a1, gelu_vjp = jax.vjp(jax.nn.gelu, ys)
        zlin = _dot(a1, gw_ref[...], "nn") + gb_ref[...]
        _, vjp = jax.vjp(lambda a, z, za: a * jax.nn.sigmoid(z) * _silu(za), a1, zlin, za_ref[...])
        da1, dzlin, dza = vjp(dya_ref[...].astype(F32))
        da1 = da1 + _dot(dzlin, gw_ref[...], "nt")
        dys_ref[...] = gelu_vjp(da1)[0]
        dza_ref[...] = dza
        a1_ref[...] = a1.astype(MXU)
        dzl_ref[...] = dzlin.astype(MXU)

        @pl.when(pl.program_id(0) == 0)
        def _():
            db_ref[...] = jnp.zeros_like(db_ref)

        db_ref[...] += jnp.sum(dzlin, axis=0, keepdims=True)

    row = pl.BlockSpec((tm, W), lambda i: (i, 0))
    vec = pl.BlockSpec((1, W), lambda i: (0, 0))
    return pl.pallas_call(
        body, name=name, grid=(L // tm,),
        in_specs=[row, row, pl.BlockSpec((tm, W), lambda i: (i, 1)), pl.BlockSpec((W, W), lambda i: (0, 0)), vec],
        out_specs=[row, row, row, row, vec],
        out_shape=[jax.ShapeDtypeStruct((L, W), F32), jax.ShapeDtypeStruct((L, W), F32),
                   jax.ShapeDtypeStruct((L, W), MXU), jax.ShapeDtypeStruct((L, W), MXU),
                   jax.ShapeDtypeStruct((1, W), F32)],
        compiler_params=_cp(("arbitrary",)),
    )(dya, ys, main, gw, gb.reshape(1, W))


def _sg_fn(ub, vb, zb, lnw, lnb, ws, bs):
    u = jax.nn.gelu(ub)
    v = _layernorm(jax.nn.gelu(vb), lnw, lnb)
    r = lax.broadcasted_iota(jnp.int32, (SG_CHUNK, SG_CHUNK), 0)
    c = lax.broadcasted_iota(jnp.int32, (SG_CHUNK, SG_CHUNK), 1)
    tri = r >= c
    outs = []
    for h in range(SG_HEADS):
        wh = jnp.where(tri, ws[h], 0.0)
        outs.append(_mm_nn(wh, v[:, h * 128:(h + 1) * 128]) + bs[h])
    mixed = jnp.concatenate(outs, axis=1)
    return u * mixed * _silu(zb)


def _sg_specs(L):
    W = SSM_WIDTH
    blk = lambda c: pl.BlockSpec((SG_CHUNK, W), lambda i, c=c: (i, c))
    vec = pl.BlockSpec((1, W), lambda i: (0, 0))
    wspec = pl.BlockSpec((SG_HEADS, SG_CHUNK, SG_CHUNK), lambda i: (0, 0, 0))
    bspec = pl.BlockSpec((SG_HEADS, SG_CHUNK, 1), lambda i: (0, 0, 0))
    return blk, vec, wspec, bspec


def _sg_fwd(main, lnw, lnb, sgw, sgb, *, name):
    L = main.shape[0]
    W = SSM_WIDTH
    blk, vec, wspec, bspec = _sg_specs(L)

    def body(ub_ref, vb_ref, zb_ref, lnw_ref, lnb_ref, w_ref, b_ref, yb_ref):
        ws = [w_ref[h] for h in range(SG_HEADS)]
        bs = [b_ref[h] for h in range(SG_HEADS)]
        yb_ref[...] = _sg_fn(ub_ref[...], vb_ref[...], zb_ref[...], lnw_ref[...], lnb_ref[...], ws, bs).astype(MXU)

    return pl.pallas_call(
        body, name=name, grid=(L // SG_CHUNK,),
        in_specs=[blk(2), blk(3), blk(4), vec, vec, wspec, bspec],
        out_specs=pl.BlockSpec((SG_CHUNK, W), lambda i: (i, 0)),
        out_shape=jax.ShapeDtypeStruct((L, W), MXU),
        compiler_params=_cp(("parallel",)),
    )(main, main, main, lnw.reshape(1, W), lnb.reshape(1, W), sgw, sgb.reshape(SG_HEADS, SG_CHUNK, 1))


def _sg_bwd(dyb, main, lnw, lnb, sgw, sgb, *, name):
    L = main.shape[0]
    W = SSM_WIDTH
    blk, vec, wspec, bspec = _sg_specs(L)

    def body(dyb_ref, ub_ref, vb_ref, zb_ref, lnw_ref, lnb_ref, w_ref, b_ref,
             dub_ref, dvb_ref, dzb_ref, dlnw_ref, dlnb_ref, dw_ref, db_ref):
        ws = [w_ref[h] for h in range(SG_HEADS)]
        bs = [b_ref[h] for h in range(SG_HEADS)]
        _, vjp = jax.vjp(_sg_fn, ub_ref[...], vb_ref[...], zb_ref[...], lnw_ref[...], lnb_ref[...], ws, bs)
        dub, dvb, dzb, dlnw, dlnb, dws, dbs = vjp(dyb_ref[...])

        @pl.when(pl.program_id(0) == 0)
        def _():
            dlnw_ref[...] = jnp.zeros_like(dlnw_ref)
            dlnb_ref[...] = jnp.zeros_like(dlnb_ref)
            dw_ref[...] = jnp.zeros_like(dw_ref)
            db_ref[...] = jnp.zeros_like(db_ref)

        dub_ref[...] = dub
        dvb_ref[...] = dvb
        dzb_ref[...] = dzb
        dlnw_ref[...] += dlnw
        dlnb_ref[...] += dlnb
        for h in range(SG_HEADS):
            dw_ref[h] += dws[h]
            db_ref[h] += dbs[h]

    row = pl.BlockSpec((SG_CHUNK, W), lambda i: (i, 0))
    out = jax.ShapeDtypeStruct((L, W), F32)
    return pl.pallas_call(
        body, name=name, grid=(L // SG_CHUNK,),
        in_specs=[row, blk(2), blk(3), blk(4), vec, vec, wspec, bspec],
        out_specs=[row, row, row, vec, vec, wspec, bspec],
        out_shape=[out, out, out, jax.ShapeDtypeStruct((1, W), F32), jax.ShapeDtypeStruct((1, W), F32),
                   jax.ShapeDtypeStruct((SG_HEADS, SG_CHUNK, SG_CHUNK), F32),
                   jax.ShapeDtypeStruct((SG_HEADS, SG_CHUNK, 1), F32)],
        compiler_params=_cp(("arbitrary",)),
    )(dyb, main, main, main, lnw.reshape(1, W), lnb.reshape(1, W), sgw, sgb.reshape(SG_HEADS, SG_CHUNK, 1))


def _rope_tables(L):
    half = ROT_DIM // 2
    inv_freq = ROPE_THETA ** (-jnp.arange(0, ROT_DIM, 2, dtype=F32) / ROT_DIM)
    ang = jnp.arange(L, dtype=F32)[:, None] * inv_freq[None, :]
    cos = jnp.cos(ang)
    sin = jnp.sin(ang)
    ones = jnp.ones((L, HEAD_DIM - ROT_DIM), F32)
    cosf = jnp.concatenate([cos, cos, ones], axis=1)
    sinf = jnp.concatenate([sin, sin, 0.0 * ones], axis=1)
    rot = np.zeros((HEAD_DIM, HEAD_DIM), np.float32)
    for d in range(half):
        rot[d + half, d] = -1.0
        rot[d, d + half] = 1.0
    return cosf, sinf, jnp.asarray(rot)


def _rope(t, cosf, sinf, rot):
    shp = t.shape
    t2 = t.reshape(-1, HEAD_DIM)
    sw = lax.dot_general(t2, rot, _DIMS["nn"], precision=lax.Precision.HIGH, preferred_element_type=F32).reshape(shp)
    return t * cosf + sw * sinf


def _attn_core_parts(s, va, sink):
    h, q, k = s.shape
    m = jnp.maximum(jnp.max(s, axis=-1, keepdims=True), sink)
    e = jnp.exp(s - m)
    es = jnp.exp(sink - m)
    ev = _dot(e.reshape(h * q, k), va, "nn")
    r = 1.0 / (ev[:, HEAD_DIM:HEAD_DIM + 1].reshape(h, q, 1) + es)
    return ev[:, :HEAD_DIM] * r.reshape(h * q, 1), e, r, es


@jax.custom_vjp
def _attn_core(s, v, va, sink):
    return _attn_core_parts(s, va, sink)[0]


def _attn_core_fwd(s, v, va, sink):
    o, e, r, es = _attn_core_parts(s, va, sink)
    return o, (o, e, r, es, v, va)


def _attn_core_bwd(res, do):
    o, e, r, es, v, va = res
    h, q, k = e.shape
    p = e * r
    t = jnp.sum(o * do, axis=-1, keepdims=True).reshape(h, q, 1)
    dp = _dot(do, v, "nt").reshape(h, q, k)
    dv = _dot(p.reshape(h * q, k), do, "tn")
    dsink = -jnp.sum(es * r * t, axis=1, keepdims=True)
    return p * (dp - t), dv, jnp.zeros_like(va), dsink


_attn_core.defvjp(_attn_core_fwd, _attn_core_bwd)


def _attn_block_fn(q, kw, vw, sinks, vaw, cq, sq, ck, sk, rot, q0, k0):
    nk = kw.shape[1]
    qr = _rope(q, cq, sq, rot)
    kr = _rope(kw, ck, sk, rot)
    qpos = q0 + lax.broadcasted_iota(jnp.int32, (1, ATT_BLOCK, nk), 1)
    kpos = k0 + lax.broadcasted_iota(jnp.int32, (1, ATT_BLOCK, nk), 2)
    diff = qpos - kpos
    allowed = (diff >= 0) & (diff < WINDOW)
    outs = []
    for kh in range(ATT_KV_HEADS):
        qh = qr[kh * GQA_GROUP:(kh + 1) * GQA_GROUP].reshape(GQA_GROUP * ATT_BLOCK, HEAD_DIM)
        s = _mm_nt(qh, kr[kh]).reshape(GQA_GROUP, ATT_BLOCK, nk) * (HEAD_DIM ** -0.5)
        s = jnp.where(allowed, s, NEG_INF)
        o = _attn_core(s, vw[kh], vaw[kh], sinks[kh * GQA_GROUP:(kh + 1) * GQA_GROUP])
        outs.append(o.reshape(GQA_GROUP, ATT_BLOCK, HEAD_DIM))
    return jnp.concatenate(outs, axis=0)


def _attn_common(L):
    nwin = min(2 * ATT_BLOCK, L)
    qspec = pl.BlockSpec((ATT_HEADS, ATT_BLOCK, HEAD_DIM), lambda n: (0, n, 0))
    kvspec = pl.BlockSpec((ATT_KV_HEADS, L, HEAD_DIM), lambda n: (0, 0, 0))
    sspec = pl.BlockSpec((ATT_HEADS, 1, 1), lambda n: (0, 0, 0))
    tq = pl.BlockSpec((ATT_BLOCK, HEAD_DIM), lambda n: (n, 0))
    tk = pl.BlockSpec((L, HEAD_DIM), lambda n: (0, 0))
    rspec = pl.BlockSpec((HEAD_DIM, HEAD_DIM), lambda n: (0, 0))
    vaspec = pl.BlockSpec((ATT_KV_HEADS, L, 2 * HEAD_DIM), lambda n: (0, 0, 0))
    return nwin, qspec, kvspec, sspec, tq, tk, rspec, vaspec


def _v_with_ones(vh):
    return jnp.concatenate([vh, jnp.ones_like(vh)], axis=-1).astype(MXU)


def _attn_fwd(qh, kh, vh, sinks, cosf, sinf, rot, *, name):
    L = qh.shape[1]
    nwin, qspec, kvspec, sspec, tq, tk, rspec, vaspec = _attn_common(L)

    def body(q_ref, k_ref, v_ref, s_ref, va_ref, cq_ref, sq_ref, ck_ref, sk_ref, r_ref, o_ref):
        n = pl.program_id(0)
        k0 = pl.multiple_of(jnp.maximum(n - 1, 0) * ATT_BLOCK, ATT_BLOCK)
        win = pl.ds(k0, nwin)
        o_ref[...] = _attn_block_fn(q_ref[...], k_ref[:, win, :], v_ref[:, win, :], s_ref[...], va_ref[:, win, :],
                                    cq_ref[...], sq_ref[...], ck_ref[win, :], sk_ref[win, :], r_ref[...],
                                    n * ATT_BLOCK, k0)

    return pl.pallas_call(
        body, name=name, grid=(L // ATT_BLOCK,),
        in_specs=[qspec, kvspec, kvspec, sspec, vaspec, tq, tq, tk, tk, rspec],
        out_specs=qspec,
        out_shape=jax.ShapeDtypeStruct((ATT_HEADS, L, HEAD_DIM), F32),
        compiler_params=_cp(("parallel",)),
    )(qh, kh, vh, sinks.reshape(ATT_HEADS, 1, 1), _v_with_ones(vh), cosf, sinf, cosf, sinf, rot)


def _attn_bwd(do, qh, kh, vh, sinks, cosf, sinf, rot, *, name):
    L = qh.shape[1]
    nwin, qspec, kvspec, sspec, tq, tk, rspec, vaspec = _attn_common(L)

    def body(do_ref, q_ref, k_ref, v_ref, s_ref, va_ref, cq_ref, sq_ref, ck_ref, sk_ref, r_ref,
             dq_ref, dk_ref, dv_ref, ds_ref):
        n = pl.program_id(0)
        k0 = pl.multiple_of(jnp.maximum(n - 1, 0) * ATT_BLOCK, ATT_BLOCK)
        win = pl.ds(k0, nwin)
        cq, sq, ck, sk, rt = cq_ref[...], sq_ref[...], ck_ref[win, :], sk_ref[win, :], r_ref[...]
        vaw = va_ref[:, win, :]
        q0 = n * ATT_BLOCK
        _, vjp = jax.vjp(lambda q, kw, vw, s: _attn_block_fn(q, kw, vw, s, vaw, cq, sq, ck, sk, rt, q0, k0),
                         q_ref[...], k_ref[:, win, :], v_ref[:, win, :], s_ref[...])
        dq, dkw, dvw, ds = vjp(do_ref[...])

        @pl.when(n == 0)
        def _():
            dk_ref[...] = jnp.zeros_like(dk_ref)
            dv_ref[...] = jnp.zeros_like(dv_ref)
            ds_ref[...] = jnp.zeros_like(ds_ref)

        dq_ref[...] = dq
        dk_ref[:, win, :] += dkw
        dv_ref[:, win, :] += dvw
        ds_ref[...] += ds

    return pl.pallas_call(
        body, name=name, grid=(L // ATT_BLOCK,),
        in_specs=[qspec, qspec, kvspec, kvspec, sspec, vaspec, tq, tq, tk, tk, rspec],
        out_specs=[qspec, kvspec, kvspec, sspec],
        out_shape=[jax.ShapeDtypeStruct((ATT_HEADS, L, HEAD_DIM), F32),
                   jax.ShapeDtypeStruct((ATT_KV_HEADS, L, HEAD_DIM), F32),
                   jax.ShapeDtypeStruct((ATT_KV_HEADS, L, HEAD_DIM), F32),
                   jax.ShapeDtypeStruct((ATT_HEADS, 1, 1), F32)],
        compiler_params=_cp(("arbitrary",)),
    )(do, qh, kh, vh, sinks.reshape(ATT_HEADS, 1, 1), _v_with_ones(vh), cosf, sinf, cosf, sinf, rot)


def _to_heads(t, nh):
    L = t.shape[0]
    return t.reshape(L, nh, HEAD_DIM).transpose(1, 0, 2)


def _from_heads(t):
    nh, L, _ = t.shape
    return t.transpose(1, 0, 2).reshape(L, nh * HEAD_DIM)


def _branch_fwd(ya, yb, o2d, zc, gates, wa, wb, wc, *, name, tm=256):
    L = ya.shape[0]
    tm = min(tm, L)
    W, D = SSM_WIDTH, D_MODEL

    def body(ya_ref, yb_ref, o_ref, zc_ref, g0_ref, g1_ref, g2_ref, wa_ref, wb_ref, wc_ref,
             mg_ref, ta_ref, tb_ref, tc_ref, yc_ref):
        yc = (o_ref[...] * _silu(zc_ref[...])).astype(MXU)
        ta = _dot(ya_ref[...], wa_ref[...], "nt")
        tb = _dot(yb_ref[...], wb_ref[...], "nt")
        tc = _dot(yc, wc_ref[...], "nt")
        ta_ref[...] = ta
        tb_ref[...] = tb
        tc_ref[...] = tc
        yc_ref[...] = yc
        mg_ref[...] = (jax.nn.sigmoid(g0_ref[...]) * ta + jax.nn.sigmoid(g1_ref[...]) * tb
                       + jax.nn.sigmoid(g2_ref[...]) * tc).astype(MXU)

    row = pl.BlockSpec((tm, W), lambda i: (i, 0))
    wide = pl.BlockSpec((tm, D), lambda i: (i, 0))
    gate = lambda c: pl.BlockSpec((tm, D), lambda i, c=c: (i, c))
    wspec = pl.BlockSpec((D, W), lambda i: (0, 0))
    return pl.pallas_call(
        body, name=name, grid=(L // tm,),
        in_specs=[row, row, row, row, gate(0), gate(1), gate(2), wspec, wspec, wspec],
        out_specs=[wide, wide, wide, wide, row],
        out_shape=[jax.ShapeDtypeStruct((L, D), MXU), jax.ShapeDtypeStruct((L, D), F32),
                   jax.ShapeDtypeStruct((L, D), F32), jax.ShapeDtypeStruct((L, D), F32),
                   jax.ShapeDtypeStruct((L, W), MXU)],
        compiler_params=_cp(("parallel",), 56),
    )(ya, yb, o2d, zc, gates, gates, gates, wa, wb, wc)


def _branch_bwd(dmg, ta, tb, tc, gates, *, name, tm=256):
    L = dmg.shape[0]
    tm = min(tm, L)
    D = D_MODEL

    def body(dm_ref, ta_ref, tb_ref, tc_ref, g0_ref, g1_ref, g2_ref, da_ref, db_ref, dc_ref, dg_ref):
        dm = dm_ref[...]
        for i, (t_ref, g_ref, d_ref) in enumerate(((ta_ref, g0_ref, da_ref), (tb_ref, g1_ref, db_ref),
                                                   (tc_ref, g2_ref, dc_ref))):
            sg = jax.nn.sigmoid(g_ref[...])
            d_ref[...] = (sg * dm).astype(MXU)
            dg_ref[:, i * D:(i + 1) * D] = (dm * t_ref[...] * sg * (1.0 - sg)).astype(MXU)

    wide = pl.BlockSpec((tm, D), lambda i: (i, 0))
    gate = lambda c: pl.BlockSpec((tm, D), lambda i, c=c: (i, c))
    bf = jax.ShapeDtypeStruct((L, D), MXU)
    return pl.pallas_call(
        body, name=name, grid=(L // tm,),
        in_specs=[wide, wide, wide, wide, gate(0), gate(1), gate(2)],
        out_specs=[wide, wide, wide, pl.BlockSpec((tm, 3 * D), lambda i: (i, 0))],
        out_shape=[bf, bf, bf, jax.ShapeDtypeStruct((L, 3 * D), MXU)],
        compiler_params=_cp(("parallel",), 56),
    )(dmg, ta, tb, tc, gates, gates, gates)


def _gate_c_bwd(dyc, o2d, zc, *, name, tm=256):
    L, W = dyc.shape
    tm = min(tm, L)

    def body(dy_ref, o_ref, z_ref, do_ref, dz_ref):
        _, vjp = jax.vjp(lambda o, z: o * _silu(z), o_ref[...], z_ref[...])
        do, dz = vjp(dy_ref[...])
        do_ref[...] = do
        dz_ref[...] = dz.astype(MXU)

    row = pl.BlockSpec((tm, W), lambda i: (i, 0))
    return pl.pallas_call(body, name=name, grid=(L // tm,), in_specs=[row, row, row], out_specs=[row, row],
                          out_shape=[jax.ShapeDtypeStruct((L, W), F32), jax.ShapeDtypeStruct((L, W), MXU)],
                          compiler_params=_cp(("parallel",)))(dyc, o2d, zc)


def _adamw(w, g, m, v, *, name):
    shape = w.shape
    cols = shape[-1]
    w2, g2, m2, v2 = (t.reshape(-1, cols) for t in (w, g, m, v))
    rows = w2.shape[0]
    tc = 1024 if cols % 1024 == 0 else cols
    lane_cols = -(-tc // 128) * 128
    tr = rows
    while tr % 16 == 0 and tr * lane_cols * 4 > 2 * _MB:
        tr //= 2

    def body(w_ref, g_ref, m_ref, v_ref, d_ref, nm_ref, nv_ref):
        gv = g_ref[...]
        nm = ADAM_B1 * m_ref[...] + (1.0 - ADAM_B1) * gv
        nv = ADAM_B2 * v_ref[...] + (1.0 - ADAM_B2) * jnp.square(gv)
        m_hat = nm / (1.0 - ADAM_B1 ** ADAM_STEP)
        v_hat = nv / (1.0 - ADAM_B2 ** ADAM_STEP)
        d_ref[...] = -ADAM_LR * (m_hat / (jnp.sqrt(v_hat) + ADAM_EPS) + ADAM_WD * w_ref[...])
        nm_ref[...] = nm
        nv_ref[...] = nv

    spec = pl.BlockSpec((tr, tc), lambda i, j: (i, j))
    out = jax.ShapeDtypeStruct((rows, cols), F32)
    d, nm, nv = pl.pallas_call(body, name=name, grid=(rows // tr, cols // tc), in_specs=[spec] * 4,
                               out_specs=[spec] * 3, out_shape=[out, out, out],
                               compiler_params=_cp(("parallel", "parallel")))(w2, g2, m2, v2)
    return d.reshape(shape), nm.reshape(shape), nv.reshape(shape)


def _adamw_layer(w, g, m, v, l, prev, *, name):
    _, rows, cols = w.shape
    tc = 1024 if cols % 1024 == 0 else cols
    tr = rows
    while tr % 16 == 0 and tr * tc * 4 > 2 * _MB:
        tr //= 2

    def body(w_ref, g_ref, m_ref, v_ref, *rest):
        go_ref, d_ref, nm_ref, nv_ref = rest[-4:]
        gv = g_ref[...]
        nm = ADAM_B1 * m_ref[...] + (1.0 - ADAM_B1) * gv
        nv = ADAM_B2 * v_ref[...] + (1.0 - ADAM_B2) * jnp.square(gv)
        m_hat = nm / (1.0 - ADAM_B1 ** ADAM_STEP)
        v_hat = nv / (1.0 - ADAM_B2 ** ADAM_STEP)
        d_ref[...] = -ADAM_LR * (m_hat / (jnp.sqrt(v_hat) + ADAM_EPS) + ADAM_WD * w_ref[...])
        nm_ref[...] = nm
        nv_ref[...] = nv
        go_ref[...] = gv

    lspec = pl.BlockSpec((None, tr, tc), lambda i, j: (l, i, j))
    gspec = pl.BlockSpec((tr, tc), lambda i, j: (i, j))
    out = jax.ShapeDtypeStruct(w.shape, F32)
    extra = [] if prev is None else list(prev)
    return pl.pallas_call(
        body, name=name, grid=(rows // tr, cols // tc),
        in_specs=[lspec, gspec, lspec, lspec] + [_ANY] * len(extra),
        out_specs=[lspec] * 4, out_shape=[out] * 4,
        input_output_aliases={4 + i: i for i in range(len(extra))},
        compiler_params=_cp(("parallel", "parallel")),
    )(w, g, m, v, *extra)


def _prep_layer(p, l, after=None):
    are = p["ssm_a_re"][l].reshape(SSM_CH, 1)
    aim = p["ssm_a_im"][l].reshape(SSM_CH, 1)
    ldt = p["ssm_log_dt"][l].reshape(1, SSM_GROUPS)
    bre = p["ssm_b_re"][l].reshape(SSM_CH, SSM_GROUP)
    bim = p["ssm_b_im"][l].reshape(SSM_CH, SSM_GROUP)
    lbr, lbi, bbr, bbi = _s5_params_fwd(are, aim, ldt, bre, bim, name=f"s5_params_fwd_{l}", after=after)
    cre = p["ssm_c_re"][l].transpose(0, 2, 1).reshape(SSM_CH, SSM_GROUP)
    cim = p["ssm_c_im"][l].transpose(0, 2, 1).reshape(SSM_CH, SSM_GROUP)
    return dict(raw=(are, aim, ldt, bre, bim),
                lbr=lbr.reshape(N_SLAB * SLAB_NC, 1, 128), lbi=lbi.reshape(N_SLAB * SLAB_NC, 1, 128),
                btr=_expand_bd(bbr), bti=_expand_bd(bbi), cbr=_expand_bd(cre), cbi=_expand_bd(cim),
                dvec=p["ssm_d"][l].reshape(1, SSM_WIDTH))


def _layer_fwd(x, h, p, sp, winT, rest_of, l, tabs, proj_after=None, after_main=None):
    L = x.shape[0]
    cosf, sinf, rot = tabs
    mm = functools.partial(_matmul, h, winT, "nt", tm=L, tn=256, tk=D_MODEL)
    main = mm(name=f"proj_main_{l}", shape=(L, N_MAIN, D_MODEL), after=proj_after)
    then = proj_after if after_main is None else after_main(main)
    zc = mm(name=f"proj_zc_{l}", shape=(L, N_ZC, D_MODEL), b_off=(N_MAIN // 256, 0), after=then)
    gates = mm(name=f"proj_gates_{l}", shape=(L, N_GATES, D_MODEL), b_off=((N_MAIN + N_ZC) // 256, 0), after=then)
    big, token = rest_of([main, zc, gates])
    ua = _time_interleave(main[:, :SSM_WIDTH])
    ys, sr, si = _s5_fwd(ua, sp["btr"], sp["bti"], sp["cbr"], sp["cbi"], sp["lbr"], sp["lbi"], sp["dvec"],
                         name=f"s5_fwd_{l}", after=token)
    ys = _time_deinterleave(ys)
    ya = _glu_fwd(ys, main, big["glu_w"], p["ssm_glu_b"][l], name=f"glu_fwd_{l}")
    yb = _sg_fwd(main, p["sg_ln_w"][l], p["sg_ln_b"][l], p["sg_w"][l], p["sg_b"][l], name=f"sg_fwd_{l}")
    qh = _to_heads(main[:, 5120:6144], ATT_HEADS)
    kh = _to_heads(main[:, 6144:6272], ATT_KV_HEADS)
    vh = _to_heads(main[:, 6272:6400], ATT_KV_HEADS)
    oh = _attn_fwd(qh, kh, vh, p["attn_sinks"][l], cosf, sinf, rot, name=f"attn_fwd_{l}")
    o2d = _from_heads(oh)
    mg, ta, tb, tc, yc = _branch_fwd(ya, yb, o2d, zc, gates, big["wbaT"], big["wbbT"], big["wbcT"],
                                     name=f"branch_fwd_{l}")
    xn = _matmul(mg, big["w_out"], "nn", name=f"out_fwd_{l}", shape=(L, D_MODEL, D_MODEL), tm=512, tn=512,
                 tk=D_MODEL, add=x)
    saved = dict(x=x, h=h, main=main, zc=zc, gates=gates, ua=ua, ys=ys, sr=sr, si=si, ya=ya, yb=yb, yc=yc, o2d=o2d,
                 qh=qh, kh=kh, vh=vh, mg=mg, ta=ta, tb=tb, tc=tc, sp=sp)
    return xn, saved, big


def _layer_bwd(dxn, s, p, big, l, tabs, early, mid):
    L = dxn.shape[0]
    D, W = D_MODEL, SSM_WIDTH
    cosf, sinf, rot = tabs
    sp = s["sp"]
    g = {}
    dmg = _matmul(dxn, big["w_out"], "nt", name=f"out_bwd_dm_{l}", shape=(L, D, D), tm=512, tn=512, tk=D)
    g["w_out"] = _matmul(s["mg"], dxn, "tn", name=f"out_bwd_dw_{l}", shape=(D, D, L), tm=512, tn=512, tk=L,
                         out_dtype=MXU)
    dta, dtb, dtc, dgates = _branch_bwd(dmg, s["ta"], s["tb"], s["tc"], s["gates"], name=f"branch_bwd_{l}")
    dys_ = {}
    for nm, dt, y, wt in (("a", dta, s["ya"], big["wbaT"]), ("b", dtb, s["yb"], big["wbbT"]),
                          ("c", dtc, s["yc"], big["wbcT"])):
        dys_[nm] = _matmul(dt, wt, "nn", name=f"branch_bwd_dy{nm}_{l}", shape=(L, W, D), tm=512, tn=512, tk=D)
        g["wb" + nm + "T"] = _matmul(dt, y, "tn", name=f"branch_bwd_dw{nm}_{l}", shape=(D, W, L),
                                     tm=512, tn=512, tk=L, out_dtype=MXU)
    do2d, dzc = _gate_c_bwd(dys_["c"], s["o2d"], s["zc"], name=f"gate_c_bwd_{l}")
    dqh, dkh, dvh, dsinks = _attn_bwd(_to_heads(do2d, ATT_HEADS), s["qh"], s["kh"], s["vh"], p["attn_sinks"][l],
                                      cosf, sinf, rot, name=f"attn_bwd_{l}")
    g["attn_sinks"] = dsinks.reshape(ATT_HEADS)
    dub, dvb, dzb, dlnw, dlnb, dsgw, dsgb = _sg_bwd(dys_["b"], s["main"], p["sg_ln_w"][l], p["sg_ln_b"][l],
                                                    p["sg_w"][l], p["sg_b"][l], name=f"sg_bwd_{l}")
    g["sg_ln_w"], g["sg_ln_b"] = dlnw.reshape(W), dlnb.reshape(W)
    g["sg_w"], g["sg_b"] = dsgw, dsgb.reshape(SG_HEADS, SG_CHUNK)
    dys, dza, a1, dzl, dgb = _glu_bwd(dys_["a"], s["ys"], s["main"], big["glu_w"], p["ssm_glu_b"][l],
                                      name=f"glu_bwd_{l}")
    g["ssm_glu_b"] = dgb.reshape(W)
    g["glu_w"] = _matmul(a1, dzl, "tn", name=f"glu_bwd_dw_{l}", shape=(W, W, L), tm=512, tn=512, tk=L, out_dtype=MXU)
    token = early(g)
    dua, dbtr, dbti, dcbr, dcbi, dlr, dli, dd = _s5_bwd(_time_interleave(dys), s["ua"], s["sr"], s["si"], sp["btr"],
                                                        sp["bti"], sp["cbr"], sp["cbi"], sp["lbr"], sp["lbi"],
                                                        sp["dvec"], name=f"s5_bwd_{l}", after=token)
    token = mid(dua)
    dua = _time_deinterleave(dua)
    g["ssm_d"] = dd.reshape(W)
    to_c = lambda t: _contract_bd(t).reshape(SSM_GROUPS, SSM_STATE, SSM_GROUP).transpose(0, 2, 1)
    g["ssm_c_re"], g["ssm_c_im"] = to_c(dcbr), to_c(dcbi)
    dare, daim, dldt, dbre, dbim = _s5_params_bwd(*sp["raw"], dlr.reshape(SSM_CH, 1), dli.reshape(SSM_CH, 1),
                                                  _contract_bd(dbtr), _contract_bd(dbti),
                                                  name=f"s5_params_bwd_{l}")
    g["ssm_a_re"] = dare.reshape(SSM_GROUPS, SSM_STATE)
    g["ssm_a_im"] = daim.reshape(SSM_GROUPS, SSM_STATE)
    g["ssm_log_dt"] = dldt.reshape(SSM_GROUPS)
    g["ssm_b_re"] = dbre.reshape(SSM_GROUPS, SSM_STATE, SSM_GROUP)
    g["ssm_b_im"] = dbim.reshape(SSM_GROUPS, SSM_STATE, SSM_GROUP)
    dproj = jnp.concatenate([t.astype(MXU) for t in (dua, dza, dub, dvb, dzb, _from_heads(dqh), _from_heads(dkh),
                                                     _from_heads(dvh), dzc, dgates)], axis=1)
    g["winT"] = _matmul(dproj, s["h"], "tn", name=f"proj_bwd_dw_{l}", shape=(D_IN, D, L), tm=256, tn=D, tk=L,
                        out_dtype=MXU, after=token)
    return dproj, g


def _proj_bwd_dh(dproj, winT, l, after):
    return _matmul(dproj, winT, "nn", name=f"proj_bwd_dh_{l}", shape=(dproj.shape[0], D_MODEL, D_IN), tm=1024, tn=512,
                   tk=D_IN // 2, after=after, vmem_mb=58)


MESH = pl.DeviceIdType.MESH
_ANY = pl.BlockSpec(memory_space=pl.ANY)
ROW_ALIGN = 16


def _coords():
    return lax.axis_index("x"), lax.axis_index("y"), lax.axis_index("c")


def _gather8(arrs, *, name):
    n = len(arrs)
    rows = [a.shape[0] for a in arrs]
    for r in rows:
        assert r % ROW_ALIGN == 0

    def body(*refs):
        ins, outs = refs[:n], refs[n:2 * n]
        send, recv, lsem = refs[2 * n:]
        x, y, c = _coords()
        me, sibling = (x, y, c), (x, y, 1 - c)
        chips = [(1 - x, y), (x, 1 - y), (1 - x, 1 - y)]

        def blk(a, px, py, pc):
            return outs[a].at[pl.ds(pl.multiple_of((4 * px + 2 * py + pc) * rows[a], ROW_ALIGN), rows[a]), :]

        def own(a):
            return ins[a]

        def copy(a, k, block, to, src=None):
            return pltpu.make_async_remote_copy(
                src_ref=blk(a, *block) if src is None else src, dst_ref=blk(a, *block),
                send_sem=send.at[a, k], recv_sem=recv.at[a, k], device_id=to, device_id_type=MESH)

        mine, first, passed = [], [], []
        for a in range(n):
            mine.append(pltpu.make_async_copy(own(a), blk(a, *me), lsem.at[a]))
            mine[a].start()
            f = [copy(a, 0, me, sibling, src=own(a))]
            f += [copy(a, 1 + j, me, (*chip, c), src=own(a)) for j, chip in enumerate(chips)]
            for cp in f:
                cp.start()
            first.append(f)
        for a in range(n):
            ps = [copy(a, 4 + j, (*chip, c), sibling) for j, chip in enumerate(chips)]
            for j, chip in enumerate(chips):
                copy(a, 1 + j, (*chip, c), me).wait_recv()
                ps[j].start()
            passed.append(ps)
        for a in range(n):
            copy(a, 0, sibling, me).wait_recv()
            for j, chip in enumerate(chips):
                copy(a, 4 + j, (*chip, 1 - c), me).wait_recv()
            for cp in first[a] + passed[a]:
                cp.wait_send()
            mine[a].wait()

    return pl.pallas_call(
        body, name=name,
        in_specs=[_ANY] * n, out_specs=[_ANY] * n,
        out_shape=[jax.ShapeDtypeStruct((8 * r,) + a.shape[1:], a.dtype) for r, a in zip(rows, arrs)],
        scratch_shapes=[pltpu.SemaphoreType.DMA((n, 7)), pltpu.SemaphoreType.DMA((n, 7)), pltpu.SemaphoreType.DMA((n,))],
    )(*arrs)


def _sibling_swap(arrs, *, name):
    n = len(arrs)

    def body(*refs):
        ins, outs = refs[:n], refs[n:2 * n]
        send, recv = refs[2 * n:]
        x, y, c = _coords()
        cps = [pltpu.make_async_remote_copy(src_ref=ins[a].at[:, 1 - c], dst_ref=outs[a], send_sem=send.at[a],
                                            recv_sem=recv.at[a], device_id=(x, y, 1 - c), device_id_type=MESH)
               for a in range(n)]
        for cp in cps:
            cp.start()
        for cp in cps:
            cp.wait_recv()
        for cp in cps:
            cp.wait_send()

    return pl.pallas_call(
        body, name=name, in_specs=[_ANY] * n, out_specs=[_ANY] * n,
        out_shape=[jax.ShapeDtypeStruct((a.shape[0],) + a.shape[2:], a.dtype) for a in arrs],
        scratch_shapes=[pltpu.SemaphoreType.DMA((n,)), pltpu.SemaphoreType.DMA((n,))],
    )(*arrs)


def _col_tile(lead, rows, cols, itemsize=4, cap=4 * _MB):
    tc = cols
    while tc % 256 == 0 and lead * rows * tc * itemsize > cap:
        tc //= 2
    return tc


def _pair_sum(mine, theirs, *, name):
    _, _, rows, cols = mine.shape
    tc = _col_tile(1, rows, cols)
    c = lax.axis_index("c")

    def body(c_ref, a_ref, b_ref, o_ref):
        o_ref[...] = (a_ref[...].astype(F32) + b_ref[...].astype(F32)).astype(MXU)

    return pl.pallas_call(
        body, name=name,
        grid_spec=pltpu.PrefetchScalarGridSpec(
            num_scalar_prefetch=1, grid=(4, cols // tc),
            in_specs=[pl.BlockSpec((None, None, rows, tc), lambda j, i, cr: (j, cr[0], 0, i)),
                      pl.BlockSpec((None, rows, tc), lambda j, i, cr: (j, 0, i))],
            out_specs=pl.BlockSpec((None, rows, tc), lambda j, i, cr: (j, 0, i))),
        out_shape=jax.ShapeDtypeStruct((4, rows, cols), MXU),
        compiler_params=_cp(("parallel", "parallel")),
    )(c.reshape(1).astype(jnp.int32), mine, theirs)


_HBM = pl.BlockSpec(memory_space=pltpu.HBM)
_SEM = pl.BlockSpec(memory_space=pltpu.SEMAPHORE)
_EFFECT = pltpu.SideEffectType.DATAFLOW_SIDE_EFFECTING
N_PEER_CHIPS = 3


def _peer_chips(x, y):
    return [(1 - x, y), (x, 1 - y), (1 - x, 1 - y)]


def _split_start(srcs, lands, src_slot, dst_slot, *, name, after=()):
    n = len(srcs)
    ns = n * N_PEER_CHIPS
    first = 2 * n + len(after)

    def body(*refs):
        src_refs, land_refs = refs[:n], refs[n:2 * n]
        send, recv, token = refs[first:first + ns], refs[first + ns:first + 2 * ns], refs[-1]
        x, y, c = _coords()
        for a in range(n):
            for k, (px, py) in enumerate(_peer_chips(x, y)):
                pltpu.make_async_remote_copy(
                    src_ref=src_refs[a].at[src_slot(x, y, c, px, py)], dst_ref=land_refs[a].at[dst_slot(x, y, c)],
                    send_sem=send[a * N_PEER_CHIPS + k], recv_sem=recv[a * N_PEER_CHIPS + k],
                    device_id=(px, py, c), device_id_type=MESH).start()
        token[...] = jnp.zeros_like(token)

    bufs = list(srcs) + list(lands)
    res = pl.pallas_call(
        body, name=name,
        out_shape=(*[pltpu.SemaphoreType.DMA(())] * (2 * ns), *[pltpu.HBM(b.shape, b.dtype) for b in bufs],
                   jax.ShapeDtypeStruct((8, 128), F32)),
        in_specs=[_HBM] * (2 * n) + [_ANY] * len(after),
        out_specs=(*[_SEM] * (2 * ns), *[_HBM] * (2 * n), pl.BlockSpec(memory_space=pltpu.VMEM)),
        input_output_aliases={i: 2 * ns + i for i in range(2 * n)},
        compiler_params=pltpu.CompilerParams(has_side_effects=_EFFECT),
    )(*[pltpu.with_memory_space_constraint(b, pltpu.HBM) for b in bufs], *after)
    sems = list(res[:2 * ns])
    return sems, list(res[2 * ns:2 * ns + n]), list(res[2 * ns + n:2 * ns + 2 * n]), res[-1]


def _split_wait(sems, srcs, lands, after, *, name):
    n = len(srcs)
    ns = n * N_PEER_CHIPS

    def body(*refs):
        src_refs, land_refs = refs[:n], refs[n:2 * n]
        send, recv = refs[2 * n:2 * n + ns], refs[2 * n + ns:2 * n + 2 * ns]
        x, y, c = _coords()
        for a in range(n):
            for k in range(N_PEER_CHIPS):
                cp = pltpu.make_async_remote_copy(
                    src_ref=src_refs[a].at[0], dst_ref=land_refs[a].at[0], send_sem=send[a * N_PEER_CHIPS + k],
                    recv_sem=recv[a * N_PEER_CHIPS + k], device_id=(x, y, 1 - c), device_id_type=MESH)
                cp.wait_send()
                cp.wait_recv()

    bufs = list(srcs) + list(lands)
    res = pl.pallas_call(
        body, name=name,
        out_shape=tuple(pltpu.HBM(b.shape, b.dtype) for b in bufs),
        in_specs=[_HBM] * (2 * n) + [_SEM] * (2 * ns) + [_ANY] * len(after),
        out_specs=tuple([_HBM] * (2 * n)),
        input_output_aliases={i: i for i in range(2 * n)},
        compiler_params=pltpu.CompilerParams(has_side_effects=_EFFECT),
    )(*bufs, *sems, *after)
    return list(res[:n]), list(res[n:])


def _gather_start(lands, *, name, after=()):
    n = len(lands)
    ns = n * N_PEER_CHIPS
    first = n + len(after)

    def body(*refs):
        land_refs = refs[:n]
        send, recv, token = refs[first:first + ns], refs[first + ns:first + 2 * ns], refs[-1]
        x, y, c = _coords()
        mine = 4 * x + 2 * y + c
        for a in range(n):
            for k, (px, py) in enumerate(_peer_chips(x, y)):
                pltpu.make_async_remote_copy(
                    src_ref=land_refs[a].at[mine], dst_ref=land_refs[a].at[mine], send_sem=send[a * N_PEER_CHIPS + k],
                    recv_sem=recv[a * N_PEER_CHIPS + k], device_id=(px, py, c), device_id_type=MESH).start()
        token[...] = jnp.zeros_like(token)

    res = pl.pallas_call(
        body, name=name,
        out_shape=(*[pltpu.SemaphoreType.DMA(())] * (2 * ns), *[pltpu.HBM(b.shape, b.dtype) for b in lands],
                   jax.ShapeDtypeStruct((8, 128), F32)),
        in_specs=[_HBM] * n + [_ANY] * len(after),
        out_specs=(*[_SEM] * (2 * ns), *[_HBM] * n, pl.BlockSpec(memory_space=pltpu.VMEM)),
        input_output_aliases={i: 2 * ns + i for i in range(n)},
        compiler_params=pltpu.CompilerParams(has_side_effects=_EFFECT),
    )(*[pltpu.with_memory_space_constraint(b, pltpu.HBM) for b in lands], *after)
    return list(res[:2 * ns]), list(res[2 * ns:2 * ns + n]), res[-1]


def _gather_wait(sems, lands, after, *, name):
    n = len(lands)
    ns = n * N_PEER_CHIPS

    def body(*refs):
        land_refs = refs[:n]
        send, recv = refs[n:n + ns], refs[n + ns:n + 2 * ns]
        x, y, c = _coords()
        for a in range(n):
            for k in range(N_PEER_CHIPS):
                cp = pltpu.make_async_remote_copy(
                    src_ref=land_refs[a].at[0], dst_ref=land_refs[a].at[0], send_sem=send[a * N_PEER_CHIPS + k],
                    recv_sem=recv[a * N_PEER_CHIPS + k], device_id=(x, y, 1 - c), device_id_type=MESH)
                cp.wait_send()
                cp.wait_recv()

    res = pl.pallas_call(
        body, name=name,
        out_shape=tuple(pltpu.HBM(b.shape, b.dtype) for b in lands),
        in_specs=[_HBM] * n + [_SEM] * (2 * ns) + [_ANY] * len(after),
        out_specs=tuple([_HBM] * n),
        input_output_aliases={i: i for i in range(n)},
        compiler_params=pltpu.CompilerParams(has_side_effects=_EFFECT),
    )(*lands, *sems, *after)
    return list(res)


def _fill_own(shards, l, *, name, after=None):
    _, rows2, cols = shards.shape
    rows = rows2 // 2
    tc = _col_tile(1, rows, cols, itemsize=shards.dtype.itemsize)
    j = 2 * lax.axis_index("x") + lax.axis_index("y")
    extra = [] if after is None else [after]

    def body(j_ref, s_ref, *rest):
        rest[-1][...] = s_ref[...].astype(MXU)

    return pl.pallas_call(
        body, name=name,
        grid_spec=pltpu.PrefetchScalarGridSpec(
            num_scalar_prefetch=1, grid=(2, cols // tc),
            in_specs=([pl.BlockSpec((None, rows, tc), lambda h, i, jr: (l, h, i))]
                      + [pl.BlockSpec(memory_space=pl.ANY)] * len(extra)),
            out_specs=pl.BlockSpec((None, rows, tc), lambda h, i, jr: (2 * jr[0] + h, 0, i))),
        out_shape=jax.ShapeDtypeStruct((8, rows, cols), MXU),
        compiler_params=_cp(("parallel", "parallel")),
    )(j.reshape(1).astype(jnp.int32), shards, *extra)


def _pass_to_sibling(lands, *, name):
    n = len(lands)

    def body(*refs):
        outs = refs[n:2 * n]
        send, recv = refs[2 * n:]
        x, y, c = _coords()
        cps = []
        for a in range(n):
            for k, (px, py) in enumerate(_peer_chips(x, y)):
                slot = 4 * px + 2 * py + c
                cps.append(pltpu.make_async_remote_copy(
                    src_ref=outs[a].at[slot], dst_ref=outs[a].at[slot], send_sem=send.at[a, k], recv_sem=recv.at[a, k],
                    device_id=(x, y, 1 - c), device_id_type=MESH))
        for cp in cps:
            cp.start()
        for cp in cps:
            cp.wait_recv()
        for cp in cps:
            cp.wait_send()

    return pl.pallas_call(
        body, name=name, in_specs=[_ANY] * n, out_specs=[_ANY] * n,
        out_shape=[jax.ShapeDtypeStruct(b.shape, b.dtype) for b in lands],
        input_output_aliases={a: a for a in range(n)},
        scratch_shapes=[pltpu.SemaphoreType.DMA((n, N_PEER_CHIPS)), pltpu.SemaphoreType.DMA((n, N_PEER_CHIPS))],
    )(*lands)


def _sum_parts(parts, got, *, name):
    _, rows, cols = parts.shape
    tc = _col_tile(4, rows, cols, itemsize=parts.dtype.itemsize)
    x, y, c = _coords()
    idx = jnp.stack([2 * x + y, 2 * (1 - x) + y, 2 * x + (1 - y), 2 * (1 - x) + (1 - y), c]).astype(jnp.int32)

    def body(i_ref, p_ref, g0_ref, g1_ref, g2_ref, o_ref):
        o_ref[...] = ((p_ref[...].astype(F32) + g0_ref[...].astype(F32)) + g1_ref[...].astype(F32)) + g2_ref[...].astype(F32)

    slot = lambda s: pl.BlockSpec((None, rows, tc), lambda i, ir, s=s: (ir[s], 0, i))
    return pl.pallas_call(
        body, name=name,
        grid_spec=pltpu.PrefetchScalarGridSpec(
            num_scalar_prefetch=1, grid=(cols // tc,),
            in_specs=[slot(0), slot(1), slot(2), slot(3)],
            out_specs=pl.BlockSpec((None, rows, tc), lambda i, ir: (ir[4], 0, i))),
        out_shape=jax.ShapeDtypeStruct((2, rows, cols), F32),
        compiler_params=_cp(("parallel",)),
    )(idx, parts, got, got, got)


def _sum_slots(t, *, name):
    S, rows, cols = t.shape
    tc = _col_tile(S, rows, cols)

    def body(t_ref, o_ref):
        acc = t_ref[0].astype(F32)
        for s in range(1, S):
            acc = acc + t_ref[s].astype(F32)
        o_ref[...] = acc

    return pl.pallas_call(
        body, name=name, grid=(cols // tc,),
        in_specs=[pl.BlockSpec((S, rows, tc), lambda i: (0, 0, i))],
        out_specs=pl.BlockSpec((rows, tc), lambda i: (0, i)),
        out_shape=jax.ShapeDtypeStruct((rows, cols), F32),
        compiler_params=_cp(("parallel",)),
    )(t)


def _halves_join(bufs, *, name):
    n = len(bufs)

    def body(*refs):
        outs = refs[n:2 * n]
        send, recv = refs[2 * n:]
        x, y, c = _coords()
        cps = [pltpu.make_async_remote_copy(src_ref=outs[a].at[c], dst_ref=outs[a].at[c], send_sem=send.at[a],
                                            recv_sem=recv.at[a], device_id=(x, y, 1 - c), device_id_type=MESH)
               for a in range(n)]
        for cp in cps:
            cp.start()
        for cp in cps:
            cp.wait_recv()
        for cp in cps:
            cp.wait_send()

    return pl.pallas_call(
        body, name=name, in_specs=[_ANY] * n, out_specs=[_ANY] * n,
        out_shape=[jax.ShapeDtypeStruct(b.shape, b.dtype) for b in bufs],
        input_output_aliases={a: a for a in range(n)},
        scratch_shapes=[pltpu.SemaphoreType.DMA((n,)), pltpu.SemaphoreType.DMA((n,))],
    )(*bufs)


def _swap_start(srcs, *, name):
    n = len(srcs)
    lands = [lax.empty((s.shape[0],) + s.shape[2:], s.dtype) for s in srcs]

    def body(*refs):
        src_refs, land_refs = refs[:n], refs[n:2 * n]
        send, recv, token = refs[2 * n:3 * n], refs[3 * n:4 * n], refs[-1]
        x, y, c = _coords()
        for a in range(n):
            pltpu.make_async_remote_copy(src_ref=src_refs[a].at[:, 1 - c], dst_ref=land_refs[a], send_sem=send[a],
                                         recv_sem=recv[a], device_id=(x, y, 1 - c), device_id_type=MESH).start()
        token[...] = jnp.zeros_like(token)

    bufs = list(srcs) + lands
    res = pl.pallas_call(
        body, name=name,
        out_shape=(*[pltpu.SemaphoreType.DMA(())] * (2 * n), *[pltpu.HBM(b.shape, b.dtype) for b in bufs],
                   jax.ShapeDtypeStruct((8, 128), F32)),
        in_specs=[_HBM] * (2 * n),
        out_specs=(*[_SEM] * (2 * n), *[_HBM] * (2 * n), pl.BlockSpec(memory_space=pltpu.VMEM)),
        input_output_aliases={i: 2 * n + i for i in range(2 * n)},
        compiler_params=pltpu.CompilerParams(has_side_effects=_EFFECT),
    )(*[pltpu.with_memory_space_constraint(b, pltpu.HBM) for b in bufs])
    return list(res[:2 * n]), list(res[2 * n:3 * n]), list(res[3 * n:4 * n]), res[-1]


def _swap_wait(sems, srcs, lands, after, *, name):
    n = len(srcs)

    def body(*refs):
        src_refs, land_refs = refs[:n], refs[n:2 * n]
        send, recv = refs[2 * n:3 * n], refs[3 * n:4 * n]
        x, y, c = _coords()
        for a in range(n):
            cp = pltpu.make_async_remote_copy(
                src_ref=src_refs[a].at[:, 0], dst_ref=land_refs[a], send_sem=send[a], recv_sem=recv[a],
                device_id=(x, y, 1 - c), device_id_type=MESH)
            cp.wait_send()
            cp.wait_recv()

    bufs = list(srcs) + list(lands)
    res = pl.pallas_call(
        body, name=name,
        out_shape=tuple(pltpu.HBM(b.shape, b.dtype) for b in bufs),
        in_specs=[_HBM] * (2 * n) + [_SEM] * (2 * n) + [_ANY] * len(after),
        out_specs=tuple([_HBM] * (2 * n)),
        input_output_aliases={i: i for i in range(2 * n)},
        compiler_params=pltpu.CompilerParams(has_side_effects=_EFFECT),
    )(*bufs, *sems, *after)
    return list(res[:n]), list(res[n:])


def _grad_views(grads):
    return [g.reshape(4, 2, g.shape[0] // 8, g.shape[1]) for g in grads]


def _scatter_begin(views, theirs, *, tag):
    parts = [_pair_sum(v, t, name=f"rs_pair_{tag}_{i}") for i, (v, t) in enumerate(zip(views, theirs))]
    got = [lax.empty(p.shape, p.dtype) for p in parts]
    sems, parts, got, token = _split_start(
        parts, got, lambda x, y, c, px, py: 2 * px + py, lambda x, y, c: 2 * x + y, name=f"rs_start_{tag}")
    return (sems, parts, got), token


def _reduce_scatter_begin(grads, *, tag):
    views = _grad_views(grads)
    theirs = _sibling_swap(views, name=f"rs_swap_{tag}")
    return _scatter_begin(views, theirs, tag=tag)


def _reduce_scatter_end(state, after, *, tag):
    sems, parts, got = state
    parts, got = _split_wait(sems, parts, got, after, name=f"rs_wait_{tag}")
    halves = [_sum_parts(p, t, name=f"rs_sum_{tag}_{i}") for i, (p, t) in enumerate(zip(parts, got))]
    joined = _halves_join(halves, name=f"rs_join_{tag}")
    return [j.reshape(2 * j.shape[1], j.shape[2]) for j in joined]


_SMALL = ("norm_w", "ssm_a_re", "ssm_a_im", "ssm_log_dt", "ssm_b_re", "ssm_b_im", "ssm_c_re", "ssm_c_im", "ssm_d",
          "ssm_glu_b", "sg_ln_w", "sg_ln_b", "sg_w", "sg_b", "attn_sinks", "final_norm_w")
_BIG = ("w_in", "ssm_glu_w", "w_branch_a", "w_branch_b", "w_branch_c", "w_out")
_WEIGHTS = ("norm_w", "w_in", "ssm_a_re", "ssm_a_im", "ssm_log_dt", "ssm_b_re", "ssm_b_im", "ssm_c_re", "ssm_c_im",
            "ssm_d", "ssm_glu_w", "ssm_glu_b", "sg_ln_w", "sg_ln_b", "sg_w", "sg_b", "attn_sinks", "w_branch_a",
            "w_branch_b", "w_branch_c", "w_out", "final_norm_w")
_PACK_COLS = 1024
_PACK_ALIGN = 8 * ROW_ALIGN * _PACK_COLS


N_PEERS = 7


def _peer(x, y, c, k):
    return x ^ (k >> 2), y ^ ((k >> 1) & 1), c ^ (k & 1)


def _own_slice(buf, *, name):
    _, rows, cols = buf.shape
    x, y, c = _coords()

    def body(me_ref, s_ref, o_ref):
        o_ref[...] = s_ref[...]

    spec = pl.BlockSpec((None, rows, cols), lambda i, mr: (mr[0], 0, 0))
    return pl.pallas_call(
        body, name=name,
        grid_spec=pltpu.PrefetchScalarGridSpec(num_scalar_prefetch=1, grid=(1,), in_specs=[spec], out_specs=spec),
        out_shape=jax.ShapeDtypeStruct(buf.shape, buf.dtype),
        compiler_params=_cp(("arbitrary",)),
    )((4 * x + 2 * y + c).reshape(1).astype(jnp.int32), buf)


def _slice_exchange_start(buf, land, *, name):
    def body(in_ref, land_ref, *rest):
        send, recv, token = rest[:N_PEERS], rest[N_PEERS:2 * N_PEERS], rest[-1]
        x, y, c = _coords()
        me = 4 * x + 2 * y + c
        for k in range(1, N_PEERS + 1):
            px, py, pc = _peer(x, y, c, k)
            pltpu.make_async_remote_copy(
                src_ref=in_ref.at[4 * px + 2 * py + pc], dst_ref=land_ref.at[me], send_sem=send[k - 1],
                recv_sem=recv[k - 1], device_id=(px, py, pc), device_id_type=MESH).start()
        token[...] = jnp.zeros_like(token)

    res = pl.pallas_call(
        body, name=name,
        out_shape=(*[pltpu.SemaphoreType.DMA(())] * (2 * N_PEERS), pltpu.HBM(buf.shape, buf.dtype),
                   pltpu.HBM(land.shape, land.dtype), jax.ShapeDtypeStruct((8, 128), F32)),
        in_specs=[_HBM, _HBM],
        out_specs=(*[_SEM] * (2 * N_PEERS), _HBM, _HBM, pl.BlockSpec(memory_space=pltpu.VMEM)),
        input_output_aliases={0: 2 * N_PEERS, 1: 2 * N_PEERS + 1},
        compiler_params=pltpu.CompilerParams(has_side_effects=_EFFECT),
    )(pltpu.with_memory_space_constraint(buf, pltpu.HBM), pltpu.with_memory_space_constraint(land, pltpu.HBM))
    return list(res[:2 * N_PEERS]), res[2 * N_PEERS], res[2 * N_PEERS + 1], res[-1]


def _slice_exchange_wait(sems, buf, land, after, *, name):
    def body(in_ref, land_ref, *rest):
        send, recv = rest[:N_PEERS], rest[N_PEERS:2 * N_PEERS]
        x, y, c = _coords()
        for k in range(N_PEERS):
            cp = pltpu.make_async_remote_copy(
                src_ref=in_ref.at[0], dst_ref=land_ref.at[0], send_sem=send[k], recv_sem=recv[k],
                device_id=(x, y, 1 - c), device_id_type=MESH)
            cp.wait_send()
            cp.wait_recv()

    res = pl.pallas_call(
        body, name=name,
        out_shape=(pltpu.HBM(buf.shape, buf.dtype), pltpu.HBM(land.shape, land.dtype)),
        in_specs=[_HBM, _HBM] + [_SEM] * (2 * N_PEERS) + [_ANY] * len(after),
        out_specs=(_HBM, _HBM),
        input_output_aliases={0: 0, 1: 1},
        compiler_params=pltpu.CompilerParams(has_side_effects=_EFFECT),
    )(buf, land, *sems, *after)
    return res[1]


def _allreduce_small_begin(packed):
    rows, cols = packed.shape
    buf = packed.reshape(8, rows // 8, cols)
    sems, buf, land, token = _slice_exchange_start(buf, _own_slice(buf, name="small_grads_own"),
                                                   name="small_grads_start")
    return (sems, buf, land), token


def _allreduce_small_end(state, after):
    got = _slice_exchange_wait(*state, list(after), name="small_grads_wait")
    mine = _sum_slots(got, name="small_grads_sum")
    return _gather8([mine], name="small_grads_gather")[0]


def _pack(ts):
    flat = jnp.concatenate([t.reshape(-1) for t in ts])
    pad = (-flat.shape[0]) % _PACK_ALIGN
    return jnp.pad(flat, (0, pad)).reshape(-1, _PACK_COLS)


def _unpack(buf, like):
    flat = buf.reshape(-1)
    out, pos = [], 0
    for t in like:
        out.append(flat[pos:pos + t.size].reshape(t.shape))
        pos += t.size
    return out


def kernel(x, norm_w, w_in, ssm_a_re, ssm_a_im, ssm_log_dt, ssm_b_re, ssm_b_im, ssm_c_re, ssm_c_im, ssm_d, ssm_glu_w, ssm_glu_b, sg_ln_w, sg_ln_b, sg_w, sg_b, attn_sinks, w_branch_a, w_branch_b, w_branch_c, w_out, final_norm_w, loss_target, m_norm_w, m_w_in, m_ssm_a_re, m_ssm_a_im, m_ssm_log_dt, m_ssm_b_re, m_ssm_b_im, m_ssm_c_re, m_ssm_c_im, m_ssm_d, m_ssm_glu_w, m_ssm_glu_b, m_sg_ln_w, m_sg_ln_b, m_sg_w, m_sg_b, m_attn_sinks, m_w_branch_a, m_w_branch_b, m_w_branch_c, m_w_out, m_final_norm_w, v_norm_w, v_w_in, v_ssm_a_re, v_ssm_a_im, v_ssm_log_dt, v_ssm_b_re, v_ssm_b_im, v_ssm_c_re, v_ssm_c_im, v_ssm_d, v_ssm_glu_w, v_ssm_glu_b, v_sg_ln_w, v_sg_ln_b, v_sg_w, v_sg_b, v_attn_sinks, v_w_branch_a, v_w_branch_b, v_w_branch_c, v_w_out, v_final_norm_w):
    w = dict(norm_w=norm_w, w_in=w_in, ssm_a_re=ssm_a_re, ssm_a_im=ssm_a_im, ssm_log_dt=ssm_log_dt, ssm_b_re=ssm_b_re,
             ssm_b_im=ssm_b_im, ssm_c_re=ssm_c_re, ssm_c_im=ssm_c_im, ssm_d=ssm_d, ssm_glu_w=ssm_glu_w,
             ssm_glu_b=ssm_glu_b, sg_ln_w=sg_ln_w, sg_ln_b=sg_ln_b, sg_w=sg_w, sg_b=sg_b, attn_sinks=attn_sinks,
             w_branch_a=w_branch_a, w_branch_b=w_branch_b, w_branch_c=w_branch_c, w_out=w_out,
             final_norm_w=final_norm_w)
    m = dict(norm_w=m_norm_w, w_in=m_w_in, ssm_a_re=m_ssm_a_re, ssm_a_im=m_ssm_a_im, ssm_log_dt=m_ssm_log_dt,
             ssm_b_re=m_ssm_b_re, ssm_b_im=m_ssm_b_im, ssm_c_re=m_ssm_c_re, ssm_c_im=m_ssm_c_im, ssm_d=m_ssm_d,
             ssm_glu_w=m_ssm_glu_w, ssm_glu_b=m_ssm_glu_b, sg_ln_w=m_sg_ln_w, sg_ln_b=m_sg_ln_b, sg_w=m_sg_w,
             sg_b=m_sg_b, attn_sinks=m_attn_sinks, w_branch_a=m_w_branch_a, w_branch_b=m_w_branch_b,
             w_branch_c=m_w_branch_c, w_out=m_w_out, final_norm_w=m_final_norm_w)
    v = dict(norm_w=v_norm_w, w_in=v_w_in, ssm_a_re=v_ssm_a_re, ssm_a_im=v_ssm_a_im, ssm_log_dt=v_ssm_log_dt,
             ssm_b_re=v_ssm_b_re, ssm_b_im=v_ssm_b_im, ssm_c_re=v_ssm_c_re, ssm_c_im=v_ssm_c_im, ssm_d=v_ssm_d,
             ssm_glu_w=v_ssm_glu_w, ssm_glu_b=v_ssm_glu_b, sg_ln_w=v_sg_ln_w, sg_ln_b=v_sg_ln_b, sg_w=v_sg_w,
             sg_b=v_sg_b, attn_sinks=v_attn_sinks, w_branch_a=v_w_branch_a, w_branch_b=v_w_branch_b,
             w_branch_c=v_w_branch_c, w_out=v_w_out, final_norm_w=v_final_norm_w)

    big_names = ("winT", "glu_w", "wbaT", "wbbT", "wbcT", "w_out")
    L = x.shape[1]
    tabs = _rope_tables(L)
    p = {k: w[k] for k in _SMALL}

    column_sharded = ("w_in", "w_branch_a", "w_branch_b", "w_branch_c")

    row_shards = {k: w[k].transpose(0, 2, 1) if k in column_sharded else w[k] for k in _BIG}
    rows_of = lambda lands: [t.reshape(8 * t.shape[1], t.shape[2]) for t in lands]
    saved = [None] * DEPTH

    land_a = [_fill_own(row_shards["w_in"], 0, name="gather_fill_0_0")]
    sems_a, land_a, token_a = _gather_start(land_a, name="gather_start_0a")
    fill_after = lambda l, names, i0: [_fill_own(row_shards[k], l, name=f"gather_fill_{l}_{i0 + i}", after=token_a)
                                       for i, k in enumerate(names)]
    lands = [[None] + fill_after(0, _BIG[1:], 1), fill_after(1, _BIG, 0)]
    sp = [_prep_layer(p, l, after=token_a) for l in range(DEPTH)]
    h0 = _rms_fwd(x[0], p["norm_w"][0], name="rms_fwd_0", after=token_a)
    land_a = _gather_wait(sems_a, land_a, [h0, sp[0]["btr"], sp[1]["btr"]] + lands[0][1:] + lands[1],
                          name="gather_wait_0a")
    land_a = _pass_to_sibling(land_a, name="gather_pass_0a")
    sems_b, land_b, token_b = _gather_start(lands[0][1:], name="gather_start_0b", after=land_a)
    split1 = {}

    def start1(main):
        split1["sems"], split1["land"], token1 = _gather_start(lands[1][:1], name="gather_start_1a", after=[main])
        return token1

    def rest(l, sems, land, first):
        def arrived(t):
            got = _pass_to_sibling(_gather_wait(sems, land, t, name=f"gather_wait_{l}b"), name=f"gather_pass_{l}b")
            return dict(zip(big_names, rows_of(first + got))), None
        return arrived

    x1, saved[0], big0 = _layer_fwd(x[0], h0, p, sp[0], rows_of(land_a)[0], rest(0, sems_b, land_b, land_a), 0,
                                    tabs, proj_after=token_b, after_main=start1)
    land_1a = _gather_wait(split1["sems"], split1["land"], [x1], name="gather_wait_1a")
    land_1a = _pass_to_sibling(land_1a, name="gather_pass_1a")
    sems_1b, land_1b, token_1b = _gather_start(lands[1][1:], name="gather_start_1b", after=land_1a)
    h1 = _rms_fwd(x1, p["norm_w"][1], name="rms_fwd_1")
    x2, saved[1], big1 = _layer_fwd(x1, h1, p, sp[1], rows_of(land_1a)[0], rest(1, sems_1b, land_1b, land_1a),
                                    1, tabs, proj_after=token_1b)
    bigs = [big0, big1]
    loss, dx, dfw = _final_loss(x2, p["final_norm_w"], loss_target[0], name="final_loss")

    grads = [None] * DEPTH
    rs = {}

    def early(l):
        def begin(g):
            *rs[f"{l}a"], token_a = _swap_start(_grad_views([g[k] for k in big_names[1:]]), name=f"rs_swap_start_{l}a")
            return token_a
        return begin

    def mid(l):
        def go_on(t):
            sems, views, lands = rs[f"{l}a"]
            views, theirs = _swap_wait(sems, views, lands, [t], name=f"rs_swap_wait_{l}a")
            rs[f"{l}a"], token_a = _scatter_begin(views, theirs, tag=f"{l}a")
            return token_a
        return go_on

    def late(l, dproj, dx):
        if l == 0:
            rs["0b"], token_s = _reduce_scatter_begin([grads[0]["winT"]], tag="0b")
            dh = _proj_bwd_dh(dproj, bigs[0]["winT"], 0, token_s)
            return _rms_bwd(saved[0]["x"], p["norm_w"][0], dh, dx, name="rms_bwd_0")
        sems, views, lands, token_b = _swap_start(_grad_views([grads[l]["winT"]]), name=f"rs_swap_start_{l}b")
        dh = _proj_bwd_dh(dproj, bigs[l]["winT"], l, token_b)
        views, theirs = _swap_wait(sems, views, lands, [dh], name=f"rs_swap_wait_{l}b")
        rs[f"{l}b"], token_s = _scatter_begin(views, theirs, tag=f"{l}b")
        return _rms_bwd(saved[l]["x"], p["norm_w"][l], dh, dx, name=f"rms_bwd_{l}", after=token_s)

    def reduced(l, after):
        return _reduce_scatter_end(rs[f"{l}b"], after, tag=f"{l}b") + _reduce_scatter_end(rs[f"{l}a"], after, tag=f"{l}a")

    dproj, grads[1] = _layer_bwd(dx, saved[1], p, bigs[1], 1, tabs, early(1), mid(1))
    dx, grads[1]["norm_w"] = late(1, dproj, dx)
    dproj, grads[0] = _layer_bwd(dx, saved[0], p, bigs[0], 0, tabs, early(0), mid(0))
    dx, grads[0]["norm_w"] = late(0, dproj, dx)
    small_like = [w[k] for k in _SMALL]
    gs = [jnp.stack([grads[l][k] for l in range(DEPTH)]) if k != "final_norm_w" else dfw for k in _SMALL]
    small, token_small = _allreduce_small_begin(_pack(gs + [loss.reshape(1)]))
    red1 = reduced(1, [dx, token_small])

    tr = lambda t: t.transpose(0, 2, 1)
    view = {k: (tr if k == "w_in" else (lambda t: t)) for k in _BIG}
    shard_grads = lambda red: dict(zip(_BIG, (red[0], red[1], red[2].T, red[3].T, red[4].T, red[5])))
    outs = {k: None for k in _BIG}

    def adamw_big(l, red):
        for k, g in shard_grads(red).items():
            outs[k] = _adamw_layer(view[k](w[k]), g, view[k](m[k]), view[k](v[k]), l, outs[k], name=f"adamw_{k}_{l}")

    adamw_big(1, red1)

    gsum = _allreduce_small_end(small, [outs[k][0] for k in _BIG])
    adamw_big(0, reduced(0, [gsum]))

    gfull, delta, new_m, new_v = {}, {}, {}, {}
    for k in _BIG:
        gfull[k], delta[k], new_m[k], new_v[k] = (view[k](t) for t in outs[k])
    *small_sums, loss = _unpack(gsum, small_like + [loss])
    for k, t in zip(_SMALL, small_sums):
        gfull[k] = t
        delta[k], new_m[k], new_v[k] = _adamw(w[k], t, m[k], v[k], name=f"adamw_{k}")

    return (loss, dx[None], *[gfull[k] for k in _WEIGHTS], *[delta[k] for k in _WEIGHTS],
            *[new_m[k] for k in _WEIGHTS], *[new_v[k] for k in _WEIGHTS])
```

```python
import functools
import math

import numpy as np
import jax
import jax.numpy as jnp
from jax import lax
from jax.experimental import pallas as pl
from jax.experimental.pallas import tpu as pltpu

F32 = jnp.float32
MXU = jnp.bfloat16
HIGHEST = lax.Precision.HIGHEST

D_MODEL = 2048
DEPTH = 2
EPS = 1e-6
NEG_INF = -1e30
SSM_WIDTH = 1024
SSM_GROUP = 16
SSM_GROUPS = 64
SSM_STATE = 64
SSM_CH = SSM_GROUPS * SSM_STATE
SLAB = 128
SLAB_CH = (SLAB // SSM_GROUP) * SSM_STATE
N_SLAB = SSM_WIDTH // SLAB
SCAN_SEG = 8
SCAN_STEPS = 4
SG_HEADS = 8
SG_CHUNK = 128
HEAD_DIM = 64
ATT_HEADS = 16
ATT_KV_HEADS = 2
GQA_GROUP = 8
ATT_BLOCK = 128
WINDOW = 128
ROT_DIM = 16
ROPE_THETA = 500000.0
N_MAIN = 6400
N_ZC = 1024
N_GATES = 6144
D_IN = N_MAIN + N_ZC + N_GATES

ADAM_LR = 0.001
ADAM_B1 = 0.9
ADAM_B2 = 0.999
ADAM_EPS = 1e-08
ADAM_WD = 0.01
ADAM_STEP = 10

_DIMS = {"nn": (((1,), (0,)), ((), ())), "nt": (((1,), (1,)), ((), ())), "tn": (((0,), (0,)), ((), ()))}
_MB = 1024 * 1024


def _cp(sem, vmem_mb=48):
    return pltpu.CompilerParams(dimension_semantics=sem, vmem_limit_bytes=vmem_mb * _MB)


def _dot(a, b, mode):
    return lax.dot_general(a.astype(MXU), b.astype(MXU), _DIMS[mode], preferred_element_type=F32)


@jax.custom_vjp
def _mm_nn(a, b):
    return _dot(a, b, "nn")


def _mm_nn_fwd(a, b):
    return _dot(a, b, "nn"), (a, b)


def _mm_nn_bwd(res, g):
    a, b = res
    return _dot(g, b, "nt"), _dot(a, g, "tn")


_mm_nn.defvjp(_mm_nn_fwd, _mm_nn_bwd)


@jax.custom_vjp
def _mm_nt(a, bt):
    return _dot(a, bt, "nt")


def _mm_nt_fwd(a, bt):
    return _dot(a, bt, "nt"), (a, bt)


def _mm_nt_bwd(res, g):
    a, bt = res
    return _dot(g, bt, "nn"), _dot(g, a, "tn")


_mm_nt.defvjp(_mm_nt_fwd, _mm_nt_bwd)


def _rmsnorm(x, w):
    return x * lax.rsqrt(jnp.mean(x * x, axis=-1, keepdims=True) + EPS) * w


def _layernorm(x, w, b):
    mu = jnp.mean(x, axis=-1, keepdims=True)
    var = jnp.mean(jnp.square(x - mu), axis=-1, keepdims=True)
    return (x - mu) * lax.rsqrt(var + EPS) * w + b


def _silu(x):
    return x * jax.nn.sigmoid(x)


def _matmul(a, b, mode, *, name, shape, tm, tn, tk, out_dtype=F32, add=None, a_off=(0, 0), b_off=(0, 0), after=None,
            vmem_mb=48):
    m, n, k = shape
    tm, tn, tk = min(tm, m), min(tn, n), min(tk, k)
    assert m % tm == 0 and n % tn == 0 and k % tk == 0, (name, shape, tm, tn, tk)
    nk = k // tk
    has_add, has_after = add is not None, after is not None

    def body(*refs):
        a_ref, b_ref = refs[0], refs[1]
        pos = 2
        add_ref = None
        if has_add:
            add_ref = refs[pos]
            pos += 1
        if has_after:
            pos += 1
        o_ref = refs[pos]
        p = _dot(a_ref[...], b_ref[...], mode)
        if nk == 1:
            if has_add:
                p = p + add_ref[...].astype(F32)
            o_ref[...] = p.astype(out_dtype)
            return
        acc_ref = refs[pos + 1]
        kk = pl.program_id(2)

        @pl.when(kk == 0)
        def _():
            acc_ref[...] = p

        @pl.when(kk > 0)
        def _():
            acc_ref[...] += p

        @pl.when(kk == nk - 1)
        def _():
            r = acc_ref[...]
            if has_add:
                r = r + add_ref[...].astype(F32)
            o_ref[...] = r.astype(out_dtype)

    a0, a1 = a_off
    b0, b1 = b_off
    if mode == "tn":
        a_spec = pl.BlockSpec((tk, tm), lambda i, j, kk: (kk + a0, i + a1))
    else:
        a_spec = pl.BlockSpec((tm, tk), lambda i, j, kk: (i + a0, kk + a1))
    if mode == "nt":
        b_spec = pl.BlockSpec((tn, tk), lambda i, j, kk: (j + b0, kk + b1))
    else:
        b_spec = pl.BlockSpec((tk, tn), lambda i, j, kk: (kk + b0, j + b1))
    in_specs = [a_spec, b_spec]
    args = [a, b]
    if has_add:
        in_specs.append(pl.BlockSpec((tm, tn), lambda i, j, kk: (i, j)))
        args.append(add)
    if has_after:
        in_specs.append(pl.BlockSpec(memory_space=pl.ANY))
        args.append(after)
    return pl.pallas_call(
        body, name=name, grid=(m // tm, n // tn, nk),
        in_specs=in_specs,
        out_specs=pl.BlockSpec((tm, tn), lambda i, j, kk: (i, j)),
        out_shape=jax.ShapeDtypeStruct((m, n), out_dtype),
        scratch_shapes=[pltpu.VMEM((tm, tn), F32)] if nk > 1 else [],
        compiler_params=_cp(("parallel", "parallel", "arbitrary"), vmem_mb),
    )(*args)


def _rms_fwd(x, w, *, name, tm=256, after=None):
    L, d = x.shape
    tm = min(tm, L)
    extra = [] if after is None else [after]

    def body(x_ref, w_ref, *rest):
        rest[-1][...] = _rmsnorm(x_ref[...], w_ref[...]).astype(MXU)

    return pl.pallas_call(
        body, name=name, grid=(L // tm,),
        in_specs=([pl.BlockSpec((tm, d), lambda i: (i, 0)), pl.BlockSpec((1, d), lambda i: (0, 0))]
                  + [pl.BlockSpec(memory_space=pl.ANY)] * len(extra)),
        out_specs=pl.BlockSpec((tm, d), lambda i: (i, 0)),
        out_shape=jax.ShapeDtypeStruct((L, d), MXU),
        compiler_params=_cp(("parallel",)),
    )(x, w.reshape(1, d), *extra)


def _rms_bwd(x, w, dh, dxn, *, name, tm=256, after=None):
    L, d = x.shape
    tm = min(tm, L)
    extra = [] if after is None else [after]

    def body(x_ref, w_ref, dh_ref, dxn_ref, *rest):
        dx_ref, dw_ref = rest[-2:]
        _, vjp = jax.vjp(_rmsnorm, x_ref[...], w_ref[...])
        dx, dw = vjp(dh_ref[...])
        dx_ref[...] = dx + dxn_ref[...]

        @pl.when(pl.program_id(0) == 0)
        def _():
            dw_ref[...] = jnp.zeros_like(dw_ref)

        dw_ref[...] += dw

    row = pl.BlockSpec((tm, d), lambda i: (i, 0))
    vec = pl.BlockSpec((1, d), lambda i: (0, 0))
    dx, dw = pl.pallas_call(
        body, name=name, grid=(L // tm,),
        in_specs=[row, vec, row, row] + [pl.BlockSpec(memory_space=pl.ANY)] * len(extra), out_specs=[row, vec],
        out_shape=[jax.ShapeDtypeStruct((L, d), F32), jax.ShapeDtypeStruct((1, d), F32)],
        compiler_params=_cp(("arbitrary",)),
    )(x, w.reshape(1, d), dh, dxn, *extra)
    return dx, dw.reshape(d)


def _final_loss(x, w, tgt, *, name, tm=256):
    L, d = x.shape
    tm = min(tm, L)

    def loss_fn(xv, wv, tv):
        err = jnp.square(_rmsnorm(xv, wv) - tv)
        return 0.5 * jnp.sum(jnp.mean(err, axis=-1, keepdims=True), axis=0, keepdims=True)

    def body(x_ref, w_ref, t_ref, loss_ref, dx_ref, dw_ref):
        tv = t_ref[...]
        val, vjp = jax.vjp(lambda xv, wv: loss_fn(xv, wv, tv), x_ref[...], w_ref[...])
        dx, dw = vjp(jnp.ones((1, 1), F32))
        dx_ref[...] = dx

        @pl.when(pl.program_id(0) == 0)
        def _():
            dw_ref[...] = jnp.zeros_like(dw_ref)
            loss_ref[...] = jnp.zeros_like(loss_ref)

        dw_ref[...] += dw
        loss_ref[...] += jnp.broadcast_to(val, loss_ref.shape)

    row = pl.BlockSpec((tm, d), lambda i: (i, 0))
    vec = pl.BlockSpec((1, d), lambda i: (0, 0))
    loss, dx, dw = pl.pallas_call(
        body, name=name, grid=(L // tm,),
        in_specs=[row, vec, row],
        out_specs=[pl.BlockSpec((8, 128), lambda i: (0, 0)), row, vec],
        out_shape=[jax.ShapeDtypeStruct((8, 128), F32), jax.ShapeDtypeStruct((L, d), F32),
                   jax.ShapeDtypeStruct((1, d), F32)],
        compiler_params=_cp(("arbitrary",)),
    )(x, w.reshape(1, d), tgt)
    return loss[0, 0], dx, dw.reshape(d)


PARAM_ROWS = 512


def _s5_param_fn(are, aim, ldt, bre, bim, row0):
    n = are.shape[0]
    grp = (row0 + lax.broadcasted_iota(jnp.int32, (n, SSM_GROUPS), 0)) // SSM_STATE
    col = lax.broadcasted_iota(jnp.int32, (n, SSM_GROUPS), 1)
    sel = (grp == col).astype(F32)
    dt = jnp.sum(sel * jnp.exp(ldt), axis=-1, keepdims=True)
    mag = jnp.exp(are * dt)
    ang = aim * dt
    lbr = mag * jnp.cos(ang)
    lbi = mag * jnp.sin(ang)
    den = are * are + aim * aim
    nr = lbr - 1.0
    kr = (nr * are + lbi * aim) / den
    ki = (lbi * are - nr * aim) / den
    return lbr, lbi, kr * bre - ki * bim, kr * bim + ki * bre


def _s5_param_specs():
    col = pl.BlockSpec((PARAM_ROWS, 1), lambda i: (i, 0))
    mat = pl.BlockSpec((PARAM_ROWS, SSM_GROUP), lambda i: (i, 0))
    vec = pl.BlockSpec((1, SSM_GROUPS), lambda i: (0, 0))
    return col, mat, vec


def _s5_params_fwd(are, aim, ldt, bre, bim, *, name, after=None):
    n = are.shape[0]
    col, mat, vec = _s5_param_specs()
    extra = [] if after is None else [after]

    def body(are_ref, aim_ref, ldt_ref, bre_ref, bim_ref, *rest):
        lbr_ref, lbi_ref, bbr_ref, bbi_ref = rest[-4:]
        row0 = pl.program_id(0) * PARAM_ROWS
        lbr, lbi, bbr, bbi = _s5_param_fn(are_ref[...], aim_ref[...], ldt_ref[...], bre_ref[...], bim_ref[...], row0)
        lbr_ref[...] = lbr
        lbi_ref[...] = lbi
        bbr_ref[...] = bbr
        bbi_ref[...] = bbi

    cshape = jax.ShapeDtypeStruct((n, 1), F32)
    mshape = jax.ShapeDtypeStruct((n, SSM_GROUP), F32)
    return pl.pallas_call(body, name=name, grid=(n // PARAM_ROWS,),
                          in_specs=[col, col, vec, mat, mat] + [pl.BlockSpec(memory_space=pl.ANY)] * len(extra),
                          out_specs=[col, col, mat, mat], out_shape=[cshape, cshape, mshape, mshape],
                          compiler_params=_cp(("parallel",)))(are, aim, ldt, bre, bim, *extra)


def _s5_params_bwd(are, aim, ldt, bre, bim, dlbr, dlbi, dbbr, dbbi, *, name):
    n = are.shape[0]
    col, mat, vec = _s5_param_specs()

    def body(are_ref, aim_ref, ldt_ref, bre_ref, bim_ref, g0, g1, g2, g3, o0, o1, o2, o3, o4):
        row0 = pl.program_id(0) * PARAM_ROWS
        _, vjp = jax.vjp(lambda a, b, c, d, e: _s5_param_fn(a, b, c, d, e, row0),
                         are_ref[...], aim_ref[...], ldt_ref[...], bre_ref[...], bim_ref[...])
        dare, daim, dldt, dbre, dbim = vjp((g0[...], g1[...], g2[...], g3[...]))
        o0[...] = dare
        o1[...] = daim
        o3[...] = dbre
        o4[...] = dbim

        @pl.when(pl.program_id(0) == 0)
        def _():
            o2[...] = jnp.zeros_like(o2)

        o2[...] += dldt

    cshape = jax.ShapeDtypeStruct((n, 1), F32)
    mshape = jax.ShapeDtypeStruct((n, SSM_GROUP), F32)
    return pl.pallas_call(body, name=name, grid=(n // PARAM_ROWS,),
                          in_specs=[col, col, vec, mat, mat, col, col, mat, mat],
                          out_specs=[col, col, vec, mat, mat],
                          out_shape=[cshape, cshape, jax.ShapeDtypeStruct((1, SSM_GROUPS), F32), mshape, mshape],
                          compiler_params=_cp(("arbitrary",)))(are, aim, ldt, bre, bim, dlbr, dlbi, dbbr, dbbi)


SLAB_NC = SLAB_CH // 128


def _s5_specs(L):
    slab = pl.BlockSpec((L, SLAB), lambda s: (0, s))
    wspec = pl.BlockSpec((SLAB_NC, 128, SLAB), lambda s: (s, 0, 0))
    lspec = pl.BlockSpec((SLAB_NC, 1, 128), lambda s: (s, 0, 0))
    sspec = pl.BlockSpec((SLAB_NC, L, 128), lambda s: (s, 0, 0))
    dspec = pl.BlockSpec((1, SLAB), lambda s: (0, s))
    return slab, wspec, lspec, sspec, dspec


def _scan_inplace(sr_ref, si_ref, lr, li, pr_ref, pi_ref, *, reverse):
    NC, L, W = sr_ref.shape
    S = SCAN_SEG
    T = L // S
    lr8 = [jnp.broadcast_to(lr[k], (S, W)) for k in range(NC)]
    li8 = [jnp.broadcast_to(li[k], (S, W)) for k in range(NC)]

    def tiles(first, count):
        return pl.ds(first * S, count * S)

    for k in range(NC):
        pr_ref[k, tiles(T - 1 if reverse else 0, 1), :] = lr8[k]
        pi_ref[k, tiles(T - 1 if reverse else 0, 1), :] = li8[k]
        n = 1
        while n < T:
            have = tiles(T - n, n) if reverse else tiles(0, n)
            new = tiles(T - 2 * n, n) if reverse else tiles(n, n)
            top = tiles(T - n, 1) if reverse else tiles(n - 1, 1)
            ar, ai = pr_ref[k, top, :][None], pi_ref[k, top, :][None]
            hr, hi = pr_ref[k, have, :].reshape(n, S, W), pi_ref[k, have, :].reshape(n, S, W)
            pr_ref[k, new, :] = (hr * ar - hi * ai).reshape(n * S, W)
            pi_ref[k, new, :] = (hr * ai + hi * ar).reshape(n * S, W)
            n *= 2

    def step(i, carry):
        for u in range(SCAN_STEPS):
            jj = i * SCAN_STEPS + u
            rows = pl.ds(pl.multiple_of(((T - 1 - jj) if reverse else jj) * S, S), S)
            out = []
            for k in range(NC):
                sr, si = carry[k]
                nsr = lr8[k] * sr - li8[k] * si + sr_ref[k, rows, :]
                nsi = lr8[k] * si + li8[k] * sr + si_ref[k, rows, :]
                sr_ref[k, rows, :] = nsr
                si_ref[k, rows, :] = nsi
                out.append((nsr, nsi))
            carry = tuple(out)
        return carry

    zero = jnp.zeros((S, W), F32)
    ends = lax.fori_loop(0, T // SCAN_STEPS, step, tuple((zero, zero) for k in range(NC)))
    sub = lax.broadcasted_iota(jnp.int32, (S, W), 0)
    order = range(S - 1, -1, -1) if reverse else range(S)
    for k in range(NC):
        er, ei = ends[k]
        full = tiles(0 if reverse else T - 1, 1)
        ltr = pr_ref[k, full, :][0:1]
        lti = pi_ref[k, full, :][0:1]
        cr = jnp.zeros((1, W), F32)
        ci = jnp.zeros((1, W), F32)
        ctr = jnp.zeros((S, W), F32)
        cti = jnp.zeros((S, W), F32)
        for seg in order:
            ctr = jnp.where(sub == seg, cr, ctr)
            cti = jnp.where(sub == seg, ci, cti)
            cr, ci = (er[seg:seg + 1, :] + ltr * cr - lti * ci, ei[seg:seg + 1, :] + ltr * ci + lti * cr)
        pr = pr_ref[k].reshape(T, S, W)
        pi = pi_ref[k].reshape(T, S, W)
        sr_ref[k] += (pr * ctr[None] - pi * cti[None]).reshape(L, W)
        si_ref[k] += (pr * cti[None] + pi * ctr[None]).reshape(L, W)


def _time_interleave(a):
    L, W = a.shape
    return a.reshape(SCAN_SEG, L // SCAN_SEG, W).transpose(1, 0, 2).reshape(L, W)


def _time_deinterleave(a):
    L, W = a.shape
    return a.reshape(L // SCAN_SEG, SCAN_SEG, W).transpose(1, 0, 2).reshape(L, W)


def _s5_fwd(u, btr, bti, cbr, cbi, lbr, lbi, dvec, *, name, after=None):
    L = u.shape[0]
    extra = [] if after is None else [after]

    def body(u_ref, btr_ref, bti_ref, cbr_ref, cbi_ref, lr_ref, li_ref, d_ref, *rest):
        ys_ref, sr_ref, si_ref, pr_ref, pi_ref = rest[-5:]
        u = u_ref[...]
        for k in range(SLAB_NC):
            sr_ref[k] = _dot(u, btr_ref[k], "nt")
            si_ref[k] = _dot(u, bti_ref[k], "nt")
        _scan_inplace(sr_ref, si_ref, lr_ref[...], li_ref[...], pr_ref, pi_ref, reverse=False)
        ys = d_ref[...] * u
        for k in range(SLAB_NC):
            ys = ys + _dot(sr_ref[k], cbr_ref[k], "nn") - _dot(si_ref[k], cbi_ref[k], "nn")
        ys_ref[...] = ys

    slab, wspec, lspec, sspec, dspec = _s5_specs(L)
    sshape = jax.ShapeDtypeStruct((N_SLAB * SLAB_NC, L, 128), F32)
    return pl.pallas_call(
        body, name=name, grid=(N_SLAB,),
        in_specs=[slab, wspec, wspec, wspec, wspec, lspec, lspec, dspec] + [pl.BlockSpec(memory_space=pl.ANY)] * len(extra),
        out_specs=[slab, sspec, sspec],
        out_shape=[jax.ShapeDtypeStruct((L, SSM_WIDTH), F32), sshape, sshape],
        scratch_shapes=[pltpu.VMEM((SLAB_NC, L, 128), F32), pltpu.VMEM((SLAB_NC, L, 128), F32)],
        compiler_params=_cp(("parallel",), 56),
    )(u, btr, bti, cbr, cbi, lbr, lbi, dvec, *extra)


def _s5_bwd(dys, u, sr, si, btr, bti, cbr, cbi, lbr, lbi, dvec, *, name, after=None):
    L = u.shape[0]
    S = SCAN_SEG
    extra = [] if after is None else [after]

    def body(dys_ref, u_ref, sr_ref, si_ref, btr_ref, bti_ref, cbr_ref, cbi_ref, lr_ref, li_ref, d_ref, *rest):
        (du_ref, dbtr_ref, dbti_ref, dcbr_ref, dcbi_ref, dlr_ref, dli_ref, dd_ref,
         ar_ref, ai_ref, pr_ref, pi_ref) = rest[-12:]
        dys = dys_ref[...]
        u = u_ref[...]
        for k in range(SLAB_NC):
            ar_ref[k] = _dot(dys, cbr_ref[k], "nt")
            ai_ref[k] = -_dot(dys, cbi_ref[k], "nt")
        _scan_inplace(ar_ref, ai_ref, lr_ref[...], -li_ref[...], pr_ref, pi_ref, reverse=True)
        head = lax.broadcasted_iota(jnp.int32, (L, 1), 0) < S
        sub0 = lax.broadcasted_iota(jnp.int32, (S, 1), 0) == 0

        def prev_state(s):
            up = pltpu.roll(s, S, 0)
            return jnp.where(head, 0.0, up), jnp.where(sub0, 0.0, pltpu.roll(up[0:S], 1, 0))

        du = d_ref[...] * dys
        for k in range(SLAB_NC):
            a_re = ar_ref[k]
            a_im = ai_ref[k]
            du = du + _dot(a_re, btr_ref[k], "nn") + _dot(a_im, bti_ref[k], "nn")
            dbtr_ref[k] = _dot(a_re, u, "tn")
            dbti_ref[k] = _dot(a_im, u, "tn")
            s_re = sr_ref[k]
            s_im = si_ref[k]
            dcbr_ref[k] = _dot(s_re, dys, "tn")
            dcbi_ref[k] = -_dot(s_im, dys, "tn")
            p_re, q_re = prev_state(s_re)
            p_im, q_im = prev_state(s_im)
            b_re, b_im = a_re[0:S], a_im[0:S]
            dlr_ref[k] = (jnp.sum(p_re * a_re + p_im * a_im, axis=0, keepdims=True)
                          + jnp.sum(q_re * b_re + q_im * b_im, axis=0, keepdims=True))
            dli_ref[k] = (jnp.sum(p_re * a_im - p_im * a_re, axis=0, keepdims=True)
                          + jnp.sum(q_re * b_im - q_im * b_re, axis=0, keepdims=True))
        du_ref[...] = du
        dd_ref[...] = jnp.sum(dys * u, axis=0, keepdims=True)

    slab, wspec, lspec, sspec, dspec = _s5_specs(L)
    wshape = jax.ShapeDtypeStruct((N_SLAB * SLAB_NC, 128, SLAB), F32)
    lshape = jax.ShapeDtypeStruct((N_SLAB * SLAB_NC, 1, 128), F32)
    return pl.pallas_call(
        body, name=name, grid=(N_SLAB,),
        in_specs=([slab, slab, sspec, sspec, wspec, wspec, wspec, wspec, lspec, lspec, dspec]
                  + [pl.BlockSpec(memory_space=pl.ANY)] * len(extra)),
        out_specs=[slab, wspec, wspec, wspec, wspec, lspec, lspec, dspec],
        out_shape=[jax.ShapeDtypeStruct((L, SSM_WIDTH), F32), wshape, wshape, wshape, wshape, lshape, lshape,
                   jax.ShapeDtypeStruct((1, SSM_WIDTH), F32)],
        scratch_shapes=[pltpu.VMEM((SLAB_NC, L, 128), F32)] * 4,
        compiler_params=_cp(("parallel",), 56),
    )(dys, u, sr, si, btr, bti, cbr, cbi, lbr, lbi, dvec, *extra)


_SLAB_MASK = (np.arange(SLAB_CH)[:, None] // SSM_STATE == np.arange(SLAB)[None, :] // SSM_GROUP)


def _expand_bd(x):
    t = jnp.tile(x.reshape(N_SLAB, SLAB_CH, SSM_GROUP), (1, 1, SLAB // SSM_GROUP))
    return jnp.where(_SLAB_MASK[None], t, 0.0).astype(MXU).reshape(N_SLAB * SLAB_NC, 128, SLAB)


def _contract_bd(dx):
    t = jnp.where(_SLAB_MASK[None], dx.reshape(N_SLAB, SLAB_CH, SLAB), 0.0)
    return jnp.sum(t.reshape(N_SLAB, SLAB_CH, SLAB // SSM_GROUP, SSM_GROUP), axis=2).reshape(SSM_CH, SSM_GROUP)


def _glu_ew(ys, zlin, za):
    a1 = jax.nn.gelu(ys)
    return a1 * jax.nn.sigmoid(zlin) * _silu(za)


def _glu_fwd(ys, main, gw, gb, *, name, tm=256):
    L = ys.shape[0]
    tm = min(tm, L)
    W = SSM_WIDTH

    def body(ys_ref, za_ref, gw_ref, gb_ref, ya_ref):
        ys = ys_ref[...]
        a1 = jax.nn.gelu(ys)
        zlin = _dot(a1, gw_ref[...], "nn") + gb_ref[...]
        ya_ref[...] = _glu_ew(ys, zlin, za_ref[...]).astype(MXU)

    return pl.pallas_call(
        body, name=name, grid=(L // tm,),
        in_specs=[pl.BlockSpec((tm, W), lambda i: (i, 0)), pl.BlockSpec((tm, W), lambda i: (i, 1)),
                  pl.BlockSpec((W, W), lambda i: (0, 0)), pl.BlockSpec((1, W), lambda i: (0, 0))],
        out_specs=pl.BlockSpec((tm, W), lambda i: (i, 0)),
        out_shape=jax.ShapeDtypeStruct((L, W), MXU),
        compiler_params=_cp(("parallel",)),
    )(ys, main, gw, gb.reshape(1, W))


def _glu_bwd(dya, ys, main, gw, gb, *, name, tm=256):
    L = ys.shape[0]
    tm = min(tm, L)
    W = SSM_WIDTH

    def body(dya_ref, ys_ref, za_ref, gw_ref, gb_ref, dys_ref, dza_ref, a1_ref, dzl_ref, db_ref):
        ys = ys_ref[...]
        a1, gelu_vjp = jax.vjp(jax.nn.gelu, ys)
        zlin = _dot(a1, gw_ref[...], "nn") + gb_ref[...]
        _, vjp = jax.vjp(lambda a, z, za: a * jax.nn.sigmoid(z) * _silu(za), a1, zlin, za_ref[...])
        da1, dzlin, dza = vjp(dya_ref[...].astype(F32))
        da1 = da1 + _dot(dzlin, gw_ref[...], "nt")
        dys_ref[...] = gelu_vjp(da1)[0]
        dza_ref[...] = dza
        a1_ref[...] = a1.astype(MXU)
        dzl_ref[...] = dzlin.astype(MXU)

        @pl.when(pl.program_id(0) == 0)
        def _():
            db_ref[...] = jnp.zeros_like(db_ref)

        db_ref[...] += jnp.sum(dzlin, axis=0, keepdims=True)

    row = pl.BlockSpec((tm, W), lambda i: (i, 0))
    vec = pl.BlockSpec((1, W), lambda i: (0, 0))
    return pl.pallas_call(
        body, name=name, grid=(L // tm,),
        in_specs=[row, row, pl.BlockSpec((tm, W), lambda i: (i, 1)), pl.BlockSpec((W, W), lambda i: (0, 0)), vec],
        out_specs=[row, row, row, row, vec],
        out_shape=[jax.ShapeDtypeStruct((L, W), F32), jax.ShapeDtypeStruct((L, W), F32),
                   jax.ShapeDtypeStruct((L, W), MXU), jax.ShapeDtypeStruct((L, W), MXU),
                   jax.ShapeDtypeStruct((1, W), F32)],
        compiler_params=_cp(("arbitrary",)),
    )(dya, ys, main, gw, gb.reshape(1, W))


def _sg_fn(ub, vb, zb, lnw, lnb, ws, bs):
    u = jax.nn.gelu(ub)
    v = _layernorm(jax.nn.gelu(vb), lnw, lnb)
    r = lax.broadcasted_iota(jnp.int32, (SG_CHUNK, SG_CHUNK), 0)
    c = lax.broadcasted_iota(jnp.int32, (SG_CHUNK, SG_CHUNK), 1)
    tri = r >= c
    outs = []
    for h in range(SG_HEADS):
        wh = jnp.where(tri, ws[h], 0.0)
        outs.append(_mm_nn(wh, v[:, h * 128:(h + 1) * 128]) + bs[h])
    mixed = jnp.concatenate(outs, axis=1)
    return u * mixed * _silu(zb)


def _sg_specs(L):
    W = SSM_WIDTH
    blk = lambda c: pl.BlockSpec((SG_CHUNK, W), lambda i, c=c: (i, c))
    vec = pl.BlockSpec((1, W), lambda i: (0, 0))
    wspec = pl.BlockSpec((SG_HEADS, SG_CHUNK, SG_CHUNK), lambda i: (0, 0, 0))
    bspec = pl.BlockSpec((SG_HEADS, SG_CHUNK, 1), lambda i: (0, 0, 0))
    return blk, vec, wspec, bspec


def _sg_fwd(main, lnw, lnb, sgw, sgb, *, name):
    L = main.shape[0]
    W = SSM_WIDTH
    blk, vec, wspec, bspec = _sg_specs(L)

    def body(ub_ref, vb_ref, zb_ref, lnw_ref, lnb_ref, w_ref, b_ref, yb_ref):
        ws = [w_ref[h] for h in range(SG_HEADS)]
        bs = [b_ref[h] for h in range(SG_HEADS)]
        yb_ref[...] = _sg_fn(ub_ref[...], vb_ref[...], zb_ref[...], lnw_ref[...], lnb_ref[...], ws, bs).astype(MXU)

    return pl.pallas_call(
        body, name=name, grid=(L // SG_CHUNK,),
        in_specs=[blk(2), blk(3), blk(4), vec, vec, wspec, bspec],
        out_specs=pl.BlockSpec((SG_CHUNK, W), lambda i: (i, 0)),
        out_shape=jax.ShapeDtypeStruct((L, W), MXU),
        compiler_params=_cp(("parallel",)),
    )(main, main, main, lnw.reshape(1, W), lnb.reshape(1, W), sgw, sgb.reshape(SG_HEADS, SG_CHUNK, 1))


def _sg_bwd(dyb, main, lnw, lnb, sgw, sgb, *, name):
    L = main.shape[0]
    W = SSM_WIDTH
    blk, vec, wspec, bspec = _sg_specs(L)

    def body(dyb_ref, ub_ref, vb_ref, zb_ref, lnw_ref, lnb_ref, w_ref, b_ref,
             dub_ref, dvb_ref, dzb_ref, dlnw_ref, dlnb_ref, dw_ref, db_ref):
        ws = [w_ref[h] for h in range(SG_HEADS)]
        bs = [b_ref[h] for h in range(SG_HEADS)]
        _, vjp = jax.vjp(_sg_fn, ub_ref[...], vb_ref[...], zb_ref[...], lnw_ref[...], lnb_ref[...], ws, bs)
        dub, dvb, dzb, dlnw, dlnb, dws, dbs = vjp(dyb_ref[...])

        @pl.when(pl.program_id(0) == 0)
        def _():
            dlnw_ref[...] = jnp.zeros_like(dlnw_ref)
            dlnb_ref[...] = jnp.zeros_like(dlnb_ref)
            dw_ref[...] = jnp.zeros_like(dw_ref)
            db_ref[...] = jnp.zeros_like(db_ref)

        dub_ref[...] = dub
        dvb_ref[...] = dvb
        dzb_ref[...] = dzb
        dlnw_ref[...] += dlnw
        dlnb_ref[...] += dlnb
        for h in range(SG_HEADS):
            dw_ref[h] += dws[h]
            db_ref[h] += dbs[h]

    row = pl.BlockSpec((SG_CHUNK, W), lambda i: (i, 0))
    out = jax.ShapeDtypeStruct((L, W), F32)
    return pl.pallas_call(
        body, name=name, grid=(L // SG_CHUNK,),
        in_specs=[row, blk(2), blk(3), blk(4), vec, vec, wspec, bspec],
        out_specs=[row, row, row, vec, vec, wspec, bspec],
        out_shape=[out, out, out, jax.ShapeDtypeStruct((1, W), F32), jax.ShapeDtypeStruct((1, W), F32),
                   jax.ShapeDtypeStruct((SG_HEADS, SG_CHUNK, SG_CHUNK), F32),
                   jax.ShapeDtypeStruct((SG_HEADS, SG_CHUNK, 1), F32)],
        compiler_params=_cp(("arbitrary",)),
    )(dyb, main, main, main, lnw.reshape(1, W), lnb.reshape(1, W), sgw, sgb.reshape(SG_HEADS, SG_CHUNK, 1))


def _rope_tables(L):
    half = ROT_DIM // 2
    inv_freq = ROPE_THETA ** (-jnp.arange(0, ROT_DIM, 2, dtype=F32) / ROT_DIM)
    ang = jnp.arange(L, dtype=F32)[:, None] * inv_freq[None, :]
    cos = jnp.cos(ang)
    sin = jnp.sin(ang)
    ones = jnp.ones((L, HEAD_DIM - ROT_DIM), F32)
    cosf = jnp.concatenate([cos, cos, ones], axis=1)
    sinf = jnp.concatenate([sin, sin, 0.0 * ones], axis=1)
    rot = np.zeros((HEAD_DIM, HEAD_DIM), np.float32)
    for d in range(half):
        rot[d + half, d] = -1.0
        rot[d, d + half] = 1.0
    return cosf, sinf, jnp.asarray(rot)


def _rope(t, cosf, sinf, rot):
    shp = t.shape
    t2 = t.reshape(-1, HEAD_DIM)
    sw = lax.dot_general(t2, rot, _DIMS["nn"], precision=lax.Precision.HIGH, preferred_element_type=F32).reshape(shp)
    return t * cosf + sw * sinf


def _attn_core_parts(s, va, sink):
    h, q, k = s.shape
    m = jnp.maximum(jnp.max(s, axis=-1, keepdims=True), sink)
    e = jnp.exp(s - m)
    es = jnp.exp(sink - m)
    ev = _dot(e.reshape(h * q, k), va, "nn")
    r = 1.0 / (ev[:, HEAD_DIM:HEAD_DIM + 1].reshape(h, q, 1) + es)
    return ev[:, :HEAD_DIM] * r.reshape(h * q, 1), e, r, es


@jax.custom_vjp
def _attn_core(s, v, va, sink):
    return _attn_core_parts(s, va, sink)[0]


def _attn_core_fwd(s, v, va, sink):
    o, e, r, es = _attn_core_parts(s, va, sink)
    return o, (o, e, r, es, v, va)


def _attn_core_bwd(res, do):
    o, e, r, es, v, va = res
    h, q, k = e.shape
    p = e * r
    t = jnp.sum(o * do, axis=-1, keepdims=True).reshape(h, q, 1)
    dp = _dot(do, v, "nt").reshape(h, q, k)
    dv = _dot(p.reshape(h * q, k), do, "tn")
    dsink = -jnp.sum(es * r * t, axis=1, keepdims=True)
    return p * (dp - t), dv, jnp.zeros_like(va), dsink


_attn_core.defvjp(_attn_core_fwd, _attn_core_bwd)


def _attn_block_fn(q, kw, vw, sinks, vaw, cq, sq, ck, sk, rot, q0, k0):
    nk = kw.shape[1]
    qr = _rope(q, cq, sq, rot)
    kr = _rope(kw, ck, sk, rot)
    qpos = q0 + lax.broadcasted_iota(jnp.int32, (1, ATT_BLOCK, nk), 1)
    kpos = k0 + lax.broadcasted_iota(jnp.int32, (1, ATT_BLOCK, nk), 2)
    diff = qpos - kpos
    allowed = (diff >= 0) & (diff < WINDOW)
    outs = []
    for kh in range(ATT_KV_HEADS):
        qh = qr[kh * GQA_GROUP:(kh + 1) * GQA_GROUP].reshape(GQA_GROUP * ATT_BLOCK, HEAD_DIM)
        s = _mm_nt(qh, kr[kh]).reshape(GQA_GROUP, ATT_BLOCK, nk) * (HEAD_DIM ** -0.5)
        s = jnp.where(allowed, s, NEG_INF)
        o = _attn_core(s, vw[kh], vaw[kh], sinks[kh * GQA_GROUP:(kh + 1) * GQA_GROUP])
        outs.append(o.reshape(GQA_GROUP, ATT_BLOCK, HEAD_DIM))
    return jnp.concatenate(outs, axis=0)


def _attn_common(L):
    nwin = min(2 * ATT_BLOCK, L)
    qspec = pl.BlockSpec((ATT_HEADS, ATT_BLOCK, HEAD_DIM), lambda n: (0, n, 0))
    kvspec = pl.BlockSpec((ATT_KV_HEADS, L, HEAD_DIM), lambda n: (0, 0, 0))
    sspec = pl.BlockSpec((ATT_HEADS, 1, 1), lambda n: (0, 0, 0))
    tq = pl.BlockSpec((ATT_BLOCK, HEAD_DIM), lambda n: (n, 0))
    tk = pl.BlockSpec((L, HEAD_DIM), lambda n: (0, 0))
    rspec = pl.BlockSpec((HEAD_DIM, HEAD_DIM), lambda n: (0, 0))
    vaspec = pl.BlockSpec((ATT_KV_HEADS, L, 2 * HEAD_DIM), lambda n: (0, 0, 0))
    return nwin, qspec, kvspec, sspec, tq, tk, rspec, vaspec


def _v_with_ones(vh):
    return jnp.concatenate([vh, jnp.ones_like(vh)], axis=-1).astype(MXU)


def _attn_fwd(qh, kh, vh, sinks, cosf, sinf, rot, *, name):
    L = qh.shape[1]
    nwin, qspec, kvspec, sspec, tq, tk, rspec, vaspec = _attn_common(L)

    def body(q_ref, k_ref, v_ref, s_ref, va_ref, cq_ref, sq_ref, ck_ref, sk_ref, r_ref, o_ref):
        n = pl.program_id(0)
        k0 = pl.multiple_of(jnp.maximum(n - 1, 0) * ATT_BLOCK, ATT_BLOCK)
        win = pl.ds(k0, nwin)
        o_ref[...] = _attn_block_fn(q_ref[...], k_ref[:, win, :], v_ref[:, win, :], s_ref[...], va_ref[:, win, :],
                                    cq_ref[...], sq_ref[...], ck_ref[win, :], sk_ref[win, :], r_ref[...],
                                    n * ATT_BLOCK, k0)

    return pl.pallas_call(
        body, name=name, grid=(L // ATT_BLOCK,),
        in_specs=[qspec, kvspec, kvspec, sspec, vaspec, tq, tq, tk, tk, rspec],
        out_specs=qspec,
        out_shape=jax.ShapeDtypeStruct((ATT_HEADS, L, HEAD_DIM), F32),
        compiler_params=_cp(("parallel",)),
    )(qh, kh, vh, sinks.reshape(ATT_HEADS, 1, 1), _v_with_ones(vh), cosf, sinf, cosf, sinf, rot)


def _attn_bwd(do, qh, kh, vh, sinks, cosf, sinf, rot, *, name):
    L = qh.shape[1]
    nwin, qspec, kvspec, sspec, tq, tk, rspec, vaspec = _attn_common(L)

    def body(do_ref, q_ref, k_ref, v_ref, s_ref, va_ref, cq_ref, sq_ref, ck_ref, sk_ref, r_ref,
             dq_ref, dk_ref, dv_ref, ds_ref):
        n = pl.program_id(0)
        k0 = pl.multiple_of(jnp.maximum(n - 1, 0) * ATT_BLOCK, ATT_BLOCK)
        win = pl.ds(k0, nwin)
        cq, sq, ck, sk, rt = cq_ref[...], sq_ref[...], ck_ref[win, :], sk_ref[win, :], r_ref[...]
        vaw = va_ref[:, win, :]
        q0 = n * ATT_BLOCK
        _, vjp = jax.vjp(lambda q, kw, vw, s: _attn_block_fn(q, kw, vw, s, vaw, cq, sq, ck, sk, rt, q0, k0),
                         q_ref[...], k_ref[:, win, :], v_ref[:, win, :], s_ref[...])
        dq, dkw, dvw, ds = vjp(do_ref[...])

        @pl.when(n == 0)
        def _():
            dk_ref[...] = jnp.zeros_like(dk_ref)
            dv_ref[...] = jnp.zeros_like(dv_ref)
            ds_ref[...] = jnp.zeros_like(ds_ref)

        dq_ref[...] = dq
        dk_ref[:, win, :] += dkw
        dv_ref[:, win, :] += dvw
        ds_ref[...] += ds

    return pl.pallas_call(
        body, name=name, grid=(L // ATT_BLOCK,),
        in_specs=[qspec, qspec, kvspec, kvspec, sspec, vaspec, tq, tq, tk, tk, rspec],
        out_specs=[qspec, kvspec, kvspec, sspec],
        out_shape=[jax.ShapeDtypeStruct((ATT_HEADS, L, HEAD_DIM), F32),
                   jax.ShapeDtypeStruct((ATT_KV_HEADS, L, HEAD_DIM), F32),
                   jax.ShapeDtypeStruct((ATT_KV_HEADS, L, HEAD_DIM), F32),
                   jax.ShapeDtypeStruct((ATT_HEADS, 1, 1), F32)],
        compiler_params=_cp(("arbitrary",)),
    )(do, qh, kh, vh, sinks.reshape(ATT_HEADS, 1, 1), _v_with_ones(vh), cosf, sinf, cosf, sinf, rot)


def _to_heads(t, nh):
    L = t.shape[0]
    return t.reshape(L, nh, HEAD_DIM).transpose(1, 0, 2)


def _from_heads(t):
    nh, L, _ = t.shape
    return t.transpose(1, 0, 2).reshape(L, nh * HEAD_DIM)


def _branch_fwd(ya, yb, o2d, zc, gates, wa, wb, wc, *, name, tm=256):
    L = ya.shape[0]
    tm = min(tm, L)
    W, D = SSM_WIDTH, D_MODEL

    def body(ya_ref, yb_ref, o_ref, zc_ref, g0_ref, g1_ref, g2_ref, wa_ref, wb_ref, wc_ref,
             mg_ref, ta_ref, tb_ref, tc_ref, yc_ref):
        yc = (o_ref[...] * _silu(zc_ref[...])).astype(MXU)
        ta = _dot(ya_ref[...], wa_ref[...], "nt")
        tb = _dot(yb_ref[...], wb_ref[...], "nt")
        tc = _dot(yc, wc_ref[...], "nt")
        ta_ref[...] = ta
        tb_ref[...] = tb
        tc_ref[...] = tc
        yc_ref[...] = yc
        mg_ref[...] = (jax.nn.sigmoid(g0_ref[...]) * ta + jax.nn.sigmoid(g1_ref[...]) * tb
                       + jax.nn.sigmoid(g2_ref[...]) * tc).astype(MXU)

    row = pl.BlockSpec((tm, W), lambda i: (i, 0))
    wide = pl.BlockSpec((tm, D), lambda i: (i, 0))
    gate = lambda c: pl.BlockSpec((tm, D), lambda i, c=c: (i, c))
    wspec = pl.BlockSpec((D, W), lambda i: (0, 0))
    return pl.pallas_call(
        body, name=name, grid=(L // tm,),
        in_specs=[row, row, row, row, gate(0), gate(1), gate(2), wspec, wspec, wspec],
        out_specs=[wide, wide, wide, wide, row],
        out_shape=[jax.ShapeDtypeStruct((L, D), MXU), jax.ShapeDtypeStruct((L, D), F32),
                   jax.ShapeDtypeStruct((L, D), F32), jax.ShapeDtypeStruct((L, D), F32),
                   jax.ShapeDtypeStruct((L, W), MXU)],
        compiler_params=_cp(("parallel",), 56),
    )(ya, yb, o2d, zc, gates, gates, gates, wa, wb, wc)


def _branch_bwd(dmg, ta, tb, tc, gates, *, name, tm=256):
    L = dmg.shape[0]
    tm = min(tm, L)
    D = D_MODEL

    def body(dm_ref, ta_ref, tb_ref, tc_ref, g0_ref, g1_ref, g2_ref, da_ref, db_ref, dc_ref, dg_ref):
        dm = dm_ref[...]
        for i, (t_ref, g_ref, d_ref) in enumerate(((ta_ref, g0_ref, da_ref), (tb_ref, g1_ref, db_ref),
                                                   (tc_ref, g2_ref, dc_ref))):
            sg = jax.nn.sigmoid(g_ref[...])
            d_ref[...] = (sg * dm).astype(MXU)
            dg_ref[:, i * D:(i + 1) * D] = (dm * t_ref[...] * sg * (1.0 - sg)).astype(MXU)

    wide = pl.BlockSpec((tm, D), lambda i: (i, 0))
    gate = lambda c: pl.BlockSpec((tm, D), lambda i, c=c: (i, c))
    bf = jax.ShapeDtypeStruct((L, D), MXU)
    return pl.pallas_call(
        body, name=name, grid=(L // tm,),
        in_specs=[wide, wide, wide, wide, gate(0), gate(1), gate(2)],
        out_specs=[wide, wide, wide, pl.BlockSpec((tm, 3 * D), lambda i: (i, 0))],
        out_shape=[bf, bf, bf, jax.ShapeDtypeStruct((L, 3 * D), MXU)],
        compiler_params=_cp(("parallel",), 56),
    )(dmg, ta, tb, tc, gates, gates, gates)


def _gate_c_bwd(dyc, o2d, zc, *, name, tm=256):
    L, W = dyc.shape
    tm = min(tm, L)

    def body(dy_ref, o_ref, z_ref, do_ref, dz_ref):
        _, vjp = jax.vjp(lambda o, z: o * _silu(z), o_ref[...], z_ref[...])
        do, dz = vjp(dy_ref[...])
        do_ref[...] = do
        dz_ref[...] = dz.astype(MXU)

    row = pl.BlockSpec((tm, W), lambda i: (i, 0))
    return pl.pallas_call(body, name=name, grid=(L // tm,), in_specs=[row, row, row], out_specs=[row, row],
                          out_shape=[jax.ShapeDtypeStruct((L, W), F32), jax.ShapeDtypeStruct((L, W), MXU)],
                          compiler_params=_cp(("parallel",)))(dyc, o2d, zc)


def _adamw(w, g, m, v, *, name):
    shape = w.shape
    cols = shape[-1]
    w2, g2, m2, v2 = (t.reshape(-1, cols) for t in (w, g, m, v))
    rows = w2.shape[0]
    tc = 1024 if cols % 1024 == 0 else cols
    lane_cols = -(-tc // 128) * 128
    tr = rows
    while tr % 16 == 0 and tr * lane_cols * 4 > 2 * _MB:
        tr //= 2

    def body(w_ref, g_ref, m_ref, v_ref, d_ref, nm_ref, nv_ref):
        gv = g_ref[...]
        nm = ADAM_B1 * m_ref[...] + (1.0 - ADAM_B1) * gv
        nv = ADAM_B2 * v_ref[...] + (1.0 - ADAM_B2) * jnp.square(gv)
        m_hat = nm / (1.0 - ADAM_B1 ** ADAM_STEP)
        v_hat = nv / (1.0 - ADAM_B2 ** ADAM_STEP)
        d_ref[...] = -ADAM_LR * (m_hat / (jnp.sqrt(v_hat) + ADAM_EPS) + ADAM_WD * w_ref[...])
        nm_ref[...] = nm
        nv_ref[...] = nv

    spec = pl.BlockSpec((tr, tc), lambda i, j: (i, j))
    out = jax.ShapeDtypeStruct((rows, cols), F32)
    d, nm, nv = pl.pallas_call(body, name=name, grid=(rows // tr, cols // tc), in_specs=[spec] * 4,
                               out_specs=[spec] * 3, out_shape=[out, out, out],
                               compiler_params=_cp(("parallel", "parallel")))(w2, g2, m2, v2)
    return d.reshape(shape), nm.reshape(shape), nv.reshape(shape)


def _adamw_layer(w, g, m, v, l, prev, *, name):
    _, rows, cols = w.shape
    tc = 1024 if cols % 1024 == 0 else cols
    tr = rows
    while tr % 16 == 0 and tr * tc * 4 > 2 * _MB:
        tr //= 2

    def body(w_ref, g_ref, m_ref, v_ref, *rest):
        go_ref, d_ref, nm_ref, nv_ref = rest[-4:]
        gv = g_ref[...]
        nm = ADAM_B1 * m_ref[...] + (1.0 - ADAM_B1) * gv
        nv = ADAM_B2 * v_ref[...] + (1.0 - ADAM_B2) * jnp.square(gv)
        m_hat = nm / (1.0 - ADAM_B1 ** ADAM_STEP)
        v_hat = nv / (1.0 - ADAM_B2 ** ADAM_STEP)
        d_ref[...] = -ADAM_LR * (m_hat / (jnp.sqrt(v_hat) + ADAM_EPS) + ADAM_WD * w_ref[...])
        nm_ref[...] = nm
        nv_ref[...] = nv
        go_ref[...] = gv

    lspec = pl.BlockSpec((None, tr, tc), lambda i, j: (l, i, j))
    gspec = pl.BlockSpec((tr, tc), lambda i, j: (i, j))
    out = jax.ShapeDtypeStruct(w.shape, F32)
    extra = [] if prev is None else list(prev)
    return pl.pallas_call(
        body, name=name, grid=(rows // tr, cols // tc),
        in_specs=[lspec, gspec, lspec, lspec] + [_ANY] * len(extra),
        out_specs=[lspec] * 4, out_shape=[out] * 4,
        input_output_aliases={4 + i: i for i in range(len(extra))},
        compiler_params=_cp(("parallel", "parallel")),
    )(w, g, m, v, *extra)


def _prep_layer(p, l, after=None):
    are = p["ssm_a_re"][l].reshape(SSM_CH, 1)
    aim = p["ssm_a_im"][l].reshape(SSM_CH, 1)
    ldt = p["ssm_log_dt"][l].reshape(1, SSM_GROUPS)
    bre = p["ssm_b_re"][l].reshape(SSM_CH, SSM_GROUP)
    bim = p["ssm_b_im"][l].reshape(SSM_CH, SSM_GROUP)
    lbr, lbi, bbr, bbi = _s5_params_fwd(are, aim, ldt, bre, bim, name=f"s5_params_fwd_{l}", after=after)
    cre = p["ssm_c_re"][l].transpose(0, 2, 1).reshape(SSM_CH, SSM_GROUP)
    cim = p["ssm_c_im"][l].transpose(0, 2, 1).reshape(SSM_CH, SSM_GROUP)
    return dict(raw=(are, aim, ldt, bre, bim),
                lbr=lbr.reshape(N_SLAB * SLAB_NC, 1, 128), lbi=lbi.reshape(N_SLAB * SLAB_NC, 1, 128),
                btr=_expand_bd(bbr), bti=_expand_bd(bbi), cbr=_expand_bd(cre), cbi=_expand_bd(cim),
                dvec=p["ssm_d"][l].reshape(1, SSM_WIDTH))


def _layer_fwd(x, h, p, sp, winT, rest_of, l, tabs, proj_after=None, after_main=None):
    L = x.shape[0]
    cosf, sinf, rot = tabs
    mm = functools.partial(_matmul, h, winT, "nt", tm=L, tn=256, tk=D_MODEL)
    main = mm(name=f"proj_main_{l}", shape=(L, N_MAIN, D_MODEL), after=proj_after)
    then = proj_after if after_main is None else after_main(main)
    zc = mm(name=f"proj_zc_{l}", shape=(L, N_ZC, D_MODEL), b_off=(N_MAIN // 256, 0), after=then)
    gates = mm(name=f"proj_gates_{l}", shape=(L, N_GATES, D_MODEL), b_off=((N_MAIN + N_ZC) // 256, 0), after=then)
    big, token = rest_of([main, zc, gates])
    ua = _time_interleave(main[:, :SSM_WIDTH])
    ys, sr, si = _s5_fwd(ua, sp["btr"], sp["bti"], sp["cbr"], sp["cbi"], sp["lbr"], sp["lbi"], sp["dvec"],
                         name=f"s5_fwd_{l}", after=token)
    ys = _time_deinterleave(ys)
    ya = _glu_fwd(ys, main, big["glu_w"], p["ssm_glu_b"][l], name=f"glu_fwd_{l}")
    yb = _sg_fwd(main, p["sg_ln_w"][l], p["sg_ln_b"][l], p["sg_w"][l], p["sg_b"][l], name=f"sg_fwd_{l}")
    qh = _to_heads(main[:, 5120:6144], ATT_HEADS)
    kh = _to_heads(main[:, 6144:6272], ATT_KV_HEADS)
    vh = _to_heads(main[:, 6272:6400], ATT_KV_HEADS)
    oh = _attn_fwd(qh, kh, vh, p["attn_sinks"][l], cosf, sinf, rot, name=f"attn_fwd_{l}")
    o2d = _from_heads(oh)
    mg, ta, tb, tc, yc = _branch_fwd(ya, yb, o2d, zc, gates, big["wbaT"], big["wbbT"], big["wbcT"],
                                     name=f"branch_fwd_{l}")
    xn = _matmul(mg, big["w_out"], "nn", name=f"out_fwd_{l}", shape=(L, D_MODEL, D_MODEL), tm=1024, tn=1024,
                 tk=D_MODEL, add=x)
    saved = dict(x=x, h=h, main=main, zc=zc, gates=gates, ua=ua, ys=ys, sr=sr, si=si, ya=ya, yb=yb, yc=yc, o2d=o2d,
                 qh=qh, kh=kh, vh=vh, mg=mg, ta=ta, tb=tb, tc=tc, sp=sp)
    return xn, saved, big


def _layer_bwd(dxn, s, p, big, l, tabs, early, mid):
    L = dxn.shape[0]
    D, W = D_MODEL, SSM_WIDTH
    cosf, sinf, rot = tabs
    sp = s["sp"]
    g = {}
    dmg = _matmul(dxn, big["w_out"], "nt", name=f"out_bwd_dm_{l}", shape=(L, D, D), tm=1024, tn=1024, tk=D)
    g["w_out"] = _matmul(s["mg"], dxn, "tn", name=f"out_bwd_dw_{l}", shape=(D, D, L), tm=1024, tn=1024, tk=L,
                         out_dtype=MXU)
    dta, dtb, dtc, dgates = _branch_bwd(dmg, s["ta"], s["tb"], s["tc"], s["gates"], name=f"branch_bwd_{l}")
    dys_ = {}
    for nm, dt, y, wt in (("a", dta, s["ya"], big["wbaT"]), ("b", dtb, s["yb"], big["wbbT"]),
                          ("c", dtc, s["yc"], big["wbcT"])):
        dys_[nm] = _matmul(dt, wt, "nn", name=f"branch_bwd_dy{nm}_{l}", shape=(L, W, D), tm=1024, tn=1024, tk=D)
        g["wb" + nm + "T"] = _matmul(dt, y, "tn", name=f"branch_bwd_dw{nm}_{l}", shape=(D, W, L),
                                     tm=1024, tn=1024, tk=L, out_dtype=MXU)
    do2d, dzc = _gate_c_bwd(dys_["c"], s["o2d"], s["zc"], name=f"gate_c_bwd_{l}")
    dqh, dkh, dvh, dsinks = _attn_bwd(_to_heads(do2d, ATT_HEADS), s["qh"], s["kh"], s["vh"], p["attn_sinks"][l],
                                      cosf, sinf, rot, name=f"attn_bwd_{l}")
    g["attn_sinks"] = dsinks.reshape(ATT_HEADS)
    dub, dvb, dzb, dlnw, dlnb, dsgw, dsgb = _sg_bwd(dys_["b"], s["main"], p["sg_ln_w"][l], p["sg_ln_b"][l],
                                                    p["sg_w"][l], p["sg_b"][l], name=f"sg_bwd_{l}")
    g["sg_ln_w"], g["sg_ln_b"] = dlnw.reshape(W), dlnb.reshape(W)
    g["sg_w"], g["sg_b"] = dsgw, dsgb.reshape(SG_HEADS, SG_CHUNK)
    dys, dza, a1, dzl, dgb = _glu_bwd(dys_["a"], s["ys"], s["main"], big["glu_w"], p["ssm_glu_b"][l],
                                      name=f"glu_bwd_{l}")
    g["ssm_glu_b"] = dgb.reshape(W)
    g["glu_w"] = _matmul(a1, dzl, "tn", name=f"glu_bwd_dw_{l}", shape=(W, W, L), tm=1024, tn=1024, tk=L, out_dtype=MXU)
    token = early(g)
    dua, dbtr, dbti, dcbr, dcbi, dlr, dli, dd = _s5_bwd(_time_interleave(dys), s["ua"], s["sr"], s["si"], sp["btr"],
                                                        sp["bti"], sp["cbr"], sp["cbi"], sp["lbr"], sp["lbi"],
                                                        sp["dvec"], name=f"s5_bwd_{l}", after=token)
    token = mid(dua)
    dua = _time_deinterleave(dua)
    g["ssm_d"] = dd.reshape(W)
    to_c = lambda t: _contract_bd(t).reshape(SSM_GROUPS, SSM_STATE, SSM_GROUP).transpose(0, 2, 1)
    g["ssm_c_re"], g["ssm_c_im"] = to_c(dcbr), to_c(dcbi)
    dare, daim, dldt, dbre, dbim = _s5_params_bwd(*sp["raw"], dlr.reshape(SSM_CH, 1), dli.reshape(SSM_CH, 1),
                                                  _contract_bd(dbtr), _contract_bd(dbti),
                                                  name=f"s5_params_bwd_{l}")
    g["ssm_a_re"] = dare.reshape(SSM_GROUPS, SSM_STATE)
    g["ssm_a_im"] = daim.reshape(SSM_GROUPS, SSM_STATE)
    g["ssm_log_dt"] = dldt.reshape(SSM_GROUPS)
    g["ssm_b_re"] = dbre.reshape(SSM_GROUPS, SSM_STATE, SSM_GROUP)
    g["ssm_b_im"] = dbim.reshape(SSM_GROUPS, SSM_STATE, SSM_GROUP)
    dproj = jnp.concatenate([t.astype(MXU) for t in (dua, dza, dub, dvb, dzb, _from_heads(dqh), _from_heads(dkh),
                                                     _from_heads(dvh), dzc, dgates)], axis=1)
    g["winT"] = _matmul(dproj, s["h"], "tn", name=f"proj_bwd_dw_{l}", shape=(D_IN, D, L), tm=256, tn=D, tk=L,
                        out_dtype=MXU, after=token)
    return dproj, g


def _proj_bwd_dh(dproj, winT, l, after):
    return _matmul(dproj, winT, "nn", name=f"proj_bwd_dh_{l}", shape=(dproj.shape[0], D_MODEL, D_IN), tm=1024, tn=512,
                   tk=D_IN // 2, after=after, vmem_mb=58)


MESH = pl.DeviceIdType.MESH
_ANY = pl.BlockSpec(memory_space=pl.ANY)
ROW_ALIGN = 16


def _coords():
    return lax.axis_index("x"), lax.axis_index("y"), lax.axis_index("c")


def _gather8(arrs, *, name):
    n = len(arrs)
    rows = [a.shape[0] for a in arrs]
    for r in rows:
        assert r % ROW_ALIGN == 0

    def body(*refs):
        ins, outs = refs[:n], refs[n:2 * n]
        send, recv, lsem = refs[2 * n:]
        x, y, c = _coords()
        me, sibling = (x, y, c), (x, y, 1 - c)
        chips = [(1 - x, y), (x, 1 - y), (1 - x, 1 - y)]

        def blk(a, px, py, pc):
            return outs[a].at[pl.ds(pl.multiple_of((4 * px + 2 * py + pc) * rows[a], ROW_ALIGN), rows[a]), :]

        def own(a):
            return ins[a]

        def copy(a, k, block, to, src=None):
            return pltpu.make_async_remote_copy(
                src_ref=blk(a, *block) if src is None else src, dst_ref=blk(a, *block),
                send_sem=send.at[a, k], recv_sem=recv.at[a, k], device_id=to, device_id_type=MESH)

        mine, first, passed = [], [], []
        for a in range(n):
            mine.append(pltpu.make_async_copy(own(a), blk(a, *me), lsem.at[a]))
            mine[a].start()
            f = [copy(a, 0, me, sibling, src=own(a))]
            f += [copy(a, 1 + j, me, (*chip, c), src=own(a)) for j, chip in enumerate(chips)]
            for cp in f:
                cp.start()
            first.append(f)
        for a in range(n):
            ps = [copy(a, 4 + j, (*chip, c), sibling) for j, chip in enumerate(chips)]
            for j, chip in enumerate(chips):
                copy(a, 1 + j, (*chip, c), me).wait_recv()
                ps[j].start()
            passed.append(ps)
        for a in range(n):
            copy(a, 0, sibling, me).wait_recv()
            for j, chip in enumerate(chips):
                copy(a, 4 + j, (*chip, 1 - c), me).wait_recv()
            for cp in first[a] + passed[a]:
                cp.wait_send()
            mine[a].wait()

    return pl.pallas_call(
        body, name=name,
        in_specs=[_ANY] * n, out_specs=[_ANY] * n,
        out_shape=[jax.ShapeDtypeStruct((8 * r,) + a.shape[1:], a.dtype) for r, a in zip(rows, arrs)],
        scratch_shapes=[pltpu.SemaphoreType.DMA((n, 7)), pltpu.SemaphoreType.DMA((n, 7)), pltpu.SemaphoreType.DMA((n,))],
    )(*arrs)


def _sibling_swap(arrs, *, name):
    n = len(arrs)

    def body(*refs):
        ins, outs = refs[:n], refs[n:2 * n]
        send, recv = refs[2 * n:]
        x, y, c = _coords()
        cps = [pltpu.make_async_remote_copy(src_ref=ins[a].at[:, 1 - c], dst_ref=outs[a], send_sem=send.at[a],
                                            recv_sem=recv.at[a], device_id=(x, y, 1 - c), device_id_type=MESH)
               for a in range(n)]
        for cp in cps:
            cp.start()
        for cp in cps:
            cp.wait_recv()
        for cp in cps:
            cp.wait_send()

    return pl.pallas_call(
        body, name=name, in_specs=[_ANY] * n, out_specs=[_ANY] * n,
        out_shape=[jax.ShapeDtypeStruct((a.shape[0],) + a.shape[2:], a.dtype) for a in arrs],
        scratch_shapes=[pltpu.SemaphoreType.DMA((n,)), pltpu.SemaphoreType.DMA((n,))],
    )(*arrs)


def _col_tile(lead, rows, cols, itemsize=4, cap=4 * _MB):
    tc = cols
    while tc % 256 == 0 and lead * rows * tc * itemsize > cap:
        tc //= 2
    return tc


def _pair_sum(mine, theirs, *, name):
    _, _, rows, cols = mine.shape
    tc = _col_tile(1, rows, cols)
    c = lax.axis_index("c")

    def body(c_ref, a_ref, b_ref, o_ref):
        o_ref[...] = (a_ref[...].astype(F32) + b_ref[...].astype(F32)).astype(MXU)

    return pl.pallas_call(
        body, name=name,
        grid_spec=pltpu.PrefetchScalarGridSpec(
            num_scalar_prefetch=1, grid=(4, cols // tc),
            in_specs=[pl.BlockSpec((None, None, rows, tc), lambda j, i, cr: (j, cr[0], 0, i)),
                      pl.BlockSpec((None, rows, tc), lambda j, i, cr: (j, 0, i))],
            out_specs=pl.BlockSpec((None, rows, tc), lambda j, i, cr: (j, 0, i))),
        out_shape=jax.ShapeDtypeStruct((4, rows, cols), MXU),
        compiler_params=_cp(("parallel", "parallel")),
    )(c.reshape(1).astype(jnp.int32), mine, theirs)


_HBM = pl.BlockSpec(memory_space=pltpu.HBM)
_SEM = pl.BlockSpec(memory_space=pltpu.SEMAPHORE)
_EFFECT = pltpu.SideEffectType.DATAFLOW_SIDE_EFFECTING
N_PEER_CHIPS = 3


def _peer_chips(x, y):
    return [(1 - x, y), (x, 1 - y), (1 - x, 1 - y)]


def _split_start(srcs, lands, src_slot, dst_slot, *, name, after=()):
    n = len(srcs)
    ns = n * N_PEER_CHIPS
    first = 2 * n + len(after)

    def body(*refs):
        src_refs, land_refs = refs[:n], refs[n:2 * n]
        send, recv, token = refs[first:first + ns], refs[first + ns:first + 2 * ns], refs[-1]
        x, y, c = _coords()
        for a in range(n):
            for k, (px, py) in enumerate(_peer_chips(x, y)):
                pltpu.make_async_remote_copy(
                    src_ref=src_refs[a].at[src_slot(x, y, c, px, py)], dst_ref=land_refs[a].at[dst_slot(x, y, c)],
                    send_sem=send[a * N_PEER_CHIPS + k], recv_sem=recv[a * N_PEER_CHIPS + k],
                    device_id=(px, py, c), device_id_type=MESH).start()
        token[...] = jnp.zeros_like(token)

    bufs = list(srcs) + list(lands)
    res = pl.pallas_call(
        body, name=name,
        out_shape=(*[pltpu.SemaphoreType.DMA(())] * (2 * ns), *[pltpu.HBM(b.shape, b.dtype) for b in bufs],
                   jax.ShapeDtypeStruct((8, 128), F32)),
        in_specs=[_HBM] * (2 * n) + [_ANY] * len(after),
        out_specs=(*[_SEM] * (2 * ns), *[_HBM] * (2 * n), pl.BlockSpec(memory_space=pltpu.VMEM)),
        input_output_aliases={i: 2 * ns + i for i in range(2 * n)},
        compiler_params=pltpu.CompilerParams(has_side_effects=_EFFECT),
    )(*[pltpu.with_memory_space_constraint(b, pltpu.HBM) for b in bufs], *after)
    sems = list(res[:2 * ns])
    return sems, list(res[2 * ns:2 * ns + n]), list(res[2 * ns + n:2 * ns + 2 * n]), res[-1]


def _split_wait(sems, srcs, lands, after, *, name):
    n = len(srcs)
    ns = n * N_PEER_CHIPS

    def body(*refs):
        src_refs, land_refs = refs[:n], refs[n:2 * n]
        send, recv = refs[2 * n:2 * n + ns], refs[2 * n + ns:2 * n + 2 * ns]
        x, y, c = _coords()
        for a in range(n):
            for k in range(N_PEER_CHIPS):
                cp = pltpu.make_async_remote_copy(
                    src_ref=src_refs[a].at[0], dst_ref=land_refs[a].at[0], send_sem=send[a * N_PEER_CHIPS + k],
                    recv_sem=recv[a * N_PEER_CHIPS + k], device_id=(x, y, 1 - c), device_id_type=MESH)
                cp.wait_send()
                cp.wait_recv()

    bufs = list(srcs) + list(lands)
    res = pl.pallas_call(
        body, name=name,
        out_shape=tuple(pltpu.HBM(b.shape, b.dtype) for b in bufs),
        in_specs=[_HBM] * (2 * n) + [_SEM] * (2 * ns) + [_ANY] * len(after),
        out_specs=tuple([_HBM] * (2 * n)),
        input_output_aliases={i: i for i in range(2 * n)},
        compiler_params=pltpu.CompilerParams(has_side_effects=_EFFECT),
    )(*bufs, *sems, *after)
    return list(res[:n]), list(res[n:])


def _gather_start(lands, *, name, after=()):
    n = len(lands)
    ns = n * N_PEER_CHIPS
    first = n + len(after)

    def body(*refs):
        land_refs = refs[:n]
        send, recv, token = refs[first:first + ns], refs[first + ns:first + 2 * ns], refs[-1]
        x, y, c = _coords()
        mine = 4 * x + 2 * y + c
        for a in range(n):
            for k, (px, py) in enumerate(_peer_chips(x, y)):
                pltpu.make_async_remote_copy(
                    src_ref=land_refs[a].at[mine], dst_ref=land_refs[a].at[mine], send_sem=send[a * N_PEER_CHIPS + k],
                    recv_sem=recv[a * N_PEER_CHIPS + k], device_id=(px, py, c), device_id_type=MESH).start()
        token[...] = jnp.zeros_like(token)

    res = pl.pallas_call(
        body, name=name,
        out_shape=(*[pltpu.SemaphoreType.DMA(())] * (2 * ns), *[pltpu.HBM(b.shape, b.dtype) for b in lands],
                   jax.ShapeDtypeStruct((8, 128), F32)),
        in_specs=[_HBM] * n + [_ANY] * len(after),
        out_specs=(*[_SEM] * (2 * ns), *[_HBM] * n, pl.BlockSpec(memory_space=pltpu.VMEM)),
        input_output_aliases={i: 2 * ns + i for i in range(n)},
        compiler_params=pltpu.CompilerParams(has_side_effects=_EFFECT),
    )(*[pltpu.with_memory_space_constraint(b, pltpu.HBM) for b in lands], *after)
    return list(res[:2 * ns]), list(res[2 * ns:2 * ns + n]), res[-1]


def _gather_wait(sems, lands, after, *, name):
    n = len(lands)
    ns = n * N_PEER_CHIPS

    def body(*refs):
        land_refs = refs[:n]
        send, recv = refs[n:n + ns], refs[n + ns:n + 2 * ns]
        x, y, c = _coords()
        for a in range(n):
            for k in range(N_PEER_CHIPS):
                cp = pltpu.make_async_remote_copy(
                    src_ref=land_refs[a].at[0], dst_ref=land_refs[a].at[0], send_sem=send[a * N_PEER_CHIPS + k],
                    recv_sem=recv[a * N_PEER_CHIPS + k], device_id=(x, y, 1 - c), device_id_type=MESH)
                cp.wait_send()
                cp.wait_recv()

    res = pl.pallas_call(
        body, name=name,
        out_shape=tuple(pltpu.HBM(b.shape, b.dtype) for b in lands),
        in_specs=[_HBM] * n + [_SEM] * (2 * ns) + [_ANY] * len(after),
        out_specs=tuple([_HBM] * n),
        input_output_aliases={i: i for i in range(n)},
        compiler_params=pltpu.CompilerParams(has_side_effects=_EFFECT),
    )(*lands, *sems, *after)
    return list(res)


def _fill_own(shards, l, *, name, after=None):
    _, rows2, cols = shards.shape
    rows = rows2 // 2
    tc = _col_tile(1, rows, cols, itemsize=shards.dtype.itemsize)
    j = 2 * lax.axis_index("x") + lax.axis_index("y")
    extra = [] if after is None else [after]

    def body(j_ref, s_ref, *rest):
        rest[-1][...] = s_ref[...].astype(MXU)

    return pl.pallas_call(
        body, name=name,
        grid_spec=pltpu.PrefetchScalarGridSpec(
            num_scalar_prefetch=1, grid=(2, cols // tc),
            in_specs=([pl.BlockSpec((None, rows, tc), lambda h, i, jr: (l, h, i))]
                      + [pl.BlockSpec(memory_space=pl.ANY)] * len(extra)),
            out_specs=pl.BlockSpec((None, rows, tc), lambda h, i, jr: (2 * jr[0] + h, 0, i))),
        out_shape=jax.ShapeDtypeStruct((8, rows, cols), MXU),
        compiler_params=_cp(("parallel", "parallel")),
    )(j.reshape(1).astype(jnp.int32), shards, *extra)


def _pass_to_sibling(lands, *, name):
    n = len(lands)

    def body(*refs):
        outs = refs[n:2 * n]
        send, recv = refs[2 * n:]
        x, y, c = _coords()
        cps = []
        for a in range(n):
            for k, (px, py) in enumerate(_peer_chips(x, y)):
                slot = 4 * px + 2 * py + c
                cps.append(pltpu.make_async_remote_copy(
                    src_ref=outs[a].at[slot], dst_ref=outs[a].at[slot], send_sem=send.at[a, k], recv_sem=recv.at[a, k],
                    device_id=(x, y, 1 - c), device_id_type=MESH))
        for cp in cps:
            cp.start()
        for cp in cps:
            cp.wait_recv()
        for cp in cps:
            cp.wait_send()

    return pl.pallas_call(
        body, name=name, in_specs=[_ANY] * n, out_specs=[_ANY] * n,
        out_shape=[jax.ShapeDtypeStruct(b.shape, b.dtype) for b in lands],
        input_output_aliases={a: a for a in range(n)},
        scratch_shapes=[pltpu.SemaphoreType.DMA((n, N_PEER_CHIPS)), pltpu.SemaphoreType.DMA((n, N_PEER_CHIPS))],
    )(*lands)


def _sum_parts(parts, got, *, name):
    _, rows, cols = parts.shape
    tc = _col_tile(4, rows, cols, itemsize=parts.dtype.itemsize)
    x, y, c = _coords()
    idx = jnp.stack([2 * x + y, 2 * (1 - x) + y, 2 * x + (1 - y), 2 * (1 - x) + (1 - y), c]).astype(jnp.int32)

    def body(i_ref, p_ref, g0_ref, g1_ref, g2_ref, o_ref):
        o_ref[...] = ((p_ref[...].astype(F32) + g0_ref[...].astype(F32)) + g1_ref[...].astype(F32)) + g2_ref[...].astype(F32)

    slot = lambda s: pl.BlockSpec((None, rows, tc), lambda i, ir, s=s: (ir[s], 0, i))
    return pl.pallas_call(
        body, name=name,
        grid_spec=pltpu.PrefetchScalarGridSpec(
            num_scalar_prefetch=1, grid=(cols // tc,),
            in_specs=[slot(0), slot(1), slot(2), slot(3)],
            out_specs=pl.BlockSpec((None, rows, tc), lambda i, ir: (ir[4], 0, i))),
        out_shape=jax.ShapeDtypeStruct((2, rows, cols), F32),
        compiler_params=_cp(("parallel",)),
    )(idx, parts, got, got, got)


def _sum_slots(t, *, name):
    S, rows, cols = t.shape
    tc = _col_tile(S, rows, cols)

    def body(t_ref, o_ref):
        acc = t_ref[0].astype(F32)
        for s in range(1, S):
            acc = acc + t_ref[s].astype(F32)
        o_ref[...] = acc

    return pl.pallas_call(
        body, name=name, grid=(cols // tc,),
        in_specs=[pl.BlockSpec((S, rows, tc), lambda i: (0, 0, i))],
        out_specs=pl.BlockSpec((rows, tc), lambda i: (0, i)),
        out_shape=jax.ShapeDtypeStruct((rows, cols), F32),
        compiler_params=_cp(("parallel",)),
    )(t)


def _halves_join(bufs, *, name):
    n = len(bufs)

    def body(*refs):
        outs = refs[n:2 * n]
        send, recv = refs[2 * n:]
        x, y, c = _coords()
        cps = [pltpu.make_async_remote_copy(src_ref=outs[a].at[c], dst_ref=outs[a].at[c], send_sem=send.at[a],
                                            recv_sem=recv.at[a], device_id=(x, y, 1 - c), device_id_type=MESH)
               for a in range(n)]
        for cp in cps:
            cp.start()
        for cp in cps:
            cp.wait_recv()
        for cp in cps:
            cp.wait_send()

    return pl.pallas_call(
        body, name=name, in_specs=[_ANY] * n, out_specs=[_ANY] * n,
        out_shape=[jax.ShapeDtypeStruct(b.shape, b.dtype) for b in bufs],
        input_output_aliases={a: a for a in range(n)},
        scratch_shapes=[pltpu.SemaphoreType.DMA((n,)), pltpu.SemaphoreType.DMA((n,))],
    )(*bufs)


def _swap_start(srcs, *, name):
    n = len(srcs)
    lands = [lax.empty((s.shape[0],) + s.shape[2:], s.dtype) for s in srcs]

    def body(*refs):
        src_refs, land_refs = refs[:n], refs[n:2 * n]
        send, recv, token = refs[2 * n:3 * n], refs[3 * n:4 * n], refs[-1]
        x, y, c = _coords()
        for a in range(n):
            pltpu.make_async_remote_copy(src_ref=src_refs[a].at[:, 1 - c], dst_ref=land_refs[a], send_sem=send[a],
                                         recv_sem=recv[a], device_id=(x, y, 1 - c), device_id_type=MESH).start()
        token[...] = jnp.zeros_like(token)

    bufs = list(srcs) + lands
    res = pl.pallas_call(
        body, name=name,
        out_shape=(*[pltpu.SemaphoreType.DMA(())] * (2 * n), *[pltpu.HBM(b.shape, b.dtype) for b in bufs],
                   jax.ShapeDtypeStruct((8, 128), F32)),
        in_specs=[_HBM] * (2 * n),
        out_specs=(*[_SEM] * (2 * n), *[_HBM] * (2 * n), pl.BlockSpec(memory_space=pltpu.VMEM)),
        input_output_aliases={i: 2 * n + i for i in range(2 * n)},
        compiler_params=pltpu.CompilerParams(has_side_effects=_EFFECT),
    )(*[pltpu.with_memory_space_constraint(b, pltpu.HBM) for b in bufs])
    return list(res[:2 * n]), list(res[2 * n:3 * n]), list(res[3 * n:4 * n]), res[-1]


def _swap_wait(sems, srcs, lands, after, *, name):
    n = len(srcs)

    def body(*refs):
        src_refs, land_refs = refs[:n], refs[n:2 * n]
        send, recv = refs[2 * n:3 * n], refs[3 * n:4 * n]
        x, y, c = _coords()
        for a in range(n):
            cp = pltpu.make_async_remote_copy(
                src_ref=src_refs[a].at[:, 0], dst_ref=land_refs[a], send_sem=send[a], recv_sem=recv[a],
                device_id=(x, y, 1 - c), device_id_type=MESH)
            cp.wait_send()
            cp.wait_recv()

    bufs = list(srcs) + list(lands)
    res = pl.pallas_call(
        body, name=name,
        out_shape=tuple(pltpu.HBM(b.shape, b.dtype) for b in bufs),
        in_specs=[_HBM] * (2 * n) + [_SEM] * (2 * n) + [_ANY] * len(after),
        out_specs=tuple([_HBM] * (2 * n)),
        input_output_aliases={i: i for i in range(2 * n)},
        compiler_params=pltpu.CompilerParams(has_side_effects=_EFFECT),
    )(*bufs, *sems, *after)
    return list(res[:n]), list(res[n:])


def _grad_views(grads):
    return [g.reshape(4, 2, g.shape[0] // 8, g.shape[1]) for g in grads]


def _scatter_begin(views, theirs, *, tag):
    parts = [_pair_sum(v, t, name=f"rs_pair_{tag}_{i}") for i, (v, t) in enumerate(zip(views, theirs))]
    got = [lax.empty(p.shape, p.dtype) for p in parts]
    sems, parts, got, token = _split_start(
        parts, got, lambda x, y, c, px, py: 2 * px + py, lambda x, y, c: 2 * x + y, name=f"rs_start_{tag}")
    return (sems, parts, got), token


def _reduce_scatter_begin(grads, *, tag):
    views = _grad_views(grads)
    theirs = _sibling_swap(views, name=f"rs_swap_{tag}")
    return _scatter_begin(views, theirs, tag=tag)


def _reduce_scatter_end(state, after, *, tag):
    sems, parts, got = state
    parts, got = _split_wait(sems, parts, got, after, name=f"rs_wait_{tag}")
    halves = [_sum_parts(p, t, name=f"rs_sum_{tag}_{i}") for i, (p, t) in enumerate(zip(parts, got))]
    joined = _halves_join(halves, name=f"rs_join_{tag}")
    return [j.reshape(2 * j.shape[1], j.shape[2]) for j in joined]


_SMALL = ("norm_w", "ssm_a_re", "ssm_a_im", "ssm_log_dt", "ssm_b_re", "ssm_b_im", "ssm_c_re", "ssm_c_im", "ssm_d",
          "ssm_glu_b", "sg_ln_w", "sg_ln_b", "sg_w", "sg_b", "attn_sinks", "final_norm_w")
_BIG = ("w_in", "ssm_glu_w", "w_branch_a", "w_branch_b", "w_branch_c", "w_out")
_WEIGHTS = ("norm_w", "w_in", "ssm_a_re", "ssm_a_im", "ssm_log_dt", "ssm_b_re", "ssm_b_im", "ssm_c_re", "ssm_c_im",
            "ssm_d", "ssm_glu_w", "ssm_glu_b", "sg_ln_w", "sg_ln_b", "sg_w", "sg_b", "attn_sinks", "w_branch_a",
            "w_branch_b", "w_branch_c", "w_out", "final_norm_w")
_PACK_COLS = 1024
_PACK_ALIGN = 8 * ROW_ALIGN * _PACK_COLS


N_PEERS = 7


def _peer(x, y, c, k):
    return x ^ (k >> 2), y ^ ((k >> 1) & 1), c ^ (k & 1)


def _own_slice(buf, *, name):
    _, rows, cols = buf.shape
    x, y, c = _coords()

    def body(me_ref, s_ref, o_ref):
        o_ref[...] = s_ref[...]

    spec = pl.BlockSpec((None, rows, cols), lambda i, mr: (mr[0], 0, 0))
    return pl.pallas_call(
        body, name=name,
        grid_spec=pltpu.PrefetchScalarGridSpec(num_scalar_prefetch=1, grid=(1,), in_specs=[spec], out_specs=spec),
        out_shape=jax.ShapeDtypeStruct(buf.shape, buf.dtype),
        compiler_params=_cp(("arbitrary",)),
    )((4 * x + 2 * y + c).reshape(1).astype(jnp.int32), buf)


def _slice_exchange_start(buf, land, *, name):
    def body(in_ref, land_ref, *rest):
        send, recv, token = rest[:N_PEERS], rest[N_PEERS:2 * N_PEERS], rest[-1]
        x, y, c = _coords()
        me = 4 * x + 2 * y + c
        for k in range(1, N_PEERS + 1):
            px, py, pc = _peer(x, y, c, k)
            pltpu.make_async_remote_copy(
                src_ref=in_ref.at[4 * px + 2 * py + pc], dst_ref=land_ref.at[me], send_sem=send[k - 1],
                recv_sem=recv[k - 1], device_id=(px, py, pc), device_id_type=MESH).start()
        token[...] = jnp.zeros_like(token)

    res = pl.pallas_call(
        body, name=name,
        out_shape=(*[pltpu.SemaphoreType.DMA(())] * (2 * N_PEERS), pltpu.HBM(buf.shape, buf.dtype),
                   pltpu.HBM(land.shape, land.dtype), jax.ShapeDtypeStruct((8, 128), F32)),
        in_specs=[_HBM, _HBM],
        out_specs=(*[_SEM] * (2 * N_PEERS), _HBM, _HBM, pl.BlockSpec(memory_space=pltpu.VMEM)),
        input_output_aliases={0: 2 * N_PEERS, 1: 2 * N_PEERS + 1},
        compiler_params=pltpu.CompilerParams(has_side_effects=_EFFECT),
    )(pltpu.with_memory_space_constraint(buf, pltpu.HBM), pltpu.with_memory_space_constraint(land, pltpu.HBM))
    return list(res[:2 * N_PEERS]), res[2 * N_PEERS], res[2 * N_PEERS + 1], res[-1]


def _slice_exchange_wait(sems, buf, land, after, *, name):
    def body(in_ref, land_ref, *rest):
        send, recv = rest[:N_PEERS], rest[N_PEERS:2 * N_PEERS]
        x, y, c = _coords()
        for k in range(N_PEERS):
            cp = pltpu.make_async_remote_copy(
                src_ref=in_ref.at[0], dst_ref=land_ref.at[0], send_sem=send[k], recv_sem=recv[k],
                device_id=(x, y, 1 - c), device_id_type=MESH)
            cp.wait_send()
            cp.wait_recv()

    res = pl.pallas_call(
        body, name=name,
        out_shape=(pltpu.HBM(buf.shape, buf.dtype), pltpu.HBM(land.shape, land.dtype)),
        in_specs=[_HBM, _HBM] + [_SEM] * (2 * N_PEERS) + [_ANY] * len(after),
        out_specs=(_HBM, _HBM),
        input_output_aliases={0: 0, 1: 1},
        compiler_params=pltpu.CompilerParams(has_side_effects=_EFFECT),
    )(buf, land, *sems, *after)
    return res[1]


def _allreduce_small_begin(packed):
    rows, cols = packed.shape
    buf = packed.reshape(8, rows // 8, cols)
    sems, buf, land, token = _slice_exchange_start(buf, _own_slice(buf, name="small_grads_own"),
                                                   name="small_grads_start")
    return (sems, buf, land), token


def _allreduce_small_end(state, after):
    got = _slice_exchange_wait(*state, list(after), name="small_grads_wait")
    mine = _sum_slots(got, name="small_grads_sum")
    return _gather8([mine], name="small_grads_gather")[0]


def _pack(ts):
    flat = jnp.concatenate([t.reshape(-1) for t in ts])
    pad = (-flat.shape[0]) % _PACK_ALIGN
    return jnp.pad(flat, (0, pad)).reshape(-1, _PACK_COLS)


def _unpack(buf, like):
    flat = buf.reshape(-1)
    out, pos = [], 0
    for t in like:
        out.append(flat[pos:pos + t.size].reshape(t.shape))
        pos += t.size
    return out


def kernel(x, norm_w, w_in, ssm_a_re, ssm_a_im, ssm_log_dt, ssm_b_re, ssm_b_im, ssm_c_re, ssm_c_im, ssm_d, ssm_glu_w, ssm_glu_b, sg_ln_w, sg_ln_b, sg_w, sg_b, attn_sinks, w_branch_a, w_branch_b, w_branch_c, w_out, final_norm_w, loss_target, m_norm_w, m_w_in, m_ssm_a_re, m_ssm_a_im, m_ssm_log_dt, m_ssm_b_re, m_ssm_b_im, m_ssm_c_re, m_ssm_c_im, m_ssm_d, m_ssm_glu_w, m_ssm_glu_b, m_sg_ln_w, m_sg_ln_b, m_sg_w, m_sg_b, m_attn_sinks, m_w_branch_a, m_w_branch_b, m_w_branch_c, m_w_out, m_final_norm_w, v_norm_w, v_w_in, v_ssm_a_re, v_ssm_a_im, v_ssm_log_dt, v_ssm_b_re, v_ssm_b_im, v_ssm_c_re, v_ssm_c_im, v_ssm_d, v_ssm_glu_w, v_ssm_glu_b, v_sg_ln_w, v_sg_ln_b, v_sg_w, v_sg_b, v_attn_sinks, v_w_branch_a, v_w_branch_b, v_w_branch_c, v_w_out, v_final_norm_w):
    w = dict(norm_w=norm_w, w_in=w_in, ssm_a_re=ssm_a_re, ssm_a_im=ssm_a_im, ssm_log_dt=ssm_log_dt, ssm_b_re=ssm_b_re,
             ssm_b_im=ssm_b_im, ssm_c_re=ssm_c_re, ssm_c_im=ssm_c_im, ssm_d=ssm_d, ssm_glu_w=ssm_glu_w,
             ssm_glu_b=ssm_glu_b, sg_ln_w=sg_ln_w, sg_ln_b=sg_ln_b, sg_w=sg_w, sg_b=sg_b, attn_sinks=attn_sinks,
             w_branch_a=w_branch_a, w_branch_b=w_branch_b, w_branch_c=w_branch_c, w_out=w_out,
             final_norm_w=final_norm_w)
    m = dict(norm_w=m_norm_w, w_in=m_w_in, ssm_a_re=m_ssm_a_re, ssm_a_im=m_ssm_a_im, ssm_log_dt=m_ssm_log_dt,
             ssm_b_re=m_ssm_b_re, ssm_b_im=m_ssm_b_im, ssm_c_re=m_ssm_c_re, ssm_c_im=m_ssm_c_im, ssm_d=m_ssm_d,
             ssm_glu_w=m_ssm_glu_w, ssm_glu_b=m_ssm_glu_b, sg_ln_w=m_sg_ln_w, sg_ln_b=m_sg_ln_b, sg_w=m_sg_w,
             sg_b=m_sg_b, attn_sinks=m_attn_sinks, w_branch_a=m_w_branch_a, w_branch_b=m_w_branch_b,
             w_branch_c=m_w_branch_c, w_out=m_w_out, final_norm_w=m_final_norm_w)
    v = dict(norm_w=v_norm_w, w_in=v_w_in, ssm_a_re=v_ssm_a_re, ssm_a_im=v_ssm_a_im, ssm_log_dt=v_ssm_log_dt,
             ssm_b_re=v_ssm_b_re, ssm_b_im=v_ssm_b_im, ssm_c_re=v_ssm_c_re, ssm_c_im=v_ssm_c_im, ssm_d=v_ssm_d,
             ssm_glu_w=v_ssm_glu_w, ssm_glu_b=v_ssm_glu_b, sg_ln_w=v_sg_ln_w, sg_ln_b=v_sg_ln_b, sg_w=v_sg_w,
             sg_b=v_sg_b, attn_sinks=v_attn_sinks, w_branch_a=v_w_branch_a, w_branch_b=v_w_branch_b,
             w_branch_c=v_w_branch_c, w_out=v_w_out, final_norm_w=v_final_norm_w)

    big_names = ("winT", "glu_w", "wbaT", "wbbT", "wbcT", "w_out")
    L = x.shape[1]
    tabs = _rope_tables(L)
    p = {k: w[k] for k in _SMALL}

    column_sharded = ("w_in", "w_branch_a", "w_branch_b", "w_branch_c")

    row_shards = {k: w[k].transpose(0, 2, 1) if k in column_sharded else w[k] for k in _BIG}
    rows_of = lambda lands: [t.reshape(8 * t.shape[1], t.shape[2]) for t in lands]
    saved = [None] * DEPTH

    land_a = [_fill_own(row_shards["w_in"], 0, name="gather_fill_0_0")]
    sems_a, land_a, token_a = _gather_start(land_a, name="gather_start_0a")
    fill_after = lambda l, names, i0: [_fill_own(row_shards[k], l, name=f"gather_fill_{l}_{i0 + i}", after=token_a)
                                       for i, k in enumerate(names)]
    lands = [[None] + fill_after(0, _BIG[1:], 1), fill_after(1, _BIG, 0)]
    sp = [_prep_layer(p, l, after=token_a) for l in range(DEPTH)]
    h0 = _rms_fwd(x[0], p["norm_w"][0], name="rms_fwd_0", after=token_a)
    land_a = _gather_wait(sems_a, land_a, [h0, sp[0]["btr"], sp[1]["btr"]] + lands[0][1:] + lands[1],
                          name="gather_wait_0a")
    land_a = _pass_to_sibling(land_a, name="gather_pass_0a")
    sems_b, land_b, token_b = _gather_start(lands[0][1:], name="gather_start_0b", after=land_a)
    split1 = {}

    def start1(main):
        split1["sems"], split1["land"], token1 = _gather_start(lands[1][:1], name="gather_start_1a", after=[main])
        return token1

    def rest(l, sems, land, first):
        def arrived(t):
            got = _pass_to_sibling(_gather_wait(sems, land, t, name=f"gather_wait_{l}b"), name=f"gather_pass_{l}b")
            return dict(zip(big_names, rows_of(first + got))), None
        return arrived

    x1, saved[0], big0 = _layer_fwd(x[0], h0, p, sp[0], rows_of(land_a)[0], rest(0, sems_b, land_b, land_a), 0,
                                    tabs, proj_after=token_b, after_main=start1)
    land_1a = _gather_wait(split1["sems"], split1["land"], [x1], name="gather_wait_1a")
    land_1a = _pass_to_sibling(land_1a, name="gather_pass_1a")
    sems_1b, land_1b, token_1b = _gather_start(lands[1][1:], name="gather_start_1b", after=land_1a)
    h1 = _rms_fwd(x1, p["norm_w"][1], name="rms_fwd_1")
    x2, saved[1], big1 = _layer_fwd(x1, h1, p, sp[1], rows_of(land_1a)[0], rest(1, sems_1b, land_1b, land_1a),
                                    1, tabs, proj_after=token_1b)
    bigs = [big0, big1]
    loss, dx, dfw = _final_loss(x2, p["final_norm_w"], loss_target[0], name="final_loss")

    grads = [None] * DEPTH
    rs = {}

    def early(l):
        def begin(g):
            *rs[f"{l}a"], token_a = _swap_start(_grad_views([g[k] for k in big_names[1:]]), name=f"rs_swap_start_{l}a")
            return token_a
        return begin

    def mid(l):
        def go_on(t):
            sems, views, lands = rs[f"{l}a"]
            views, theirs = _swap_wait(sems, views, lands, [t], name=f"rs_swap_wait_{l}a")
            rs[f"{l}a"], token_a = _scatter_begin(views, theirs, tag=f"{l}a")
            return token_a
        return go_on

    def late(l, dproj, dx):
        if l == 0:
            rs["0b"], token_s = _reduce_scatter_begin([grads[0]["winT"]], tag="0b")
            dh = _proj_bwd_dh(dproj, bigs[0]["winT"], 0, token_s)
            return _rms_bwd(saved[0]["x"], p["norm_w"][0], dh, dx, name="rms_bwd_0")
        sems, views, lands, token_b = _swap_start(_grad_views([grads[l]["winT"]]), name=f"rs_swap_start_{l}b")
        dh = _proj_bwd_dh(dproj, bigs[l]["winT"], l, token_b)
        views, theirs = _swap_wait(sems, views, lands, [dh], name=f"rs_swap_wait_{l}b")
        rs[f"{l}b"], token_s = _scatter_begin(views, theirs, tag=f"{l}b")
        return _rms_bwd(saved[l]["x"], p["norm_w"][l], dh, dx, name=f"rms_bwd_{l}", after=token_s)

    def reduced(l, after):
        return _reduce_scatter_end(rs[f"{l}b"], after, tag=f"{l}b") + _reduce_scatter_end(rs[f"{l}a"], after, tag=f"{l}a")

    dproj, grads[1] = _layer_bwd(dx, saved[1], p, bigs[1], 1, tabs, early(1), mid(1))
    dx, grads[1]["norm_w"] = late(1, dproj, dx)
    dproj, grads[0] = _layer_bwd(dx, saved[0], p, bigs[0], 0, tabs, early(0), mid(0))
    dx, grads[0]["norm_w"] = late(0, dproj, dx)
    small_like = [w[k] for k in _SMALL]
    gs = [jnp.stack([grads[l][k] for l in range(DEPTH)]) if k != "final_norm_w" else dfw for k in _SMALL]
    small, token_small = _allreduce_small_begin(_pack(gs + [loss.reshape(1)]))
    red1 = reduced(1, [dx, token_small])

    tr = lambda t: t.transpose(0, 2, 1)
    view = {k: (tr if k == "w_in" else (lambda t: t)) for k in _BIG}
    shard_grads = lambda red: dict(zip(_BIG, (red[0], red[1], red[2].T, red[3].T, red[4].T, red[5])))
    outs = {k: None for k in _BIG}

    def adamw_big(l, red):
        for k, g in shard_grads(red).items():
            outs[k] = _adamw_layer(view[k](w[k]), g, view[k](m[k]), view[k](v[k]), l, outs[k], name=f"adamw_{k}_{l}")

    adamw_big(1, red1)

    gsum = _allreduce_small_end(small, [outs[k][0] for k in _BIG])
    adamw_big(0, reduced(0, [gsum]))

    gfull, delta, new_m, new_v = {}, {}, {}, {}
    for k in _BIG:
        gfull[k], delta[k], new_m[k], new_v[k] = (view[k](t) for t in outs[k])
    *small_sums, loss = _unpack(gsum, small_like + [loss])
    for k, t in zip(_SMALL, small_sums):
        gfull[k] = t
        delta[k], new_m[k], new_v[k] = _adamw(w[k], t, m[k], v[k], name=f"adamw_{k}")

    return (loss, dx[None], *[gfull[k] for k in _WEIGHTS], *[delta[k] for k in _WEIGHTS],
            *[new_m[k] for k in _WEIGHTS], *[new_v[k] for k in _WEIGHTS])
```

```python
import functools
import math

import numpy as np
import jax
import jax.numpy as jnp
from jax import lax
from jax.experimental import pallas as pl
from jax.experimental.pallas import tpu as pltpu

F32 = jnp.float32
MXU = jnp.bfloat16
HIGHEST = lax.Precision.HIGHEST

D_MODEL = 2048
DEPTH = 2
EPS = 1e-6
NEG_INF = -1e30
SSM_WIDTH = 1024
SSM_GROUP = 16
SSM_GROUPS = 64
SSM_STATE = 64
SSM_CH = SSM_GROUPS * SSM_STATE
SLAB = 128
SLAB_CH = (SLAB // SSM_GROUP) * SSM_STATE
N_SLAB = SSM_WIDTH // SLAB
SCAN_SEG = 8
SCAN_STEPS = 4
SG_HEADS = 8
SG_CHUNK = 128
HEAD_DIM = 64
ATT_HEADS = 16
ATT_KV_HEADS = 2
GQA_GROUP = 8
ATT_BLOCK = 128
WINDOW = 128
ROT_DIM = 16
ROPE_THETA = 500000.0
N_MAIN = 6400
N_ZC = 1024
N_GATES = 6144
D_IN = N_MAIN + N_ZC + N_GATES

ADAM_LR = 0.001
ADAM_B1 = 0.9
ADAM_B2 = 0.999
ADAM_EPS = 1e-08
ADAM_WD = 0.01
ADAM_STEP = 10

_DIMS = {"nn": (((1,), (0,)), ((), ())), "nt": (((1,), (1,)), ((), ())), "tn": (((0,), (0,)), ((), ()))}
_MB = 1024 * 1024


def _cp(sem, vmem_mb=48):
    return pltpu.CompilerParams(dimension_semantics=sem, vmem_limit_bytes=vmem_mb * _MB)


def _dot(a, b, mode):
    return lax.dot_general(a.astype(MXU), b.astype(MXU), _DIMS[mode], preferred_element_type=F32)


@jax.custom_vjp
def _mm_nn(a, b):
    return _dot(a, b, "nn")


def _mm_nn_fwd(a, b):
    return _dot(a, b, "nn"), (a, b)


def _mm_nn_bwd(res, g):
    a, b = res
    return _dot(g, b, "nt"), _dot(a, g, "tn")


_mm_nn.defvjp(_mm_nn_fwd, _mm_nn_bwd)


@jax.custom_vjp
def _mm_nt(a, bt):
    return _dot(a, bt, "nt")


def _mm_nt_fwd(a, bt):
    return _dot(a, bt, "nt"), (a, bt)


def _mm_nt_bwd(res, g):
    a, bt = res
    return _dot(g, bt, "nn"), _dot(g, a, "tn")


_mm_nt.defvjp(_mm_nt_fwd, _mm_nt_bwd)


def _rmsnorm(x, w):
    return x * lax.rsqrt(jnp.mean(x * x, axis=-1, keepdims=True) + EPS) * w


def _layernorm(x, w, b):
    mu = jnp.mean(x, axis=-1, keepdims=True)
    var = jnp.mean(jnp.square(x - mu), axis=-1, keepdims=True)
    return (x - mu) * lax.rsqrt(var + EPS) * w + b


def _silu(x):
    return x * jax.nn.sigmoid(x)


def _matmul(a, b, mode, *, name, shape, tm, tn, tk, out_dtype=F32, add=None, a_off=(0, 0), b_off=(0, 0), after=None,
            vmem_mb=48):
    m, n, k = shape
    tm, tn, tk = min(tm, m), min(tn, n), min(tk, k)
    assert m % tm == 0 and n % tn == 0 and k % tk == 0, (name, shape, tm, tn, tk)
    nk = k // tk
    has_add, has_after = add is not None, after is not None

    def body(*refs):
        a_ref, b_ref = refs[0], refs[1]
        pos = 2
        add_ref = None
        if has_add:
            add_ref = refs[pos]
            pos += 1
        if has_after:
            pos += 1
        o_ref = refs[pos]
        p = _dot(a_ref[...], b_ref[...], mode)
        if nk == 1:
            if has_add:
                p = p + add_ref[...].astype(F32)
            o_ref[...] = p.astype(out_dtype)
            return
        acc_ref = refs[pos + 1]
        kk = pl.program_id(2)

        @pl.when(kk == 0)
        def _():
            acc_ref[...] = p

        @pl.when(kk > 0)
        def _():
            acc_ref[...] += p

        @pl.when(kk == nk - 1)
        def _():
            r = acc_ref[...]
            if has_add:
                r = r + add_ref[...].astype(F32)
            o_ref[...] = r.astype(out_dtype)

    a0, a1 = a_off
    b0, b1 = b_off
    if mode == "tn":
        a_spec = pl.BlockSpec((tk, tm), lambda i, j, kk: (kk + a0, i + a1))
    else:
        a_spec = pl.BlockSpec((tm, tk), lambda i, j, kk: (i + a0, kk + a1))
    if mode == "nt":
        b_spec = pl.BlockSpec((tn, tk), lambda i, j, kk: (j + b0, kk + b1))
    else:
        b_spec = pl.BlockSpec((tk, tn), lambda i, j, kk: (kk + b0, j + b1))
    in_specs = [a_spec, b_spec]
    args = [a, b]
    if has_add:
        in_specs.append(pl.BlockSpec((tm, tn), lambda i, j, kk: (i, j)))
        args.append(add)
    if has_after:
        in_specs.append(pl.BlockSpec(memory_space=pl.ANY))
        args.append(after)
    return pl.pallas_call(
        body, name=name, grid=(m // tm, n // tn, nk),
        in_specs=in_specs,
        out_specs=pl.BlockSpec((tm, tn), lambda i, j, kk: (i, j)),
        out_shape=jax.ShapeDtypeStruct((m, n), out_dtype),
        scratch_shapes=[pltpu.VMEM((tm, tn), F32)] if nk > 1 else [],
        compiler_params=_cp(("parallel", "parallel", "arbitrary"), vmem_mb),
    )(*args)


def _rms_fwd(x, w, *, name, tm=256, after=None):
    L, d = x.shape
    tm = min(tm, L)
    extra = [] if after is None else [after]

    def body(x_ref, w_ref, *rest):
        rest[-1][...] = _rmsnorm(x_ref[...], w_ref[...]).astype(MXU)

    return pl.pallas_call(
        body, name=name, grid=(L // tm,),
        in_specs=([pl.BlockSpec((tm, d), lambda i: (i, 0)), pl.BlockSpec((1, d), lambda i: (0, 0))]
                  + [pl.BlockSpec(memory_space=pl.ANY)] * len(extra)),
        out_specs=pl.BlockSpec((tm, d), lambda i: (i, 0)),
        out_shape=jax.ShapeDtypeStruct((L, d), MXU),
        compiler_params=_cp(("parallel",)),
    )(x, w.reshape(1, d), *extra)


def _rms_bwd(x, w, dh, dxn, *, name, tm=256, after=None):
    L, d = x.shape
    tm = min(tm, L)
    extra = [] if after is None else [after]

    def body(x_ref, w_ref, dh_ref, dxn_ref, *rest):
        dx_ref, dw_ref = rest[-2:]
        _, vjp = jax.vjp(_rmsnorm, x_ref[...], w_ref[...])
        dx, dw = vjp(dh_ref[...])
        dx_ref[...] = dx + dxn_ref[...]

        @pl.when(pl.program_id(0) == 0)
        def _():
            dw_ref[...] = jnp.zeros_like(dw_ref)

        dw_ref[...] += dw

    row = pl.BlockSpec((tm, d), lambda i: (i, 0))
    vec = pl.BlockSpec((1, d), lambda i: (0, 0))
    dx, dw = pl.pallas_call(
        body, name=name, grid=(L // tm,),
        in_specs=[row, vec, row, row] + [pl.BlockSpec(memory_space=pl.ANY)] * len(extra), out_specs=[row, vec],
        out_shape=[jax.ShapeDtypeStruct((L, d), F32), jax.ShapeDtypeStruct((1, d), F32)],
        compiler_params=_cp(("arbitrary",)),
    )(x, w.reshape(1, d), dh, dxn, *extra)
    return dx, dw.reshape(d)


def _final_loss(x, w, tgt, *, name, tm=256):
    L, d = x.shape
    tm = min(tm, L)

    def loss_fn(xv, wv, tv):
        err = jnp.square(_rmsnorm(xv, wv) - tv)
        return 0.5 * jnp.sum(jnp.mean(err, axis=-1, keepdims=True), axis=0, keepdims=True)

    def body(x_ref, w_ref, t_ref, loss_ref, dx_ref, dw_ref):
        tv = t_ref[...]
        val, vjp = jax.vjp(lambda xv, wv: loss_fn(xv, wv, tv), x_ref[...], w_ref[...])
        dx, dw = vjp(jnp.ones((1, 1), F32))
        dx_ref[...] = dx

        @pl.when(pl.program_id(0) == 0)
        def _():
            dw_ref[...] = jnp.zeros_like(dw_ref)
            loss_ref[...] = jnp.zeros_like(loss_ref)

        dw_ref[...] += dw
        loss_ref[...] += jnp.broadcast_to(val, loss_ref.shape)

    row = pl.BlockSpec((tm, d), lambda i: (i, 0))
    vec = pl.BlockSpec((1, d), lambda i: (0, 0))
    loss, dx, dw = pl.pallas_call(
        body, name=name, grid=(L // tm,),
        in_specs=[row, vec, row],
        out_specs=[pl.BlockSpec((8, 128), lambda i: (0, 0)), row, vec],
        out_shape=[jax.ShapeDtypeStruct((8, 128), F32), jax.ShapeDtypeStruct((L, d), F32),
                   jax.ShapeDtypeStruct((1, d), F32)],
        compiler_params=_cp(("arbitrary",)),
    )(x, w.reshape(1, d), tgt)
    return loss[0, 0], dx, dw.reshape(d)


PARAM_ROWS = 512


def _s5_param_fn(are, aim, ldt, bre, bim, row0):
    n = are.shape[0]
    grp = (row0 + lax.broadcasted_iota(jnp.int32, (n, SSM_GROUPS), 0)) // SSM_STATE
    col = lax.broadcasted_iota(jnp.int32, (n, SSM_GROUPS), 1)
    sel = (grp == col).astype(F32)
    dt = jnp.sum(sel * jnp.exp(ldt), axis=-1, keepdims=True)
    mag = jnp.exp(are * dt)
    ang = aim * dt
    lbr = mag * jnp.cos(ang)
    lbi = mag * jnp.sin(ang)
    den = are * are + aim * aim
    nr = lbr - 1.0
    kr = (nr * are + lbi * aim) / den
    ki = (lbi * are - nr * aim) / den
    return lbr, lbi, kr * bre - ki * bim, kr * bim + ki * bre


def _s5_param_specs():
    col = pl.BlockSpec((PARAM_ROWS, 1), lambda i: (i, 0))
    mat = pl.BlockSpec((PARAM_ROWS, SSM_GROUP), lambda i: (i, 0))
    vec = pl.BlockSpec((1, SSM_GROUPS), lambda i: (0, 0))
    return col, mat, vec


def _s5_params_fwd(are, aim, ldt, bre, bim, *, name, after=None):
    n = are.shape[0]
    col, mat, vec = _s5_param_specs()
    extra = [] if after is None else [after]

    def body(are_ref, aim_ref, ldt_ref, bre_ref, bim_ref, *rest):
        lbr_ref, lbi_ref, bbr_ref, bbi_ref = rest[-4:]
        row0 = pl.program_id(0) * PARAM_ROWS
        lbr, lbi, bbr, bbi = _s5_param_fn(are_ref[...], aim_ref[...], ldt_ref[...], bre_ref[...], bim_ref[...], row0)
        lbr_ref[...] = lbr
        lbi_ref[...] = lbi
        bbr_ref[...] = bbr
        bbi_ref[...] = bbi

    cshape = jax.ShapeDtypeStruct((n, 1), F32)
    mshape = jax.ShapeDtypeStruct((n, SSM_GROUP), F32)
    return pl.pallas_call(body, name=name, grid=(n // PARAM_ROWS,),
                          in_specs=[col, col, vec, mat, mat] + [pl.BlockSpec(memory_space=pl.ANY)] * len(extra),
                          out_specs=[col, col, mat, mat], out_shape=[cshape, cshape, mshape, mshape],
                          compiler_params=_cp(("parallel",)))(are, aim, ldt, bre, bim, *extra)


def _s5_params_bwd(are, aim, ldt, bre, bim, dlbr, dlbi, dbbr, dbbi, *, name):
    n = are.shape[0]
    col, mat, vec = _s5_param_specs()

    def body(are_ref, aim_ref, ldt_ref, bre_ref, bim_ref, g0, g1, g2, g3, o0, o1, o2, o3, o4):
        row0 = pl.program_id(0) * PARAM_ROWS
        _, vjp = jax.vjp(lambda a, b, c, d, e: _s5_param_fn(a, b, c, d, e, row0),
                         are_ref[...], aim_ref[...], ldt_ref[...], bre_ref[...], bim_ref[...])
        dare, daim, dldt, dbre, dbim = vjp((g0[...], g1[...], g2[...], g3[...]))
        o0[...] = dare
        o1[...] = daim
        o3[...] = dbre
        o4[...] = dbim

        @pl.when(pl.program_id(0) == 0)
        def _():
            o2[...] = jnp.zeros_like(o2)

        o2[...] += dldt

    cshape = jax.ShapeDtypeStruct((n, 1), F32)
    mshape = jax.ShapeDtypeStruct((n, SSM_GROUP), F32)
    return pl.pallas_call(body, name=name, grid=(n // PARAM_ROWS,),
                          in_specs=[col, col, vec, mat, mat, col, col, mat, mat],
                          out_specs=[col, col, vec, mat, mat],
                          out_shape=[cshape, cshape, jax.ShapeDtypeStruct((1, SSM_GROUPS), F32), mshape, mshape],
                          compiler_params=_cp(("arbitrary",)))(are, aim, ldt, bre, bim, dlbr, dlbi, dbbr, dbbi)


SLAB_NC = SLAB_CH // 128


def _s5_specs(L):
    slab = pl.BlockSpec((L, SLAB), lambda s: (0, s))
    wspec = pl.BlockSpec((SLAB_NC, 128, SLAB), lambda s: (s, 0, 0))
    lspec = pl.BlockSpec((SLAB_NC, 1, 128), lambda s: (s, 0, 0))
    sspec = pl.BlockSpec((SLAB_NC, L, 128), lambda s: (s, 0, 0))
    dspec = pl.BlockSpec((1, SLAB), lambda s: (0, s))
    return slab, wspec, lspec, sspec, dspec


def _scan_inplace(sr_ref, si_ref, lr, li, pr_ref, pi_ref, *, reverse):
    NC, L, W = sr_ref.shape
    S = SCAN_SEG
    T = L // S
    lr8 = [jnp.broadcast_to(lr[k], (S, W)) for k in range(NC)]
    li8 = [jnp.broadcast_to(li[k], (S, W)) for k in range(NC)]

    def tiles(first, count):
        return pl.ds(first * S, count * S)

    for k in range(NC):
        pr_ref[k, tiles(T - 1 if reverse else 0, 1), :] = lr8[k]
        pi_ref[k, tiles(T - 1 if reverse else 0, 1), :] = li8[k]
        n = 1
        while n < T:
            have = tiles(T - n, n) if reverse else tiles(0, n)
            new = tiles(T - 2 * n, n) if reverse else tiles(n, n)
            top = tiles(T - n, 1) if reverse else tiles(n - 1, 1)
            ar, ai = pr_ref[k, top, :][None], pi_ref[k, top, :][None]
            hr, hi = pr_ref[k, have, :].reshape(n, S, W), pi_ref[k, have, :].reshape(n, S, W)
            pr_ref[k, new, :] = (hr * ar - hi * ai).reshape(n * S, W)
            pi_ref[k, new, :] = (hr * ai + hi * ar).reshape(n * S, W)
            n *= 2

    def step(i, carry):
        for u in range(SCAN_STEPS):
            jj = i * SCAN_STEPS + u
            rows = pl.ds(pl.multiple_of(((T - 1 - jj) if reverse else jj) * S, S), S)
            out = []
            for k in range(NC):
                sr, si = carry[k]
                nsr = lr8[k] * sr - li8[k] * si + sr_ref[k, rows, :]
                nsi = lr8[k] * si + li8[k] * sr + si_ref[k, rows, :]
                sr_ref[k, rows, :] = nsr
                si_ref[k, rows, :] = nsi
                out.append((nsr, nsi))
            carry = tuple(out)
        return carry

    zero = jnp.zeros((S, W), F32)
    ends = lax.fori_loop(0, T // SCAN_STEPS, step, tuple((zero, zero) for k in range(NC)))
    sub = lax.broadcasted_iota(jnp.int32, (S, W), 0)
    order = range(S - 1, -1, -1) if reverse else range(S)
    for k in range(NC):
        er, ei = ends[k]
        full = tiles(0 if reverse else T - 1, 1)
        ltr = pr_ref[k, full, :][0:1]
        lti = pi_ref[k, full, :][0:1]
        cr = jnp.zeros((1, W), F32)
        ci = jnp.zeros((1, W), F32)
        ctr = jnp.zeros((S, W), F32)
        cti = jnp.zeros((S, W), F32)
        for seg in order:
            ctr = jnp.where(sub == seg, cr, ctr)
            cti = jnp.where(sub == seg, ci, cti)
            cr, ci = (er[seg:seg + 1, :] + ltr * cr - lti * ci, ei[seg:seg + 1, :] + ltr * ci + lti * cr)
        pr = pr_ref[k].reshape(T, S, W)
        pi = pi_ref[k].reshape(T, S, W)
        sr_ref[k] += (pr * ctr[None] - pi * cti[None]).reshape(L, W)
        si_ref[k] += (pr * cti[None] + pi * ctr[None]).reshape(L, W)


def _time_interleave(a):
    L, W = a.shape
    return a.reshape(SCAN_SEG, L // SCAN_SEG, W).transpose(1, 0, 2).reshape(L, W)


def _time_deinterleave(a):
    L, W = a.shape
    return a.reshape(L // SCAN_SEG, SCAN_SEG, W).transpose(1, 0, 2).reshape(L, W)


def _s5_fwd(u, btr, bti, cbr, cbi, lbr, lbi, dvec, *, name, after=None):
    L = u.shape[0]
    extra = [] if after is None else [after]

    def body(u_ref, btr_ref, bti_ref, cbr_ref, cbi_ref, lr_ref, li_ref, d_ref, *rest):
        ys_ref, sr_ref, si_ref, pr_ref, pi_ref = rest[-5:]
        u = u_ref[...]
        for k in range(SLAB_NC):
            sr_ref[k] = _dot(u, btr_ref[k], "nt")
            si_ref[k] = _dot(u, bti_ref[k], "nt")
        _scan_inplace(sr_ref, si_ref, lr_ref[...], li_ref[...], pr_ref, pi_ref, reverse=False)
        ys = d_ref[...] * u
        for k in range(SLAB_NC):
            ys = ys + _dot(sr_ref[k], cbr_ref[k], "nn") - _dot(si_ref[k], cbi_ref[k], "nn")
        ys_ref[...] = ys

    slab, wspec, lspec, sspec, dspec = _s5_specs(L)
    sshape = jax.ShapeDtypeStruct((N_SLAB * SLAB_NC, L, 128), F32)
    return pl.pallas_call(
        body, name=name, grid=(N_SLAB,),
        in_specs=[slab, wspec, wspec, wspec, wspec, lspec, lspec, dspec] + [pl.BlockSpec(memory_space=pl.ANY)] * len(extra),
        out_specs=[slab, sspec, sspec],
        out_shape=[jax.ShapeDtypeStruct((L, SSM_WIDTH), F32), sshape, sshape],
        scratch_shapes=[pltpu.VMEM((SLAB_NC, L, 128), F32), pltpu.VMEM((SLAB_NC, L, 128), F32)],
        compiler_params=_cp(("parallel",), 56),
    )(u, btr, bti, cbr, cbi, lbr, lbi, dvec, *extra)


def _s5_bwd(dys, u, sr, si, btr, bti, cbr, cbi, lbr, lbi, dvec, *, name, after=None):
    L = u.shape[0]
    S = SCAN_SEG
    extra = [] if after is None else [after]

    def body(dys_ref, u_ref, sr_ref, si_ref, btr_ref, bti_ref, cbr_ref, cbi_ref, lr_ref, li_ref, d_ref, *rest):
        (du_ref, dbtr_ref, dbti_ref, dcbr_ref, dcbi_ref, dlr_ref, dli_ref, dd_ref,
         ar_ref, ai_ref, pr_ref, pi_ref) = rest[-12:]
        dys = dys_ref[...]
        u = u_ref[...]
        for k in range(SLAB_NC):
            ar_ref[k] = _dot(dys, cbr_ref[k], "nt")
            ai_ref[k] = -_dot(dys, cbi_ref[k], "nt")
        _scan_inplace(ar_ref, ai_ref, lr_ref[...], -li_ref[...], pr_ref, pi_ref, reverse=True)
        head = lax.broadcasted_iota(jnp.int32, (L, 1), 0) < S
        sub0 = lax.broadcasted_iota(jnp.int32, (S, 1), 0) == 0

        def prev_state(s):
            up = pltpu.roll(s, S, 0)
            return jnp.where(head, 0.0, up), jnp.where(sub0, 0.0, pltpu.roll(up[0:S], 1, 0))

        du = d_ref[...] * dys
        for k in range(SLAB_NC):
            a_re = ar_ref[k]
            a_im = ai_ref[k]
            du = du + _dot(a_re, btr_ref[k], "nn") + _dot(a_im, bti_ref[k], "nn")
            dbtr_ref[k] = _dot(a_re, u, "tn")
            dbti_ref[k] = _dot(a_im, u, "tn")
            s_re = sr_ref[k]
            s_im = si_ref[k]
            dcbr_ref[k] = _dot(s_re, dys, "tn")
            dcbi_ref[k] = -_dot(s_im, dys, "tn")
            p_re, q_re = prev_state(s_re)
            p_im, q_im = prev_state(s_im)
            b_re, b_im = a_re[0:S], a_im[0:S]
            dlr_ref[k] = (jnp.sum(p_re * a_re + p_im * a_im, axis=0, keepdims=True)
                          + jnp.sum(q_re * b_re + q_im * b_im, axis=0, keepdims=True))
            dli_ref[k] = (jnp.sum(p_re * a_im - p_im * a_re, axis=0, keepdims=True)
                          + jnp.sum(q_re * b_im - q_im * b_re, axis=0, keepdims=True))
        du_ref[...] = du
        dd_ref[...] = jnp.sum(dys * u, axis=0, keepdims=True)

    slab, wspec, lspec, sspec, dspec = _s5_specs(L)
    wshape = jax.ShapeDtypeStruct((N_SLAB * SLAB_NC, 128, SLAB), F32)
    lshape = jax.ShapeDtypeStruct((N_SLAB * SLAB_NC, 1, 128), F32)
    return pl.pallas_call(
        body, name=name, grid=(N_SLAB,),
        in_specs=([slab, slab, sspec, sspec, wspec, wspec, wspec, wspec, lspec, lspec, dspec]
                  + [pl.BlockSpec(memory_space=pl.ANY)] * len(extra)),
        out_specs=[slab, wspec, wspec, wspec, wspec, lspec, lspec, dspec],
        out_shape=[jax.ShapeDtypeStruct((L, SSM_WIDTH), F32), wshape, wshape, wshape, wshape, lshape, lshape,
                   jax.ShapeDtypeStruct((1, SSM_WIDTH), F32)],
        scratch_shapes=[pltpu.VMEM((SLAB_NC, L, 128), F32)] * 4,
        compiler_params=_cp(("parallel",), 56),
    )(dys, u, sr, si, btr, bti, cbr, cbi, lbr, lbi, dvec, *extra)


_SLAB_MASK = (np.arange(SLAB_CH)[:, None] // SSM_STATE == np.arange(SLAB)[None, :] // SSM_GROUP)


def _expand_bd(x):
    t = jnp.tile(x.reshape(N_SLAB, SLAB_CH, SSM_GROUP), (1, 1, SLAB // SSM_GROUP))
    return jnp.where(_SLAB_MASK[None], t, 0.0).astype(MXU).reshape(N_SLAB * SLAB_NC, 128, SLAB)


def _contract_bd(dx):
    t = jnp.where(_SLAB_MASK[None], dx.reshape(N_SLAB, SLAB_CH, SLAB), 0.0)
    return jnp.sum(t.reshape(N_SLAB, SLAB_CH, SLAB // SSM_GROUP, SSM_GROUP), axis=2).reshape(SSM_CH, SSM_GROUP)


def _glu_ew(ys, zlin, za):
    a1 = jax.nn.gelu(ys)
    return a1 * jax.nn.sigmoid(zlin) * _silu(za)


def _glu_fwd(ys, main, gw, gb, *, name, tm=256):
    L = ys.shape[0]
    tm = min(tm, L)
    W = SSM_WIDTH

    def body(ys_ref, za_ref, gw_ref, gb_ref, ya_ref):
        ys = ys_ref[...]
        a1 = jax.nn.gelu(ys)
        zlin = _dot(a1, gw_ref[...], "nn") + gb_ref[...]
        ya_ref[...] = _glu_ew(ys, zlin, za_ref[...]).astype(MXU)

    return pl.pallas_call(
        body, name=name, grid=(L // tm,),
        in_specs=[pl.BlockSpec((tm, W), lambda i: (i, 0)), pl.BlockSpec((tm, W), lambda i: (i, 1)),
                  pl.BlockSpec((W, W), lambda i: (0, 0)), pl.BlockSpec((1, W), lambda i: (0, 0))],
        out_specs=pl.BlockSpec((tm, W), lambda i: (i, 0)),
        out_shape=jax.ShapeDtypeStruct((L, W), MXU),
        compiler_params=_cp(("parallel",)),
    )(ys, main, gw, gb.reshape(1, W))


def _glu_bwd(dya, ys, main, gw, gb, *, name, tm=256):
    L = ys.shape[0]
    tm = min(tm, L)
    W = SSM_WIDTH

    def body(dya_ref, ys_ref, za_ref, gw_ref, gb_ref, dys_ref, dza_ref, a1_ref, dzl_ref, db_ref):
        ys = ys_ref[...]
        a1, gelu_vjp = jax.vjp(jax.nn.gelu, ys)
        zlin = _dot(a1, gw_ref[...], "nn") + gb_ref[...]
        _, vjp = jax.vjp(lambda a, z, za: a * jax.nn.sigmoid(z) * _silu(za), a1, zlin, za_ref[...])
        da1, dzlin, dza = vjp(dya_ref[...].astype(F32))
        da1 = da1 + _dot(dzlin, gw_ref[...], "nt")
        dys_ref[...] = gelu_vjp(da1)[0]
        dza_ref[...] = dza
        a1_ref[...] = a1.astype(MXU)
        dzl_ref[...] = dzlin.astype(MXU)

        @pl.when(pl.program_id(0) == 0)
        def _():
            db_ref[...] = jnp.zeros_like(db_ref)

        db_ref[...] += jnp.sum(dzlin, axis=0, keepdims=True)

    row = pl.BlockSpec((tm, W), lambda i: (i, 0))
    vec = pl.BlockSpec((1, W), lambda i: (0, 0))
    return pl.pallas_call(
        body, name=name, grid=(L // tm,),
        in_specs=[row, row, pl.BlockSpec((tm, W), lambda i: (i, 1)), pl.BlockSpec((W, W), lambda i: (0, 0)), vec],
        out_specs=[row, row, row, row, vec],
        out_shape=[jax.ShapeDtypeStruct((L, W), F32), jax.ShapeDtypeStruct((L, W), F32),
                   jax.ShapeDtypeStruct((L, W), MXU), jax.ShapeDtypeStruct((L, W), MXU),
                   jax.ShapeDtypeStruct((1, W), F32)],
        compiler_params=_cp(("arbitrary",)),
    )(dya, ys, main, gw, gb.reshape(1, W))


def _sg_fn(ub, vb, zb, lnw, lnb, ws, bs):
    u = jax.nn.gelu(ub)
    v = _layernorm(jax.nn.gelu(vb), lnw, lnb)
    r = lax.broadcasted_iota(jnp.int32, (SG_CHUNK, SG_CHUNK), 0)
    c = lax.broadcasted_iota(jnp.int32, (SG_CHUNK, SG_CHUNK), 1)
    tri = r >= c
    outs = []
    for h in range(SG_HEADS):
        wh = jnp.where(tri, ws[h], 0.0)
        outs.append(_mm_nn(wh, v[:, h * 128:(h + 1) * 128]) + bs[h])
    mixed = jnp.concatenate(outs, axis=1)
    return u * mixed * _silu(zb)


def _sg_specs(L):
    W = SSM_WIDTH
    blk = lambda c: pl.BlockSpec((SG_CHUNK, W), lambda i, c=c: (i, c))
    vec = pl.BlockSpec((1, W), lambda i: (0, 0))
    wspec = pl.BlockSpec((SG_HEADS, SG_CHUNK, SG_CHUNK), lambda i: (0, 0, 0))
    bspec = pl.BlockSpec((SG_HEADS, SG_CHUNK, 1), lambda i: (0, 0, 0))
    return blk, vec, wspec, bspec


def _sg_fwd(main, lnw, lnb, sgw, sgb, *, name):
    L = main.shape[0]
    W = SSM_WIDTH
    blk, vec, wspec, bspec = _sg_specs(L)

    def body(ub_ref, vb_ref, zb_ref, lnw_ref, lnb_ref, w_ref, b_ref, yb_ref):
        ws = [w_ref[h] for h in range(SG_HEADS)]
        bs = [b_ref[h] for h in range(SG_HEADS)]
        yb_ref[...] = _sg_fn(ub_ref[...], vb_ref[...], zb_ref[...], lnw_ref[...], lnb_ref[...], ws, bs).astype(MXU)

    return pl.pallas_call(
        body, name=name, grid=(L // SG_CHUNK,),
        in_specs=[blk(2), blk(3), blk(4), vec, vec, wspec, bspec],
        out_specs=pl.BlockSpec((SG_CHUNK, W), lambda i: (i, 0)),
        out_shape=jax.ShapeDtypeStruct((L, W), MXU),
        compiler_params=_cp(("parallel",)),
    )(main, main, main, lnw.reshape(1, W), lnb.reshape(1, W), sgw, sgb.reshape(SG_HEADS, SG_CHUNK, 1))


def _sg_bwd(dyb, main, lnw, lnb, sgw, sgb, *, name):
    L = main.shape[0]
    W = SSM_WIDTH
    blk, vec, wspec, bspec = _sg_specs(L)

    def body(dyb_ref, ub_ref, vb_ref, zb_ref, lnw_ref, lnb_ref, w_ref, b_ref,
             dub_ref, dvb_ref, dzb_ref, dlnw_ref, dlnb_ref, dw_ref, db_ref):
        ws = [w_ref[h] for h in range(SG_HEADS)]
        bs = [b_ref[h] for h in range(SG_HEADS)]
        _, vjp = jax.vjp(_sg_fn, ub_ref[...], vb_ref[...], zb_ref[...], lnw_ref[...], lnb_ref[...], ws, bs)
        dub, dvb, dzb, dlnw, dlnb, dws, dbs = vjp(dyb_ref[...])

        @pl.when(pl.program_id(0) == 0)
        def _():
            dlnw_ref[...] = jnp.zeros_like(dlnw_ref)
            dlnb_ref[...] = jnp.zeros_like(dlnb_ref)
            dw_ref[...] = jnp.zeros_like(dw_ref)
            db_ref[...] = jnp.zeros_like(db_ref)

        dub_ref[...] = dub
        dvb_ref[...] = dvb
        dzb_ref[...] = dzb
        dlnw_ref[...] += dlnw
        dlnb_ref[...] += dlnb
        for h in range(SG_HEADS):
            dw_ref[h] += dws[h]
            db_ref[h] += dbs[h]

    row = pl.BlockSpec((SG_CHUNK, W), lambda i: (i, 0))
    out = jax.ShapeDtypeStruct((L, W), F32)
    return pl.pallas_call(
        body, name=name, grid=(L // SG_CHUNK,),
        in_specs=[row, blk(2), blk(3), blk(4), vec, vec, wspec, bspec],
        out_specs=[row, row, row, vec, vec, wspec, bspec],
        out_shape=[out, out, out, jax.ShapeDtypeStruct((1, W), F32), jax.ShapeDtypeStruct((1, W), F32),
                   jax.ShapeDtypeStruct((SG_HEADS, SG_CHUNK, SG_CHUNK), F32),
                   jax.ShapeDtypeStruct((SG_HEADS, SG_CHUNK, 1), F32)],
        compiler_params=_cp(("arbitrary",)),
    )(dyb, main, main, main, lnw.reshape(1, W), lnb.reshape(1, W), sgw, sgb.reshape(SG_HEADS, SG_CHUNK, 1))


def _rope_tables(L):
    half = ROT_DIM // 2
    inv_freq = ROPE_THETA ** (-jnp.arange(0, ROT_DIM, 2, dtype=F32) / ROT_DIM)
    ang = jnp.arange(L, dtype=F32)[:, None] * inv_freq[None, :]
    cos = jnp.cos(ang)
    sin = jnp.sin(ang)
    ones = jnp.ones((L, HEAD_DIM - ROT_DIM), F32)
    cosf = jnp.concatenate([cos, cos, ones], axis=1)
    sinf = jnp.concatenate([sin, sin, 0.0 * ones], axis=1)
    rot = np.zeros((HEAD_DIM, HEAD_DIM), np.float32)
    for d in range(half):
        rot[d + half, d] = -1.0
        rot[d, d + half] = 1.0
    return cosf, sinf, jnp.asarray(rot)


def _rope(t, cosf, sinf, rot):
    shp = t.shape
    t2 = t.reshape(-1, HEAD_DIM)
    sw = lax.dot_general(t2, rot, _DIMS["nn"], precision=lax.Precision.HIGH, preferred_element_type=F32).reshape(shp)
    return t * cosf + sw * sinf


def _attn_core_parts(s, va, sink):
    h, q, k = s.shape
    m = jnp.maximum(jnp.max(s, axis=-1, keepdims=True), sink)
    e = jnp.exp(s - m)
    es = jnp.exp(sink - m)
    ev = _dot(e.reshape(h * q, k), va, "nn")
    r = 1.0 / (ev[:, HEAD_DIM:HEAD_DIM + 1].reshape(h, q, 1) + es)
    return ev[:, :HEAD_DIM] * r.reshape(h * q, 1), e, r, es


@jax.custom_vjp
def _attn_core(s, v, va, sink):
    return _attn_core_parts(s, va, sink)[0]


def _attn_core_fwd(s, v, va, sink):
    o, e, r, es = _attn_core_parts(s, va, sink)
    return o, (o, e, r, es, v, va)


def _attn_core_bwd(res, do):
    o, e, r, es, v, va = res
    h, q, k = e.shape
    p = e * r
    t = jnp.sum(o * do, axis=-1, keepdims=True).reshape(h, q, 1)
    dp = _dot(do, v, "nt").reshape(h, q, k)
    dv = _dot(p.reshape(h * q, k), do, "tn")
    dsink = -jnp.sum(es * r * t, axis=1, keepdims=True)
    return p * (dp - t), dv, jnp.zeros_like(va), dsink


_attn_core.defvjp(_attn_core_fwd, _attn_core_bwd)


def _attn_block_fn(q, kw, vw, sinks, vaw, cq, sq, ck, sk, rot, q0, k0):
    nk = kw.shape[1]
    qr = _rope(q, cq, sq, rot)
    kr = _rope(kw, ck, sk, rot)
    qpos = q0 + lax.broadcasted_iota(jnp.int32, (1, ATT_BLOCK, nk), 1)
    kpos = k0 + lax.broadcasted_iota(jnp.int32, (1, ATT_BLOCK, nk), 2)
    diff = qpos - kpos
    allowed = (diff >= 0) & (diff < WINDOW)
    outs = []
    for kh in range(ATT_KV_HEADS):
        qh = qr[kh * GQA_GROUP:(kh + 1) * GQA_GROUP].reshape(GQA_GROUP * ATT_BLOCK, HEAD_DIM)
        s = _mm_nt(qh, kr[kh]).reshape(GQA_GROUP, ATT_BLOCK, nk) * (HEAD_DIM ** -0.5)
        s = jnp.where(allowed, s, NEG_INF)
        o = _attn_core(s, vw[kh], vaw[kh], sinks[kh * GQA_GROUP:(kh + 1) * GQA_GROUP])
        outs.append(o.reshape(GQA_GROUP, ATT_BLOCK, HEAD_DIM))
    return jnp.concatenate(outs, axis=0)


def _attn_common(L):
    nwin = min(2 * ATT_BLOCK, L)
    qspec = pl.BlockSpec((ATT_HEADS, ATT_BLOCK, HEAD_DIM), lambda n: (0, n, 0))
    kvspec = pl.BlockSpec((ATT_KV_HEADS, L, HEAD_DIM), lambda n: (0, 0, 0))
    sspec = pl.BlockSpec((ATT_HEADS, 1, 1), lambda n: (0, 0, 0))
    tq = pl.BlockSpec((ATT_BLOCK, HEAD_DIM), lambda n: (n, 0))
    tk = pl.BlockSpec((L, HEAD_DIM), lambda n: (0, 0))
    rspec = pl.BlockSpec((HEAD_DIM, HEAD_DIM), lambda n: (0, 0))
    vaspec = pl.BlockSpec((ATT_KV_HEADS, L, 2 * HEAD_DIM), lambda n: (0, 0, 0))
    return nwin, qspec, kvspec, sspec, tq, tk, rspec, vaspec


def _v_with_ones(vh):
    return jnp.concatenate([vh, jnp.ones_like(vh)], axis=-1).astype(MXU)


def _attn_fwd(qh, kh, vh, sinks, cosf, sinf, rot, *, name):
    L = qh.shape[1]
    nwin, qspec, kvspec, sspec, tq, tk, rspec, vaspec = _attn_common(L)

    def body(q_ref, k_ref, v_ref, s_ref, va_ref, cq_ref, sq_ref, ck_ref, sk_ref, r_ref, o_ref):
        n = pl.program_id(0)
        k0 = pl.multiple_of(jnp.maximum(n - 1, 0) * ATT_BLOCK, ATT_BLOCK)
        win = pl.ds(k0, nwin)
        o_ref[...] = _attn_block_fn(q_ref[...], k_ref[:, win, :], v_ref[:, win, :], s_ref[...], va_ref[:, win, :],
                                    cq_ref[...], sq_ref[...], ck_ref[win, :], sk_ref[win, :], r_ref[...],
                                    n * ATT_BLOCK, k0)

    return pl.pallas_call(
        body, name=name, grid=(L // ATT_BLOCK,),
        in_specs=[qspec, kvspec, kvspec, sspec, vaspec, tq, tq, tk, tk, rspec],
        out_specs=qspec,
        out_shape=jax.ShapeDtypeStruct((ATT_HEADS, L, HEAD_DIM), F32),
        compiler_params=_cp(("parallel",)),
    )(qh, kh, vh, sinks.reshape(ATT_HEADS, 1, 1), _v_with_ones(vh), cosf, sinf, cosf, sinf, rot)


def _attn_bwd(do, qh, kh, vh, sinks, cosf, sinf, rot, *, name):
    L = qh.shape[1]
    nwin, qspec, kvspec, sspec, tq, tk, rspec, vaspec = _attn_common(L)

    def body(do_ref, q_ref, k_ref, v_ref, s_ref, va_ref, cq_ref, sq_ref, ck_ref, sk_ref, r_ref,
             dq_ref, dk_ref, dv_ref, ds_ref):
        n = pl.program_id(0)
        k0 = pl.multiple_of(jnp.maximum(n - 1, 0) * ATT_BLOCK, ATT_BLOCK)
        win = pl.ds(k0, nwin)
        cq, sq, ck, sk, rt = cq_ref[...], sq_ref[...], ck_ref[win, :], sk_ref[win, :], r_ref[...]
        vaw = va_ref[:, win, :]
        q0 = n * ATT_BLOCK
        _, vjp = jax.vjp(lambda q, kw, vw, s: _attn_block_fn(q, kw, vw, s, vaw, cq, sq, ck, sk, rt, q0, k0),
                         q_ref[...], k_ref[:, win, :], v_ref[:, win, :], s_ref[...])
        dq, dkw, dvw, ds = vjp(do_ref[...])

        @pl.when(n == 0)
        def _():
            dk_ref[...] = jnp.zeros_like(dk_ref)
            dv_ref[...] = jnp.zeros_like(dv_ref)
            ds_ref[...] = jnp.zeros_like(ds_ref)

        dq_ref[...] = dq
        dk_ref[:, win, :] += dkw
        dv_ref[:, win, :] += dvw
        ds_ref[...] += ds

    return pl.pallas_call(
        body, name=name, grid=(L // ATT_BLOCK,),
        in_specs=[qspec, qspec, kvspec, kvspec, sspec, vaspec, tq, tq, tk, tk, rspec],
        out_specs=[qspec, kvspec, kvspec, sspec],
        out_shape=[jax.ShapeDtypeStruct((ATT_HEADS, L, HEAD_DIM), F32),
                   jax.ShapeDtypeStruct((ATT_KV_HEADS, L, HEAD_DIM), F32),
                   jax.ShapeDtypeStruct((ATT_KV_HEADS, L, HEAD_DIM), F32),
                   jax.ShapeDtypeStruct((ATT_HEADS, 1, 1), F32)],
        compiler_params=_cp(("arbitrary",)),
    )(do, qh, kh, vh, sinks.reshape(ATT_HEADS, 1, 1), _v_with_ones(vh), cosf, sinf, cosf, sinf, rot)


def _to_heads(t, nh):
    L = t.shape[0]
    return t.reshape(L, nh, HEAD_DIM).transpose(1, 0, 2)


def _from_heads(t):
    nh, L, _ = t.shape
    return t.transpose(1, 0, 2).reshape(L, nh * HEAD_DIM)


def _branch_fwd(ya, yb, o2d, zc, gates, wa, wb, wc, *, name, tm=256):
    L = ya.shape[0]
    tm = min(tm, L)
    W, D = SSM_WIDTH, D_MODEL

    def body(ya_ref, yb_ref, o_ref, zc_ref, g0_ref, g1_ref, g2_ref, wa_ref, wb_ref, wc_ref,
             mg_ref, ta_ref, tb_ref, tc_ref, yc_ref):
        yc = (o_ref[...] * _silu(zc_ref[...])).astype(MXU)
        ta = _dot(ya_ref[...], wa_ref[...], "nt")
        tb = _dot(yb_ref[...], wb_ref[...], "nt")
        tc = _dot(yc, wc_ref[...], "nt")
        ta_ref[...] = ta
        tb_ref[...] = tb
        tc_ref[...] = tc
        yc_ref[...] = yc
        mg_ref[...] = (jax.nn.sigmoid(g0_ref[...]) * ta + jax.nn.sigmoid(g1_ref[...]) * tb
                       + jax.nn.sigmoid(g2_ref[...]) * tc).astype(MXU)

    row = pl.BlockSpec((tm, W), lambda i: (i, 0))
    wide = pl.BlockSpec((tm, D), lambda i: (i, 0))
    gate = lambda c: pl.BlockSpec((tm, D), lambda i, c=c: (i, c))
    wspec = pl.BlockSpec((D, W), lambda i: (0, 0))
    return pl.pallas_call(
        body, name=name, grid=(L // tm,),
        in_specs=[row, row, row, row, gate(0), gate(1), gate(2), wspec, wspec, wspec],
        out_specs=[wide, wide, wide, wide, row],
        out_shape=[jax.ShapeDtypeStruct((L, D), MXU), jax.ShapeDtypeStruct((L, D), F32),
                   jax.ShapeDtypeStruct((L, D), F32), jax.ShapeDtypeStruct((L, D), F32),
                   jax.ShapeDtypeStruct((L, W), MXU)],
        compiler_params=_cp(("parallel",), 56),
    )(ya, yb, o2d, zc, gates, gates, gates, wa, wb, wc)


def _branch_bwd(dmg, ta, tb, tc, gates, *, name, tm=256):
    L = dmg.shape[0]
    tm = min(tm, L)
    D = D_MODEL

    def body(dm_ref, ta_ref, tb_ref, tc_ref, g0_ref, g1_ref, g2_ref, da_ref, db_ref, dc_ref, dg_ref):
        dm = dm_ref[...]
        for i, (t_ref, g_ref, d_ref) in enumerate(((ta_ref, g0_ref, da_ref), (tb_ref, g1_ref, db_ref),
                                                   (tc_ref, g2_ref, dc_ref))):
            sg = jax.nn.sigmoid(g_ref[...])
            d_ref[...] = (sg * dm).astype(MXU)
            dg_ref[:, i * D:(i + 1) * D] = (dm * t_ref[...] * sg * (1.0 - sg)).astype(MXU)

    wide = pl.BlockSpec((tm, D), lambda i: (i, 0))
    gate = lambda c: pl.BlockSpec((tm, D), lambda i, c=c: (i, c))
    bf = jax.ShapeDtypeStruct((L, D), MXU)
    return pl.pallas_call(
        body, name=name, grid=(L // tm,),
        in_specs=[wide, wide, wide, wide, gate(0), gate(1), gate(2)],
        out_specs=[wide, wide, wide, pl.BlockSpec((tm, 3 * D), lambda i: (i, 0))],
        out_shape=[bf, bf, bf, jax.ShapeDtypeStruct((L, 3 * D), MXU)],
        compiler_params=_cp(("parallel",), 56),
    )(dmg, ta, tb, tc, gates, gates, gates)


def _gate_c_bwd(dyc, o2d, zc, *, name, tm=256):
    L, W = dyc.shape
    tm = min(tm, L)

    def body(dy_ref, o_ref, z_ref, do_ref, dz_ref):
        _, vjp = jax.vjp(lambda o, z: o * _silu(z), o_ref[...], z_ref[...])
        do, dz = vjp(dy_ref[...])
        do_ref[...] = do
        dz_ref[...] = dz.astype(MXU)

    row = pl.BlockSpec((tm, W), lambda i: (i, 0))
    return pl.pallas_call(body, name=name, grid=(L // tm,), in_specs=[row, row, row], out_specs=[row, row],
                          out_shape=[jax.ShapeDtypeStruct((L, W), F32), jax.ShapeDtypeStruct((L, W), MXU)],
                          compiler_params=_cp(("parallel",)))(dyc, o2d, zc)


def _adamw(w, g, m, v, *, name):
    shape = w.shape
    cols = shape[-1]
    w2, g2, m2, v2 = (t.reshape(-1, cols) for t in (w, g, m, v))
    rows = w2.shape[0]
    tc = 1024 if cols % 1024 == 0 else cols
    lane_cols = -(-tc // 128) * 128
    tr = rows
    while tr % 16 == 0 and tr * lane_cols * 4 > 2 * _MB:
        tr //= 2

    def body(w_ref, g_ref, m_ref, v_ref, d_ref, nm_ref, nv_ref):
        gv = g_ref[...]
        nm = ADAM_B1 * m_ref[...] + (1.0 - ADAM_B1) * gv
        nv = ADAM_B2 * v_ref[...] + (1.0 - ADAM_B2) * jnp.square(gv)
        m_hat = nm / (1.0 - ADAM_B1 ** ADAM_STEP)
        v_hat = nv / (1.0 - ADAM_B2 ** ADAM_STEP)
        d_ref[...] = -ADAM_LR * (m_hat / (jnp.sqrt(v_hat) + ADAM_EPS) + ADAM_WD * w_ref[...])
        nm_ref[...] = nm
        nv_ref[...] = nv

    spec = pl.BlockSpec((tr, tc), lambda i, j: (i, j))
    out = jax.ShapeDtypeStruct((rows, cols), F32)
    d, nm, nv = pl.pallas_call(body, name=name, grid=(rows // tr, cols // tc), in_specs=[spec] * 4,
                               out_specs=[spec] * 3, out_shape=[out, out, out],
                               compiler_params=_cp(("parallel", "parallel")))(w2, g2, m2, v2)
    return d.reshape(shape), nm.reshape(shape), nv.reshape(shape)


def _adamw_layer(w, g, m, v, l, prev, *, name):
    _, rows, cols = w.shape
    tc = 1024 if cols % 1024 == 0 else cols
    tr = rows
    while tr % 16 == 0 and tr * tc * 4 > 2 * _MB:
        tr //= 2

    def body(w_ref, g_ref, m_ref, v_ref, *rest):
        go_ref, d_ref, nm_ref, nv_ref = rest[-4:]
        gv = g_ref[...]
        nm = ADAM_B1 * m_ref[...] + (1.0 - ADAM_B1) * gv
        nv = ADAM_B2 * v_ref[...] + (1.0 - ADAM_B2) * jnp.square(gv)
        m_hat = nm / (1.0 - ADAM_B1 ** ADAM_STEP)
        v_hat = nv / (1.0 - ADAM_B2 ** ADAM_STEP)
        d_ref[...] = -ADAM_LR * (m_hat / (jnp.sqrt(v_hat) + ADAM_EPS) + ADAM_WD * w_ref[...])
        nm_ref[...] = nm
        nv_ref[...] = nv
        go_ref[...] = gv

    lspec = pl.BlockSpec((None, tr, tc), lambda i, j: (l, i, j))
    gspec = pl.BlockSpec((tr, tc), lambda i, j: (i, j))
    out = jax.ShapeDtypeStruct(w.shape, F32)
    extra = [] if prev is None else list(prev)
    return pl.pallas_call(
        body, name=name, grid=(rows // tr, cols // tc),
        in_specs=[lspec, gspec, lspec, lspec] + [_ANY] * len(extra),
        out_specs=[lspec] * 4, out_shape=[out] * 4,
        input_output_aliases={4 + i: i for i in range(len(extra))},
        compiler_params=_cp(("parallel", "parallel")),
    )(w, g, m, v, *extra)


def _prep_layer(p, l, after=None):
    are = p["ssm_a_re"][l].reshape(SSM_CH, 1)
    aim = p["ssm_a_im"][l].reshape(SSM_CH, 1)
    ldt = p["ssm_log_dt"][l].reshape(1, SSM_GROUPS)
    bre = p["ssm_b_re"][l].reshape(SSM_CH, SSM_GROUP)
    bim = p["ssm_b_im"][l].reshape(SSM_CH, SSM_GROUP)
    lbr, lbi, bbr, bbi = _s5_params_fwd(are, aim, ldt, bre, bim, name=f"s5_params_fwd_{l}", after=after)
    cre = p["ssm_c_re"][l].transpose(0, 2, 1).reshape(SSM_CH, SSM_GROUP)
    cim = p["ssm_c_im"][l].transpose(0, 2, 1).reshape(SSM_CH, SSM_GROUP)
    return dict(raw=(are, aim, ldt, bre, bim),
                lbr=lbr.reshape(N_SLAB * SLAB_NC, 1, 128), lbi=lbi.reshape(N_SLAB * SLAB_NC, 1, 128),
                btr=_expand_bd(bbr), bti=_expand_bd(bbi), cbr=_expand_bd(cre), cbi=_expand_bd(cim),
                dvec=p["ssm_d"][l].reshape(1, SSM_WIDTH))


def _layer_fwd(x, h, p, sp, winT, rest_of, l, tabs, proj_after=None, after_main=None):
    L = x.shape[0]
    cosf, sinf, rot = tabs
    mm = functools.partial(_matmul, h, winT, "nt", tm=L, tn=256, tk=D_MODEL)
    main = mm(name=f"proj_main_{l}", shape=(L, N_MAIN, D_MODEL), after=proj_after)
    then = proj_after if after_main is None else after_main(main)
    zc = mm(name=f"proj_zc_{l}", shape=(L, N_ZC, D_MODEL), b_off=(N_MAIN // 256, 0), after=then)
    gates = mm(name=f"proj_gates_{l}", shape=(L, N_GATES, D_MODEL), b_off=((N_MAIN + N_ZC) // 256, 0), after=then)
    big, token = rest_of([main, zc, gates])
    ua = _time_interleave(main[:, :SSM_WIDTH])
    ys, sr, si = _s5_fwd(ua, sp["btr"], sp["bti"], sp["cbr"], sp["cbi"], sp["lbr"], sp["lbi"], sp["dvec"],
                         name=f"s5_fwd_{l}", after=token)
    ys = _time_deinterleave(ys)
    ya = _glu_fwd(ys, main, big["glu_w"], p["ssm_glu_b"][l], name=f"glu_fwd_{l}")
    yb = _sg_fwd(main, p["sg_ln_w"][l], p["sg_ln_b"][l], p["sg_w"][l], p["sg_b"][l], name=f"sg_fwd_{l}")
    qh = _to_heads(main[:, 5120:6144], ATT_HEADS)
    kh = _to_heads(main[:, 6144:6272], ATT_KV_HEADS)
    vh = _to_heads(main[:, 6272:6400], ATT_KV_HEADS)
    oh = _attn_fwd(qh, kh, vh, p["attn_sinks"][l], cosf, sinf, rot, name=f"attn_fwd_{l}")
    o2d = _from_heads(oh)
    mg, ta, tb, tc, yc = _branch_fwd(ya, yb, o2d, zc, gates, big["wbaT"], big["wbbT"], big["wbcT"],
                                     name=f"branch_fwd_{l}")
    xn = _matmul(mg, big["w_out"], "nn", name=f"out_fwd_{l}", shape=(L, D_MODEL, D_MODEL), tm=1024, tn=1024,
                 tk=D_MODEL, add=x)
    saved = dict(x=x, h=h, main=main, zc=zc, gates=gates, ua=ua, ys=ys, sr=sr, si=si, ya=ya, yb=yb, yc=yc, o2d=o2d,
                 qh=qh, kh=kh, vh=vh, mg=mg, ta=ta, tb=tb, tc=tc, sp=sp)
    return xn, saved, big


def _layer_bwd(dxn, s, p, big, l, tabs, early, mid):
    L = dxn.shape[0]
    D, W = D_MODEL, SSM_WIDTH
    cosf, sinf, rot = tabs
    sp = s["sp"]
    g = {}
    dmg = _matmul(dxn, big["w_out"], "nt", name=f"out_bwd_dm_{l}", shape=(L, D, D), tm=1024, tn=1024, tk=D)
    g["w_out"] = _matmul(s["mg"], dxn, "tn", name=f"out_bwd_dw_{l}", shape=(D, D, L), tm=1024, tn=1024, tk=L,
                         out_dtype=MXU)
    dta, dtb, dtc, dgates = _branch_bwd(dmg, s["ta"], s["tb"], s["tc"], s["gates"], name=f"branch_bwd_{l}")
    dys_ = {}
    for nm, dt, y, wt in (("a", dta, s["ya"], big["wbaT"]), ("b", dtb, s["yb"], big["wbbT"]),
                          ("c", dtc, s["yc"], big["wbcT"])):
        dys_[nm] = _matmul(dt, wt, "nn", name=f"branch_bwd_dy{nm}_{l}", shape=(L, W, D), tm=1024, tn=1024, tk=D)
        g["wb" + nm + "T"] = _matmul(dt, y, "tn", name=f"branch_bwd_dw{nm}_{l}", shape=(D, W, L),
                                     tm=1024, tn=1024, tk=L, out_dtype=MXU)
    do2d, dzc = _gate_c_bwd(dys_["c"], s["o2d"], s["zc"], name=f"gate_c_bwd_{l}")
    dqh, dkh, dvh, dsinks = _attn_bwd(_to_heads(do2d, ATT_HEADS), s["qh"], s["kh"], s["vh"], p["attn_sinks"][l],
                                      cosf, sinf, rot, name=f"attn_bwd_{l}")
    g["attn_sinks"] = dsinks.reshape(ATT_HEADS)
    dub, dvb, dzb, dlnw, dlnb, dsgw, dsgb = _sg_bwd(dys_["b"], s["main"], p["sg_ln_w"][l], p["sg_ln_b"][l],
                                                    p["sg_w"][l], p["sg_b"][l], name=f"sg_bwd_{l}")
    g["sg_ln_w"], g["sg_ln_b"] = dlnw.reshape(W), dlnb.reshape(W)
    g["sg_w"], g["sg_b"] = dsgw, dsgb.reshape(SG_HEADS, SG_CHUNK)
    dys, dza, a1, dzl, dgb = _glu_bwd(dys_["a"], s["ys"], s["main"], big["glu_w"], p["ssm_glu_b"][l],
                                      name=f"glu_bwd_{l}")
    g["ssm_glu_b"] = dgb.reshape(W)
    g["glu_w"] = _matmul(a1, dzl, "tn", name=f"glu_bwd_dw_{l}", shape=(W, W, L), tm=1024, tn=1024, tk=L, out_dtype=MXU)
    token = early(g)
    dua, dbtr, dbti, dcbr, dcbi, dlr, dli, dd = _s5_bwd(_time_interleave(dys), s["ua"], s["sr"], s["si"], sp["btr"],
                                                        sp["bti"], sp["cbr"], sp["cbi"], sp["lbr"], sp["lbi"],
                                                        sp["dvec"], name=f"s5_bwd_{l}", after=token)
    token = mid(dua)
    dua = _time_deinterleave(dua)
    g["ssm_d"] = dd.reshape(W)
    to_c = lambda t: _contract_bd(t).reshape(SSM_GROUPS, SSM_STATE, SSM_GROUP).transpose(0, 2, 1)
    g["ssm_c_re"], g["ssm_c_im"] = to_c(dcbr), to_c(dcbi)
    dare, daim, dldt, dbre, dbim = _s5_params_bwd(*sp["raw"], dlr.reshape(SSM_CH, 1), dli.reshape(SSM_CH, 1),
                                                  _contract_bd(dbtr), _contract_bd(dbti),
                                                  name=f"s5_params_bwd_{l}")
    g["ssm_a_re"] = dare.reshape(SSM_GROUPS, SSM_STATE)
    g["ssm_a_im"] = daim.reshape(SSM_GROUPS, SSM_STATE)
    g["ssm_log_dt"] = dldt.reshape(SSM_GROUPS)
    g["ssm_b_re"] = dbre.reshape(SSM_GROUPS, SSM_STATE, SSM_GROUP)
    g["ssm_b_im"] = dbim.reshape(SSM_GROUPS, SSM_STATE, SSM_GROUP)
    dproj = jnp.concatenate([t.astype(MXU) for t in (dua, dza, dub, dvb, dzb, _from_heads(dqh), _from_heads(dkh),
                                                     _from_heads(dvh), dzc, dgates)], axis=1)
    g["winT"] = _matmul(dproj, s["h"], "tn", name=f"proj_bwd_dw_{l}", shape=(D_IN, D, L), tm=256, tn=D, tk=L,
                        out_dtype=MXU, after=token)
    return dproj, g


def _proj_bwd_dh(dproj, winT, l, after):
    return _matmul(dproj, winT, "nn", name=f"proj_bwd_dh_{l}", shape=(dproj.shape[0], D_MODEL, D_IN), tm=1024, tn=512,
                   tk=D_IN // 2, after=after, vmem_mb=58)


MESH = pl.DeviceIdType.MESH
_ANY = pl.BlockSpec(memory_space=pl.ANY)
ROW_ALIGN = 16


def _coords():
    return lax.axis_index("x"), lax.axis_index("y"), lax.axis_index("c")


def _gather8(arrs, *, name):
    n = len(arrs)
    rows = [a.shape[0] for a in arrs]
    for r in rows:
        assert r % ROW_ALIGN == 0

    def body(*refs):
        ins, outs = refs[:n], refs[n:2 * n]
        send, recv, lsem = refs[2 * n:]
        x, y, c = _coords()
        me, sibling = (x, y, c), (x, y, 1 - c)
        chips = [(1 - x, y), (x, 1 - y), (1 - x, 1 - y)]

        def blk(a, px, py, pc):
            return outs[a].at[pl.ds(pl.multiple_of((4 * px + 2 * py + pc) * rows[a], ROW_ALIGN), rows[a]), :]

        def own(a):
            return ins[a]

        def copy(a, k, block, to, src=None):
            return pltpu.make_async_remote_copy(
                src_ref=blk(a, *block) if src is None else src, dst_ref=blk(a, *block),
                send_sem=send.at[a, k], recv_sem=recv.at[a, k], device_id=to, device_id_type=MESH)

        mine, first, passed = [], [], []
        for a in range(n):
            mine.append(pltpu.make_async_copy(own(a), blk(a, *me), lsem.at[a]))
            mine[a].start()
            f = [copy(a, 0, me, sibling, src=own(a))]
            f += [copy(a, 1 + j, me, (*chip, c), src=own(a)) for j, chip in enumerate(chips)]
            for cp in f:
                cp.start()
            first.append(f)
        for a in range(n):
            ps = [copy(a, 4 + j, (*chip, c), sibling) for j, chip in enumerate(chips)]
            for j, chip in enumerate(chips):
                copy(a, 1 + j, (*chip, c), me).wait_recv()
                ps[j].start()
            passed.append(ps)
        for a in range(n):
            copy(a, 0, sibling, me).wait_recv()
            for j, chip in enumerate(chips):
                copy(a, 4 + j, (*chip, 1 - c), me).wait_recv()
            for cp in first[a] + passed[a]:
                cp.wait_send()
            mine[a].wait()

    return pl.pallas_call(
        body, name=name,
        in_specs=[_ANY] * n, out_specs=[_ANY] * n,
        out_shape=[jax.ShapeDtypeStruct((8 * r,) + a.shape[1:], a.dtype) for r, a in zip(rows, arrs)],
        scratch_shapes=[pltpu.SemaphoreType.DMA((n, 7)), pltpu.SemaphoreType.DMA((n, 7)), pltpu.SemaphoreType.DMA((n,))],
    )(*arrs)


def _sibling_swap(arrs, *, name):
    n = len(arrs)

    def body(*refs):
        ins, outs = refs[:n], refs[n:2 * n]
        send, recv = refs[2 * n:]
        x, y, c = _coords()
        cps = [pltpu.make_async_remote_copy(src_ref=ins[a].at[:, 1 - c], dst_ref=outs[a], send_sem=send.at[a],
                                            recv_sem=recv.at[a], device_id=(x, y, 1 - c), device_id_type=MESH)
               for a in range(n)]
        for cp in cps:
            cp.start()
        for cp in cps:
            cp.wait_recv()
        for cp in cps:
            cp.wait_send()

    return pl.pallas_call(
        body, name=name, in_specs=[_ANY] * n, out_specs=[_ANY] * n,
        out_shape=[jax.ShapeDtypeStruct((a.shape[0],) + a.shape[2:], a.dtype) for a in arrs],
        scratch_shapes=[pltpu.SemaphoreType.DMA((n,)), pltpu.SemaphoreType.DMA((n,))],
    )(*arrs)


def _col_tile(lead, rows, cols, itemsize=4, cap=4 * _MB):
    tc = cols
    while tc % 256 == 0 and lead * rows * tc * itemsize > cap:
        tc //= 2
    return tc


def _pair_sum(mine, theirs, *, name):
    _, _, rows, cols = mine.shape
    tc = _col_tile(1, rows, cols)
    c = lax.axis_index("c")

    def body(c_ref, a_ref, b_ref, o_ref):
        o_ref[...] = (a_ref[...].astype(F32) + b_ref[...].astype(F32)).astype(MXU)

    return pl.pallas_call(
        body, name=name,
        grid_spec=pltpu.PrefetchScalarGridSpec(
            num_scalar_prefetch=1, grid=(4, cols // tc),
            in_specs=[pl.BlockSpec((None, None, rows, tc), lambda j, i, cr: (j, cr[0], 0, i)),
                      pl.BlockSpec((None, rows, tc), lambda j, i, cr: (j, 0, i))],
            out_specs=pl.BlockSpec((None, rows, tc), lambda j, i, cr: (j, 0, i))),
        out_shape=jax.ShapeDtypeStruct((4, rows, cols), MXU),
        compiler_params=_cp(("parallel", "parallel")),
    )(c.reshape(1).astype(jnp.int32), mine, theirs)


_HBM = pl.BlockSpec(memory_space=pltpu.HBM)
_SEM = pl.BlockSpec(memory_space=pltpu.SEMAPHORE)
_EFFECT = pltpu.SideEffectType.DATAFLOW_SIDE_EFFECTING
N_PEER_CHIPS = 3


def _peer_chips(x, y):
    return [(1 - x, y), (x, 1 - y), (1 - x, 1 - y)]


def _split_start(srcs, lands, src_slot, dst_slot, *, name, after=()):
    n = len(srcs)
    ns = n * N_PEER_CHIPS
    first = 2 * n + len(after)

    def body(*refs):
        src_refs, land_refs = refs[:n], refs[n:2 * n]
        send, recv, token = refs[first:first + ns], refs[first + ns:first + 2 * ns], refs[-1]
        x, y, c = _coords()
        for a in range(n):
            for k, (px, py) in enumerate(_peer_chips(x, y)):
                pltpu.make_async_remote_copy(
                    src_ref=src_refs[a].at[src_slot(x, y, c, px, py)], dst_ref=land_refs[a].at[dst_slot(x, y, c)],
                    send_sem=send[a * N_PEER_CHIPS + k], recv_sem=recv[a * N_PEER_CHIPS + k],
                    device_id=(px, py, c), device_id_type=MESH).start()
        token[...] = jnp.zeros_like(token)

    bufs = list(srcs) + list(lands)
    res = pl.pallas_call(
        body, name=name,
        out_shape=(*[pltpu.SemaphoreType.DMA(())] * (2 * ns), *[pltpu.HBM(b.shape, b.dtype) for b in bufs],
                   jax.ShapeDtypeStruct((8, 128), F32)),
        in_specs=[_HBM] * (2 * n) + [_ANY] * len(after),
        out_specs=(*[_SEM] * (2 * ns), *[_HBM] * (2 * n), pl.BlockSpec(memory_space=pltpu.VMEM)),
        input_output_aliases={i: 2 * ns + i for i in range(2 * n)},
        compiler_params=pltpu.CompilerParams(has_side_effects=_EFFECT),
    )(*[pltpu.with_memory_space_constraint(b, pltpu.HBM) for b in bufs], *after)
    sems = list(res[:2 * ns])
    return sems, list(res[2 * ns:2 * ns + n]), list(res[2 * ns + n:2 * ns + 2 * n]), res[-1]


def _split_wait(sems, srcs, lands, after, *, name):
    n = len(srcs)
    ns = n * N_PEER_CHIPS

    def body(*refs):
        src_refs, land_refs = refs[:n], refs[n:2 * n]
        send, recv = refs[2 * n:2 * n + ns], refs[2 * n + ns:2 * n + 2 * ns]
        x, y, c = _coords()
        for a in range(n):
            for k in range(N_PEER_CHIPS):
                cp = pltpu.make_async_remote_copy(
                    src_ref=src_refs[a].at[0], dst_ref=land_refs[a].at[0], send_sem=send[a * N_PEER_CHIPS + k],
                    recv_sem=recv[a * N_PEER_CHIPS + k], device_id=(x, y, 1 - c), device_id_type=MESH)
                cp.wait_send()
                cp.wait_recv()

    bufs = list(srcs) + list(lands)
    res = pl.pallas_call(
        body, name=name,
        out_shape=tuple(pltpu.HBM(b.shape, b.dtype) for b in bufs),
        in_specs=[_HBM] * (2 * n) + [_SEM] * (2 * ns) + [_ANY] * len(after),
        out_specs=tuple([_HBM] * (2 * n)),
        input_output_aliases={i: i for i in range(2 * n)},
        compiler_params=pltpu.CompilerParams(has_side_effects=_EFFECT),
    )(*bufs, *sems, *after)
    return list(res[:n]), list(res[n:])


def _gather_start(lands, *, name, after=()):
    n = len(lands)
    ns = n * N_PEER_CHIPS
    first = n + len(after)

    def body(*refs):
        land_refs = refs[:n]
        send, recv, token = refs[first:first + ns], refs[first + ns:first + 2 * ns], refs[-1]
        x, y, c = _coords()
        mine = 4 * x + 2 * y + c
        for a in range(n):
            for k, (px, py) in enumerate(_peer_chips(x, y)):
                pltpu.make_async_remote_copy(
                    src_ref=land_refs[a].at[mine], dst_ref=land_refs[a].at[mine], send_sem=send[a * N_PEER_CHIPS + k],
                    recv_sem=recv[a * N_PEER_CHIPS + k], device_id=(px, py, c), device_id_type=MESH).start()
        token[...] = jnp.zeros_like(token)

    res = pl.pallas_call(
        body, name=name,
        out_shape=(*[pltpu.SemaphoreType.DMA(())] * (2 * ns), *[pltpu.HBM(b.shape, b.dtype) for b in lands],
                   jax.ShapeDtypeStruct((8, 128), F32)),
        in_specs=[_HBM] * n + [_ANY] * len(after),
        out_specs=(*[_SEM] * (2 * ns), *[_HBM] * n, pl.BlockSpec(memory_space=pltpu.VMEM)),
        input_output_aliases={i: 2 * ns + i for i in range(n)},
        compiler_params=pltpu.CompilerParams(has_side_effects=_EFFECT),
    )(*[pltpu.with_memory_space_constraint(b, pltpu.HBM) for b in lands], *after)
    return list(res[:2 * ns]), list(res[2 * ns:2 * ns + n]), res[-1]


def _gather_wait(sems, lands, after, *, name):
    n = len(lands)
    ns = n * N_PEER_CHIPS

    def body(*refs):
        land_refs = refs[:n]
        send, recv = refs[n:n + ns], refs[n + ns:n + 2 * ns]
        x, y, c = _coords()
        for a in range(n):
            for k in range(N_PEER_CHIPS):
                cp = pltpu.make_async_remote_copy(
                    src_ref=land_refs[a].at[0], dst_ref=land_refs[a].at[0], send_sem=send[a * N_PEER_CHIPS + k],
                    recv_sem=recv[a * N_PEER_CHIPS + k], device_id=(x, y, 1 - c), device_id_type=MESH)
                cp.wait_send()
                cp.wait_recv()

    res = pl.pallas_call(
        body, name=name,
        out_shape=tuple(pltpu.HBM(b.shape, b.dtype) for b in lands),
        in_specs=[_HBM] * n + [_SEM] * (2 * ns) + [_ANY] * len(after),
        out_specs=tuple([_HBM] * n),
        input_output_aliases={i: i for i in range(n)},
        compiler_params=pltpu.CompilerParams(has_side_effects=_EFFECT),
    )(*lands, *sems, *after)
    return list(res)


def _fill_own(shards, l, *, name, after=None):
    _, rows2, cols = shards.shape
    rows = rows2 // 2
    tc = _col_tile(1, rows, cols, itemsize=shards.dtype.itemsize)
    j = 2 * lax.axis_index("x") + lax.axis_index("y")
    extra = [] if after is None else [after]

    def body(j_ref, s_ref, *rest):
        rest[-1][...] = s_ref[...].astype(MXU)

    return pl.pallas_call(
        body, name=name,
        grid_spec=pltpu.PrefetchScalarGridSpec(
            num_scalar_prefetch=1, grid=(2, cols // tc),
            in_specs=([pl.BlockSpec((None, rows, tc), lambda h, i, jr: (l, h, i))]
                      + [pl.BlockSpec(memory_space=pl.ANY)] * len(extra)),
            out_specs=pl.BlockSpec((None, rows, tc), lambda h, i, jr: (2 * jr[0] + h, 0, i))),
        out_shape=jax.ShapeDtypeStruct((8, rows, cols), MXU),
        compiler_params=_cp(("parallel", "parallel")),
    )(j.reshape(1).astype(jnp.int32), shards, *extra)


def _pass_to_sibling(lands, *, name):
    n = len(lands)

    def body(*refs):
        outs = refs[n:2 * n]
        send, recv = refs[2 * n:]
        x, y, c = _coords()
        cps = []
        for a in range(n):
            for k, (px, py) in enumerate(_peer_chips(x, y)):
                slot = 4 * px + 2 * py + c
                cps.append(pltpu.make_async_remote_copy(
                    src_ref=outs[a].at[slot], dst_ref=outs[a].at[slot], send_sem=send.at[a, k], recv_sem=recv.at[a, k],
                    device_id=(x, y, 1 - c), device_id_type=MESH))
        for cp in cps:
            cp.start()
        for cp in cps:
            cp.wait_recv()
        for cp in cps:
            cp.wait_send()

    return pl.pallas_call(
        body, name=name, in_specs=[_ANY] * n, out_specs=[_ANY] * n,
        out_shape=[jax.ShapeDtypeStruct(b.shape, b.dtype) for b in lands],
        input_output_aliases={a: a for a in range(n)},
        scratch_shapes=[pltpu.SemaphoreType.DMA((n, N_PEER_CHIPS)), pltpu.SemaphoreType.DMA((n, N_PEER_CHIPS))],
    )(*lands)


def _sum_parts(parts, got, *, name):
    _, rows, cols = parts.shape
    tc = _col_tile(4, rows, cols, itemsize=parts.dtype.itemsize)
    x, y, c = _coords()
    idx = jnp.stack([2 * x + y, 2 * (1 - x) + y, 2 * x + (1 - y), 2 * (1 - x) + (1 - y), c]).astype(jnp.int32)

    def body(i_ref, p_ref, g0_ref, g1_ref, g2_ref, o_ref):
        o_ref[...] = ((p_ref[...].astype(F32) + g0_ref[...].astype(F32)) + g1_ref[...].astype(F32)) + g2_ref[...].astype(F32)

    slot = lambda s: pl.BlockSpec((None, rows, tc), lambda i, ir, s=s: (ir[s], 0, i))
    return pl.pallas_call(
        body, name=name,
        grid_spec=pltpu.PrefetchScalarGridSpec(
            num_scalar_prefetch=1, grid=(cols // tc,),
            in_specs=[slot(0), slot(1), slot(2), slot(3)],
            out_specs=pl.BlockSpec((None, rows, tc), lambda i, ir: (ir[4], 0, i))),
        out_shape=jax.ShapeDtypeStruct((2, rows, cols), F32),
        compiler_params=_cp(("parallel",)),
    )(idx, parts, got, got, got)


def _sum_slots(t, *, name):
    S, rows, cols = t.shape
    tc = _col_tile(S, rows, cols)

    def body(t_ref, o_ref):
        acc = t_ref[0].astype(F32)
        for s in range(1, S):
            acc = acc + t_ref[s].astype(F32)
        o_ref[...] = acc

    return pl.pallas_call(
        body, name=name, grid=(cols // tc,),
        in_specs=[pl.BlockSpec((S, rows, tc), lambda i: (0, 0, i))],
        out_specs=pl.BlockSpec((rows, tc), lambda i: (0, i)),
        out_shape=jax.ShapeDtypeStruct((rows, cols), F32),
        compiler_params=_cp(("parallel",)),
    )(t)


def _halves_join(bufs, *, name):
    n = len(bufs)

    def body(*refs):
        outs = refs[n:2 * n]
        send, recv = refs[2 * n:]
        x, y, c = _coords()
        cps = [pltpu.make_async_remote_copy(src_ref=outs[a].at[c], dst_ref=outs[a].at[c], send_sem=send.at[a],
                                            recv_sem=recv.at[a], device_id=(x, y, 1 - c), device_id_type=MESH)
               for a in range(n)]
        for cp in cps:
            cp.start()
        for cp in cps:
            cp.wait_recv()
        for cp in cps:
            cp.wait_send()

    return pl.pallas_call(
        body, name=name, in_specs=[_ANY] * n, out_specs=[_ANY] * n,
        out_shape=[jax.ShapeDtypeStruct(b.shape, b.dtype) for b in bufs],
        input_output_aliases={a: a for a in range(n)},
        scratch_shapes=[pltpu.SemaphoreType.DMA((n,)), pltpu.SemaphoreType.DMA((n,))],
    )(*bufs)


def _swap_start(srcs, *, name):
    n = len(srcs)
    lands = [lax.empty((s.shape[0],) + s.shape[2:], s.dtype) for s in srcs]

    def body(*refs):
        src_refs, land_refs = refs[:n], refs[n:2 * n]
        send, recv, token = refs[2 * n:3 * n], refs[3 * n:4 * n], refs[-1]
        x, y, c = _coords()
        for a in range(n):
            pltpu.make_async_remote_copy(src_ref=src_refs[a].at[:, 1 - c], dst_ref=land_refs[a], send_sem=send[a],
                                         recv_sem=recv[a], device_id=(x, y, 1 - c), device_id_type=MESH).start()
        token[...] = jnp.zeros_like(token)

    bufs = list(srcs) + lands
    res = pl.pallas_call(
        body, name=name,
        out_shape=(*[pltpu.SemaphoreType.DMA(())] * (2 * n), *[pltpu.HBM(b.shape, b.dtype) for b in bufs],
                   jax.ShapeDtypeStruct((8, 128), F32)),
        in_specs=[_HBM] * (2 * n),
        out_specs=(*[_SEM] * (2 * n), *[_HBM] * (2 * n), pl.BlockSpec(memory_space=pltpu.VMEM)),
        input_output_aliases={i: 2 * n + i for i in range(2 * n)},
        compiler_params=pltpu.CompilerParams(has_side_effects=_EFFECT),
    )(*[pltpu.with_memory_space_constraint(b, pltpu.HBM) for b in bufs])
    return list(res[:2 * n]), list(res[2 * n:3 * n]), list(res[3 * n:4 * n]), res[-1]


def _swap_wait(sems, srcs, lands, after, *, name):
    n = len(srcs)

    def body(*refs):
        src_refs, land_refs = refs[:n], refs[n:2 * n]
        send, recv = refs[2 * n:3 * n], refs[3 * n:4 * n]
        x, y, c = _coords()
        for a in range(n):
            cp = pltpu.make_async_remote_copy(
                src_ref=src_refs[a].at[:, 0], dst_ref=land_refs[a], send_sem=send[a], recv_sem=recv[a],
                device_id=(x, y, 1 - c), device_id_type=MESH)
            cp.wait_send()
            cp.wait_recv()

    bufs = list(srcs) + list(lands)
    res = pl.pallas_call(
        body, name=name,
        out_shape=tuple(pltpu.HBM(b.shape, b.dtype) for b in bufs),
        in_specs=[_HBM] * (2 * n) + [_SEM] * (2 * n) + [_ANY] * len(after),
        out_specs=tuple([_HBM] * (2 * n)),
        input_output_aliases={i: i for i in range(2 * n)},
        compiler_params=pltpu.CompilerParams(has_side_effects=_EFFECT),
    )(*bufs, *sems, *after)
    return list(res[:n]), list(res[n:])


def _grad_views(grads):
    return [g.reshape(4, 2, g.shape[0] // 8, g.shape[1]) for g in grads]


def _scatter_begin(views, theirs, *, tag):
    parts = [_pair_sum(v, t, name=f"rs_pair_{tag}_{i}") for i, (v, t) in enumerate(zip(views, theirs))]
    got = [lax.empty(p.shape, p.dtype) for p in parts]
    sems, parts, got, token = _split_start(
        parts, got, lambda x, y, c, px, py: 2 * px + py, lambda x, y, c: 2 * x + y, name=f"rs_start_{tag}")
    return (sems, parts, got), token


def _reduce_scatter_begin(grads, *, tag):
    views = _grad_views(grads)
    theirs = _sibling_swap(views, name=f"rs_swap_{tag}")
    return _scatter_begin(views, theirs, tag=tag)


def _reduce_scatter_sums(state, after, *, tag):
    sems, parts, got = state
    parts, got = _split_wait(sems, parts, got, after, name=f"rs_wait_{tag}")
    return [_sum_parts(p, t, name=f"rs_sum_{tag}_{i}") for i, (p, t) in enumerate(zip(parts, got))]


def _reduce_scatter_join(halves, *, tag):
    joined = _halves_join(halves, name=f"rs_join_{tag}")
    return [j.reshape(2 * j.shape[1], j.shape[2]) for j in joined]


_SMALL = ("norm_w", "ssm_a_re", "ssm_a_im", "ssm_log_dt", "ssm_b_re", "ssm_b_im", "ssm_c_re", "ssm_c_im", "ssm_d",
          "ssm_glu_b", "sg_ln_w", "sg_ln_b", "sg_w", "sg_b", "attn_sinks", "final_norm_w")
_BIG = ("w_in", "ssm_glu_w", "w_branch_a", "w_branch_b", "w_branch_c", "w_out")
_WEIGHTS = ("norm_w", "w_in", "ssm_a_re", "ssm_a_im", "ssm_log_dt", "ssm_b_re", "ssm_b_im", "ssm_c_re", "ssm_c_im",
            "ssm_d", "ssm_glu_w", "ssm_glu_b", "sg_ln_w", "sg_ln_b", "sg_w", "sg_b", "attn_sinks", "w_branch_a",
            "w_branch_b", "w_branch_c", "w_out", "final_norm_w")
_PACK_COLS = 1024
_PACK_ALIGN = 8 * ROW_ALIGN * _PACK_COLS


N_PEERS = 7


def _peer(x, y, c, k):
    return x ^ (k >> 2), y ^ ((k >> 1) & 1), c ^ (k & 1)


def _own_slice(buf, *, name):
    _, rows, cols = buf.shape
    x, y, c = _coords()

    def body(me_ref, s_ref, o_ref):
        o_ref[...] = s_ref[...]

    spec = pl.BlockSpec((None, rows, cols), lambda i, mr: (mr[0], 0, 0))
    return pl.pallas_call(
        body, name=name,
        grid_spec=pltpu.PrefetchScalarGridSpec(num_scalar_prefetch=1, grid=(1,), in_specs=[spec], out_specs=spec),
        out_shape=jax.ShapeDtypeStruct(buf.shape, buf.dtype),
        compiler_params=_cp(("arbitrary",)),
    )((4 * x + 2 * y + c).reshape(1).astype(jnp.int32), buf)


def _slice_exchange_start(buf, land, *, name):
    def body(in_ref, land_ref, *rest):
        send, recv, token = rest[:N_PEERS], rest[N_PEERS:2 * N_PEERS], rest[-1]
        x, y, c = _coords()
        me = 4 * x + 2 * y + c
        for k in range(1, N_PEERS + 1):
            px, py, pc = _peer(x, y, c, k)
            pltpu.make_async_remote_copy(
                src_ref=in_ref.at[4 * px + 2 * py + pc], dst_ref=land_ref.at[me], send_sem=send[k - 1],
                recv_sem=recv[k - 1], device_id=(px, py, pc), device_id_type=MESH).start()
        token[...] = jnp.zeros_like(token)

    res = pl.pallas_call(
        body, name=name,
        out_shape=(*[pltpu.SemaphoreType.DMA(())] * (2 * N_PEERS), pltpu.HBM(buf.shape, buf.dtype),
                   pltpu.HBM(land.shape, land.dtype), jax.ShapeDtypeStruct((8, 128), F32)),
        in_specs=[_HBM, _HBM],
        out_specs=(*[_SEM] * (2 * N_PEERS), _HBM, _HBM, pl.BlockSpec(memory_space=pltpu.VMEM)),
        input_output_aliases={0: 2 * N_PEERS, 1: 2 * N_PEERS + 1},
        compiler_params=pltpu.CompilerParams(has_side_effects=_EFFECT),
    )(pltpu.with_memory_space_constraint(buf, pltpu.HBM), pltpu.with_memory_space_constraint(land, pltpu.HBM))
    return list(res[:2 * N_PEERS]), res[2 * N_PEERS], res[2 * N_PEERS + 1], res[-1]


def _slice_exchange_wait(sems, buf, land, after, *, name):
    def body(in_ref, land_ref, *rest):
        send, recv = rest[:N_PEERS], rest[N_PEERS:2 * N_PEERS]
        x, y, c = _coords()
        for k in range(N_PEERS):
            cp = pltpu.make_async_remote_copy(
                src_ref=in_ref.at[0], dst_ref=land_ref.at[0], send_sem=send[k], recv_sem=recv[k],
                device_id=(x, y, 1 - c), device_id_type=MESH)
            cp.wait_send()
            cp.wait_recv()

    res = pl.pallas_call(
        body, name=name,
        out_shape=(pltpu.HBM(buf.shape, buf.dtype), pltpu.HBM(land.shape, land.dtype)),
        in_specs=[_HBM, _HBM] + [_SEM] * (2 * N_PEERS) + [_ANY] * len(after),
        out_specs=(_HBM, _HBM),
        input_output_aliases={0: 0, 1: 1},
        compiler_params=pltpu.CompilerParams(has_side_effects=_EFFECT),
    )(buf, land, *sems, *after)
    return res[1]


def _allreduce_small_begin(packed):
    rows, cols = packed.shape
    buf = packed.reshape(8, rows // 8, cols)
    sems, buf, land, token = _slice_exchange_start(buf, _own_slice(buf, name="small_grads_own"),
                                                   name="small_grads_start")
    return (sems, buf, land), token


def _allreduce_small_end(state, after):
    got = _slice_exchange_wait(*state, list(after), name="small_grads_wait")
    mine = _sum_slots(got, name="small_grads_sum")
    return _gather8([mine], name="small_grads_gather")[0]


def _pack(ts):
    flat = jnp.concatenate([t.reshape(-1) for t in ts])
    pad = (-flat.shape[0]) % _PACK_ALIGN
    return jnp.pad(flat, (0, pad)).reshape(-1, _PACK_COLS)


def _unpack(buf, like):
    flat = buf.reshape(-1)
    out, pos = [], 0
    for t in like:
        out.append(flat[pos:pos + t.size].reshape(t.shape))
        pos += t.size
    return out


def kernel(x, norm_w, w_in, ssm_a_re, ssm_a_im, ssm_log_dt, ssm_b_re, ssm_b_im, ssm_c_re, ssm_c_im, ssm_d, ssm_glu_w, ssm_glu_b, sg_ln_w, sg_ln_b, sg_w, sg_b, attn_sinks, w_branch_a, w_branch_b, w_branch_c, w_out, final_norm_w, loss_target, m_norm_w, m_w_in, m_ssm_a_re, m_ssm_a_im, m_ssm_log_dt, m_ssm_b_re, m_ssm_b_im, m_ssm_c_re, m_ssm_c_im, m_ssm_d, m_ssm_glu_w, m_ssm_glu_b, m_sg_ln_w, m_sg_ln_b, m_sg_w, m_sg_b, m_attn_sinks, m_w_branch_a, m_w_branch_b, m_w_branch_c, m_w_out, m_final_norm_w, v_norm_w, v_w_in, v_ssm_a_re, v_ssm_a_im, v_ssm_log_dt, v_ssm_b_re, v_ssm_b_im, v_ssm_c_re, v_ssm_c_im, v_ssm_d, v_ssm_glu_w, v_ssm_glu_b, v_sg_ln_w, v_sg_ln_b, v_sg_w, v_sg_b, v_attn_sinks, v_w_branch_a, v_w_branch_b, v_w_branch_c, v_w_out, v_final_norm_w):
    w = dict(norm_w=norm_w, w_in=w_in, ssm_a_re=ssm_a_re, ssm_a_im=ssm_a_im, ssm_log_dt=ssm_log_dt, ssm_b_re=ssm_b_re,
             ssm_b_im=ssm_b_im, ssm_c_re=ssm_c_re, ssm_c_im=ssm_c_im, ssm_d=ssm_d, ssm_glu_w=ssm_glu_w,
             ssm_glu_b=ssm_glu_b, sg_ln_w=sg_ln_w, sg_ln_b=sg_ln_b, sg_w=sg_w, sg_b=sg_b, attn_sinks=attn_sinks,
             w_branch_a=w_branch_a, w_branch_b=w_branch_b, w_branch_c=w_branch_c, w_out=w_out,
             final_norm_w=final_norm_w)
    m = dict(norm_w=m_norm_w, w_in=m_w_in, ssm_a_re=m_ssm_a_re, ssm_a_im=m_ssm_a_im, ssm_log_dt=m_ssm_log_dt,
             ssm_b_re=m_ssm_b_re, ssm_b_im=m_ssm_b_im, ssm_c_re=m_ssm_c_re, ssm_c_im=m_ssm_c_im, ssm_d=m_ssm_d,
             ssm_glu_w=m_ssm_glu_w, ssm_glu_b=m_ssm_glu_b, sg_ln_w=m_sg_ln_w, sg_ln_b=m_sg_ln_b, sg_w=m_sg_w,
             sg_b=m_sg_b, attn_sinks=m_attn_sinks, w_branch_a=m_w_branch_a, w_branch_b=m_w_branch_b,
             w_branch_c=m_w_branch_c, w_out=m_w_out, final_norm_w=m_final_norm_w)
    v = dict(norm_w=v_norm_w, w_in=v_w_in, ssm_a_re=v_ssm_a_re, ssm_a_im=v_ssm_a_im, ssm_log_dt=v_ssm_log_dt,
             ssm_b_re=v_ssm_b_re, ssm_b_im=v_ssm_b_im, ssm_c_re=v_ssm_c_re, ssm_c_im=v_ssm_c_im, ssm_d=v_ssm_d,
             ssm_glu_w=v_ssm_glu_w, ssm_glu_b=v_ssm_glu_b, sg_ln_w=v_sg_ln_w, sg_ln_b=v_sg_ln_b, sg_w=v_sg_w,
             sg_b=v_sg_b, attn_sinks=v_attn_sinks, w_branch_a=v_w_branch_a, w_branch_b=v_w_branch_b,
             w_branch_c=v_w_branch_c, w_out=v_w_out, final_norm_w=v_final_norm_w)

    big_names = ("winT", "glu_w", "wbaT", "wbbT", "wbcT", "w_out")
    L = x.shape[1]
    tabs = _rope_tables(L)
    p = {k: w[k] for k in _SMALL}

    column_sharded = ("w_in", "w_branch_a", "w_branch_b", "w_branch_c")

    row_shards = {k: w[k].transpose(0, 2, 1) if k in column_sharded else w[k] for k in _BIG}
    rows_of = lambda lands: [t.reshape(8 * t.shape[1], t.shape[2]) for t in lands]
    saved = [None] * DEPTH

    land_a = [_fill_own(row_shards["w_in"], 0, name="gather_fill_0_0")]
    sems_a, land_a, token_a = _gather_start(land_a, name="gather_start_0a")
    fill_after = lambda l, names, i0: [_fill_own(row_shards[k], l, name=f"gather_fill_{l}_{i0 + i}", after=token_a)
                                       for i, k in enumerate(names)]
    lands = [[None] + fill_after(0, _BIG[1:], 1), fill_after(1, _BIG, 0)]
    sp = [_prep_layer(p, l, after=token_a) for l in range(DEPTH)]
    h0 = _rms_fwd(x[0], p["norm_w"][0], name="rms_fwd_0", after=token_a)
    land_a = _gather_wait(sems_a, land_a, [h0, sp[0]["btr"], sp[1]["btr"]] + lands[0][1:] + lands[1],
                          name="gather_wait_0a")
    land_a = _pass_to_sibling(land_a, name="gather_pass_0a")
    sems_b, land_b, token_b = _gather_start(lands[0][1:], name="gather_start_0b", after=land_a)
    split1 = {}

    def start1(main):
        split1["sems"], split1["land"], token1 = _gather_start(lands[1][:1], name="gather_start_1a", after=[main])
        return token1

    def rest(l, sems, land, first):
        def arrived(t):
            got = _pass_to_sibling(_gather_wait(sems, land, t, name=f"gather_wait_{l}b"), name=f"gather_pass_{l}b")
            return dict(zip(big_names, rows_of(first + got))), None
        return arrived

    x1, saved[0], big0 = _layer_fwd(x[0], h0, p, sp[0], rows_of(land_a)[0], rest(0, sems_b, land_b, land_a), 0,
                                    tabs, proj_after=token_b, after_main=start1)
    land_1a = _gather_wait(split1["sems"], split1["land"], [x1], name="gather_wait_1a")
    land_1a = _pass_to_sibling(land_1a, name="gather_pass_1a")
    sems_1b, land_1b, token_1b = _gather_start(lands[1][1:], name="gather_start_1b", after=land_1a)
    h1 = _rms_fwd(x1, p["norm_w"][1], name="rms_fwd_1")
    x2, saved[1], big1 = _layer_fwd(x1, h1, p, sp[1], rows_of(land_1a)[0], rest(1, sems_1b, land_1b, land_1a),
                                    1, tabs, proj_after=token_1b)
    bigs = [big0, big1]
    loss, dx, dfw = _final_loss(x2, p["final_norm_w"], loss_target[0], name="final_loss")

    grads = [None] * DEPTH
    rs = {}

    def early(l):
        def begin(g):
            *rs[f"{l}a"], token_a = _swap_start(_grad_views([g[k] for k in big_names[1:]]), name=f"rs_swap_start_{l}a")
            return token_a
        return begin

    def mid(l):
        def go_on(t):
            sems, views, lands = rs[f"{l}a"]
            views, theirs = _swap_wait(sems, views, lands, [t], name=f"rs_swap_wait_{l}a")
            rs[f"{l}a"], token_a = _scatter_begin(views, theirs, tag=f"{l}a")
            return token_a
        return go_on

    def late(l, dproj, dx):
        if l == 0:
            rs["0b"], token_s = _reduce_scatter_begin([grads[0]["winT"]], tag="0b")
            dh = _proj_bwd_dh(dproj, bigs[0]["winT"], 0, token_s)
            return _rms_bwd(saved[0]["x"], p["norm_w"][0], dh, dx, name="rms_bwd_0")
        sems, views, lands, token_b = _swap_start(_grad_views([grads[l]["winT"]]), name=f"rs_swap_start_{l}b")
        dh = _proj_bwd_dh(dproj, bigs[l]["winT"], l, token_b)
        views, theirs = _swap_wait(sems, views, lands, [dh], name=f"rs_swap_wait_{l}b")
        rs[f"{l}b"], token_s = _scatter_begin(views, theirs, tag=f"{l}b")
        return _rms_bwd(saved[l]["x"], p["norm_w"][l], dh, dx, name=f"rms_bwd_{l}", after=token_s)

    def reduced(l, after):
        halves = (_reduce_scatter_sums(rs[f"{l}b"], after, tag=f"{l}b")
                  + _reduce_scatter_sums(rs[f"{l}a"], after, tag=f"{l}a"))
        return _reduce_scatter_join(halves, tag=str(l))

    dproj, grads[1] = _layer_bwd(dx, saved[1], p, bigs[1], 1, tabs, early(1), mid(1))
    dx, grads[1]["norm_w"] = late(1, dproj, dx)
    dproj, grads[0] = _layer_bwd(dx, saved[0], p, bigs[0], 0, tabs, early(0), mid(0))
    dx, grads[0]["norm_w"] = late(0, dproj, dx)
    small_like = [w[k] for k in _SMALL]
    gs = [jnp.stack([grads[l][k] for l in range(DEPTH)]) if k != "final_norm_w" else dfw for k in _SMALL]
    small, token_small = _allreduce_small_begin(_pack(gs + [loss.reshape(1)]))
    red1 = reduced(1, [dx, token_small])

    tr = lambda t: t.transpose(0, 2, 1)
    view = {k: (tr if k == "w_in" else (lambda t: t)) for k in _BIG}
    shard_grads = lambda red: dict(zip(_BIG, (red[0], red[1], red[2].T, red[3].T, red[4].T, red[5])))
    outs = {k: None for k in _BIG}

    def adamw_big(l, red):
        for k, g in shard_grads(red).items():
            outs[k] = _adamw_layer(view[k](w[k]), g, view[k](m[k]), view[k](v[k]), l, outs[k], name=f"adamw_{k}_{l}")

    adamw_big(1, red1)

    gsum = _allreduce_small_end(small, [outs[k][0] for k in _BIG])
    adamw_big(0, reduced(0, [gsum]))

    gfull, delta, new_m, new_v = {}, {}, {}, {}
    for k in _BIG:
        gfull[k], delta[k], new_m[k], new_v[k] = (view[k](t) for t in outs[k])
    *small_sums, loss = _unpack(gsum, small_like + [loss])
    for k, t in zip(_SMALL, small_sums):
        gfull[k] = t
        delta[k], new_m[k], new_v[k] = _adamw(w[k], t, m[k], v[k], name=f"adamw_{k}")

    return (loss, dx[None], *[gfull[k] for k in _WEIGHTS], *[delta[k] for k in _WEIGHTS],
            *[new_m[k] for k in _WEIGHTS], *[new_v[k] for k in _WEIGHTS])
```

```python
import functools
import math

import numpy as np
import jax
import jax.numpy as jnp
from jax import lax
from jax.experimental import pallas as pl
from jax.experimental.pallas import tpu as pltpu

F32 = jnp.float32
MXU = jnp.bfloat16
HIGHEST = lax.Precision.HIGHEST

D_MODEL = 2048
DEPTH = 2
EPS = 1e-6
NEG_INF = -1e30
SSM_WIDTH = 1024
SSM_GROUP = 16
SSM_GROUPS = 64
SSM_STATE = 64
SSM_CH = SSM_GROUPS * SSM_STATE
SLAB = 128
SLAB_CH = (SLAB // SSM_GROUP) * SSM_STATE
N_SLAB = SSM_WIDTH // SLAB
SCAN_SEG = 8
SCAN_STEPS = 4
FIXUP_ROWS = 64
SG_HEADS = 8
SG_CHUNK = 128
HEAD_DIM = 64
ATT_HEADS = 16
ATT_KV_HEADS = 2
GQA_GROUP = 8
ATT_BLOCK = 128
WINDOW = 128
ROT_DIM = 16
ROPE_THETA = 500000.0
N_MAIN = 6400
N_ZC = 1024
N_GATES = 6144
D_IN = N_MAIN + N_ZC + N_GATES

ADAM_LR = 0.001
ADAM_B1 = 0.9
ADAM_B2 = 0.999
ADAM_EPS = 1e-08
ADAM_WD = 0.01
ADAM_STEP = 10

_DIMS = {"nn": (((1,), (0,)), ((), ())), "nt": (((1,), (1,)), ((), ())), "tn": (((0,), (0,)), ((), ()))}
_MB = 1024 * 1024


def _cp(sem, vmem_mb=48):
    return pltpu.CompilerParams(dimension_semantics=sem, vmem_limit_bytes=vmem_mb * _MB)


def _dot(a, b, mode):
    return lax.dot_general(a.astype(MXU), b.astype(MXU), _DIMS[mode], preferred_element_type=F32)


@jax.custom_vjp
def _mm_nn(a, b):
    return _dot(a, b, "nn")


def _mm_nn_fwd(a, b):
    return _dot(a, b, "nn"), (a, b)


def _mm_nn_bwd(res, g):
    a, b = res
    return _dot(g, b, "nt"), _dot(a, g, "tn")


_mm_nn.defvjp(_mm_nn_fwd, _mm_nn_bwd)


@jax.custom_vjp
def _mm_nt(a, bt):
    return _dot(a, bt, "nt")


def _mm_nt_fwd(a, bt):
    return _dot(a, bt, "nt"), (a, bt)


def _mm_nt_bwd(res, g):
    a, bt = res
    return _dot(g, bt, "nn"), _dot(g, a, "tn")


_mm_nt.defvjp(_mm_nt_fwd, _mm_nt_bwd)


def _rmsnorm(x, w):
    return x * lax.rsqrt(jnp.mean(x * x, axis=-1, keepdims=True) + EPS) * w


def _layernorm(x, w, b):
    mu = jnp.mean(x, axis=-1, keepdims=True)
    var = jnp.mean(jnp.square(x - mu), axis=-1, keepdims=True)
    return (x - mu) * lax.rsqrt(var + EPS) * w + b


def _silu(x):
    return x * jax.nn.sigmoid(x)


def _matmul(a, b, mode, *, name, shape, tm, tn, tk, out_dtype=F32, add=None, a_off=(0, 0), b_off=(0, 0), after=None,
            vmem_mb=48):
    m, n, k = shape
    tm, tn, tk = min(tm, m), min(tn, n), min(tk, k)
    assert m % tm == 0 and n % tn == 0 and k % tk == 0, (name, shape, tm, tn, tk)
    nk = k // tk
    has_add, has_after = add is not None, after is not None

    def body(*refs):
        a_ref, b_ref = refs[0], refs[1]
        pos = 2
        add_ref = None
        if has_add:
            add_ref = refs[pos]
            pos += 1
        if has_after:
            pos += 1
        o_ref = refs[pos]
        p = _dot(a_ref[...], b_ref[...], mode)
        if nk == 1:
            if has_add:
                p = p + add_ref[...].astype(F32)
            o_ref[...] = p.astype(out_dtype)
            return
        acc_ref = refs[pos + 1]
        kk = pl.program_id(2)

        @pl.when(kk == 0)
        def _():
            acc_ref[...] = p

        @pl.when(kk > 0)
        def _():
            acc_ref[...] += p

        @pl.when(kk == nk - 1)
        def _():
            r = acc_ref[...]
            if has_add:
                r = r + add_ref[...].astype(F32)
            o_ref[...] = r.astype(out_dtype)

    a0, a1 = a_off
    b0, b1 = b_off
    if mode == "tn":
        a_spec = pl.BlockSpec((tk, tm), lambda i, j, kk: (kk + a0, i + a1))
    else:
        a_spec = pl.BlockSpec((tm, tk), lambda i, j, kk: (i + a0, kk + a1))
    if mode == "nt":
        b_spec = pl.BlockSpec((tn, tk), lambda i, j, kk: (j + b0, kk + b1))
    else:
        b_spec = pl.BlockSpec((tk, tn), lambda i, j, kk: (kk + b0, j + b1))
    in_specs = [a_spec, b_spec]
    args = [a, b]
    if has_add:
        in_specs.append(pl.BlockSpec((tm, tn), lambda i, j, kk: (i, j)))
        args.append(add)
    if has_after:
        in_specs.append(pl.BlockSpec(memory_space=pl.ANY))
        args.append(after)
    return pl.pallas_call(
        body, name=name, grid=(m // tm, n // tn, nk),
        in_specs=in_specs,
        out_specs=pl.BlockSpec((tm, tn), lambda i, j, kk: (i, j)),
        out_shape=jax.ShapeDtypeStruct((m, n), out_dtype),
        scratch_shapes=[pltpu.VMEM((tm, tn), F32)] if nk > 1 else [],
        compiler_params=_cp(("parallel", "parallel", "arbitrary"), vmem_mb),
    )(*args)


def _rms_fwd(x, w, *, name, tm=256, after=None):
    L, d = x.shape
    tm = min(tm, L)
    extra = [] if after is None else [after]

    def body(x_ref, w_ref, *rest):
        rest[-1][...] = _rmsnorm(x_ref[...], w_ref[...]).astype(MXU)

    return pl.pallas_call(
        body, name=name, grid=(L // tm,),
        in_specs=([pl.BlockSpec((tm, d), lambda i: (i, 0)), pl.BlockSpec((1, d), lambda i: (0, 0))]
                  + [pl.BlockSpec(memory_space=pl.ANY)] * len(extra)),
        out_specs=pl.BlockSpec((tm, d), lambda i: (i, 0)),
        out_shape=jax.ShapeDtypeStruct((L, d), MXU),
        compiler_params=_cp(("parallel",)),
    )(x, w.reshape(1, d), *extra)


def _rms_bwd(x, w, dh, dxn, *, name, tm=256, after=None):
    L, d = x.shape
    tm = min(tm, L)
    extra = [] if after is None else [after]

    def body(x_ref, w_ref, dh_ref, dxn_ref, *rest):
        dx_ref, dw_ref = rest[-2:]
        _, vjp = jax.vjp(_rmsnorm, x_ref[...], w_ref[...])
        dx, dw = vjp(dh_ref[...])
        dx_ref[...] = dx + dxn_ref[...]

        @pl.when(pl.program_id(0) == 0)
        def _():
            dw_ref[...] = jnp.zeros_like(dw_ref)

        dw_ref[...] += dw

    row = pl.BlockSpec((tm, d), lambda i: (i, 0))
    vec = pl.BlockSpec((1, d), lambda i: (0, 0))
    dx, dw = pl.pallas_call(
        body, name=name, grid=(L // tm,),
        in_specs=[row, vec, row, row] + [pl.BlockSpec(memory_space=pl.ANY)] * len(extra), out_specs=[row, vec],
        out_shape=[jax.ShapeDtypeStruct((L, d), F32), jax.ShapeDtypeStruct((1, d), F32)],
        compiler_params=_cp(("arbitrary",)),
    )(x, w.reshape(1, d), dh, dxn, *extra)
    return dx, dw.reshape(d)


def _final_loss(x, w, tgt, *, name, tm=256):
    L, d = x.shape
    tm = min(tm, L)

    def loss_fn(xv, wv, tv):
        err = jnp.square(_rmsnorm(xv, wv) - tv)
        return 0.5 * jnp.sum(jnp.mean(err, axis=-1, keepdims=True), axis=0, keepdims=True)

    def body(x_ref, w_ref, t_ref, loss_ref, dx_ref, dw_ref):
        tv = t_ref[...]
        val, vjp = jax.vjp(lambda xv, wv: loss_fn(xv, wv, tv), x_ref[...], w_ref[...])
        dx, dw = vjp(jnp.ones((1, 1), F32))
        dx_ref[...] = dx

        @pl.when(pl.program_id(0) == 0)
        def _():
            dw_ref[...] = jnp.zeros_like(dw_ref)
            loss_ref[...] = jnp.zeros_like(loss_ref)

        dw_ref[...] += dw
        loss_ref[...] += jnp.broadcast_to(val, loss_ref.shape)

    row = pl.BlockSpec((tm, d), lambda i: (i, 0))
    vec = pl.BlockSpec((1, d), lambda i: (0, 0))
    loss, dx, dw = pl.pallas_call(
        body, name=name, grid=(L // tm,),
        in_specs=[row, vec, row],
        out_specs=[pl.BlockSpec((8, 128), lambda i: (0, 0)), row, vec],
        out_shape=[jax.ShapeDtypeStruct((8, 128), F32), jax.ShapeDtypeStruct((L, d), F32),
                   jax.ShapeDtypeStruct((1, d), F32)],
        compiler_params=_cp(("arbitrary",)),
    )(x, w.reshape(1, d), tgt)
    return loss[0, 0], dx, dw.reshape(d)


PARAM_ROWS = 512


def _s5_param_fn(are, aim, ldt, bre, bim, row0):
    n = are.shape[0]
    grp = (row0 + lax.broadcasted_iota(jnp.int32, (n, SSM_GROUPS), 0)) // SSM_STATE
    col = lax.broadcasted_iota(jnp.int32, (n, SSM_GROUPS), 1)
    sel = (grp == col).astype(F32)
    dt = jnp.sum(sel * jnp.exp(ldt), axis=-1, keepdims=True)
    mag = jnp.exp(are * dt)
    ang = aim * dt
    lbr = mag * jnp.cos(ang)
    lbi = mag * jnp.sin(ang)
    den = are * are + aim * aim
    nr = lbr - 1.0
    kr = (nr * are + lbi * aim) / den
    ki = (lbi * are - nr * aim) / den
    return lbr, lbi, kr * bre - ki * bim, kr * bim + ki * bre


def _s5_param_specs():
    col = pl.BlockSpec((PARAM_ROWS, 1), lambda i: (i, 0))
    mat = pl.BlockSpec((PARAM_ROWS, SSM_GROUP), lambda i: (i, 0))
    vec = pl.BlockSpec((1, SSM_GROUPS), lambda i: (0, 0))
    return col, mat, vec


def _s5_params_fwd(are, aim, ldt, bre, bim, *, name, after=None):
    n = are.shape[0]
    col, mat, vec = _s5_param_specs()
    extra = [] if after is None else [after]

    def body(are_ref, aim_ref, ldt_ref, bre_ref, bim_ref, *rest):
        lbr_ref, lbi_ref, bbr_ref, bbi_ref = rest[-4:]
        row0 = pl.program_id(0) * PARAM_ROWS
        lbr, lbi, bbr, bbi = _s5_param_fn(are_ref[...], aim_ref[...], ldt_ref[...], bre_ref[...], bim_ref[...], row0)
        lbr_ref[...] = lbr
        lbi_ref[...] = lbi
        bbr_ref[...] = bbr
        bbi_ref[...] = bbi

    cshape = jax.ShapeDtypeStruct((n, 1), F32)
    mshape = jax.ShapeDtypeStruct((n, SSM_GROUP), F32)
    return pl.pallas_call(body, name=name, grid=(n // PARAM_ROWS,),
                          in_specs=[col, col, vec, mat, mat] + [pl.BlockSpec(memory_space=pl.ANY)] * len(extra),
                          out_specs=[col, col, mat, mat], out_shape=[cshape, cshape, mshape, mshape],
                          compiler_params=_cp(("parallel",)))(are, aim, ldt, bre, bim, *extra)


def _s5_params_bwd(are, aim, ldt, bre, bim, dlbr, dlbi, dbbr, dbbi, *, name):
    n = are.shape[0]
    col, mat, vec = _s5_param_specs()

    def body(are_ref, aim_ref, ldt_ref, bre_ref, bim_ref, g0, g1, g2, g3, o0, o1, o2, o3, o4):
        row0 = pl.program_id(0) * PARAM_ROWS
        _, vjp = jax.vjp(lambda a, b, c, d, e: _s5_param_fn(a, b, c, d, e, row0),
                         are_ref[...], aim_ref[...], ldt_ref[...], bre_ref[...], bim_ref[...])
        dare, daim, dldt, dbre, dbim = vjp((g0[...], g1[...], g2[...], g3[...]))
        o0[...] = dare
        o1[...] = daim
        o3[...] = dbre
        o4[...] = dbim

        @pl.when(pl.program_id(0) == 0)
        def _():
            o2[...] = jnp.zeros_like(o2)

        o2[...] += dldt

    cshape = jax.ShapeDtypeStruct((n, 1), F32)
    mshape = jax.ShapeDtypeStruct((n, SSM_GROUP), F32)
    return pl.pallas_call(body, name=name, grid=(n // PARAM_ROWS,),
                          in_specs=[col, col, vec, mat, mat, col, col, mat, mat],
                          out_specs=[col, col, vec, mat, mat],
                          out_shape=[cshape, cshape, jax.ShapeDtypeStruct((1, SSM_GROUPS), F32), mshape, mshape],
                          compiler_params=_cp(("arbitrary",)))(are, aim, ldt, bre, bim, dlbr, dlbi, dbbr, dbbi)


SLAB_NC = SLAB_CH // 128


def _s5_specs(L):
    slab = pl.BlockSpec((L, SLAB), lambda s: (0, s))
    wspec = pl.BlockSpec((SLAB_NC, 128, SLAB), lambda s: (s, 0, 0))
    lspec = pl.BlockSpec((SLAB_NC, 1, 128), lambda s: (s, 0, 0))
    sspec = pl.BlockSpec((SLAB_NC, L, 128), lambda s: (s, 0, 0))
    dspec = pl.BlockSpec((1, SLAB), lambda s: (0, s))
    return slab, wspec, lspec, sspec, dspec


def _scan_inplace(sr_ref, si_ref, lr, li, pr_ref, pi_ref, *, reverse):
    NC, L, W = sr_ref.shape
    S = SCAN_SEG
    T = L // S
    lr8 = [jnp.broadcast_to(lr[k], (S, W)) for k in range(NC)]
    li8 = [jnp.broadcast_to(li[k], (S, W)) for k in range(NC)]

    def tiles(first, count):
        return pl.ds(first * S, count * S)

    for k in range(NC):
        pr_ref[k, tiles(T - 1 if reverse else 0, 1), :] = lr8[k]
        pi_ref[k, tiles(T - 1 if reverse else 0, 1), :] = li8[k]
        n = 1
        while n < T:
            have0 = (T - n) if reverse else 0
            new0 = (T - 2 * n) if reverse else n
            top = tiles(T - n, 1) if reverse else tiles(n - 1, 1)
            ar, ai = pr_ref[k, top, :][None], pi_ref[k, top, :][None]
            m = min(n, FIXUP_ROWS // S)
            for t0 in range(0, n, m):
                hr = pr_ref[k, tiles(have0 + t0, m), :].reshape(m, S, W)
                hi = pi_ref[k, tiles(have0 + t0, m), :].reshape(m, S, W)
                pr_ref[k, tiles(new0 + t0, m), :] = (hr * ar - hi * ai).reshape(m * S, W)
                pi_ref[k, tiles(new0 + t0, m), :] = (hr * ai + hi * ar).reshape(m * S, W)
            n *= 2

    def step(i, carry):
        for u in range(SCAN_STEPS):
            jj = i * SCAN_STEPS + u
            rows = pl.ds(pl.multiple_of(((T - 1 - jj) if reverse else jj) * S, S), S)
            out = []
            for k in range(NC):
                sr, si = carry[k]
                nsr = lr8[k] * sr - li8[k] * si + sr_ref[k, rows, :]
                nsi = lr8[k] * si + li8[k] * sr + si_ref[k, rows, :]
                sr_ref[k, rows, :] = nsr
                si_ref[k, rows, :] = nsi
                out.append((nsr, nsi))
            carry = tuple(out)
        return carry

    zero = jnp.zeros((S, W), F32)
    ends = lax.fori_loop(0, T // SCAN_STEPS, step, tuple((zero, zero) for k in range(NC)))
    sub = lax.broadcasted_iota(jnp.int32, (S, W), 0)
    order = range(S - 1, -1, -1) if reverse else range(S)
    for k in range(NC):
        er, ei = ends[k]
        full = tiles(0 if reverse else T - 1, 1)
        ltr = pr_ref[k, full, :][0:1]
        lti = pi_ref[k, full, :][0:1]
        cr = jnp.zeros((1, W), F32)
        ci = jnp.zeros((1, W), F32)
        ctr = jnp.zeros((S, W), F32)
        cti = jnp.zeros((S, W), F32)
        for seg in order:
            ctr = jnp.where(sub == seg, cr, ctr)
            cti = jnp.where(sub == seg, ci, cti)
            cr, ci = (er[seg:seg + 1, :] + ltr * cr - lti * ci, ei[seg:seg + 1, :] + ltr * ci + lti * cr)
        chunk = min(L, FIXUP_ROWS)
        for r0 in range(0, L, chunk):
            rows = pl.ds(r0, chunk)
            pr = pr_ref[k, rows, :].reshape(chunk // S, S, W)
            pi = pi_ref[k, rows, :].reshape(chunk // S, S, W)
            sr_ref[k, rows, :] += (pr * ctr[None] - pi * cti[None]).reshape(chunk, W)
            si_ref[k, rows, :] += (pr * cti[None] + pi * ctr[None]).reshape(chunk, W)


def _time_interleave(a):
    L, W = a.shape
    return a.reshape(SCAN_SEG, L // SCAN_SEG, W).transpose(1, 0, 2).reshape(L, W)


def _time_deinterleave(a):
    L, W = a.shape
    return a.reshape(L // SCAN_SEG, SCAN_SEG, W).transpose(1, 0, 2).reshape(L, W)


def _s5_fwd(u, btr, bti, cbr, cbi, lbr, lbi, dvec, *, name, after=None):
    L = u.shape[0]
    extra = [] if after is None else [after]

    def body(u_ref, btr_ref, bti_ref, cbr_ref, cbi_ref, lr_ref, li_ref, d_ref, *rest):
        ys_ref, sr_ref, si_ref, pr_ref, pi_ref = rest[-5:]
        u = u_ref[...]
        for k in range(SLAB_NC):
            sr_ref[k] = _dot(u, btr_ref[k], "nt")
            si_ref[k] = _dot(u, bti_ref[k], "nt")
        _scan_inplace(sr_ref, si_ref, lr_ref[...], li_ref[...], pr_ref, pi_ref, reverse=False)
        ys = d_ref[...] * u
        for k in range(SLAB_NC):
            ys = ys + _dot(sr_ref[k], cbr_ref[k], "nn") - _dot(si_ref[k], cbi_ref[k], "nn")
        ys_ref[...] = ys

    slab, wspec, lspec, sspec, dspec = _s5_specs(L)
    sshape = jax.ShapeDtypeStruct((N_SLAB * SLAB_NC, L, 128), F32)
    return pl.pallas_call(
        body, name=name, grid=(N_SLAB,),
        in_specs=[slab, wspec, wspec, wspec, wspec, lspec, lspec, dspec] + [pl.BlockSpec(memory_space=pl.ANY)] * len(extra),
        out_specs=[slab, sspec, sspec],
        out_shape=[jax.ShapeDtypeStruct((L, SSM_WIDTH), F32), sshape, sshape],
        scratch_shapes=[pltpu.VMEM((SLAB_NC, L, 128), F32), pltpu.VMEM((SLAB_NC, L, 128), F32)],
        compiler_params=_cp(("parallel",), 56),
    )(u, btr, bti, cbr, cbi, lbr, lbi, dvec, *extra)


def _s5_bwd(dys, u, sr, si, btr, bti, cbr, cbi, lbr, lbi, dvec, *, name, after=None):
    L = u.shape[0]
    S = SCAN_SEG
    extra = [] if after is None else [after]

    def body(dys_ref, u_ref, sr_ref, si_ref, btr_ref, bti_ref, cbr_ref, cbi_ref, lr_ref, li_ref, d_ref, *rest):
        (du_ref, dbtr_ref, dbti_ref, dcbr_ref, dcbi_ref, dlr_ref, dli_ref, dd_ref,
         ar_ref, ai_ref, pr_ref, pi_ref) = rest[-12:]
        dys = dys_ref[...]
        u = u_ref[...]
        for k in range(SLAB_NC):
            ar_ref[k] = _dot(dys, cbr_ref[k], "nt")
            ai_ref[k] = -_dot(dys, cbi_ref[k], "nt")
        _scan_inplace(ar_ref, ai_ref, lr_ref[...], -li_ref[...], pr_ref, pi_ref, reverse=True)
        head = lax.broadcasted_iota(jnp.int32, (L, 1), 0) < S
        sub0 = lax.broadcasted_iota(jnp.int32, (S, 1), 0) == 0

        def prev_state(s):
            up = pltpu.roll(s, S, 0)
            return jnp.where(head, 0.0, up), jnp.where(sub0, 0.0, pltpu.roll(up[0:S], 1, 0))

        du = d_ref[...] * dys
        for k in range(SLAB_NC):
            a_re = ar_ref[k]
            a_im = ai_ref[k]
            du = du + _dot(a_re, btr_ref[k], "nn") + _dot(a_im, bti_ref[k], "nn")
            dbtr_ref[k] = _dot(a_re, u, "tn")
            dbti_ref[k] = _dot(a_im, u, "tn")
            s_re = sr_ref[k]
            s_im = si_ref[k]
            dcbr_ref[k] = _dot(s_re, dys, "tn")
            dcbi_ref[k] = -_dot(s_im, dys, "tn")
            p_re, q_re = prev_state(s_re)
            p_im, q_im = prev_state(s_im)
            b_re, b_im = a_re[0:S], a_im[0:S]
            dlr_ref[k] = (jnp.sum(p_re * a_re + p_im * a_im, axis=0, keepdims=True)
                          + jnp.sum(q_re * b_re + q_im * b_im, axis=0, keepdims=True))
            dli_ref[k] = (jnp.sum(p_re * a_im - p_im * a_re, axis=0, keepdims=True)
                          + jnp.sum(q_re * b_im - q_im * b_re, axis=0, keepdims=True))
        du_ref[...] = du
        dd_ref[...] = jnp.sum(dys * u, axis=0, keepdims=True)

    slab, wspec, lspec, sspec, dspec = _s5_specs(L)
    wshape = jax.ShapeDtypeStruct((N_SLAB * SLAB_NC, 128, SLAB), F32)
    lshape = jax.ShapeDtypeStruct((N_SLAB * SLAB_NC, 1, 128), F32)
    return pl.pallas_call(
        body, name=name, grid=(N_SLAB,),
        in_specs=([slab, slab, sspec, sspec, wspec, wspec, wspec, wspec, lspec, lspec, dspec]
                  + [pl.BlockSpec(memory_space=pl.ANY)] * len(extra)),
        out_specs=[slab, wspec, wspec, wspec, wspec, lspec, lspec, dspec],
        out_shape=[jax.ShapeDtypeStruct((L, SSM_WIDTH), F32), wshape, wshape, wshape, wshape, lshape, lshape,
                   jax.ShapeDtypeStruct((1, SSM_WIDTH), F32)],
        scratch_shapes=[pltpu.VMEM((SLAB_NC, L, 128), F32)] * 4,
        compiler_params=_cp(("parallel",), 56),
    )(dys, u, sr, si, btr, bti, cbr, cbi, lbr, lbi, dvec, *extra)


_SLAB_MASK = (np.arange(SLAB_CH)[:, None] // SSM_STATE == np.arange(SLAB)[None, :] // SSM_GROUP)


def _expand_bd(x):
    t = jnp.tile(x.reshape(N_SLAB, SLAB_CH, SSM_GROUP), (1, 1, SLAB // SSM_GROUP))
    return jnp.where(_SLAB_MASK[None], t, 0.0).astype(MXU).reshape(N_SLAB * SLAB_NC, 128, SLAB)


def _contract_bd(dx):
    t = jnp.where(_SLAB_MASK[None], dx.reshape(N_SLAB, SLAB_CH, SLAB), 0.0)
    return jnp.sum(t.reshape(N_SLAB, SLAB_CH, SLAB // SSM_GROUP, SSM_GROUP), axis=2).reshape(SSM_CH, SSM_GROUP)


def _glu_ew(ys, zlin, za):
    a1 = jax.nn.gelu(ys)
    return a1 * jax.nn.sigmoid(zlin) * _silu(za)


def _glu_fwd(ys, main, gw, gb, *, name, tm=256):
    L = ys.shape[0]
    tm = min(tm, L)
    W = SSM_WIDTH

    def body(ys_ref, za_ref, gw_ref, gb_ref, ya_ref):
        ys = ys_ref[...]
        a1 = jax.nn.gelu(ys)
        zlin = _dot(a1, gw_ref[...], "nn") + gb_ref[...]
        ya_ref[...] = _glu_ew(ys, zlin, za_ref[...]).astype(MXU)

    return pl.pallas_call(
        body, name=name, grid=(L // tm,),
        in_specs=[pl.BlockSpec((tm, W), lambda i: (i, 0)), pl.BlockSpec((tm, W), lambda i: (i, 1)),
                  pl.BlockSpec((W, W), lambda i: (0, 0)), pl.BlockSpec((1, W), lambda i: (0, 0))],
        out_specs=pl.BlockSpec((tm, W), lambda i: (i, 0)),
        out_shape=jax.ShapeDtypeStruct((L, W), MXU),
        compiler_params=_cp(("parallel",)),
    )(ys, main, gw, gb.reshape(1, W))


def _glu_bwd(dya, ys, main, gw, gb, *, name, tm=256):
    L = ys.shape[0]
    tm = min(tm, L)
    W = SSM_WIDTH

    def body(dya_ref, ys_ref, za_ref, gw_ref, gb_ref, dys_ref, dza_ref, a1_ref, dzl_ref, db_ref):
        ys = ys_ref[...]
        a1, gelu_vjp = jax.vjp(jax.nn.gelu, ys)
        zlin = _dot(a1, gw_ref[...], "nn") + gb_ref[...]
        _, vjp = jax.vjp(lambda a, z, za: a * jax.nn.sigmoid(z) * _silu(za), a1, zlin, za_ref[...])
        da1, dzlin, dza = vjp(dya_ref[...].astype(F32))
        da1 = da1 + _dot(dzlin, gw_ref[...], "nt")
        dys_ref[...] = gelu_vjp(da1)[0]
        dza_ref[...] = dza
        a1_ref[...] = a1.astype(MXU)
        dzl_ref[...] = dzlin.astype(MXU)

        @pl.when(pl.program_id(0) == 0)
        def _():
            db_ref[...] = jnp.zeros_like(db_ref)

        db_ref[...] += jnp.sum(dzlin, axis=0, keepdims=True)

    row = pl.BlockSpec((tm, W), lambda i: (i, 0))
    vec = pl.BlockSpec((1, W), lambda i: (0, 0))
    return pl.pallas_call(
        body, name=name, grid=(L // tm,),
        in_specs=[row, row, pl.BlockSpec((tm, W), lambda i: (i, 1)), pl.BlockSpec((W, W), lambda i: (0, 0)), vec],
        out_specs=[row, row, row, row, vec],
        out_shape=[jax.ShapeDtypeStruct((L, W), F32), jax.ShapeDtypeStruct((L, W), F32),
                   jax.ShapeDtypeStruct((L, W), MXU), jax.ShapeDtypeStruct((L, W), MXU),
                   jax.ShapeDtypeStruct((1, W), F32)],
        compiler_params=_cp(("arbitrary",)),
    )(dya, ys, main, gw, gb.reshape(1, W))


def _sg_fn(ub, vb, zb, lnw, lnb, ws, bs):
    u = jax.nn.gelu(ub)
    v = _layernorm(jax.nn.gelu(vb), lnw, lnb)
    r = lax.broadcasted_iota(jnp.int32, (SG_CHUNK, SG_CHUNK), 0)
    c = lax.broadcasted_iota(jnp.int32, (SG_CHUNK, SG_CHUNK), 1)
    tri = r >= c
    outs = []
    for h in range(SG_HEADS):
        wh = jnp.where(tri, ws[h], 0.0)
        outs.append(_mm_nn(wh, v[:, h * 128:(h + 1) * 128]) + bs[h])
    mixed = jnp.concatenate(outs, axis=1)
    return u * mixed * _silu(zb)


def _sg_specs(L):
    W = SSM_WIDTH
    blk = lambda c: pl.BlockSpec((SG_CHUNK, W), lambda i, c=c: (i, c))
    vec = pl.BlockSpec((1, W), lambda i: (0, 0))
    wspec = pl.BlockSpec((SG_HEADS, SG_CHUNK, SG_CHUNK), lambda i: (0, 0, 0))
    bspec = pl.BlockSpec((SG_HEADS, SG_CHUNK, 1), lambda i: (0, 0, 0))
    return blk, vec, wspec, bspec


def _sg_fwd(main, lnw, lnb, sgw, sgb, *, name):
    L = main.shape[0]
    W = SSM_WIDTH
    blk, vec, wspec, bspec = _sg_specs(L)

    def body(ub_ref, vb_ref, zb_ref, lnw_ref, lnb_ref, w_ref, b_ref, yb_ref):
        ws = [w_ref[h] for h in range(SG_HEADS)]
        bs = [b_ref[h] for h in range(SG_HEADS)]
        yb_ref[...] = _sg_fn(ub_ref[...], vb_ref[...], zb_ref[...], lnw_ref[...], lnb_ref[...], ws, bs).astype(MXU)

    return pl.pallas_call(
        body, name=name, grid=(L // SG_CHUNK,),
        in_specs=[blk(2), blk(3), blk(4), vec, vec, wspec, bspec],
        out_specs=pl.BlockSpec((SG_CHUNK, W), lambda i: (i, 0)),
        out_shape=jax.ShapeDtypeStruct((L, W), MXU),
        compiler_params=_cp(("parallel",)),
    )(main, main, main, lnw.reshape(1, W), lnb.reshape(1, W), sgw, sgb.reshape(SG_HEADS, SG_CHUNK, 1))


def _sg_bwd(dyb, main, lnw, lnb, sgw, sgb, *, name):
    L = main.shape[0]
    W = SSM_WIDTH
    blk, vec, wspec, bspec = _sg_specs(L)

    def body(dyb_ref, ub_ref, vb_ref, zb_ref, lnw_ref, lnb_ref, w_ref, b_ref,
             dub_ref, dvb_ref, dzb_ref, dlnw_ref, dlnb_ref, dw_ref, db_ref):
        ws = [w_ref[h] for h in range(SG_HEADS)]
        bs = [b_ref[h] for h in range(SG_HEADS)]
        _, vjp = jax.vjp(_sg_fn, ub_ref[...], vb_ref[...], zb_ref[...], lnw_ref[...], lnb_ref[...], ws, bs)
        dub, dvb, dzb, dlnw, dlnb, dws, dbs = vjp(dyb_ref[...])

        @pl.when(pl.program_id(0) == 0)
        def _():
            dlnw_ref[...] = jnp.zeros_like(dlnw_ref)
            dlnb_ref[...] = jnp.zeros_like(dlnb_ref)
            dw_ref[...] = jnp.zeros_like(dw_ref)
            db_ref[...] = jnp.zeros_like(db_ref)

        dub_ref[...] = dub
        dvb_ref[...] = dvb
        dzb_ref[...] = dzb
        dlnw_ref[...] += dlnw
        dlnb_ref[...] += dlnb
        for h in range(SG_HEADS):
            dw_ref[h] += dws[h]
            db_ref[h] += dbs[h]

    row = pl.BlockSpec((SG_CHUNK, W), lambda i: (i, 0))
    out = jax.ShapeDtypeStruct((L, W), F32)
    return pl.pallas_call(
        body, name=name, grid=(L // SG_CHUNK,),
        in_specs=[row, blk(2), blk(3), blk(4), vec, vec, wspec, bspec],
        out_specs=[row, row, row, vec, vec, wspec, bspec],
        out_shape=[out, out, out, jax.ShapeDtypeStruct((1, W), F32), jax.ShapeDtypeStruct((1, W), F32),
                   jax.ShapeDtypeStruct((SG_HEADS, SG_CHUNK, SG_CHUNK), F32),
                   jax.ShapeDtypeStruct((SG_HEADS, SG_CHUNK, 1), F32)],
        compiler_params=_cp(("arbitrary",)),
    )(dyb, main, main, main, lnw.reshape(1, W), lnb.reshape(1, W), sgw, sgb.reshape(SG_HEADS, SG_CHUNK, 1))


def _rope_tables(L):
    half = ROT_DIM // 2
    inv_freq = ROPE_THETA ** (-jnp.arange(0, ROT_DIM, 2, dtype=F32) / ROT_DIM)
    ang = jnp.arange(L, dtype=F32)[:, None] * inv_freq[None, :]
    cos = jnp.cos(ang)
    sin = jnp.sin(ang)
    ones = jnp.ones((L, HEAD_DIM - ROT_DIM), F32)
    cosf = jnp.concatenate([cos, cos, ones], axis=1)
    sinf = jnp.concatenate([sin, sin, 0.0 * ones], axis=1)
    rot = np.zeros((HEAD_DIM, HEAD_DIM), np.float32)
    for d in range(half):
        rot[d + half, d] = -1.0
        rot[d, d + half] = 1.0
    return cosf, sinf, jnp.asarray(rot)


def _rope(t, cosf, sinf, rot):
    shp = t.shape
    t2 = t.reshape(-1, HEAD_DIM)
    sw = lax.dot_general(t2, rot, _DIMS["nn"], precision=lax.Precision.HIGH, preferred_element_type=F32).reshape(shp)
    return t * cosf + sw * sinf


def _attn_core_parts(s, va, sink):
    h, q, k = s.shape
    m = jnp.maximum(jnp.max(s, axis=-1, keepdims=True), sink)
    e = jnp.exp(s - m)
    es = jnp.exp(sink - m)
    ev = _dot(e.reshape(h * q, k), va, "nn")
    r = 1.0 / (ev[:, HEAD_DIM:HEAD_DIM + 1].reshape(h, q, 1) + es)
    return ev[:, :HEAD_DIM] * r.reshape(h * q, 1), e, r, es


@jax.custom_vjp
def _attn_core(s, v, va, sink):
    return _attn_core_parts(s, va, sink)[0]


def _attn_core_fwd(s, v, va, sink):
    o, e, r, es = _attn_core_parts(s, va, sink)
    return o, (o, e, r, es, v, va)


def _attn_core_bwd(res, do):
    o, e, r, es, v, va = res
    h, q, k = e.shape
    p = e * r
    t = jnp.sum(o * do, axis=-1, keepdims=True).reshape(h, q, 1)
    dp = _dot(do, v, "nt").reshape(h, q, k)
    dv = _dot(p.reshape(h * q, k), do, "tn")
    dsink = -jnp.sum(es * r * t, axis=1, keepdims=True)
    return p * (dp - t), dv, jnp.zeros_like(va), dsink


_attn_core.defvjp(_attn_core_fwd, _attn_core_bwd)


def _attn_block_fn(q, kw, vw, sinks, vaw, cq, sq, ck, sk, rot, q0, k0):
    nk = kw.shape[1]
    qr = _rope(q, cq, sq, rot)
    kr = _rope(kw, ck, sk, rot)
    qpos = q0 + lax.broadcasted_iota(jnp.int32, (1, ATT_BLOCK, nk), 1)
    kpos = k0 + lax.broadcasted_iota(jnp.int32, (1, ATT_BLOCK, nk), 2)
    diff = qpos - kpos
    allowed = (diff >= 0) & (diff < WINDOW)
    outs = []
    for kh in range(ATT_KV_HEADS):
        qh = qr[kh * GQA_GROUP:(kh + 1) * GQA_GROUP].reshape(GQA_GROUP * ATT_BLOCK, HEAD_DIM)
        s = _mm_nt(qh, kr[kh]).reshape(GQA_GROUP, ATT_BLOCK, nk) * (HEAD_DIM ** -0.5)
        s = jnp.where(allowed, s, NEG_INF)
        o = _attn_core(s, vw[kh], vaw[kh], sinks[kh * GQA_GROUP:(kh + 1) * GQA_GROUP])
        outs.append(o.reshape(GQA_GROUP, ATT_BLOCK, HEAD_DIM))
    return jnp.concatenate(outs, axis=0)


def _attn_common(L):
    nwin = min(2 * ATT_BLOCK, L)
    qspec = pl.BlockSpec((ATT_HEADS, ATT_BLOCK, HEAD_DIM), lambda n: (0, n, 0))
    kvspec = pl.BlockSpec((ATT_KV_HEADS, L, HEAD_DIM), lambda n: (0, 0, 0))
    sspec = pl.BlockSpec((ATT_HEADS, 1, 1), lambda n: (0, 0, 0))
    tq = pl.BlockSpec((ATT_BLOCK, HEAD_DIM), lambda n: (n, 0))
    tk = pl.BlockSpec((L, HEAD_DIM), lambda n: (0, 0))
    rspec = pl.BlockSpec((HEAD_DIM, HEAD_DIM), lambda n: (0, 0))
    vaspec = pl.BlockSpec((ATT_KV_HEADS, L, 2 * HEAD_DIM), lambda n: (0, 0, 0))
    return nwin, qspec, kvspec, sspec, tq, tk, rspec, vaspec


def _v_with_ones(vh):
    return jnp.concatenate([vh, jnp.ones_like(vh)], axis=-1).astype(MXU)


def _attn_fwd(qh, kh, vh, sinks, cosf, sinf, rot, *, name):
    L = qh.shape[1]
    nwin, qspec, kvspec, sspec, tq, tk, rspec, vaspec = _attn_common(L)

    def body(q_ref, k_ref, v_ref, s_ref, va_ref, cq_ref, sq_ref, ck_ref, sk_ref, r_ref, o_ref):
        n = pl.program_id(0)
        k0 = pl.multiple_of(jnp.maximum(n - 1, 0) * ATT_BLOCK, ATT_BLOCK)
        win = pl.ds(k0, nwin)
        o_ref[...] = _attn_block_fn(q_ref[...], k_ref[:, win, :], v_ref[:, win, :], s_ref[...], va_ref[:, win, :],
                                    cq_ref[...], sq_ref[...], ck_ref[win, :], sk_ref[win, :], r_ref[...],
                                    n * ATT_BLOCK, k0)

    return pl.pallas_call(
        body, name=name, grid=(L // ATT_BLOCK,),
        in_specs=[qspec, kvspec, kvspec, sspec, vaspec, tq, tq, tk, tk, rspec],
        out_specs=qspec,
        out_shape=jax.ShapeDtypeStruct((ATT_HEADS, L, HEAD_DIM), F32),
        compiler_params=_cp(("parallel",)),
    )(qh, kh, vh, sinks.reshape(ATT_HEADS, 1, 1), _v_with_ones(vh), cosf, sinf, cosf, sinf, rot)


def _attn_bwd(do, qh, kh, vh, sinks, cosf, sinf, rot, *, name):
    L = qh.shape[1]
    nwin, qspec, kvspec, sspec, tq, tk, rspec, vaspec = _attn_common(L)

    def body(do_ref, q_ref, k_ref, v_ref, s_ref, va_ref, cq_ref, sq_ref, ck_ref, sk_ref, r_ref,
             dq_ref, dk_ref, dv_ref, ds_ref):
        n = pl.program_id(0)
        k0 = pl.multiple_of(jnp.maximum(n - 1, 0) * ATT_BLOCK, ATT_BLOCK)
        win = pl.ds(k0, nwin)
        cq, sq, ck, sk, rt = cq_ref[...], sq_ref[...], ck_ref[win, :], sk_ref[win, :], r_ref[...]
        vaw = va_ref[:, win, :]
        q0 = n * ATT_BLOCK
        _, vjp = jax.vjp(lambda q, kw, vw, s: _attn_block_fn(q, kw, vw, s, vaw, cq, sq, ck, sk, rt, q0, k0),
                         q_ref[...], k_ref[:, win, :], v_ref[:, win, :], s_ref[...])
        dq, dkw, dvw, ds = vjp(do_ref[...])

        @pl.when(n == 0)
        def _():
            dk_ref[...] = jnp.zeros_like(dk_ref)
            dv_ref[...] = jnp.zeros_like(dv_ref)
            ds_ref[...] = jnp.zeros_like(ds_ref)

        dq_ref[...] = dq
        dk_ref[:, win, :] += dkw
        dv_ref[:, win, :] += dvw
        ds_ref[...] += ds

    return pl.pallas_call(
        body, name=name, grid=(L // ATT_BLOCK,),
        in_specs=[qspec, qspec, kvspec, kvspec, sspec, vaspec, tq, tq, tk, tk, rspec],
        out_specs=[qspec, kvspec, kvspec, sspec],
        out_shape=[jax.ShapeDtypeStruct((ATT_HEADS, L, HEAD_DIM), F32),
                   jax.ShapeDtypeStruct((ATT_KV_HEADS, L, HEAD_DIM), F32),
                   jax.ShapeDtypeStruct((ATT_KV_HEADS, L, HEAD_DIM), F32),
                   jax.ShapeDtypeStruct((ATT_HEADS, 1, 1), F32)],
        compiler_params=_cp(("arbitrary",)),
    )(do, qh, kh, vh, sinks.reshape(ATT_HEADS, 1, 1), _v_with_ones(vh), cosf, sinf, cosf, sinf, rot)


def _to_heads(t, nh):
    L = t.shape[0]
    return t.reshape(L, nh, HEAD_DIM).transpose(1, 0, 2)


def _from_heads(t):
    nh, L, _ = t.shape
    return t.transpose(1, 0, 2).reshape(L, nh * HEAD_DIM)


def _branch_fwd(ya, yb, o2d, zc, gates, wa, wb, wc, *, name, tm=256):
    L = ya.shape[0]
    tm = min(tm, L)
    W, D = SSM_WIDTH, D_MODEL

    def body(ya_ref, yb_ref, o_ref, zc_ref, g0_ref, g1_ref, g2_ref, wa_ref, wb_ref, wc_ref,
             mg_ref, ta_ref, tb_ref, tc_ref, yc_ref):
        yc = (o_ref[...] * _silu(zc_ref[...])).astype(MXU)
        ta = _dot(ya_ref[...], wa_ref[...], "nt")
        tb = _dot(yb_ref[...], wb_ref[...], "nt")
        tc = _dot(yc, wc_ref[...], "nt")
        ta_ref[...] = ta
        tb_ref[...] = tb
        tc_ref[...] = tc
        yc_ref[...] = yc
        mg_ref[...] = (jax.nn.sigmoid(g0_ref[...]) * ta + jax.nn.sigmoid(g1_ref[...]) * tb
                       + jax.nn.sigmoid(g2_ref[...]) * tc).astype(MXU)

    row = pl.BlockSpec((tm, W), lambda i: (i, 0))
    wide = pl.BlockSpec((tm, D), lambda i: (i, 0))
    gate = lambda c: pl.BlockSpec((tm, D), lambda i, c=c: (i, c))
    wspec = pl.BlockSpec((D, W), lambda i: (0, 0))
    return pl.pallas_call(
        body, name=name, grid=(L // tm,),
        in_specs=[row, row, row, row, gate(0), gate(1), gate(2), wspec, wspec, wspec],
        out_specs=[wide, wide, wide, wide, row],
        out_shape=[jax.ShapeDtypeStruct((L, D), MXU), jax.ShapeDtypeStruct((L, D), F32),
                   jax.ShapeDtypeStruct((L, D), F32), jax.ShapeDtypeStruct((L, D), F32),
                   jax.ShapeDtypeStruct((L, W), MXU)],
        compiler_params=_cp(("parallel",), 56),
    )(ya, yb, o2d, zc, gates, gates, gates, wa, wb, wc)


def _branch_bwd(dmg, ta, tb, tc, gates, *, name, tm=256):
    L = dmg.shape[0]
    tm = min(tm, L)
    D = D_MODEL

    def body(dm_ref, ta_ref, tb_ref, tc_ref, g0_ref, g1_ref, g2_ref, da_ref, db_ref, dc_ref, dg_ref):
        dm = dm_ref[...]
        for i, (t_ref, g_ref, d_ref) in enumerate(((ta_ref, g0_ref, da_ref), (tb_ref, g1_ref, db_ref),
                                                   (tc_ref, g2_ref, dc_ref))):
            sg = jax.nn.sigmoid(g_ref[...])
            d_ref[...] = (sg * dm).astype(MXU)
            dg_ref[:, i * D:(i + 1) * D] = (dm * t_ref[...] * sg * (1.0 - sg)).astype(MXU)

    wide = pl.BlockSpec((tm, D), lambda i: (i, 0))
    gate = lambda c: pl.BlockSpec((tm, D), lambda i, c=c: (i, c))
    bf = jax.ShapeDtypeStruct((L, D), MXU)
    return pl.pallas_call(
        body, name=name, grid=(L // tm,),
        in_specs=[wide, wide, wide, wide, gate(0), gate(1), gate(2)],
        out_specs=[wide, wide, wide, pl.BlockSpec((tm, 3 * D), lambda i: (i, 0))],
        out_shape=[bf, bf, bf, jax.ShapeDtypeStruct((L, 3 * D), MXU)],
        compiler_params=_cp(("parallel",), 56),
    )(dmg, ta, tb, tc, gates, gates, gates)


def _gate_c_bwd(dyc, o2d, zc, *, name, tm=256):
    L, W = dyc.shape
    tm = min(tm, L)

    def body(dy_ref, o_ref, z_ref, do_ref, dz_ref):
        _, vjp = jax.vjp(lambda o, z: o * _silu(z), o_ref[...], z_ref[...])
        do, dz = vjp(dy_ref[...])
        do_ref[...] = do
        dz_ref[...] = dz.astype(MXU)

    row = pl.BlockSpec((tm, W), lambda i: (i, 0))
    return pl.pallas_call(body, name=name, grid=(L // tm,), in_specs=[row, row, row], out_specs=[row, row],
                          out_shape=[jax.ShapeDtypeStruct((L, W), F32), jax.ShapeDtypeStruct((L, W), MXU)],
                          compiler_params=_cp(("parallel",)))(dyc, o2d, zc)


def _adamw(w, g, m, v, *, name):
    shape = w.shape
    cols = shape[-1]
    w2, g2, m2, v2 = (t.reshape(-1, cols) for t in (w, g, m, v))
    rows = w2.shape[0]
    tc = 1024 if cols % 1024 == 0 else cols
    lane_cols = -(-tc // 128) * 128
    tr = rows
    while tr % 16 == 0 and tr * lane_cols * 4 > 2 * _MB:
        tr //= 2

    def body(w_ref, g_ref, m_ref, v_ref, d_ref, nm_ref, nv_ref):
        gv = g_ref[...]
        nm = ADAM_B1 * m_ref[...] + (1.0 - ADAM_B1) * gv
        nv = ADAM_B2 * v_ref[...] + (1.0 - ADAM_B2) * jnp.square(gv)
        m_hat = nm / (1.0 - ADAM_B1 ** ADAM_STEP)
        v_hat = nv / (1.0 - ADAM_B2 ** ADAM_STEP)
        d_ref[...] = -ADAM_LR * (m_hat / (jnp.sqrt(v_hat) + ADAM_EPS) + ADAM_WD * w_ref[...])
        nm_ref[...] = nm
        nv_ref[...] = nv

    spec = pl.BlockSpec((tr, tc), lambda i, j: (i, j))
    out = jax.ShapeDtypeStruct((rows, cols), F32)
    d, nm, nv = pl.pallas_call(body, name=name, grid=(rows // tr, cols // tc), in_specs=[spec] * 4,
                               out_specs=[spec] * 3, out_shape=[out, out, out],
                               compiler_params=_cp(("parallel", "parallel")))(w2, g2, m2, v2)
    return d.reshape(shape), nm.reshape(shape), nv.reshape(shape)


def _adamw_layer(w, g, m, v, l, prev, *, name):
    _, rows, cols = w.shape
    tc = 1024 if cols % 1024 == 0 else cols
    tr = rows
    while tr % 16 == 0 and tr * tc * 4 > 2 * _MB:
        tr //= 2

    def body(w_ref, g_ref, m_ref, v_ref, *rest):
        go_ref, d_ref, nm_ref, nv_ref = rest[-4:]
        gv = g_ref[...]
        nm = ADAM_B1 * m_ref[...] + (1.0 - ADAM_B1) * gv
        nv = ADAM_B2 * v_ref[...] + (1.0 - ADAM_B2) * jnp.square(gv)
        m_hat = nm / (1.0 - ADAM_B1 ** ADAM_STEP)
        v_hat = nv / (1.0 - ADAM_B2 ** ADAM_STEP)
        d_ref[...] = -ADAM_LR * (m_hat / (jnp.sqrt(v_hat) + ADAM_EPS) + ADAM_WD * w_ref[...])
        nm_ref[...] = nm
        nv_ref[...] = nv
        go_ref[...] = gv

    lspec = pl.BlockSpec((None, tr, tc), lambda i, j: (l, i, j))
    gspec = pl.BlockSpec((tr, tc), lambda i, j: (i, j))
    out = jax.ShapeDtypeStruct(w.shape, F32)
    extra = [] if prev is None else list(prev)
    return pl.pallas_call(
        body, name=name, grid=(rows // tr, cols // tc),
        in_specs=[lspec, gspec, lspec, lspec] + [_ANY] * len(extra),
        out_specs=[lspec] * 4, out_shape=[out] * 4,
        input_output_aliases={4 + i: i for i in range(len(extra))},
        compiler_params=_cp(("parallel", "parallel")),
    )(w, g, m, v, *extra)


def _prep_layer(p, l, after=None):
    are = p["ssm_a_re"][l].reshape(SSM_CH, 1)
    aim = p["ssm_a_im"][l].reshape(SSM_CH, 1)
    ldt = p["ssm_log_dt"][l].reshape(1, SSM_GROUPS)
    bre = p["ssm_b_re"][l].reshape(SSM_CH, SSM_GROUP)
    bim = p["ssm_b_im"][l].reshape(SSM_CH, SSM_GROUP)
    lbr, lbi, bbr, bbi = _s5_params_fwd(are, aim, ldt, bre, bim, name=f"s5_params_fwd_{l}", after=after)
    cre = p["ssm_c_re"][l].transpose(0, 2, 1).reshape(SSM_CH, SSM_GROUP)
    cim = p["ssm_c_im"][l].transpose(0, 2, 1).reshape(SSM_CH, SSM_GROUP)
    return dict(raw=(are, aim, ldt, bre, bim),
                lbr=lbr.reshape(N_SLAB * SLAB_NC, 1, 128), lbi=lbi.reshape(N_SLAB * SLAB_NC, 1, 128),
                btr=_expand_bd(bbr), bti=_expand_bd(bbi), cbr=_expand_bd(cre), cbi=_expand_bd(cim),
                dvec=p["ssm_d"][l].reshape(1, SSM_WIDTH))


def _layer_fwd(x, h, p, sp, winT, rest_of, l, tabs, proj_after=None, after_main=None):
    L = x.shape[0]
    cosf, sinf, rot = tabs
    mm = functools.partial(_matmul, h, winT, "nt", tm=L, tn=256, tk=D_MODEL)
    main = mm(name=f"proj_main_{l}", shape=(L, N_MAIN, D_MODEL), after=proj_after)
    then = proj_after if after_main is None else after_main(main)
    zc = mm(name=f"proj_zc_{l}", shape=(L, N_ZC, D_MODEL), b_off=(N_MAIN // 256, 0), after=then)
    gates = mm(name=f"proj_gates_{l}", shape=(L, N_GATES, D_MODEL), b_off=((N_MAIN + N_ZC) // 256, 0), after=then)
    big, token = rest_of([main, zc, gates])
    ua = _time_interleave(main[:, :SSM_WIDTH])
    ys, sr, si = _s5_fwd(ua, sp["btr"], sp["bti"], sp["cbr"], sp["cbi"], sp["lbr"], sp["lbi"], sp["dvec"],
                         name=f"s5_fwd_{l}", after=token)
    ys = _time_deinterleave(ys)
    ya = _glu_fwd(ys, main, big["glu_w"], p["ssm_glu_b"][l], name=f"glu_fwd_{l}")
    yb = _sg_fwd(main, p["sg_ln_w"][l], p["sg_ln_b"][l], p["sg_w"][l], p["sg_b"][l], name=f"sg_fwd_{l}")
    qh = _to_heads(main[:, 5120:6144], ATT_HEADS)
    kh = _to_heads(main[:, 6144:6272], ATT_KV_HEADS)
    vh = _to_heads(main[:, 6272:6400], ATT_KV_HEADS)
    oh = _attn_fwd(qh, kh, vh, p["attn_sinks"][l], cosf, sinf, rot, name=f"attn_fwd_{l}")
    o2d = _from_heads(oh)
    mg, ta, tb, tc, yc = _branch_fwd(ya, yb, o2d, zc, gates, big["wbaT"], big["wbbT"], big["wbcT"],
                                     name=f"branch_fwd_{l}")
    xn = _matmul(mg, big["w_out"], "nn", name=f"out_fwd_{l}", shape=(L, D_MODEL, D_MODEL), tm=1024, tn=1024,
                 tk=D_MODEL, add=x)
    saved = dict(x=x, h=h, main=main, zc=zc, gates=gates, ua=ua, ys=ys, sr=sr, si=si, ya=ya, yb=yb, yc=yc, o2d=o2d,
                 qh=qh, kh=kh, vh=vh, mg=mg, ta=ta, tb=tb, tc=tc, sp=sp)
    return xn, saved, big


def _layer_bwd(dxn, s, p, big, l, tabs, early, mid):
    L = dxn.shape[0]
    D, W = D_MODEL, SSM_WIDTH
    cosf, sinf, rot = tabs
    sp = s["sp"]
    g = {}
    dmg = _matmul(dxn, big["w_out"], "nt", name=f"out_bwd_dm_{l}", shape=(L, D, D), tm=1024, tn=1024, tk=D)
    g["w_out"] = _matmul(s["mg"], dxn, "tn", name=f"out_bwd_dw_{l}", shape=(D, D, L), tm=1024, tn=1024, tk=L,
                         out_dtype=MXU)
    dta, dtb, dtc, dgates = _branch_bwd(dmg, s["ta"], s["tb"], s["tc"], s["gates"], name=f"branch_bwd_{l}")
    dys_ = {}
    for nm, dt, y, wt in (("a", dta, s["ya"], big["wbaT"]), ("b", dtb, s["yb"], big["wbbT"]),
                          ("c", dtc, s["yc"], big["wbcT"])):
        dys_[nm] = _matmul(dt, wt, "nn", name=f"branch_bwd_dy{nm}_{l}", shape=(L, W, D), tm=1024, tn=1024, tk=D)
        g["wb" + nm + "T"] = _matmul(dt, y, "tn", name=f"branch_bwd_dw{nm}_{l}", shape=(D, W, L),
                                     tm=1024, tn=1024, tk=L, out_dtype=MXU)
    do2d, dzc = _gate_c_bwd(dys_["c"], s["o2d"], s["zc"], name=f"gate_c_bwd_{l}")
    dqh, dkh, dvh, dsinks = _attn_bwd(_to_heads(do2d, ATT_HEADS), s["qh"], s["kh"], s["vh"], p["attn_sinks"][l],
                                      cosf, sinf, rot, name=f"attn_bwd_{l}")
    g["attn_sinks"] = dsinks.reshape(ATT_HEADS)
    dub, dvb, dzb, dlnw, dlnb, dsgw, dsgb = _sg_bwd(dys_["b"], s["main"], p["sg_ln_w"][l], p["sg_ln_b"][l],
                                                    p["sg_w"][l], p["sg_b"][l], name=f"sg_bwd_{l}")
    g["sg_ln_w"], g["sg_ln_b"] = dlnw.reshape(W), dlnb.reshape(W)
    g["sg_w"], g["sg_b"] = dsgw, dsgb.reshape(SG_HEADS, SG_CHUNK)
    dys, dza, a1, dzl, dgb = _glu_bwd(dys_["a"], s["ys"], s["main"], big["glu_w"], p["ssm_glu_b"][l],
                                      name=f"glu_bwd_{l}")
    g["ssm_glu_b"] = dgb.reshape(W)
    g["glu_w"] = _matmul(a1, dzl, "tn", name=f"glu_bwd_dw_{l}", shape=(W, W, L), tm=1024, tn=1024, tk=L, out_dtype=MXU)
    token = early(g)
    dua, dbtr, dbti, dcbr, dcbi, dlr, dli, dd = _s5_bwd(_time_interleave(dys), s["ua"], s["sr"], s["si"], sp["btr"],
                                                        sp["bti"], sp["cbr"], sp["cbi"], sp["lbr"], sp["lbi"],
                                                        sp["dvec"], name=f"s5_bwd_{l}", after=token)
    token = mid(dua)
    dua = _time_deinterleave(dua)
    g["ssm_d"] = dd.reshape(W)
    to_c = lambda t: _contract_bd(t).reshape(SSM_GROUPS, SSM_STATE, SSM_GROUP).transpose(0, 2, 1)
    g["ssm_c_re"], g["ssm_c_im"] = to_c(dcbr), to_c(dcbi)
    dare, daim, dldt, dbre, dbim = _s5_params_bwd(*sp["raw"], dlr.reshape(SSM_CH, 1), dli.reshape(SSM_CH, 1),
                                                  _contract_bd(dbtr), _contract_bd(dbti),
                                                  name=f"s5_params_bwd_{l}")
    g["ssm_a_re"] = dare.reshape(SSM_GROUPS, SSM_STATE)
    g["ssm_a_im"] = daim.reshape(SSM_GROUPS, SSM_STATE)
    g["ssm_log_dt"] = dldt.reshape(SSM_GROUPS)
    g["ssm_b_re"] = dbre.reshape(SSM_GROUPS, SSM_STATE, SSM_GROUP)
    g["ssm_b_im"] = dbim.reshape(SSM_GROUPS, SSM_STATE, SSM_GROUP)
    dproj = jnp.concatenate([t.astype(MXU) for t in (dua, dza, dub, dvb, dzb, _from_heads(dqh), _from_heads(dkh),
                                                     _from_heads(dvh), dzc, dgates)], axis=1)
    g["winT"] = _matmul(dproj, s["h"], "tn", name=f"proj_bwd_dw_{l}", shape=(D_IN, D, L), tm=256, tn=D, tk=L,
                        out_dtype=MXU, after=token)
    return dproj, g


def _proj_bwd_dh(dproj, winT, l, after):
    return _matmul(dproj, winT, "nn", name=f"proj_bwd_dh_{l}", shape=(dproj.shape[0], D_MODEL, D_IN), tm=1024, tn=512,
                   tk=D_IN // 2, after=after, vmem_mb=58)


MESH = pl.DeviceIdType.MESH
_ANY = pl.BlockSpec(memory_space=pl.ANY)
ROW_ALIGN = 16


def _coords():
    return lax.axis_index("x"), lax.axis_index("y"), lax.axis_index("c")


def _gather8(arrs, *, name):
    n = len(arrs)
    rows = [a.shape[0] for a in arrs]
    for r in rows:
        assert r % ROW_ALIGN == 0

    def body(*refs):
        ins, outs = refs[:n], refs[n:2 * n]
        send, recv, lsem = refs[2 * n:]
        x, y, c = _coords()
        me, sibling = (x, y, c), (x, y, 1 - c)
        chips = [(1 - x, y), (x, 1 - y), (1 - x, 1 - y)]

        def blk(a, px, py, pc):
            return outs[a].at[pl.ds(pl.multiple_of((4 * px + 2 * py + pc) * rows[a], ROW_ALIGN), rows[a]), :]

        def own(a):
            return ins[a]

        def copy(a, k, block, to, src=None):
            return pltpu.make_async_remote_copy(
                src_ref=blk(a, *block) if src is None else src, dst_ref=blk(a, *block),
                send_sem=send.at[a, k], recv_sem=recv.at[a, k], device_id=to, device_id_type=MESH)

        mine, first, passed = [], [], []
        for a in range(n):
            mine.append(pltpu.make_async_copy(own(a), blk(a, *me), lsem.at[a]))
            mine[a].start()
            f = [copy(a, 0, me, sibling, src=own(a))]
            f += [copy(a, 1 + j, me, (*chip, c), src=own(a)) for j, chip in enumerate(chips)]
            for cp in f:
                cp.start()
            first.append(f)
        for a in range(n):
            ps = [copy(a, 4 + j, (*chip, c), sibling) for j, chip in enumerate(chips)]
            for j, chip in enumerate(chips):
                copy(a, 1 + j, (*chip, c), me).wait_recv()
                ps[j].start()
            passed.append(ps)
        for a in range(n):
            copy(a, 0, sibling, me).wait_recv()
            for j, chip in enumerate(chips):
                copy(a, 4 + j, (*chip, 1 - c), me).wait_recv()
            for cp in first[a] + passed[a]:
                cp.wait_send()
            mine[a].wait()

    return pl.pallas_call(
        body, name=name,
        in_specs=[_ANY] * n, out_specs=[_ANY] * n,
        out_shape=[jax.ShapeDtypeStruct((8 * r,) + a.shape[1:], a.dtype) for r, a in zip(rows, arrs)],
        scratch_shapes=[pltpu.SemaphoreType.DMA((n, 7)), pltpu.SemaphoreType.DMA((n, 7)), pltpu.SemaphoreType.DMA((n,))],
    )(*arrs)


def _sibling_swap(arrs, *, name):
    n = len(arrs)

    def body(*refs):
        ins, outs = refs[:n], refs[n:2 * n]
        send, recv = refs[2 * n:]
        x, y, c = _coords()
        cps = [pltpu.make_async_remote_copy(src_ref=ins[a].at[:, 1 - c], dst_ref=outs[a], send_sem=send.at[a],
                                            recv_sem=recv.at[a], device_id=(x, y, 1 - c), device_id_type=MESH)
               for a in range(n)]
        for cp in cps:
            cp.start()
        for cp in cps:
            cp.wait_recv()
        for cp in cps:
            cp.wait_send()

    return pl.pallas_call(
        body, name=name, in_specs=[_ANY] * n, out_specs=[_ANY] * n,
        out_shape=[jax.ShapeDtypeStruct((a.shape[0],) + a.shape[2:], a.dtype) for a in arrs],
        scratch_shapes=[pltpu.SemaphoreType.DMA((n,)), pltpu.SemaphoreType.DMA((n,))],
    )(*arrs)


def _col_tile(lead, rows, cols, itemsize=4, cap=4 * _MB):
    tc = cols
    while tc % 256 == 0 and lead * rows * tc * itemsize > cap:
        tc //= 2
    return tc


def _pair_sum(mine, theirs, *, name):
    _, _, rows, cols = mine.shape
    tc = _col_tile(1, rows, cols)
    c = lax.axis_index("c")

    def body(c_ref, a_ref, b_ref, o_ref):
        o_ref[...] = (a_ref[...].astype(F32) + b_ref[...].astype(F32)).astype(MXU)

    return pl.pallas_call(
        body, name=name,
        grid_spec=pltpu.PrefetchScalarGridSpec(
            num_scalar_prefetch=1, grid=(4, cols // tc),
            in_specs=[pl.BlockSpec((None, None, rows, tc), lambda j, i, cr: (j, cr[0], 0, i)),
                      pl.BlockSpec((None, rows, tc), lambda j, i, cr: (j, 0, i))],
            out_specs=pl.BlockSpec((None, rows, tc), lambda j, i, cr: (j, 0, i))),
        out_shape=jax.ShapeDtypeStruct((4, rows, cols), MXU),
        compiler_params=_cp(("parallel", "parallel")),
    )(c.reshape(1).astype(jnp.int32), mine, theirs)


_HBM = pl.BlockSpec(memory_space=pltpu.HBM)
_SEM = pl.BlockSpec(memory_space=pltpu.SEMAPHORE)
_EFFECT = pltpu.SideEffectType.DATAFLOW_SIDE_EFFECTING
N_PEER_CHIPS = 3


def _peer_chips(x, y):
    return [(1 - x, y), (x, 1 - y), (1 - x, 1 - y)]


def _split_start(srcs, lands, src_slot, dst_slot, *, name, after=()):
    n = len(srcs)
    ns = n * N_PEER_CHIPS
    first = 2 * n + len(after)

    def body(*refs):
        src_refs, land_refs = refs[:n], refs[n:2 * n]
        send, recv, token = refs[first:first + ns], refs[first + ns:first + 2 * ns], refs[-1]
        x, y, c = _coords()
        for a in range(n):
            for k, (px, py) in enumerate(_peer_chips(x, y)):
                pltpu.make_async_remote_copy(
                    src_ref=src_refs[a].at[src_slot(x, y, c, px, py)], dst_ref=land_refs[a].at[dst_slot(x, y, c)],
                    send_sem=send[a * N_PEER_CHIPS + k], recv_sem=recv[a * N_PEER_CHIPS + k],
                    device_id=(px, py, c), device_id_type=MESH).start()
        token[...] = jnp.zeros_like(token)

    bufs = list(srcs) + list(lands)
    res = pl.pallas_call(
        body, name=name,
        out_shape=(*[pltpu.SemaphoreType.DMA(())] * (2 * ns), *[pltpu.HBM(b.shape, b.dtype) for b in bufs],
                   jax.ShapeDtypeStruct((8, 128), F32)),
        in_specs=[_HBM] * (2 * n) + [_ANY] * len(after),
        out_specs=(*[_SEM] * (2 * ns), *[_HBM] * (2 * n), pl.BlockSpec(memory_space=pltpu.VMEM)),
        input_output_aliases={i: 2 * ns + i for i in range(2 * n)},
        compiler_params=pltpu.CompilerParams(has_side_effects=_EFFECT),
    )(*[pltpu.with_memory_space_constraint(b, pltpu.HBM) for b in bufs], *after)
    sems = list(res[:2 * ns])
    return sems, list(res[2 * ns:2 * ns + n]), list(res[2 * ns + n:2 * ns + 2 * n]), res[-1]


def _split_wait(sems, srcs, lands, after, *, name):
    n = len(srcs)
    ns = n * N_PEER_CHIPS

    def body(*refs):
        src_refs, land_refs = refs[:n], refs[n:2 * n]
        send, recv = refs[2 * n:2 * n + ns], refs[2 * n + ns:2 * n + 2 * ns]
        x, y, c = _coords()
        for a in range(n):
            for k in range(N_PEER_CHIPS):
                cp = pltpu.make_async_remote_copy(
                    src_ref=src_refs[a].at[0], dst_ref=land_refs[a].at[0], send_sem=send[a * N_PEER_CHIPS + k],
                    recv_sem=recv[a * N_PEER_CHIPS + k], device_id=(x, y, 1 - c), device_id_type=MESH)
                cp.wait_send()
                cp.wait_recv()

    bufs = list(srcs) + list(lands)
    res = pl.pallas_call(
        body, name=name,
        out_shape=tuple(pltpu.HBM(b.shape, b.dtype) for b in bufs),
        in_specs=[_HBM] * (2 * n) + [_SEM] * (2 * ns) + [_ANY] * len(after),
        out_specs=tuple([_HBM] * (2 * n)),
        input_output_aliases={i: i for i in range(2 * n)},
        compiler_params=pltpu.CompilerParams(has_side_effects=_EFFECT),
    )(*bufs, *sems, *after)
    return list(res[:n]), list(res[n:])


def _gather_start(lands, *, name, after=()):
    n = len(lands)
    ns = n * N_PEER_CHIPS
    first = n + len(after)

    def body(*refs):
        land_refs = refs[:n]
        send, recv, token = refs[first:first + ns], refs[first + ns:first + 2 * ns], refs[-1]
        x, y, c = _coords()
        mine = 4 * x + 2 * y + c
        for a in range(n):
            for k, (px, py) in enumerate(_peer_chips(x, y)):
                pltpu.make_async_remote_copy(
                    src_ref=land_refs[a].at[mine], dst_ref=land_refs[a].at[mine], send_sem=send[a * N_PEER_CHIPS + k],
                    recv_sem=recv[a * N_PEER_CHIPS + k], device_id=(px, py, c), device_id_type=MESH).start()
        token[...] = jnp.zeros_like(token)

    res = pl.pallas_call(
        body, name=name,
        out_shape=(*[pltpu.SemaphoreType.DMA(())] * (2 * ns), *[pltpu.HBM(b.shape, b.dtype) for b in lands],
                   jax.ShapeDtypeStruct((8, 128), F32)),
        in_specs=[_HBM] * n + [_ANY] * len(after),
        out_specs=(*[_SEM] * (2 * ns), *[_HBM] * n, pl.BlockSpec(memory_space=pltpu.VMEM)),
        input_output_aliases={i: 2 * ns + i for i in range(n)},
        compiler_params=pltpu.CompilerParams(has_side_effects=_EFFECT),
    )(*[pltpu.with_memory_space_constraint(b, pltpu.HBM) for b in lands], *after)
    return list(res[:2 * ns]), list(res[2 * ns:2 * ns + n]), res[-1]


def _gather_wait(sems, lands, after, *, name):
    n = len(lands)
    ns = n * N_PEER_CHIPS

    def body(*refs):
        land_refs = refs[:n]
        send, recv = refs[n:n + ns], refs[n + ns:n + 2 * ns]
        x, y, c = _coords()
        for a in range(n):
            for k in range(N_PEER_CHIPS):
                cp = pltpu.make_async_remote_copy(
                    src_ref=land_refs[a].at[0], dst_ref=land_refs[a].at[0], send_sem=send[a * N_PEER_CHIPS + k],
                    recv_sem=recv[a * N_PEER_CHIPS + k], device_id=(x, y, 1 - c), device_id_type=MESH)
                cp.wait_send()
                cp.wait_recv()

    res = pl.pallas_call(
        body, name=name,
        out_shape=tuple(pltpu.HBM(b.shape, b.dtype) for b in lands),
        in_specs=[_HBM] * n + [_SEM] * (2 * ns) + [_ANY] * len(after),
        out_specs=tuple([_HBM] * n),
        input_output_aliases={i: i for i in range(n)},
        compiler_params=pltpu.CompilerParams(has_side_effects=_EFFECT),
    )(*lands, *sems, *after)
    return list(res)


def _fill_own(shards, l, *, name, after=None):
    _, rows2, cols = shards.shape
    rows = rows2 // 2
    tc = _col_tile(1, rows, cols, itemsize=shards.dtype.itemsize)
    j = 2 * lax.axis_index("x") + lax.axis_index("y")
    extra = [] if after is None else [after]

    def body(j_ref, s_ref, *rest):
        rest[-1][...] = s_ref[...].astype(MXU)

    return pl.pallas_call(
        body, name=name,
        grid_spec=pltpu.PrefetchScalarGridSpec(
            num_scalar_prefetch=1, grid=(2, cols // tc),
            in_specs=([pl.BlockSpec((None, rows, tc), lambda h, i, jr: (l, h, i))]
                      + [pl.BlockSpec(memory_space=pl.ANY)] * len(extra)),
            out_specs=pl.BlockSpec((None, rows, tc), lambda h, i, jr: (2 * jr[0] + h, 0, i))),
        out_shape=jax.ShapeDtypeStruct((8, rows, cols), MXU),
        compiler_params=_cp(("parallel", "parallel")),
    )(j.reshape(1).astype(jnp.int32), shards, *extra)


def _pass_to_sibling(lands, *, name):
    n = len(lands)

    def body(*refs):
        outs = refs[n:2 * n]
        send, recv = refs[2 * n:]
        x, y, c = _coords()
        cps = []
        for a in range(n):
            for k, (px, py) in enumerate(_peer_chips(x, y)):
                slot = 4 * px + 2 * py + c
                cps.append(pltpu.make_async_remote_copy(
                    src_ref=outs[a].at[slot], dst_ref=outs[a].at[slot], send_sem=send.at[a, k], recv_sem=recv.at[a, k],
                    device_id=(x, y, 1 - c), device_id_type=MESH))
        for cp in cps:
            cp.start()
        for cp in cps:
            cp.wait_recv()
        for cp in cps:
            cp.wait_send()

    return pl.pallas_call(
        body, name=name, in_specs=[_ANY] * n, out_specs=[_ANY] * n,
        out_shape=[jax.ShapeDtypeStruct(b.shape, b.dtype) for b in lands],
        input_output_aliases={a: a for a in range(n)},
        scratch_shapes=[pltpu.SemaphoreType.DMA((n, N_PEER_CHIPS)), pltpu.SemaphoreType.DMA((n, N_PEER_CHIPS))],
    )(*lands)


def _sum_parts(parts, got, *, name):
    _, rows, cols = parts.shape
    tc = _col_tile(4, rows, cols, itemsize=parts.dtype.itemsize)
    x, y, c = _coords()
    idx = jnp.stack([2 * x + y, 2 * (1 - x) + y, 2 * x + (1 - y), 2 * (1 - x) + (1 - y), c]).astype(jnp.int32)

    def body(i_ref, p_ref, g0_ref, g1_ref, g2_ref, o_ref):
        o_ref[...] = ((p_ref[...].astype(F32) + g0_ref[...].astype(F32)) + g1_ref[...].astype(F32)) + g2_ref[...].astype(F32)

    slot = lambda s: pl.BlockSpec((None, rows, tc), lambda i, ir, s=s: (ir[s], 0, i))
    return pl.pallas_call(
        body, name=name,
        grid_spec=pltpu.PrefetchScalarGridSpec(
            num_scalar_prefetch=1, grid=(cols // tc,),
            in_specs=[slot(0), slot(1), slot(2), slot(3)],
            out_specs=pl.BlockSpec((None, rows, tc), lambda i, ir: (ir[4], 0, i))),
        out_shape=jax.ShapeDtypeStruct((2, rows, cols), F32),
        compiler_params=_cp(("parallel",)),
    )(idx, parts, got, got, got)


def _sum_slots(t, *, name):
    S, rows, cols = t.shape
    tc = _col_tile(S, rows, cols)

    def body(t_ref, o_ref):
        acc = t_ref[0].astype(F32)
        for s in range(1, S):
            acc = acc + t_ref[s].astype(F32)
        o_ref[...] = acc

    return pl.pallas_call(
        body, name=name, grid=(cols // tc,),
        in_specs=[pl.BlockSpec((S, rows, tc), lambda i: (0, 0, i))],
        out_specs=pl.BlockSpec((rows, tc), lambda i: (0, i)),
        out_shape=jax.ShapeDtypeStruct((rows, cols), F32),
        compiler_params=_cp(("parallel",)),
    )(t)


def _halves_join(bufs, *, name):
    n = len(bufs)

    def body(*refs):
        outs = refs[n:2 * n]
        send, recv = refs[2 * n:]
        x, y, c = _coords()
        cps = [pltpu.make_async_remote_copy(src_ref=outs[a].at[c], dst_ref=outs[a].at[c], send_sem=send.at[a],
                                            recv_sem=recv.at[a], device_id=(x, y, 1 - c), device_id_type=MESH)
               for a in range(n)]
        for cp in cps:
            cp.start()
        for cp in cps:
            cp.wait_recv()
        for cp in cps:
            cp.wait_send()

    return pl.pallas_call(
        body, name=name, in_specs=[_ANY] * n, out_specs=[_ANY] * n,
        out_shape=[jax.ShapeDtypeStruct(b.shape, b.dtype) for b in bufs],
        input_output_aliases={a: a for a in range(n)},
        scratch_shapes=[pltpu.SemaphoreType.DMA((n,)), pltpu.SemaphoreType.DMA((n,))],
    )(*bufs)


def _swap_start(srcs, *, name):
    n = len(srcs)
    lands = [lax.empty((s.shape[0],) + s.shape[2:], s.dtype) for s in srcs]

    def body(*refs):
        src_refs, land_refs = refs[:n], refs[n:2 * n]
        send, recv, token = refs[2 * n:3 * n], refs[3 * n:4 * n], refs[-1]
        x, y, c = _coords()
        for a in range(n):
            pltpu.make_async_remote_copy(src_ref=src_refs[a].at[:, 1 - c], dst_ref=land_refs[a], send_sem=send[a],
                                         recv_sem=recv[a], device_id=(x, y, 1 - c), device_id_type=MESH).start()
        token[...] = jnp.zeros_like(token)

    bufs = list(srcs) + lands
    res = pl.pallas_call(
        body, name=name,
        out_shape=(*[pltpu.SemaphoreType.DMA(())] * (2 * n), *[pltpu.HBM(b.shape, b.dtype) for b in bufs],
                   jax.ShapeDtypeStruct((8, 128), F32)),
        in_specs=[_HBM] * (2 * n),
        out_specs=(*[_SEM] * (2 * n), *[_HBM] * (2 * n), pl.BlockSpec(memory_space=pltpu.VMEM)),
        input_output_aliases={i: 2 * n + i for i in range(2 * n)},
        compiler_params=pltpu.CompilerParams(has_side_effects=_EFFECT),
    )(*[pltpu.with_memory_space_constraint(b, pltpu.HBM) for b in bufs])
    return list(res[:2 * n]), list(res[2 * n:3 * n]), list(res[3 * n:4 * n]), res[-1]


def _swap_wait(sems, srcs, lands, after, *, name):
    n = len(srcs)

    def body(*refs):
        src_refs, land_refs = refs[:n], refs[n:2 * n]
        send, recv = refs[2 * n:3 * n], refs[3 * n:4 * n]
        x, y, c = _coords()
        for a in range(n):
            cp = pltpu.make_async_remote_copy(
                src_ref=src_refs[a].at[:, 0], dst_ref=land_refs[a], send_sem=send[a], recv_sem=recv[a],
                device_id=(x, y, 1 - c), device_id_type=MESH)
            cp.wait_send()
            cp.wait_recv()

    bufs = list(srcs) + list(lands)
    res = pl.pallas_call(
        body, name=name,
        out_shape=tuple(pltpu.HBM(b.shape, b.dtype) for b in bufs),
        in_specs=[_HBM] * (2 * n) + [_SEM] * (2 * n) + [_ANY] * len(after),
        out_specs=tuple([_HBM] * (2 * n)),
        input_output_aliases={i: i for i in range(2 * n)},
        compiler_params=pltpu.CompilerParams(has_side_effects=_EFFECT),
    )(*bufs, *sems, *after)
    return list(res[:n]), list(res[n:])


def _grad_views(grads):
    return [g.reshape(4, 2, g.shape[0] // 8, g.shape[1]) for g in grads]


def _scatter_begin(views, theirs, *, tag):
    parts = [_pair_sum(v, t, name=f"rs_pair_{tag}_{i}") for i, (v, t) in enumerate(zip(views, theirs))]
    got = [lax.empty(p.shape, p.dtype) for p in parts]
    sems, parts, got, token = _split_start(
        parts, got, lambda x, y, c, px, py: 2 * px + py, lambda x, y, c: 2 * x + y, name=f"rs_start_{tag}")
    return (sems, parts, got), token


def _reduce_scatter_begin(grads, *, tag):
    views = _grad_views(grads)
    theirs = _sibling_swap(views, name=f"rs_swap_{tag}")
    return _scatter_begin(views, theirs, tag=tag)


def _reduce_scatter_sums(state, after, *, tag):
    sems, parts, got = state
    parts, got = _split_wait(sems, parts, got, after, name=f"rs_wait_{tag}")
    return [_sum_parts(p, t, name=f"rs_sum_{tag}_{i}") for i, (p, t) in enumerate(zip(parts, got))]


def _reduce_scatter_join(halves, *, tag):
    joined = _halves_join(halves, name=f"rs_join_{tag}")
    return [j.reshape(2 * j.shape[1], j.shape[2]) for j in joined]


_SMALL = ("norm_w", "ssm_a_re", "ssm_a_im", "ssm_log_dt", "ssm_b_re", "ssm_b_im", "ssm_c_re", "ssm_c_im", "ssm_d",
          "ssm_glu_b", "sg_ln_w", "sg_ln_b", "sg_w", "sg_b", "attn_sinks", "final_norm_w")
_BIG = ("w_in", "ssm_glu_w", "w_branch_a", "w_branch_b", "w_branch_c", "w_out")
_WEIGHTS = ("norm_w", "w_in", "ssm_a_re", "ssm_a_im", "ssm_log_dt", "ssm_b_re", "ssm_b_im", "ssm_c_re", "ssm_c_im",
            "ssm_d", "ssm_glu_w", "ssm_glu_b", "sg_ln_w", "sg_ln_b", "sg_w", "sg_b", "attn_sinks", "w_branch_a",
            "w_branch_b", "w_branch_c", "w_out", "final_norm_w")
_PACK_COLS = 1024
_PACK_ALIGN = 8 * ROW_ALIGN * _PACK_COLS


N_PEERS = 7


def _peer(x, y, c, k):
    return x ^ (k >> 2), y ^ ((k >> 1) & 1), c ^ (k & 1)


def _own_slice(buf, *, name):
    _, rows, cols = buf.shape
    x, y, c = _coords()

    def body(me_ref, s_ref, o_ref):
        o_ref[...] = s_ref[...]

    spec = pl.BlockSpec((None, rows, cols), lambda i, mr: (mr[0], 0, 0))
    return pl.pallas_call(
        body, name=name,
        grid_spec=pltpu.PrefetchScalarGridSpec(num_scalar_prefetch=1, grid=(1,), in_specs=[spec], out_specs=spec),
        out_shape=jax.ShapeDtypeStruct(buf.shape, buf.dtype),
        compiler_params=_cp(("arbitrary",)),
    )((4 * x + 2 * y + c).reshape(1).astype(jnp.int32), buf)


def _slice_exchange_start(buf, land, *, name):
    def body(in_ref, land_ref, *rest):
        send, recv, token = rest[:N_PEERS], rest[N_PEERS:2 * N_PEERS], rest[-1]
        x, y, c = _coords()
        me = 4 * x + 2 * y + c
        for k in range(1, N_PEERS + 1):
            px, py, pc = _peer(x, y, c, k)
            pltpu.make_async_remote_copy(
                src_ref=in_ref.at[4 * px + 2 * py + pc], dst_ref=land_ref.at[me], send_sem=send[k - 1],
                recv_sem=recv[k - 1], device_id=(px, py, pc), device_id_type=MESH).start()
        token[...] = jnp.zeros_like(token)

    res = pl.pallas_call(
        body, name=name,
        out_shape=(*[pltpu.SemaphoreType.DMA(())] * (2 * N_PEERS), pltpu.HBM(buf.shape, buf.dtype),
                   pltpu.HBM(land.shape, land.dtype), jax.ShapeDtypeStruct((8, 128), F32)),
        in_specs=[_HBM, _HBM],
        out_specs=(*[_SEM] * (2 * N_PEERS), _HBM, _HBM, pl.BlockSpec(memory_space=pltpu.VMEM)),
        input_output_aliases={0: 2 * N_PEERS, 1: 2 * N_PEERS + 1},
        compiler_params=pltpu.CompilerParams(has_side_effects=_EFFECT),
    )(pltpu.with_memory_space_constraint(buf, pltpu.HBM), pltpu.with_memory_space_constraint(land, pltpu.HBM))
    return list(res[:2 * N_PEERS]), res[2 * N_PEERS], res[2 * N_PEERS + 1], res[-1]


def _slice_exchange_wait(sems, buf, land, after, *, name):
    def body(in_ref, land_ref, *rest):
        send, recv = rest[:N_PEERS], rest[N_PEERS:2 * N_PEERS]
        x, y, c = _coords()
        for k in range(N_PEERS):
            cp = pltpu.make_async_remote_copy(
                src_ref=in_ref.at[0], dst_ref=land_ref.at[0], send_sem=send[k], recv_sem=recv[k],
                device_id=(x, y, 1 - c), device_id_type=MESH)
            cp.wait_send()
            cp.wait_recv()

    res = pl.pallas_call(
        body, name=name,
        out_shape=(pltpu.HBM(buf.shape, buf.dtype), pltpu.HBM(land.shape, land.dtype)),
        in_specs=[_HBM, _HBM] + [_SEM] * (2 * N_PEERS) + [_ANY] * len(after),
        out_specs=(_HBM, _HBM),
        input_output_aliases={0: 0, 1: 1},
        compiler_params=pltpu.CompilerParams(has_side_effects=_EFFECT),
    )(buf, land, *sems, *after)
    return res[1]


def _allreduce_small_begin(packed):
    rows, cols = packed.shape
    buf = packed.reshape(8, rows // 8, cols)
    sems, buf, land, token = _slice_exchange_start(buf, _own_slice(buf, name="small_grads_own"),
                                                   name="small_grads_start")
    return (sems, buf, land), token


def _allreduce_small_end(state, after):
    got = _slice_exchange_wait(*state, list(after), name="small_grads_wait")
    mine = _sum_slots(got, name="small_grads_sum")
    return _gather8([mine], name="small_grads_gather")[0]


def _pack(ts):
    flat = jnp.concatenate([t.reshape(-1) for t in ts])
    pad = (-flat.shape[0]) % _PACK_ALIGN
    return jnp.pad(flat, (0, pad)).reshape(-1, _PACK_COLS)


def _unpack(buf, like):
    flat = buf.reshape(-1)
    out, pos = [], 0
    for t in like:
        out.append(flat[pos:pos + t.size].reshape(t.shape))
        pos += t.size
    return out


def kernel(x, norm_w, w_in, ssm_a_re, ssm_a_im, ssm_log_dt, ssm_b_re, ssm_b_im, ssm_c_re, ssm_c_im, ssm_d, ssm_glu_w, ssm_glu_b, sg_ln_w, sg_ln_b, sg_w, sg_b, attn_sinks, w_branch_a, w_branch_b, w_branch_c, w_out, final_norm_w, loss_target, m_norm_w, m_w_in, m_ssm_a_re, m_ssm_a_im, m_ssm_log_dt, m_ssm_b_re, m_ssm_b_im, m_ssm_c_re, m_ssm_c_im, m_ssm_d, m_ssm_glu_w, m_ssm_glu_b, m_sg_ln_w, m_sg_ln_b, m_sg_w, m_sg_b, m_attn_sinks, m_w_branch_a, m_w_branch_b, m_w_branch_c, m_w_out, m_final_norm_w, v_norm_w, v_w_in, v_ssm_a_re, v_ssm_a_im, v_ssm_log_dt, v_ssm_b_re, v_ssm_b_im, v_ssm_c_re, v_ssm_c_im, v_ssm_d, v_ssm_glu_w, v_ssm_glu_b, v_sg_ln_w, v_sg_ln_b, v_sg_w, v_sg_b, v_attn_sinks, v_w_branch_a, v_w_branch_b, v_w_branch_c, v_w_out, v_final_norm_w):
    w = dict(norm_w=norm_w, w_in=w_in, ssm_a_re=ssm_a_re, ssm_a_im=ssm_a_im, ssm_log_dt=ssm_log_dt, ssm_b_re=ssm_b_re,
             ssm_b_im=ssm_b_im, ssm_c_re=ssm_c_re, ssm_c_im=ssm_c_im, ssm_d=ssm_d, ssm_glu_w=ssm_glu_w,
             ssm_glu_b=ssm_glu_b, sg_ln_w=sg_ln_w, sg_ln_b=sg_ln_b, sg_w=sg_w, sg_b=sg_b, attn_sinks=attn_sinks,
             w_branch_a=w_branch_a, w_branch_b=w_branch_b, w_branch_c=w_branch_c, w_out=w_out,
             final_norm_w=final_norm_w)
    m = dict(norm_w=m_norm_w, w_in=m_w_in, ssm_a_re=m_ssm_a_re, ssm_a_im=m_ssm_a_im, ssm_log_dt=m_ssm_log_dt,
             ssm_b_re=m_ssm_b_re, ssm_b_im=m_ssm_b_im, ssm_c_re=m_ssm_c_re, ssm_c_im=m_ssm_c_im, ssm_d=m_ssm_d,
             ssm_glu_w=m_ssm_glu_w, ssm_glu_b=m_ssm_glu_b, sg_ln_w=m_sg_ln_w, sg_ln_b=m_sg_ln_b, sg_w=m_sg_w,
             sg_b=m_sg_b, attn_sinks=m_attn_sinks, w_branch_a=m_w_branch_a, w_branch_b=m_w_branch_b,
             w_branch_c=m_w_branch_c, w_out=m_w_out, final_norm_w=m_final_norm_w)
    v = dict(norm_w=v_norm_w, w_in=v_w_in, ssm_a_re=v_ssm_a_re, ssm_a_im=v_ssm_a_im, ssm_log_dt=v_ssm_log_dt,
             ssm_b_re=v_ssm_b_re, ssm_b_im=v_ssm_b_im, ssm_c_re=v_ssm_c_re, ssm_c_im=v_ssm_c_im, ssm_d=v_ssm_d,
             ssm_glu_w=v_ssm_glu_w, ssm_glu_b=v_ssm_glu_b, sg_ln_w=v_sg_ln_w, sg_ln_b=v_sg_ln_b, sg_w=v_sg_w,
             sg_b=v_sg_b, attn_sinks=v_attn_sinks, w_branch_a=v_w_branch_a, w_branch_b=v_w_branch_b,
             w_branch_c=v_w_branch_c, w_out=v_w_out, final_norm_w=v_final_norm_w)

    big_names = ("winT", "glu_w", "wbaT", "wbbT", "wbcT", "w_out")
    L = x.shape[1]
    tabs = _rope_tables(L)
    p = {k: w[k] for k in _SMALL}

    column_sharded = ("w_in", "w_branch_a", "w_branch_b", "w_branch_c")

    row_shards = {k: w[k].transpose(0, 2, 1) if k in column_sharded else w[k] for k in _BIG}
    rows_of = lambda lands: [t.reshape(8 * t.shape[1], t.shape[2]) for t in lands]
    saved = [None] * DEPTH

    land_a = [_fill_own(row_shards["w_in"], 0, name="gather_fill_0_0")]
    sems_a, land_a, token_a = _gather_start(land_a, name="gather_start_0a")
    fill_after = lambda l, names, i0: [_fill_own(row_shards[k], l, name=f"gather_fill_{l}_{i0 + i}", after=token_a)
                                       for i, k in enumerate(names)]
    lands = [[None] + fill_after(0, _BIG[1:], 1), fill_after(1, _BIG, 0)]
    sp = [_prep_layer(p, l, after=token_a) for l in range(DEPTH)]
    h0 = _rms_fwd(x[0], p["norm_w"][0], name="rms_fwd_0", after=token_a)
    land_a = _gather_wait(sems_a, land_a, [h0, sp[0]["btr"], sp[1]["btr"]] + lands[0][1:] + lands[1],
                          name="gather_wait_0a")
    land_a = _pass_to_sibling(land_a, name="gather_pass_0a")
    sems_b, land_b, token_b = _gather_start(lands[0][1:], name="gather_start_0b", after=land_a)
    split1 = {}

    def start1(main):
        split1["sems"], split1["land"], token1 = _gather_start(lands[1][:1], name="gather_start_1a", after=[main])
        return token1

    def rest(l, sems, land, first):
        def arrived(t):
            got = _pass_to_sibling(_gather_wait(sems, land, t, name=f"gather_wait_{l}b"), name=f"gather_pass_{l}b")
            return dict(zip(big_names, rows_of(first + got))), None
        return arrived

    x1, saved[0], big0 = _layer_fwd(x[0], h0, p, sp[0], rows_of(land_a)[0], rest(0, sems_b, land_b, land_a), 0,
                                    tabs, proj_after=token_b, after_main=start1)
    land_1a = _gather_wait(split1["sems"], split1["land"], [x1], name="gather_wait_1a")
    land_1a = _pass_to_sibling(land_1a, name="gather_pass_1a")
    sems_1b, land_1b, token_1b = _gather_start(lands[1][1:], name="gather_start_1b", after=land_1a)
    h1 = _rms_fwd(x1, p["norm_w"][1], name="rms_fwd_1")
    x2, saved[1], big1 = _layer_fwd(x1, h1, p, sp[1], rows_of(land_1a)[0], rest(1, sems_1b, land_1b, land_1a),
                                    1, tabs, proj_after=token_1b)
    bigs = [big0, big1]
    loss, dx, dfw = _final_loss(x2, p["final_norm_w"], loss_target[0], name="final_loss")

    grads = [None] * DEPTH
    rs = {}

    def early(l):
        def begin(g):
            *rs[f"{l}a"], token_a = _swap_start(_grad_views([g[k] for k in big_names[1:]]), name=f"rs_swap_start_{l}a")
            return token_a
        return begin

    def mid(l):
        def go_on(t):
            sems, views, lands = rs[f"{l}a"]
            views, theirs = _swap_wait(sems, views, lands, [t], name=f"rs_swap_wait_{l}a")
            rs[f"{l}a"], token_a = _scatter_begin(views, theirs, tag=f"{l}a")
            return token_a
        return go_on

    def late(l, dproj, dx):
        if l == 0:
            rs["0b"], token_s = _reduce_scatter_begin([grads[0]["winT"]], tag="0b")
            dh = _proj_bwd_dh(dproj, bigs[0]["winT"], 0, token_s)
            return _rms_bwd(saved[0]["x"], p["norm_w"][0], dh, dx, name="rms_bwd_0")
        sems, views, lands, token_b = _swap_start(_grad_views([grads[l]["winT"]]), name=f"rs_swap_start_{l}b")
        dh = _proj_bwd_dh(dproj, bigs[l]["winT"], l, token_b)
        views, theirs = _swap_wait(sems, views, lands, [dh], name=f"rs_swap_wait_{l}b")
        rs[f"{l}b"], token_s = _scatter_begin(views, theirs, tag=f"{l}b")
        return _rms_bwd(saved[l]["x"], p["norm_w"][l], dh, dx, name=f"rms_bwd_{l}", after=token_s)

    def reduced(l, after):
        halves = (_reduce_scatter_sums(rs[f"{l}b"], after, tag=f"{l}b")
                  + _reduce_scatter_sums(rs[f"{l}a"], after, tag=f"{l}a"))
        return _reduce_scatter_join(halves, tag=str(l))

    dproj, grads[1] = _layer_bwd(dx, saved[1], p, bigs[1], 1, tabs, early(1), mid(1))
    dx, grads[1]["norm_w"] = late(1, dproj, dx)
    dproj, grads[0] = _layer_bwd(dx, saved[0], p, bigs[0], 0, tabs, early(0), mid(0))
    dx, grads[0]["norm_w"] = late(0, dproj, dx)
    small_like = [w[k] for k in _SMALL]
    gs = [jnp.stack([grads[l][k] for l in range(DEPTH)]) if k != "final_norm_w" else dfw for k in _SMALL]
    small, token_small = _allreduce_small_begin(_pack(gs + [loss.reshape(1)]))
    red1 = reduced(1, [dx, token_small])

    tr = lambda t: t.transpose(0, 2, 1)
    view = {k: (tr if k == "w_in" else (lambda t: t)) for k in _BIG}
    shard_grads = lambda red: dict(zip(_BIG, (red[0], red[1], red[2].T, red[3].T, red[4].T, red[5])))
    outs = {k: None for k in _BIG}

    def adamw_big(l, red):
        for k, g in shard_grads(red).items():
            outs[k] = _adamw_layer(view[k](w[k]), g, view[k](m[k]), view[k](v[k]), l, outs[k], name=f"adamw_{k}_{l}")

    adamw_big(1, red1)

    gsum = _allreduce_small_end(small, [outs[k][0] for k in _BIG])
    adamw_big(0, reduced(0, [gsum]))

    gfull, delta, new_m, new_v = {}, {}, {}, {}
    for k in _BIG:
        gfull[k], delta[k], new_m[k], new_v[k] = (view[k](t) for t in outs[k])
    *small_sums, loss = _unpack(gsum, small_like + [loss])
    for k, t in zip(_SMALL, small_sums):
        gfull[k] = t
        delta[k], new_m[k], new_v[k] = _adamw(w[k], t, m[k], v[k], name=f"adamw_{k}")

    return (loss, dx[None], *[gfull[k] for k in _WEIGHTS], *[delta[k] for k in _WEIGHTS],
            *[new_m[k] for k in _WEIGHTS], *[new_v[k] for k in _WEIGHTS])
```
